```python
import jax, jax.numpy as jnp
from jax import lax
import numpy as np

D_MODEL = 1024
BATCH = 8
SEQ = 2048
DEPTH = 2

N_A = DEPTH // 2
N_B = DEPTH - N_A
RET_HEADS = 4
RET_QK_DIM = D_MODEL // RET_HEADS
RET_V_DIM = 2 * D_MODEL // RET_HEADS
RET_CHUNK = 128
ROPE_BASE = 10000.0
RET_IN_WIDTH = 2 * RET_HEADS * RET_QK_DIM + 2 * RET_HEADS * RET_V_DIM
FOX_HEADS = 16
FOX_HEAD_DIM = D_MODEL // FOX_HEADS
Q_BLOCK = 128
FORGET_BIAS_CENTER = 2.0
D_FF = 4 * D_MODEL
EPS = 1e-6
MAX_POS_OFFSET = 4096

kernel_name = "yoco_retention_forgetting_attention_adaln"


def rms_norm(x, gain):
    xf = x.astype(jnp.float32)
    y = xf * lax.rsqrt(jnp.mean(xf * xf, axis=-1, keepdims=True) + EPS)
    return (y * gain.astype(jnp.float32)).astype(x.dtype)


def ada_modulate(x, gain, shift, scale):
    return rms_norm(x, gain) * (1 + scale[:, None, :]) + shift[:, None, :]


def rotary(x, positions):
    half = x.shape[-1] // 2
    inv_freq = ROPE_BASE ** (-jnp.arange(half, dtype=jnp.float32) / half)
    ang = positions.astype(jnp.float32)[..., None] * inv_freq
    cos = jnp.cos(ang)[:, :, None, :]
    sin = jnp.sin(ang)[:, :, None, :]
    xf = x.astype(jnp.float32)
    x1, x2 = xf[..., :half], xf[..., half:]
    out = jnp.concatenate([x1 * cos - x2 * sin, x1 * sin + x2 * cos], axis=-1)
    return out.astype(x.dtype)


def retention(q, k, v, gammas):
    B, S, H, Dk = q.shape
    Dv = v.shape[-1]
    C = RET_CHUNK
    NC = S // C
    dt = q.dtype
    k = k * (Dk ** -0.5)
    qc = q.reshape(B, NC, C, H, Dk).transpose(1, 0, 3, 2, 4)
    kc = k.reshape(B, NC, C, H, Dk).transpose(1, 0, 3, 2, 4)
    vc = v.reshape(B, NC, C, H, Dv).transpose(1, 0, 3, 2, 4)
    log_g = jnp.log(gammas.astype(jnp.float32))
    idx = jnp.arange(C, dtype=jnp.float32)
    rel = idx[:, None] - idx[None, :]
    intra_decay = jnp.where(rel >= 0, jnp.exp(log_g[:, None, None] * jnp.maximum(rel, 0.0)), 0.0)
    q_decay = jnp.exp(log_g[:, None] * (idx + 1.0))
    k_decay = jnp.exp(log_g[:, None] * (C - 1.0 - idx))
    chunk_decay = jnp.exp(log_g * C)
    scores = jnp.einsum('nbhqd,nbhkd->nbhqk', qc, kc) * intra_decay
    intra = jnp.einsum('nbhqk,nbhkv->nbhqv', scores.astype(dt), vc).astype(jnp.float32)

    def step(state, inp):
        q_i, k_i, v_i = inp
        cross = jnp.einsum('bhqd,bhdv->bhqv', q_i.astype(jnp.float32), state) * q_decay[None, :, :, None]
        upd = jnp.einsum('bhkd,bhkv->bhdv', (k_i.astype(jnp.float32) * k_decay[None, :, :, None]),
                         v_i.astype(jnp.float32))
        return state * chunk_decay[None, :, None, None] + upd, cross

    state0 = jnp.zeros((B, H, Dk, Dv), jnp.float32)
    _, cross = lax.scan(step, state0, (qc, kc, vc))
    out = intra + cross
    return out.transpose(1, 0, 3, 2, 4).reshape(B, S, H, Dv).astype(dt)


def retention_mixer(h, positions, w_in, norm_gain, w_out, gammas):
    B, S, _ = h.shape
    proj = h @ w_in
    qk_w = RET_HEADS * RET_QK_DIM
    v_w = RET_HEADS * RET_V_DIM
    q, k, v, g = jnp.split(proj, [qk_w, 2 * qk_w, 2 * qk_w + v_w], axis=-1)
    q = rotary(q.reshape(B, S, RET_HEADS, RET_QK_DIM), positions)
    k = rotary(k.reshape(B, S, RET_HEADS, RET_QK_DIM), positions)
    v = v.reshape(B, S, RET_HEADS, RET_V_DIM)
    y = retention(q, k, v, gammas)
    y = rms_norm(y, norm_gain).reshape(B, S, v_w)
    return (jax.nn.silu(g) * y) @ w_out


def shared_kv(x, c_act, kv_norm_gain, kv_w_ada, kv_b_ada, kv_w, forget_bias, k_norm_gain):
    B, S, D = x.shape
    shift, scale = jnp.split(c_act @ kv_w_ada + kv_b_ada, 2, axis=-1)
    h = ada_modulate(x, kv_norm_gain, shift, scale)
    kvf = h @ kv_w
    k, v, f = jnp.split(kvf, [D, 2 * D], axis=-1)
    k = rms_norm(k.reshape(B, S, FOX_HEADS, FOX_HEAD_DIM), k_norm_gain)
    v = v.reshape(B, S, FOX_HEADS, FOX_HEAD_DIM)
    log_f = jax.nn.log_sigmoid(f.astype(jnp.float32) + forget_bias.astype(jnp.float32))
    f_cum = jnp.cumsum(log_f, axis=1)
    return k, v, f_cum


def forgetting_attention(q, k, v, f_cum):
    B, S, H, Dh = q.shape
    NB = S // Q_BLOCK
    scale = Dh ** -0.5
    k_pos = jnp.arange(S)
    fk = f_cum.transpose(0, 2, 1)
    qb = q.reshape(B, NB, Q_BLOCK, H, Dh).transpose(1, 0, 3, 2, 4)
    fq = fk.reshape(B, H, NB, Q_BLOCK).transpose(2, 0, 1, 3)

    def block(args):
        q_blk, fq_blk, blk = args
        logits = jnp.einsum('bhqd,bkhd->bhqk', q_blk, k).astype(jnp.float32) * scale
        logits = logits + fq_blk[..., None] - fk[:, :, None, :]
        q_pos = blk * Q_BLOCK + jnp.arange(Q_BLOCK)
        causal = k_pos[None, :] <= q_pos[:, None]
        logits = jnp.where(causal, logits, -jnp.inf)
        p = jax.nn.softmax(logits, axis=-1).astype(v.dtype)
        return jnp.einsum('bhqk,bkhd->bqhd', p, v)

    out = lax.map(block, (qb, fq, jnp.arange(NB)))
    return out.transpose(1, 0, 2, 3, 4).reshape(B, S, H, Dh)


def fox_mixer(h, k, v, f_cum, w_in, q_norm_gain, w_out):
    B, S, D = h.shape
    q, og = jnp.split(h @ w_in, 2, axis=-1)
    q = rms_norm(q.reshape(B, S, FOX_HEADS, FOX_HEAD_DIM), q_norm_gain)
    y = forgetting_attention(q, k, v, f_cum).reshape(B, S, D)
    return (jax.nn.sigmoid(og) * y) @ w_out


def sq_relu_mlp(h, w1, w2):
    return jnp.square(jax.nn.relu(h @ w1)) @ w2


def _fwd_setup_inputs(seed: int = 0) -> dict:
    key = jax.random.key(seed)
    ks = jax.random.split(key, 24)
    f32 = jnp.float32
    D = D_MODEL

    def w(k, shape, fan_in, s=1.0):
        return jax.random.normal(k, shape, f32) * (s * fan_in ** -0.5)

    def gain(k, shape):
        return 1.0 + 0.02 * jax.random.normal(k, shape, f32)

    x = jax.random.normal(ks[0], (BATCH, SEQ, D), f32)
    c = jax.random.normal(ks[1], (BATCH, D), f32)
    offset = jax.random.randint(ks[2], (BATCH, 1), 0, MAX_POS_OFFSET, dtype=jnp.int32)
    positions = offset + jnp.arange(SEQ, dtype=jnp.int32)[None, :]
    return {
        "x": x,
        "c": c,
        "positions": positions,
        "norm_mix_gain": gain(ks[3], (DEPTH, D)),
        "norm_mlp_gain": gain(ks[4], (DEPTH, D)),
        "w_ada": w(ks[5], (DEPTH, D, 6 * D), D, 0.5),
        "b_ada": 0.02 * jax.random.normal(ks[6], (DEPTH, 6 * D), f32),
        "w_mlp_in": w(ks[7], (DEPTH, D, D_FF), D),
        "w_mlp_out": w(ks[8], (DEPTH, D_FF, D), D_FF),
        "ret_w_in": w(ks[9], (N_A, D, RET_IN_WIDTH), D),
        "ret_norm_gain": gain(ks[10], (N_A, RET_HEADS, RET_V_DIM)),
        "ret_w_out": w(ks[11], (N_A, RET_HEADS * RET_V_DIM, D), RET_HEADS * RET_V_DIM),
        "kv_norm_gain": gain(ks[12], (D,)),
        "kv_w_ada": w(ks[13], (D, 2 * D), D, 0.5),
        "kv_b_ada": 0.02 * jax.random.normal(ks[14], (2 * D,), f32),
        "kv_w": w(ks[15], (D, 2 * D + FOX_HEADS), D),
        "forget_bias": FORGET_BIAS_CENTER + 0.1 * jax.random.normal(ks[16], (FOX_HEADS,), f32),
        "k_norm_gain": gain(ks[17], (FOX_HEAD_DIM,)),
        "fox_w_in": w(ks[18], (N_B, D, 2 * D), D),
        "q_norm_gain": gain(ks[19], (N_B, FOX_HEAD_DIM)),
        "fox_w_out": w(ks[20], (N_B, D, D), D),
    }


def _fwd_reference(x, c, positions, norm_mix_gain, norm_mlp_gain, w_ada, b_ada, w_mlp_in, w_mlp_out,
              ret_w_in, ret_norm_gain, ret_w_out, kv_norm_gain, kv_w_ada, kv_b_ada, kv_w,
              forget_bias, k_norm_gain, fox_w_in, q_norm_gain, fox_w_out):
    c_act = jax.nn.silu(c)
    gammas = 1.0 - jnp.power(2.0, -5.0 - jnp.arange(RET_HEADS, dtype=jnp.float32))
    k_sh = v_sh = f_sh = None
    for layer in range(DEPTH):
        if layer == N_A:
            k_sh, v_sh, f_sh = shared_kv(x, c_act, kv_norm_gain, kv_w_ada, kv_b_ada, kv_w,
                                         forget_bias, k_norm_gain)
        ada = c_act @ w_ada[layer] + b_ada[layer]
        sh1, sc1, g1, sh2, sc2, g2 = jnp.split(ada, 6, axis=-1)
        h = ada_modulate(x, norm_mix_gain[layer], sh1, sc1)
        if layer < N_A:
            mix = retention_mixer(h, positions, ret_w_in[layer], ret_norm_gain[layer],
                                  ret_w_out[layer], gammas)
        else:
            j = layer - N_A
            mix = fox_mixer(h, k_sh, v_sh, f_sh, fox_w_in[j], q_norm_gain[j], fox_w_out[j])
        x = x + g1[:, None, :] * mix
        h = ada_modulate(x, norm_mlp_gain[layer], sh2, sc2)
        x = x + g2[:, None, :] * sq_relu_mlp(h, w_mlp_in[layer], w_mlp_out[layer])
    return x


import jax as _jax
import jax.numpy as _jnp

TWIN_FORMAT = 'train_step'
FWD_PARAMS = ['x', 'c', 'positions', 'norm_mix_gain', 'norm_mlp_gain', 'w_ada', 'b_ada', 'w_mlp_in', 'w_mlp_out', 'ret_w_in', 'ret_norm_gain', 'ret_w_out', 'kv_norm_gain', 'kv_w_ada', 'kv_b_ada', 'kv_w', 'forget_bias', 'k_norm_gain', 'fox_w_in', 'q_norm_gain', 'fox_w_out']
TWIN_WEIGHTS = ['norm_mix_gain', 'norm_mlp_gain', 'w_ada', 'b_ada', 'w_mlp_in', 'w_mlp_out', 'ret_w_in', 'ret_norm_gain', 'ret_w_out', 'kv_norm_gain', 'kv_w_ada', 'kv_b_ada', 'kv_w', 'forget_bias', 'k_norm_gain', 'fox_w_in', 'q_norm_gain', 'fox_w_out']
TWIN_DIFF_INPUT = 'x'
TWIN_INPUTS = ['x', 'c', 'positions', 'norm_mix_gain', 'norm_mlp_gain', 'w_ada', 'b_ada', 'w_mlp_in', 'w_mlp_out', 'ret_w_in', 'ret_norm_gain', 'ret_w_out', 'kv_norm_gain', 'kv_w_ada', 'kv_b_ada', 'kv_w', 'forget_bias', 'k_norm_gain', 'fox_w_in', 'q_norm_gain', 'fox_w_out', 'loss_target', 'm_norm_mix_gain', 'm_norm_mlp_gain', 'm_w_ada', 'm_b_ada', 'm_w_mlp_in', 'm_w_mlp_out', 'm_ret_w_in', 'm_ret_norm_gain', 'm_ret_w_out', 'm_kv_norm_gain', 'm_kv_w_ada', 'm_kv_b_ada', 'm_kv_w', 'm_forget_bias', 'm_k_norm_gain', 'm_fox_w_in', 'm_q_norm_gain', 'm_fox_w_out', 'v_norm_mix_gain', 'v_norm_mlp_gain', 'v_w_ada', 'v_b_ada', 'v_w_mlp_in', 'v_w_mlp_out', 'v_ret_w_in', 'v_ret_norm_gain', 'v_ret_w_out', 'v_kv_norm_gain', 'v_kv_w_ada', 'v_kv_b_ada', 'v_kv_w', 'v_forget_bias', 'v_k_norm_gain', 'v_fox_w_in', 'v_q_norm_gain', 'v_fox_w_out']
TWIN_OUTPUTS = ['loss', 'grad_x', 'grad_norm_mix_gain', 'grad_norm_mlp_gain', 'grad_w_ada', 'grad_b_ada', 'grad_w_mlp_in', 'grad_w_mlp_out', 'grad_ret_w_in', 'grad_ret_norm_gain', 'grad_ret_w_out', 'grad_kv_norm_gain', 'grad_kv_w_ada', 'grad_kv_b_ada', 'grad_kv_w', 'grad_forget_bias', 'grad_k_norm_gain', 'grad_fox_w_in', 'grad_q_norm_gain', 'grad_fox_w_out', 'delta_norm_mix_gain', 'delta_norm_mlp_gain', 'delta_w_ada', 'delta_b_ada', 'delta_w_mlp_in', 'delta_w_mlp_out', 'delta_ret_w_in', 'delta_ret_norm_gain', 'delta_ret_w_out', 'delta_kv_norm_gain', 'delta_kv_w_ada', 'delta_kv_b_ada', 'delta_kv_w', 'delta_forget_bias', 'delta_k_norm_gain', 'delta_fox_w_in', 'delta_q_norm_gain', 'delta_fox_w_out', 'new_m_norm_mix_gain', 'new_m_norm_mlp_gain', 'new_m_w_ada', 'new_m_b_ada', 'new_m_w_mlp_in', 'new_m_w_mlp_out', 'new_m_ret_w_in', 'new_m_ret_norm_gain', 'new_m_ret_w_out', 'new_m_kv_norm_gain', 'new_m_kv_w_ada', 'new_m_kv_b_ada', 'new_m_kv_w', 'new_m_forget_bias', 'new_m_k_norm_gain', 'new_m_fox_w_in', 'new_m_q_norm_gain', 'new_m_fox_w_out', 'new_v_norm_mix_gain', 'new_v_norm_mlp_gain', 'new_v_w_ada', 'new_v_b_ada', 'new_v_w_mlp_in', 'new_v_w_mlp_out', 'new_v_ret_w_in', 'new_v_ret_norm_gain', 'new_v_ret_w_out', 'new_v_kv_norm_gain', 'new_v_kv_w_ada', 'new_v_kv_b_ada', 'new_v_kv_w', 'new_v_forget_bias', 'new_v_k_norm_gain', 'new_v_fox_w_in', 'new_v_q_norm_gain', 'new_v_fox_w_out']
TWIN_LEAF_KINDS = {'loss': 'loss', 'grad_x': 'grad_x', 'grad_norm_mix_gain': 'grad_w', 'grad_norm_mlp_gain': 'grad_w', 'grad_w_ada': 'grad_w', 'grad_b_ada': 'grad_w', 'grad_w_mlp_in': 'grad_w', 'grad_w_mlp_out': 'grad_w', 'grad_ret_w_in': 'grad_w', 'grad_ret_norm_gain': 'grad_w', 'grad_ret_w_out': 'grad_w', 'grad_kv_norm_gain': 'grad_w', 'grad_kv_w_ada': 'grad_w', 'grad_kv_b_ada': 'grad_w', 'grad_kv_w': 'grad_w', 'grad_forget_bias': 'grad_w', 'grad_k_norm_gain': 'grad_w', 'grad_fox_w_in': 'grad_w', 'grad_q_norm_gain': 'grad_w', 'grad_fox_w_out': 'grad_w', 'delta_norm_mix_gain': 'delta_w', 'delta_norm_mlp_gain': 'delta_w', 'delta_w_ada': 'delta_w', 'delta_b_ada': 'delta_w', 'delta_w_mlp_in': 'delta_w', 'delta_w_mlp_out': 'delta_w', 'delta_ret_w_in': 'delta_w', 'delta_ret_norm_gain': 'delta_w', 'delta_ret_w_out': 'delta_w', 'delta_kv_norm_gain': 'delta_w', 'delta_kv_w_ada': 'delta_w', 'delta_kv_b_ada': 'delta_w', 'delta_kv_w': 'delta_w', 'delta_forget_bias': 'delta_w', 'delta_k_norm_gain': 'delta_w', 'delta_fox_w_in': 'delta_w', 'delta_q_norm_gain': 'delta_w', 'delta_fox_w_out': 'delta_w', 'new_m_norm_mix_gain': 'new_m', 'new_m_norm_mlp_gain': 'new_m', 'new_m_w_ada': 'new_m', 'new_m_b_ada': 'new_m', 'new_m_w_mlp_in': 'new_m', 'new_m_w_mlp_out': 'new_m', 'new_m_ret_w_in': 'new_m', 'new_m_ret_norm_gain': 'new_m', 'new_m_ret_w_out': 'new_m', 'new_m_kv_norm_gain': 'new_m', 'new_m_kv_w_ada': 'new_m', 'new_m_kv_b_ada': 'new_m', 'new_m_kv_w': 'new_m', 'new_m_forget_bias': 'new_m', 'new_m_k_norm_gain': 'new_m', 'new_m_fox_w_in': 'new_m', 'new_m_q_norm_gain': 'new_m', 'new_m_fox_w_out': 'new_m', 'new_v_norm_mix_gain': 'new_v', 'new_v_norm_mlp_gain': 'new_v', 'new_v_w_ada': 'new_v', 'new_v_b_ada': 'new_v', 'new_v_w_mlp_in': 'new_v', 'new_v_w_mlp_out': 'new_v', 'new_v_ret_w_in': 'new_v', 'new_v_ret_norm_gain': 'new_v', 'new_v_ret_w_out': 'new_v', 'new_v_kv_norm_gain': 'new_v', 'new_v_kv_w_ada': 'new_v', 'new_v_kv_b_ada': 'new_v', 'new_v_kv_w': 'new_v', 'new_v_forget_bias': 'new_v', 'new_v_k_norm_gain': 'new_v', 'new_v_fox_w_in': 'new_v', 'new_v_q_norm_gain': 'new_v', 'new_v_fox_w_out': 'new_v'}


def _forward(args):
    return _fwd_reference(*[args[k] for k in FWD_PARAMS])


def _output_shape():
    out = _jax.eval_shape(lambda: _forward(_fwd_setup_inputs(0)))
    return out.shape, out.dtype

N_MICROBATCH = 1
ADAM_LR = 0.001
ADAM_B1 = 0.9
ADAM_B2 = 0.999
ADAM_EPS = 1e-08
ADAM_WD = 0.01
ADAM_STEP = 10
PER_EXAMPLE_BATCH_AXIS = {'x': 0, 'c': 0, 'positions': 0, 'loss_target': 0}
SHARED_INPUTS = []
_WEIGHT_DTYPES = {'norm_mix_gain': _jnp.float32, 'norm_mlp_gain': _jnp.float32, 'w_ada': _jnp.float32, 'b_ada': _jnp.float32, 'w_mlp_in': _jnp.float32, 'w_mlp_out': _jnp.float32, 'ret_w_in': _jnp.float32, 'ret_norm_gain': _jnp.float32, 'ret_w_out': _jnp.float32, 'kv_norm_gain': _jnp.float32, 'kv_w_ada': _jnp.float32, 'kv_b_ada': _jnp.float32, 'kv_w': _jnp.float32, 'forget_bias': _jnp.float32, 'k_norm_gain': _jnp.float32, 'fox_w_in': _jnp.float32, 'q_norm_gain': _jnp.float32, 'fox_w_out': _jnp.float32}
MOMENT_SCALE = {'norm_mix_gain': 5.130180e-01, 'norm_mlp_gain': 6.151440e+00, 'w_ada': 1.703572e+00, 'b_ada': 3.536544e+00, 'w_mlp_in': 2.308316e-01, 'w_mlp_out': 8.340330e-01, 'ret_w_in': 5.866783e-02, 'ret_norm_gain': 3.439789e-01, 'ret_w_out': 6.645933e-02, 'kv_norm_gain': 1.553659e-01, 'kv_w_ada': 2.153423e-01, 'kv_b_ada': 4.108883e-01, 'kv_w': 1.196177e-01, 'forget_bias': 1.700361e+00, 'k_norm_gain': 4.454678e-01, 'fox_w_in': 1.836460e-02, 'q_norm_gain': 4.433128e-01, 'fox_w_out': 1.608219e-01}


def _to_microbatches(a, axis):
    t = _jnp.moveaxis(a, axis, 0)
    t = t.reshape((N_MICROBATCH, t.shape[0] // N_MICROBATCH) + t.shape[1:])
    return _jnp.moveaxis(t, 1, axis + 1)


def setup_inputs(seed: int = 0) -> dict:
    inp = _fwd_setup_inputs(seed)
    key = _jax.random.fold_in(_jax.random.key(seed), 7919)
    shape, _ = _output_shape()
    out = dict(inp)
    out["loss_target"] = _jax.random.normal(_jax.random.fold_in(key, 0), shape, _jnp.float32)
    for i, name in enumerate(TWIN_WEIGHTS):
        w = inp[name].astype(_jnp.float32)
        if MOMENT_SCALE is None:
            s = _jnp.sqrt(_jnp.mean(_jnp.square(w)) + 1e-30)
        else:
            s = MOMENT_SCALE[name]
        km, kv = _jax.random.split(_jax.random.fold_in(key, i + 1))
        out[name] = w
        out["m_" + name] = s * _jax.random.normal(km, w.shape, _jnp.float32)
        out["v_" + name] = (s * s) * _jax.random.uniform(kv, w.shape, _jnp.float32, 0.5, 1.5)
    if N_MICROBATCH > 1:
        for name, axis in PER_EXAMPLE_BATCH_AXIS.items():
            out[name] = _to_microbatches(out[name], axis)
    return {'x': out['x'], 'c': out['c'], 'positions': out['positions'], 'norm_mix_gain': out['norm_mix_gain'], 'norm_mlp_gain': out['norm_mlp_gain'], 'w_ada': out['w_ada'], 'b_ada': out['b_ada'], 'w_mlp_in': out['w_mlp_in'], 'w_mlp_out': out['w_mlp_out'], 'ret_w_in': out['ret_w_in'], 'ret_norm_gain': out['ret_norm_gain'], 'ret_w_out': out['ret_w_out'], 'kv_norm_gain': out['kv_norm_gain'], 'kv_w_ada': out['kv_w_ada'], 'kv_b_ada': out['kv_b_ada'], 'kv_w': out['kv_w'], 'forget_bias': out['forget_bias'], 'k_norm_gain': out['k_norm_gain'], 'fox_w_in': out['fox_w_in'], 'q_norm_gain': out['q_norm_gain'], 'fox_w_out': out['fox_w_out'], 'loss_target': out['loss_target'], 'm_norm_mix_gain': out['m_norm_mix_gain'], 'm_norm_mlp_gain': out['m_norm_mlp_gain'], 'm_w_ada': out['m_w_ada'], 'm_b_ada': out['m_b_ada'], 'm_w_mlp_in': out['m_w_mlp_in'], 'm_w_mlp_out': out['m_w_mlp_out'], 'm_ret_w_in': out['m_ret_w_in'], 'm_ret_norm_gain': out['m_ret_norm_gain'], 'm_ret_w_out': out['m_ret_w_out'], 'm_kv_norm_gain': out['m_kv_norm_gain'], 'm_kv_w_ada': out['m_kv_w_ada'], 'm_kv_b_ada': out['m_kv_b_ada'], 'm_kv_w': out['m_kv_w'], 'm_forget_bias': out['m_forget_bias'], 'm_k_norm_gain': out['m_k_norm_gain'], 'm_fox_w_in': out['m_fox_w_in'], 'm_q_norm_gain': out['m_q_norm_gain'], 'm_fox_w_out': out['m_fox_w_out'], 'v_norm_mix_gain': out['v_norm_mix_gain'], 'v_norm_mlp_gain': out['v_norm_mlp_gain'], 'v_w_ada': out['v_w_ada'], 'v_b_ada': out['v_b_ada'], 'v_w_mlp_in': out['v_w_mlp_in'], 'v_w_mlp_out': out['v_w_mlp_out'], 'v_ret_w_in': out['v_ret_w_in'], 'v_ret_norm_gain': out['v_ret_norm_gain'], 'v_ret_w_out': out['v_ret_w_out'], 'v_kv_norm_gain': out['v_kv_norm_gain'], 'v_kv_w_ada': out['v_kv_w_ada'], 'v_kv_b_ada': out['v_kv_b_ada'], 'v_kv_w': out['v_kv_w'], 'v_forget_bias': out['v_forget_bias'], 'v_k_norm_gain': out['v_k_norm_gain'], 'v_fox_w_in': out['v_fox_w_in'], 'v_q_norm_gain': out['v_q_norm_gain'], 'v_fox_w_out': out['v_fox_w_out']}


def _loss(weights, diff, rest, loss_target):
    with _jax.named_scope("forward"):
        args = {**rest, TWIN_DIFF_INPUT: diff, **{k: w.astype(_WEIGHT_DTYPES[k]) for k, w in weights.items()}}
        y = _forward(args)
    with _jax.named_scope("loss_head"):
        err = _jnp.square(y.astype(_jnp.float32) - loss_target)
        return 0.5 * _jnp.sum(_jnp.mean(err, axis=-1)) if err.ndim else 0.5 * err


def _adamw(w, g, m, v):
    m = ADAM_B1 * m + (1.0 - ADAM_B1) * g
    v = ADAM_B2 * v + (1.0 - ADAM_B2) * _jnp.square(g)
    m_hat = m / (1.0 - ADAM_B1 ** ADAM_STEP)
    v_hat = v / (1.0 - ADAM_B2 ** ADAM_STEP)
    delta = -ADAM_LR * (m_hat / (_jnp.sqrt(v_hat) + ADAM_EPS) + ADAM_WD * w)
    return delta, m, v


def reference(x, c, positions, norm_mix_gain, norm_mlp_gain, w_ada, b_ada, w_mlp_in, w_mlp_out, ret_w_in, ret_norm_gain, ret_w_out, kv_norm_gain, kv_w_ada, kv_b_ada, kv_w, forget_bias, k_norm_gain, fox_w_in, q_norm_gain, fox_w_out, loss_target, m_norm_mix_gain, m_norm_mlp_gain, m_w_ada, m_b_ada, m_w_mlp_in, m_w_mlp_out, m_ret_w_in, m_ret_norm_gain, m_ret_w_out, m_kv_norm_gain, m_kv_w_ada, m_kv_b_ada, m_kv_w, m_forget_bias, m_k_norm_gain, m_fox_w_in, m_q_norm_gain, m_fox_w_out, v_norm_mix_gain, v_norm_mlp_gain, v_w_ada, v_b_ada, v_w_mlp_in, v_w_mlp_out, v_ret_w_in, v_ret_norm_gain, v_ret_w_out, v_kv_norm_gain, v_kv_w_ada, v_kv_b_ada, v_kv_w, v_forget_bias, v_k_norm_gain, v_fox_w_in, v_q_norm_gain, v_fox_w_out):
    given = dict(x=x, c=c, positions=positions, norm_mix_gain=norm_mix_gain, norm_mlp_gain=norm_mlp_gain, w_ada=w_ada, b_ada=b_ada, w_mlp_in=w_mlp_in, w_mlp_out=w_mlp_out, ret_w_in=ret_w_in, ret_norm_gain=ret_norm_gain, ret_w_out=ret_w_out, kv_norm_gain=kv_norm_gain, kv_w_ada=kv_w_ada, kv_b_ada=kv_b_ada, kv_w=kv_w, forget_bias=forget_bias, k_norm_gain=k_norm_gain, fox_w_in=fox_w_in, q_norm_gain=q_norm_gain, fox_w_out=fox_w_out, loss_target=loss_target, m_norm_mix_gain=m_norm_mix_gain, m_norm_mlp_gain=m_norm_mlp_gain, m_w_ada=m_w_ada, m_b_ada=m_b_ada, m_w_mlp_in=m_w_mlp_in, m_w_mlp_out=m_w_mlp_out, m_ret_w_in=m_ret_w_in, m_ret_norm_gain=m_ret_norm_gain, m_ret_w_out=m_ret_w_out, m_kv_norm_gain=m_kv_norm_gain, m_kv_w_ada=m_kv_w_ada, m_kv_b_ada=m_kv_b_ada, m_kv_w=m_kv_w, m_forget_bias=m_forget_bias, m_k_norm_gain=m_k_norm_gain, m_fox_w_in=m_fox_w_in, m_q_norm_gain=m_q_norm_gain, m_fox_w_out=m_fox_w_out, v_norm_mix_gain=v_norm_mix_gain, v_norm_mlp_gain=v_norm_mlp_gain, v_w_ada=v_w_ada, v_b_ada=v_b_ada, v_w_mlp_in=v_w_mlp_in, v_w_mlp_out=v_w_mlp_out, v_ret_w_in=v_ret_w_in, v_ret_norm_gain=v_ret_norm_gain, v_ret_w_out=v_ret_w_out, v_kv_norm_gain=v_kv_norm_gain, v_kv_w_ada=v_kv_w_ada, v_kv_b_ada=v_kv_b_ada, v_kv_w=v_kv_w, v_forget_bias=v_forget_bias, v_k_norm_gain=v_k_norm_gain, v_fox_w_in=v_fox_w_in, v_q_norm_gain=v_q_norm_gain, v_fox_w_out=v_fox_w_out)
    weights = {n: given[n] for n in TWIN_WEIGHTS}
    shared = {n: given[n] for n in SHARED_INPUTS}
    per_example = {n: given[n] for n in ['x', 'c', 'positions']}
    grad_fn = _jax.value_and_grad(_loss, argnums=(0, 1))

    def one_microbatch(ex, loss_target):
        ex = dict(ex)
        diff = ex.pop(TWIN_DIFF_INPUT)
        return grad_fn(weights, diff, {**shared, **ex}, loss_target)

    if N_MICROBATCH == 1:
        loss, (grad_w, grad_x) = one_microbatch(per_example, given["loss_target"])
    else:
        def body(carry, xs):
            loss_sum, grad_sum = carry
            l_k, (gw_k, gx_k) = one_microbatch(xs[0], xs[1])
            with _jax.named_scope("update"):
                return (loss_sum + l_k, _jax.tree.map(_jnp.add, grad_sum, gw_k)), gx_k

        init = (_jnp.zeros((), _jnp.float32), _jax.tree.map(_jnp.zeros_like, weights))
        (loss, grad_w), grad_x = _jax.lax.scan(body, init, (per_example, given["loss_target"]))
    with _jax.named_scope("update"):
        delta_w, new_m, new_v = {}, {}, {}
        for n in TWIN_WEIGHTS:
            delta_w[n], new_m[n], new_v[n] = _adamw(weights[n], grad_w[n], given["m_" + n], given["v_" + n])
    return (loss, grad_x, *[grad_w[n] for n in TWIN_WEIGHTS], *[delta_w[n] for n in TWIN_WEIGHTS],
            *[new_m[n] for n in TWIN_WEIGHTS], *[new_v[n] for n in TWIN_WEIGHTS])
```

```python
import functools
import math

import numpy as np
import jax
import jax.numpy as jnp
from jax import lax
from jax.experimental import pallas as pl
from jax.experimental.pallas import tpu as pltpu

F32 = jnp.float32
BF16 = jnp.bfloat16

N_DEV = 8
D_MODEL = 1024
RET_HEADS = 4
RET_QK = D_MODEL // RET_HEADS
RET_V = 2 * D_MODEL // RET_HEADS
RET_CHUNK = 128
ROPE_BASE = 10000.0
FOX_HEADS = 16
FOX_DH = D_MODEL // FOX_HEADS
EPS = 1e-6
LANES = 128
KV_PAD = 2 * D_MODEL + LANES

ADAM_LR = 0.001
ADAM_B1 = 0.9
ADAM_B2 = 0.999
ADAM_EPS = 1e-08
ADAM_WD = 0.01
ADAM_STEP = 10

VMEM_LIMIT_BYTES = 56 * 1024 * 1024


def _params(sem=None):
    return pltpu.CompilerParams(dimension_semantics=sem, vmem_limit_bytes=VMEM_LIMIT_BYTES)


def _me():
    return lax.axis_index("x"), lax.axis_index("y"), lax.axis_index("c")


def _peer(k):
    x, y, c = _me()
    return (1 - x if k & 4 else x, 1 - y if k & 2 else y, 1 - c if k & 1 else c)


def _peer_index(k):
    px, py, pc = _peer(k)
    return 4 * px + 2 * py + pc


def _exchange(name, xs, scatter):
    n = len(xs)

    def body(*refs):
        x_refs, o_refs = refs[:n], refs[n:2 * n]
        send_sems, recv_sems, local_sems = refs[2 * n:]
        x, y, c = _me()
        me = 4 * x + 2 * y + c
        local = []
        for i in range(n):
            src = x_refs[i].at[me] if scatter else x_refs[i]
            cp = pltpu.make_async_copy(src, o_refs[i].at[me], local_sems.at[i])
            cp.start()
            local.append(cp)
        remote = []
        for k in range(1, N_DEV):
            for i in range(n):
                src = x_refs[i].at[_peer_index(k)] if scatter else x_refs[i]
                cp = pltpu.make_async_remote_copy(
                    src_ref=src, dst_ref=o_refs[i].at[me],
                    send_sem=send_sems.at[(k - 1) * n + i], recv_sem=recv_sems.at[(k - 1) * n + i],
                    device_id=_peer(k), device_id_type=pl.DeviceIdType.MESH)
                cp.start()
                remote.append(cp)
        for cp in remote:
            cp.wait()
        for cp in local:
            cp.wait()

    out_shape = [jax.ShapeDtypeStruct(x.shape if scatter else (N_DEV,) + x.shape, x.dtype) for x in xs]
    any_spec = pl.BlockSpec(memory_space=pl.ANY)
    return pl.pallas_call(
        body, name=name, out_shape=out_shape,
        in_specs=[any_spec] * n, out_specs=[any_spec] * n,
        scratch_shapes=[pltpu.SemaphoreType.DMA(((N_DEV - 1) * n,)),
                        pltpu.SemaphoreType.DMA(((N_DEV - 1) * n,)),
                        pltpu.SemaphoreType.DMA((n,))],
    )(*xs)


def all_gather(name, xs):
    return _exchange(name, xs, scatter=False)


def all_to_all(name, xs):
    return _exchange(name, xs, scatter=True)


def _tile(n, cap):
    best = None
    for t in range(LANES, min(n, cap) + 1, LANES):
        if n % t == 0:
            best = t
    if best is None or (best < 256 and n <= 2304):
        return n
    return best


def _row_tile(m, cap):
    if m <= cap:
        return m
    t = cap
    while m % t:
        t //= 2
    return t


def _mm_body(nk, dims, a_ref, b_ref, o_ref, acc_ref):
    k = pl.program_id(2)

    @pl.when(k == 0)
    def _():
        acc_ref[...] = jnp.zeros_like(acc_ref)

    acc_ref[...] += lax.dot_general(a_ref[...].astype(BF16), b_ref[...].astype(BF16), (dims, ((), ())),
                                    preferred_element_type=F32)

    @pl.when(k == nk - 1)
    def _():
        o_ref[...] = acc_ref[...].astype(o_ref.dtype)


def mm_nn(name, a, w, out_dtype=F32):
    M, K = a.shape
    G, _, n = w.shape
    tn = _tile(n, 1024)
    tm = _row_tile(M, 512 if tn > 1024 else 1024)
    tk = _row_tile(K, 1024)
    r = n // tn
    nk = K // tk
    return pl.pallas_call(
        functools.partial(_mm_body, nk, ((1,), (0,))), name=name,
        out_shape=jax.ShapeDtypeStruct((M, G * n), out_dtype),
        grid=(M // tm, G * r, nk),
        in_specs=[pl.BlockSpec((tm, tk), lambda i, j, k: (i, k)),
                  pl.BlockSpec((None, tk, tn), lambda i, j, k: (j // r, k, j % r))],
        out_specs=pl.BlockSpec((tm, tn), lambda i, j, k: (i, j)),
        scratch_shapes=[pltpu.VMEM((tm, tn), F32)],
        compiler_params=_params(("parallel", "parallel", "arbitrary")),
    )(a, w)


def mm_nt(name, dy, w, out_dtype=F32):
    M, N = dy.shape
    G, K, n = w.shape
    tn = _tile(n, 1024)
    tm = _row_tile(M, 512 if tn > 1024 else 1024)
    tk = _row_tile(K, 1024)
    r = n // tn
    nk = G * r
    return pl.pallas_call(
        functools.partial(_mm_body, nk, ((1,), (1,))), name=name,
        out_shape=jax.ShapeDtypeStruct((M, K), out_dtype),
        grid=(M // tm, K // tk, nk),
        in_specs=[pl.BlockSpec((tm, tn), lambda i, j, k: (i, k)),
                  pl.BlockSpec((None, tk, tn), lambda i, j, k: (k // r, j, k % r))],
        out_specs=pl.BlockSpec((tm, tk), lambda i, j, k: (i, j)),
        scratch_shapes=[pltpu.VMEM((tm, tk), F32)],
        compiler_params=_params(("parallel", "parallel", "arbitrary")),
    )(dy, w)


def mm_tn(name, a, dy, G, out_dtype=F32):
    M, K = a.shape
    N = dy.shape[1]
    n = N // G
    tn = _tile(n, 1024)
    tk = _row_tile(K, 512 if tn > 1024 else 1024)
    tm = _row_tile(M, 1024)
    r = n // tn
    nk = M // tm
    return pl.pallas_call(
        functools.partial(_mm_body, nk, ((0,), (0,))), name=name,
        out_shape=jax.ShapeDtypeStruct((G, K, n), out_dtype),
        grid=(K // tk, G * r, nk),
        in_specs=[pl.BlockSpec((tm, tk), lambda i, j, k: (k, i)),
                  pl.BlockSpec((tm, tn), lambda i, j, k: (k, j))],
        out_specs=pl.BlockSpec((None, tk, tn), lambda i, j, k: (j // r, i, j % r)),
        scratch_shapes=[pltpu.VMEM((tk, tn), F32)],
        compiler_params=_params(("parallel", "parallel", "arbitrary")),
    )(a, dy)


def rowwise(name, fn, rows, vecs, outs, accs=(), tm=256):
    rows = [r if isinstance(r, tuple) else (r, r.shape[1], 0) for r in rows]
    S = rows[0][0].shape[0]
    tm = _row_tile(S, tm)
    n_r, n_v, n_o, n_a = len(rows), len(vecs), len(outs), len(accs)

    def body(*refs):
        ins = [ref[...] for ref in refs[:n_r + n_v]]
        o_refs = refs[n_r + n_v:n_r + n_v + n_o]
        a_refs = refs[n_r + n_v + n_o:]
        res = fn(*ins)
        res = res if isinstance(res, (tuple, list)) else (res,)
        for ref, val in zip(o_refs, res[:n_o]):
            ref[...] = val.astype(ref.dtype)
        if n_a:
            @pl.when(pl.program_id(0) == 0)
            def _():
                for ref in a_refs:
                    ref[...] = jnp.zeros_like(ref)
            for ref, val in zip(a_refs, res[n_o:]):
                ref[...] += val

    in_specs = [pl.BlockSpec((tm, w), functools.partial(lambda cb, i: (i, cb), cb)) for _, w, cb in rows]
    in_specs += [pl.BlockSpec(v.shape, lambda i: (0, 0)) for v in vecs]
    out_specs = [pl.BlockSpec((tm, w), lambda i: (i, 0)) for w, _ in outs]
    out_specs += [pl.BlockSpec(a, lambda i: (0, 0)) for a in accs]
    out_shape = [jax.ShapeDtypeStruct((S, w), dt) for w, dt in outs]
    out_shape += [jax.ShapeDtypeStruct(a, F32) for a in accs]
    res = pl.pallas_call(
        body, name=name, out_shape=out_shape, grid=(S // tm,),
        in_specs=in_specs, out_specs=out_specs,
        compiler_params=_params(("arbitrary",)),
    )(*[r[0] for r in rows], *vecs)
    return res


def _rms(x):
    return x * lax.rsqrt(jnp.mean(x * x, axis=-1, keepdims=True) + EPS)


def _modulate(x, gain, scale, shift):
    return _rms(x) * gain * (1.0 + scale) + shift


def _ada_slices(ada_raw, bias):
    ada = ada_raw + bias
    return [ada[:, i * D_MODEL:(i + 1) * D_MODEL] for i in range(ada.shape[1] // D_MODEL)]


def _norm_wide_heads(y, gain, heads):
    w = y.shape[1] // heads
    return jnp.concatenate([_rms(y[:, h * w:(h + 1) * w]) * gain[:, h * w:(h + 1) * w] for h in range(heads)], axis=1)


def _norm_fox_heads(x, gain):
    outs = []
    for p in range(x.shape[1] // LANES):
        blk = x[:, p * LANES:(p + 1) * LANES]
        low = lax.broadcasted_iota(jnp.int32, blk.shape, 1) < FOX_DH
        sq = blk * blk
        ss_low = jnp.sum(jnp.where(low, sq, 0.0), axis=1, keepdims=True)
        ss_high = jnp.sum(jnp.where(low, 0.0, sq), axis=1, keepdims=True)
        outs.append(blk * lax.rsqrt(jnp.where(low, ss_low, ss_high) * (1.0 / FOX_DH) + EPS))
    return jnp.concatenate(outs, axis=1) * gain


def _silu(x):
    return x * jax.nn.sigmoid(x)


def _log_sigmoid(z):
    return -(jnp.maximum(-z, 0.0) + jnp.log(1.0 + jnp.exp(-jnp.abs(z))))


def _rotate(x, cos, sin, heads, sign):
    w = x.shape[1] // heads
    half = w // 2
    outs = []
    for h in range(heads):
        x1 = x[:, h * w:h * w + half]
        x2 = x[:, h * w + half:(h + 1) * w]
        outs += [x1 * cos - sign * x2 * sin, sign * x1 * sin + x2 * cos]
    return jnp.concatenate(outs, axis=1)


def _vjp(fn, primals, ct):
    return jax.vjp(fn, *primals)[1](ct)


_LOG_GAMMAS = [float(np.log(np.float32(1.0) - np.float32(2.0) ** np.float32(-5.0 - h))) for h in range(RET_HEADS)]


def retention(name, q, k, v, reverse):
    (qa, dk, qo), (ka, _, ko), (va, dv, vo) = q, k, v
    S = qa.shape[0]
    C = RET_CHUNK
    nc = S // C

    def body(q_ref, k_ref, v_ref, o_ref, state):
        h = pl.program_id(0)

        @pl.when(pl.program_id(1) == 0)
        def _():
            state[...] = jnp.zeros_like(state)

        log_g = jnp.float32(_LOG_GAMMAS[RET_HEADS - 1])
        for i in range(RET_HEADS - 2, -1, -1):
            log_g = jnp.where(h == i, jnp.float32(_LOG_GAMMAS[i]), log_g)
        row = lax.broadcasted_iota(jnp.int32, (C, C), 0)
        col = lax.broadcasted_iota(jnp.int32, (C, C), 1)
        rel = (col - row if reverse else row - col).astype(F32)
        decay = jnp.where(rel >= 0, jnp.exp(log_g * jnp.maximum(rel, 0.0)), 0.0)
        j = lax.broadcasted_iota(jnp.int32, (C, 1), 0).astype(F32)
        q_decay = jnp.exp(log_g * (C - j if reverse else j + 1.0))
        k_decay = jnp.exp(log_g * (j if reverse else C - 1.0 - j))
        chunk_decay = jnp.exp(jnp.full((1, 1), log_g * C, F32))

        qc = q_ref[...].astype(BF16)
        kf = k_ref[...].astype(F32)
        vc = v_ref[...].astype(BF16)
        scores = lax.dot_general(qc, kf.astype(BF16), (((1,), (1,)), ((), ())), preferred_element_type=F32) * decay
        intra = jnp.dot(scores.astype(BF16), vc, preferred_element_type=F32)
        cross = jnp.dot(qc, state[...].astype(BF16), preferred_element_type=F32) * q_decay
        o_ref[...] = intra + cross
        upd = lax.dot_general((kf * k_decay).astype(BF16), vc, (((0,), (0,)), ((), ())), preferred_element_type=F32)
        state[...] = state[...] * chunk_decay + upd

    def chunk(i):
        return nc - 1 - i if reverse else i

    return pl.pallas_call(
        body, name=name, out_shape=jax.ShapeDtypeStruct((S, RET_HEADS * dv), F32),
        grid=(RET_HEADS, nc),
        in_specs=[pl.BlockSpec((C, dk), lambda h, i: (chunk(i), qo + h)),
                  pl.BlockSpec((C, dk), lambda h, i: (chunk(i), ko + h)),
                  pl.BlockSpec((C, dv), lambda h, i: (chunk(i), vo + h))],
        out_specs=pl.BlockSpec((C, dv), lambda h, i: (chunk(i), h)),
        scratch_shapes=[pltpu.VMEM((dk, dv), F32)],
        compiler_params=_params(("parallel", "arbitrary")),
    )(qa, ka, va)


FOX_TQ = 256
N_PAIR = FOX_HEADS // 2


def _fox_logits(q2, k2, f_blk, ft_ref, pair, a, i, tq):
    S = k2.shape[0]
    head = 2 * pair + a
    low = lax.broadcasted_iota(jnp.int32, (1, LANES), 1) < FOX_DH
    mask = low if a == 0 else jnp.logical_not(low)
    qa = jnp.where(mask, q2, 0.0).astype(BF16)
    lane = lax.broadcasted_iota(jnp.int32, f_blk.shape, 1)
    fq = jnp.sum(jnp.where(lane == head, f_blk, 0.0), axis=1, keepdims=True)
    fk = ft_ref[pl.ds(head, 1), :]
    logits = lax.dot_general(qa, k2, (((1,), (1,)), ((), ())), preferred_element_type=F32) * (FOX_DH ** -0.5)
    logits = logits + fq - fk
    row = i * tq + lax.broadcasted_iota(jnp.int32, (tq, S), 0)
    col = lax.broadcasted_iota(jnp.int32, (tq, S), 1)
    causal = col <= row
    return jnp.where(causal, logits, -jnp.inf), causal, mask, qa


def fox_forward(name, qn, kn, kvf, f_cum, f_cum_t):
    S = qn.shape[0]
    tq = _row_tile(S, FOX_TQ)
    v_block0 = D_MODEL // LANES

    def body(q_ref, k_ref, v_ref, f_ref, ft_ref, y_ref, lse_ref):
        pair, i = pl.program_id(0), pl.program_id(1)
        q2 = q_ref[...]
        k2 = k_ref[...].astype(BF16)
        v2 = v_ref[...].astype(BF16)
        f_blk = f_ref[...]
        ys, lses = [], []
        for a in range(2):
            logits, _, mask, _ = _fox_logits(q2, k2, f_blk, ft_ref, pair, a, i, tq)
            m = jnp.max(logits, axis=1, keepdims=True)
            e = jnp.exp(logits - m)
            l = jnp.sum(e, axis=1, keepdims=True)
            p = e / l
            ys.append(jnp.dot(p.astype(BF16), v2, preferred_element_type=F32))
            lses.append(m + jnp.log(l))
        low = lax.broadcasted_iota(jnp.int32, (1, LANES), 1) < FOX_DH
        y_ref[...] = jnp.where(low, ys[0], ys[1])
        lse_ref[...] = jnp.where(low, lses[0], lses[1])

    return pl.pallas_call(
        body, name=name,
        out_shape=[jax.ShapeDtypeStruct((S, D_MODEL), F32), jax.ShapeDtypeStruct((S, D_MODEL), F32)],
        grid=(N_PAIR, S // tq),
        in_specs=[pl.BlockSpec((tq, LANES), lambda p, i: (i, p)),
                  pl.BlockSpec((S, LANES), lambda p, i: (0, p)),
                  pl.BlockSpec((S, LANES), lambda p, i: (0, v_block0 + p)),
                  pl.BlockSpec((tq, LANES), lambda p, i: (i, 0)),
                  pl.BlockSpec((LANES, S), lambda p, i: (0, 0))],
        out_specs=[pl.BlockSpec((tq, LANES), lambda p, i: (i, p)),
                   pl.BlockSpec((tq, LANES), lambda p, i: (i, p))],
        compiler_params=_params(("parallel", "arbitrary")),
    )(qn, kn, kvf, f_cum, f_cum_t)


def fox_backward(name, qn, kn, kvf, f_cum, f_cum_t, y, dy, lse):
    S = qn.shape[0]
    tq = _row_tile(S, FOX_TQ)
    v_block0 = D_MODEL // LANES
    scale = FOX_DH ** -0.5

    def body(q_ref, k_ref, v_ref, f_ref, ft_ref, y_ref, dy_ref, lse_ref,
             dq_ref, dk_ref, dv_ref, dfq_ref, dfk_ref):
        pair, i = pl.program_id(0), pl.program_id(1)

        @pl.when(i == 0)
        def _():
            dk_ref[...] = jnp.zeros_like(dk_ref)
            dv_ref[...] = jnp.zeros_like(dv_ref)
            dfk_ref[...] = jnp.zeros_like(dfk_ref)

        q2 = q_ref[...]
        k2 = k_ref[...].astype(BF16)
        v2 = v_ref[...].astype(BF16)
        f_blk = f_ref[...]
        y2, dy2, lse2 = y_ref[...], dy_ref[...], lse_ref[...]
        dqs, dfq = [], jnp.zeros((tq, LANES), F32)
        lane = lax.broadcasted_iota(jnp.int32, (tq, LANES), 1)
        for a in range(2):
            logits, causal, mask, qa = _fox_logits(q2, k2, f_blk, ft_ref, pair, a, i, tq)
            lse_a = jnp.max(jnp.where(mask, lse2, -jnp.inf), axis=1, keepdims=True)
            p = jnp.where(causal, jnp.exp(logits - lse_a), 0.0)
            dy_a = jnp.where(mask, dy2, 0.0)
            delta = jnp.sum(dy_a * y2, axis=1, keepdims=True)
            dy_b = dy_a.astype(BF16)
            dp = lax.dot_general(dy_b, v2, (((1,), (1,)), ((), ())), preferred_element_type=F32)
            ds = p * (dp - delta)
            dfq = dfq + jnp.where(lane == 2 * pair + a, jnp.sum(ds, axis=1, keepdims=True), 0.0)
            dfk_ref[pl.ds(a, 1), :] += -jnp.sum(ds, axis=0, keepdims=True)
            ds_b = ds.astype(BF16)
            dqs.append(jnp.dot(ds_b, k2, preferred_element_type=F32) * scale)
            dk_ref[...] += lax.dot_general(ds_b, qa, (((0,), (0,)), ((), ())), preferred_element_type=F32) * scale
            dv_ref[...] += lax.dot_general(p.astype(BF16), dy_b, (((0,), (0,)), ((), ())), preferred_element_type=F32)
        low = lax.broadcasted_iota(jnp.int32, (1, LANES), 1) < FOX_DH
        dq_ref[...] = jnp.where(low, dqs[0], dqs[1])
        dfq_ref[...] = dfq

    row_blk = pl.BlockSpec((tq, LANES), lambda p, i: (i, p))
    col_blk = pl.BlockSpec((S, LANES), lambda p, i: (0, p))
    return pl.pallas_call(
        body, name=name,
        out_shape=[jax.ShapeDtypeStruct((S, D_MODEL), F32)] * 3
        + [jax.ShapeDtypeStruct((N_PAIR, S, LANES), F32), jax.ShapeDtypeStruct((N_PAIR, 8, S), F32)],
        grid=(N_PAIR, S // tq),
        in_specs=[row_blk, col_blk,
                  pl.BlockSpec((S, LANES), lambda p, i: (0, v_block0 + p)),
                  pl.BlockSpec((tq, LANES), lambda p, i: (i, 0)),
                  pl.BlockSpec((LANES, S), lambda p, i: (0, 0)),
                  row_blk, row_blk, row_blk],
        out_specs=[row_blk, col_blk, col_blk,
                   pl.BlockSpec((None, tq, LANES), lambda p, i: (p, i, 0)),
                   pl.BlockSpec((None, 8, S), lambda p, i: (p, 0, 0))],
        compiler_params=_params(("parallel", "arbitrary")),
    )(qn, kn, kvf, f_cum, f_cum_t, y, dy, lse)


def cumsum_rows(name, x, reverse):
    S = x.shape[0]
    C = LANES
    nc = S // C

    def body(x_ref, o_ref):
        row = lax.broadcasted_iota(jnp.int32, (C, C), 0)
        col = lax.broadcasted_iota(jnp.int32, (C, C), 1)
        tri = jnp.where(col >= row if reverse else col <= row, 1.0, 0.0).astype(F32)
        carry = jnp.zeros((1, LANES), F32)
        for i in (range(nc - 1, -1, -1) if reverse else range(nc)):
            blk = x_ref[i * C:(i + 1) * C, :]
            loc = jnp.dot(tri, blk, preferred_element_type=F32, precision=lax.Precision.HIGHEST)
            o_ref[i * C:(i + 1) * C, :] = loc + carry
            carry = carry + (loc[0:1, :] if reverse else loc[C - 1:C, :])

    return pl.pallas_call(body, name=name, out_shape=jax.ShapeDtypeStruct((S, LANES), F32),
                          compiler_params=_params())(x)


def adamw(name, parts, w, m, v):
    P, R, C = parts.shape
    tr = _row_tile(R, 256)

    def body(p_ref, w_ref, m_ref, v_ref, g_out, d_out, m_out, v_out):
        g = p_ref[0].astype(F32)
        for i in range(1, P):
            g = g + p_ref[i].astype(F32)
        m2 = ADAM_B1 * m_ref[...] + (1.0 - ADAM_B1) * g
        v2 = ADAM_B2 * v_ref[...] + (1.0 - ADAM_B2) * jnp.square(g)
        m_hat = m2 / (1.0 - ADAM_B1 ** ADAM_STEP)
        v_hat = v2 / (1.0 - ADAM_B2 ** ADAM_STEP)
        g_out[...] = g
        d_out[...] = -ADAM_LR * (m_hat / (jnp.sqrt(v_hat) + ADAM_EPS) + ADAM_WD * w_ref[...])
        m_out[...] = m2
        v_out[...] = v2

    blk = pl.BlockSpec((tr, C), lambda i: (i, 0))
    return pl.pallas_call(
        body, name=name, out_shape=[jax.ShapeDtypeStruct((R, C), F32)] * 4, grid=(R // tr,),
        in_specs=[pl.BlockSpec((P, tr, C), lambda i: (0, i, 0)), blk, blk, blk],
        out_specs=[blk] * 4, compiler_params=_params(("parallel",)),
    )(parts, w, m, v)


def kernel(x, c, positions, norm_mix_gain, norm_mlp_gain, w_ada, b_ada, w_mlp_in, w_mlp_out, ret_w_in, ret_norm_gain, ret_w_out, kv_norm_gain, kv_w_ada, kv_b_ada, kv_w, forget_bias, k_norm_gain, fox_w_in, q_norm_gain, fox_w_out, loss_target, m_norm_mix_gain, m_norm_mlp_gain, m_w_ada, m_b_ada, m_w_mlp_in, m_w_mlp_out, m_ret_w_in, m_ret_norm_gain, m_ret_w_out, m_kv_norm_gain, m_kv_w_ada, m_kv_b_ada, m_kv_w, m_forget_bias, m_k_norm_gain, m_fox_w_in, m_q_norm_gain, m_fox_w_out, v_norm_mix_gain, v_norm_mlp_gain, v_w_ada, v_b_ada, v_w_mlp_in, v_w_mlp_out, v_ret_w_in, v_ret_norm_gain, v_ret_w_out, v_kv_norm_gain, v_kv_w_ada, v_kv_b_ada, v_kv_w, v_forget_bias, v_k_norm_gain, v_fox_w_in, v_q_norm_gain, v_fox_w_out):
    D = D_MODEL
    S = x.shape[1]
    x0 = x.reshape(S, D)
    target = loss_target.reshape(S, D)
    me = 4 * lax.axis_index("x") + 2 * lax.axis_index("y") + lax.axis_index("c")
    n_ada = w_ada.shape[2]
    n_kvada = kv_w_ada.shape[1]
    n_kv = kv_w.shape[1]

    (c_all, g_ret_in, g_mlp_in, g_mlp_out, g_ret_out, g_kv, g_fox_in, g_fox_out) = all_gather("gather_weights", [
        c.reshape(1, D), ret_w_in[0].astype(BF16), w_mlp_in.astype(BF16), w_mlp_out.astype(BF16),
        ret_w_out[0].astype(BF16), kv_w.astype(BF16), fox_w_in[0].astype(BF16), fox_w_out[0].astype(BF16)])
    W_ret_in = g_ret_in
    W_mlp_in = [g_mlp_in[:, l] for l in range(2)]
    W_mlp_out = [g_mlp_out[:, l].reshape(1, 4 * D, D) for l in range(2)]
    W_ret_out = g_ret_out.reshape(1, 2 * D, D)
    W_fox_in = g_fox_in
    W_fox_out = g_fox_out.reshape(1, D, D)
    kv_full = jnp.transpose(g_kv, (1, 0, 2)).reshape(D, N_DEV * n_kv)
    W_kv = jnp.pad(kv_full, ((0, 0), (0, KV_PAD - N_DEV * n_kv)))[None]

    c_act = rowwise("silu_c", _silu, [c_all.reshape(N_DEV, D)], [], [(D, F32)])[0]
    w_ada_cat = jnp.concatenate([w_ada[0], w_ada[1], kv_w_ada], axis=1).astype(BF16)[None]
    n_cat = 2 * n_ada + n_kvada
    ada_part = mm_nn("ada_proj", c_act, w_ada_cat)
    ada_mine = all_to_all("ada_rows", [ada_part.reshape(N_DEV, 1, n_cat)])[0][:, 0]
    ada_raw = [ada_mine[:, l * n_ada:(l + 1) * n_ada].reshape(1, 6 * D) for l in range(2)]
    kvada_raw = ada_mine[:, 2 * n_ada:].reshape(1, 2 * D)
    kv_bias = kv_b_ada.reshape(1, 2 * D)
    kv_gain = kv_norm_gain.reshape(1, D)
    fb = jnp.pad(forget_bias.reshape(1, FOX_HEADS), ((0, 0), (0, LANES - FOX_HEADS)))
    k_gain = jnp.tile(k_norm_gain.reshape(1, FOX_DH), (1, FOX_HEADS))
    q_gain = jnp.tile(q_norm_gain.reshape(1, FOX_DH), (1, FOX_HEADS))
    ret_gain = all_gather("gather_ret_gain", [ret_norm_gain.reshape(RET_HEADS, -1)])[0]
    ret_gain = jnp.transpose(ret_gain, (1, 0, 2)).reshape(1, RET_HEADS * RET_V)

    pos = positions.reshape(S, 1).astype(F32)
    half = RET_QK // 2
    inv_freq = jnp.asarray((ROPE_BASE ** (-np.arange(half, dtype=np.float32) / half)).reshape(1, half), F32)

    def angles(p, f):
        ang = p * f
        return jnp.cos(ang), jnp.sin(ang)

    cos, sin = rowwise("rope_table", angles, [pos], [inv_freq], [(half, F32), (half, F32)])

    def mod_mix(layer):
        def fn(xb, ada, bias, gain):
            sh, sc = _ada_slices(ada, bias)[:2]
            return _modulate(xb, gain[layer:layer + 1], sc, sh)
        return fn

    def mod_mlp(layer):
        def fn(xb, ada, bias, gain):
            sh, sc = _ada_slices(ada, bias)[3:5]
            return _modulate(xb, gain[layer:layer + 1], sc, sh)
        return fn

    h1_0 = rowwise("mod_mix0", mod_mix(0), [x0], [ada_raw[0], b_ada[0:1], norm_mix_gain], [(D, BF16)])[0]
    proj = mm_nn("ret_proj", h1_0, W_ret_in)

    def rope_fwd(qb, kb, cs, sn):
        return _rotate(qb, cs, sn, RET_HEADS, 1.0), _rotate(kb, cs, sn, RET_HEADS, 1.0) * (RET_QK ** -0.5)

    q_rot, k_rot = rowwise("rope", rope_fwd, [(proj, D, 0), (proj, D, 1), cos, sin], [], [(D, F32), (D, F32)])
    v_ret = (proj, RET_V, (2 * D) // RET_V)
    y_ret = retention("ret_fwd", (q_rot, RET_QK, 0), (k_rot, RET_QK, 0), v_ret, reverse=False)

    def ret_gate(yb, gb, gain):
        return _silu(gb) * _norm_wide_heads(yb, gain, RET_HEADS)

    mixin0 = rowwise("ret_gate", ret_gate, [y_ret, (proj, 2 * D, 2)], [ret_gain], [(2 * D, BF16)])[0]
    mix0 = mm_nn("ret_out", mixin0, W_ret_out)

    def residual_mod(layer, slot):
        def fn(xb, bb, ada, bias, gain):
            s = _ada_slices(ada, bias)
            xn = xb + s[2] * bb
            return xn, _modulate(xn, gain[layer:layer + 1], s[4], s[3])
        return fn

    x1, h2_0 = rowwise("res_mix0", residual_mod(0, 0), [x0, mix0], [ada_raw[0], b_ada[0:1], norm_mlp_gain],
                       [(D, F32), (D, BF16)])

    def mlp_forward(tag, h2, layer):
        u = mm_nn("mlp_in" + tag, h2, W_mlp_in[layer])
        act = rowwise("relu2" + tag, lambda ub: jnp.square(jnp.maximum(ub, 0.0)), [u], [], [(4 * D, BF16)])[0]
        return u, act, mm_nn("mlp_out" + tag, act, W_mlp_out[layer])

    u0, act0, mlp0 = mlp_forward("0", h2_0, 0)

    def res_mlp0(xb, bb, ada0, bias0, ada1, bias1, kva, kvb, gain_mix, gain_kv):
        xn = xb + _ada_slices(ada0, bias0)[5] * bb
        s1 = _ada_slices(ada1, bias1)
        kv_shift, kv_scale = _ada_slices(kva, kvb)
        return xn, _modulate(xn, gain_kv, kv_scale, kv_shift), _modulate(xn, gain_mix[1:2], s1[1], s1[0])

    x2, h_kv, h1_1 = rowwise("res_mlp0", res_mlp0, [x1, mlp0],
                             [ada_raw[0], b_ada[0:1], ada_raw[1], b_ada[1:2], kvada_raw, kv_bias, norm_mix_gain, kv_gain],
                             [(D, F32), (D, BF16), (D, BF16)])

    kvf = mm_nn("kv_proj", h_kv, W_kv)

    def kv_post(kb, fblk, kg, bias):
        head = lax.broadcasted_iota(jnp.int32, fblk.shape, 1) < FOX_HEADS
        return _norm_fox_heads(kb, kg), jnp.where(head, _log_sigmoid(fblk + bias), 0.0)

    kn, log_f = rowwise("kv_post", kv_post, [(kvf, D, 0), (kvf, LANES, 2 * D // LANES)], [k_gain, fb],
                        [(D, F32), (LANES, F32)])
    f_cum = cumsum_rows("f_cumsum", log_f, reverse=False)
    f_cum_t = f_cum.T

    qo = mm_nn("fox_proj", h1_1, W_fox_in)
    qn = rowwise("q_norm", _norm_fox_heads, [(qo, D, 0)], [q_gain], [(D, F32)])[0]
    y_att, lse = fox_forward("fox_fwd", qn, kn, kvf, f_cum, f_cum_t)
    mixin1 = rowwise("fox_gate", lambda ob, yb: jax.nn.sigmoid(ob) * yb, [(qo, D, 1), y_att], [], [(D, BF16)])[0]
    mix1 = mm_nn("fox_out", mixin1, W_fox_out)
    x3, h2_1 = rowwise("res_mix1", residual_mod(1, 0), [x2, mix1], [ada_raw[1], b_ada[1:2], norm_mlp_gain],
                       [(D, F32), (D, BF16)])
    u1, act1, mlp1 = mlp_forward("1", h2_1, 1)

    def loss_head(xb, bb, tb, ada, bias):
        g2 = _ada_slices(ada, bias)[5]
        err = xb + g2 * bb - tb
        dx = err * (1.0 / D)
        loss = 0.5 * jnp.sum(jnp.sum(err * err, axis=1, keepdims=True) * (1.0 / D), axis=0, keepdims=True)
        return dx, (dx * g2), jnp.broadcast_to(loss, (1, LANES)), jnp.sum(dx * bb, axis=0, keepdims=True)

    dx4, dmlp1, loss_acc, dg2_1 = rowwise("loss_head", loss_head, [x3, mlp1, target], [ada_raw[1], b_ada[1:2]],
                                          [(D, F32), (D, BF16)], [(1, LANES), (1, D)])
    loss = lax.psum(loss_acc[0, 0], ("x", "y", "c"))

    def mlp_backward(tag, dmlp, act, u, h2, layer):
        d_act = mm_nt("mlp_out_dx" + tag, dmlp, W_mlp_out[layer])
        gw_out = mm_tn("mlp_out_dw" + tag, act, dmlp, 1, BF16).reshape(N_DEV, -1, D)
        du = rowwise("relu2_bwd" + tag, lambda db, ub: db * (2.0 * jnp.maximum(ub, 0.0)), [d_act, u], [], [(4 * D, BF16)])[0]
        dh2 = mm_nt("mlp_in_dx" + tag, du, W_mlp_in[layer])
        gw_in = mm_tn("mlp_in_dw" + tag, h2, du, N_DEV, BF16)
        return dh2, gw_in, gw_out

    def mod_backward(layer, slots, gate_slot):
        def fn(xb, dhb, dresb, branchb, ada, bias, gain):
            s = _ada_slices(ada, bias)
            g = gain[layer:layer + 1]
            dx, dgain, dsc, dsh = _vjp(_modulate, (xb, g, s[slots[1]], s[slots[0]]), dhb)
            dx = dx + dresb
            d_branch = dx * s[gate_slot]
            return dx, d_branch, dgain, dsc, dsh, jnp.sum(dx * branchb, axis=0, keepdims=True)
        return fn

    vec = (1, D)
    dh2_1, gw_mlp_in1, gw_mlp_out1 = mlp_backward("1", dmlp1, act1, u1, h2_1, 1)
    dx3, dmix1, dgain_mlp1, dsc2_1, dsh2_1, dg1_1 = rowwise(
        "mod_mlp1_bwd", mod_backward(1, (3, 4), 2), [x3, dh2_1, dx4, mix1], [ada_raw[1], b_ada[1:2], norm_mlp_gain],
        [(D, F32), (D, BF16)], [vec] * 4)
    dmixin1 = mm_nt("fox_out_dx", dmix1, W_fox_out)
    gw_fox_out = mm_tn("fox_out_dw", mixin1, dmix1, 1, BF16).reshape(N_DEV, -1, D)

    def fox_gate_bwd(db, ob, yb):
        sg = jax.nn.sigmoid(ob)
        return db * sg, db * yb * sg * (1.0 - sg)

    dy_att, d_og = rowwise("fox_gate_bwd", fox_gate_bwd, [dmixin1, (qo, D, 1), y_att], [], [(D, F32), (D, F32)])
    dqn, dkn, dv_att, dfq, dfk = fox_backward("fox_bwd", qn, kn, kvf, f_cum, f_cum_t, y_att, dy_att, lse)

    def q_norm_bwd(qb, db, ogb, gain):
        dq, dgain = _vjp(_norm_fox_heads, (qb, gain), db)
        return jnp.concatenate([dq, ogb], axis=1), dgain

    dqo, dq_gain = rowwise("q_norm_bwd", q_norm_bwd, [(qo, D, 0), dqn, d_og], [q_gain], [(2 * D, BF16)], [vec])
    dh1_1 = mm_nt("fox_proj_dx", dqo, W_fox_in)
    gw_fox_in = mm_tn("fox_proj_dw", h1_1, dqo, N_DEV, BF16)

    dfk_rows = jnp.pad(dfk[:, :2, :].reshape(FOX_HEADS, S).T, ((0, 0), (0, LANES - FOX_HEADS)))

    def df_total(*blks):
        tot = blks[0]
        for b in blks[1:]:
            tot = tot + b
        return tot

    d_fcum = rowwise("df_sum", df_total, [dfk_rows] + [dfq[p] for p in range(N_PAIR)], [], [(LANES, F32)])[0]
    d_logf = cumsum_rows("df_cumsum", d_fcum, reverse=True)

    def kv_post_bwd(kb, fblk, dkb, dvb, dlf, kg, bias):
        dk, dgain = _vjp(_norm_fox_heads, (kb, kg), dkb)
        df = dlf * (1.0 / (1.0 + jnp.exp(fblk + bias)))
        return jnp.concatenate([dk, dvb, df], axis=1), dgain, jnp.sum(df, axis=0, keepdims=True)

    dkvf, dk_gain, dfb = rowwise("kv_post_bwd", kv_post_bwd,
                                 [(kvf, D, 0), (kvf, LANES, 2 * D // LANES), dkn, dv_att, d_logf], [k_gain, fb],
                                 [(KV_PAD, BF16)], [vec, (1, LANES)])
    dh_kv = mm_nt("kv_proj_dx", dkvf, W_kv)
    gw_kv = mm_tn("kv_proj_dw", h_kv, dkvf, 1, BF16)[0, :, :N_DEV * n_kv]
    gw_kv = jnp.transpose(gw_kv.reshape(D, N_DEV, n_kv), (1, 0, 2))

    def x2_bwd(xb, dh1b, dhkb, dresb, branchb, ada0, bias0, ada1, bias1, kva, kvb, gain_mix, gain_kv):
        s1 = _ada_slices(ada1, bias1)
        kv_shift, kv_scale = _ada_slices(kva, kvb)
        dxa, dgain_mix, dsc1, dsh1 = _vjp(_modulate, (xb, gain_mix[1:2], s1[1], s1[0]), dh1b)
        dxb, dgain_kv, dkv_scale, dkv_shift = _vjp(_modulate, (xb, gain_kv, kv_scale, kv_shift), dhkb)
        dx = dresb + dxa + dxb
        g2 = _ada_slices(ada0, bias0)[5]
        return (dx, dx * g2, dgain_mix, dsc1, dsh1, dgain_kv, dkv_scale, dkv_shift,
                jnp.sum(dx * branchb, axis=0, keepdims=True))

    (dx2, dmlp0, dgain_mix1, dsc1_1, dsh1_1, dgain_kv, dkv_scale, dkv_shift, dg2_0) = rowwise(
        "x2_bwd", x2_bwd, [x2, dh1_1, dh_kv, dx3, mlp0],
        [ada_raw[0], b_ada[0:1], ada_raw[1], b_ada[1:2], kvada_raw, kv_bias, norm_mix_gain, kv_gain],
        [(D, F32), (D, BF16)], [vec] * 7)

    dh2_0, gw_mlp_in0, gw_mlp_out0 = mlp_backward("0", dmlp0, act0, u0, h2_0, 0)
    dx1, dmix0, dgain_mlp0, dsc2_0, dsh2_0, dg1_0 = rowwise(
        "mod_mlp0_bwd", mod_backward(0, (3, 4), 2), [x1, dh2_0, dx2, mix0], [ada_raw[0], b_ada[0:1], norm_mlp_gain],
        [(D, F32), (D, BF16)], [vec] * 4)
    dmixin0 = mm_nt("ret_out_dx", dmix0, W_ret_out)
    gw_ret_out = mm_tn("ret_out_dw", mixin0, dmix0, 1, BF16).reshape(N_DEV, -1, D)

    def ret_gate_bwd(db, yb, gb, gain):
        dy, dg, dgain = _vjp(lambda y_, g_, gn_: ret_gate(y_, g_, gn_), (yb, gb, gain), db)
        return dy, dg, dgain

    dy_ret, dgate, dret_gain = rowwise("ret_gate_bwd", ret_gate_bwd, [dmixin0, y_ret, (proj, 2 * D, 2)], [ret_gain],
                                       [(2 * D, F32), (2 * D, BF16)], [(1, 2 * D)])
    dy_h = (dy_ret, RET_V, 0)
    dq_rot = retention("ret_dq", dy_h, v_ret, (k_rot, RET_QK, 0), reverse=False)
    dk_rot = retention("ret_dk", v_ret, dy_h, (q_rot, RET_QK, 0), reverse=True)
    dv_ret = retention("ret_dv", (k_rot, RET_QK, 0), (q_rot, RET_QK, 0), dy_h, reverse=True)

    def rope_bwd(dqb, dkb, dvb, dgb, cs, sn):
        dq = _rotate(dqb, cs, sn, RET_HEADS, -1.0)
        dk = _rotate(dkb, cs, sn, RET_HEADS, -1.0) * (RET_QK ** -0.5)
        return jnp.concatenate([dq, dk, dvb, dgb.astype(F32)], axis=1)

    dproj = rowwise("rope_bwd", rope_bwd, [dq_rot, dk_rot, dv_ret, dgate, cos, sin], [], [(6 * D, BF16)])[0]
    dh1_0 = mm_nt("ret_proj_dx", dproj, W_ret_in)
    gw_ret_in = mm_tn("ret_proj_dw", h1_0, dproj, N_DEV, BF16)

    def x0_bwd(xb, dhb, dresb, ada, bias, gain):
        s = _ada_slices(ada, bias)
        dx, dgain, dsc, dsh = _vjp(_modulate, (xb, gain[0:1], s[1], s[0]), dhb)
        return dx + dresb, dgain, dsc, dsh

    grad_x, dgain_mix0, dsc1_0, dsh1_0 = rowwise("x0_bwd", x0_bwd, [x0, dh1_0, dx1],
                                                 [ada_raw[0], b_ada[0:1], norm_mix_gain], [(D, F32)], [vec] * 3)

    small = jnp.concatenate([
        dsh1_0, dsc1_0, dg1_0, dsh2_0, dsc2_0, dg2_0,
        dsh1_1, dsc1_1, dg1_1, dsh2_1, dsc2_1, dg2_1,
        dkv_shift, dkv_scale,
        dgain_mix0, dgain_mix1, dgain_mlp0, dgain_mlp1, dgain_kv,
        dret_gain,
        dq_gain.reshape(FOX_HEADS, FOX_DH).sum(axis=0).reshape(1, FOX_DH),
        dk_gain.reshape(FOX_HEADS, FOX_DH).sum(axis=0).reshape(1, FOX_DH),
        dfb,
    ], axis=1)
    small_all = all_gather("gather_small", [small])[0]
    o_ada = 14 * D
    d_ada = small_all[:, 0, :o_ada]
    d_cat = jnp.concatenate([
        lax.dynamic_slice_in_dim(d_ada[:, 0:6 * D], me * n_ada, n_ada, axis=1),
        lax.dynamic_slice_in_dim(d_ada[:, 6 * D:12 * D], me * n_ada, n_ada, axis=1),
        lax.dynamic_slice_in_dim(d_ada[:, 12 * D:14 * D], me * n_kvada, n_kvada, axis=1)], axis=1)
    gw_ada_cat = mm_tn("ada_dw", c_act, d_cat, 1, F32)[0]

    (r_ret_in, r_mlp_in0, r_mlp_in1, r_mlp_out0, r_mlp_out1, r_ret_out, r_kv, r_fox_in, r_fox_out) = all_to_all(
        "scatter_grads", [gw_ret_in, gw_mlp_in0, gw_mlp_in1, gw_mlp_out0, gw_mlp_out1, gw_ret_out, gw_kv,
                          gw_fox_in, gw_fox_out])

    results = {}

    def update(name, parts, w, m, v):
        shape = w.shape
        R = int(np.prod(shape[:-1])) if len(shape) > 1 else 1
        C = shape[-1]
        g, d, m2, v2 = adamw("adamw_" + name, parts.reshape(parts.shape[0], R, C), w.reshape(R, C), m.reshape(R, C),
                             v.reshape(R, C))
        results[name] = tuple(t.reshape(shape) for t in (g, d, m2, v2))

    def small_parts(lo, width):
        return small_all[:, :, lo:lo + width]

    update("norm_mix_gain", jnp.concatenate([small_parts(o_ada, D), small_parts(o_ada + D, D)], axis=1),
           norm_mix_gain, m_norm_mix_gain, v_norm_mix_gain)
    update("norm_mlp_gain", jnp.concatenate([small_parts(o_ada + 2 * D, D), small_parts(o_ada + 3 * D, D)], axis=1),
           norm_mlp_gain, m_norm_mlp_gain, v_norm_mlp_gain)
    update("w_ada", jnp.stack([gw_ada_cat[:, :n_ada], gw_ada_cat[:, n_ada:2 * n_ada]])[None].reshape(1, 2 * D, n_ada),
           w_ada, m_w_ada, v_w_ada)
    update("b_ada", jnp.concatenate([small_parts(0, 6 * D), small_parts(6 * D, 6 * D)], axis=1), b_ada, m_b_ada, v_b_ada)
    update("w_mlp_in", jnp.concatenate([r_mlp_in0, r_mlp_in1], axis=1), w_mlp_in, m_w_mlp_in, v_w_mlp_in)
    update("w_mlp_out", jnp.concatenate([r_mlp_out0, r_mlp_out1], axis=1), w_mlp_out, m_w_mlp_out, v_w_mlp_out)
    update("ret_w_in", r_ret_in, ret_w_in, m_ret_w_in, v_ret_w_in)
    o_ret = o_ada + 5 * D
    n_rg = ret_norm_gain.shape[2]
    ret_gain_parts = small_parts(o_ret, 2 * D).reshape(N_DEV, RET_HEADS, RET_V)
    ret_gain_parts = lax.dynamic_slice_in_dim(ret_gain_parts, me * n_rg, n_rg, axis=2)
    update("ret_norm_gain", ret_gain_parts, ret_norm_gain, m_ret_norm_gain, v_ret_norm_gain)
    update("ret_w_out", r_ret_out, ret_w_out, m_ret_w_out, v_ret_w_out)
    update("kv_norm_gain", small_parts(o_ada + 4 * D, D), kv_norm_gain, m_kv_norm_gain, v_kv_norm_gain)
    update("kv_w_ada", gw_ada_cat[None, :, 2 * n_ada:], kv_w_ada, m_kv_w_ada, v_kv_w_ada)
    update("kv_b_ada", small_parts(12 * D, 2 * D), kv_b_ada, m_kv_b_ada, v_kv_b_ada)
    update("kv_w", r_kv, kv_w, m_kv_w, v_kv_w)
    o_q = o_ret + 2 * D
    update("forget_bias", small_parts(o_q + 2 * FOX_DH, FOX_HEADS), forget_bias, m_forget_bias, v_forget_bias)
    update("k_norm_gain", small_parts(o_q + FOX_DH, FOX_DH), k_norm_gain, m_k_norm_gain, v_k_norm_gain)
    update("fox_w_in", r_fox_in, fox_w_in, m_fox_w_in, v_fox_w_in)
    update("q_norm_gain", small_parts(o_q, FOX_DH), q_norm_gain, m_q_norm_gain, v_q_norm_gain)
    update("fox_w_out", r_fox_out, fox_w_out, m_fox_w_out, v_fox_w_out)

    order = ["norm_mix_gain", "norm_mlp_gain", "w_ada", "b_ada", "w_mlp_in", "w_mlp_out", "ret_w_in", "ret_norm_gain",
             "ret_w_out", "kv_norm_gain", "kv_w_ada", "kv_b_ada", "kv_w", "forget_bias", "k_norm_gain", "fox_w_in",
             "q_norm_gain", "fox_w_out"]
    out = [loss, grad_x.reshape(x.shape)]
    for slot in range(4):
        out += [results[n][slot] for n in order]
    return tuple(out)
```

```python
import functools
import math

import numpy as np
import jax
import jax.numpy as jnp
from jax import lax
from jax.experimental import pallas as pl
from jax.experimental.pallas import tpu as pltpu

F32 = jnp.float32
BF16 = jnp.bfloat16

N_DEV = 8
D_MODEL = 1024
RET_HEADS = 4
RET_QK = D_MODEL // RET_HEADS
RET_V = 2 * D_MODEL // RET_HEADS
RET_CHUNK = 128
ROPE_BASE = 10000.0
FOX_HEADS = 16
FOX_DH = D_MODEL // FOX_HEADS
EPS = 1e-6
LANES = 128
KV_PAD = 2 * D_MODEL + LANES

ADAM_LR = 0.001
ADAM_B1 = 0.9
ADAM_B2 = 0.999
ADAM_EPS = 1e-08
ADAM_WD = 0.01
ADAM_STEP = 10

VMEM_LIMIT_BYTES = 56 * 1024 * 1024


def _params(sem=None):
    return pltpu.CompilerParams(dimension_semantics=sem, vmem_limit_bytes=VMEM_LIMIT_BYTES)


def _me():
    return lax.axis_index("x"), lax.axis_index("y"), lax.axis_index("c")


def _peer(k):
    x, y, c = _me()
    return (1 - x if k & 4 else x, 1 - y if k & 2 else y, 1 - c if k & 1 else c)


def _peer_index(k):
    px, py, pc = _peer(k)
    return 4 * px + 2 * py + pc


def _exchange(name, xs, scatter):
    n = len(xs)

    def body(*refs):
        x_refs, o_refs = refs[:n], refs[n:2 * n]
        send_sems, recv_sems, local_sems = refs[2 * n:]
        x, y, c = _me()
        me = 4 * x + 2 * y + c
        local = []
        for i in range(n):
            src = x_refs[i].at[me] if scatter else x_refs[i]
            cp = pltpu.make_async_copy(src, o_refs[i].at[me], local_sems.at[i])
            cp.start()
            local.append(cp)
        remote = []
        for k in range(1, N_DEV):
            for i in range(n):
                src = x_refs[i].at[_peer_index(k)] if scatter else x_refs[i]
                cp = pltpu.make_async_remote_copy(
                    src_ref=src, dst_ref=o_refs[i].at[me],
                    send_sem=send_sems.at[(k - 1) * n + i], recv_sem=recv_sems.at[(k - 1) * n + i],
                    device_id=_peer(k), device_id_type=pl.DeviceIdType.MESH)
                cp.start()
                remote.append(cp)
        for cp in remote:
            cp.wait()
        for cp in local:
            cp.wait()

    out_shape = [jax.ShapeDtypeStruct(x.shape if scatter else (N_DEV,) + x.shape, x.dtype) for x in xs]
    any_spec = pl.BlockSpec(memory_space=pl.ANY)
    return pl.pallas_call(
        body, name=name, out_shape=out_shape,
        in_specs=[any_spec] * n, out_specs=[any_spec] * n,
        scratch_shapes=[pltpu.SemaphoreType.DMA(((N_DEV - 1) * n,)),
                        pltpu.SemaphoreType.DMA(((N_DEV - 1) * n,)),
                        pltpu.SemaphoreType.DMA((n,))],
    )(*xs)


def all_gather(name, xs):
    return _exchange(name, xs, scatter=False)


def all_to_all(name, xs):
    return _exchange(name, xs, scatter=True)


_HBM = pl.BlockSpec(memory_space=pltpu.HBM)
_SEM = pl.BlockSpec(memory_space=pltpu.SEMAPHORE)
_EFFECT = pltpu.SideEffectType.DATAFLOW_SIDE_EFFECTING


def _landing(block, shape, dtype):
    me = 4 * lax.axis_index("x") + 2 * lax.axis_index("y") + lax.axis_index("c")
    start = (me,) + (0,) * (len(shape) - 1)
    return lax.dynamic_update_slice(lax.empty(shape, dtype), block[None], start)


def exchange_start(name, xs, scatter):
    n = len(xs)
    if scatter:
        me = 4 * lax.axis_index("x") + 2 * lax.axis_index("y") + lax.axis_index("c")
        lands = [_landing(lax.dynamic_index_in_dim(x, me, 0, keepdims=False), x.shape, x.dtype) for x in xs]
    else:
        lands = [_landing(x, (N_DEV,) + x.shape, x.dtype) for x in xs]

    def body(*refs):
        x_refs, land_refs = refs[:n], refs[n:2 * n]
        send_sems, recv_sems = refs[2 * n:3 * n], refs[3 * n:4 * n]
        token = refs[6 * n]
        x, y, c = _me()
        me = 4 * x + 2 * y + c
        for i in range(n):
            for k in range(1, N_DEV):
                src = x_refs[i].at[_peer_index(k)] if scatter else x_refs[i]
                pltpu.make_async_remote_copy(
                    src_ref=src, dst_ref=land_refs[i].at[me],
                    send_sem=send_sems[i].at[k - 1], recv_sem=recv_sems[i].at[k - 1],
                    device_id=_peer(k), device_id_type=pl.DeviceIdType.MESH).start()
        token[...] = jnp.zeros_like(token)

    sems = [pltpu.SemaphoreType.DMA((N_DEV - 1,))] * (2 * n)
    thru = [pltpu.HBM(a.shape, a.dtype) for a in list(xs) + lands]
    res = pl.pallas_call(
        body, name=name,
        out_shape=sems + thru + [jax.ShapeDtypeStruct((8, LANES), F32)],
        in_specs=[_HBM] * (2 * n),
        out_specs=[_SEM] * (2 * n) + [_HBM] * (2 * n) + [pl.BlockSpec(memory_space=pltpu.VMEM)],
        input_output_aliases={i: 2 * n + i for i in range(2 * n)},
        compiler_params=pltpu.CompilerParams(has_side_effects=_EFFECT),
    )(*[pltpu.with_memory_space_constraint(a, pltpu.HBM) for a in list(xs) + lands])
    handles = [(res[i], res[n + i], res[2 * n + i], res[3 * n + i]) for i in range(n)]
    return handles, res[4 * n]


def exchange_wait(name, handles, after, scatter):
    n = len(handles)

    def body(*refs):
        x_refs, land_refs = refs[:n], refs[n:2 * n]
        send_sems, recv_sems = refs[2 * n:3 * n], refs[3 * n:4 * n]
        x, y, c = _me()
        me = 4 * x + 2 * y + c
        for i in range(n):
            for k in range(1, N_DEV):
                src = x_refs[i].at[_peer_index(k)] if scatter else x_refs[i]
                cp = pltpu.make_async_remote_copy(
                    src_ref=src, dst_ref=land_refs[i].at[me],
                    send_sem=send_sems[i].at[k - 1], recv_sem=recv_sems[i].at[k - 1],
                    device_id=_peer(k), device_id_type=pl.DeviceIdType.MESH)
                cp.wait_send()
                cp.wait_recv()

    xs = [h[2] for h in handles]
    lands = [h[3] for h in handles]
    res = pl.pallas_call(
        body, name=name,
        out_shape=[pltpu.HBM(a.shape, a.dtype) for a in xs + lands],
        in_specs=[_HBM] * (2 * n) + [_SEM] * (2 * n) + [pl.BlockSpec(memory_space=pl.ANY)],
        out_specs=[_HBM] * (2 * n),
        input_output_aliases={i: i for i in range(2 * n)},
        compiler_params=pltpu.CompilerParams(has_side_effects=_EFFECT),
    )(*xs, *lands, *[h[0] for h in handles], *[h[1] for h in handles], after)
    return res[n:]


def _tile(n, cap):
    best = None
    for t in range(LANES, min(n, cap) + 1, LANES):
        if n % t == 0:
            best = t
    if best is None or (best < 256 and n <= 2304):
        return n
    return best


def _row_tile(m, cap):
    if m <= cap:
        return m
    t = cap
    while m % t:
        t //= 2
    return t


def _after_spec(after):
    return [] if after is None else [pl.BlockSpec(memory_space=pl.ANY)]


def _after_arg(after):
    return [] if after is None else [after]


def _mm_body(nk, dims, a_ref, b_ref, *refs):
    o_ref, acc_ref = refs[-2:]
    k = pl.program_id(2)

    @pl.when(k == 0)
    def _():
        acc_ref[...] = jnp.zeros_like(acc_ref)

    acc_ref[...] += lax.dot_general(a_ref[...].astype(BF16), b_ref[...].astype(BF16), (dims, ((), ())),
                                    preferred_element_type=F32)

    @pl.when(k == nk - 1)
    def _():
        o_ref[...] = acc_ref[...].astype(o_ref.dtype)


def mm_nn(name, a, w, out_dtype=F32, after=None):
    M, K = a.shape
    G, _, n = w.shape
    tn = _tile(n, 1024)
    tm = _row_tile(M, 512 if tn > 1024 else 1024)
    tk = _row_tile(K, 1024)
    r = n // tn
    nk = K // tk
    return pl.pallas_call(
        functools.partial(_mm_body, nk, ((1,), (0,))), name=name,
        out_shape=jax.ShapeDtypeStruct((M, G * n), out_dtype),
        grid=(M // tm, G * r, nk),
        in_specs=[pl.BlockSpec((tm, tk), lambda i, j, k: (i, k)),
                  pl.BlockSpec((None, tk, tn), lambda i, j, k: (j // r, k, j % r))] + _after_spec(after),
        out_specs=pl.BlockSpec((tm, tn), lambda i, j, k: (i, j)),
        scratch_shapes=[pltpu.VMEM((tm, tn), F32)],
        compiler_params=_params(("parallel", "parallel", "arbitrary")),
    )(a, w, *_after_arg(after))


def mm_nt(name, dy, w, out_dtype=F32, after=None):
    M, N = dy.shape
    G, K, n = w.shape
    tn = _tile(n, 1024)
    tm = _row_tile(M, 512 if tn > 1024 else 1024)
    tk = _row_tile(K, 1024)
    r = n // tn
    nk = G * r
    return pl.pallas_call(
        functools.partial(_mm_body, nk, ((1,), (1,))), name=name,
        out_shape=jax.ShapeDtypeStruct((M, K), out_dtype),
        grid=(M // tm, K // tk, nk),
        in_specs=[pl.BlockSpec((tm, tn), lambda i, j, k: (i, k)),
                  pl.BlockSpec((None, tk, tn), lambda i, j, k: (k // r, j, k % r))] + _after_spec(after),
        out_specs=pl.BlockSpec((tm, tk), lambda i, j, k: (i, j)),
        scratch_shapes=[pltpu.VMEM((tm, tk), F32)],
        compiler_params=_params(("parallel", "parallel", "arbitrary")),
    )(dy, w, *_after_arg(after))


def mm_tn(name, a, dy, G, out_dtype=F32, after=None):
    M, K = a.shape
    N = dy.shape[1]
    n = N // G
    tn = _tile(n, 1024)
    tk = _row_tile(K, 512 if tn > 1024 else 1024)
    tm = _row_tile(M, 1024)
    r = n // tn
    nk = M // tm
    return pl.pallas_call(
        functools.partial(_mm_body, nk, ((0,), (0,))), name=name,
        out_shape=jax.ShapeDtypeStruct((G, K, n), out_dtype),
        grid=(K // tk, G * r, nk),
        in_specs=[pl.BlockSpec((tm, tk), lambda i, j, k: (k, i)),
                  pl.BlockSpec((tm, tn), lambda i, j, k: (k, j))] + _after_spec(after),
        out_specs=pl.BlockSpec((None, tk, tn), lambda i, j, k: (j // r, i, j % r)),
        scratch_shapes=[pltpu.VMEM((tk, tn), F32)],
        compiler_params=_params(("parallel", "parallel", "arbitrary")),
    )(a, dy, *_after_arg(after))


def rowwise(name, fn, rows, vecs, outs, accs=(), tm=256, after=None):
    rows = [r if isinstance(r, tuple) else (r, r.shape[1], 0) for r in rows]
    n_fn = len(rows) + len(vecs)
    vecs = list(vecs) + _after_arg(after)
    S = rows[0][0].shape[0]
    tm = _row_tile(S, tm)
    n_r, n_v, n_o, n_a = len(rows), len(vecs), len(outs), len(accs)

    def body(*refs):
        ins = [ref[...] for ref in refs[:n_r + n_v]]
        o_refs = refs[n_r + n_v:n_r + n_v + n_o]
        a_refs = refs[n_r + n_v + n_o:]
        res = fn(*ins[:n_fn])
        res = res if isinstance(res, (tuple, list)) else (res,)
        for ref, val in zip(o_refs, res[:n_o]):
            ref[...] = val.astype(ref.dtype)
        if n_a:
            @pl.when(pl.program_id(0) == 0)
            def _():
                for ref in a_refs:
                    ref[...] = jnp.zeros_like(ref)
            for ref, val in zip(a_refs, res[n_o:]):
                ref[...] += val

    in_specs = [pl.BlockSpec((tm, w), functools.partial(lambda cb, i: (i, cb), cb)) for _, w, cb in rows]
    in_specs += [pl.BlockSpec(v.shape, lambda i: (0, 0)) for v in vecs]
    out_specs = [pl.BlockSpec((tm, w), lambda i: (i, 0)) for w, _ in outs]
    out_specs += [pl.BlockSpec(a, lambda i: (0, 0)) for a in accs]
    out_shape = [jax.ShapeDtypeStruct((S, w), dt) for w, dt in outs]
    out_shape += [jax.ShapeDtypeStruct(a, F32) for a in accs]
    res = pl.pallas_call(
        body, name=name, out_shape=out_shape, grid=(S // tm,),
        in_specs=in_specs, out_specs=out_specs,
        compiler_params=_params(("arbitrary",)),
    )(*[r[0] for r in rows], *vecs)
    return res


def _rms(x):
    return x * lax.rsqrt(jnp.mean(x * x, axis=-1, keepdims=True) + EPS)


def _modulate(x, gain, scale, shift):
    return _rms(x) * gain * (1.0 + scale) + shift


def _ada_slices(ada_raw, bias):
    ada = ada_raw + bias
    return [ada[:, i * D_MODEL:(i + 1) * D_MODEL] for i in range(ada.shape[1] // D_MODEL)]


def _norm_wide_heads(y, gain, heads):
    w = y.shape[1] // heads
    return jnp.concatenate([_rms(y[:, h * w:(h + 1) * w]) * gain[:, h * w:(h + 1) * w] for h in range(heads)], axis=1)


def _norm_fox_heads(x, gain):
    outs = []
    for p in range(x.shape[1] // LANES):
        blk = x[:, p * LANES:(p + 1) * LANES]
        low = lax.broadcasted_iota(jnp.int32, blk.shape, 1) < FOX_DH
        sq = blk * blk
        ss_low = jnp.sum(jnp.where(low, sq, 0.0), axis=1, keepdims=True)
        ss_high = jnp.sum(jnp.where(low, 0.0, sq), axis=1, keepdims=True)
        outs.append(blk * lax.rsqrt(jnp.where(low, ss_low, ss_high) * (1.0 / FOX_DH) + EPS))
    return jnp.concatenate(outs, axis=1) * gain


def _silu(x):
    return x * jax.nn.sigmoid(x)


def _log_sigmoid(z):
    return -(jnp.maximum(-z, 0.0) + jnp.log(1.0 + jnp.exp(-jnp.abs(z))))


def _rotate(x, cos, sin, heads, sign):
    w = x.shape[1] // heads
    half = w // 2
    outs = []
    for h in range(heads):
        x1 = x[:, h * w:h * w + half]
        x2 = x[:, h * w + half:(h + 1) * w]
        outs += [x1 * cos - sign * x2 * sin, sign * x1 * sin + x2 * cos]
    return jnp.concatenate(outs, axis=1)


def _vjp(fn, primals, ct):
    return jax.vjp(fn, *primals)[1](ct)


_LOG_GAMMAS = [float(np.log(np.float32(1.0) - np.float32(2.0) ** np.float32(-5.0 - h))) for h in range(RET_HEADS)]


def retention(name, q, k, v, reverse):
    (qa, dk, qo), (ka, _, ko), (va, dv, vo) = q, k, v
    S = qa.shape[0]
    C = RET_CHUNK
    nc = S // C

    def body(q_ref, k_ref, v_ref, o_ref, state):
        h = pl.program_id(0)

        @pl.when(pl.program_id(1) == 0)
        def _():
            state[...] = jnp.zeros_like(state)

        log_g = jnp.float32(_LOG_GAMMAS[RET_HEADS - 1])
        for i in range(RET_HEADS - 2, -1, -1):
            log_g = jnp.where(h == i, jnp.float32(_LOG_GAMMAS[i]), log_g)
        row = lax.broadcasted_iota(jnp.int32, (C, C), 0)
        col = lax.broadcasted_iota(jnp.int32, (C, C), 1)
        rel = (col - row if reverse else row - col).astype(F32)
        decay = jnp.where(rel >= 0, jnp.exp(log_g * jnp.maximum(rel, 0.0)), 0.0)
        j = lax.broadcasted_iota(jnp.int32, (C, 1), 0).astype(F32)
        q_decay = jnp.exp(log_g * (C - j if reverse else j + 1.0))
        k_decay = jnp.exp(log_g * (j if reverse else C - 1.0 - j))
        chunk_decay = jnp.exp(jnp.full((1, 1), log_g * C, F32))

        qc = q_ref[...].astype(BF16)
        kf = k_ref[...].astype(F32)
        vc = v_ref[...].astype(BF16)
        scores = lax.dot_general(qc, kf.astype(BF16), (((1,), (1,)), ((), ())), preferred_element_type=F32) * decay
        intra = jnp.dot(scores.astype(BF16), vc, preferred_element_type=F32)
        cross = jnp.dot(qc, state[...].astype(BF16), preferred_element_type=F32) * q_decay
        o_ref[...] = intra + cross
        upd = lax.dot_general((kf * k_decay).astype(BF16), vc, (((0,), (0,)), ((), ())), preferred_element_type=F32)
        state[...] = state[...] * chunk_decay + upd

    def chunk(i):
        return nc - 1 - i if reverse else i

    return pl.pallas_call(
        body, name=name, out_shape=jax.ShapeDtypeStruct((S, RET_HEADS * dv), F32),
        grid=(RET_HEADS, nc),
        in_specs=[pl.BlockSpec((C, dk), lambda h, i: (chunk(i), qo + h)),
                  pl.BlockSpec((C, dk), lambda h, i: (chunk(i), ko + h)),
                  pl.BlockSpec((C, dv), lambda h, i: (chunk(i), vo + h))],
        out_specs=pl.BlockSpec((C, dv), lambda h, i: (chunk(i), h)),
        scratch_shapes=[pltpu.VMEM((dk, dv), F32)],
        compiler_params=_params(("parallel", "arbitrary")),
    )(qa, ka, va)


FOX_TQ = 256
N_PAIR = FOX_HEADS // 2


def _fox_logits(q2, k2, f_blk, ft_ref, pair, a, i, tq):
    S = k2.shape[0]
    head = 2 * pair + a
    low = lax.broadcasted_iota(jnp.int32, (1, LANES), 1) < FOX_DH
    mask = low if a == 0 else jnp.logical_not(low)
    qa = jnp.where(mask, q2, 0.0).astype(BF16)
    lane = lax.broadcasted_iota(jnp.int32, f_blk.shape, 1)
    fq = jnp.sum(jnp.where(lane == head, f_blk, 0.0), axis=1, keepdims=True)
    fk = ft_ref[pl.ds(head, 1), :]
    logits = lax.dot_general(qa, k2, (((1,), (1,)), ((), ())), preferred_element_type=F32) * (FOX_DH ** -0.5)
    logits = logits + fq - fk
    row = i * tq + lax.broadcasted_iota(jnp.int32, (tq, S), 0)
    col = lax.broadcasted_iota(jnp.int32, (tq, S), 1)
    causal = col <= row
    return jnp.where(causal, logits, -jnp.inf), causal, mask, qa


def fox_forward(name, qn, kn, kvf, f_cum, f_cum_t):
    S = qn.shape[0]
    tq = _row_tile(S, FOX_TQ)
    v_block0 = D_MODEL // LANES

    def body(q_ref, k_ref, v_ref, f_ref, ft_ref, y_ref, lse_ref):
        pair, i = pl.program_id(0), pl.program_id(1)
        q2 = q_ref[...]
        k2 = k_ref[...].astype(BF16)
        v2 = v_ref[...].astype(BF16)
        f_blk = f_ref[...]
        ys, lses = [], []
        for a in range(2):
            logits, _, mask, _ = _fox_logits(q2, k2, f_blk, ft_ref, pair, a, i, tq)
            m = jnp.max(logits, axis=1, keepdims=True)
            e = jnp.exp(logits - m)
            l = jnp.sum(e, axis=1, keepdims=True)
            p = e / l
            ys.append(jnp.dot(p.astype(BF16), v2, preferred_element_type=F32))
            lses.append(m + jnp.log(l))
        low = lax.broadcasted_iota(jnp.int32, (1, LANES), 1) < FOX_DH
        y_ref[...] = jnp.where(low, ys[0], ys[1])
        lse_ref[...] = jnp.where(low, lses[0], lses[1])

    return pl.pallas_call(
        body, name=name,
        out_shape=[jax.ShapeDtypeStruct((S, D_MODEL), F32), jax.ShapeDtypeStruct((S, D_MODEL), F32)],
        grid=(N_PAIR, S // tq),
        in_specs=[pl.BlockSpec((tq, LANES), lambda p, i: (i, p)),
                  pl.BlockSpec((S, LANES), lambda p, i: (0, p)),
                  pl.BlockSpec((S, LANES), lambda p, i: (0, v_block0 + p)),
                  pl.BlockSpec((tq, LANES), lambda p, i: (i, 0)),
                  pl.BlockSpec((LANES, S), lambda p, i: (0, 0))],
        out_specs=[pl.BlockSpec((tq, LANES), lambda p, i: (i, p)),
                   pl.BlockSpec((tq, LANES), lambda p, i: (i, p))],
        compiler_params=_params(("parallel", "arbitrary")),
    )(qn, kn, kvf, f_cum, f_cum_t)


def fox_backward(name, qn, kn, kvf, f_cum, f_cum_t, y, dy, lse):
    S = qn.shape[0]
    tq = _row_tile(S, FOX_TQ)
    v_block0 = D_MODEL // LANES
    scale = FOX_DH ** -0.5

    def body(q_ref, k_ref, v_ref, f_ref, ft_ref, y_ref, dy_ref, lse_ref,
             dq_ref, dk_ref, dv_ref, dfq_ref, dfk_ref):
        pair, i = pl.program_id(0), pl.program_id(1)

        @pl.when(i == 0)
        def _():
            dk_ref[...] = jnp.zeros_like(dk_ref)
            dv_ref[...] = jnp.zeros_like(dv_ref)
            dfk_ref[...] = jnp.zeros_like(dfk_ref)

        q2 = q_ref[...]
        k2 = k_ref[...].astype(BF16)
        v2 = v_ref[...].astype(BF16)
        f_blk = f_ref[...]
        y2, dy2, lse2 = y_ref[...], dy_ref[...], lse_ref[...]
        dqs, dfq = [], jnp.zeros((tq, LANES), F32)
        lane = lax.broadcasted_iota(jnp.int32, (tq, LANES), 1)
        for a in range(2):
            logits, causal, mask, qa = _fox_logits(q2, k2, f_blk, ft_ref, pair, a, i, tq)
            lse_a = jnp.max(jnp.where(mask, lse2, -jnp.inf), axis=1, keepdims=True)
            p = jnp.where(causal, jnp.exp(logits - lse_a), 0.0)
            dy_a = jnp.where(mask, dy2, 0.0)
            delta = jnp.sum(dy_a * y2, axis=1, keepdims=True)
            dy_b = dy_a.astype(BF16)
            dp = lax.dot_general(dy_b, v2, (((1,), (1,)), ((), ())), preferred_element_type=F32)
            ds = p * (dp - delta)
            dfq = dfq + jnp.where(lane == 2 * pair + a, jnp.sum(ds, axis=1, keepdims=True), 0.0)
            dfk_ref[pl.ds(a, 1), :] += -jnp.sum(ds, axis=0, keepdims=True)
            ds_b = ds.astype(BF16)
            dqs.append(jnp.dot(ds_b, k2, preferred_element_type=F32) * scale)
            dk_ref[...] += lax.dot_general(ds_b, qa, (((0,), (0,)), ((), ())), preferred_element_type=F32) * scale
            dv_ref[...] += lax.dot_general(p.astype(BF16), dy_b, (((0,), (0,)), ((), ())), preferred_element_type=F32)
        low = lax.broadcasted_iota(jnp.int32, (1, LANES), 1) < FOX_DH
        dq_ref[...] = jnp.where(low, dqs[0], dqs[1])
        dfq_ref[...] = dfq

    row_blk = pl.BlockSpec((tq, LANES), lambda p, i: (i, p))
    col_blk = pl.BlockSpec((S, LANES), lambda p, i: (0, p))
    return pl.pallas_call(
        body, name=name,
        out_shape=[jax.ShapeDtypeStruct((S, D_MODEL), F32)] * 3
        + [jax.ShapeDtypeStruct((N_PAIR, S, LANES), F32), jax.ShapeDtypeStruct((N_PAIR, 8, S), F32)],
        grid=(N_PAIR, S // tq),
        in_specs=[row_blk, col_blk,
                  pl.BlockSpec((S, LANES), lambda p, i: (0, v_block0 + p)),
                  pl.BlockSpec((tq, LANES), lambda p, i: (i, 0)),
                  pl.BlockSpec((LANES, S), lambda p, i: (0, 0)),
                  row_blk, row_blk, row_blk],
        out_specs=[row_blk, col_blk, col_blk,
                   pl.BlockSpec((None, tq, LANES), lambda p, i: (p, i, 0)),
                   pl.BlockSpec((None, 8, S), lambda p, i: (p, 0, 0))],
        compiler_params=_params(("parallel", "arbitrary")),
    )(qn, kn, kvf, f_cum, f_cum_t, y, dy, lse)


def cumsum_rows(name, x, reverse):
    S = x.shape[0]
    C = LANES
    nc = S // C

    def body(x_ref, o_ref):
        row = lax.broadcasted_iota(jnp.int32, (C, C), 0)
        col = lax.broadcasted_iota(jnp.int32, (C, C), 1)
        tri = jnp.where(col >= row if reverse else col <= row, 1.0, 0.0).astype(F32)
        carry = jnp.zeros((1, LANES), F32)
        for i in (range(nc - 1, -1, -1) if reverse else range(nc)):
            blk = x_ref[i * C:(i + 1) * C, :]
            loc = jnp.dot(tri, blk, preferred_element_type=F32, precision=lax.Precision.HIGHEST)
            o_ref[i * C:(i + 1) * C, :] = loc + carry
            carry = carry + (loc[0:1, :] if reverse else loc[C - 1:C, :])

    return pl.pallas_call(body, name=name, out_shape=jax.ShapeDtypeStruct((S, LANES), F32),
                          compiler_params=_params())(x)


def adamw(name, parts, w, m, v):
    P, R, C = parts.shape
    tr = _row_tile(R, 256)

    def body(p_ref, w_ref, m_ref, v_ref, g_out, d_out, m_out, v_out):
        g = p_ref[0].astype(F32)
        for i in range(1, P):
            g = g + p_ref[i].astype(F32)
        m2 = ADAM_B1 * m_ref[...] + (1.0 - ADAM_B1) * g
        v2 = ADAM_B2 * v_ref[...] + (1.0 - ADAM_B2) * jnp.square(g)
        m_hat = m2 / (1.0 - ADAM_B1 ** ADAM_STEP)
        v_hat = v2 / (1.0 - ADAM_B2 ** ADAM_STEP)
        g_out[...] = g
        d_out[...] = -ADAM_LR * (m_hat / (jnp.sqrt(v_hat) + ADAM_EPS) + ADAM_WD * w_ref[...])
        m_out[...] = m2
        v_out[...] = v2

    blk = pl.BlockSpec((tr, C), lambda i: (i, 0))
    return pl.pallas_call(
        body, name=name, out_shape=[jax.ShapeDtypeStruct((R, C), F32)] * 4, grid=(R // tr,),
        in_specs=[pl.BlockSpec((P, tr, C), lambda i: (0, i, 0)), blk, blk, blk],
        out_specs=[blk] * 4, compiler_params=_params(("parallel",)),
    )(parts, w, m, v)


def kernel(x, c, positions, norm_mix_gain, norm_mlp_gain, w_ada, b_ada, w_mlp_in, w_mlp_out, ret_w_in, ret_norm_gain, ret_w_out, kv_norm_gain, kv_w_ada, kv_b_ada, kv_w, forget_bias, k_norm_gain, fox_w_in, q_norm_gain, fox_w_out, loss_target, m_norm_mix_gain, m_norm_mlp_gain, m_w_ada, m_b_ada, m_w_mlp_in, m_w_mlp_out, m_ret_w_in, m_ret_norm_gain, m_ret_w_out, m_kv_norm_gain, m_kv_w_ada, m_kv_b_ada, m_kv_w, m_forget_bias, m_k_norm_gain, m_fox_w_in, m_q_norm_gain, m_fox_w_out, v_norm_mix_gain, v_norm_mlp_gain, v_w_ada, v_b_ada, v_w_mlp_in, v_w_mlp_out, v_ret_w_in, v_ret_norm_gain, v_ret_w_out, v_kv_norm_gain, v_kv_w_ada, v_kv_b_ada, v_kv_w, v_forget_bias, v_k_norm_gain, v_fox_w_in, v_q_norm_gain, v_fox_w_out):
    D = D_MODEL
    S = x.shape[1]
    x0 = x.reshape(S, D)
    target = loss_target.reshape(S, D)
    me = 4 * lax.axis_index("x") + 2 * lax.axis_index("y") + lax.axis_index("c")
    n_ada = w_ada.shape[2]
    n_kvada = kv_w_ada.shape[1]
    n_kv = kv_w.shape[1]

    w_names = ["ret_in", "ret_out", "mlp_in0", "mlp_out0", "kv", "fox_in", "fox_out", "mlp_in1", "mlp_out1"]
    w_handles, token = exchange_start("gather_weights_start", [
        ret_w_in[0].astype(BF16), ret_w_out[0].astype(BF16), w_mlp_in[0].astype(BF16), w_mlp_out[0].astype(BF16),
        kv_w.astype(BF16), fox_w_in[0].astype(BF16), fox_w_out[0].astype(BF16), w_mlp_in[1].astype(BF16),
        w_mlp_out[1].astype(BF16)], scatter=False)
    w_handles = dict(zip(w_names, w_handles))

    def weight(name, after):
        return exchange_wait("gather_wait_" + name, [w_handles[name]], after, scatter=False)[0]

    c_all = all_gather("gather_c", [c.reshape(1, D) + token[0:1, 0:1]])[0]
    c_act = rowwise("silu_c", _silu, [c_all.reshape(N_DEV, D)], [], [(D, F32)])[0]
    w_ada_cat = jnp.concatenate([w_ada[0], w_ada[1], kv_w_ada], axis=1).astype(BF16)[None]
    n_cat = 2 * n_ada + n_kvada
    ada_part = mm_nn("ada_proj", c_act, w_ada_cat)
    ada_mine = all_to_all("ada_rows", [ada_part.reshape(N_DEV, 1, n_cat)])[0][:, 0]
    ada_raw = [ada_mine[:, l * n_ada:(l + 1) * n_ada].reshape(1, 6 * D) for l in range(2)]
    kvada_raw = ada_mine[:, 2 * n_ada:].reshape(1, 2 * D)
    kv_bias = kv_b_ada.reshape(1, 2 * D)
    kv_gain = kv_norm_gain.reshape(1, D)
    fb = jnp.pad(forget_bias.reshape(1, FOX_HEADS), ((0, 0), (0, LANES - FOX_HEADS)))
    k_gain = jnp.tile(k_norm_gain.reshape(1, FOX_DH), (1, FOX_HEADS))
    q_gain = jnp.tile(q_norm_gain.reshape(1, FOX_DH), (1, FOX_HEADS))
    ret_gain = all_gather("gather_ret_gain", [ret_norm_gain.reshape(RET_HEADS, -1)])[0]
    ret_gain = jnp.transpose(ret_gain, (1, 0, 2)).reshape(1, RET_HEADS * RET_V)

    pos = positions.reshape(S, 1).astype(F32)
    half = RET_QK // 2
    inv_freq = jnp.asarray((ROPE_BASE ** (-np.arange(half, dtype=np.float32) / half)).reshape(1, half), F32)

    def angles(p, f):
        ang = p * f
        return jnp.cos(ang), jnp.sin(ang)

    cos, sin = rowwise("rope_table", angles, [pos], [inv_freq], [(half, F32), (half, F32)])

    def mod_mix(layer):
        def fn(xb, ada, bias, gain):
            sh, sc = _ada_slices(ada, bias)[:2]
            return _modulate(xb, gain[layer:layer + 1], sc, sh)
        return fn

    def mod_mlp(layer):
        def fn(xb, ada, bias, gain):
            sh, sc = _ada_slices(ada, bias)[3:5]
            return _modulate(xb, gain[layer:layer + 1], sc, sh)
        return fn

    h1_0 = rowwise("mod_mix0", mod_mix(0), [x0], [ada_raw[0], b_ada[0:1], norm_mix_gain], [(D, BF16)])[0]
    W_ret_in = weight("ret_in", h1_0)
    proj = mm_nn("ret_proj", h1_0, W_ret_in)

    def rope_fwd(qb, kb, cs, sn):
        return _rotate(qb, cs, sn, RET_HEADS, 1.0), _rotate(kb, cs, sn, RET_HEADS, 1.0) * (RET_QK ** -0.5)

    q_rot, k_rot = rowwise("rope", rope_fwd, [(proj, D, 0), (proj, D, 1), cos, sin], [], [(D, F32), (D, F32)])
    v_ret = (proj, RET_V, (2 * D) // RET_V)
    y_ret = retention("ret_fwd", (q_rot, RET_QK, 0), (k_rot, RET_QK, 0), v_ret, reverse=False)

    def ret_gate(yb, gb, gain):
        return _silu(gb) * _norm_wide_heads(yb, gain, RET_HEADS)

    mixin0 = rowwise("ret_gate", ret_gate, [y_ret, (proj, 2 * D, 2)], [ret_gain], [(2 * D, BF16)])[0]
    W_ret_out = weight("ret_out", mixin0).reshape(1, 2 * D, D)
    mix0 = mm_nn("ret_out", mixin0, W_ret_out)

    def residual_mod(layer, slot):
        def fn(xb, bb, ada, bias, gain):
            s = _ada_slices(ada, bias)
            xn = xb + s[2] * bb
            return xn, _modulate(xn, gain[layer:layer + 1], s[4], s[3])
        return fn

    x1, h2_0 = rowwise("res_mix0", residual_mod(0, 0), [x0, mix0], [ada_raw[0], b_ada[0:1], norm_mlp_gain],
                       [(D, F32), (D, BF16)])

    W_mlp_in, W_mlp_out = {}, {}

    def mlp_forward(tag, h2, layer):
        W_mlp_in[layer] = weight("mlp_in" + tag, h2)
        u = mm_nn("mlp_in" + tag, h2, W_mlp_in[layer])
        act = rowwise("relu2" + tag, lambda ub: jnp.square(jnp.maximum(ub, 0.0)), [u], [], [(4 * D, BF16)])[0]
        W_mlp_out[layer] = weight("mlp_out" + tag, act).reshape(1, 4 * D, D)
        return u, act, mm_nn("mlp_out" + tag, act, W_mlp_out[layer])

    u0, act0, mlp0 = mlp_forward("0", h2_0, 0)

    def res_mlp0(xb, bb, ada0, bias0, ada1, bias1, kva, kvb, gain_mix, gain_kv):
        xn = xb + _ada_slices(ada0, bias0)[5] * bb
        s1 = _ada_slices(ada1, bias1)
        kv_shift, kv_scale = _ada_slices(kva, kvb)
        return xn, _modulate(xn, gain_kv, kv_scale, kv_shift), _modulate(xn, gain_mix[1:2], s1[1], s1[0])

    x2, h_kv, h1_1 = rowwise("res_mlp0", res_mlp0, [x1, mlp0],
                             [ada_raw[0], b_ada[0:1], ada_raw[1], b_ada[1:2], kvada_raw, kv_bias, norm_mix_gain, kv_gain],
                             [(D, F32), (D, BF16), (D, BF16)])

    kv_full = jnp.transpose(weight("kv", h_kv), (1, 0, 2)).reshape(D, N_DEV * n_kv)
    W_kv = jnp.pad(kv_full, ((0, 0), (0, KV_PAD - N_DEV * n_kv)))[None]
    kvf = mm_nn("kv_proj", h_kv, W_kv)

    def kv_post(kb, fblk, kg, bias):
        head = lax.broadcasted_iota(jnp.int32, fblk.shape, 1) < FOX_HEADS
        return _norm_fox_heads(kb, kg), jnp.where(head, _log_sigmoid(fblk + bias), 0.0)

    kn, log_f = rowwise("kv_post", kv_post, [(kvf, D, 0), (kvf, LANES, 2 * D // LANES)], [k_gain, fb],
                        [(D, F32), (LANES, F32)])
    f_cum = cumsum_rows("f_cumsum", log_f, reverse=False)
    f_cum_t = f_cum.T

    W_fox_in = weight("fox_in", kvf)
    qo = mm_nn("fox_proj", h1_1, W_fox_in)
    qn = rowwise("q_norm", _norm_fox_heads, [(qo, D, 0)], [q_gain], [(D, F32)])[0]
    y_att, lse = fox_forward("fox_fwd", qn, kn, kvf, f_cum, f_cum_t)
    mixin1 = rowwise("fox_gate", lambda ob, yb: jax.nn.sigmoid(ob) * yb, [(qo, D, 1), y_att], [], [(D, BF16)])[0]
    W_fox_out = weight("fox_out", mixin1).reshape(1, D, D)
    mix1 = mm_nn("fox_out", mixin1, W_fox_out)
    x3, h2_1 = rowwise("res_mix1", residual_mod(1, 0), [x2, mix1], [ada_raw[1], b_ada[1:2], norm_mlp_gain],
                       [(D, F32), (D, BF16)])
    u1, act1, mlp1 = mlp_forward("1", h2_1, 1)

    def loss_head(xb, bb, tb, ada, bias):
        g2 = _ada_slices(ada, bias)[5]
        err = xb + g2 * bb - tb
        dx = err * (1.0 / D)
        loss = 0.5 * jnp.sum(jnp.sum(err * err, axis=1, keepdims=True) * (1.0 / D), axis=0, keepdims=True)
        return dx, (dx * g2), jnp.broadcast_to(loss, (1, LANES)), jnp.sum(dx * bb, axis=0, keepdims=True)

    dx4, dmlp1, loss_acc, dg2_1 = rowwise("loss_head", loss_head, [x3, mlp1, target], [ada_raw[1], b_ada[1:2]],
                                          [(D, F32), (D, BF16)], [(1, LANES), (1, D)])
    loss = lax.psum(loss_acc[0, 0], ("x", "y", "c"))

    def mlp_backward(tag, dmlp, act, u, h2, layer):
        d_act = mm_nt("mlp_out_dx" + tag, dmlp, W_mlp_out[layer])
        gw_out = mm_tn("mlp_out_dw" + tag, act, dmlp, 1, BF16).reshape(N_DEV, -1, D)
        du = rowwise("relu2_bwd" + tag, lambda db, ub: db * (2.0 * jnp.maximum(ub, 0.0)), [d_act, u], [], [(4 * D, BF16)])[0]
        dh2 = mm_nt("mlp_in_dx" + tag, du, W_mlp_in[layer])
        gw_in = mm_tn("mlp_in_dw" + tag, h2, du, N_DEV, BF16)
        return dh2, gw_in, gw_out

    def mod_backward(layer, slots, gate_slot):
        def fn(xb, dhb, dresb, branchb, ada, bias, gain):
            s = _ada_slices(ada, bias)
            g = gain[layer:layer + 1]
            dx, dgain, dsc, dsh = _vjp(_modulate, (xb, g, s[slots[1]], s[slots[0]]), dhb)
            dx = dx + dresb
            d_branch = dx * s[gate_slot]
            return dx, d_branch, dgain, dsc, dsh, jnp.sum(dx * branchb, axis=0, keepdims=True)
        return fn

    vec = (1, D)
    dh2_1, gw_mlp_in1, gw_mlp_out1 = mlp_backward("1", dmlp1, act1, u1, h2_1, 1)
    dx3, dmix1, dgain_mlp1, dsc2_1, dsh2_1, dg1_1 = rowwise(
        "mod_mlp1_bwd", mod_backward(1, (3, 4), 2), [x3, dh2_1, dx4, mix1], [ada_raw[1], b_ada[1:2], norm_mlp_gain],
        [(D, F32), (D, BF16)], [vec] * 4)
    dmixin1 = mm_nt("fox_out_dx", dmix1, W_fox_out)
    gw_fox_out = mm_tn("fox_out_dw", mixin1, dmix1, 1, BF16).reshape(N_DEV, -1, D)

    def fox_gate_bwd(db, ob, yb):
        sg = jax.nn.sigmoid(ob)
        return db * sg, db * yb * sg * (1.0 - sg)

    dy_att, d_og = rowwise("fox_gate_bwd", fox_gate_bwd, [dmixin1, (qo, D, 1), y_att], [], [(D, F32), (D, F32)])
    dqn, dkn, dv_att, dfq, dfk = fox_backward("fox_bwd", qn, kn, kvf, f_cum, f_cum_t, y_att, dy_att, lse)

    def q_norm_bwd(qb, db, ogb, gain):
        dq, dgain = _vjp(_norm_fox_heads, (qb, gain), db)
        return jnp.concatenate([dq, ogb], axis=1), dgain

    dqo, dq_gain = rowwise("q_norm_bwd", q_norm_bwd, [(qo, D, 0), dqn, d_og], [q_gain], [(2 * D, BF16)], [vec])
    dh1_1 = mm_nt("fox_proj_dx", dqo, W_fox_in)
    gw_fox_in = mm_tn("fox_proj_dw", h1_1, dqo, N_DEV, BF16)

    dfk_rows = jnp.pad(dfk[:, :2, :].reshape(FOX_HEADS, S).T, ((0, 0), (0, LANES - FOX_HEADS)))

    def df_total(*blks):
        tot = blks[0]
        for b in blks[1:]:
            tot = tot + b
        return tot

    d_fcum = rowwise("df_sum", df_total, [dfk_rows] + [dfq[p] for p in range(N_PAIR)], [], [(LANES, F32)])[0]
    d_logf = cumsum_rows("df_cumsum", d_fcum, reverse=True)

    def kv_post_bwd(kb, fblk, dkb, dvb, dlf, kg, bias):
        dk, dgain = _vjp(_norm_fox_heads, (kb, kg), dkb)
        df = dlf * (1.0 / (1.0 + jnp.exp(fblk + bias)))
        return jnp.concatenate([dk, dvb, df], axis=1), dgain, jnp.sum(df, axis=0, keepdims=True)

    dkvf, dk_gain, dfb = rowwise("kv_post_bwd", kv_post_bwd,
                                 [(kvf, D, 0), (kvf, LANES, 2 * D // LANES), dkn, dv_att, d_logf], [k_gain, fb],
                                 [(KV_PAD, BF16)], [vec, (1, LANES)])
    dh_kv = mm_nt("kv_proj_dx", dkvf, W_kv)
    gw_kv = mm_tn("kv_proj_dw", h_kv, dkvf, 1, BF16)[0, :, :N_DEV * n_kv]
    gw_kv = jnp.transpose(gw_kv.reshape(D, N_DEV, n_kv), (1, 0, 2))
    scat_a, token_a = exchange_start("scatter_start_a", [gw_mlp_in1, gw_mlp_out1, gw_fox_out, gw_fox_in, gw_kv], scatter=True)

    def x2_bwd(xb, dh1b, dhkb, dresb, branchb, ada0, bias0, ada1, bias1, kva, kvb, gain_mix, gain_kv):
        s1 = _ada_slices(ada1, bias1)
        kv_shift, kv_scale = _ada_slices(kva, kvb)
        dxa, dgain_mix, dsc1, dsh1 = _vjp(_modulate, (xb, gain_mix[1:2], s1[1], s1[0]), dh1b)
        dxb, dgain_kv, dkv_scale, dkv_shift = _vjp(_modulate, (xb, gain_kv, kv_scale, kv_shift), dhkb)
        dx = dresb + dxa + dxb
        g2 = _ada_slices(ada0, bias0)[5]
        return (dx, dx * g2, dgain_mix, dsc1, dsh1, dgain_kv, dkv_scale, dkv_shift,
                jnp.sum(dx * branchb, axis=0, keepdims=True))

    (dx2, dmlp0, dgain_mix1, dsc1_1, dsh1_1, dgain_kv, dkv_scale, dkv_shift, dg2_0) = rowwise(
        "x2_bwd", x2_bwd, [x2, dh1_1, dh_kv, dx3, mlp0],
        [ada_raw[0], b_ada[0:1], ada_raw[1], b_ada[1:2], kvada_raw, kv_bias, norm_mix_gain, kv_gain],
        [(D, F32), (D, BF16)], [vec] * 7, after=token_a)

    dh2_0, gw_mlp_in0, gw_mlp_out0 = mlp_backward("0", dmlp0, act0, u0, h2_0, 0)
    dx1, dmix0, dgain_mlp0, dsc2_0, dsh2_0, dg1_0 = rowwise(
        "mod_mlp0_bwd", mod_backward(0, (3, 4), 2), [x1, dh2_0, dx2, mix0], [ada_raw[0], b_ada[0:1], norm_mlp_gain],
        [(D, F32), (D, BF16)], [vec] * 4)
    dmixin0 = mm_nt("ret_out_dx", dmix0, W_ret_out)
    gw_ret_out = mm_tn("ret_out_dw", mixin0, dmix0, 1, BF16).reshape(N_DEV, -1, D)
    scat_b, token_b = exchange_start("scatter_start_b", [gw_mlp_in0, gw_mlp_out0, gw_ret_out], scatter=True)

    def ret_gate_bwd(db, yb, gb, gain):
        dy, dg, dgain = _vjp(lambda y_, g_, gn_: ret_gate(y_, g_, gn_), (yb, gb, gain), db)
        return dy, dg, dgain

    dy_ret, dgate, dret_gain = rowwise("ret_gate_bwd", ret_gate_bwd, [dmixin0, y_ret, (proj, 2 * D, 2)], [ret_gain],
                                       [(2 * D, F32), (2 * D, BF16)], [(1, 2 * D)], after=token_b)
    dy_h = (dy_ret, RET_V, 0)
    dq_rot = retention("ret_dq", dy_h, v_ret, (k_rot, RET_QK, 0), reverse=False)
    dk_rot = retention("ret_dk", v_ret, dy_h, (q_rot, RET_QK, 0), reverse=True)
    dv_ret = retention("ret_dv", (k_rot, RET_QK, 0), (q_rot, RET_QK, 0), dy_h, reverse=True)

    def rope_bwd(dqb, dkb, dvb, dgb, cs, sn):
        dq = _rotate(dqb, cs, sn, RET_HEADS, -1.0)
        dk = _rotate(dkb, cs, sn, RET_HEADS, -1.0) * (RET_QK ** -0.5)
        return jnp.concatenate([dq, dk, dvb, dgb.astype(F32)], axis=1)

    dproj = rowwise("rope_bwd", rope_bwd, [dq_rot, dk_rot, dv_ret, dgate, cos, sin], [], [(6 * D, BF16)])[0]
    gw_ret_in = mm_tn("ret_proj_dw", h1_0, dproj, N_DEV, BF16)
    scat_c, token_c = exchange_start("scatter_start_c", [gw_ret_in], scatter=True)
    dh1_0 = mm_nt("ret_proj_dx", dproj, W_ret_in, after=token_c)

    def x0_bwd(xb, dhb, dresb, ada, bias, gain):
        s = _ada_slices(ada, bias)
        dx, dgain, dsc, dsh = _vjp(_modulate, (xb, gain[0:1], s[1], s[0]), dhb)
        return dx + dresb, dgain, dsc, dsh

    grad_x, dgain_mix0, dsc1_0, dsh1_0 = rowwise("x0_bwd", x0_bwd, [x0, dh1_0, dx1],
                                                 [ada_raw[0], b_ada[0:1], norm_mix_gain], [(D, F32)], [vec] * 3)

    small = jnp.concatenate([
        dsh1_0, dsc1_0, dg1_0, dsh2_0, dsc2_0, dg2_0,
        dsh1_1, dsc1_1, dg1_1, dsh2_1, dsc2_1, dg2_1,
        dkv_shift, dkv_scale,
        dgain_mix0, dgain_mix1, dgain_mlp0, dgain_mlp1, dgain_kv,
        dret_gain,
        dq_gain.reshape(FOX_HEADS, FOX_DH).sum(axis=0).reshape(1, FOX_DH),
        dk_gain.reshape(FOX_HEADS, FOX_DH).sum(axis=0).reshape(1, FOX_DH),
        dfb,
    ], axis=1)
    small_all = all_gather("gather_small", [small])[0]
    o_ada = 14 * D
    d_ada = small_all[:, 0, :o_ada]
    d_cat = jnp.concatenate([
        lax.dynamic_slice_in_dim(d_ada[:, 0:6 * D], me * n_ada, n_ada, axis=1),
        lax.dynamic_slice_in_dim(d_ada[:, 6 * D:12 * D], me * n_ada, n_ada, axis=1),
        lax.dynamic_slice_in_dim(d_ada[:, 12 * D:14 * D], me * n_kvada, n_kvada, axis=1)], axis=1)
    gw_ada_cat = mm_tn("ada_dw", c_act, d_cat, 1, F32)[0]

    results = {}

    def update(name, parts, w, m, v):
        shape = w.shape
        R = int(np.prod(shape[:-1])) if len(shape) > 1 else 1
        C = shape[-1]
        g, d, m2, v2 = adamw("adamw_" + name, parts.reshape(parts.shape[0], R, C), w.reshape(R, C), m.reshape(R, C),
                             v.reshape(R, C))
        results[name] = tuple(t.reshape(shape) for t in (g, d, m2, v2))

    def small_parts(lo, width):
        return small_all[:, :, lo:lo + width]

    update("norm_mix_gain", jnp.concatenate([small_parts(o_ada, D), small_parts(o_ada + D, D)], axis=1),
           norm_mix_gain, m_norm_mix_gain, v_norm_mix_gain)
    update("norm_mlp_gain", jnp.concatenate([small_parts(o_ada + 2 * D, D), small_parts(o_ada + 3 * D, D)], axis=1),
           norm_mlp_gain, m_norm_mlp_gain, v_norm_mlp_gain)
    update("w_ada", jnp.stack([gw_ada_cat[:, :n_ada], gw_ada_cat[:, n_ada:2 * n_ada]])[None].reshape(1, 2 * D, n_ada),
           w_ada, m_w_ada, v_w_ada)
    update("b_ada", jnp.concatenate([small_parts(0, 6 * D), small_parts(6 * D, 6 * D)], axis=1), b_ada, m_b_ada, v_b_ada)
    o_ret = o_ada + 5 * D
    n_rg = ret_norm_gain.shape[2]
    ret_gain_parts = small_parts(o_ret, 2 * D).reshape(N_DEV, RET_HEADS, RET_V)
    ret_gain_parts = lax.dynamic_slice_in_dim(ret_gain_parts, me * n_rg, n_rg, axis=2)
    update("ret_norm_gain", ret_gain_parts, ret_norm_gain, m_ret_norm_gain, v_ret_norm_gain)
    update("kv_norm_gain", small_parts(o_ada + 4 * D, D), kv_norm_gain, m_kv_norm_gain, v_kv_norm_gain)
    update("kv_w_ada", gw_ada_cat[None, :, 2 * n_ada:], kv_w_ada, m_kv_w_ada, v_kv_w_ada)
    update("kv_b_ada", small_parts(12 * D, 2 * D), kv_b_ada, m_kv_b_ada, v_kv_b_ada)
    o_q = o_ret + 2 * D
    update("forget_bias", small_parts(o_q + 2 * FOX_DH, FOX_HEADS), forget_bias, m_forget_bias, v_forget_bias)
    update("k_norm_gain", small_parts(o_q + FOX_DH, FOX_DH), k_norm_gain, m_k_norm_gain, v_k_norm_gain)
    update("q_norm_gain", small_parts(o_q, FOX_DH), q_norm_gain, m_q_norm_gain, v_q_norm_gain)

    r_mlp_in1, r_mlp_out1, r_fox_out, r_fox_in, r_kv = exchange_wait(
        "scatter_wait_a", scat_a, results["w_ada"][1], scatter=True)
    r_mlp_in0, r_mlp_out0, r_ret_out = exchange_wait("scatter_wait_b", scat_b, r_kv, scatter=True)
    update("kv_w", r_kv, kv_w, m_kv_w, v_kv_w)
    update("fox_w_in", r_fox_in, fox_w_in, m_fox_w_in, v_fox_w_in)
    update("fox_w_out", r_fox_out, fox_w_out, m_fox_w_out, v_fox_w_out)
    update("ret_w_out", r_ret_out, ret_w_out, m_ret_w_out, v_ret_w_out)
    update("w_mlp_in", jnp.concatenate([r_mlp_in0, r_mlp_in1], axis=1), w_mlp_in, m_w_mlp_in, v_w_mlp_in)
    update("w_mlp_out", jnp.concatenate([r_mlp_out0, r_mlp_out1], axis=1), w_mlp_out, m_w_mlp_out, v_w_mlp_out)
    r_ret_in = exchange_wait("scatter_wait_c", scat_c, results["w_mlp_out"][1], scatter=True)[0]
    update("ret_w_in", r_ret_in, ret_w_in, m_ret_w_in, v_ret_w_in)

    order = ["norm_mix_gain", "norm_mlp_gain", "w_ada", "b_ada", "w_mlp_in", "w_mlp_out", "ret_w_in", "ret_norm_gain",
             "ret_w_out", "kv_norm_gain", "kv_w_ada", "kv_b_ada", "kv_w", "forget_bias", "k_norm_gain", "fox_w_in",
             "q_norm_gain", "fox_w_out"]
    out = [loss, grad_x.reshape(x.shape)]
    for slot in range(4):
        out += [results[n][slot] for n in order]
    return tuple(out)
```

```python
import functools
import math

import numpy as np
import jax
import jax.numpy as jnp
from jax import lax
from jax.experimental import pallas as pl
from jax.experimental.pallas import tpu as pltpu

F32 = jnp.float32
BF16 = jnp.bfloat16

N_DEV = 8
D_MODEL = 1024
RET_HEADS = 4
RET_QK = D_MODEL // RET_HEADS
RET_V = 2 * D_MODEL // RET_HEADS
RET_CHUNK = 128
ROPE_BASE = 10000.0
FOX_HEADS = 16
FOX_DH = D_MODEL // FOX_HEADS
EPS = 1e-6
LANES = 128
KV_PAD = 2 * D_MODEL + LANES

ADAM_LR = 0.001
ADAM_B1 = 0.9
ADAM_B2 = 0.999
ADAM_EPS = 1e-08
ADAM_WD = 0.01
ADAM_STEP = 10

VMEM_LIMIT_BYTES = 56 * 1024 * 1024


def _params(sem=None):
    return pltpu.CompilerParams(dimension_semantics=sem, vmem_limit_bytes=VMEM_LIMIT_BYTES)


def _me():
    return lax.axis_index("x"), lax.axis_index("y"), lax.axis_index("c")


def _peer(k):
    x, y, c = _me()
    return (1 - x if k & 4 else x, 1 - y if k & 2 else y, 1 - c if k & 1 else c)


def _peer_index(k):
    px, py, pc = _peer(k)
    return 4 * px + 2 * py + pc


def _exchange(name, xs, scatter):
    n = len(xs)

    def body(*refs):
        x_refs, o_refs = refs[:n], refs[n:2 * n]
        send_sems, recv_sems, local_sems = refs[2 * n:]
        x, y, c = _me()
        me = 4 * x + 2 * y + c
        local = []
        for i in range(n):
            src = x_refs[i].at[me] if scatter else x_refs[i]
            cp = pltpu.make_async_copy(src, o_refs[i].at[me], local_sems.at[i])
            cp.start()
            local.append(cp)
        remote = []
        for k in range(1, N_DEV):
            for i in range(n):
                src = x_refs[i].at[_peer_index(k)] if scatter else x_refs[i]
                cp = pltpu.make_async_remote_copy(
                    src_ref=src, dst_ref=o_refs[i].at[me],
                    send_sem=send_sems.at[(k - 1) * n + i], recv_sem=recv_sems.at[(k - 1) * n + i],
                    device_id=_peer(k), device_id_type=pl.DeviceIdType.MESH)
                cp.start()
                remote.append(cp)
        for cp in remote:
            cp.wait()
        for cp in local:
            cp.wait()

    out_shape = [jax.ShapeDtypeStruct(x.shape if scatter else (N_DEV,) + x.shape, x.dtype) for x in xs]
    any_spec = pl.BlockSpec(memory_space=pl.ANY)
    return pl.pallas_call(
        body, name=name, out_shape=out_shape,
        in_specs=[any_spec] * n, out_specs=[any_spec] * n,
        scratch_shapes=[pltpu.SemaphoreType.DMA(((N_DEV - 1) * n,)),
                        pltpu.SemaphoreType.DMA(((N_DEV - 1) * n,)),
                        pltpu.SemaphoreType.DMA((n,))],
    )(*xs)


def all_gather(name, xs):
    return _exchange(name, xs, scatter=False)


def all_to_all(name, xs):
    return _exchange(name, xs, scatter=True)


_HBM = pl.BlockSpec(memory_space=pltpu.HBM)
_SEM = pl.BlockSpec(memory_space=pltpu.SEMAPHORE)
_EFFECT = pltpu.SideEffectType.DATAFLOW_SIDE_EFFECTING


def _landing(block, shape, dtype):
    me = 4 * lax.axis_index("x") + 2 * lax.axis_index("y") + lax.axis_index("c")
    start = (me,) + (0,) * (len(shape) - 1)
    return lax.dynamic_update_slice(lax.empty(shape, dtype), block[None], start)


def exchange_start(name, xs, scatter, after=()):
    n, m = len(xs), len(after)
    if scatter:
        me = 4 * lax.axis_index("x") + 2 * lax.axis_index("y") + lax.axis_index("c")
        lands = [_landing(lax.dynamic_index_in_dim(x, me, 0, keepdims=False), x.shape, x.dtype) for x in xs]
    else:
        lands = [_landing(x, (N_DEV,) + x.shape, x.dtype) for x in xs]

    def body(*refs):
        x_refs, land_refs = refs[:n], refs[n:2 * n]
        send_sems, recv_sems = refs[2 * n + m:3 * n + m], refs[3 * n + m:4 * n + m]
        token = refs[6 * n + m]
        x, y, c = _me()
        me = 4 * x + 2 * y + c
        for i in range(n):
            for k in range(1, N_DEV):
                src = x_refs[i].at[_peer_index(k)] if scatter else x_refs[i]
                pltpu.make_async_remote_copy(
                    src_ref=src, dst_ref=land_refs[i].at[me],
                    send_sem=send_sems[i].at[k - 1], recv_sem=recv_sems[i].at[k - 1],
                    device_id=_peer(k), device_id_type=pl.DeviceIdType.MESH).start()
        token[...] = jnp.zeros_like(token)

    sems = [pltpu.SemaphoreType.DMA((N_DEV - 1,))] * (2 * n)
    thru = [pltpu.HBM(a.shape, a.dtype) for a in list(xs) + lands]
    res = pl.pallas_call(
        body, name=name,
        out_shape=sems + thru + [jax.ShapeDtypeStruct((8, LANES), F32)],
        in_specs=[_HBM] * (2 * n) + [pl.BlockSpec(memory_space=pl.ANY)] * m,
        out_specs=[_SEM] * (2 * n) + [_HBM] * (2 * n) + [pl.BlockSpec(memory_space=pltpu.VMEM)],
        input_output_aliases={i: 2 * n + i for i in range(2 * n)},
        compiler_params=pltpu.CompilerParams(has_side_effects=_EFFECT),
    )(*[pltpu.with_memory_space_constraint(a, pltpu.HBM) for a in list(xs) + lands], *after)
    handles = [(res[i], res[n + i], res[2 * n + i], res[3 * n + i]) for i in range(n)]
    return handles, res[4 * n]


def exchange_wait(name, handles, after, scatter):
    n = len(handles)

    def body(*refs):
        x_refs, land_refs = refs[:n], refs[n:2 * n]
        send_sems, recv_sems = refs[2 * n:3 * n], refs[3 * n:4 * n]
        x, y, c = _me()
        me = 4 * x + 2 * y + c
        for i in range(n):
            for k in range(1, N_DEV):
                src = x_refs[i].at[_peer_index(k)] if scatter else x_refs[i]
                cp = pltpu.make_async_remote_copy(
                    src_ref=src, dst_ref=land_refs[i].at[me],
                    send_sem=send_sems[i].at[k - 1], recv_sem=recv_sems[i].at[k - 1],
                    device_id=_peer(k), device_id_type=pl.DeviceIdType.MESH)
                cp.wait_send()
                cp.wait_recv()

    xs = [h[2] for h in handles]
    lands = [h[3] for h in handles]
    res = pl.pallas_call(
        body, name=name,
        out_shape=[pltpu.HBM(a.shape, a.dtype) for a in xs + lands],
        in_specs=[_HBM] * (2 * n) + [_SEM] * (2 * n) + [pl.BlockSpec(memory_space=pl.ANY)],
        out_specs=[_HBM] * (2 * n),
        input_output_aliases={i: i for i in range(2 * n)},
        compiler_params=pltpu.CompilerParams(has_side_effects=_EFFECT),
    )(*xs, *lands, *[h[0] for h in handles], *[h[1] for h in handles], after)
    return res[n:]


def _tile(n, cap):
    best = None
    for t in range(LANES, min(n, cap) + 1, LANES):
        if n % t == 0:
            best = t
    if best is None or (best < 256 and n <= 2304):
        return n
    return best


def _row_tile(m, cap):
    if m <= cap:
        return m
    t = cap
    while m % t:
        t //= 2
    return t


def _after_spec(after):
    return [] if after is None else [pl.BlockSpec(memory_space=pl.ANY)]


def _after_arg(after):
    return [] if after is None else [after]


def _mm_body(nk, dims, a_ref, b_ref, *refs):
    o_ref, acc_ref = refs[-2:]
    k = pl.program_id(2)

    @pl.when(k == 0)
    def _():
        acc_ref[...] = jnp.zeros_like(acc_ref)

    acc_ref[...] += lax.dot_general(a_ref[...].astype(BF16), b_ref[...].astype(BF16), (dims, ((), ())),
                                    preferred_element_type=F32)

    @pl.when(k == nk - 1)
    def _():
        o_ref[...] = acc_ref[...].astype(o_ref.dtype)


def mm_nn(name, a, w, out_dtype=F32, after=None):
    M, K = a.shape
    G, _, n = w.shape
    tn = _tile(n, 1024)
    tm = _row_tile(M, 512 if tn > 1024 else 1024)
    tk = _row_tile(K, 1024)
    r = n // tn
    nk = K // tk
    return pl.pallas_call(
        functools.partial(_mm_body, nk, ((1,), (0,))), name=name,
        out_shape=jax.ShapeDtypeStruct((M, G * n), out_dtype),
        grid=(M // tm, G * r, nk),
        in_specs=[pl.BlockSpec((tm, tk), lambda i, j, k: (i, k)),
                  pl.BlockSpec((None, tk, tn), lambda i, j, k: (j // r, k, j % r))] + _after_spec(after),
        out_specs=pl.BlockSpec((tm, tn), lambda i, j, k: (i, j)),
        scratch_shapes=[pltpu.VMEM((tm, tn), F32)],
        compiler_params=_params(("parallel", "parallel", "arbitrary")),
    )(a, w, *_after_arg(after))


def mm_nt(name, dy, w, out_dtype=F32, after=None):
    M, N = dy.shape
    G, K, n = w.shape
    tn = _tile(n, 1024)
    tm = _row_tile(M, 512 if tn > 1024 else 1024)
    tk = _row_tile(K, 1024)
    r = n // tn
    nk = G * r
    return pl.pallas_call(
        functools.partial(_mm_body, nk, ((1,), (1,))), name=name,
        out_shape=jax.ShapeDtypeStruct((M, K), out_dtype),
        grid=(M // tm, K // tk, nk),
        in_specs=[pl.BlockSpec((tm, tn), lambda i, j, k: (i, k)),
                  pl.BlockSpec((None, tk, tn), lambda i, j, k: (k // r, j, k % r))] + _after_spec(after),
        out_specs=pl.BlockSpec((tm, tk), lambda i, j, k: (i, j)),
        scratch_shapes=[pltpu.VMEM((tm, tk), F32)],
        compiler_params=_params(("parallel", "parallel", "arbitrary")),
    )(dy, w, *_after_arg(after))


def mm_tn(name, a, dy, G, out_dtype=F32, after=None):
    M, K = a.shape
    N = dy.shape[1]
    n = N // G
    tn = _tile(n, 1024)
    tk = _row_tile(K, 512 if tn > 1024 else 1024)
    tm = _row_tile(M, 1024)
    r = n // tn
    nk = M // tm
    return pl.pallas_call(
        functools.partial(_mm_body, nk, ((0,), (0,))), name=name,
        out_shape=jax.ShapeDtypeStruct((G, K, n), out_dtype),
        grid=(K // tk, G * r, nk),
        in_specs=[pl.BlockSpec((tm, tk), lambda i, j, k: (k, i)),
                  pl.BlockSpec((tm, tn), lambda i, j, k: (k, j))] + _after_spec(after),
        out_specs=pl.BlockSpec((None, tk, tn), lambda i, j, k: (j // r, i, j % r)),
        scratch_shapes=[pltpu.VMEM((tk, tn), F32)],
        compiler_params=_params(("parallel", "parallel", "arbitrary")),
    )(a, dy, *_after_arg(after))


def rowwise(name, fn, rows, vecs, outs, accs=(), tm=256, after=None):
    rows = [r if isinstance(r, tuple) else (r, r.shape[1], 0) for r in rows]
    n_fn = len(rows) + len(vecs)
    vecs = list(vecs) + _after_arg(after)
    S = rows[0][0].shape[0]
    tm = _row_tile(S, tm)
    n_r, n_v, n_o, n_a = len(rows), len(vecs), len(outs), len(accs)

    def body(*refs):
        ins = [ref[...] for ref in refs[:n_r + n_v]]
        o_refs = refs[n_r + n_v:n_r + n_v + n_o]
        a_refs = refs[n_r + n_v + n_o:]
        res = fn(*ins[:n_fn])
        res = res if isinstance(res, (tuple, list)) else (res,)
        for ref, val in zip(o_refs, res[:n_o]):
            ref[...] = val.astype(ref.dtype)
        if n_a:
            @pl.when(pl.program_id(0) == 0)
            def _():
                for ref in a_refs:
                    ref[...] = jnp.zeros_like(ref)
            for ref, val in zip(a_refs, res[n_o:]):
                ref[...] += val

    in_specs = [pl.BlockSpec((tm, w), functools.partial(lambda cb, i: (i, cb), cb)) for _, w, cb in rows]
    in_specs += [pl.BlockSpec(v.shape, lambda i: (0, 0)) for v in vecs]
    out_specs = [pl.BlockSpec((tm, w), lambda i: (i, 0)) for w, _ in outs]
    out_specs += [pl.BlockSpec(a, lambda i: (0, 0)) for a in accs]
    out_shape = [jax.ShapeDtypeStruct((S, w), dt) for w, dt in outs]
    out_shape += [jax.ShapeDtypeStruct(a, F32) for a in accs]
    res = pl.pallas_call(
        body, name=name, out_shape=out_shape, grid=(S // tm,),
        in_specs=in_specs, out_specs=out_specs,
        compiler_params=_params(("arbitrary",)),
    )(*[r[0] for r in rows], *vecs)
    return res


def _rms(x):
    return x * lax.rsqrt(jnp.mean(x * x, axis=-1, keepdims=True) + EPS)


def _modulate(x, gain, scale, shift):
    return _rms(x) * gain * (1.0 + scale) + shift


def _ada_slices(ada_raw, bias):
    ada = ada_raw + bias
    return [ada[:, i * D_MODEL:(i + 1) * D_MODEL] for i in range(ada.shape[1] // D_MODEL)]


def _norm_wide_heads(y, gain, heads):
    w = y.shape[1] // heads
    return jnp.concatenate([_rms(y[:, h * w:(h + 1) * w]) * gain[:, h * w:(h + 1) * w] for h in range(heads)], axis=1)


def _norm_fox_heads(x, gain):
    outs = []
    for p in range(x.shape[1] // LANES):
        blk = x[:, p * LANES:(p + 1) * LANES]
        low = lax.broadcasted_iota(jnp.int32, blk.shape, 1) < FOX_DH
        sq = blk * blk
        ss_low = jnp.sum(jnp.where(low, sq, 0.0), axis=1, keepdims=True)
        ss_high = jnp.sum(jnp.where(low, 0.0, sq), axis=1, keepdims=True)
        outs.append(blk * lax.rsqrt(jnp.where(low, ss_low, ss_high) * (1.0 / FOX_DH) + EPS))
    return jnp.concatenate(outs, axis=1) * gain


def _silu(x):
    return x * jax.nn.sigmoid(x)


def _log_sigmoid(z):
    return -(jnp.maximum(-z, 0.0) + jnp.log(1.0 + jnp.exp(-jnp.abs(z))))


def _rotate(x, cos, sin, heads, sign):
    w = x.shape[1] // heads
    half = w // 2
    outs = []
    for h in range(heads):
        x1 = x[:, h * w:h * w + half]
        x2 = x[:, h * w + half:(h + 1) * w]
        outs += [x1 * cos - sign * x2 * sin, sign * x1 * sin + x2 * cos]
    return jnp.concatenate(outs, axis=1)


def _vjp(fn, primals, ct):
    return jax.vjp(fn, *primals)[1](ct)


_LOG_GAMMAS = [float(np.log(np.float32(1.0) - np.float32(2.0) ** np.float32(-5.0 - h))) for h in range(RET_HEADS)]


def retention(name, q, k, v, reverse):
    (qa, dk, qo), (ka, _, ko), (va, dv, vo) = q, k, v
    S = qa.shape[0]
    C = RET_CHUNK
    nc = S // C

    def body(q_ref, k_ref, v_ref, o_ref, state):
        h = pl.program_id(0)

        @pl.when(pl.program_id(1) == 0)
        def _():
            state[...] = jnp.zeros_like(state)

        log_g = jnp.float32(_LOG_GAMMAS[RET_HEADS - 1])
        for i in range(RET_HEADS - 2, -1, -1):
            log_g = jnp.where(h == i, jnp.float32(_LOG_GAMMAS[i]), log_g)
        row = lax.broadcasted_iota(jnp.int32, (C, C), 0)
        col = lax.broadcasted_iota(jnp.int32, (C, C), 1)
        rel = (col - row if reverse else row - col).astype(F32)
        decay = jnp.where(rel >= 0, jnp.exp(log_g * jnp.maximum(rel, 0.0)), 0.0)
        j = lax.broadcasted_iota(jnp.int32, (C, 1), 0).astype(F32)
        q_decay = jnp.exp(log_g * (C - j if reverse else j + 1.0))
        k_decay = jnp.exp(log_g * (j if reverse else C - 1.0 - j))
        chunk_decay = jnp.exp(jnp.full((1, 1), log_g * C, F32))

        qc = q_ref[...].astype(BF16)
        kf = k_ref[...].astype(F32)
        vc = v_ref[...].astype(BF16)
        scores = lax.dot_general(qc, kf.astype(BF16), (((1,), (1,)), ((), ())), preferred_element_type=F32) * decay
        intra = jnp.dot(scores.astype(BF16), vc, preferred_element_type=F32)
        cross = jnp.dot(qc, state[...].astype(BF16), preferred_element_type=F32) * q_decay
        o_ref[...] = intra + cross
        upd = lax.dot_general((kf * k_decay).astype(BF16), vc, (((0,), (0,)), ((), ())), preferred_element_type=F32)
        state[...] = state[...] * chunk_decay + upd

    def chunk(i):
        return nc - 1 - i if reverse else i

    return pl.pallas_call(
        body, name=name, out_shape=jax.ShapeDtypeStruct((S, RET_HEADS * dv), F32),
        grid=(RET_HEADS, nc),
        in_specs=[pl.BlockSpec((C, dk), lambda h, i: (chunk(i), qo + h)),
                  pl.BlockSpec((C, dk), lambda h, i: (chunk(i), ko + h)),
                  pl.BlockSpec((C, dv), lambda h, i: (chunk(i), vo + h))],
        out_specs=pl.BlockSpec((C, dv), lambda h, i: (chunk(i), h)),
        scratch_shapes=[pltpu.VMEM((dk, dv), F32)],
        compiler_params=_params(("parallel", "arbitrary")),
    )(qa, ka, va)


FOX_TQ = 256
N_PAIR = FOX_HEADS // 2


def _fox_heads(q2, f_blk, pair):
    low = lax.broadcasted_iota(jnp.int32, (1, LANES), 1) < FOX_DH
    lane = lax.broadcasted_iota(jnp.int32, f_blk.shape, 1)
    heads = []
    for a in range(2):
        mask = low if a == 0 else jnp.logical_not(low)
        qa = jnp.where(mask, q2, 0.0).astype(BF16)
        fq = jnp.sum(jnp.where(lane == 2 * pair + a, f_blk, 0.0), axis=1, keepdims=True)
        heads.append((mask, qa, fq))
    return heads


def _fox_logits(qa, k_blk, fq, fk, i, kb, t):
    logits = lax.dot_general(qa, k_blk, (((1,), (1,)), ((), ())), preferred_element_type=F32) * (FOX_DH ** -0.5)
    logits = logits + fq - fk
    row = i * t + lax.broadcasted_iota(jnp.int32, (t, t), 0)
    col = kb * t + lax.broadcasted_iota(jnp.int32, (t, t), 1)
    return jnp.where(col <= row, logits, -jnp.inf)


def fox_forward(name, qn, kn, kvf, f_cum, f_cum_t):
    S = qn.shape[0]
    t = _row_tile(S, FOX_TQ)
    v_block0 = D_MODEL // LANES

    def body(q_ref, k_ref, v_ref, f_ref, ft_ref, y_ref, lse_ref):
        pair, i = pl.program_id(0), pl.program_id(1)
        heads = _fox_heads(q_ref[...], f_ref[...], pair)

        def step(kb, carry):
            off = pl.multiple_of(kb * t, t)
            k_blk = k_ref[pl.ds(off, t), :].astype(BF16)
            v_blk = v_ref[pl.ds(off, t), :].astype(BF16)
            out = []
            for a, (_, qa, fq) in enumerate(heads):
                m, l, acc = carry[3 * a:3 * a + 3]
                s = _fox_logits(qa, k_blk, fq, ft_ref[kb, pl.ds(2 * pair + a, 1), :], i, kb, t)
                m_new = jnp.maximum(m, jnp.max(s, axis=1, keepdims=True))
                alpha = jnp.exp(m - m_new)
                p = jnp.exp(s - m_new)
                out += [m_new, alpha * l + jnp.sum(p, axis=1, keepdims=True),
                        alpha * acc + jnp.dot(p.astype(BF16), v_blk, preferred_element_type=F32)]
            return tuple(out)

        init = (jnp.full((t, 1), -jnp.inf, F32), jnp.zeros((t, 1), F32), jnp.zeros((t, LANES), F32)) * 2
        m0, l0, acc0, m1, l1, acc1 = lax.fori_loop(0, i + 1, step, init)
        low = heads[0][0]
        y_ref[...] = jnp.where(low, acc0 / l0, acc1 / l1)
        lse_ref[...] = jnp.where(low, m0 + jnp.log(l0), m1 + jnp.log(l1))

    return pl.pallas_call(
        body, name=name,
        out_shape=[jax.ShapeDtypeStruct((S, D_MODEL), F32), jax.ShapeDtypeStruct((S, D_MODEL), F32)],
        grid=(N_PAIR, S // t),
        in_specs=[pl.BlockSpec((t, LANES), lambda p, i: (i, p)),
                  pl.BlockSpec((S, LANES), lambda p, i: (0, p)),
                  pl.BlockSpec((S, LANES), lambda p, i: (0, v_block0 + p)),
                  pl.BlockSpec((t, LANES), lambda p, i: (i, 0)),
                  pl.BlockSpec((S // t, LANES, t), lambda p, i: (0, 0, 0))],
        out_specs=[pl.BlockSpec((t, LANES), lambda p, i: (i, p)),
                   pl.BlockSpec((t, LANES), lambda p, i: (i, p))],
        compiler_params=_params(("parallel", "arbitrary")),
    )(qn, kn, kvf, f_cum, f_cum_t)


def fox_backward(name, qn, kn, kvf, f_cum, f_cum_t, y, dy, lse):
    S = qn.shape[0]
    t = _row_tile(S, FOX_TQ)
    v_block0 = D_MODEL // LANES
    scale = FOX_DH ** -0.5

    def body(q_ref, k_ref, v_ref, f_ref, ft_ref, y_ref, dy_ref, lse_ref,
             dq_ref, dk_ref, dv_ref, dfq_ref, dfk_ref):
        pair, i = pl.program_id(0), pl.program_id(1)

        @pl.when(i == 0)
        def _():
            dk_ref[...] = jnp.zeros_like(dk_ref)
            dv_ref[...] = jnp.zeros_like(dv_ref)
            dfk_ref[...] = jnp.zeros_like(dfk_ref)

        heads = _fox_heads(q_ref[...], f_ref[...], pair)
        y2, dy2, lse2 = y_ref[...], dy_ref[...], lse_ref[...]
        per_head = []
        for mask, qa, fq in heads:
            dy_a = jnp.where(mask, dy2, 0.0)
            per_head.append((qa, fq, jnp.max(jnp.where(mask, lse2, -jnp.inf), axis=1, keepdims=True),
                             jnp.sum(dy_a * y2, axis=1, keepdims=True), dy_a.astype(BF16)))
        lane = lax.broadcasted_iota(jnp.int32, (t, LANES), 1)

        def step(kb, carry):
            dqs, dfq = list(carry[:2]), carry[2]
            off = pl.multiple_of(kb * t, t)
            k_blk = k_ref[pl.ds(off, t), :].astype(BF16)
            v_blk = v_ref[pl.ds(off, t), :].astype(BF16)
            for a, (qa, fq, lse_a, delta, dy_b) in enumerate(per_head):
                s = _fox_logits(qa, k_blk, fq, ft_ref[kb, pl.ds(2 * pair + a, 1), :], i, kb, t)
                p = jnp.exp(s - lse_a)
                dp = lax.dot_general(dy_b, v_blk, (((1,), (1,)), ((), ())), preferred_element_type=F32)
                ds = p * (dp - delta)
                dfq = dfq + jnp.where(lane == 2 * pair + a, jnp.sum(ds, axis=1, keepdims=True), 0.0)
                dfk_ref[pl.ds(kb * 8 + a, 1), :] += -jnp.sum(ds, axis=0, keepdims=True)
                ds_b = ds.astype(BF16)
                dqs[a] = dqs[a] + jnp.dot(ds_b, k_blk, preferred_element_type=F32) * scale
                dk_ref[pl.ds(off, t), :] += lax.dot_general(
                    ds_b, qa, (((0,), (0,)), ((), ())), preferred_element_type=F32) * scale
                dv_ref[pl.ds(off, t), :] += lax.dot_general(
                    p.astype(BF16), dy_b, (((0,), (0,)), ((), ())), preferred_element_type=F32)
            return dqs[0], dqs[1], dfq

        zero = jnp.zeros((t, LANES), F32)
        dq0, dq1, dfq = lax.fori_loop(0, i + 1, step, (zero, zero, zero))
        dq_ref[...] = jnp.where(heads[0][0], dq0, dq1)
        dfq_ref[...] = dfq

    row_blk = pl.BlockSpec((t, LANES), lambda p, i: (i, p))
    col_blk = pl.BlockSpec((S, LANES), lambda p, i: (0, p))
    return pl.pallas_call(
        body, name=name,
        out_shape=[jax.ShapeDtypeStruct((S, D_MODEL), F32)] * 3
        + [jax.ShapeDtypeStruct((N_PAIR, S, LANES), F32), jax.ShapeDtypeStruct((N_PAIR, 8 * (S // t), t), F32)],
        grid=(N_PAIR, S // t),
        in_specs=[row_blk, col_blk,
                  pl.BlockSpec((S, LANES), lambda p, i: (0, v_block0 + p)),
                  pl.BlockSpec((t, LANES), lambda p, i: (i, 0)),
                  pl.BlockSpec((S // t, LANES, t), lambda p, i: (0, 0, 0)),
                  row_blk, row_blk, row_blk],
        out_specs=[row_blk, col_blk, col_blk,
                   pl.BlockSpec((None, t, LANES), lambda p, i: (p, i, 0)),
                   pl.BlockSpec((None, 8 * (S // t), t), lambda p, i: (p, 0, 0))],
        compiler_params=_params(("parallel", "arbitrary")),
    )(qn, kn, kvf, f_cum, f_cum_t, y, dy, lse)


def cumsum_rows(name, x, reverse):
    S = x.shape[0]
    C = LANES
    nc = S // C

    def body(x_ref, o_ref):
        row = lax.broadcasted_iota(jnp.int32, (C, C), 0)
        col = lax.broadcasted_iota(jnp.int32, (C, C), 1)
        tri = jnp.where(col >= row if reverse else col <= row, 1.0, 0.0).astype(F32)
        carry = jnp.zeros((1, LANES), F32)
        for i in (range(nc - 1, -1, -1) if reverse else range(nc)):
            blk = x_ref[i * C:(i + 1) * C, :]
            loc = jnp.dot(tri, blk, preferred_element_type=F32, precision=lax.Precision.HIGHEST)
            o_ref[i * C:(i + 1) * C, :] = loc + carry
            carry = carry + (loc[0:1, :] if reverse else loc[C - 1:C, :])

    return pl.pallas_call(body, name=name, out_shape=jax.ShapeDtypeStruct((S, LANES), F32),
                          compiler_params=_params())(x)


def adamw(name, parts, w, m, v):
    P, R, C = parts.shape
    tr = _row_tile(R, 256)

    def body(p_ref, w_ref, m_ref, v_ref, g_out, d_out, m_out, v_out):
        g = p_ref[0].astype(F32)
        for i in range(1, P):
            g = g + p_ref[i].astype(F32)
        m2 = ADAM_B1 * m_ref[...] + (1.0 - ADAM_B1) * g
        v2 = ADAM_B2 * v_ref[...] + (1.0 - ADAM_B2) * jnp.square(g)
        m_hat = m2 / (1.0 - ADAM_B1 ** ADAM_STEP)
        v_hat = v2 / (1.0 - ADAM_B2 ** ADAM_STEP)
        g_out[...] = g
        d_out[...] = -ADAM_LR * (m_hat / (jnp.sqrt(v_hat) + ADAM_EPS) + ADAM_WD * w_ref[...])
        m_out[...] = m2
        v_out[...] = v2

    blk = pl.BlockSpec((tr, C), lambda i: (i, 0))
    return pl.pallas_call(
        body, name=name, out_shape=[jax.ShapeDtypeStruct((R, C), F32)] * 4, grid=(R // tr,),
        in_specs=[pl.BlockSpec((P, tr, C), lambda i: (0, i, 0)), blk, blk, blk],
        out_specs=[blk] * 4, compiler_params=_params(("parallel",)),
    )(parts, w, m, v)


def kernel(x, c, positions, norm_mix_gain, norm_mlp_gain, w_ada, b_ada, w_mlp_in, w_mlp_out, ret_w_in, ret_norm_gain, ret_w_out, kv_norm_gain, kv_w_ada, kv_b_ada, kv_w, forget_bias, k_norm_gain, fox_w_in, q_norm_gain, fox_w_out, loss_target, m_norm_mix_gain, m_norm_mlp_gain, m_w_ada, m_b_ada, m_w_mlp_in, m_w_mlp_out, m_ret_w_in, m_ret_norm_gain, m_ret_w_out, m_kv_norm_gain, m_kv_w_ada, m_kv_b_ada, m_kv_w, m_forget_bias, m_k_norm_gain, m_fox_w_in, m_q_norm_gain, m_fox_w_out, v_norm_mix_gain, v_norm_mlp_gain, v_w_ada, v_b_ada, v_w_mlp_in, v_w_mlp_out, v_ret_w_in, v_ret_norm_gain, v_ret_w_out, v_kv_norm_gain, v_kv_w_ada, v_kv_b_ada, v_kv_w, v_forget_bias, v_k_norm_gain, v_fox_w_in, v_q_norm_gain, v_fox_w_out):
    D = D_MODEL
    S = x.shape[1]
    x0 = x.reshape(S, D)
    target = loss_target.reshape(S, D)
    me = 4 * lax.axis_index("x") + 2 * lax.axis_index("y") + lax.axis_index("c")
    n_ada = w_ada.shape[2]
    n_kvada = kv_w_ada.shape[1]
    n_kv = kv_w.shape[1]

    c_all, ret_gain = all_gather("gather_c", [c.reshape(1, D), ret_norm_gain.reshape(RET_HEADS, -1)])
    ret_gain = jnp.transpose(ret_gain, (1, 0, 2)).reshape(1, RET_HEADS * RET_V)
    c_act = rowwise("silu_c", _silu, [c_all.reshape(N_DEV, D)], [], [(D, F32)])[0]
    w_ada_cat = jnp.concatenate([w_ada[0], w_ada[1], kv_w_ada], axis=1).astype(BF16)[None]
    n_cat = 2 * n_ada + n_kvada
    ada_part = mm_nn("ada_proj", c_act, w_ada_cat)
    ada_mine = all_to_all("ada_rows", [ada_part.reshape(N_DEV, 1, n_cat)])[0][:, 0]
    ada_raw = [ada_mine[:, l * n_ada:(l + 1) * n_ada].reshape(1, 6 * D) for l in range(2)]
    kvada_raw = ada_mine[:, 2 * n_ada:].reshape(1, 2 * D)
    kv_bias = kv_b_ada.reshape(1, 2 * D)
    kv_gain = kv_norm_gain.reshape(1, D)
    fb = jnp.pad(forget_bias.reshape(1, FOX_HEADS), ((0, 0), (0, LANES - FOX_HEADS)))
    k_gain = jnp.tile(k_norm_gain.reshape(1, FOX_DH), (1, FOX_HEADS))
    q_gain = jnp.tile(q_norm_gain.reshape(1, FOX_DH), (1, FOX_HEADS))

    w_names = ["ret_in", "ret_out", "mlp_in0", "mlp_out0", "kv", "fox_in", "fox_out", "mlp_in1", "mlp_out1"]
    w_handles, token = exchange_start("gather_weights_start", [
        ret_w_in[0].astype(BF16), ret_w_out[0].astype(BF16), w_mlp_in[0].astype(BF16), w_mlp_out[0].astype(BF16),
        kv_w.astype(BF16), fox_w_in[0].astype(BF16), fox_w_out[0].astype(BF16), w_mlp_in[1].astype(BF16),
        w_mlp_out[1].astype(BF16)], scatter=False, after=(ada_mine, ret_gain))
    w_handles = dict(zip(w_names, w_handles))

    def weight(name, after):
        return exchange_wait("gather_wait_" + name, [w_handles[name]], after, scatter=False)[0]

    pos = positions.reshape(S, 1).astype(F32)
    half = RET_QK // 2
    inv_freq = jnp.asarray((ROPE_BASE ** (-np.arange(half, dtype=np.float32) / half)).reshape(1, half), F32)

    def angles(p, f):
        ang = p * f
        return jnp.cos(ang), jnp.sin(ang)

    cos, sin = rowwise("rope_table", angles, [pos], [inv_freq], [(half, F32), (half, F32)])

    def mod_mix(layer):
        def fn(xb, ada, bias, gain):
            sh, sc = _ada_slices(ada, bias)[:2]
            return _modulate(xb, gain[layer:layer + 1], sc, sh)
        return fn

    def mod_mlp(layer):
        def fn(xb, ada, bias, gain):
            sh, sc = _ada_slices(ada, bias)[3:5]
            return _modulate(xb, gain[layer:layer + 1], sc, sh)
        return fn

    h1_0 = rowwise("mod_mix0", mod_mix(0), [x0], [ada_raw[0], b_ada[0:1], norm_mix_gain], [(D, BF16)], after=token)[0]
    W_ret_in = weight("ret_in", h1_0)
    proj = mm_nn("ret_proj", h1_0, W_ret_in)

    def rope_fwd(qb, kb, cs, sn):
        return _rotate(qb, cs, sn, RET_HEADS, 1.0), _rotate(kb, cs, sn, RET_HEADS, 1.0) * (RET_QK ** -0.5)

    q_rot, k_rot = rowwise("rope", rope_fwd, [(proj, D, 0), (proj, D, 1), cos, sin], [], [(D, F32), (D, F32)])
    v_ret = (proj, RET_V, (2 * D) // RET_V)
    y_ret = retention("ret_fwd", (q_rot, RET_QK, 0), (k_rot, RET_QK, 0), v_ret, reverse=False)

    def ret_gate(yb, gb, gain):
        return _silu(gb) * _norm_wide_heads(yb, gain, RET_HEADS)

    mixin0 = rowwise("ret_gate", ret_gate, [y_ret, (proj, 2 * D, 2)], [ret_gain], [(2 * D, BF16)])[0]
    W_ret_out = weight("ret_out", mixin0).reshape(1, 2 * D, D)
    mix0 = mm_nn("ret_out", mixin0, W_ret_out)

    def residual_mod(layer, slot):
        def fn(xb, bb, ada, bias, gain):
            s = _ada_slices(ada, bias)
            xn = xb + s[2] * bb
            return xn, _modulate(xn, gain[layer:layer + 1], s[4], s[3])
        return fn

    x1, h2_0 = rowwise("res_mix0", residual_mod(0, 0), [x0, mix0], [ada_raw[0], b_ada[0:1], norm_mlp_gain],
                       [(D, F32), (D, BF16)])

    W_mlp_in, W_mlp_out = {}, {}

    def mlp_forward(tag, h2, layer):
        W_mlp_in[layer] = weight("mlp_in" + tag, h2)
        u = mm_nn("mlp_in" + tag, h2, W_mlp_in[layer])
        act = rowwise("relu2" + tag, lambda ub: jnp.square(jnp.maximum(ub, 0.0)), [u], [], [(4 * D, BF16)])[0]
        W_mlp_out[layer] = weight("mlp_out" + tag, act).reshape(1, 4 * D, D)
        return u, act, mm_nn("mlp_out" + tag, act, W_mlp_out[layer])

    u0, act0, mlp0 = mlp_forward("0", h2_0, 0)

    def res_mlp0(xb, bb, ada0, bias0, ada1, bias1, kva, kvb, gain_mix, gain_kv):
        xn = xb + _ada_slices(ada0, bias0)[5] * bb
        s1 = _ada_slices(ada1, bias1)
        kv_shift, kv_scale = _ada_slices(kva, kvb)
        return xn, _modulate(xn, gain_kv, kv_scale, kv_shift), _modulate(xn, gain_mix[1:2], s1[1], s1[0])

    x2, h_kv, h1_1 = rowwise("res_mlp0", res_mlp0, [x1, mlp0],
                             [ada_raw[0], b_ada[0:1], ada_raw[1], b_ada[1:2], kvada_raw, kv_bias, norm_mix_gain, kv_gain],
                             [(D, F32), (D, BF16), (D, BF16)])

    kv_full = jnp.transpose(weight("kv", h_kv), (1, 0, 2)).reshape(D, N_DEV * n_kv)
    W_kv = jnp.pad(kv_full, ((0, 0), (0, KV_PAD - N_DEV * n_kv)))[None]
    kvf = mm_nn("kv_proj", h_kv, W_kv)

    def kv_post(kb, fblk, kg, bias):
        head = lax.broadcasted_iota(jnp.int32, fblk.shape, 1) < FOX_HEADS
        return _norm_fox_heads(kb, kg), jnp.where(head, _log_sigmoid(fblk + bias), 0.0)

    kn, log_f = rowwise("kv_post", kv_post, [(kvf, D, 0), (kvf, LANES, 2 * D // LANES)], [k_gain, fb],
                        [(D, F32), (LANES, F32)])
    f_cum = cumsum_rows("f_cumsum", log_f, reverse=False)
    n_kb = S // _row_tile(S, FOX_TQ)
    f_cum_t = jnp.transpose(f_cum.T.reshape(LANES, n_kb, S // n_kb), (1, 0, 2))

    W_fox_in = weight("fox_in", kvf)
    qo = mm_nn("fox_proj", h1_1, W_fox_in)
    qn = rowwise("q_norm", _norm_fox_heads, [(qo, D, 0)], [q_gain], [(D, F32)])[0]
    y_att, lse = fox_forward("fox_fwd", qn, kn, kvf, f_cum, f_cum_t)
    mixin1 = rowwise("fox_gate", lambda ob, yb: jax.nn.sigmoid(ob) * yb, [(qo, D, 1), y_att], [], [(D, BF16)])[0]
    W_fox_out = weight("fox_out", mixin1).reshape(1, D, D)
    mix1 = mm_nn("fox_out", mixin1, W_fox_out)
    x3, h2_1 = rowwise("res_mix1", residual_mod(1, 0), [x2, mix1], [ada_raw[1], b_ada[1:2], norm_mlp_gain],
                       [(D, F32), (D, BF16)])
    u1, act1, mlp1 = mlp_forward("1", h2_1, 1)

    def loss_head(xb, bb, tb, ada, bias):
        g2 = _ada_slices(ada, bias)[5]
        err = xb + g2 * bb - tb
        dx = err * (1.0 / D)
        loss = 0.5 * jnp.sum(jnp.sum(err * err, axis=1, keepdims=True) * (1.0 / D), axis=0, keepdims=True)
        return dx, (dx * g2), jnp.broadcast_to(loss, (1, LANES)), jnp.sum(dx * bb, axis=0, keepdims=True)

    dx4, dmlp1, loss_acc, dg2_1 = rowwise("loss_head", loss_head, [x3, mlp1, target], [ada_raw[1], b_ada[1:2]],
                                          [(D, F32), (D, BF16)], [(1, LANES), (1, D)])
    loss = lax.psum(loss_acc[0, 0], ("x", "y", "c"))

    def mlp_backward(tag, dmlp, act, u, h2, layer):
        d_act = mm_nt("mlp_out_dx" + tag, dmlp, W_mlp_out[layer])
        gw_out = mm_tn("mlp_out_dw" + tag, act, dmlp, 1, BF16).reshape(N_DEV, -1, D)
        du = rowwise("relu2_bwd" + tag, lambda db, ub: db * (2.0 * jnp.maximum(ub, 0.0)), [d_act, u], [], [(4 * D, BF16)])[0]
        dh2 = mm_nt("mlp_in_dx" + tag, du, W_mlp_in[layer])
        gw_in = mm_tn("mlp_in_dw" + tag, h2, du, N_DEV, BF16)
        return dh2, gw_in, gw_out

    def mod_backward(layer, slots, gate_slot):
        def fn(xb, dhb, dresb, branchb, ada, bias, gain):
            s = _ada_slices(ada, bias)
            g = gain[layer:layer + 1]
            dx, dgain, dsc, dsh = _vjp(_modulate, (xb, g, s[slots[1]], s[slots[0]]), dhb)
            dx = dx + dresb
            d_branch = dx * s[gate_slot]
            return dx, d_branch, dgain, dsc, dsh, jnp.sum(dx * branchb, axis=0, keepdims=True)
        return fn

    vec = (1, D)
    dh2_1, gw_mlp_in1, gw_mlp_out1 = mlp_backward("1", dmlp1, act1, u1, h2_1, 1)
    dx3, dmix1, dgain_mlp1, dsc2_1, dsh2_1, dg1_1 = rowwise(
        "mod_mlp1_bwd", mod_backward(1, (3, 4), 2), [x3, dh2_1, dx4, mix1], [ada_raw[1], b_ada[1:2], norm_mlp_gain],
        [(D, F32), (D, BF16)], [vec] * 4)
    dmixin1 = mm_nt("fox_out_dx", dmix1, W_fox_out)
    gw_fox_out = mm_tn("fox_out_dw", mixin1, dmix1, 1, BF16).reshape(N_DEV, -1, D)

    def fox_gate_bwd(db, ob, yb):
        sg = jax.nn.sigmoid(ob)
        return db * sg, db * yb * sg * (1.0 - sg)

    dy_att, d_og = rowwise("fox_gate_bwd", fox_gate_bwd, [dmixin1, (qo, D, 1), y_att], [], [(D, F32), (D, F32)])
    dqn, dkn, dv_att, dfq, dfk = fox_backward("fox_bwd", qn, kn, kvf, f_cum, f_cum_t, y_att, dy_att, lse)

    def q_norm_bwd(qb, db, ogb, gain):
        dq, dgain = _vjp(_norm_fox_heads, (qb, gain), db)
        return jnp.concatenate([dq, ogb], axis=1), dgain

    dqo, dq_gain = rowwise("q_norm_bwd", q_norm_bwd, [(qo, D, 0), dqn, d_og], [q_gain], [(2 * D, BF16)], [vec])
    dh1_1 = mm_nt("fox_proj_dx", dqo, W_fox_in)
    gw_fox_in = mm_tn("fox_proj_dw", h1_1, dqo, N_DEV, BF16)

    dfk_rows = jnp.transpose(dfk.reshape(N_PAIR, n_kb, 8, S // n_kb)[:, :, :2], (0, 2, 1, 3)).reshape(FOX_HEADS, S)
    dfk_rows = jnp.pad(dfk_rows.T, ((0, 0), (0, LANES - FOX_HEADS)))

    def df_total(*blks):
        tot = blks[0]
        for b in blks[1:]:
            tot = tot + b
        return tot

    d_fcum = rowwise("df_sum", df_total, [dfk_rows] + [dfq[p] for p in range(N_PAIR)], [], [(LANES, F32)])[0]
    d_logf = cumsum_rows("df_cumsum", d_fcum, reverse=True)

    def kv_post_bwd(kb, fblk, dkb, dvb, dlf, kg, bias):
        dk, dgain = _vjp(_norm_fox_heads, (kb, kg), dkb)
        df = dlf * (1.0 / (1.0 + jnp.exp(fblk + bias)))
        return jnp.concatenate([dk, dvb, df], axis=1), dgain, jnp.sum(df, axis=0, keepdims=True)

    dkvf, dk_gain, dfb = rowwise("kv_post_bwd", kv_post_bwd,
                                 [(kvf, D, 0), (kvf, LANES, 2 * D // LANES), dkn, dv_att, d_logf], [k_gain, fb],
                                 [(KV_PAD, BF16)], [vec, (1, LANES)])
    dh_kv = mm_nt("kv_proj_dx", dkvf, W_kv)
    gw_kv = mm_tn("kv_proj_dw", h_kv, dkvf, 1, BF16)[0, :, :N_DEV * n_kv]
    gw_kv = jnp.transpose(gw_kv.reshape(D, N_DEV, n_kv), (1, 0, 2))
    scat_a, token_a = exchange_start("scatter_start_a", [gw_mlp_in1, gw_mlp_out1, gw_fox_out, gw_fox_in, gw_kv], scatter=True)

    def x2_bwd(xb, dh1b, dhkb, dresb, branchb, ada0, bias0, ada1, bias1, kva, kvb, gain_mix, gain_kv):
        s1 = _ada_slices(ada1, bias1)
        kv_shift, kv_scale = _ada_slices(kva, kvb)
        dxa, dgain_mix, dsc1, dsh1 = _vjp(_modulate, (xb, gain_mix[1:2], s1[1], s1[0]), dh1b)
        dxb, dgain_kv, dkv_scale, dkv_shift = _vjp(_modulate, (xb, gain_kv, kv_scale, kv_shift), dhkb)
        dx = dresb + dxa + dxb
        g2 = _ada_slices(ada0, bias0)[5]
        return (dx, dx * g2, dgain_mix, dsc1, dsh1, dgain_kv, dkv_scale, dkv_shift,
                jnp.sum(dx * branchb, axis=0, keepdims=True))

    (dx2, dmlp0, dgain_mix1, dsc1_1, dsh1_1, dgain_kv, dkv_scale, dkv_shift, dg2_0) = rowwise(
        "x2_bwd", x2_bwd, [x2, dh1_1, dh_kv, dx3, mlp0],
        [ada_raw[0], b_ada[0:1], ada_raw[1], b_ada[1:2], kvada_raw, kv_bias, norm_mix_gain, kv_gain],
        [(D, F32), (D, BF16)], [vec] * 7, after=token_a)

    dh2_0, gw_mlp_in0, gw_mlp_out0 = mlp_backward("0", dmlp0, act0, u0, h2_0, 0)
    dx1, dmix0, dgain_mlp0, dsc2_0, dsh2_0, dg1_0 = rowwise(
        "mod_mlp0_bwd", mod_backward(0, (3, 4), 2), [x1, dh2_0, dx2, mix0], [ada_raw[0], b_ada[0:1], norm_mlp_gain],
        [(D, F32), (D, BF16)], [vec] * 4)
    dmixin0 = mm_nt("ret_out_dx", dmix0, W_ret_out)
    gw_ret_out = mm_tn("ret_out_dw", mixin0, dmix0, 1, BF16).reshape(N_DEV, -1, D)
    scat_b, token_b = exchange_start("scatter_start_b", [gw_mlp_in0, gw_mlp_out0, gw_ret_out], scatter=True)

    def ret_gate_bwd(db, yb, gb, gain):
        dy, dg, dgain = _vjp(lambda y_, g_, gn_: ret_gate(y_, g_, gn_), (yb, gb, gain), db)
        return dy, dg, dgain

    dy_ret, dgate, dret_gain = rowwise("ret_gate_bwd", ret_gate_bwd, [dmixin0, y_ret, (proj, 2 * D, 2)], [ret_gain],
                                       [(2 * D, F32), (2 * D, BF16)], [(1, 2 * D)], after=token_b)
    dy_h = (dy_ret, RET_V, 0)
    dq_rot = retention("ret_dq", dy_h, v_ret, (k_rot, RET_QK, 0), reverse=False)
    dk_rot = retention("ret_dk", v_ret, dy_h, (q_rot, RET_QK, 0), reverse=True)
    dv_ret = retention("ret_dv", (k_rot, RET_QK, 0), (q_rot, RET_QK, 0), dy_h, reverse=True)

    def rope_bwd(dqb, dkb, dvb, dgb, cs, sn):
        dq = _rotate(dqb, cs, sn, RET_HEADS, -1.0)
        dk = _rotate(dkb, cs, sn, RET_HEADS, -1.0) * (RET_QK ** -0.5)
        return jnp.concatenate([dq, dk, dvb, dgb.astype(F32)], axis=1)

    dproj = rowwise("rope_bwd", rope_bwd, [dq_rot, dk_rot, dv_ret, dgate, cos, sin], [], [(6 * D, BF16)])[0]
    gw_ret_in = mm_tn("ret_proj_dw", h1_0, dproj, N_DEV, BF16)
    scat_c, token_c = exchange_start("scatter_start_c", [gw_ret_in], scatter=True)
    dh1_0 = mm_nt("ret_proj_dx", dproj, W_ret_in, after=token_c)

    def x0_bwd(xb, dhb, dresb, ada, bias, gain):
        s = _ada_slices(ada, bias)
        dx, dgain, dsc, dsh = _vjp(_modulate, (xb, gain[0:1], s[1], s[0]), dhb)
        return dx + dresb, dgain, dsc, dsh

    grad_x, dgain_mix0, dsc1_0, dsh1_0 = rowwise("x0_bwd", x0_bwd, [x0, dh1_0, dx1],
                                                 [ada_raw[0], b_ada[0:1], norm_mix_gain], [(D, F32)], [vec] * 3)

    small = jnp.concatenate([
        dsh1_0, dsc1_0, dg1_0, dsh2_0, dsc2_0, dg2_0,
        dsh1_1, dsc1_1, dg1_1, dsh2_1, dsc2_1, dg2_1,
        dkv_shift, dkv_scale,
        dgain_mix0, dgain_mix1, dgain_mlp0, dgain_mlp1, dgain_kv,
        dret_gain,
        dq_gain.reshape(FOX_HEADS, FOX_DH).sum(axis=0).reshape(1, FOX_DH),
        dk_gain.reshape(FOX_HEADS, FOX_DH).sum(axis=0).reshape(1, FOX_DH),
        dfb,
    ], axis=1)
    small_all = all_gather("gather_small", [small])[0]
    o_ada = 14 * D
    d_ada = small_all[:, 0, :o_ada]
    d_cat = jnp.concatenate([
        lax.dynamic_slice_in_dim(d_ada[:, 0:6 * D], me * n_ada, n_ada, axis=1),
        lax.dynamic_slice_in_dim(d_ada[:, 6 * D:12 * D], me * n_ada, n_ada, axis=1),
        lax.dynamic_slice_in_dim(d_ada[:, 12 * D:14 * D], me * n_kvada, n_kvada, axis=1)], axis=1)
    gw_ada_cat = mm_tn("ada_dw", c_act, d_cat, 1, F32)[0]

    results = {}

    def update(name, parts, w, m, v):
        shape = w.shape
        R = int(np.prod(shape[:-1])) if len(shape) > 1 else 1
        C = shape[-1]
        g, d, m2, v2 = adamw("adamw_" + name, parts.reshape(parts.shape[0], R, C), w.reshape(R, C), m.reshape(R, C),
                             v.reshape(R, C))
        results[name] = tuple(t.reshape(shape) for t in (g, d, m2, v2))

    def small_parts(lo, width):
        return small_all[:, :, lo:lo + width]

    update("norm_mix_gain", jnp.concatenate([small_parts(o_ada, D), small_parts(o_ada + D, D)], axis=1),
           norm_mix_gain, m_norm_mix_gain, v_norm_mix_gain)
    update("norm_mlp_gain", jnp.concatenate([small_parts(o_ada + 2 * D, D), small_parts(o_ada + 3 * D, D)], axis=1),
           norm_mlp_gain, m_norm_mlp_gain, v_norm_mlp_gain)
    update("w_ada", jnp.stack([gw_ada_cat[:, :n_ada], gw_ada_cat[:, n_ada:2 * n_ada]])[None].reshape(1, 2 * D, n_ada),
           w_ada, m_w_ada, v_w_ada)
    update("b_ada", jnp.concatenate([small_parts(0, 6 * D), small_parts(6 * D, 6 * D)], axis=1), b_ada, m_b_ada, v_b_ada)
    o_ret = o_ada + 5 * D
    n_rg = ret_norm_gain.shape[2]
    ret_gain_parts = small_parts(o_ret, 2 * D).reshape(N_DEV, RET_HEADS, RET_V)
    ret_gain_parts = lax.dynamic_slice_in_dim(ret_gain_parts, me * n_rg, n_rg, axis=2)
    update("ret_norm_gain", ret_gain_parts, ret_norm_gain, m_ret_norm_gain, v_ret_norm_gain)
    update("kv_norm_gain", small_parts(o_ada + 4 * D, D), kv_norm_gain, m_kv_norm_gain, v_kv_norm_gain)
    update("kv_w_ada", gw_ada_cat[None, :, 2 * n_ada:], kv_w_ada, m_kv_w_ada, v_kv_w_ada)
    update("kv_b_ada", small_parts(12 * D, 2 * D), kv_b_ada, m_kv_b_ada, v_kv_b_ada)
    o_q = o_ret + 2 * D
    update("forget_bias", small_parts(o_q + 2 * FOX_DH, FOX_HEADS), forget_bias, m_forget_bias, v_forget_bias)
    update("k_norm_gain", small_parts(o_q + FOX_DH, FOX_DH), k_norm_gain, m_k_norm_gain, v_k_norm_gain)
    update("q_norm_gain", small_parts(o_q, FOX_DH), q_norm_gain, m_q_norm_gain, v_q_norm_gain)

    r_mlp_in1, r_mlp_out1, r_fox_out, r_fox_in, r_kv = exchange_wait(
        "scatter_wait_a", scat_a, results["w_ada"][1], scatter=True)
    r_mlp_in0, r_mlp_out0, r_ret_out = exchange_wait("scatter_wait_b", scat_b, r_kv, scatter=True)
    update("kv_w", r_kv, kv_w, m_kv_w, v_kv_w)
    update("fox_w_in", r_fox_in, fox_w_in, m_fox_w_in, v_fox_w_in)
    update("fox_w_out", r_fox_out, fox_w_out, m_fox_w_out, v_fox_w_out)
    update("ret_w_out", r_ret_out, ret_w_out, m_ret_w_out, v_ret_w_out)
    update("w_mlp_in", jnp.concatenate([r_mlp_in0, r_mlp_in1], axis=1), w_mlp_in, m_w_mlp_in, v_w_mlp_in)
    update("w_mlp_out", jnp.concatenate([r_mlp_out0, r_mlp_out1], axis=1), w_mlp_out, m_w_mlp_out, v_w_mlp_out)
    r_ret_in = exchange_wait("scatter_wait_c", scat_c, results["w_mlp_out"][1], scatter=True)[0]
    update("ret_w_in", r_ret_in, ret_w_in, m_ret_w_in, v_ret_w_in)

    order = ["norm_mix_gain", "norm_mlp_gain", "w_ada", "b_ada", "w_mlp_in", "w_mlp_out", "ret_w_in", "ret_norm_gain",
             "ret_w_out", "kv_norm_gain", "kv_w_ada", "kv_b_ada", "kv_w", "forget_bias", "k_norm_gain", "fox_w_in",
             "q_norm_gain", "fox_w_out"]
    out = [loss, grad_x.reshape(x.shape)]
    for slot in range(4):
        out += [results[n][slot] for n in order]
    return tuple(out)
```

```python
import functools
import math

import numpy as np
import jax
import jax.numpy as jnp
from jax import lax
from jax.experimental import pallas as pl
from jax.experimental.pallas import tpu as pltpu

F32 = jnp.float32
BF16 = jnp.bfloat16

N_DEV = 8
D_MODEL = 1024
RET_HEADS = 4
RET_QK = D_MODEL // RET_HEADS
RET_V = 2 * D_MODEL // RET_HEADS
RET_CHUNK = 128
ROPE_BASE = 10000.0
FOX_HEADS = 16
FOX_DH = D_MODEL // FOX_HEADS
EPS = 1e-6
LANES = 128
KV_PAD = 2 * D_MODEL + LANES

ADAM_LR = 0.001
ADAM_B1 = 0.9
ADAM_B2 = 0.999
ADAM_EPS = 1e-08
ADAM_WD = 0.01
ADAM_STEP = 10

VMEM_LIMIT_BYTES = 56 * 1024 * 1024


def _params(sem=None):
    return pltpu.CompilerParams(dimension_semantics=sem, vmem_limit_bytes=VMEM_LIMIT_BYTES)


def _me():
    return lax.axis_index("x"), lax.axis_index("y"), lax.axis_index("c")


def _peer(k):
    x, y, c = _me()
    return (1 - x if k & 4 else x, 1 - y if k & 2 else y, 1 - c if k & 1 else c)


def _peer_index(k):
    px, py, pc = _peer(k)
    return 4 * px + 2 * py + pc


def _exchange(name, xs, scatter):
    n = len(xs)

    def body(*refs):
        x_refs, o_refs = refs[:n], refs[n:2 * n]
        send_sems, recv_sems, local_sems = refs[2 * n:]
        x, y, c = _me()
        me = 4 * x + 2 * y + c
        local = []
        for i in range(n):
            src = x_refs[i].at[me] if scatter else x_refs[i]
            cp = pltpu.make_async_copy(src, o_refs[i].at[me], local_sems.at[i])
            cp.start()
            local.append(cp)
        remote = []
        for k in range(1, N_DEV):
            for i in range(n):
                src = x_refs[i].at[_peer_index(k)] if scatter else x_refs[i]
                cp = pltpu.make_async_remote_copy(
                    src_ref=src, dst_ref=o_refs[i].at[me],
                    send_sem=send_sems.at[(k - 1) * n + i], recv_sem=recv_sems.at[(k - 1) * n + i],
                    device_id=_peer(k), device_id_type=pl.DeviceIdType.MESH)
                cp.start()
                remote.append(cp)
        for cp in remote:
            cp.wait()
        for cp in local:
            cp.wait()

    out_shape = [jax.ShapeDtypeStruct(x.shape if scatter else (N_DEV,) + x.shape, x.dtype) for x in xs]
    any_spec = pl.BlockSpec(memory_space=pl.ANY)
    return pl.pallas_call(
        body, name=name, out_shape=out_shape,
        in_specs=[any_spec] * n, out_specs=[any_spec] * n,
        scratch_shapes=[pltpu.SemaphoreType.DMA(((N_DEV - 1) * n,)),
                        pltpu.SemaphoreType.DMA(((N_DEV - 1) * n,)),
                        pltpu.SemaphoreType.DMA((n,))],
    )(*xs)


def all_gather(name, xs):
    return _exchange(name, xs, scatter=False)


def all_to_all(name, xs):
    return _exchange(name, xs, scatter=True)


_HBM = pl.BlockSpec(memory_space=pltpu.HBM)
_SEM = pl.BlockSpec(memory_space=pltpu.SEMAPHORE)
_EFFECT = pltpu.SideEffectType.DATAFLOW_SIDE_EFFECTING


def _landing(block, shape, dtype):
    me = 4 * lax.axis_index("x") + 2 * lax.axis_index("y") + lax.axis_index("c")
    start = (me,) + (0,) * (len(shape) - 1)
    return lax.dynamic_update_slice(lax.empty(shape, dtype), block[None], start)


def exchange_start(name, xs, scatter, after=()):
    n, m = len(xs), len(after)
    if scatter:
        me = 4 * lax.axis_index("x") + 2 * lax.axis_index("y") + lax.axis_index("c")
        lands = [_landing(lax.dynamic_index_in_dim(x, me, 0, keepdims=False), x.shape, x.dtype) for x in xs]
    else:
        lands = [_landing(x, (N_DEV,) + x.shape, x.dtype) for x in xs]

    def body(*refs):
        x_refs, land_refs = refs[:n], refs[n:2 * n]
        send_sems, recv_sems = refs[2 * n + m:3 * n + m], refs[3 * n + m:4 * n + m]
        token = refs[6 * n + m]
        x, y, c = _me()
        me = 4 * x + 2 * y + c
        for i in range(n):
            for k in range(1, N_DEV):
                src = x_refs[i].at[_peer_index(k)] if scatter else x_refs[i]
                pltpu.make_async_remote_copy(
                    src_ref=src, dst_ref=land_refs[i].at[me],
                    send_sem=send_sems[i].at[k - 1], recv_sem=recv_sems[i].at[k - 1],
                    device_id=_peer(k), device_id_type=pl.DeviceIdType.MESH).start()
        token[...] = jnp.zeros_like(token)

    sems = [pltpu.SemaphoreType.DMA((N_DEV - 1,))] * (2 * n)
    thru = [pltpu.HBM(a.shape, a.dtype) for a in list(xs) + lands]
    res = pl.pallas_call(
        body, name=name,
        out_shape=sems + thru + [jax.ShapeDtypeStruct((8, LANES), F32)],
        in_specs=[_HBM] * (2 * n) + [pl.BlockSpec(memory_space=pl.ANY)] * m,
        out_specs=[_SEM] * (2 * n) + [_HBM] * (2 * n) + [pl.BlockSpec(memory_space=pltpu.VMEM)],
        input_output_aliases={i: 2 * n + i for i in range(2 * n)},
        compiler_params=pltpu.CompilerParams(has_side_effects=_EFFECT),
    )(*[pltpu.with_memory_space_constraint(a, pltpu.HBM) for a in list(xs) + lands], *after)
    handles = [(res[i], res[n + i], res[2 * n + i], res[3 * n + i]) for i in range(n)]
    return handles, res[4 * n]


def exchange_wait(name, handles, after, scatter):
    n = len(handles)

    def body(*refs):
        x_refs, land_refs = refs[:n], refs[n:2 * n]
        send_sems, recv_sems = refs[2 * n:3 * n], refs[3 * n:4 * n]
        x, y, c = _me()
        me = 4 * x + 2 * y + c
        for i in range(n):
            for k in range(1, N_DEV):
                src = x_refs[i].at[_peer_index(k)] if scatter else x_refs[i]
                cp = pltpu.make_async_remote_copy(
                    src_ref=src, dst_ref=land_refs[i].at[me],
                    send_sem=send_sems[i].at[k - 1], recv_sem=recv_sems[i].at[k - 1],
                    device_id=_peer(k), device_id_type=pl.DeviceIdType.MESH)
                cp.wait_send()
                cp.wait_recv()

    xs = [h[2] for h in handles]
    lands = [h[3] for h in handles]
    res = pl.pallas_call(
        body, name=name,
        out_shape=[pltpu.HBM(a.shape, a.dtype) for a in xs + lands],
        in_specs=[_HBM] * (2 * n) + [_SEM] * (2 * n) + [pl.BlockSpec(memory_space=pl.ANY)],
        out_specs=[_HBM] * (2 * n),
        input_output_aliases={i: i for i in range(2 * n)},
        compiler_params=pltpu.CompilerParams(has_side_effects=_EFFECT),
    )(*xs, *lands, *[h[0] for h in handles], *[h[1] for h in handles], after)
    return res[n:]


def _tile(n, cap):
    best = None
    for t in range(LANES, min(n, cap) + 1, LANES):
        if n % t == 0:
            best = t
    if best is None or (best < 256 and n <= 2304):
        return n
    return best


def _row_tile(m, cap):
    if m <= cap:
        return m
    t = cap
    while m % t:
        t //= 2
    return t


def _after_spec(after):
    return [] if after is None else [pl.BlockSpec(memory_space=pl.ANY)]


def _after_arg(after):
    return [] if after is None else [after]


def _mm_body(nk, dims, a_ref, b_ref, *refs):
    o_ref, acc_ref = refs[-2:]
    k = pl.program_id(2)

    @pl.when(k == 0)
    def _():
        acc_ref[...] = jnp.zeros_like(acc_ref)

    acc_ref[...] += lax.dot_general(a_ref[...].astype(BF16), b_ref[...].astype(BF16), (dims, ((), ())),
                                    preferred_element_type=F32)

    @pl.when(k == nk - 1)
    def _():
        o_ref[...] = acc_ref[...].astype(o_ref.dtype)


def mm_nn(name, a, w, out_dtype=F32, after=None):
    M, K = a.shape
    G, _, n = w.shape
    tn = _tile(n, 1024)
    tm = _row_tile(M, 512 if tn > 1024 else 1024)
    tk = _row_tile(K, 1024)
    r = n // tn
    nk = K // tk
    return pl.pallas_call(
        functools.partial(_mm_body, nk, ((1,), (0,))), name=name,
        out_shape=jax.ShapeDtypeStruct((M, G * n), out_dtype),
        grid=(M // tm, G * r, nk),
        in_specs=[pl.BlockSpec((tm, tk), lambda i, j, k: (i, k)),
                  pl.BlockSpec((None, tk, tn), lambda i, j, k: (j // r, k, j % r))] + _after_spec(after),
        out_specs=pl.BlockSpec((tm, tn), lambda i, j, k: (i, j)),
        scratch_shapes=[pltpu.VMEM((tm, tn), F32)],
        compiler_params=_params(("parallel", "parallel", "arbitrary")),
    )(a, w, *_after_arg(after))


def mm_nt(name, dy, w, out_dtype=F32, after=None):
    M, N = dy.shape
    G, K, n = w.shape
    tn = _tile(n, 1024)
    tm = _row_tile(M, 512 if tn > 1024 else 1024)
    tk = _row_tile(K, 1024)
    r = n // tn
    nk = G * r
    return pl.pallas_call(
        functools.partial(_mm_body, nk, ((1,), (1,))), name=name,
        out_shape=jax.ShapeDtypeStruct((M, K), out_dtype),
        grid=(M // tm, K // tk, nk),
        in_specs=[pl.BlockSpec((tm, tn), lambda i, j, k: (i, k)),
                  pl.BlockSpec((None, tk, tn), lambda i, j, k: (k // r, j, k % r))] + _after_spec(after),
        out_specs=pl.BlockSpec((tm, tk), lambda i, j, k: (i, j)),
        scratch_shapes=[pltpu.VMEM((tm, tk), F32)],
        compiler_params=_params(("parallel", "parallel", "arbitrary")),
    )(dy, w, *_after_arg(after))


def mm_tn(name, a, dy, G, out_dtype=F32, after=None):
    M, K = a.shape
    N = dy.shape[1]
    n = N // G
    tn = _tile(n, 1024)
    tk = _row_tile(K, 512 if tn > 1024 else 1024)
    tm = _row_tile(M, 1024)
    r = n // tn
    nk = M // tm
    return pl.pallas_call(
        functools.partial(_mm_body, nk, ((0,), (0,))), name=name,
        out_shape=jax.ShapeDtypeStruct((G, K, n), out_dtype),
        grid=(K // tk, G * r, nk),
        in_specs=[pl.BlockSpec((tm, tk), lambda i, j, k: (k, i)),
                  pl.BlockSpec((tm, tn), lambda i, j, k: (k, j))] + _after_spec(after),
        out_specs=pl.BlockSpec((None, tk, tn), lambda i, j, k: (j // r, i, j % r)),
        scratch_shapes=[pltpu.VMEM((tk, tn), F32)],
        compiler_params=_params(("parallel", "parallel", "arbitrary")),
    )(a, dy, *_after_arg(after))


def rowwise(name, fn, rows, vecs, outs, accs=(), tm=256, after=None):
    rows = [r if isinstance(r, tuple) else (r, r.shape[1], 0) for r in rows]
    n_fn = len(rows) + len(vecs)
    vecs = list(vecs) + _after_arg(after)
    S = rows[0][0].shape[0]
    tm = _row_tile(S, tm)
    n_r, n_v, n_o, n_a = len(rows), len(vecs), len(outs), len(accs)

    def body(*refs):
        ins = [ref[...] for ref in refs[:n_r + n_v]]
        o_refs = refs[n_r + n_v:n_r + n_v + n_o]
        a_refs = refs[n_r + n_v + n_o:]
        res = fn(*ins[:n_fn])
        res = res if isinstance(res, (tuple, list)) else (res,)
        for ref, val in zip(o_refs, res[:n_o]):
            ref[...] = val.astype(ref.dtype)
        if n_a:
            @pl.when(pl.program_id(0) == 0)
            def _():
                for ref in a_refs:
                    ref[...] = jnp.zeros_like(ref)
            for ref, val in zip(a_refs, res[n_o:]):
                ref[...] += val

    in_specs = [pl.BlockSpec((tm, w), functools.partial(lambda cb, i: (i, cb), cb)) for _, w, cb in rows]
    in_specs += [pl.BlockSpec(v.shape, lambda i: (0, 0)) for v in vecs]
    out_specs = [pl.BlockSpec((tm, w), lambda i: (i, 0)) for w, _ in outs]
    out_specs += [pl.BlockSpec(a, lambda i: (0, 0)) for a in accs]
    out_shape = [jax.ShapeDtypeStruct((S, w), dt) for w, dt in outs]
    out_shape += [jax.ShapeDtypeStruct(a, F32) for a in accs]
    res = pl.pallas_call(
        body, name=name, out_shape=out_shape, grid=(S // tm,),
        in_specs=in_specs, out_specs=out_specs,
        compiler_params=_params(("arbitrary",)),
    )(*[r[0] for r in rows], *vecs)
    return res


def _rms(x):
    return x * lax.rsqrt(jnp.mean(x * x, axis=-1, keepdims=True) + EPS)


def _modulate(x, gain, scale, shift):
    return _rms(x) * gain * (1.0 + scale) + shift


def _ada_slices(ada_raw, bias):
    ada = ada_raw + bias
    return [ada[:, i * D_MODEL:(i + 1) * D_MODEL] for i in range(ada.shape[1] // D_MODEL)]


def _norm_wide_heads(y, gain, heads):
    w = y.shape[1] // heads
    return jnp.concatenate([_rms(y[:, h * w:(h + 1) * w]) * gain[:, h * w:(h + 1) * w] for h in range(heads)], axis=1)


def _norm_fox_heads(x, gain):
    outs = []
    for p in range(x.shape[1] // LANES):
        blk = x[:, p * LANES:(p + 1) * LANES]
        low = lax.broadcasted_iota(jnp.int32, blk.shape, 1) < FOX_DH
        sq = blk * blk
        ss_low = jnp.sum(jnp.where(low, sq, 0.0), axis=1, keepdims=True)
        ss_high = jnp.sum(jnp.where(low, 0.0, sq), axis=1, keepdims=True)
        outs.append(blk * lax.rsqrt(jnp.where(low, ss_low, ss_high) * (1.0 / FOX_DH) + EPS))
    return jnp.concatenate(outs, axis=1) * gain


def _silu(x):
    return x * jax.nn.sigmoid(x)


def _log_sigmoid(z):
    return -(jnp.maximum(-z, 0.0) + jnp.log(1.0 + jnp.exp(-jnp.abs(z))))


def _rotate(x, cos, sin, heads, sign):
    w = x.shape[1] // heads
    half = w // 2
    outs = []
    for h in range(heads):
        x1 = x[:, h * w:h * w + half]
        x2 = x[:, h * w + half:(h + 1) * w]
        outs += [x1 * cos - sign * x2 * sin, sign * x1 * sin + x2 * cos]
    return jnp.concatenate(outs, axis=1)


def _vjp(fn, primals, ct):
    return jax.vjp(fn, *primals)[1](ct)


_LOG_GAMMAS = [float(np.log(np.float32(1.0) - np.float32(2.0) ** np.float32(-5.0 - h))) for h in range(RET_HEADS)]


RET_ROWS = 512


def retention(name, q, k, v, reverse):
    (qa, dk, qo), (ka, _, ko), (va, dv, vo) = q, k, v
    S = qa.shape[0]
    C = RET_CHUNK
    rows = _row_tile(S, RET_ROWS)
    nb = S // rows

    def body(q_ref, k_ref, v_ref, o_ref, state):
        h = pl.program_id(0)

        @pl.when(pl.program_id(1) == 0)
        def _():
            state[...] = jnp.zeros_like(state)

        log_g = jnp.float32(_LOG_GAMMAS[RET_HEADS - 1])
        for i in range(RET_HEADS - 2, -1, -1):
            log_g = jnp.where(h == i, jnp.float32(_LOG_GAMMAS[i]), log_g)
        row = lax.broadcasted_iota(jnp.int32, (C, C), 0)
        col = lax.broadcasted_iota(jnp.int32, (C, C), 1)
        rel = (col - row if reverse else row - col).astype(F32)
        decay = jnp.where(rel >= 0, jnp.exp(log_g * jnp.maximum(rel, 0.0)), 0.0)
        j = lax.broadcasted_iota(jnp.int32, (C, 1), 0).astype(F32)
        q_decay = jnp.exp(log_g * (C - j if reverse else j + 1.0))
        k_decay = jnp.exp(log_g * (j if reverse else C - 1.0 - j))
        chunk_decay = jnp.exp(jnp.full((1, 1), log_g * C, F32))

        chunks = range(rows // C)
        for ci in (reversed(chunks) if reverse else chunks):
            rs = slice(ci * C, (ci + 1) * C)
            qc = q_ref[rs, :].astype(BF16)
            kf = k_ref[rs, :].astype(F32)
            vc = v_ref[rs, :].astype(BF16)
            scores = lax.dot_general(qc, kf.astype(BF16), (((1,), (1,)), ((), ())), preferred_element_type=F32) * decay
            intra = jnp.dot(scores.astype(BF16), vc, preferred_element_type=F32)
            cross = jnp.dot(qc, state[...].astype(BF16), preferred_element_type=F32) * q_decay
            o_ref[rs, :] = intra + cross
            upd = lax.dot_general((kf * k_decay).astype(BF16), vc, (((0,), (0,)), ((), ())), preferred_element_type=F32)
            state[...] = state[...] * chunk_decay + upd

    def block(i):
        return nb - 1 - i if reverse else i

    return pl.pallas_call(
        body, name=name, out_shape=jax.ShapeDtypeStruct((S, RET_HEADS * dv), F32),
        grid=(RET_HEADS, nb),
        in_specs=[pl.BlockSpec((rows, dk), lambda h, i: (block(i), qo + h)),
                  pl.BlockSpec((rows, dk), lambda h, i: (block(i), ko + h)),
                  pl.BlockSpec((rows, dv), lambda h, i: (block(i), vo + h))],
        out_specs=pl.BlockSpec((rows, dv), lambda h, i: (block(i), h)),
        scratch_shapes=[pltpu.VMEM((dk, dv), F32)],
        compiler_params=_params(("parallel", "arbitrary")),
    )(qa, ka, va)


FOX_T = 256
N_PAIR = FOX_HEADS // 2
FOX_SCALE = FOX_DH ** -0.5


def _fox_heads(q2):
    low = lax.broadcasted_iota(jnp.int32, (1, LANES), 1) < FOX_DH
    return [(mask, jnp.where(mask, q2 * FOX_SCALE, 0.0).astype(BF16)) for mask in (low, jnp.logical_not(low))]


def _fox_parts(j, t):
    return ([(0, j * t, False)] if j else []) + [(j * t, (j + 1) * t, True)]


def _fox_scores(qa, k_ref, ft_ref, head, lo, hi, diagonal):
    k_blk = k_ref[lo:hi, :].astype(BF16)
    s = lax.dot_general(qa, k_blk, (((1,), (1,)), ((), ())), preferred_element_type=F32) - ft_ref[pl.ds(head, 1), lo:hi]
    if diagonal:
        n = hi - lo
        s = jnp.where(lax.broadcasted_iota(jnp.int32, (n, n), 1) <= lax.broadcasted_iota(jnp.int32, (n, n), 0), s, -jnp.inf)
    return s


def fox_forward(name, qn, kn, kvf, f_cum_t):
    S = qn.shape[0]
    t = _row_tile(S, FOX_T)
    v_block0 = D_MODEL // LANES

    def variant(j, pair, q_ref, k_ref, v_ref, ft_ref, y_ref, lse_ref):
        ys, lses = [], []
        for a, (mask, qa) in enumerate(_fox_heads(q_ref[...])):
            parts = [(lo, hi, _fox_scores(qa, k_ref, ft_ref, 2 * pair + a, lo, hi, dg)) for lo, hi, dg in _fox_parts(j, t)]
            m = functools.reduce(jnp.maximum, [jnp.max(s, axis=1, keepdims=True) for _, _, s in parts])
            l, acc = 0.0, 0.0
            for lo, hi, s in parts:
                e = jnp.exp(s - m)
                l = l + jnp.sum(e, axis=1, keepdims=True)
                acc = acc + jnp.dot(e.astype(BF16), v_ref[lo:hi, :].astype(BF16), preferred_element_type=F32)
            ys.append(acc / l)
            lses.append(m + jnp.log(l))
        low = lax.broadcasted_iota(jnp.int32, (1, LANES), 1) < FOX_DH
        y_ref[...] = jnp.where(low, ys[0], ys[1])
        lse_ref[...] = jnp.where(low, lses[0], lses[1])

    def body(*refs):
        pair, i = pl.program_id(0), pl.program_id(1)
        for j in range(S // t):
            pl.when(i == j)(functools.partial(variant, j, pair, *refs))

    return pl.pallas_call(
        body, name=name,
        out_shape=[jax.ShapeDtypeStruct((S, D_MODEL), F32), jax.ShapeDtypeStruct((S, D_MODEL), F32)],
        grid=(N_PAIR, S // t),
        in_specs=[pl.BlockSpec((t, LANES), lambda p, i: (i, p)),
                  pl.BlockSpec((S, LANES), lambda p, i: (0, p)),
                  pl.BlockSpec((S, LANES), lambda p, i: (0, v_block0 + p)),
                  pl.BlockSpec((LANES, S), lambda p, i: (0, 0))],
        out_specs=[pl.BlockSpec((t, LANES), lambda p, i: (i, p)),
                   pl.BlockSpec((t, LANES), lambda p, i: (i, p))],
        compiler_params=_params(("parallel", "arbitrary")),
    )(qn, kn, kvf, f_cum_t)


def fox_backward(name, qn, kn, kvf, f_cum_t, y, dy, lse):
    S = qn.shape[0]
    t = _row_tile(S, FOX_T)
    v_block0 = D_MODEL // LANES

    def variant(j, pair, q_ref, k_ref, v_ref, ft_ref, y_ref, dy_ref, lse_ref, dq_ref, dk_ref, dv_ref, dfq_ref, dfk_ref):
        y2, dy2, lse2 = y_ref[...], dy_ref[...], lse_ref[...]
        lane = lax.broadcasted_iota(jnp.int32, (t, LANES), 1)
        dqs, dfq = [], jnp.zeros((t, LANES), F32)
        for a, (mask, qa) in enumerate(_fox_heads(q_ref[...])):
            lse_a = jnp.max(jnp.where(mask, lse2, -jnp.inf), axis=1, keepdims=True)
            dy_a = jnp.where(mask, dy2, 0.0)
            delta = jnp.sum(dy_a * y2, axis=1, keepdims=True)
            dy_b = dy_a.astype(BF16)
            dq, row_sum = 0.0, 0.0
            for lo, hi, dg in _fox_parts(j, t):
                p = jnp.exp(_fox_scores(qa, k_ref, ft_ref, 2 * pair + a, lo, hi, dg) - lse_a)
                dp = lax.dot_general(dy_b, v_ref[lo:hi, :].astype(BF16), (((1,), (1,)), ((), ())), preferred_element_type=F32)
                ds = p * (dp - delta)
                row_sum = row_sum + jnp.sum(ds, axis=1, keepdims=True)
                dfk_ref[pl.ds(a, 1), lo:hi] += -jnp.sum(ds, axis=0, keepdims=True)
                ds_b = ds.astype(BF16)
                dq = dq + jnp.dot(ds_b, k_ref[lo:hi, :].astype(BF16), preferred_element_type=F32)
                dk_ref[lo:hi, :] += lax.dot_general(ds_b, qa, (((0,), (0,)), ((), ())), preferred_element_type=F32)
                dv_ref[lo:hi, :] += lax.dot_general(p.astype(BF16), dy_b, (((0,), (0,)), ((), ())), preferred_element_type=F32)
            dqs.append(dq * FOX_SCALE)
            dfq = dfq + jnp.where(lane == 2 * pair + a, row_sum, 0.0)
        low = lax.broadcasted_iota(jnp.int32, (1, LANES), 1) < FOX_DH
        dq_ref[...] = jnp.where(low, dqs[0], dqs[1])
        dfq_ref[...] = dfq

    def body(*refs):
        pair, i = pl.program_id(0), pl.program_id(1)
        dk_ref, dv_ref, dfk_ref = refs[8], refs[9], refs[11]

        @pl.when(i == 0)
        def _():
            dk_ref[...] = jnp.zeros_like(dk_ref)
            dv_ref[...] = jnp.zeros_like(dv_ref)
            dfk_ref[...] = jnp.zeros_like(dfk_ref)

        for j in range(S // t):
            pl.when(i == j)(functools.partial(variant, j, pair, *refs))

    row_blk = pl.BlockSpec((t, LANES), lambda p, i: (i, p))
    col_blk = pl.BlockSpec((S, LANES), lambda p, i: (0, p))
    return pl.pallas_call(
        body, name=name,
        out_shape=[jax.ShapeDtypeStruct((S, D_MODEL), F32)] * 3
        + [jax.ShapeDtypeStruct((N_PAIR, S, LANES), F32), jax.ShapeDtypeStruct((N_PAIR, 8, S), F32)],
        grid=(N_PAIR, S // t),
        in_specs=[row_blk, col_blk,
                  pl.BlockSpec((S, LANES), lambda p, i: (0, v_block0 + p)),
                  pl.BlockSpec((LANES, S), lambda p, i: (0, 0)),
                  row_blk, row_blk, row_blk],
        out_specs=[row_blk, col_blk, col_blk,
                   pl.BlockSpec((None, t, LANES), lambda p, i: (p, i, 0)),
                   pl.BlockSpec((None, 8, S), lambda p, i: (p, 0, 0))],
        compiler_params=_params(("parallel", "arbitrary")),
    )(qn, kn, kvf, f_cum_t, y, dy, lse)


def cumsum_rows(name, x, reverse):
    S = x.shape[0]
    C = LANES
    nc = S // C

    def body(x_ref, o_ref):
        row = lax.broadcasted_iota(jnp.int32, (C, C), 0)
        col = lax.broadcasted_iota(jnp.int32, (C, C), 1)
        tri = jnp.where(col >= row if reverse else col <= row, 1.0, 0.0).astype(F32)
        carry = jnp.zeros((1, LANES), F32)
        for i in (range(nc - 1, -1, -1) if reverse else range(nc)):
            blk = x_ref[i * C:(i + 1) * C, :]
            loc = jnp.dot(tri, blk, preferred_element_type=F32, precision=lax.Precision.HIGHEST)
            o_ref[i * C:(i + 1) * C, :] = loc + carry
            carry = carry + (loc[0:1, :] if reverse else loc[C - 1:C, :])

    return pl.pallas_call(body, name=name, out_shape=jax.ShapeDtypeStruct((S, LANES), F32),
                          compiler_params=_params())(x)


def adamw(name, parts, w, m, v):
    P, R, C = parts.shape
    tr = _row_tile(R, 256)

    def body(p_ref, w_ref, m_ref, v_ref, g_out, d_out, m_out, v_out):
        g = p_ref[0].astype(F32)
        for i in range(1, P):
            g = g + p_ref[i].astype(F32)
        m2 = ADAM_B1 * m_ref[...] + (1.0 - ADAM_B1) * g
        v2 = ADAM_B2 * v_ref[...] + (1.0 - ADAM_B2) * jnp.square(g)
        m_hat = m2 / (1.0 - ADAM_B1 ** ADAM_STEP)
        v_hat = v2 / (1.0 - ADAM_B2 ** ADAM_STEP)
        g_out[...] = g
        d_out[...] = -ADAM_LR * (m_hat / (jnp.sqrt(v_hat) + ADAM_EPS) + ADAM_WD * w_ref[...])
        m_out[...] = m2
        v_out[...] = v2

    blk = pl.BlockSpec((tr, C), lambda i: (i, 0))
    return pl.pallas_call(
        body, name=name, out_shape=[jax.ShapeDtypeStruct((R, C), F32)] * 4, grid=(R // tr,),
        in_specs=[pl.BlockSpec((P, tr, C), lambda i: (0, i, 0)), blk, blk, blk],
        out_specs=[blk] * 4, compiler_params=_params(("parallel",)),
    )(parts, w, m, v)


def kernel(x, c, positions, norm_mix_gain, norm_mlp_gain, w_ada, b_ada, w_mlp_in, w_mlp_out, ret_w_in, ret_norm_gain, ret_w_out, kv_norm_gain, kv_w_ada, kv_b_ada, kv_w, forget_bias, k_norm_gain, fox_w_in, q_norm_gain, fox_w_out, loss_target, m_norm_mix_gain, m_norm_mlp_gain, m_w_ada, m_b_ada, m_w_mlp_in, m_w_mlp_out, m_ret_w_in, m_ret_norm_gain, m_ret_w_out, m_kv_norm_gain, m_kv_w_ada, m_kv_b_ada, m_kv_w, m_forget_bias, m_k_norm_gain, m_fox_w_in, m_q_norm_gain, m_fox_w_out, v_norm_mix_gain, v_norm_mlp_gain, v_w_ada, v_b_ada, v_w_mlp_in, v_w_mlp_out, v_ret_w_in, v_ret_norm_gain, v_ret_w_out, v_kv_norm_gain, v_kv_w_ada, v_kv_b_ada, v_kv_w, v_forget_bias, v_k_norm_gain, v_fox_w_in, v_q_norm_gain, v_fox_w_out):
    D = D_MODEL
    S = x.shape[1]
    x0 = x.reshape(S, D)
    target = loss_target.reshape(S, D)
    me = 4 * lax.axis_index("x") + 2 * lax.axis_index("y") + lax.axis_index("c")
    n_ada = w_ada.shape[2]
    n_kvada = kv_w_ada.shape[1]
    n_kv = kv_w.shape[1]

    c_all, ret_gain = all_gather("gather_c", [c.reshape(1, D), ret_norm_gain.reshape(RET_HEADS, -1)])
    ret_gain = jnp.transpose(ret_gain, (1, 0, 2)).reshape(1, RET_HEADS * RET_V)
    c_act = rowwise("silu_c", _silu, [c_all.reshape(N_DEV, D)], [], [(D, F32)])[0]
    w_ada_cat = jnp.concatenate([w_ada[0], w_ada[1], kv_w_ada], axis=1).astype(BF16)[None]
    n_cat = 2 * n_ada + n_kvada
    ada_part = mm_nn("ada_proj", c_act, w_ada_cat)
    ada_mine = all_to_all("ada_rows", [ada_part.reshape(N_DEV, 1, n_cat)])[0][:, 0]
    ada_raw = [ada_mine[:, l * n_ada:(l + 1) * n_ada].reshape(1, 6 * D) for l in range(2)]
    kvada_raw = ada_mine[:, 2 * n_ada:].reshape(1, 2 * D)
    kv_bias = kv_b_ada.reshape(1, 2 * D)
    kv_gain = kv_norm_gain.reshape(1, D)
    fb = jnp.pad(forget_bias.reshape(1, FOX_HEADS), ((0, 0), (0, LANES - FOX_HEADS)))
    k_gain = jnp.tile(k_norm_gain.reshape(1, FOX_DH), (1, FOX_HEADS))
    q_gain = jnp.tile(q_norm_gain.reshape(1, FOX_DH), (1, FOX_HEADS))

    w_names = ["ret_in", "ret_out", "mlp_in0", "mlp_out0", "kv", "fox_in", "fox_out", "mlp_in1", "mlp_out1"]
    w_handles, token = exchange_start("gather_weights_start", [
        ret_w_in[0].astype(BF16), ret_w_out[0].astype(BF16), w_mlp_in[0].astype(BF16), w_mlp_out[0].astype(BF16),
        kv_w.astype(BF16), fox_w_in[0].astype(BF16), fox_w_out[0].astype(BF16), w_mlp_in[1].astype(BF16),
        w_mlp_out[1].astype(BF16)], scatter=False, after=(ada_mine, ret_gain))
    w_handles = dict(zip(w_names, w_handles))

    def weight(name, after):
        return exchange_wait("gather_wait_" + name, [w_handles[name]], after, scatter=False)[0]

    pos = positions.reshape(S, 1).astype(F32)
    half = RET_QK // 2
    inv_freq = jnp.asarray((ROPE_BASE ** (-np.arange(half, dtype=np.float32) / half)).reshape(1, half), F32)

    def angles(p, f):
        ang = p * f
        return jnp.cos(ang), jnp.sin(ang)

    cos, sin = rowwise("rope_table", angles, [pos], [inv_freq], [(half, F32), (half, F32)])

    def mod_mix(layer):
        def fn(xb, ada, bias, gain):
            sh, sc = _ada_slices(ada, bias)[:2]
            return _modulate(xb, gain[layer:layer + 1], sc, sh)
        return fn

    def mod_mlp(layer):
        def fn(xb, ada, bias, gain):
            sh, sc = _ada_slices(ada, bias)[3:5]
            return _modulate(xb, gain[layer:layer + 1], sc, sh)
        return fn

    h1_0 = rowwise("mod_mix0", mod_mix(0), [x0], [ada_raw[0], b_ada[0:1], norm_mix_gain], [(D, BF16)], after=token)[0]
    W_ret_in = weight("ret_in", h1_0)
    proj = mm_nn("ret_proj", h1_0, W_ret_in)

    def rope_fwd(qb, kb, cs, sn):
        return _rotate(qb, cs, sn, RET_HEADS, 1.0), _rotate(kb, cs, sn, RET_HEADS, 1.0) * (RET_QK ** -0.5)

    q_rot, k_rot = rowwise("rope", rope_fwd, [(proj, D, 0), (proj, D, 1), cos, sin], [], [(D, F32), (D, F32)])
    v_ret = (proj, RET_V, (2 * D) // RET_V)
    y_ret = retention("ret_fwd", (q_rot, RET_QK, 0), (k_rot, RET_QK, 0), v_ret, reverse=False)

    def ret_gate(yb, gb, gain):
        return _silu(gb) * _norm_wide_heads(yb, gain, RET_HEADS)

    mixin0 = rowwise("ret_gate", ret_gate, [y_ret, (proj, 2 * D, 2)], [ret_gain], [(2 * D, BF16)])[0]
    W_ret_out = weight("ret_out", mixin0).reshape(1, 2 * D, D)
    mix0 = mm_nn("ret_out", mixin0, W_ret_out)

    def residual_mod(layer, slot):
        def fn(xb, bb, ada, bias, gain):
            s = _ada_slices(ada, bias)
            xn = xb + s[2] * bb
            return xn, _modulate(xn, gain[layer:layer + 1], s[4], s[3])
        return fn

    x1, h2_0 = rowwise("res_mix0", residual_mod(0, 0), [x0, mix0], [ada_raw[0], b_ada[0:1], norm_mlp_gain],
                       [(D, F32), (D, BF16)])

    W_mlp_in, W_mlp_out = {}, {}

    def mlp_forward(tag, h2, layer):
        W_mlp_in[layer] = weight("mlp_in" + tag, h2)
        u = mm_nn("mlp_in" + tag, h2, W_mlp_in[layer])
        act = rowwise("relu2" + tag, lambda ub: jnp.square(jnp.maximum(ub, 0.0)), [u], [], [(4 * D, BF16)])[0]
        W_mlp_out[layer] = weight("mlp_out" + tag, act).reshape(1, 4 * D, D)
        return u, act, mm_nn("mlp_out" + tag, act, W_mlp_out[layer])

    u0, act0, mlp0 = mlp_forward("0", h2_0, 0)

    def res_mlp0(xb, bb, ada0, bias0, ada1, bias1, kva, kvb, gain_mix, gain_kv):
        xn = xb + _ada_slices(ada0, bias0)[5] * bb
        s1 = _ada_slices(ada1, bias1)
        kv_shift, kv_scale = _ada_slices(kva, kvb)
        return xn, _modulate(xn, gain_kv, kv_scale, kv_shift), _modulate(xn, gain_mix[1:2], s1[1], s1[0])

    x2, h_kv, h1_1 = rowwise("res_mlp0", res_mlp0, [x1, mlp0],
                             [ada_raw[0], b_ada[0:1], ada_raw[1], b_ada[1:2], kvada_raw, kv_bias, norm_mix_gain, kv_gain],
                             [(D, F32), (D, BF16), (D, BF16)])

    kv_full = jnp.transpose(weight("kv", h_kv), (1, 0, 2)).reshape(D, N_DEV * n_kv)
    W_kv = jnp.pad(kv_full, ((0, 0), (0, KV_PAD - N_DEV * n_kv)))[None]
    kvf = mm_nn("kv_proj", h_kv, W_kv)

    def kv_post(kb, fblk, kg, bias):
        head = lax.broadcasted_iota(jnp.int32, fblk.shape, 1) < FOX_HEADS
        return _norm_fox_heads(kb, kg), jnp.where(head, _log_sigmoid(fblk + bias), 0.0)

    kn, log_f = rowwise("kv_post", kv_post, [(kvf, D, 0), (kvf, LANES, 2 * D // LANES)], [k_gain, fb],
                        [(D, F32), (LANES, F32)])
    f_cum = cumsum_rows("f_cumsum", log_f, reverse=False)
    f_cum_t = f_cum.T

    W_fox_in = weight("fox_in", kvf)
    qo = mm_nn("fox_proj", h1_1, W_fox_in)
    qn = rowwise("q_norm", _norm_fox_heads, [(qo, D, 0)], [q_gain], [(D, F32)])[0]
    y_att, lse = fox_forward("fox_fwd", qn, kn, kvf, f_cum_t)
    mixin1 = rowwise("fox_gate", lambda ob, yb: jax.nn.sigmoid(ob) * yb, [(qo, D, 1), y_att], [], [(D, BF16)])[0]
    W_fox_out = weight("fox_out", mixin1).reshape(1, D, D)
    mix1 = mm_nn("fox_out", mixin1, W_fox_out)
    x3, h2_1 = rowwise("res_mix1", residual_mod(1, 0), [x2, mix1], [ada_raw[1], b_ada[1:2], norm_mlp_gain],
                       [(D, F32), (D, BF16)])
    u1, act1, mlp1 = mlp_forward("1", h2_1, 1)

    def loss_head(xb, bb, tb, ada, bias):
        g2 = _ada_slices(ada, bias)[5]
        err = xb + g2 * bb - tb
        dx = err * (1.0 / D)
        loss = 0.5 * jnp.sum(jnp.sum(err * err, axis=1, keepdims=True) * (1.0 / D), axis=0, keepdims=True)
        return dx, (dx * g2), jnp.broadcast_to(loss, (1, LANES)), jnp.sum(dx * bb, axis=0, keepdims=True)

    dx4, dmlp1, loss_acc, dg2_1 = rowwise("loss_head", loss_head, [x3, mlp1, target], [ada_raw[1], b_ada[1:2]],
                                          [(D, F32), (D, BF16)], [(1, LANES), (1, D)])
    loss = lax.psum(loss_acc[0, 0], ("x", "y", "c"))

    def mlp_backward(tag, dmlp, act, u, h2, layer):
        d_act = mm_nt("mlp_out_dx" + tag, dmlp, W_mlp_out[layer])
        gw_out = mm_tn("mlp_out_dw" + tag, act, dmlp, 1, BF16).reshape(N_DEV, -1, D)
        du = rowwise("relu2_bwd" + tag, lambda db, ub: db * (2.0 * jnp.maximum(ub, 0.0)), [d_act, u], [], [(4 * D, BF16)])[0]
        dh2 = mm_nt("mlp_in_dx" + tag, du, W_mlp_in[layer])
        gw_in = mm_tn("mlp_in_dw" + tag, h2, du, N_DEV, BF16)
        return dh2, gw_in, gw_out

    def mod_backward(layer, slots, gate_slot):
        def fn(xb, dhb, dresb, branchb, ada, bias, gain):
            s = _ada_slices(ada, bias)
            g = gain[layer:layer + 1]
            dx, dgain, dsc, dsh = _vjp(_modulate, (xb, g, s[slots[1]], s[slots[0]]), dhb)
            dx = dx + dresb
            d_branch = dx * s[gate_slot]
            return dx, d_branch, dgain, dsc, dsh, jnp.sum(dx * branchb, axis=0, keepdims=True)
        return fn

    vec = (1, D)
    dh2_1, gw_mlp_in1, gw_mlp_out1 = mlp_backward("1", dmlp1, act1, u1, h2_1, 1)
    dx3, dmix1, dgain_mlp1, dsc2_1, dsh2_1, dg1_1 = rowwise(
        "mod_mlp1_bwd", mod_backward(1, (3, 4), 2), [x3, dh2_1, dx4, mix1], [ada_raw[1], b_ada[1:2], norm_mlp_gain],
        [(D, F32), (D, BF16)], [vec] * 4)
    dmixin1 = mm_nt("fox_out_dx", dmix1, W_fox_out)
    gw_fox_out = mm_tn("fox_out_dw", mixin1, dmix1, 1, BF16).reshape(N_DEV, -1, D)

    def fox_gate_bwd(db, ob, yb):
        sg = jax.nn.sigmoid(ob)
        return db * sg, db * yb * sg * (1.0 - sg)

    dy_att, d_og = rowwise("fox_gate_bwd", fox_gate_bwd, [dmixin1, (qo, D, 1), y_att], [], [(D, F32), (D, F32)])
    dqn, dkn, dv_att, dfq, dfk = fox_backward("fox_bwd", qn, kn, kvf, f_cum_t, y_att, dy_att, lse)

    def q_norm_bwd(qb, db, ogb, gain):
        dq, dgain = _vjp(_norm_fox_heads, (qb, gain), db)
        return jnp.concatenate([dq, ogb], axis=1), dgain

    dqo, dq_gain = rowwise("q_norm_bwd", q_norm_bwd, [(qo, D, 0), dqn, d_og], [q_gain], [(2 * D, BF16)], [vec])
    dh1_1 = mm_nt("fox_proj_dx", dqo, W_fox_in)
    gw_fox_in = mm_tn("fox_proj_dw", h1_1, dqo, N_DEV, BF16)

    dfk_rows = jnp.pad(dfk[:, :2, :].reshape(FOX_HEADS, S).T, ((0, 0), (0, LANES - FOX_HEADS)))

    def df_total(*blks):
        tot = blks[0]
        for b in blks[1:]:
            tot = tot + b
        return tot

    d_fcum = rowwise("df_sum", df_total, [dfk_rows] + [dfq[p] for p in range(N_PAIR)], [], [(LANES, F32)])[0]
    d_logf = cumsum_rows("df_cumsum", d_fcum, reverse=True)

    def kv_post_bwd(kb, fblk, dkb, dvb, dlf, kg, bias):
        dk, dgain = _vjp(_norm_fox_heads, (kb, kg), dkb)
        df = dlf * (1.0 / (1.0 + jnp.exp(fblk + bias)))
        return jnp.concatenate([dk, dvb, df], axis=1), dgain, jnp.sum(df, axis=0, keepdims=True)

    dkvf, dk_gain, dfb = rowwise("kv_post_bwd", kv_post_bwd,
                                 [(kvf, D, 0), (kvf, LANES, 2 * D // LANES), dkn, dv_att, d_logf], [k_gain, fb],
                                 [(KV_PAD, BF16)], [vec, (1, LANES)])
    dh_kv = mm_nt("kv_proj_dx", dkvf, W_kv)
    gw_kv = mm_tn("kv_proj_dw", h_kv, dkvf, 1, BF16)[0, :, :N_DEV * n_kv]
    gw_kv = jnp.transpose(gw_kv.reshape(D, N_DEV, n_kv), (1, 0, 2))
    scat_a, token_a = exchange_start("scatter_start_a", [gw_mlp_in1, gw_mlp_out1, gw_fox_out, gw_fox_in, gw_kv], scatter=True)

    def x2_bwd(xb, dh1b, dhkb, dresb, branchb, ada0, bias0, ada1, bias1, kva, kvb, gain_mix, gain_kv):
        s1 = _ada_slices(ada1, bias1)
        kv_shift, kv_scale = _ada_slices(kva, kvb)
        dxa, dgain_mix, dsc1, dsh1 = _vjp(_modulate, (xb, gain_mix[1:2], s1[1], s1[0]), dh1b)
        dxb, dgain_kv, dkv_scale, dkv_shift = _vjp(_modulate, (xb, gain_kv, kv_scale, kv_shift), dhkb)
        dx = dresb + dxa + dxb
        g2 = _ada_slices(ada0, bias0)[5]
        return (dx, dx * g2, dgain_mix, dsc1, dsh1, dgain_kv, dkv_scale, dkv_shift,
                jnp.sum(dx * branchb, axis=0, keepdims=True))

    (dx2, dmlp0, dgain_mix1, dsc1_1, dsh1_1, dgain_kv, dkv_scale, dkv_shift, dg2_0) = rowwise(
        "x2_bwd", x2_bwd, [x2, dh1_1, dh_kv, dx3, mlp0],
        [ada_raw[0], b_ada[0:1], ada_raw[1], b_ada[1:2], kvada_raw, kv_bias, norm_mix_gain, kv_gain],
        [(D, F32), (D, BF16)], [vec] * 7, after=token_a)

    dh2_0, gw_mlp_in0, gw_mlp_out0 = mlp_backward("0", dmlp0, act0, u0, h2_0, 0)
    dx1, dmix0, dgain_mlp0, dsc2_0, dsh2_0, dg1_0 = rowwise(
        "mod_mlp0_bwd", mod_backward(0, (3, 4), 2), [x1, dh2_0, dx2, mix0], [ada_raw[0], b_ada[0:1], norm_mlp_gain],
        [(D, F32), (D, BF16)], [vec] * 4)
    dmixin0 = mm_nt("ret_out_dx", dmix0, W_ret_out)
    gw_ret_out = mm_tn("ret_out_dw", mixin0, dmix0, 1, BF16).reshape(N_DEV, -1, D)
    scat_b, token_b = exchange_start("scatter_start_b", [gw_mlp_in0, gw_mlp_out0, gw_ret_out], scatter=True)

    def ret_gate_bwd(db, yb, gb, gain):
        dy, dg, dgain = _vjp(lambda y_, g_, gn_: ret_gate(y_, g_, gn_), (yb, gb, gain), db)
        return dy, dg, dgain

    dy_ret, dgate, dret_gain = rowwise("ret_gate_bwd", ret_gate_bwd, [dmixin0, y_ret, (proj, 2 * D, 2)], [ret_gain],
                                       [(2 * D, F32), (2 * D, BF16)], [(1, 2 * D)], after=token_b)
    dy_h = (dy_ret, RET_V, 0)
    dq_rot = retention("ret_dq", dy_h, v_ret, (k_rot, RET_QK, 0), reverse=False)
    dk_rot = retention("ret_dk", v_ret, dy_h, (q_rot, RET_QK, 0), reverse=True)
    dv_ret = retention("ret_dv", (k_rot, RET_QK, 0), (q_rot, RET_QK, 0), dy_h, reverse=True)

    def rope_bwd(dqb, dkb, dvb, dgb, cs, sn):
        dq = _rotate(dqb, cs, sn, RET_HEADS, -1.0)
        dk = _rotate(dkb, cs, sn, RET_HEADS, -1.0) * (RET_QK ** -0.5)
        return jnp.concatenate([dq, dk, dvb, dgb.astype(F32)], axis=1)

    dproj = rowwise("rope_bwd", rope_bwd, [dq_rot, dk_rot, dv_ret, dgate, cos, sin], [], [(6 * D, BF16)])[0]
    dh1_0 = mm_nt("ret_proj_dx", dproj, W_ret_in)

    def x0_bwd(xb, dhb, dresb, ada, bias, gain):
        s = _ada_slices(ada, bias)
        dx, dgain, dsc, dsh = _vjp(_modulate, (xb, gain[0:1], s[1], s[0]), dhb)
        return dx + dresb, dgain, dsc, dsh

    grad_x, dgain_mix0, dsc1_0, dsh1_0 = rowwise("x0_bwd", x0_bwd, [x0, dh1_0, dx1],
                                                 [ada_raw[0], b_ada[0:1], norm_mix_gain], [(D, F32)], [vec] * 3)

    small = jnp.concatenate([
        dsh1_0, dsc1_0, dg1_0, dsh2_0, dsc2_0, dg2_0,
        dsh1_1, dsc1_1, dg1_1, dsh2_1, dsc2_1, dg2_1,
        dkv_shift, dkv_scale,
        dgain_mix0, dgain_mix1, dgain_mlp0, dgain_mlp1, dgain_kv,
        dret_gain,
        dq_gain.reshape(FOX_HEADS, FOX_DH).sum(axis=0).reshape(1, FOX_DH),
        dk_gain.reshape(FOX_HEADS, FOX_DH).sum(axis=0).reshape(1, FOX_DH),
        dfb,
    ], axis=1)
    small_all = all_gather("gather_small", [small])[0]
    gw_ret_in = mm_tn("ret_proj_dw", h1_0, dproj, N_DEV, BF16, after=small_all)
    scat_c, token_c = exchange_start("scatter_start_c", [gw_ret_in], scatter=True)
    o_ada = 14 * D
    d_ada = small_all[:, 0, :o_ada]
    d_cat = jnp.concatenate([
        lax.dynamic_slice_in_dim(d_ada[:, 0:6 * D], me * n_ada, n_ada, axis=1),
        lax.dynamic_slice_in_dim(d_ada[:, 6 * D:12 * D], me * n_ada, n_ada, axis=1),
        lax.dynamic_slice_in_dim(d_ada[:, 12 * D:14 * D], me * n_kvada, n_kvada, axis=1)], axis=1)
    gw_ada_cat = mm_tn("ada_dw", c_act, d_cat, 1, F32, after=token_c)[0]

    results = {}

    def update(name, parts, w, m, v):
        shape = w.shape
        R = int(np.prod(shape[:-1])) if len(shape) > 1 else 1
        C = shape[-1]
        g, d, m2, v2 = adamw("adamw_" + name, parts.reshape(parts.shape[0], R, C), w.reshape(R, C), m.reshape(R, C),
                             v.reshape(R, C))
        results[name] = tuple(t.reshape(shape) for t in (g, d, m2, v2))

    def small_parts(lo, width):
        return small_all[:, :, lo:lo + width]

    update("norm_mix_gain", jnp.concatenate([small_parts(o_ada, D), small_parts(o_ada + D, D)], axis=1),
           norm_mix_gain, m_norm_mix_gain, v_norm_mix_gain)
    update("norm_mlp_gain", jnp.concatenate([small_parts(o_ada + 2 * D, D), small_parts(o_ada + 3 * D, D)], axis=1),
           norm_mlp_gain, m_norm_mlp_gain, v_norm_mlp_gain)
    update("w_ada", jnp.stack([gw_ada_cat[:, :n_ada], gw_ada_cat[:, n_ada:2 * n_ada]])[None].reshape(1, 2 * D, n_ada),
           w_ada, m_w_ada, v_w_ada)
    update("b_ada", jnp.concatenate([small_parts(0, 6 * D), small_parts(6 * D, 6 * D)], axis=1), b_ada, m_b_ada, v_b_ada)
    o_ret = o_ada + 5 * D
    n_rg = ret_norm_gain.shape[2]
    ret_gain_parts = small_parts(o_ret, 2 * D).reshape(N_DEV, RET_HEADS, RET_V)
    ret_gain_parts = lax.dynamic_slice_in_dim(ret_gain_parts, me * n_rg, n_rg, axis=2)
    update("ret_norm_gain", ret_gain_parts, ret_norm_gain, m_ret_norm_gain, v_ret_norm_gain)
    update("kv_norm_gain", small_parts(o_ada + 4 * D, D), kv_norm_gain, m_kv_norm_gain, v_kv_norm_gain)
    update("kv_w_ada", gw_ada_cat[None, :, 2 * n_ada:], kv_w_ada, m_kv_w_ada, v_kv_w_ada)
    update("kv_b_ada", small_parts(12 * D, 2 * D), kv_b_ada, m_kv_b_ada, v_kv_b_ada)
    o_q = o_ret + 2 * D
    update("forget_bias", small_parts(o_q + 2 * FOX_DH, FOX_HEADS), forget_bias, m_forget_bias, v_forget_bias)
    update("k_norm_gain", small_parts(o_q + FOX_DH, FOX_DH), k_norm_gain, m_k_norm_gain, v_k_norm_gain)
    update("q_norm_gain", small_parts(o_q, FOX_DH), q_norm_gain, m_q_norm_gain, v_q_norm_gain)

    r_mlp_in1, r_mlp_out1, r_fox_out, r_fox_in, r_kv = exchange_wait(
        "scatter_wait_a", scat_a, results["w_ada"][1], scatter=True)
    r_mlp_in0, r_mlp_out0, r_ret_out = exchange_wait("scatter_wait_b", scat_b, r_kv, scatter=True)
    update("kv_w", r_kv, kv_w, m_kv_w, v_kv_w)
    update("fox_w_in", r_fox_in, fox_w_in, m_fox_w_in, v_fox_w_in)
    update("fox_w_out", r_fox_out, fox_w_out, m_fox_w_out, v_fox_w_out)
    update("ret_w_out", r_ret_out, ret_w_out, m_ret_w_out, v_ret_w_out)
    update("w_mlp_in", jnp.concatenate([r_mlp_in0, r_mlp_in1], axis=1), w_mlp_in, m_w_mlp_in, v_w_mlp_in)
    update("w_mlp_out", jnp.concatenate([r_mlp_out0, r_mlp_out1], axis=1), w_mlp_out, m_w_mlp_out, v_w_mlp_out)
    r_ret_in = exchange_wait("scatter_wait_c", scat_c, results["w_mlp_out"][1], scatter=True)[0]
    update("ret_w_in", r_ret_in, ret_w_in, m_ret_w_in, v_ret_w_in)

    order = ["norm_mix_gain", "norm_mlp_gain", "w_ada", "b_ada", "w_mlp_in", "w_mlp_out", "ret_w_in", "ret_norm_gain",
             "ret_w_out", "kv_norm_gain", "kv_w_ada", "kv_b_ada", "kv_w", "forget_bias", "k_norm_gain", "fox_w_in",
             "q_norm_gain", "fox_w_out"]
    out = [loss, grad_x.reshape(x.shape)]
    for slot in range(4):
        out += [results[n][slot] for n in order]
    return tuple(out)
```

```python
import functools
import math

import numpy as np
import jax
import jax.numpy as jnp
from jax import lax
from jax.experimental import pallas as pl
from jax.experimental.pallas import tpu as pltpu

F32 = jnp.float32
BF16 = jnp.bfloat16

N_DEV = 8
D_MODEL = 1024
RET_HEADS = 4
RET_QK = D_MODEL // RET_HEADS
RET_V = 2 * D_MODEL // RET_HEADS
RET_CHUNK = 128
ROPE_BASE = 10000.0
FOX_HEADS = 16
FOX_DH = D_MODEL // FOX_HEADS
EPS = 1e-6
LANES = 128
KV_PAD = 2 * D_MODEL + LANES

ADAM_LR = 0.001
ADAM_B1 = 0.9
ADAM_B2 = 0.999
ADAM_EPS = 1e-08
ADAM_WD = 0.01
ADAM_STEP = 10

VMEM_LIMIT_BYTES = 56 * 1024 * 1024


def _params(sem=None):
    return pltpu.CompilerParams(dimension_semantics=sem, vmem_limit_bytes=VMEM_LIMIT_BYTES)


def _me():
    return lax.axis_index("x"), lax.axis_index("y"), lax.axis_index("c")


def _peer(k):
    x, y, c = _me()
    return (1 - x if k & 4 else x, 1 - y if k & 2 else y, 1 - c if k & 1 else c)


def _peer_index(k):
    px, py, pc = _peer(k)
    return 4 * px + 2 * py + pc


def _exchange(name, xs, scatter):
    n = len(xs)

    def body(*refs):
        x_refs, o_refs = refs[:n], refs[n:2 * n]
        send_sems, recv_sems, local_sems = refs[2 * n:]
        x, y, c = _me()
        me = 4 * x + 2 * y + c
        local = []
        for i in range(n):
            src = x_refs[i].at[me] if scatter else x_refs[i]
            cp = pltpu.make_async_copy(src, o_refs[i].at[me], local_sems.at[i])
            cp.start()
            local.append(cp)
        remote = []
        for k in range(1, N_DEV):
            for i in range(n):
                src = x_refs[i].at[_peer_index(k)] if scatter else x_refs[i]
                cp = pltpu.make_async_remote_copy(
                    src_ref=src, dst_ref=o_refs[i].at[me],
                    send_sem=send_sems.at[(k - 1) * n + i], recv_sem=recv_sems.at[(k - 1) * n + i],
                    device_id=_peer(k), device_id_type=pl.DeviceIdType.MESH)
                cp.start()
                remote.append(cp)
        for cp in remote:
            cp.wait()
        for cp in local:
            cp.wait()

    out_shape = [jax.ShapeDtypeStruct(x.shape if scatter else (N_DEV,) + x.shape, x.dtype) for x in xs]
    any_spec = pl.BlockSpec(memory_space=pl.ANY)
    return pl.pallas_call(
        body, name=name, out_shape=out_shape,
        in_specs=[any_spec] * n, out_specs=[any_spec] * n,
        scratch_shapes=[pltpu.SemaphoreType.DMA(((N_DEV - 1) * n,)),
                        pltpu.SemaphoreType.DMA(((N_DEV - 1) * n,)),
                        pltpu.SemaphoreType.DMA((n,))],
    )(*xs)


def all_gather(name, xs):
    return _exchange(name, xs, scatter=False)


def all_to_all(name, xs):
    return _exchange(name, xs, scatter=True)


_HBM = pl.BlockSpec(memory_space=pltpu.HBM)
_SEM = pl.BlockSpec(memory_space=pltpu.SEMAPHORE)
_EFFECT = pltpu.SideEffectType.DATAFLOW_SIDE_EFFECTING


def _landing(block, shape, dtype):
    me = 4 * lax.axis_index("x") + 2 * lax.axis_index("y") + lax.axis_index("c")
    start = (me,) + (0,) * (len(shape) - 1)
    return lax.dynamic_update_slice(lax.empty(shape, dtype), block[None], start)


def exchange_start(name, xs, scatter, after=()):
    n, m = len(xs), len(after)
    if scatter:
        me = 4 * lax.axis_index("x") + 2 * lax.axis_index("y") + lax.axis_index("c")
        lands = [_landing(lax.dynamic_index_in_dim(x, me, 0, keepdims=False), x.shape, x.dtype) for x in xs]
    else:
        lands = [_landing(x, (N_DEV,) + x.shape, x.dtype) for x in xs]

    def body(*refs):
        x_refs, land_refs = refs[:n], refs[n:2 * n]
        send_sems, recv_sems = refs[2 * n + m:3 * n + m], refs[3 * n + m:4 * n + m]
        token = refs[6 * n + m]
        x, y, c = _me()
        me = 4 * x + 2 * y + c
        for i in range(n):
            for k in range(1, N_DEV):
                src = x_refs[i].at[_peer_index(k)] if scatter else x_refs[i]
                pltpu.make_async_remote_copy(
                    src_ref=src, dst_ref=land_refs[i].at[me],
                    send_sem=send_sems[i].at[k - 1], recv_sem=recv_sems[i].at[k - 1],
                    device_id=_peer(k), device_id_type=pl.DeviceIdType.MESH).start()
        token[...] = jnp.zeros_like(token)

    sems = [pltpu.SemaphoreType.DMA((N_DEV - 1,))] * (2 * n)
    thru = [pltpu.HBM(a.shape, a.dtype) for a in list(xs) + lands]
    res = pl.pallas_call(
        body, name=name,
        out_shape=sems + thru + [jax.ShapeDtypeStruct((8, LANES), F32)],
        in_specs=[_HBM] * (2 * n) + [pl.BlockSpec(memory_space=pl.ANY)] * m,
        out_specs=[_SEM] * (2 * n) + [_HBM] * (2 * n) + [pl.BlockSpec(memory_space=pltpu.VMEM)],
        input_output_aliases={i: 2 * n + i for i in range(2 * n)},
        compiler_params=pltpu.CompilerParams(has_side_effects=_EFFECT),
    )(*[pltpu.with_memory_space_constraint(a, pltpu.HBM) for a in list(xs) + lands], *after)
    handles = [(res[i], res[n + i], res[2 * n + i], res[3 * n + i]) for i in range(n)]
    return handles, res[4 * n]


def exchange_wait(name, handles, after, scatter):
    n = len(handles)

    def body(*refs):
        x_refs, land_refs = refs[:n], refs[n:2 * n]
        send_sems, recv_sems = refs[2 * n:3 * n], refs[3 * n:4 * n]
        x, y, c = _me()
        me = 4 * x + 2 * y + c
        for i in range(n):
            for k in range(1, N_DEV):
                src = x_refs[i].at[_peer_index(k)] if scatter else x_refs[i]
                cp = pltpu.make_async_remote_copy(
                    src_ref=src, dst_ref=land_refs[i].at[me],
                    send_sem=send_sems[i].at[k - 1], recv_sem=recv_sems[i].at[k - 1],
                    device_id=_peer(k), device_id_type=pl.DeviceIdType.MESH)
                cp.wait_send()
                cp.wait_recv()

    xs = [h[2] for h in handles]
    lands = [h[3] for h in handles]
    res = pl.pallas_call(
        body, name=name,
        out_shape=[pltpu.HBM(a.shape, a.dtype) for a in xs + lands],
        in_specs=[_HBM] * (2 * n) + [_SEM] * (2 * n) + [pl.BlockSpec(memory_space=pl.ANY)],
        out_specs=[_HBM] * (2 * n),
        input_output_aliases={i: i for i in range(2 * n)},
        compiler_params=pltpu.CompilerParams(has_side_effects=_EFFECT),
    )(*xs, *lands, *[h[0] for h in handles], *[h[1] for h in handles], after)
    return res[n:]


def _tile(n, cap):
    best = None
    for t in range(LANES, min(n, cap) + 1, LANES):
        if n % t == 0:
            best = t
    if best is None or (best < 256 and n <= 2304):
        return n
    return best


def _row_tile(m, cap):
    if m <= cap:
        return m
    t = cap
    while m % t:
        t //= 2
    return t


def _after_spec(after):
    return [] if after is None else [pl.BlockSpec(memory_space=pl.ANY)]


def _after_arg(after):
    return [] if after is None else [after]


def _mm_call(name, dims, grid, a_spec, b_spec, o_spec, o_shape, tile, a, b, out_dtypes, epilogue, extra, after):
    nk = grid[2]
    n_x, n_o = len(extra), len(out_dtypes)

    def body(a_ref, b_ref, *refs):
        x_refs, o_refs, acc_ref = refs[:n_x], refs[len(refs) - 1 - n_o:len(refs) - 1], refs[-1]
        k = pl.program_id(2)

        @pl.when(k == 0)
        def _():
            acc_ref[...] = jnp.zeros_like(acc_ref)

        acc_ref[...] += lax.dot_general(a_ref[...].astype(BF16), b_ref[...].astype(BF16), (dims, ((), ())),
                                        preferred_element_type=F32)

        @pl.when(k == nk - 1)
        def _():
            acc = acc_ref[...]
            vals = (acc,) if epilogue is None else epilogue(acc, *[x[...] for x in x_refs])
            for o_ref, val in zip(o_refs, vals):
                o_ref[...] = val.astype(o_ref.dtype)

    return pl.pallas_call(
        body, name=name, out_shape=[jax.ShapeDtypeStruct(o_shape, dt) for dt in out_dtypes], grid=grid,
        in_specs=[a_spec, b_spec] + [o_spec] * n_x + _after_spec(after), out_specs=[o_spec] * n_o,
        scratch_shapes=[pltpu.VMEM(tile, F32)],
        compiler_params=_params(("parallel", "parallel", "arbitrary")),
    )(a, b, *extra, *_after_arg(after))


def mm_nn(name, a, w, out_dtypes=(F32,), epilogue=None, extra=(), after=None):
    M, K = a.shape
    G, _, n = w.shape
    tn = _tile(n, 1024)
    tm = _row_tile(M, 512 if tn > 1024 else 1024)
    tk = _row_tile(K, 1024)
    r = n // tn
    return _mm_call(
        name, ((1,), (0,)), (M // tm, G * r, K // tk),
        pl.BlockSpec((tm, tk), lambda i, j, k: (i, k)),
        pl.BlockSpec((None, tk, tn), lambda i, j, k: (j // r, k, j % r)),
        pl.BlockSpec((tm, tn), lambda i, j, k: (i, j)), (M, G * n), (tm, tn),
        a, w, out_dtypes, epilogue, extra, after)


def mm_nt(name, dy, w, out_dtypes=(F32,), epilogue=None, extra=(), after=None):
    M, N = dy.shape
    G, K, n = w.shape
    tn = _tile(n, 1024)
    tm = _row_tile(M, 512 if tn > 1024 else 1024)
    tk = _row_tile(K, 1024)
    r = n // tn
    return _mm_call(
        name, ((1,), (1,)), (M // tm, K // tk, G * r),
        pl.BlockSpec((tm, tn), lambda i, j, k: (i, k)),
        pl.BlockSpec((None, tk, tn), lambda i, j, k: (k // r, j, k % r)),
        pl.BlockSpec((tm, tk), lambda i, j, k: (i, j)), (M, K), (tm, tk),
        dy, w, out_dtypes, epilogue, extra, after)


def mm_tn(name, a, dy, G, out_dtype=F32, after=None):
    M, K = a.shape
    n = dy.shape[1] // G
    tn = _tile(n, 1024)
    tk = _row_tile(K, 512 if tn > 1024 else 1024)
    tm = _row_tile(M, 1024)
    r = n // tn
    return _mm_call(
        name, ((0,), (0,)), (K // tk, G * r, M // tm),
        pl.BlockSpec((tm, tk), lambda i, j, k: (k, i)),
        pl.BlockSpec((tm, tn), lambda i, j, k: (k, j)),
        pl.BlockSpec((None, tk, tn), lambda i, j, k: (j // r, i, j % r)), (G, K, n), (tk, tn),
        a, dy, (out_dtype,), None, (), after)[0]


def rowwise(name, fn, rows, vecs, outs, accs=(), tm=256, after=None):
    rows = [r if isinstance(r, tuple) else (r, r.shape[1], 0) for r in rows]
    n_fn = len(rows) + len(vecs)
    vecs = list(vecs) + _after_arg(after)
    S = rows[0][0].shape[0]
    tm = _row_tile(S, tm)
    n_r, n_v, n_o, n_a = len(rows), len(vecs), len(outs), len(accs)

    def body(*refs):
        ins = [ref[...] for ref in refs[:n_r + n_v]]
        o_refs = refs[n_r + n_v:n_r + n_v + n_o]
        a_refs = refs[n_r + n_v + n_o:]
        res = fn(*ins[:n_fn])
        res = res if isinstance(res, (tuple, list)) else (res,)
        for ref, val in zip(o_refs, res[:n_o]):
            ref[...] = val.astype(ref.dtype)
        if n_a:
            @pl.when(pl.program_id(0) == 0)
            def _():
                for ref in a_refs:
                    ref[...] = jnp.zeros_like(ref)
            for ref, val in zip(a_refs, res[n_o:]):
                ref[...] += val

    in_specs = [pl.BlockSpec((tm, w), functools.partial(lambda cb, i: (i, cb), cb)) for _, w, cb in rows]
    in_specs += [pl.BlockSpec(v.shape, lambda i: (0, 0)) for v in vecs]
    out_specs = [pl.BlockSpec((tm, w), lambda i: (i, 0)) for w, _ in outs]
    out_specs += [pl.BlockSpec(a, lambda i: (0, 0)) for a in accs]
    out_shape = [jax.ShapeDtypeStruct((S, w), dt) for w, dt in outs]
    out_shape += [jax.ShapeDtypeStruct(a, F32) for a in accs]
    res = pl.pallas_call(
        body, name=name, out_shape=out_shape, grid=(S // tm,),
        in_specs=in_specs, out_specs=out_specs,
        compiler_params=_params(("arbitrary",)),
    )(*[r[0] for r in rows], *vecs)
    return res


def _rms(x):
    return x * lax.rsqrt(jnp.mean(x * x, axis=-1, keepdims=True) + EPS)


def _modulate(x, gain, scale, shift):
    return _rms(x) * gain * (1.0 + scale) + shift


def _ada_slices(ada_raw, bias):
    ada = ada_raw + bias
    return [ada[:, i * D_MODEL:(i + 1) * D_MODEL] for i in range(ada.shape[1] // D_MODEL)]


def _norm_wide_heads(y, gain, heads):
    w = y.shape[1] // heads
    return jnp.concatenate([_rms(y[:, h * w:(h + 1) * w]) * gain[:, h * w:(h + 1) * w] for h in range(heads)], axis=1)


def _norm_fox_heads(x, gain):
    outs = []
    for p in range(x.shape[1] // LANES):
        blk = x[:, p * LANES:(p + 1) * LANES]
        low = lax.broadcasted_iota(jnp.int32, blk.shape, 1) < FOX_DH
        sq = blk * blk
        ss_low = jnp.sum(jnp.where(low, sq, 0.0), axis=1, keepdims=True)
        ss_high = jnp.sum(jnp.where(low, 0.0, sq), axis=1, keepdims=True)
        outs.append(blk * lax.rsqrt(jnp.where(low, ss_low, ss_high) * (1.0 / FOX_DH) + EPS))
    return jnp.concatenate(outs, axis=1) * gain


def _silu(x):
    return x * jax.nn.sigmoid(x)


def _log_sigmoid(z):
    return -(jnp.maximum(-z, 0.0) + jnp.log(1.0 + jnp.exp(-jnp.abs(z))))


def _rotate(x, cos, sin, heads, sign):
    w = x.shape[1] // heads
    half = w // 2
    outs = []
    for h in range(heads):
        x1 = x[:, h * w:h * w + half]
        x2 = x[:, h * w + half:(h + 1) * w]
        outs += [x1 * cos - sign * x2 * sin, sign * x1 * sin + x2 * cos]
    return jnp.concatenate(outs, axis=1)


def _vjp(fn, primals, ct):
    return jax.vjp(fn, *primals)[1](ct)


_LOG_GAMMAS = [float(np.log(np.float32(1.0) - np.float32(2.0) ** np.float32(-5.0 - h))) for h in range(RET_HEADS)]


RET_ROWS = 512


def retention(name, q, k, v, reverse):
    (qa, dk, qo), (ka, _, ko), (va, dv, vo) = q, k, v
    S = qa.shape[0]
    C = RET_CHUNK
    rows = _row_tile(S, RET_ROWS)
    nb = S // rows

    def body(q_ref, k_ref, v_ref, o_ref, state):
        h = pl.program_id(0)

        @pl.when(pl.program_id(1) == 0)
        def _():
            state[...] = jnp.zeros_like(state)

        log_g = jnp.float32(_LOG_GAMMAS[RET_HEADS - 1])
        for i in range(RET_HEADS - 2, -1, -1):
            log_g = jnp.where(h == i, jnp.float32(_LOG_GAMMAS[i]), log_g)
        row = lax.broadcasted_iota(jnp.int32, (C, C), 0)
        col = lax.broadcasted_iota(jnp.int32, (C, C), 1)
        rel = (col - row if reverse else row - col).astype(F32)
        decay = jnp.where(rel >= 0, jnp.exp(log_g * jnp.maximum(rel, 0.0)), 0.0)
        j = lax.broadcasted_iota(jnp.int32, (C, 1), 0).astype(F32)
        q_decay = jnp.exp(log_g * (C - j if reverse else j + 1.0))
        k_decay = jnp.exp(log_g * (j if reverse else C - 1.0 - j))
        chunk_decay = jnp.exp(jnp.full((1, 1), log_g * C, F32))

        chunks = range(rows // C)
        for ci in (reversed(chunks) if reverse else chunks):
            rs = slice(ci * C, (ci + 1) * C)
            qc = q_ref[rs, :].astype(BF16)
            kf = k_ref[rs, :].astype(F32)
            vc = v_ref[rs, :].astype(BF16)
            scores = lax.dot_general(qc, kf.astype(BF16), (((1,), (1,)), ((), ())), preferred_element_type=F32) * decay
            intra = jnp.dot(scores.astype(BF16), vc, preferred_element_type=F32)
            cross = jnp.dot(qc, state[...].astype(BF16), preferred_element_type=F32) * q_decay
            o_ref[rs, :] = intra + cross
            upd = lax.dot_general((kf * k_decay).astype(BF16), vc, (((0,), (0,)), ((), ())), preferred_element_type=F32)
            state[...] = state[...] * chunk_decay + upd

    def block(i):
        return nb - 1 - i if reverse else i

    return pl.pallas_call(
        body, name=name, out_shape=jax.ShapeDtypeStruct((S, RET_HEADS * dv), F32),
        grid=(RET_HEADS, nb),
        in_specs=[pl.BlockSpec((rows, dk), lambda h, i: (block(i), qo + h)),
                  pl.BlockSpec((rows, dk), lambda h, i: (block(i), ko + h)),
                  pl.BlockSpec((rows, dv), lambda h, i: (block(i), vo + h))],
        out_specs=pl.BlockSpec((rows, dv), lambda h, i: (block(i), h)),
        scratch_shapes=[pltpu.VMEM((dk, dv), F32)],
        compiler_params=_params(("parallel", "arbitrary")),
    )(qa, ka, va)


FOX_T = 256
N_PAIR = FOX_HEADS // 2
FOX_SCALE = FOX_DH ** -0.5


def _fox_heads(q2):
    low = lax.broadcasted_iota(jnp.int32, (1, LANES), 1) < FOX_DH
    return [(mask, jnp.where(mask, q2 * FOX_SCALE, 0.0).astype(BF16)) for mask in (low, jnp.logical_not(low))]


def _fox_parts(j, t):
    return ([(0, j * t, False)] if j else []) + [(j * t, (j + 1) * t, True)]


def _fox_scores(qa, k_ref, ft_ref, head, lo, hi, diagonal):
    k_blk = k_ref[lo:hi, :].astype(BF16)
    s = lax.dot_general(qa, k_blk, (((1,), (1,)), ((), ())), preferred_element_type=F32) - ft_ref[pl.ds(head, 1), lo:hi]
    if diagonal:
        n = hi - lo
        s = jnp.where(lax.broadcasted_iota(jnp.int32, (n, n), 1) <= lax.broadcasted_iota(jnp.int32, (n, n), 0), s, -jnp.inf)
    return s


def fox_forward(name, qn, kn, kvf, f_cum_t):
    S = qn.shape[0]
    t = _row_tile(S, FOX_T)
    v_block0 = D_MODEL // LANES

    def variant(j, pair, q_ref, k_ref, v_ref, ft_ref, y_ref, lse_ref):
        ys, lses = [], []
        for a, (mask, qa) in enumerate(_fox_heads(q_ref[...])):
            parts = [(lo, hi, _fox_scores(qa, k_ref, ft_ref, 2 * pair + a, lo, hi, dg)) for lo, hi, dg in _fox_parts(j, t)]
            m = functools.reduce(jnp.maximum, [jnp.max(s, axis=1, keepdims=True) for _, _, s in parts])
            l, acc = 0.0, 0.0
            for lo, hi, s in parts:
                e = jnp.exp(s - m)
                l = l + jnp.sum(e, axis=1, keepdims=True)
                acc = acc + jnp.dot(e.astype(BF16), v_ref[lo:hi, :].astype(BF16), preferred_element_type=F32)
            ys.append(acc / l)
            lses.append(m + jnp.log(l))
        low = lax.broadcasted_iota(jnp.int32, (1, LANES), 1) < FOX_DH
        y_ref[...] = jnp.where(low, ys[0], ys[1])
        lse_ref[...] = jnp.where(low, lses[0], lses[1])

    def body(*refs):
        pair, i = pl.program_id(0), pl.program_id(1)
        for j in range(S // t):
            pl.when(i == j)(functools.partial(variant, j, pair, *refs))

    return pl.pallas_call(
        body, name=name,
        out_shape=[jax.ShapeDtypeStruct((S, D_MODEL), F32), jax.ShapeDtypeStruct((S, D_MODEL), F32)],
        grid=(N_PAIR, S // t),
        in_specs=[pl.BlockSpec((t, LANES), lambda p, i: (i, p)),
                  pl.BlockSpec((S, LANES), lambda p, i: (0, p)),
                  pl.BlockSpec((S, LANES), lambda p, i: (0, v_block0 + p)),
                  pl.BlockSpec((LANES, S), lambda p, i: (0, 0))],
        out_specs=[pl.BlockSpec((t, LANES), lambda p, i: (i, p)),
                   pl.BlockSpec((t, LANES), lambda p, i: (i, p))],
        compiler_params=_params(("parallel", "arbitrary")),
    )(qn, kn, kvf, f_cum_t)


def fox_backward(name, qn, kn, kvf, f_cum_t, y, dy, lse):
    S = qn.shape[0]
    t = _row_tile(S, FOX_T)
    v_block0 = D_MODEL // LANES

    def variant(j, pair, q_ref, k_ref, v_ref, ft_ref, y_ref, dy_ref, lse_ref, dq_ref, dk_ref, dv_ref, dfq_ref, dfk_ref):
        y2, dy2, lse2 = y_ref[...], dy_ref[...], lse_ref[...]
        lane = lax.broadcasted_iota(jnp.int32, (t, LANES), 1)
        dqs, dfq = [], jnp.zeros((t, LANES), F32)
        for a, (mask, qa) in enumerate(_fox_heads(q_ref[...])):
            lse_a = jnp.max(jnp.where(mask, lse2, -jnp.inf), axis=1, keepdims=True)
            dy_a = jnp.where(mask, dy2, 0.0)
            delta = jnp.sum(dy_a * y2, axis=1, keepdims=True)
            dy_b = dy_a.astype(BF16)
            dq, row_sum = 0.0, 0.0
            for lo, hi, dg in _fox_parts(j, t):
                p = jnp.exp(_fox_scores(qa, k_ref, ft_ref, 2 * pair + a, lo, hi, dg) - lse_a)
                dp = lax.dot_general(dy_b, v_ref[lo:hi, :].astype(BF16), (((1,), (1,)), ((), ())), preferred_element_type=F32)
                ds = p * (dp - delta)
                row_sum = row_sum + jnp.sum(ds, axis=1, keepdims=True)
                dfk_ref[pl.ds(a, 1), lo:hi] += -jnp.sum(ds, axis=0, keepdims=True)
                ds_b = ds.astype(BF16)
                dq = dq + jnp.dot(ds_b, k_ref[lo:hi, :].astype(BF16), preferred_element_type=F32)
                dk_ref[lo:hi, :] += lax.dot_general(ds_b, qa, (((0,), (0,)), ((), ())), preferred_element_type=F32)
                dv_ref[lo:hi, :] += lax.dot_general(p.astype(BF16), dy_b, (((0,), (0,)), ((), ())), preferred_element_type=F32)
            dqs.append(dq * FOX_SCALE)
            dfq = dfq + jnp.where(lane == 2 * pair + a, row_sum, 0.0)
        low = lax.broadcasted_iota(jnp.int32, (1, LANES), 1) < FOX_DH
        dq_ref[...] = jnp.where(low, dqs[0], dqs[1])
        dfq_ref[...] = dfq

    def body(*refs):
        pair, i = pl.program_id(0), pl.program_id(1)
        dk_ref, dv_ref, dfk_ref = refs[8], refs[9], refs[11]

        @pl.when(i == 0)
        def _():
            dk_ref[...] = jnp.zeros_like(dk_ref)
            dv_ref[...] = jnp.zeros_like(dv_ref)
            dfk_ref[...] = jnp.zeros_like(dfk_ref)

        for j in range(S // t):
            pl.when(i == j)(functools.partial(variant, j, pair, *refs))

    row_blk = pl.BlockSpec((t, LANES), lambda p, i: (i, p))
    col_blk = pl.BlockSpec((S, LANES), lambda p, i: (0, p))
    return pl.pallas_call(
        body, name=name,
        out_shape=[jax.ShapeDtypeStruct((S, D_MODEL), F32)] * 3
        + [jax.ShapeDtypeStruct((N_PAIR, S, LANES), F32), jax.ShapeDtypeStruct((N_PAIR, 8, S), F32)],
        grid=(N_PAIR, S // t),
        in_specs=[row_blk, col_blk,
                  pl.BlockSpec((S, LANES), lambda p, i: (0, v_block0 + p)),
                  pl.BlockSpec((LANES, S), lambda p, i: (0, 0)),
                  row_blk, row_blk, row_blk],
        out_specs=[row_blk, col_blk, col_blk,
                   pl.BlockSpec((None, t, LANES), lambda p, i: (p, i, 0)),
                   pl.BlockSpec((None, 8, S), lambda p, i: (p, 0, 0))],
        compiler_params=_params(("parallel", "arbitrary")),
    )(qn, kn, kvf, f_cum_t, y, dy, lse)


def cumsum_rows(name, x, reverse):
    S = x.shape[0]
    C = LANES
    nc = S // C

    def body(x_ref, o_ref):
        row = lax.broadcasted_iota(jnp.int32, (C, C), 0)
        col = lax.broadcasted_iota(jnp.int32, (C, C), 1)
        tri = jnp.where(col >= row if reverse else col <= row, 1.0, 0.0).astype(F32)
        carry = jnp.zeros((1, LANES), F32)
        for i in (range(nc - 1, -1, -1) if reverse else range(nc)):
            blk = x_ref[i * C:(i + 1) * C, :]
            loc = jnp.dot(tri, blk, preferred_element_type=F32, precision=lax.Precision.HIGHEST)
            o_ref[i * C:(i + 1) * C, :] = loc + carry
            carry = carry + (loc[0:1, :] if reverse else loc[C - 1:C, :])

    return pl.pallas_call(body, name=name, out_shape=jax.ShapeDtypeStruct((S, LANES), F32),
                          compiler_params=_params())(x)


def adamw(name, parts, w, m, v):
    L, R, C = w.shape
    P = parts[0].shape[0]
    tr = _row_tile(R, 256)
    nr = R // tr

    def body(*refs):
        p_refs = refs[:L]
        w_ref, m_ref, v_ref, g_out, d_out, m_out, v_out = refs[L:]
        for layer in range(L):
            @pl.when(pl.program_id(0) == layer)
            def _(p_ref=p_refs[layer]):
                g = p_ref[0].astype(F32)
                for i in range(1, P):
                    g = g + p_ref[i].astype(F32)
                m2 = ADAM_B1 * m_ref[...] + (1.0 - ADAM_B1) * g
                v2 = ADAM_B2 * v_ref[...] + (1.0 - ADAM_B2) * jnp.square(g)
                m_hat = m2 / (1.0 - ADAM_B1 ** ADAM_STEP)
                v_hat = v2 / (1.0 - ADAM_B2 ** ADAM_STEP)
                g_out[...] = g
                d_out[...] = -ADAM_LR * (m_hat / (jnp.sqrt(v_hat) + ADAM_EPS) + ADAM_WD * w_ref[...])
                m_out[...] = m2
                v_out[...] = v2

    def part_spec(layer):
        return pl.BlockSpec((P, tr, C), lambda l, i: (0, jnp.where(l == layer, i, jnp.where(l < layer, 0, nr - 1)), 0))

    blk = pl.BlockSpec((None, tr, C), lambda l, i: (l, i, 0))
    return pl.pallas_call(
        body, name=name, out_shape=[jax.ShapeDtypeStruct((L, R, C), F32)] * 4, grid=(L, nr),
        in_specs=[part_spec(layer) for layer in range(L)] + [blk, blk, blk],
        out_specs=[blk] * 4, compiler_params=_params(("arbitrary", "arbitrary")),
    )(*parts, w, m, v)


def kernel(x, c, positions, norm_mix_gain, norm_mlp_gain, w_ada, b_ada, w_mlp_in, w_mlp_out, ret_w_in, ret_norm_gain, ret_w_out, kv_norm_gain, kv_w_ada, kv_b_ada, kv_w, forget_bias, k_norm_gain, fox_w_in, q_norm_gain, fox_w_out, loss_target, m_norm_mix_gain, m_norm_mlp_gain, m_w_ada, m_b_ada, m_w_mlp_in, m_w_mlp_out, m_ret_w_in, m_ret_norm_gain, m_ret_w_out, m_kv_norm_gain, m_kv_w_ada, m_kv_b_ada, m_kv_w, m_forget_bias, m_k_norm_gain, m_fox_w_in, m_q_norm_gain, m_fox_w_out, v_norm_mix_gain, v_norm_mlp_gain, v_w_ada, v_b_ada, v_w_mlp_in, v_w_mlp_out, v_ret_w_in, v_ret_norm_gain, v_ret_w_out, v_kv_norm_gain, v_kv_w_ada, v_kv_b_ada, v_kv_w, v_forget_bias, v_k_norm_gain, v_fox_w_in, v_q_norm_gain, v_fox_w_out):
    D = D_MODEL
    S = x.shape[1]
    x0 = x.reshape(S, D)
    target = loss_target.reshape(S, D)
    me = 4 * lax.axis_index("x") + 2 * lax.axis_index("y") + lax.axis_index("c")
    n_ada = w_ada.shape[2]
    n_kvada = kv_w_ada.shape[1]
    n_kv = kv_w.shape[1]

    c_all, ret_gain = all_gather("gather_c", [c.reshape(1, D), ret_norm_gain.reshape(RET_HEADS, -1)])
    ret_gain = jnp.transpose(ret_gain, (1, 0, 2)).reshape(1, RET_HEADS * RET_V)
    c_act = rowwise("silu_c", _silu, [c_all.reshape(N_DEV, D)], [], [(D, F32)])[0]
    w_ada_cat = jnp.concatenate([w_ada[0], w_ada[1], kv_w_ada], axis=1).astype(BF16)[None]
    n_cat = 2 * n_ada + n_kvada
    ada_part = mm_nn("ada_proj", c_act, w_ada_cat)[0]
    ada_mine = all_to_all("ada_rows", [ada_part.reshape(N_DEV, 1, n_cat)])[0][:, 0]
    ada_raw = [ada_mine[:, l * n_ada:(l + 1) * n_ada].reshape(1, 6 * D) for l in range(2)]
    kvada_raw = ada_mine[:, 2 * n_ada:].reshape(1, 2 * D)
    kv_bias = kv_b_ada.reshape(1, 2 * D)
    kv_gain = kv_norm_gain.reshape(1, D)
    fb = jnp.pad(forget_bias.reshape(1, FOX_HEADS), ((0, 0), (0, LANES - FOX_HEADS)))
    k_gain = jnp.tile(k_norm_gain.reshape(1, FOX_DH), (1, FOX_HEADS))
    q_gain = jnp.tile(q_norm_gain.reshape(1, FOX_DH), (1, FOX_HEADS))

    w_names = ["ret_in", "ret_out", "mlp_in0", "mlp_out0", "kv", "fox_in", "fox_out", "mlp_in1", "mlp_out1"]
    w_handles, token = exchange_start("gather_weights_start", [
        ret_w_in[0].astype(BF16), ret_w_out[0].astype(BF16), w_mlp_in[0].astype(BF16), w_mlp_out[0].astype(BF16),
        kv_w.astype(BF16), fox_w_in[0].astype(BF16), fox_w_out[0].astype(BF16), w_mlp_in[1].astype(BF16),
        w_mlp_out[1].astype(BF16)], scatter=False, after=(ada_mine, ret_gain))
    w_handles = dict(zip(w_names, w_handles))

    def weight(name, after):
        return exchange_wait("gather_wait_" + name, [w_handles[name]], after, scatter=False)[0]

    pos = positions.reshape(S, 1).astype(F32)
    half = RET_QK // 2
    inv_freq = jnp.asarray((ROPE_BASE ** (-np.arange(half, dtype=np.float32) / half)).reshape(1, half), F32)

    def angles(p, f):
        ang = p * f
        return jnp.cos(ang), jnp.sin(ang)

    cos, sin = rowwise("rope_table", angles, [pos], [inv_freq], [(half, F32), (half, F32)])

    def mod_mix(layer):
        def fn(xb, ada, bias, gain):
            sh, sc = _ada_slices(ada, bias)[:2]
            return _modulate(xb, gain[layer:layer + 1], sc, sh)
        return fn

    def mod_mlp(layer):
        def fn(xb, ada, bias, gain):
            sh, sc = _ada_slices(ada, bias)[3:5]
            return _modulate(xb, gain[layer:layer + 1], sc, sh)
        return fn

    h1_0 = rowwise("mod_mix0", mod_mix(0), [x0], [ada_raw[0], b_ada[0:1], norm_mix_gain], [(D, BF16)], after=token)[0]
    W_ret_in = weight("ret_in", h1_0)
    proj = mm_nn("ret_proj", h1_0, W_ret_in)[0]

    def rope_fwd(qb, kb, cs, sn):
        return _rotate(qb, cs, sn, RET_HEADS, 1.0), _rotate(kb, cs, sn, RET_HEADS, 1.0) * (RET_QK ** -0.5)

    q_rot, k_rot = rowwise("rope", rope_fwd, [(proj, D, 0), (proj, D, 1), cos, sin], [], [(D, F32), (D, F32)])
    v_ret = (proj, RET_V, (2 * D) // RET_V)
    y_ret = retention("ret_fwd", (q_rot, RET_QK, 0), (k_rot, RET_QK, 0), v_ret, reverse=False)

    def ret_gate(yb, gb, gain):
        return _silu(gb) * _norm_wide_heads(yb, gain, RET_HEADS)

    mixin0 = rowwise("ret_gate", ret_gate, [y_ret, (proj, 2 * D, 2)], [ret_gain], [(2 * D, BF16)])[0]
    W_ret_out = weight("ret_out", mixin0).reshape(1, 2 * D, D)
    mix0 = mm_nn("ret_out", mixin0, W_ret_out)[0]

    def residual_mod(layer, slot):
        def fn(xb, bb, ada, bias, gain):
            s = _ada_slices(ada, bias)
            xn = xb + s[2] * bb
            return xn, _modulate(xn, gain[layer:layer + 1], s[4], s[3])
        return fn

    x1, h2_0 = rowwise("res_mix0", residual_mod(0, 0), [x0, mix0], [ada_raw[0], b_ada[0:1], norm_mlp_gain],
                       [(D, F32), (D, BF16)])

    W_mlp_in, W_mlp_out = {}, {}

    def mlp_forward(tag, h2, layer):
        W_mlp_in[layer] = weight("mlp_in" + tag, h2)
        u, act = mm_nn("mlp_in" + tag, h2, W_mlp_in[layer], (BF16, BF16),
                       epilogue=lambda acc: (acc, jnp.square(jnp.maximum(acc, 0.0))))
        W_mlp_out[layer] = weight("mlp_out" + tag, act).reshape(1, 4 * D, D)
        return u, act, mm_nn("mlp_out" + tag, act, W_mlp_out[layer])[0]

    u0, act0, mlp0 = mlp_forward("0", h2_0, 0)

    def res_mlp0(xb, bb, ada0, bias0, ada1, bias1, kva, kvb, gain_mix, gain_kv):
        xn = xb + _ada_slices(ada0, bias0)[5] * bb
        s1 = _ada_slices(ada1, bias1)
        kv_shift, kv_scale = _ada_slices(kva, kvb)
        return xn, _modulate(xn, gain_kv, kv_scale, kv_shift), _modulate(xn, gain_mix[1:2], s1[1], s1[0])

    x2, h_kv, h1_1 = rowwise("res_mlp0", res_mlp0, [x1, mlp0],
                             [ada_raw[0], b_ada[0:1], ada_raw[1], b_ada[1:2], kvada_raw, kv_bias, norm_mix_gain, kv_gain],
                             [(D, F32), (D, BF16), (D, BF16)])

    kv_full = jnp.transpose(weight("kv", h_kv), (1, 0, 2)).reshape(D, N_DEV * n_kv)
    W_kv = jnp.pad(kv_full, ((0, 0), (0, KV_PAD - N_DEV * n_kv)))[None]
    kvf = mm_nn("kv_proj", h_kv, W_kv)[0]

    def kv_post(kb, fblk, kg, bias):
        head = lax.broadcasted_iota(jnp.int32, fblk.shape, 1) < FOX_HEADS
        return _norm_fox_heads(kb, kg), jnp.where(head, _log_sigmoid(fblk + bias), 0.0)

    kn, log_f = rowwise("kv_post", kv_post, [(kvf, D, 0), (kvf, LANES, 2 * D // LANES)], [k_gain, fb],
                        [(D, F32), (LANES, F32)])
    f_cum = cumsum_rows("f_cumsum", log_f, reverse=False)
    f_cum_t = f_cum.T

    W_fox_in = weight("fox_in", kvf)
    qo = mm_nn("fox_proj", h1_1, W_fox_in)[0]
    qn = rowwise("q_norm", _norm_fox_heads, [(qo, D, 0)], [q_gain], [(D, F32)])[0]
    y_att, lse = fox_forward("fox_fwd", qn, kn, kvf, f_cum_t)
    mixin1 = rowwise("fox_gate", lambda ob, yb: jax.nn.sigmoid(ob) * yb, [(qo, D, 1), y_att], [], [(D, BF16)])[0]
    W_fox_out = weight("fox_out", mixin1).reshape(1, D, D)
    mix1 = mm_nn("fox_out", mixin1, W_fox_out)[0]
    x3, h2_1 = rowwise("res_mix1", residual_mod(1, 0), [x2, mix1], [ada_raw[1], b_ada[1:2], norm_mlp_gain],
                       [(D, F32), (D, BF16)])
    u1, act1, mlp1 = mlp_forward("1", h2_1, 1)

    def loss_head(xb, bb, tb, ada, bias):
        g2 = _ada_slices(ada, bias)[5]
        err = xb + g2 * bb - tb
        dx = err * (1.0 / D)
        loss = 0.5 * jnp.sum(jnp.sum(err * err, axis=1, keepdims=True) * (1.0 / D), axis=0, keepdims=True)
        return dx, (dx * g2), jnp.broadcast_to(loss, (1, LANES)), jnp.sum(dx * bb, axis=0, keepdims=True)

    dx4, dmlp1, loss_acc, dg2_1 = rowwise("loss_head", loss_head, [x3, mlp1, target], [ada_raw[1], b_ada[1:2]],
                                          [(D, F32), (D, BF16)], [(1, LANES), (1, D)])
    loss = lax.psum(loss_acc[0, 0], ("x", "y", "c"))

    def mlp_backward(tag, dmlp, act, u, h2, layer):
        du = mm_nt("mlp_out_dx" + tag, dmlp, W_mlp_out[layer], (BF16,), extra=(u,),
                   epilogue=lambda acc, ub: (acc * (2.0 * jnp.maximum(ub.astype(F32), 0.0)),))[0]
        gw_out = mm_tn("mlp_out_dw" + tag, act, dmlp, 1, BF16).reshape(N_DEV, -1, D)
        dh2 = mm_nt("mlp_in_dx" + tag, du, W_mlp_in[layer])[0]
        gw_in = mm_tn("mlp_in_dw" + tag, h2, du, N_DEV, BF16)
        return dh2, gw_in, gw_out

    def mod_backward(layer, slots, gate_slot):
        def fn(xb, dhb, dresb, branchb, ada, bias, gain):
            s = _ada_slices(ada, bias)
            g = gain[layer:layer + 1]
            dx, dgain, dsc, dsh = _vjp(_modulate, (xb, g, s[slots[1]], s[slots[0]]), dhb)
            dx = dx + dresb
            d_branch = dx * s[gate_slot]
            return dx, d_branch, dgain, dsc, dsh, jnp.sum(dx * branchb, axis=0, keepdims=True)
        return fn

    vec = (1, D)
    dh2_1, gw_mlp_in1, gw_mlp_out1 = mlp_backward("1", dmlp1, act1, u1, h2_1, 1)
    dx3, dmix1, dgain_mlp1, dsc2_1, dsh2_1, dg1_1 = rowwise(
        "mod_mlp1_bwd", mod_backward(1, (3, 4), 2), [x3, dh2_1, dx4, mix1], [ada_raw[1], b_ada[1:2], norm_mlp_gain],
        [(D, F32), (D, BF16)], [vec] * 4)
    dmixin1 = mm_nt("fox_out_dx", dmix1, W_fox_out)[0]
    gw_fox_out = mm_tn("fox_out_dw", mixin1, dmix1, 1, BF16).reshape(N_DEV, -1, D)

    def fox_gate_bwd(db, ob, yb):
        sg = jax.nn.sigmoid(ob)
        return db * sg, db * yb * sg * (1.0 - sg)

    dy_att, d_og = rowwise("fox_gate_bwd", fox_gate_bwd, [dmixin1, (qo, D, 1), y_att], [], [(D, F32), (D, F32)])
    dqn, dkn, dv_att, dfq, dfk = fox_backward("fox_bwd", qn, kn, kvf, f_cum_t, y_att, dy_att, lse)

    def q_norm_bwd(qb, db, ogb, gain):
        dq, dgain = _vjp(_norm_fox_heads, (qb, gain), db)
        return jnp.concatenate([dq, ogb], axis=1), dgain

    dqo, dq_gain = rowwise("q_norm_bwd", q_norm_bwd, [(qo, D, 0), dqn, d_og], [q_gain], [(2 * D, BF16)], [vec])
    dh1_1 = mm_nt("fox_proj_dx", dqo, W_fox_in)[0]
    gw_fox_in = mm_tn("fox_proj_dw", h1_1, dqo, N_DEV, BF16)

    dfk_rows = jnp.pad(dfk[:, :2, :].reshape(FOX_HEADS, S).T, ((0, 0), (0, LANES - FOX_HEADS)))

    def df_total(*blks):
        tot = blks[0]
        for b in blks[1:]:
            tot = tot + b
        return tot

    d_fcum = rowwise("df_sum", df_total, [dfk_rows] + [dfq[p] for p in range(N_PAIR)], [], [(LANES, F32)])[0]
    d_logf = cumsum_rows("df_cumsum", d_fcum, reverse=True)

    def kv_post_bwd(kb, fblk, dkb, dvb, dlf, kg, bias):
        dk, dgain = _vjp(_norm_fox_heads, (kb, kg), dkb)
        df = dlf * (1.0 / (1.0 + jnp.exp(fblk + bias)))
        return jnp.concatenate([dk, dvb, df], axis=1), dgain, jnp.sum(df, axis=0, keepdims=True)

    dkvf, dk_gain, dfb = rowwise("kv_post_bwd", kv_post_bwd,
                                 [(kvf, D, 0), (kvf, LANES, 2 * D // LANES), dkn, dv_att, d_logf], [k_gain, fb],
                                 [(KV_PAD, BF16)], [vec, (1, LANES)])
    dh_kv = mm_nt("kv_proj_dx", dkvf, W_kv)[0]
    gw_kv = mm_tn("kv_proj_dw", h_kv, dkvf, 1, BF16)[0, :, :N_DEV * n_kv]
    gw_kv = jnp.transpose(gw_kv.reshape(D, N_DEV, n_kv), (1, 0, 2))
    scat_a, token_a = exchange_start("scatter_start_a", [gw_mlp_in1, gw_mlp_out1, gw_fox_out, gw_fox_in, gw_kv], scatter=True)

    def x2_bwd(xb, dh1b, dhkb, dresb, branchb, ada0, bias0, ada1, bias1, kva, kvb, gain_mix, gain_kv):
        s1 = _ada_slices(ada1, bias1)
        kv_shift, kv_scale = _ada_slices(kva, kvb)
        dxa, dgain_mix, dsc1, dsh1 = _vjp(_modulate, (xb, gain_mix[1:2], s1[1], s1[0]), dh1b)
        dxb, dgain_kv, dkv_scale, dkv_shift = _vjp(_modulate, (xb, gain_kv, kv_scale, kv_shift), dhkb)
        dx = dresb + dxa + dxb
        g2 = _ada_slices(ada0, bias0)[5]
        return (dx, dx * g2, dgain_mix, dsc1, dsh1, dgain_kv, dkv_scale, dkv_shift,
                jnp.sum(dx * branchb, axis=0, keepdims=True))

    (dx2, dmlp0, dgain_mix1, dsc1_1, dsh1_1, dgain_kv, dkv_scale, dkv_shift, dg2_0) = rowwise(
        "x2_bwd", x2_bwd, [x2, dh1_1, dh_kv, dx3, mlp0],
        [ada_raw[0], b_ada[0:1], ada_raw[1], b_ada[1:2], kvada_raw, kv_bias, norm_mix_gain, kv_gain],
        [(D, F32), (D, BF16)], [vec] * 7, after=token_a)

    dh2_0, gw_mlp_in0, gw_mlp_out0 = mlp_backward("0", dmlp0, act0, u0, h2_0, 0)
    dx1, dmix0, dgain_mlp0, dsc2_0, dsh2_0, dg1_0 = rowwise(
        "mod_mlp0_bwd", mod_backward(0, (3, 4), 2), [x1, dh2_0, dx2, mix0], [ada_raw[0], b_ada[0:1], norm_mlp_gain],
        [(D, F32), (D, BF16)], [vec] * 4)
    dmixin0 = mm_nt("ret_out_dx", dmix0, W_ret_out)[0]
    gw_ret_out = mm_tn("ret_out_dw", mixin0, dmix0, 1, BF16).reshape(N_DEV, -1, D)
    scat_b, token_b = exchange_start("scatter_start_b", [gw_mlp_in0, gw_mlp_out0, gw_ret_out], scatter=True)

    def ret_gate_bwd(db, yb, gb, gain):
        dy, dg, dgain = _vjp(lambda y_, g_, gn_: ret_gate(y_, g_, gn_), (yb, gb, gain), db)
        return dy, dg, dgain

    dy_ret, dgate, dret_gain = rowwise("ret_gate_bwd", ret_gate_bwd, [dmixin0, y_ret, (proj, 2 * D, 2)], [ret_gain],
                                       [(2 * D, F32), (2 * D, BF16)], [(1, 2 * D)], after=token_b)
    dy_h = (dy_ret, RET_V, 0)
    dq_rot = retention("ret_dq", dy_h, v_ret, (k_rot, RET_QK, 0), reverse=False)
    dk_rot = retention("ret_dk", v_ret, dy_h, (q_rot, RET_QK, 0), reverse=True)
    dv_ret = retention("ret_dv", (k_rot, RET_QK, 0), (q_rot, RET_QK, 0), dy_h, reverse=True)

    def rope_bwd(dqb, dkb, dvb, dgb, cs, sn):
        dq = _rotate(dqb, cs, sn, RET_HEADS, -1.0)
        dk = _rotate(dkb, cs, sn, RET_HEADS, -1.0) * (RET_QK ** -0.5)
        return jnp.concatenate([dq, dk, dvb, dgb.astype(F32)], axis=1)

    dproj = rowwise("rope_bwd", rope_bwd, [dq_rot, dk_rot, dv_ret, dgate, cos, sin], [], [(6 * D, BF16)])[0]
    dh1_0 = mm_nt("ret_proj_dx", dproj, W_ret_in)[0]

    def x0_bwd(xb, dhb, dresb, ada, bias, gain):
        s = _ada_slices(ada, bias)
        dx, dgain, dsc, dsh = _vjp(_modulate, (xb, gain[0:1], s[1], s[0]), dhb)
        return dx + dresb, dgain, dsc, dsh

    grad_x, dgain_mix0, dsc1_0, dsh1_0 = rowwise("x0_bwd", x0_bwd, [x0, dh1_0, dx1],
                                                 [ada_raw[0], b_ada[0:1], norm_mix_gain], [(D, F32)], [vec] * 3)

    small = jnp.concatenate([
        dsh1_0, dsc1_0, dg1_0, dsh2_0, dsc2_0, dg2_0,
        dsh1_1, dsc1_1, dg1_1, dsh2_1, dsc2_1, dg2_1,
        dkv_shift, dkv_scale,
        dgain_mix0, dgain_mix1, dgain_mlp0, dgain_mlp1, dgain_kv,
        dret_gain,
        dq_gain.reshape(FOX_HEADS, FOX_DH).sum(axis=0).reshape(1, FOX_DH),
        dk_gain.reshape(FOX_HEADS, FOX_DH).sum(axis=0).reshape(1, FOX_DH),
        dfb,
    ], axis=1)
    small_all = all_gather("gather_small", [small])[0]
    gw_ret_in = mm_tn("ret_proj_dw", h1_0, dproj, N_DEV, BF16, after=small_all)
    scat_c, token_c = exchange_start("scatter_start_c", [gw_ret_in], scatter=True)
    o_ada = 14 * D
    d_ada = small_all[:, 0, :o_ada]
    d_cat = jnp.concatenate([
        lax.dynamic_slice_in_dim(d_ada[:, 0:6 * D], me * n_ada, n_ada, axis=1),
        lax.dynamic_slice_in_dim(d_ada[:, 6 * D:12 * D], me * n_ada, n_ada, axis=1),
        lax.dynamic_slice_in_dim(d_ada[:, 12 * D:14 * D], me * n_kvada, n_kvada, axis=1)], axis=1)
    gw_ada_cat = mm_tn("ada_dw", c_act, d_cat, 1, F32, after=token_c)[0]

    results = {}

    def update(name, parts, w, m, v, layers=1):
        parts = parts if isinstance(parts, (list, tuple)) else [parts]
        shape = w.shape
        C = shape[-1]
        R = int(np.prod(shape)) // (layers * C)
        outs = adamw("adamw_" + name, [p.reshape(p.shape[0], R, C) for p in parts], w.reshape(layers, R, C),
                     m.reshape(layers, R, C), v.reshape(layers, R, C))
        results[name] = tuple(t.reshape(shape) for t in outs)

    def small_parts(lo, width):
        return small_all[:, :, lo:lo + width]

    update("norm_mix_gain", jnp.concatenate([small_parts(o_ada, D), small_parts(o_ada + D, D)], axis=1),
           norm_mix_gain, m_norm_mix_gain, v_norm_mix_gain)
    update("norm_mlp_gain", jnp.concatenate([small_parts(o_ada + 2 * D, D), small_parts(o_ada + 3 * D, D)], axis=1),
           norm_mlp_gain, m_norm_mlp_gain, v_norm_mlp_gain)
    update("w_ada", [gw_ada_cat[None, :, :n_ada], gw_ada_cat[None, :, n_ada:2 * n_ada]], w_ada, m_w_ada, v_w_ada, layers=2)
    update("b_ada", jnp.concatenate([small_parts(0, 6 * D), small_parts(6 * D, 6 * D)], axis=1), b_ada, m_b_ada, v_b_ada)
    o_ret = o_ada + 5 * D
    n_rg = ret_norm_gain.shape[2]
    ret_gain_parts = small_parts(o_ret, 2 * D).reshape(N_DEV, RET_HEADS, RET_V)
    ret_gain_parts = lax.dynamic_slice_in_dim(ret_gain_parts, me * n_rg, n_rg, axis=2)
    update("ret_norm_gain", ret_gain_parts, ret_norm_gain, m_ret_norm_gain, v_ret_norm_gain)
    update("kv_norm_gain", small_parts(o_ada + 4 * D, D), kv_norm_gain, m_kv_norm_gain, v_kv_norm_gain)
    update("kv_w_ada", gw_ada_cat[None, :, 2 * n_ada:], kv_w_ada, m_kv_w_ada, v_kv_w_ada)
    update("kv_b_ada", small_parts(12 * D, 2 * D), kv_b_ada, m_kv_b_ada, v_kv_b_ada)
    o_q = o_ret + 2 * D
    update("forget_bias", small_parts(o_q + 2 * FOX_DH, FOX_HEADS), forget_bias, m_forget_bias, v_forget_bias)
    update("k_norm_gain", small_parts(o_q + FOX_DH, FOX_DH), k_norm_gain, m_k_norm_gain, v_k_norm_gain)
    update("q_norm_gain", small_parts(o_q, FOX_DH), q_norm_gain, m_q_norm_gain, v_q_norm_gain)

    r_mlp_in1, r_mlp_out1, r_fox_out, r_fox_in, r_kv = exchange_wait(
        "scatter_wait_a", scat_a, results["w_ada"][1], scatter=True)
    r_mlp_in0, r_mlp_out0, r_ret_out = exchange_wait("scatter_wait_b", scat_b, r_kv, scatter=True)
    update("kv_w", r_kv, kv_w, m_kv_w, v_kv_w)
    update("fox_w_in", r_fox_in, fox_w_in, m_fox_w_in, v_fox_w_in)
    update("fox_w_out", r_fox_out, fox_w_out, m_fox_w_out, v_fox_w_out)
    update("ret_w_out", r_ret_out, ret_w_out, m_ret_w_out, v_ret_w_out)
    update("w_mlp_in", [r_mlp_in0, r_mlp_in1], w_mlp_in, m_w_mlp_in, v_w_mlp_in, layers=2)
    update("w_mlp_out", [r_mlp_out0, r_mlp_out1], w_mlp_out, m_w_mlp_out, v_w_mlp_out, layers=2)
    r_ret_in = exchange_wait("scatter_wait_c", scat_c, results["w_mlp_out"][1], scatter=True)[0]
    update("ret_w_in", r_ret_in, ret_w_in, m_ret_w_in, v_ret_w_in)

    order = ["norm_mix_gain", "norm_mlp_gain", "w_ada", "b_ada", "w_mlp_in", "w_mlp_out", "ret_w_in", "ret_norm_gain",
             "ret_w_out", "kv_norm_gain", "kv_w_ada", "kv_b_ada", "kv_w", "forget_bias", "k_norm_gain", "fox_w_in",
             "q_norm_gain", "fox_w_out"]
    out = [loss, grad_x.reshape(x.shape)]
    for slot in range(4):
        out += [results[n][slot] for n in order]
    return tuple(out)
```

```python
import functools
import math

import numpy as np
import jax
import jax.numpy as jnp
from jax import lax
from jax.experimental import pallas as pl
from jax.experimental.pallas import tpu as pltpu

F32 = jnp.float32
BF16 = jnp.bfloat16

N_DEV = 8
D_MODEL = 1024
RET_HEADS = 4
RET_QK = D_MODEL // RET_HEADS
RET_V = 2 * D_MODEL // RET_HEADS
RET_CHUNK = 128
ROPE_BASE = 10000.0
FOX_HEADS = 16
FOX_DH = D_MODEL // FOX_HEADS
EPS = 1e-6
LANES = 128
KV_PAD = 2 * D_MODEL + LANES

ADAM_LR = 0.001
ADAM_B1 = 0.9
ADAM_B2 = 0.999
ADAM_EPS = 1e-08
ADAM_WD = 0.01
ADAM_STEP = 10

VMEM_LIMIT_BYTES = 56 * 1024 * 1024


def _params(sem=None):
    return pltpu.CompilerParams(dimension_semantics=sem, vmem_limit_bytes=VMEM_LIMIT_BYTES)


def _me():
    return lax.axis_index("x"), lax.axis_index("y"), lax.axis_index("c")


def _peer(k):
    x, y, c = _me()
    return (1 - x if k & 4 else x, 1 - y if k & 2 else y, 1 - c if k & 1 else c)


def _peer_index(k):
    px, py, pc = _peer(k)
    return 4 * px + 2 * py + pc


def _exchange(name, xs, scatter):
    n = len(xs)

    def body(*refs):
        x_refs, o_refs = refs[:n], refs[n:2 * n]
        send_sems, recv_sems, local_sems = refs[2 * n:]
        x, y, c = _me()
        me = 4 * x + 2 * y + c
        local = []
        for i in range(n):
            src = x_refs[i].at[me] if scatter else x_refs[i]
            cp = pltpu.make_async_copy(src, o_refs[i].at[me], local_sems.at[i])
            cp.start()
            local.append(cp)
        remote = []
        for k in range(1, N_DEV):
            for i in range(n):
                src = x_refs[i].at[_peer_index(k)] if scatter else x_refs[i]
                cp = pltpu.make_async_remote_copy(
                    src_ref=src, dst_ref=o_refs[i].at[me],
                    send_sem=send_sems.at[(k - 1) * n + i], recv_sem=recv_sems.at[(k - 1) * n + i],
                    device_id=_peer(k), device_id_type=pl.DeviceIdType.MESH)
                cp.start()
                remote.append(cp)
        for cp in remote:
            cp.wait()
        for cp in local:
            cp.wait()

    out_shape = [jax.ShapeDtypeStruct(x.shape if scatter else (N_DEV,) + x.shape, x.dtype) for x in xs]
    any_spec = pl.BlockSpec(memory_space=pl.ANY)
    return pl.pallas_call(
        body, name=name, out_shape=out_shape,
        in_specs=[any_spec] * n, out_specs=[any_spec] * n,
        scratch_shapes=[pltpu.SemaphoreType.DMA(((N_DEV - 1) * n,)),
                        pltpu.SemaphoreType.DMA(((N_DEV - 1) * n,)),
                        pltpu.SemaphoreType.DMA((n,))],
    )(*xs)


def all_gather(name, xs):
    return _exchange(name, xs, scatter=False)


def all_to_all(name, xs):
    return _exchange(name, xs, scatter=True)


_HBM = pl.BlockSpec(memory_space=pltpu.HBM)
_SEM = pl.BlockSpec(memory_space=pltpu.SEMAPHORE)
_ANY = pl.BlockSpec(memory_space=pl.ANY)
_EFFECT = pltpu.SideEffectType.DATAFLOW_SIDE_EFFECTING

SIBLING = 1
CHIP_PEERS = (2, 4, 6)


def _my_index():
    x, y, c = _me()
    return 4 * x + 2 * y + c


def plan_gather(x, land, me):
    return [(x, land.at[me], k) for k in (SIBLING,) + CHIP_PEERS]


def plan_forward(x, land, me):
    return [(x.at[me ^ k], land.at[me ^ k], SIBLING) for k in CHIP_PEERS]


def plan_to_sibling(x, land, me):
    return [(x.at[me ^ SIBLING ^ (2 * q)], land.at[q], SIBLING) for q in range(4)]


def plan_to_owners(x, land, me):
    return [(x.at[q], land.at[q - 1], 2 * q) for q in (1, 2, 3)]


def _plan_copies(plan, x_refs, land_refs, send_sems, recv_sems):
    me = _my_index()
    return [pltpu.make_async_remote_copy(src_ref=src, dst_ref=dst, send_sem=send_sems[i].at[s], recv_sem=recv_sems[i].at[s],
                                         device_id=_peer(k), device_id_type=pl.DeviceIdType.MESH)
            for i in range(len(land_refs)) for s, (src, dst, k) in enumerate(plan(x_refs[i], land_refs[i], me))]


def _landing(block, shape, dtype):
    start = (_my_index(),) + (0,) * (len(shape) - 1)
    return lax.dynamic_update_slice(lax.empty(shape, dtype), block[None], start)


def exchange_now(name, xs, lands, plan, n_copies):
    n = len(lands)
    n_x = 0 if xs is None else n

    def body(*refs):
        land_in, land_out = refs[n_x:n_x + n], refs[n_x + n:n_x + 2 * n]
        x_refs = land_in if xs is None else refs[:n]
        sems = refs[n_x + 2 * n:]
        copies = _plan_copies(plan, x_refs, land_out, sems[:n], sems[n:])
        for cp in copies:
            cp.start()
        for cp in copies:
            cp.wait()

    return pl.pallas_call(
        body, name=name, out_shape=[jax.ShapeDtypeStruct(a.shape, a.dtype) for a in lands],
        in_specs=[_ANY] * (n_x + n), out_specs=[_ANY] * n,
        input_output_aliases={n_x + i: i for i in range(n)},
        scratch_shapes=[pltpu.SemaphoreType.DMA((n_copies,))] * (2 * n),
    )(*([] if xs is None else xs), *lands)


def exchange_start(name, xs, lands, plan, n_copies, after=()):
    n, m = len(xs), len(after)

    def body(*refs):
        x_refs, land_refs = refs[:n], refs[n:2 * n]
        send_sems, recv_sems = refs[2 * n + m:3 * n + m], refs[3 * n + m:4 * n + m]
        token = refs[6 * n + m]
        for cp in _plan_copies(plan, x_refs, land_refs, send_sems, recv_sems):
            cp.start()
        token[...] = jnp.zeros_like(token)

    sems = [pltpu.SemaphoreType.DMA((n_copies,))] * (2 * n)
    thru = [pltpu.HBM(a.shape, a.dtype) for a in list(xs) + list(lands)]
    res = pl.pallas_call(
        body, name=name,
        out_shape=sems + thru + [jax.ShapeDtypeStruct((8, LANES), F32)],
        in_specs=[_HBM] * (2 * n) + [_ANY] * m,
        out_specs=[_SEM] * (2 * n) + [_HBM] * (2 * n) + [pl.BlockSpec(memory_space=pltpu.VMEM)],
        input_output_aliases={i: 2 * n + i for i in range(2 * n)},
        compiler_params=pltpu.CompilerParams(has_side_effects=_EFFECT),
    )(*[pltpu.with_memory_space_constraint(a, pltpu.HBM) for a in list(xs) + list(lands)], *after)
    handles = [(res[i], res[n + i], res[2 * n + i], res[3 * n + i]) for i in range(n)]
    return handles, res[4 * n]


def exchange_wait(name, handles, plan, after):
    n = len(handles)

    def body(*refs):
        x_refs, land_refs = refs[:n], refs[n:2 * n]
        send_sems, recv_sems = refs[2 * n:3 * n], refs[3 * n:4 * n]
        for cp in _plan_copies(plan, x_refs, land_refs, send_sems, recv_sems):
            cp.wait_send()
            cp.wait_recv()

    xs = [h[2] for h in handles]
    lands = [h[3] for h in handles]
    res = pl.pallas_call(
        body, name=name,
        out_shape=[pltpu.HBM(a.shape, a.dtype) for a in xs + lands],
        in_specs=[_HBM] * (2 * n) + [_SEM] * (2 * n) + [_ANY],
        out_specs=[_HBM] * (2 * n),
        input_output_aliases={i: i for i in range(2 * n)},
        compiler_params=pltpu.CompilerParams(has_side_effects=_EFFECT),
    )(*xs, *lands, *[h[0] for h in handles], *[h[1] for h in handles], after)
    return res[:n], res[n:]


def chip_sum(name, gw, from_sibling):
    _, R, C = gw.shape
    tr = _row_tile(R, 512)
    me = _my_index().astype(jnp.int32).reshape(1)

    def body(me_ref, g_ref, s_ref, o_ref):
        o_ref[...] = (g_ref[...].astype(F32) + s_ref[...].astype(F32)).astype(o_ref.dtype)

    slot = pl.BlockSpec((None, tr, C), lambda q, i, me_ref: (q, i, 0))
    return pl.pallas_call(
        body, name=name, out_shape=jax.ShapeDtypeStruct((4, R, C), BF16),
        grid_spec=pltpu.PrefetchScalarGridSpec(
            num_scalar_prefetch=1, grid=(4, R // tr),
            in_specs=[pl.BlockSpec((None, tr, C), lambda q, i, me_ref: (me_ref[0] ^ (2 * q), i, 0)), slot],
            out_specs=slot),
        compiler_params=_params(("arbitrary", "arbitrary")),
    )(me, gw, from_sibling)


def _tile(n, cap):
    best = None
    for t in range(LANES, min(n, cap) + 1, LANES):
        if n % t == 0:
            best = t
    if best is None or (best < 256 and n <= 2304):
        return n
    return best


def _row_tile(m, cap):
    if m <= cap:
        return m
    t = cap
    while m % t:
        t //= 2
    return t


def _after_spec(after):
    return [] if after is None else [pl.BlockSpec(memory_space=pl.ANY)]


def _after_arg(after):
    return [] if after is None else [after]


def _mm_call(name, dims, grid, a_spec, b_spec, o_spec, o_shape, tile, a, b, out_dtypes, epilogue, extra, after):
    nk = grid[2]
    n_x, n_o = len(extra), len(out_dtypes)

    def body(a_ref, b_ref, *refs):
        x_refs, o_refs, acc_ref = refs[:n_x], refs[len(refs) - 1 - n_o:len(refs) - 1], refs[-1]
        k = pl.program_id(2)

        @pl.when(k == 0)
        def _():
            acc_ref[...] = jnp.zeros_like(acc_ref)

        acc_ref[...] += lax.dot_general(a_ref[...].astype(BF16), b_ref[...].astype(BF16), (dims, ((), ())),
                                        preferred_element_type=F32)

        @pl.when(k == nk - 1)
        def _():
            acc = acc_ref[...]
            vals = (acc,) if epilogue is None else epilogue(acc, *[x[...] for x in x_refs])
            for o_ref, val in zip(o_refs, vals):
                o_ref[...] = val.astype(o_ref.dtype)

    return pl.pallas_call(
        body, name=name, out_shape=[jax.ShapeDtypeStruct(o_shape, dt) for dt in out_dtypes], grid=grid,
        in_specs=[a_spec, b_spec] + [o_spec] * n_x + _after_spec(after), out_specs=[o_spec] * n_o,
        scratch_shapes=[pltpu.VMEM(tile, F32)],
        compiler_params=_params(("parallel", "parallel", "arbitrary")),
    )(a, b, *extra, *_after_arg(after))


def mm_nn(name, a, w, out_dtypes=(F32,), epilogue=None, extra=(), after=None):
    M, K = a.shape
    G, _, n = w.shape
    tn = _tile(n, 1024)
    tm = _row_tile(M, 512 if tn > 1024 else 1024)
    tk = _row_tile(K, 1024)
    r = n // tn
    return _mm_call(
        name, ((1,), (0,)), (M // tm, G * r, K // tk),
        pl.BlockSpec((tm, tk), lambda i, j, k: (i, k)),
        pl.BlockSpec((None, tk, tn), lambda i, j, k: (j // r, k, j % r)),
        pl.BlockSpec((tm, tn), lambda i, j, k: (i, j)), (M, G * n), (tm, tn),
        a, w, out_dtypes, epilogue, extra, after)


def mm_nt(name, dy, w, out_dtypes=(F32,), epilogue=None, extra=(), after=None):
    M, N = dy.shape
    G, K, n = w.shape
    tn = _tile(n, 1024)
    tm = _row_tile(M, 512 if tn > 1024 else 1024)
    tk = _row_tile(K, 1024)
    r = n // tn
    return _mm_call(
        name, ((1,), (1,)), (M // tm, K // tk, G * r),
        pl.BlockSpec((tm, tn), lambda i, j, k: (i, k)),
        pl.BlockSpec((None, tk, tn), lambda i, j, k: (k // r, j, k % r)),
        pl.BlockSpec((tm, tk), lambda i, j, k: (i, j)), (M, K), (tm, tk),
        dy, w, out_dtypes, epilogue, extra, after)


def mm_tn(name, a, dy, G, out_dtype=F32, after=None):
    M, K = a.shape
    n = dy.shape[1] // G
    tn = _tile(n, 1024)
    tk = _row_tile(K, 512 if tn > 1024 else 1024)
    tm = _row_tile(M, 1024)
    r = n // tn
    return _mm_call(
        name, ((0,), (0,)), (K // tk, G * r, M // tm),
        pl.BlockSpec((tm, tk), lambda i, j, k: (k, i)),
        pl.BlockSpec((tm, tn), lambda i, j, k: (k, j)),
        pl.BlockSpec((None, tk, tn), lambda i, j, k: (j // r, i, j % r)), (G, K, n), (tk, tn),
        a, dy, (out_dtype,), None, (), after)[0]


def rowwise(name, fn, rows, vecs, outs, accs=(), tm=256, after=None):
    rows = [r if isinstance(r, tuple) else (r, r.shape[1], 0) for r in rows]
    n_fn = len(rows) + len(vecs)
    vecs = list(vecs) + _after_arg(after)
    S = rows[0][0].shape[0]
    tm = _row_tile(S, tm)
    n_r, n_v, n_o, n_a = len(rows), len(vecs), len(outs), len(accs)

    def body(*refs):
        ins = [ref[...] for ref in refs[:n_r + n_v]]
        o_refs = refs[n_r + n_v:n_r + n_v + n_o]
        a_refs = refs[n_r + n_v + n_o:]
        res = fn(*ins[:n_fn])
        res = res if isinstance(res, (tuple, list)) else (res,)
        for ref, val in zip(o_refs, res[:n_o]):
            ref[...] = val.astype(ref.dtype)
        if n_a:
            @pl.when(pl.program_id(0) == 0)
            def _():
                for ref in a_refs:
                    ref[...] = jnp.zeros_like(ref)
            for ref, val in zip(a_refs, res[n_o:]):
                ref[...] += val

    in_specs = [pl.BlockSpec((tm, w), functools.partial(lambda cb, i: (i, cb), cb)) for _, w, cb in rows]
    in_specs += [pl.BlockSpec(v.shape, lambda i: (0, 0)) for v in vecs]
    out_specs = [pl.BlockSpec((tm, w), lambda i: (i, 0)) for w, _ in outs]
    out_specs += [pl.BlockSpec(a, lambda i: (0, 0)) for a in accs]
    out_shape = [jax.ShapeDtypeStruct((S, w), dt) for w, dt in outs]
    out_shape += [jax.ShapeDtypeStruct(a, F32) for a in accs]
    res = pl.pallas_call(
        body, name=name, out_shape=out_shape, grid=(S // tm,),
        in_specs=in_specs, out_specs=out_specs,
        compiler_params=_params(("arbitrary",)),
    )(*[r[0] for r in rows], *vecs)
    return res


def _rms(x):
    return x * lax.rsqrt(jnp.mean(x * x, axis=-1, keepdims=True) + EPS)


def _modulate(x, gain, scale, shift):
    return _rms(x) * gain * (1.0 + scale) + shift


def _ada_slices(ada_raw, bias):
    ada = ada_raw + bias
    return [ada[:, i * D_MODEL:(i + 1) * D_MODEL] for i in range(ada.shape[1] // D_MODEL)]


def _norm_wide_heads(y, gain, heads):
    w = y.shape[1] // heads
    return jnp.concatenate([_rms(y[:, h * w:(h + 1) * w]) * gain[:, h * w:(h + 1) * w] for h in range(heads)], axis=1)


def _norm_fox_heads(x, gain):
    outs = []
    for p in range(x.shape[1] // LANES):
        blk = x[:, p * LANES:(p + 1) * LANES]
        low = lax.broadcasted_iota(jnp.int32, blk.shape, 1) < FOX_DH
        sq = blk * blk
        ss_low = jnp.sum(jnp.where(low, sq, 0.0), axis=1, keepdims=True)
        ss_high = jnp.sum(jnp.where(low, 0.0, sq), axis=1, keepdims=True)
        outs.append(blk * lax.rsqrt(jnp.where(low, ss_low, ss_high) * (1.0 / FOX_DH) + EPS))
    return jnp.concatenate(outs, axis=1) * gain


def _silu(x):
    return x * jax.nn.sigmoid(x)


def _log_sigmoid(z):
    return -(jnp.maximum(-z, 0.0) + jnp.log(1.0 + jnp.exp(-jnp.abs(z))))


def _rotate(x, cos, sin, heads, sign):
    w = x.shape[1] // heads
    half = w // 2
    outs = []
    for h in range(heads):
        x1 = x[:, h * w:h * w + half]
        x2 = x[:, h * w + half:(h + 1) * w]
        outs += [x1 * cos - sign * x2 * sin, sign * x1 * sin + x2 * cos]
    return jnp.concatenate(outs, axis=1)


def _vjp(fn, primals, ct):
    return jax.vjp(fn, *primals)[1](ct)


_LOG_GAMMAS = [float(np.log(np.float32(1.0) - np.float32(2.0) ** np.float32(-5.0 - h))) for h in range(RET_HEADS)]


RET_ROWS = 512


def retention(name, q, k, v, reverse):
    (qa, dk, qo), (ka, _, ko), (va, dv, vo) = q, k, v
    S = qa.shape[0]
    C = RET_CHUNK
    rows = _row_tile(S, RET_ROWS)
    nb = S // rows

    def body(q_ref, k_ref, v_ref, o_ref, state):
        h = pl.program_id(0)

        @pl.when(pl.program_id(1) == 0)
        def _():
            state[...] = jnp.zeros_like(state)

        log_g = jnp.float32(_LOG_GAMMAS[RET_HEADS - 1])
        for i in range(RET_HEADS - 2, -1, -1):
            log_g = jnp.where(h == i, jnp.float32(_LOG_GAMMAS[i]), log_g)
        row = lax.broadcasted_iota(jnp.int32, (C, C), 0)
        col = lax.broadcasted_iota(jnp.int32, (C, C), 1)
        rel = (col - row if reverse else row - col).astype(F32)
        decay = jnp.where(rel >= 0, jnp.exp(log_g * jnp.maximum(rel, 0.0)), 0.0)
        j = lax.broadcasted_iota(jnp.int32, (C, 1), 0).astype(F32)
        q_decay = jnp.exp(log_g * (C - j if reverse else j + 1.0))
        k_decay = jnp.exp(log_g * (j if reverse else C - 1.0 - j))
        chunk_decay = jnp.exp(jnp.full((1, 1), log_g * C, F32))

        chunks = range(rows // C)
        for ci in (reversed(chunks) if reverse else chunks):
            rs = slice(ci * C, (ci + 1) * C)
            qc = q_ref[rs, :].astype(BF16)
            kf = k_ref[rs, :].astype(F32)
            vc = v_ref[rs, :].astype(BF16)
            scores = lax.dot_general(qc, kf.astype(BF16), (((1,), (1,)), ((), ())), preferred_element_type=F32) * decay
            intra = jnp.dot(scores.astype(BF16), vc, preferred_element_type=F32)
            cross = jnp.dot(qc, state[...].astype(BF16), preferred_element_type=F32) * q_decay
            o_ref[rs, :] = intra + cross
            upd = lax.dot_general((kf * k_decay).astype(BF16), vc, (((0,), (0,)), ((), ())), preferred_element_type=F32)
            state[...] = state[...] * chunk_decay + upd

    def block(i):
        return nb - 1 - i if reverse else i

    return pl.pallas_call(
        body, name=name, out_shape=jax.ShapeDtypeStruct((S, RET_HEADS * dv), F32),
        grid=(RET_HEADS, nb),
        in_specs=[pl.BlockSpec((rows, dk), lambda h, i: (block(i), qo + h)),
                  pl.BlockSpec((rows, dk), lambda h, i: (block(i), ko + h)),
                  pl.BlockSpec((rows, dv), lambda h, i: (block(i), vo + h))],
        out_specs=pl.BlockSpec((rows, dv), lambda h, i: (block(i), h)),
        scratch_shapes=[pltpu.VMEM((dk, dv), F32)],
        compiler_params=_params(("parallel", "arbitrary")),
    )(qa, ka, va)


FOX_T = 256
N_PAIR = FOX_HEADS // 2
FOX_SCALE = FOX_DH ** -0.5


def _fox_heads(q2):
    low = lax.broadcasted_iota(jnp.int32, (1, LANES), 1) < FOX_DH
    return [(mask, jnp.where(mask, q2 * FOX_SCALE, 0.0).astype(BF16)) for mask in (low, jnp.logical_not(low))]


def _fox_parts(j, t):
    return ([(0, j * t, False)] if j else []) + [(j * t, (j + 1) * t, True)]


def _fox_scores(qa, k_ref, ft_ref, head, lo, hi, diagonal):
    k_blk = k_ref[lo:hi, :].astype(BF16)
    s = lax.dot_general(qa, k_blk, (((1,), (1,)), ((), ())), preferred_element_type=F32) - ft_ref[pl.ds(head, 1), lo:hi]
    if diagonal:
        n = hi - lo
        s = jnp.where(lax.broadcasted_iota(jnp.int32, (n, n), 1) <= lax.broadcasted_iota(jnp.int32, (n, n), 0), s, -jnp.inf)
    return s


def fox_forward(name, qn, kn, kvf, f_cum_t):
    S = qn.shape[0]
    t = _row_tile(S, FOX_T)
    v_block0 = D_MODEL // LANES

    def variant(j, pair, q_ref, k_ref, v_ref, ft_ref, y_ref, lse_ref):
        ys, lses = [], []
        for a, (mask, qa) in enumerate(_fox_heads(q_ref[...])):
            parts = [(lo, hi, _fox_scores(qa, k_ref, ft_ref, 2 * pair + a, lo, hi, dg)) for lo, hi, dg in _fox_parts(j, t)]
            m = functools.reduce(jnp.maximum, [jnp.max(s, axis=1, keepdims=True) for _, _, s in parts])
            l, acc = 0.0, 0.0
            for lo, hi, s in parts:
                e = jnp.exp(s - m)
                l = l + jnp.sum(e, axis=1, keepdims=True)
                acc = acc + jnp.dot(e.astype(BF16), v_ref[lo:hi, :].astype(BF16), preferred_element_type=F32)
            ys.append(acc / l)
            lses.append(m + jnp.log(l))
        low = lax.broadcasted_iota(jnp.int32, (1, LANES), 1) < FOX_DH
        y_ref[...] = jnp.where(low, ys[0], ys[1])
        lse_ref[...] = jnp.where(low, lses[0], lses[1])

    def body(*refs):
        pair, i = pl.program_id(0), pl.program_id(1)
        for j in range(S // t):
            pl.when(i == j)(functools.partial(variant, j, pair, *refs))

    return pl.pallas_call(
        body, name=name,
        out_shape=[jax.ShapeDtypeStruct((S, D_MODEL), F32), jax.ShapeDtypeStruct((S, D_MODEL), F32)],
        grid=(N_PAIR, S // t),
        in_specs=[pl.BlockSpec((t, LANES), lambda p, i: (i, p)),
                  pl.BlockSpec((S, LANES), lambda p, i: (0, p)),
                  pl.BlockSpec((S, LANES), lambda p, i: (0, v_block0 + p)),
                  pl.BlockSpec((LANES, S), lambda p, i: (0, 0))],
        out_specs=[pl.BlockSpec((t, LANES), lambda p, i: (i, p)),
                   pl.BlockSpec((t, LANES), lambda p, i: (i, p))],
        compiler_params=_params(("parallel", "arbitrary")),
    )(qn, kn, kvf, f_cum_t)


def fox_backward(name, qn, kn, kvf, f_cum_t, y, dy, lse):
    S = qn.shape[0]
    t = _row_tile(S, FOX_T)
    v_block0 = D_MODEL // LANES

    def variant(j, pair, q_ref, k_ref, v_ref, ft_ref, y_ref, dy_ref, lse_ref, dq_ref, dk_ref, dv_ref, dfq_ref, dfk_ref):
        y2, dy2, lse2 = y_ref[...], dy_ref[...], lse_ref[...]
        lane = lax.broadcasted_iota(jnp.int32, (t, LANES), 1)
        dqs, dfq = [], jnp.zeros((t, LANES), F32)
        for a, (mask, qa) in enumerate(_fox_heads(q_ref[...])):
            lse_a = jnp.max(jnp.where(mask, lse2, -jnp.inf), axis=1, keepdims=True)
            dy_a = jnp.where(mask, dy2, 0.0)
            delta = jnp.sum(dy_a * y2, axis=1, keepdims=True)
            dy_b = dy_a.astype(BF16)
            dq, row_sum = 0.0, 0.0
            for lo, hi, dg in _fox_parts(j, t):
                p = jnp.exp(_fox_scores(qa, k_ref, ft_ref, 2 * pair + a, lo, hi, dg) - lse_a)
                dp = lax.dot_general(dy_b, v_ref[lo:hi, :].astype(BF16), (((1,), (1,)), ((), ())), preferred_element_type=F32)
                ds = p * (dp - delta)
                row_sum = row_sum + jnp.sum(ds, axis=1, keepdims=True)
                dfk_ref[pl.ds(a, 1), lo:hi] += -jnp.sum(ds, axis=0, keepdims=True)
                ds_b = ds.astype(BF16)
                dq = dq + jnp.dot(ds_b, k_ref[lo:hi, :].astype(BF16), preferred_element_type=F32)
                dk_ref[lo:hi, :] += lax.dot_general(ds_b, qa, (((0,), (0,)), ((), ())), preferred_element_type=F32)
                dv_ref[lo:hi, :] += lax.dot_general(p.astype(BF16), dy_b, (((0,), (0,)), ((), ())), preferred_element_type=F32)
            dqs.append(dq * FOX_SCALE)
            dfq = dfq + jnp.where(lane == 2 * pair + a, row_sum, 0.0)
        low = lax.broadcasted_iota(jnp.int32, (1, LANES), 1) < FOX_DH
        dq_ref[...] = jnp.where(low, dqs[0], dqs[1])
        dfq_ref[...] = dfq

    def body(*refs):
        pair, i = pl.program_id(0), pl.program_id(1)
        dk_ref, dv_ref, dfk_ref = refs[8], refs[9], refs[11]

        @pl.when(i == 0)
        def _():
            dk_ref[...] = jnp.zeros_like(dk_ref)
            dv_ref[...] = jnp.zeros_like(dv_ref)
            dfk_ref[...] = jnp.zeros_like(dfk_ref)

        for j in range(S // t):
            pl.when(i == j)(functools.partial(variant, j, pair, *refs))

    row_blk = pl.BlockSpec((t, LANES), lambda p, i: (i, p))
    col_blk = pl.BlockSpec((S, LANES), lambda p, i: (0, p))
    return pl.pallas_call(
        body, name=name,
        out_shape=[jax.ShapeDtypeStruct((S, D_MODEL), F32)] * 3
        + [jax.ShapeDtypeStruct((N_PAIR, S, LANES), F32), jax.ShapeDtypeStruct((N_PAIR, 8, S), F32)],
        grid=(N_PAIR, S // t),
        in_specs=[row_blk, col_blk,
                  pl.BlockSpec((S, LANES), lambda p, i: (0, v_block0 + p)),
                  pl.BlockSpec((LANES, S), lambda p, i: (0, 0)),
                  row_blk, row_blk, row_blk],
        out_specs=[row_blk, col_blk, col_blk,
                   pl.BlockSpec((None, t, LANES), lambda p, i: (p, i, 0)),
                   pl.BlockSpec((None, 8, S), lambda p, i: (p, 0, 0))],
        compiler_params=_params(("parallel", "arbitrary")),
    )(qn, kn, kvf, f_cum_t, y, dy, lse)


def cumsum_rows(name, x, reverse):
    S = x.shape[0]
    C = LANES
    nc = S // C

    def body(x_ref, o_ref):
        row = lax.broadcasted_iota(jnp.int32, (C, C), 0)
        col = lax.broadcasted_iota(jnp.int32, (C, C), 1)
        tri = jnp.where(col >= row if reverse else col <= row, 1.0, 0.0).astype(F32)
        carry = jnp.zeros((1, LANES), F32)
        for i in (range(nc - 1, -1, -1) if reverse else range(nc)):
            blk = x_ref[i * C:(i + 1) * C, :]
            loc = jnp.dot(tri, blk, preferred_element_type=F32, precision=lax.Precision.HIGHEST)
            o_ref[i * C:(i + 1) * C, :] = loc + carry
            carry = carry + (loc[0:1, :] if reverse else loc[C - 1:C, :])

    return pl.pallas_call(body, name=name, out_shape=jax.ShapeDtypeStruct((S, LANES), F32),
                          compiler_params=_params())(x)


def adamw(name, parts, w, m, v):
    L, R, C = w.shape
    tr = _row_tile(R, 256)
    nr = R // tr
    counts = [len(p) for p in parts]

    def body(*refs):
        w_ref, m_ref, v_ref, g_out, d_out, m_out, v_out = refs[sum(counts):]
        for layer in range(L):
            p_refs = refs[sum(counts[:layer]):sum(counts[:layer + 1])]

            @pl.when(pl.program_id(0) == layer)
            def _(p_refs=p_refs, slots=[n for _, n in parts[layer]]):
                g = None
                for p_ref, n in zip(p_refs, slots):
                    for i in range(n):
                        g = p_ref[i].astype(F32) if g is None else g + p_ref[i].astype(F32)
                m2 = ADAM_B1 * m_ref[...] + (1.0 - ADAM_B1) * g
                v2 = ADAM_B2 * v_ref[...] + (1.0 - ADAM_B2) * jnp.square(g)
                m_hat = m2 / (1.0 - ADAM_B1 ** ADAM_STEP)
                v_hat = v2 / (1.0 - ADAM_B2 ** ADAM_STEP)
                g_out[...] = g
                d_out[...] = -ADAM_LR * (m_hat / (jnp.sqrt(v_hat) + ADAM_EPS) + ADAM_WD * w_ref[...])
                m_out[...] = m2
                v_out[...] = v2

    def part_spec(layer, n):
        return pl.BlockSpec((n, tr, C), lambda l, i: (0, jnp.where(l == layer, i, jnp.where(l < layer, 0, nr - 1)), 0))

    blk = pl.BlockSpec((None, tr, C), lambda l, i: (l, i, 0))
    return pl.pallas_call(
        body, name=name, out_shape=[jax.ShapeDtypeStruct((L, R, C), F32)] * 4, grid=(L, nr),
        in_specs=[part_spec(layer, n) for layer in range(L) for _, n in parts[layer]] + [blk, blk, blk],
        out_specs=[blk] * 4, compiler_params=_params(("arbitrary", "arbitrary")),
    )(*[a for layer in parts for a, _ in layer], w, m, v)


def kernel(x, c, positions, norm_mix_gain, norm_mlp_gain, w_ada, b_ada, w_mlp_in, w_mlp_out, ret_w_in, ret_norm_gain, ret_w_out, kv_norm_gain, kv_w_ada, kv_b_ada, kv_w, forget_bias, k_norm_gain, fox_w_in, q_norm_gain, fox_w_out, loss_target, m_norm_mix_gain, m_norm_mlp_gain, m_w_ada, m_b_ada, m_w_mlp_in, m_w_mlp_out, m_ret_w_in, m_ret_norm_gain, m_ret_w_out, m_kv_norm_gain, m_kv_w_ada, m_kv_b_ada, m_kv_w, m_forget_bias, m_k_norm_gain, m_fox_w_in, m_q_norm_gain, m_fox_w_out, v_norm_mix_gain, v_norm_mlp_gain, v_w_ada, v_b_ada, v_w_mlp_in, v_w_mlp_out, v_ret_w_in, v_ret_norm_gain, v_ret_w_out, v_kv_norm_gain, v_kv_w_ada, v_kv_b_ada, v_kv_w, v_forget_bias, v_k_norm_gain, v_fox_w_in, v_q_norm_gain, v_fox_w_out):
    D = D_MODEL
    S = x.shape[1]
    x0 = x.reshape(S, D)
    target = loss_target.reshape(S, D)
    me = 4 * lax.axis_index("x") + 2 * lax.axis_index("y") + lax.axis_index("c")
    n_ada = w_ada.shape[2]
    n_kvada = kv_w_ada.shape[1]
    n_kv = kv_w.shape[1]

    c_all, ret_gain = all_gather("gather_c", [c.reshape(1, D), ret_norm_gain.reshape(RET_HEADS, -1)])
    ret_gain = jnp.transpose(ret_gain, (1, 0, 2)).reshape(1, RET_HEADS * RET_V)
    c_act = rowwise("silu_c", _silu, [c_all.reshape(N_DEV, D)], [], [(D, F32)])[0]
    w_ada_cat = jnp.concatenate([w_ada[0], w_ada[1], kv_w_ada], axis=1).astype(BF16)[None]
    n_cat = 2 * n_ada + n_kvada
    ada_part = mm_nn("ada_proj", c_act, w_ada_cat)[0]
    ada_mine = all_to_all("ada_rows", [ada_part.reshape(N_DEV, 1, n_cat)])[0][:, 0]
    ada_raw = [ada_mine[:, l * n_ada:(l + 1) * n_ada].reshape(1, 6 * D) for l in range(2)]
    kvada_raw = ada_mine[:, 2 * n_ada:].reshape(1, 2 * D)
    kv_bias = kv_b_ada.reshape(1, 2 * D)
    kv_gain = kv_norm_gain.reshape(1, D)
    fb = jnp.pad(forget_bias.reshape(1, FOX_HEADS), ((0, 0), (0, LANES - FOX_HEADS)))
    k_gain = jnp.tile(k_norm_gain.reshape(1, FOX_DH), (1, FOX_HEADS))
    q_gain = jnp.tile(q_norm_gain.reshape(1, FOX_DH), (1, FOX_HEADS))

    w_names = ["ret_in", "ret_out", "mlp_in0", "mlp_out0", "kv", "fox_in", "fox_out", "mlp_in1", "mlp_out1"]
    shards = [ret_w_in[0].astype(BF16), ret_w_out[0].astype(BF16), w_mlp_in[0].astype(BF16), w_mlp_out[0].astype(BF16),
              kv_w.astype(BF16), fox_w_in[0].astype(BF16), fox_w_out[0].astype(BF16), w_mlp_in[1].astype(BF16),
              w_mlp_out[1].astype(BF16)]
    w_handles, token = exchange_start("gather_weights_start", shards,
                                      [_landing(a, (N_DEV,) + a.shape, a.dtype) for a in shards], plan_gather, 4,
                                      after=(ada_mine, ret_gain))
    w_handles = dict(zip(w_names, w_handles))

    def weight(name, after):
        arrived = exchange_wait("gather_wait_" + name, [w_handles[name]], plan_gather, after)[1]
        return exchange_now("gather_forward_" + name, None, arrived, plan_forward, 3)[0]

    pos = positions.reshape(S, 1).astype(F32)
    half = RET_QK // 2
    inv_freq = jnp.asarray((ROPE_BASE ** (-np.arange(half, dtype=np.float32) / half)).reshape(1, half), F32)

    def angles(p, f):
        ang = p * f
        return jnp.cos(ang), jnp.sin(ang)

    cos, sin = rowwise("rope_table", angles, [pos], [inv_freq], [(half, F32), (half, F32)])

    def mod_mix(layer):
        def fn(xb, ada, bias, gain):
            sh, sc = _ada_slices(ada, bias)[:2]
            return _modulate(xb, gain[layer:layer + 1], sc, sh)
        return fn

    def mod_mlp(layer):
        def fn(xb, ada, bias, gain):
            sh, sc = _ada_slices(ada, bias)[3:5]
            return _modulate(xb, gain[layer:layer + 1], sc, sh)
        return fn

    h1_0 = rowwise("mod_mix0", mod_mix(0), [x0], [ada_raw[0], b_ada[0:1], norm_mix_gain], [(D, BF16)], after=token)[0]
    W_ret_in = weight("ret_in", h1_0)
    proj = mm_nn("ret_proj", h1_0, W_ret_in)[0]

    def rope_fwd(qb, kb, cs, sn):
        return _rotate(qb, cs, sn, RET_HEADS, 1.0), _rotate(kb, cs, sn, RET_HEADS, 1.0) * (RET_QK ** -0.5)

    q_rot, k_rot = rowwise("rope", rope_fwd, [(proj, D, 0), (proj, D, 1), cos, sin], [], [(D, F32), (D, F32)])
    v_ret = (proj, RET_V, (2 * D) // RET_V)
    y_ret = retention("ret_fwd", (q_rot, RET_QK, 0), (k_rot, RET_QK, 0), v_ret, reverse=False)

    def ret_gate(yb, gb, gain):
        return _silu(gb) * _norm_wide_heads(yb, gain, RET_HEADS)

    mixin0 = rowwise("ret_gate", ret_gate, [y_ret, (proj, 2 * D, 2)], [ret_gain], [(2 * D, BF16)])[0]
    W_ret_out = weight("ret_out", mixin0).reshape(1, 2 * D, D)
    mix0 = mm_nn("ret_out", mixin0, W_ret_out)[0]

    def residual_mod(layer, slot):
        def fn(xb, bb, ada, bias, gain):
            s = _ada_slices(ada, bias)
            xn = xb + s[2] * bb
            return xn, _modulate(xn, gain[layer:layer + 1], s[4], s[3])
        return fn

    x1, h2_0 = rowwise("res_mix0", residual_mod(0, 0), [x0, mix0], [ada_raw[0], b_ada[0:1], norm_mlp_gain],
                       [(D, F32), (D, BF16)])

    W_mlp_in, W_mlp_out = {}, {}

    def mlp_forward(tag, h2, layer):
        W_mlp_in[layer] = weight("mlp_in" + tag, h2)
        u, act = mm_nn("mlp_in" + tag, h2, W_mlp_in[layer], (BF16, BF16),
                       epilogue=lambda acc: (acc, jnp.square(jnp.maximum(acc, 0.0))))
        W_mlp_out[layer] = weight("mlp_out" + tag, act).reshape(1, 4 * D, D)
        return u, act, mm_nn("mlp_out" + tag, act, W_mlp_out[layer])[0]

    u0, act0, mlp0 = mlp_forward("0", h2_0, 0)

    def res_mlp0(xb, bb, ada0, bias0, ada1, bias1, kva, kvb, gain_mix, gain_kv):
        xn = xb + _ada_slices(ada0, bias0)[5] * bb
        s1 = _ada_slices(ada1, bias1)
        kv_shift, kv_scale = _ada_slices(kva, kvb)
        return xn, _modulate(xn, gain_kv, kv_scale, kv_shift), _modulate(xn, gain_mix[1:2], s1[1], s1[0])

    x2, h_kv, h1_1 = rowwise("res_mlp0", res_mlp0, [x1, mlp0],
                             [ada_raw[0], b_ada[0:1], ada_raw[1], b_ada[1:2], kvada_raw, kv_bias, norm_mix_gain, kv_gain],
                             [(D, F32), (D, BF16), (D, BF16)])

    kv_full = jnp.transpose(weight("kv", h_kv), (1, 0, 2)).reshape(D, N_DEV * n_kv)
    W_kv = jnp.pad(kv_full, ((0, 0), (0, KV_PAD - N_DEV * n_kv)))[None]
    kvf = mm_nn("kv_proj", h_kv, W_kv)[0]

    def kv_post(kb, fblk, kg, bias):
        head = lax.broadcasted_iota(jnp.int32, fblk.shape, 1) < FOX_HEADS
        return _norm_fox_heads(kb, kg), jnp.where(head, _log_sigmoid(fblk + bias), 0.0)

    kn, log_f = rowwise("kv_post", kv_post, [(kvf, D, 0), (kvf, LANES, 2 * D // LANES)], [k_gain, fb],
                        [(D, F32), (LANES, F32)])
    f_cum = cumsum_rows("f_cumsum", log_f, reverse=False)
    f_cum_t = f_cum.T

    W_fox_in = weight("fox_in", kvf)
    qo = mm_nn("fox_proj", h1_1, W_fox_in)[0]
    qn = rowwise("q_norm", _norm_fox_heads, [(qo, D, 0)], [q_gain], [(D, F32)])[0]
    y_att, lse = fox_forward("fox_fwd", qn, kn, kvf, f_cum_t)
    mixin1 = rowwise("fox_gate", lambda ob, yb: jax.nn.sigmoid(ob) * yb, [(qo, D, 1), y_att], [], [(D, BF16)])[0]
    W_fox_out = weight("fox_out", mixin1).reshape(1, D, D)
    mix1 = mm_nn("fox_out", mixin1, W_fox_out)[0]
    x3, h2_1 = rowwise("res_mix1", residual_mod(1, 0), [x2, mix1], [ada_raw[1], b_ada[1:2], norm_mlp_gain],
                       [(D, F32), (D, BF16)])
    u1, act1, mlp1 = mlp_forward("1", h2_1, 1)

    def scatter_start(tag, gws, after=()):
        lands = [lax.empty((4,) + g.shape[1:], g.dtype) for g in gws]
        return exchange_start("scatter_sibling_start_" + tag, gws, lands, plan_to_sibling, 4, after)

    def scatter_relay(tag, handles, after):
        gws, from_sibling = exchange_wait("scatter_sibling_wait_" + tag, handles, plan_to_sibling, after)
        sums = [chip_sum("chip_sum_%s%d" % (tag, i), g, s) for i, (g, s) in enumerate(zip(gws, from_sibling))]
        lands = [lax.empty((3,) + s.shape[1:], s.dtype) for s in sums]
        return exchange_start("scatter_owner_start_" + tag, sums, lands, plan_to_owners, 3)

    def scatter_finish(tag, handles, after):
        sums, received = exchange_wait("scatter_owner_wait_" + tag, handles, plan_to_owners, after)
        return [[(s, 1), (r, 3)] for s, r in zip(sums, received)]

    def loss_head(xb, bb, tb, ada, bias):
        g2 = _ada_slices(ada, bias)[5]
        err = xb + g2 * bb - tb
        dx = err * (1.0 / D)
        loss = 0.5 * jnp.sum(jnp.sum(err * err, axis=1, keepdims=True) * (1.0 / D), axis=0, keepdims=True)
        return dx, (dx * g2), jnp.broadcast_to(loss, (1, LANES)), jnp.sum(dx * bb, axis=0, keepdims=True)

    dx4, dmlp1, loss_acc, dg2_1 = rowwise("loss_head", loss_head, [x3, mlp1, target], [ada_raw[1], b_ada[1:2]],
                                          [(D, F32), (D, BF16)], [(1, LANES), (1, D)])
    loss = lax.psum(loss_acc[0, 0], ("x", "y", "c"))

    def mlp_backward(tag, dmlp, act, u, h2, layer, after=None):
        du = mm_nt("mlp_out_dx" + tag, dmlp, W_mlp_out[layer], (BF16,), extra=(u,), after=after,
                   epilogue=lambda acc, ub: (acc * (2.0 * jnp.maximum(ub.astype(F32), 0.0)),))[0]
        gw_out = mm_tn("mlp_out_dw" + tag, act, dmlp, 1, BF16).reshape(N_DEV, -1, D)
        dh2 = mm_nt("mlp_in_dx" + tag, du, W_mlp_in[layer])[0]
        gw_in = mm_tn("mlp_in_dw" + tag, h2, du, N_DEV, BF16)
        return dh2, gw_in, gw_out

    def mod_backward(layer, slots, gate_slot):
        def fn(xb, dhb, dresb, branchb, ada, bias, gain):
            s = _ada_slices(ada, bias)
            g = gain[layer:layer + 1]
            dx, dgain, dsc, dsh = _vjp(_modulate, (xb, g, s[slots[1]], s[slots[0]]), dhb)
            dx = dx + dresb
            d_branch = dx * s[gate_slot]
            return dx, d_branch, dgain, dsc, dsh, jnp.sum(dx * branchb, axis=0, keepdims=True)
        return fn

    vec = (1, D)
    dh2_1, gw_mlp_in1, gw_mlp_out1 = mlp_backward("1", dmlp1, act1, u1, h2_1, 1)
    dx3, dmix1, dgain_mlp1, dsc2_1, dsh2_1, dg1_1 = rowwise(
        "mod_mlp1_bwd", mod_backward(1, (3, 4), 2), [x3, dh2_1, dx4, mix1], [ada_raw[1], b_ada[1:2], norm_mlp_gain],
        [(D, F32), (D, BF16)], [vec] * 4)
    dmixin1 = mm_nt("fox_out_dx", dmix1, W_fox_out)[0]
    gw_fox_out = mm_tn("fox_out_dw", mixin1, dmix1, 1, BF16).reshape(N_DEV, -1, D)

    def fox_gate_bwd(db, ob, yb):
        sg = jax.nn.sigmoid(ob)
        return db * sg, db * yb * sg * (1.0 - sg)

    dy_att, d_og = rowwise("fox_gate_bwd", fox_gate_bwd, [dmixin1, (qo, D, 1), y_att], [], [(D, F32), (D, F32)])
    dqn, dkn, dv_att, dfq, dfk = fox_backward("fox_bwd", qn, kn, kvf, f_cum_t, y_att, dy_att, lse)

    def q_norm_bwd(qb, db, ogb, gain):
        dq, dgain = _vjp(_norm_fox_heads, (qb, gain), db)
        return jnp.concatenate([dq, ogb], axis=1), dgain

    dqo, dq_gain = rowwise("q_norm_bwd", q_norm_bwd, [(qo, D, 0), dqn, d_og], [q_gain], [(2 * D, BF16)], [vec])
    dh1_1 = mm_nt("fox_proj_dx", dqo, W_fox_in)[0]
    gw_fox_in = mm_tn("fox_proj_dw", h1_1, dqo, N_DEV, BF16)

    dfk_rows = jnp.pad(dfk[:, :2, :].reshape(FOX_HEADS, S).T, ((0, 0), (0, LANES - FOX_HEADS)))

    def df_total(*blks):
        tot = blks[0]
        for b in blks[1:]:
            tot = tot + b
        return tot

    d_fcum = rowwise("df_sum", df_total, [dfk_rows] + [dfq[p] for p in range(N_PAIR)], [], [(LANES, F32)])[0]
    d_logf = cumsum_rows("df_cumsum", d_fcum, reverse=True)

    def kv_post_bwd(kb, fblk, dkb, dvb, dlf, kg, bias):
        dk, dgain = _vjp(_norm_fox_heads, (kb, kg), dkb)
        df = dlf * (1.0 / (1.0 + jnp.exp(fblk + bias)))
        return jnp.concatenate([dk, dvb, df], axis=1), dgain, jnp.sum(df, axis=0, keepdims=True)

    dkvf, dk_gain, dfb = rowwise("kv_post_bwd", kv_post_bwd,
                                 [(kvf, D, 0), (kvf, LANES, 2 * D // LANES), dkn, dv_att, d_logf], [k_gain, fb],
                                 [(KV_PAD, BF16)], [vec, (1, LANES)])
    dh_kv = mm_nt("kv_proj_dx", dkvf, W_kv)[0]
    gw_kv = mm_tn("kv_proj_dw", h_kv, dkvf, 1, BF16)[0, :, :N_DEV * n_kv]
    gw_kv = jnp.transpose(gw_kv.reshape(D, N_DEV, n_kv), (1, 0, 2))
    scat_a, token_a = scatter_start("a", [gw_mlp_in1, gw_mlp_out1, gw_fox_out, gw_fox_in, gw_kv])

    def x2_bwd(xb, dh1b, dhkb, dresb, branchb, ada0, bias0, ada1, bias1, kva, kvb, gain_mix, gain_kv):
        s1 = _ada_slices(ada1, bias1)
        kv_shift, kv_scale = _ada_slices(kva, kvb)
        dxa, dgain_mix, dsc1, dsh1 = _vjp(_modulate, (xb, gain_mix[1:2], s1[1], s1[0]), dh1b)
        dxb, dgain_kv, dkv_scale, dkv_shift = _vjp(_modulate, (xb, gain_kv, kv_scale, kv_shift), dhkb)
        dx = dresb + dxa + dxb
        g2 = _ada_slices(ada0, bias0)[5]
        return (dx, dx * g2, dgain_mix, dsc1, dsh1, dgain_kv, dkv_scale, dkv_shift,
                jnp.sum(dx * branchb, axis=0, keepdims=True))

    (dx2, dmlp0, dgain_mix1, dsc1_1, dsh1_1, dgain_kv, dkv_scale, dkv_shift, dg2_0) = rowwise(
        "x2_bwd", x2_bwd, [x2, dh1_1, dh_kv, dx3, mlp0],
        [ada_raw[0], b_ada[0:1], ada_raw[1], b_ada[1:2], kvada_raw, kv_bias, norm_mix_gain, kv_gain],
        [(D, F32), (D, BF16)], [vec] * 7, after=token_a)
    scat_a, token_a = scatter_relay("a", scat_a, dmlp0)

    dh2_0, gw_mlp_in0, gw_mlp_out0 = mlp_backward("0", dmlp0, act0, u0, h2_0, 0, after=token_a)
    dx1, dmix0, dgain_mlp0, dsc2_0, dsh2_0, dg1_0 = rowwise(
        "mod_mlp0_bwd", mod_backward(0, (3, 4), 2), [x1, dh2_0, dx2, mix0], [ada_raw[0], b_ada[0:1], norm_mlp_gain],
        [(D, F32), (D, BF16)], [vec] * 4)
    dmixin0 = mm_nt("ret_out_dx", dmix0, W_ret_out)[0]
    gw_ret_out = mm_tn("ret_out_dw", mixin0, dmix0, 1, BF16).reshape(N_DEV, -1, D)
    scat_b, token_b = scatter_start("b", [gw_mlp_in0, gw_mlp_out0, gw_ret_out])

    def ret_gate_bwd(db, yb, gb, gain):
        dy, dg, dgain = _vjp(lambda y_, g_, gn_: ret_gate(y_, g_, gn_), (yb, gb, gain), db)
        return dy, dg, dgain

    dy_ret, dgate, dret_gain = rowwise("ret_gate_bwd", ret_gate_bwd, [dmixin0, y_ret, (proj, 2 * D, 2)], [ret_gain],
                                       [(2 * D, F32), (2 * D, BF16)], [(1, 2 * D)], after=token_b)
    scat_b, token_b = scatter_relay("b", scat_b, dy_ret)
    dy_h = (dy_ret, RET_V, 0)
    dq_rot = retention("ret_dq", dy_h, v_ret, (k_rot, RET_QK, 0), reverse=False)
    dk_rot = retention("ret_dk", v_ret, dy_h, (q_rot, RET_QK, 0), reverse=True)
    dv_ret = retention("ret_dv", (k_rot, RET_QK, 0), (q_rot, RET_QK, 0), dy_h, reverse=True)

    def rope_bwd(dqb, dkb, dvb, dgb, cs, sn):
        dq = _rotate(dqb, cs, sn, RET_HEADS, -1.0)
        dk = _rotate(dkb, cs, sn, RET_HEADS, -1.0) * (RET_QK ** -0.5)
        return jnp.concatenate([dq, dk, dvb, dgb.astype(F32)], axis=1)

    dproj = rowwise("rope_bwd", rope_bwd, [dq_rot, dk_rot, dv_ret, dgate, cos, sin], [], [(6 * D, BF16)], after=token_b)[0]
    dh1_0 = mm_nt("ret_proj_dx", dproj, W_ret_in)[0]

    def x0_bwd(xb, dhb, dresb, ada, bias, gain):
        s = _ada_slices(ada, bias)
        dx, dgain, dsc, dsh = _vjp(_modulate, (xb, gain[0:1], s[1], s[0]), dhb)
        return dx + dresb, dgain, dsc, dsh

    grad_x, dgain_mix0, dsc1_0, dsh1_0 = rowwise("x0_bwd", x0_bwd, [x0, dh1_0, dx1],
                                                 [ada_raw[0], b_ada[0:1], norm_mix_gain], [(D, F32)], [vec] * 3)

    small = jnp.concatenate([
        dsh1_0, dsc1_0, dg1_0, dsh2_0, dsc2_0, dg2_0,
        dsh1_1, dsc1_1, dg1_1, dsh2_1, dsc2_1, dg2_1,
        dkv_shift, dkv_scale,
        dgain_mix0, dgain_mix1, dgain_mlp0, dgain_mlp1, dgain_kv,
        dret_gain,
        dq_gain.reshape(FOX_HEADS, FOX_DH).sum(axis=0).reshape(1, FOX_DH),
        dk_gain.reshape(FOX_HEADS, FOX_DH).sum(axis=0).reshape(1, FOX_DH),
        dfb,
    ], axis=1)
    small_all = all_gather("gather_small", [small])[0]
    gw_ret_in = mm_tn("ret_proj_dw", h1_0, dproj, N_DEV, BF16, after=small_all)
    scat_c, token_c = scatter_start("c", [gw_ret_in])
    o_ada = 14 * D
    d_ada = small_all[:, 0, :o_ada]
    d_cat = jnp.concatenate([
        lax.dynamic_slice_in_dim(d_ada[:, 0:6 * D], me * n_ada, n_ada, axis=1),
        lax.dynamic_slice_in_dim(d_ada[:, 6 * D:12 * D], me * n_ada, n_ada, axis=1),
        lax.dynamic_slice_in_dim(d_ada[:, 12 * D:14 * D], me * n_kvada, n_kvada, axis=1)], axis=1)
    gw_ada_cat = mm_tn("ada_dw", c_act, d_cat, 1, F32, after=token_c)[0]

    results = {}

    def update(name, parts, w, m, v, layers=1):
        per_layer = parts if layers > 1 else [parts]
        shape = w.shape
        C = shape[-1]
        R = int(np.prod(shape)) // (layers * C)
        per_layer = [p if isinstance(p, list) else [(p, p.shape[0])] for p in per_layer]
        per_layer = [[(a.reshape(a.shape[0], R, C), n) for a, n in p] for p in per_layer]
        outs = adamw("adamw_" + name, per_layer, w.reshape(layers, R, C), m.reshape(layers, R, C), v.reshape(layers, R, C))
        results[name] = tuple(t.reshape(shape) for t in outs)

    def small_parts(lo, width):
        return small_all[:, :, lo:lo + width]

    update("norm_mix_gain", jnp.concatenate([small_parts(o_ada, D), small_parts(o_ada + D, D)], axis=1),
           norm_mix_gain, m_norm_mix_gain, v_norm_mix_gain)
    update("norm_mlp_gain", jnp.concatenate([small_parts(o_ada + 2 * D, D), small_parts(o_ada + 3 * D, D)], axis=1),
           norm_mlp_gain, m_norm_mlp_gain, v_norm_mlp_gain)
    update("w_ada", [gw_ada_cat[None, :, :n_ada], gw_ada_cat[None, :, n_ada:2 * n_ada]], w_ada, m_w_ada, v_w_ada, layers=2)
    update("b_ada", jnp.concatenate([small_parts(0, 6 * D), small_parts(6 * D, 6 * D)], axis=1), b_ada, m_b_ada, v_b_ada)
    o_ret = o_ada + 5 * D
    n_rg = ret_norm_gain.shape[2]
    ret_gain_parts = small_parts(o_ret, 2 * D).reshape(N_DEV, RET_HEADS, RET_V)
    ret_gain_parts = lax.dynamic_slice_in_dim(ret_gain_parts, me * n_rg, n_rg, axis=2)
    update("ret_norm_gain", ret_gain_parts, ret_norm_gain, m_ret_norm_gain, v_ret_norm_gain)
    update("kv_norm_gain", small_parts(o_ada + 4 * D, D), kv_norm_gain, m_kv_norm_gain, v_kv_norm_gain)
    update("kv_w_ada", gw_ada_cat[None, :, 2 * n_ada:], kv_w_ada, m_kv_w_ada, v_kv_w_ada)
    update("kv_b_ada", small_parts(12 * D, 2 * D), kv_b_ada, m_kv_b_ada, v_kv_b_ada)
    o_q = o_ret + 2 * D
    update("forget_bias", small_parts(o_q + 2 * FOX_DH, FOX_HEADS), forget_bias, m_forget_bias, v_forget_bias)
    update("k_norm_gain", small_parts(o_q + FOX_DH, FOX_DH), k_norm_gain, m_k_norm_gain, v_k_norm_gain)
    update("q_norm_gain", small_parts(o_q, FOX_DH), q_norm_gain, m_q_norm_gain, v_q_norm_gain)

    scat_c, _ = scatter_relay("c", scat_c, results["w_ada"][1])
    r_mlp_in1, r_mlp_out1, r_fox_out, r_fox_in, r_kv = scatter_finish("a", scat_a, results["q_norm_gain"][1])
    r_mlp_in0, r_mlp_out0, r_ret_out = scatter_finish("b", scat_b, r_kv[1][0])
    update("kv_w", r_kv, kv_w, m_kv_w, v_kv_w)
    update("fox_w_in", r_fox_in, fox_w_in, m_fox_w_in, v_fox_w_in)
    update("fox_w_out", r_fox_out, fox_w_out, m_fox_w_out, v_fox_w_out)
    update("ret_w_out", r_ret_out, ret_w_out, m_ret_w_out, v_ret_w_out)
    update("w_mlp_in", [r_mlp_in0, r_mlp_in1], w_mlp_in, m_w_mlp_in, v_w_mlp_in, layers=2)
    update("w_mlp_out", [r_mlp_out0, r_mlp_out1], w_mlp_out, m_w_mlp_out, v_w_mlp_out, layers=2)
    r_ret_in = scatter_finish("c", scat_c, results["w_mlp_out"][1])[0]
    update("ret_w_in", r_ret_in, ret_w_in, m_ret_w_in, v_ret_w_in)

    order = ["norm_mix_gain", "norm_mlp_gain", "w_ada", "b_ada", "w_mlp_in", "w_mlp_out", "ret_w_in", "ret_norm_gain",
             "ret_w_out", "kv_norm_gain", "kv_w_ada", "kv_b_ada", "kv_w", "forget_bias", "k_norm_gain", "fox_w_in",
             "q_norm_gain", "fox_w_out"]
    out = [loss, grad_x.reshape(x.shape)]
    for slot in range(4):
        out += [results[n][slot] for n in order]
    return tuple(out)
```

```python
import functools
import math

import numpy as np
import jax
import jax.numpy as jnp
from jax import lax
from jax.experimental import pallas as pl
from jax.experimental.pallas import tpu as pltpu

F32 = jnp.float32
BF16 = jnp.bfloat16

N_DEV = 8
D_MODEL = 1024
RET_HEADS = 4
RET_QK = D_MODEL // RET_HEADS
RET_V = 2 * D_MODEL // RET_HEADS
RET_CHUNK = 128
ROPE_BASE = 10000.0
FOX_HEADS = 16
FOX_DH = D_MODEL // FOX_HEADS
EPS = 1e-6
LANES = 128
KV_PAD = 2 * D_MODEL + LANES

ADAM_LR = 0.001
ADAM_B1 = 0.9
ADAM_B2 = 0.999
ADAM_EPS = 1e-08
ADAM_WD = 0.01
ADAM_STEP = 10

VMEM_LIMIT_BYTES = 56 * 1024 * 1024


def _params(sem=None):
    return pltpu.CompilerParams(dimension_semantics=sem, vmem_limit_bytes=VMEM_LIMIT_BYTES)


def _me():
    return lax.axis_index("x"), lax.axis_index("y"), lax.axis_index("c")


def _peer(k):
    x, y, c = _me()
    return (1 - x if k & 4 else x, 1 - y if k & 2 else y, 1 - c if k & 1 else c)


def _peer_index(k):
    px, py, pc = _peer(k)
    return 4 * px + 2 * py + pc


def _exchange(name, xs, scatter):
    n = len(xs)

    def body(*refs):
        x_refs, o_refs = refs[:n], refs[n:2 * n]
        send_sems, recv_sems, local_sems = refs[2 * n:]
        x, y, c = _me()
        me = 4 * x + 2 * y + c
        local = []
        for i in range(n):
            src = x_refs[i].at[me] if scatter else x_refs[i]
            cp = pltpu.make_async_copy(src, o_refs[i].at[me], local_sems.at[i])
            cp.start()
            local.append(cp)
        remote = []
        for k in range(1, N_DEV):
            for i in range(n):
                src = x_refs[i].at[_peer_index(k)] if scatter else x_refs[i]
                cp = pltpu.make_async_remote_copy(
                    src_ref=src, dst_ref=o_refs[i].at[me],
                    send_sem=send_sems.at[(k - 1) * n + i], recv_sem=recv_sems.at[(k - 1) * n + i],
                    device_id=_peer(k), device_id_type=pl.DeviceIdType.MESH)
                cp.start()
                remote.append(cp)
        for cp in remote:
            cp.wait()
        for cp in local:
            cp.wait()

    out_shape = [jax.ShapeDtypeStruct(x.shape if scatter else (N_DEV,) + x.shape, x.dtype) for x in xs]
    any_spec = pl.BlockSpec(memory_space=pl.ANY)
    return pl.pallas_call(
        body, name=name, out_shape=out_shape,
        in_specs=[any_spec] * n, out_specs=[any_spec] * n,
        scratch_shapes=[pltpu.SemaphoreType.DMA(((N_DEV - 1) * n,)),
                        pltpu.SemaphoreType.DMA(((N_DEV - 1) * n,)),
                        pltpu.SemaphoreType.DMA((n,))],
    )(*xs)


def all_gather(name, xs):
    return _exchange(name, xs, scatter=False)


def all_to_all(name, xs):
    return _exchange(name, xs, scatter=True)


_HBM = pl.BlockSpec(memory_space=pltpu.HBM)
_SEM = pl.BlockSpec(memory_space=pltpu.SEMAPHORE)
_ANY = pl.BlockSpec(memory_space=pl.ANY)
_EFFECT = pltpu.SideEffectType.DATAFLOW_SIDE_EFFECTING

SIBLING = 1
CHIP_PEERS = (2, 4, 6)


def _my_index():
    x, y, c = _me()
    return 4 * x + 2 * y + c


def plan_gather(x, land, me):
    return [(x, land.at[me], k) for k in (SIBLING,) + CHIP_PEERS]


def plan_gather_direct(x, land, me):
    return [(x, land.at[me], k) for k in range(1, N_DEV)]


def plan_scatter_direct(x, land, me):
    return [(x.at[me ^ k], land.at[me], k) for k in range(1, N_DEV)]


def plan_forward(x, land, me):
    return [(x.at[me ^ k], land.at[me ^ k], SIBLING) for k in CHIP_PEERS]


def plan_to_sibling(x, land, me):
    return [(x.at[me ^ SIBLING ^ (2 * q)], land.at[q], SIBLING) for q in range(4)]


def plan_to_owners(x, land, me):
    return [(x.at[q], land.at[q - 1], 2 * q) for q in (1, 2, 3)]


N_COPIES = {plan_gather: 4, plan_gather_direct: 7, plan_scatter_direct: 7, plan_forward: 3, plan_to_sibling: 4,
            plan_to_owners: 3}


def _plans(plan, n):
    return list(plan) if isinstance(plan, (list, tuple)) else [plan] * n


def _plan_copies(plan, x_refs, land_refs, send_sems, recv_sems):
    me = _my_index()
    plans = _plans(plan, len(land_refs))
    return [pltpu.make_async_remote_copy(src_ref=src, dst_ref=dst, send_sem=send_sems[i].at[s], recv_sem=recv_sems[i].at[s],
                                         device_id=_peer(k), device_id_type=pl.DeviceIdType.MESH)
            for i in range(len(land_refs)) for s, (src, dst, k) in enumerate(plans[i](x_refs[i], land_refs[i], me))]


def _landing(block, shape, dtype):
    start = (_my_index(),) + (0,) * (len(shape) - 1)
    return lax.dynamic_update_slice(lax.empty(shape, dtype), block[None], start)


def exchange_now(name, xs, lands, plan):
    n = len(lands)
    n_x = 0 if xs is None else n

    def body(*refs):
        land_in, land_out = refs[n_x:n_x + n], refs[n_x + n:n_x + 2 * n]
        x_refs = land_in if xs is None else refs[:n]
        sems = refs[n_x + 2 * n:]
        copies = _plan_copies(plan, x_refs, land_out, sems[:n], sems[n:])
        for cp in copies:
            cp.start()
        for cp in copies:
            cp.wait()

    return pl.pallas_call(
        body, name=name, out_shape=[jax.ShapeDtypeStruct(a.shape, a.dtype) for a in lands],
        in_specs=[_ANY] * (n_x + n), out_specs=[_ANY] * n,
        input_output_aliases={n_x + i: i for i in range(n)},
        scratch_shapes=[pltpu.SemaphoreType.DMA((N_COPIES[p],)) for p in _plans(plan, n)] * 2,
    )(*([] if xs is None else xs), *lands)


def exchange_start(name, xs, lands, plan, after=()):
    n, m = len(xs), len(after)

    def body(*refs):
        x_refs, land_refs = refs[:n], refs[n:2 * n]
        send_sems, recv_sems = refs[2 * n + m:3 * n + m], refs[3 * n + m:4 * n + m]
        token = refs[6 * n + m]
        for cp in _plan_copies(plan, x_refs, land_refs, send_sems, recv_sems):
            cp.start()
        token[...] = jnp.zeros_like(token)

    sems = [pltpu.SemaphoreType.DMA((N_COPIES[p],)) for p in _plans(plan, n)] * 2
    thru = [pltpu.HBM(a.shape, a.dtype) for a in list(xs) + list(lands)]
    res = pl.pallas_call(
        body, name=name,
        out_shape=sems + thru + [jax.ShapeDtypeStruct((8, LANES), F32)],
        in_specs=[_HBM] * (2 * n) + [_ANY] * m,
        out_specs=[_SEM] * (2 * n) + [_HBM] * (2 * n) + [pl.BlockSpec(memory_space=pltpu.VMEM)],
        input_output_aliases={i: 2 * n + i for i in range(2 * n)},
        compiler_params=pltpu.CompilerParams(has_side_effects=_EFFECT),
    )(*[pltpu.with_memory_space_constraint(a, pltpu.HBM) for a in list(xs) + list(lands)], *after)
    handles = [(res[i], res[n + i], res[2 * n + i], res[3 * n + i]) for i in range(n)]
    return handles, res[4 * n]


def exchange_wait(name, handles, plan, after):
    n = len(handles)

    def body(*refs):
        x_refs, land_refs = refs[:n], refs[n:2 * n]
        send_sems, recv_sems = refs[2 * n:3 * n], refs[3 * n:4 * n]
        for cp in _plan_copies(plan, x_refs, land_refs, send_sems, recv_sems):
            cp.wait_send()
            cp.wait_recv()

    xs = [h[2] for h in handles]
    lands = [h[3] for h in handles]
    res = pl.pallas_call(
        body, name=name,
        out_shape=[pltpu.HBM(a.shape, a.dtype) for a in xs + lands],
        in_specs=[_HBM] * (2 * n) + [_SEM] * (2 * n) + [_ANY],
        out_specs=[_HBM] * (2 * n),
        input_output_aliases={i: i for i in range(2 * n)},
        compiler_params=pltpu.CompilerParams(has_side_effects=_EFFECT),
    )(*xs, *lands, *[h[0] for h in handles], *[h[1] for h in handles], after)
    return res[:n], res[n:]


def chip_sum(name, gw, from_sibling):
    _, R, C = gw.shape
    tr = _row_tile(R, 512)
    me = _my_index().astype(jnp.int32).reshape(1)

    def body(me_ref, g_ref, s_ref, o_ref):
        o_ref[...] = (g_ref[...].astype(F32) + s_ref[...].astype(F32)).astype(o_ref.dtype)

    slot = pl.BlockSpec((None, tr, C), lambda q, i, me_ref: (q, i, 0))
    return pl.pallas_call(
        body, name=name, out_shape=jax.ShapeDtypeStruct((4, R, C), BF16),
        grid_spec=pltpu.PrefetchScalarGridSpec(
            num_scalar_prefetch=1, grid=(4, R // tr),
            in_specs=[pl.BlockSpec((None, tr, C), lambda q, i, me_ref: (me_ref[0] ^ (2 * q), i, 0)), slot],
            out_specs=slot),
        compiler_params=_params(("arbitrary", "arbitrary")),
    )(me, gw, from_sibling)


def _tile(n, cap):
    best = None
    for t in range(LANES, min(n, cap) + 1, LANES):
        if n % t == 0:
            best = t
    if best is None or (best < 256 and n <= 2304):
        return n
    return best


def _row_tile(m, cap):
    if m <= cap:
        return m
    t = cap
    while m % t:
        t //= 2
    return t


def _after_spec(after):
    return [] if after is None else [pl.BlockSpec(memory_space=pl.ANY)]


def _after_arg(after):
    return [] if after is None else [after]


def _mm_call(name, dims, grid, a_spec, b_spec, o_spec, o_shape, tile, a, b, out_dtypes, epilogue, extra, after):
    nk = grid[2]
    n_x, n_o = len(extra), len(out_dtypes)

    def body(a_ref, b_ref, *refs):
        x_refs, o_refs, acc_ref = refs[:n_x], refs[len(refs) - 1 - n_o:len(refs) - 1], refs[-1]
        k = pl.program_id(2)

        @pl.when(k == 0)
        def _():
            acc_ref[...] = jnp.zeros_like(acc_ref)

        acc_ref[...] += lax.dot_general(a_ref[...].astype(BF16), b_ref[...].astype(BF16), (dims, ((), ())),
                                        preferred_element_type=F32)

        @pl.when(k == nk - 1)
        def _():
            acc = acc_ref[...]
            vals = (acc,) if epilogue is None else epilogue(acc, *[x[...] for x in x_refs])
            for o_ref, val in zip(o_refs, vals):
                o_ref[...] = val.astype(o_ref.dtype)

    return pl.pallas_call(
        body, name=name, out_shape=[jax.ShapeDtypeStruct(o_shape, dt) for dt in out_dtypes], grid=grid,
        in_specs=[a_spec, b_spec] + [o_spec] * n_x + _after_spec(after), out_specs=[o_spec] * n_o,
        scratch_shapes=[pltpu.VMEM(tile, F32)],
        compiler_params=_params(("parallel", "parallel", "arbitrary")),
    )(a, b, *extra, *_after_arg(after))


def mm_nn(name, a, w, out_dtypes=(F32,), epilogue=None, extra=(), after=None):
    M, K = a.shape
    G, _, n = w.shape
    tn = _tile(n, 1024)
    tm = _row_tile(M, 512 if tn > 1024 else 1024)
    tk = _row_tile(K, 1024)
    r = n // tn
    return _mm_call(
        name, ((1,), (0,)), (M // tm, G * r, K // tk),
        pl.BlockSpec((tm, tk), lambda i, j, k: (i, k)),
        pl.BlockSpec((None, tk, tn), lambda i, j, k: (j // r, k, j % r)),
        pl.BlockSpec((tm, tn), lambda i, j, k: (i, j)), (M, G * n), (tm, tn),
        a, w, out_dtypes, epilogue, extra, after)


def mm_nt(name, dy, w, out_dtypes=(F32,), epilogue=None, extra=(), after=None):
    M, N = dy.shape
    G, K, n = w.shape
    tn = _tile(n, 1024)
    tm = _row_tile(M, 512 if tn > 1024 else 1024)
    tk = _row_tile(K, 1024)
    r = n // tn
    return _mm_call(
        name, ((1,), (1,)), (M // tm, K // tk, G * r),
        pl.BlockSpec((tm, tn), lambda i, j, k: (i, k)),
        pl.BlockSpec((None, tk, tn), lambda i, j, k: (k // r, j, k % r)),
        pl.BlockSpec((tm, tk), lambda i, j, k: (i, j)), (M, K), (tm, tk),
        dy, w, out_dtypes, epilogue, extra, after)


def mm_tn(name, a, dy, G, out_dtype=F32, after=None):
    M, K = a.shape
    n = dy.shape[1] // G
    tn = _tile(n, 1024)
    tk = _row_tile(K, 512 if tn > 1024 else 1024)
    tm = _row_tile(M, 1024)
    r = n // tn
    return _mm_call(
        name, ((0,), (0,)), (K // tk, G * r, M // tm),
        pl.BlockSpec((tm, tk), lambda i, j, k: (k, i)),
        pl.BlockSpec((tm, tn), lambda i, j, k: (k, j)),
        pl.BlockSpec((None, tk, tn), lambda i, j, k: (j // r, i, j % r)), (G, K, n), (tk, tn),
        a, dy, (out_dtype,), None, (), after)[0]


def rowwise(name, fn, rows, vecs, outs, accs=(), tm=256, after=None):
    rows = [r if isinstance(r, tuple) else (r, r.shape[1], 0) for r in rows]
    n_fn = len(rows) + len(vecs)
    vecs = list(vecs) + _after_arg(after)
    S = rows[0][0].shape[0]
    tm = _row_tile(S, tm)
    n_r, n_v, n_o, n_a = len(rows), len(vecs), len(outs), len(accs)

    def body(*refs):
        ins = [ref[...] for ref in refs[:n_r + n_v]]
        o_refs = refs[n_r + n_v:n_r + n_v + n_o]
        a_refs = refs[n_r + n_v + n_o:]
        res = fn(*ins[:n_fn])
        res = res if isinstance(res, (tuple, list)) else (res,)
        for ref, val in zip(o_refs, res[:n_o]):
            ref[...] = val.astype(ref.dtype)
        if n_a:
            @pl.when(pl.program_id(0) == 0)
            def _():
                for ref in a_refs:
                    ref[...] = jnp.zeros_like(ref)
            for ref, val in zip(a_refs, res[n_o:]):
                ref[...] += val

    in_specs = [pl.BlockSpec((tm, w), functools.partial(lambda cb, i: (i, cb), cb)) for _, w, cb in rows]
    in_specs += [pl.BlockSpec(v.shape, lambda i: (0, 0)) for v in vecs]
    out_specs = [pl.BlockSpec((tm, w), lambda i: (i, 0)) for w, _ in outs]
    out_specs += [pl.BlockSpec(a, lambda i: (0, 0)) for a in accs]
    out_shape = [jax.ShapeDtypeStruct((S, w), dt) for w, dt in outs]
    out_shape += [jax.ShapeDtypeStruct(a, F32) for a in accs]
    res = pl.pallas_call(
        body, name=name, out_shape=out_shape, grid=(S // tm,),
        in_specs=in_specs, out_specs=out_specs,
        compiler_params=_params(("arbitrary",)),
    )(*[r[0] for r in rows], *vecs)
    return res


def _rms(x):
    return x * lax.rsqrt(jnp.mean(x * x, axis=-1, keepdims=True) + EPS)


def _modulate(x, gain, scale, shift):
    return _rms(x) * gain * (1.0 + scale) + shift


def _ada_slices(ada_raw, bias):
    ada = ada_raw + bias
    return [ada[:, i * D_MODEL:(i + 1) * D_MODEL] for i in range(ada.shape[1] // D_MODEL)]


def _norm_wide_heads(y, gain, heads):
    w = y.shape[1] // heads
    return jnp.concatenate([_rms(y[:, h * w:(h + 1) * w]) * gain[:, h * w:(h + 1) * w] for h in range(heads)], axis=1)


def _norm_fox_heads(x, gain):
    outs = []
    for p in range(x.shape[1] // LANES):
        blk = x[:, p * LANES:(p + 1) * LANES]
        low = lax.broadcasted_iota(jnp.int32, blk.shape, 1) < FOX_DH
        sq = blk * blk
        ss_low = jnp.sum(jnp.where(low, sq, 0.0), axis=1, keepdims=True)
        ss_high = jnp.sum(jnp.where(low, 0.0, sq), axis=1, keepdims=True)
        outs.append(blk * lax.rsqrt(jnp.where(low, ss_low, ss_high) * (1.0 / FOX_DH) + EPS))
    return jnp.concatenate(outs, axis=1) * gain


def _silu(x):
    return x * jax.nn.sigmoid(x)


def _log_sigmoid(z):
    return -(jnp.maximum(-z, 0.0) + jnp.log(1.0 + jnp.exp(-jnp.abs(z))))


def _rotate(x, cos, sin, heads, sign):
    w = x.shape[1] // heads
    half = w // 2
    outs = []
    for h in range(heads):
        x1 = x[:, h * w:h * w + half]
        x2 = x[:, h * w + half:(h + 1) * w]
        outs += [x1 * cos - sign * x2 * sin, sign * x1 * sin + x2 * cos]
    return jnp.concatenate(outs, axis=1)


def _vjp(fn, primals, ct):
    return jax.vjp(fn, *primals)[1](ct)


_LOG_GAMMAS = [float(np.log(np.float32(1.0) - np.float32(2.0) ** np.float32(-5.0 - h))) for h in range(RET_HEADS)]


RET_ROWS = 512


def retention(name, q, k, v, reverse):
    (qa, dk, qo), (ka, _, ko), (va, dv, vo) = q, k, v
    S = qa.shape[0]
    C = RET_CHUNK
    rows = _row_tile(S, RET_ROWS)
    nb = S // rows

    def body(q_ref, k_ref, v_ref, o_ref, state):
        h = pl.program_id(0)

        @pl.when(pl.program_id(1) == 0)
        def _():
            state[...] = jnp.zeros_like(state)

        log_g = jnp.float32(_LOG_GAMMAS[RET_HEADS - 1])
        for i in range(RET_HEADS - 2, -1, -1):
            log_g = jnp.where(h == i, jnp.float32(_LOG_GAMMAS[i]), log_g)
        row = lax.broadcasted_iota(jnp.int32, (C, C), 0)
        col = lax.broadcasted_iota(jnp.int32, (C, C), 1)
        rel = (col - row if reverse else row - col).astype(F32)
        decay = jnp.where(rel >= 0, jnp.exp(log_g * jnp.maximum(rel, 0.0)), 0.0)
        j = lax.broadcasted_iota(jnp.int32, (C, 1), 0).astype(F32)
        q_decay = jnp.exp(log_g * (C - j if reverse else j + 1.0))
        k_decay = jnp.exp(log_g * (j if reverse else C - 1.0 - j))
        chunk_decay = jnp.exp(jnp.full((1, 1), log_g * C, F32))

        chunks = range(rows // C)
        for ci in (reversed(chunks) if reverse else chunks):
            rs = slice(ci * C, (ci + 1) * C)
            qc = q_ref[rs, :].astype(BF16)
            kf = k_ref[rs, :].astype(F32)
            vc = v_ref[rs, :].astype(BF16)
            scores = lax.dot_general(qc, kf.astype(BF16), (((1,), (1,)), ((), ())), preferred_element_type=F32) * decay
            intra = jnp.dot(scores.astype(BF16), vc, preferred_element_type=F32)
            cross = jnp.dot(qc, state[...].astype(BF16), preferred_element_type=F32) * q_decay
            o_ref[rs, :] = intra + cross
            upd = lax.dot_general((kf * k_decay).astype(BF16), vc, (((0,), (0,)), ((), ())), preferred_element_type=F32)
            state[...] = state[...] * chunk_decay + upd

    def block(i):
        return nb - 1 - i if reverse else i

    return pl.pallas_call(
        body, name=name, out_shape=jax.ShapeDtypeStruct((S, RET_HEADS * dv), F32),
        grid=(RET_HEADS, nb),
        in_specs=[pl.BlockSpec((rows, dk), lambda h, i: (block(i), qo + h)),
                  pl.BlockSpec((rows, dk), lambda h, i: (block(i), ko + h)),
                  pl.BlockSpec((rows, dv), lambda h, i: (block(i), vo + h))],
        out_specs=pl.BlockSpec((rows, dv), lambda h, i: (block(i), h)),
        scratch_shapes=[pltpu.VMEM((dk, dv), F32)],
        compiler_params=_params(("parallel", "arbitrary")),
    )(qa, ka, va)


FOX_T = 256
N_PAIR = FOX_HEADS // 2
FOX_SCALE = FOX_DH ** -0.5


def _fox_heads(q2):
    low = lax.broadcasted_iota(jnp.int32, (1, LANES), 1) < FOX_DH
    return [(mask, jnp.where(mask, q2 * FOX_SCALE, 0.0).astype(BF16)) for mask in (low, jnp.logical_not(low))]


def _fox_parts(j, t):
    return ([(0, j * t, False)] if j else []) + [(j * t, (j + 1) * t, True)]


def _fox_scores(qa, k_ref, ft_ref, head, lo, hi, diagonal):
    k_blk = k_ref[lo:hi, :].astype(BF16)
    s = lax.dot_general(qa, k_blk, (((1,), (1,)), ((), ())), preferred_element_type=F32) - ft_ref[pl.ds(head, 1), lo:hi]
    if diagonal:
        n = hi - lo
        s = jnp.where(lax.broadcasted_iota(jnp.int32, (n, n), 1) <= lax.broadcasted_iota(jnp.int32, (n, n), 0), s, -jnp.inf)
    return s


def fox_forward(name, qn, kn, kvf, f_cum_t):
    S = qn.shape[0]
    t = _row_tile(S, FOX_T)
    v_block0 = D_MODEL // LANES

    def variant(j, pair, q_ref, k_ref, v_ref, ft_ref, y_ref, lse_ref):
        ys, lses = [], []
        for a, (mask, qa) in enumerate(_fox_heads(q_ref[...])):
            parts = [(lo, hi, _fox_scores(qa, k_ref, ft_ref, 2 * pair + a, lo, hi, dg)) for lo, hi, dg in _fox_parts(j, t)]
            m = functools.reduce(jnp.maximum, [jnp.max(s, axis=1, keepdims=True) for _, _, s in parts])
            l, acc = 0.0, 0.0
            for lo, hi, s in parts:
                e = jnp.exp(s - m)
                l = l + jnp.sum(e, axis=1, keepdims=True)
                acc = acc + jnp.dot(e.astype(BF16), v_ref[lo:hi, :].astype(BF16), preferred_element_type=F32)
            ys.append(acc / l)
            lses.append(m + jnp.log(l))
        low = lax.broadcasted_iota(jnp.int32, (1, LANES), 1) < FOX_DH
        y_ref[...] = jnp.where(low, ys[0], ys[1])
        lse_ref[...] = jnp.where(low, lses[0], lses[1])

    def body(*refs):
        pair, i = pl.program_id(0), pl.program_id(1)
        for j in range(S // t):
            pl.when(i == j)(functools.partial(variant, j, pair, *refs))

    return pl.pallas_call(
        body, name=name,
        out_shape=[jax.ShapeDtypeStruct((S, D_MODEL), F32), jax.ShapeDtypeStruct((S, D_MODEL), F32)],
        grid=(N_PAIR, S // t),
        in_specs=[pl.BlockSpec((t, LANES), lambda p, i: (i, p)),
                  pl.BlockSpec((S, LANES), lambda p, i: (0, p)),
                  pl.BlockSpec((S, LANES), lambda p, i: (0, v_block0 + p)),
                  pl.BlockSpec((LANES, S), lambda p, i: (0, 0))],
        out_specs=[pl.BlockSpec((t, LANES), lambda p, i: (i, p)),
                   pl.BlockSpec((t, LANES), lambda p, i: (i, p))],
        compiler_params=_params(("parallel", "arbitrary")),
    )(qn, kn, kvf, f_cum_t)


def fox_backward(name, qn, kn, kvf, f_cum_t, y, dy, lse):
    S = qn.shape[0]
    t = _row_tile(S, FOX_T)
    v_block0 = D_MODEL // LANES

    def variant(j, pair, q_ref, k_ref, v_ref, ft_ref, y_ref, dy_ref, lse_ref, dq_ref, dk_ref, dv_ref, dfq_ref, dfk_ref):
        y2, dy2, lse2 = y_ref[...], dy_ref[...], lse_ref[...]
        lane = lax.broadcasted_iota(jnp.int32, (t, LANES), 1)
        dqs, dfq = [], jnp.zeros((t, LANES), F32)
        for a, (mask, qa) in enumerate(_fox_heads(q_ref[...])):
            lse_a = jnp.max(jnp.where(mask, lse2, -jnp.inf), axis=1, keepdims=True)
            dy_a = jnp.where(mask, dy2, 0.0)
            delta = jnp.sum(dy_a * y2, axis=1, keepdims=True)
            dy_b = dy_a.astype(BF16)
            dq, row_sum = 0.0, 0.0
            for lo, hi, dg in _fox_parts(j, t):
                p = jnp.exp(_fox_scores(qa, k_ref, ft_ref, 2 * pair + a, lo, hi, dg) - lse_a)
                dp = lax.dot_general(dy_b, v_ref[lo:hi, :].astype(BF16), (((1,), (1,)), ((), ())), preferred_element_type=F32)
                ds = p * (dp - delta)
                row_sum = row_sum + jnp.sum(ds, axis=1, keepdims=True)
                dfk_ref[pl.ds(a, 1), lo:hi] += -jnp.sum(ds, axis=0, keepdims=True)
                ds_b = ds.astype(BF16)
                dq = dq + jnp.dot(ds_b, k_ref[lo:hi, :].astype(BF16), preferred_element_type=F32)
                dk_ref[lo:hi, :] += lax.dot_general(ds_b, qa, (((0,), (0,)), ((), ())), preferred_element_type=F32)
                dv_ref[lo:hi, :] += lax.dot_general(p.astype(BF16), dy_b, (((0,), (0,)), ((), ())), preferred_element_type=F32)
            dqs.append(dq * FOX_SCALE)
            dfq = dfq + jnp.where(lane == 2 * pair + a, row_sum, 0.0)
        low = lax.broadcasted_iota(jnp.int32, (1, LANES), 1) < FOX_DH
        dq_ref[...] = jnp.where(low, dqs[0], dqs[1])
        dfq_ref[...] = dfq

    def body(*refs):
        pair, i = pl.program_id(0), pl.program_id(1)
        dk_ref, dv_ref, dfk_ref = refs[8], refs[9], refs[11]

        @pl.when(i == 0)
        def _():
            dk_ref[...] = jnp.zeros_like(dk_ref)
            dv_ref[...] = jnp.zeros_like(dv_ref)
            dfk_ref[...] = jnp.zeros_like(dfk_ref)

        for j in range(S // t):
            pl.when(i == j)(functools.partial(variant, j, pair, *refs))

    row_blk = pl.BlockSpec((t, LANES), lambda p, i: (i, p))
    col_blk = pl.BlockSpec((S, LANES), lambda p, i: (0, p))
    return pl.pallas_call(
        body, name=name,
        out_shape=[jax.ShapeDtypeStruct((S, D_MODEL), F32)] * 3
        + [jax.ShapeDtypeStruct((N_PAIR, S, LANES), F32), jax.ShapeDtypeStruct((N_PAIR, 8, S), F32)],
        grid=(N_PAIR, S // t),
        in_specs=[row_blk, col_blk,
                  pl.BlockSpec((S, LANES), lambda p, i: (0, v_block0 + p)),
                  pl.BlockSpec((LANES, S), lambda p, i: (0, 0)),
                  row_blk, row_blk, row_blk],
        out_specs=[row_blk, col_blk, col_blk,
                   pl.BlockSpec((None, t, LANES), lambda p, i: (p, i, 0)),
                   pl.BlockSpec((None, 8, S), lambda p, i: (p, 0, 0))],
        compiler_params=_params(("parallel", "arbitrary")),
    )(qn, kn, kvf, f_cum_t, y, dy, lse)


def cumsum_rows(name, x, reverse):
    S = x.shape[0]
    C = LANES
    nc = S // C

    def body(x_ref, o_ref):
        row = lax.broadcasted_iota(jnp.int32, (C, C), 0)
        col = lax.broadcasted_iota(jnp.int32, (C, C), 1)
        tri = jnp.where(col >= row if reverse else col <= row, 1.0, 0.0).astype(F32)
        carry = jnp.zeros((1, LANES), F32)
        for i in (range(nc - 1, -1, -1) if reverse else range(nc)):
            blk = x_ref[i * C:(i + 1) * C, :]
            loc = jnp.dot(tri, blk, preferred_element_type=F32, precision=lax.Precision.HIGHEST)
            o_ref[i * C:(i + 1) * C, :] = loc + carry
            carry = carry + (loc[0:1, :] if reverse else loc[C - 1:C, :])

    return pl.pallas_call(body, name=name, out_shape=jax.ShapeDtypeStruct((S, LANES), F32),
                          compiler_params=_params())(x)


def adamw(name, parts, w, m, v):
    L, R, C = w.shape
    tr = _row_tile(R, 256)
    nr = R // tr
    counts = [len(p) for p in parts]

    def body(*refs):
        w_ref, m_ref, v_ref, g_out, d_out, m_out, v_out = refs[sum(counts):]
        for layer in range(L):
            p_refs = refs[sum(counts[:layer]):sum(counts[:layer + 1])]

            @pl.when(pl.program_id(0) == layer)
            def _(p_refs=p_refs, slots=[n for _, n in parts[layer]]):
                g = None
                for p_ref, n in zip(p_refs, slots):
                    for i in range(n):
                        g = p_ref[i].astype(F32) if g is None else g + p_ref[i].astype(F32)
                m2 = ADAM_B1 * m_ref[...] + (1.0 - ADAM_B1) * g
                v2 = ADAM_B2 * v_ref[...] + (1.0 - ADAM_B2) * jnp.square(g)
                m_hat = m2 / (1.0 - ADAM_B1 ** ADAM_STEP)
                v_hat = v2 / (1.0 - ADAM_B2 ** ADAM_STEP)
                g_out[...] = g
                d_out[...] = -ADAM_LR * (m_hat / (jnp.sqrt(v_hat) + ADAM_EPS) + ADAM_WD * w_ref[...])
                m_out[...] = m2
                v_out[...] = v2

    def part_spec(layer, n):
        return pl.BlockSpec((n, tr, C), lambda l, i: (0, jnp.where(l == layer, i, jnp.where(l < layer, 0, nr - 1)), 0))

    blk = pl.BlockSpec((None, tr, C), lambda l, i: (l, i, 0))
    return pl.pallas_call(
        body, name=name, out_shape=[jax.ShapeDtypeStruct((L, R, C), F32)] * 4, grid=(L, nr),
        in_specs=[part_spec(layer, n) for layer in range(L) for _, n in parts[layer]] + [blk, blk, blk],
        out_specs=[blk] * 4, compiler_params=_params(("arbitrary", "arbitrary")),
    )(*[a for layer in parts for a, _ in layer], w, m, v)


def kernel(x, c, positions, norm_mix_gain, norm_mlp_gain, w_ada, b_ada, w_mlp_in, w_mlp_out, ret_w_in, ret_norm_gain, ret_w_out, kv_norm_gain, kv_w_ada, kv_b_ada, kv_w, forget_bias, k_norm_gain, fox_w_in, q_norm_gain, fox_w_out, loss_target, m_norm_mix_gain, m_norm_mlp_gain, m_w_ada, m_b_ada, m_w_mlp_in, m_w_mlp_out, m_ret_w_in, m_ret_norm_gain, m_ret_w_out, m_kv_norm_gain, m_kv_w_ada, m_kv_b_ada, m_kv_w, m_forget_bias, m_k_norm_gain, m_fox_w_in, m_q_norm_gain, m_fox_w_out, v_norm_mix_gain, v_norm_mlp_gain, v_w_ada, v_b_ada, v_w_mlp_in, v_w_mlp_out, v_ret_w_in, v_ret_norm_gain, v_ret_w_out, v_kv_norm_gain, v_kv_w_ada, v_kv_b_ada, v_kv_w, v_forget_bias, v_k_norm_gain, v_fox_w_in, v_q_norm_gain, v_fox_w_out):
    D = D_MODEL
    S = x.shape[1]
    x0 = x.reshape(S, D)
    target = loss_target.reshape(S, D)
    me = 4 * lax.axis_index("x") + 2 * lax.axis_index("y") + lax.axis_index("c")
    n_ada = w_ada.shape[2]
    n_kvada = kv_w_ada.shape[1]
    n_kv = kv_w.shape[1]

    c_all, ret_gain = all_gather("gather_c", [c.reshape(1, D), ret_norm_gain.reshape(RET_HEADS, -1)])
    ret_gain = jnp.transpose(ret_gain, (1, 0, 2)).reshape(1, RET_HEADS * RET_V)
    c_act = rowwise("silu_c", _silu, [c_all.reshape(N_DEV, D)], [], [(D, F32)])[0]
    w_ada_cat = jnp.concatenate([w_ada[0], w_ada[1], kv_w_ada], axis=1).astype(BF16)[None]
    n_cat = 2 * n_ada + n_kvada
    ada_part = mm_nn("ada_proj", c_act, w_ada_cat)[0]
    ada_mine = all_to_all("ada_rows", [ada_part.reshape(N_DEV, 1, n_cat)])[0][:, 0]
    ada_raw = [ada_mine[:, l * n_ada:(l + 1) * n_ada].reshape(1, 6 * D) for l in range(2)]
    kvada_raw = ada_mine[:, 2 * n_ada:].reshape(1, 2 * D)
    kv_bias = kv_b_ada.reshape(1, 2 * D)
    kv_gain = kv_norm_gain.reshape(1, D)
    fb = jnp.pad(forget_bias.reshape(1, FOX_HEADS), ((0, 0), (0, LANES - FOX_HEADS)))
    k_gain = jnp.tile(k_norm_gain.reshape(1, FOX_DH), (1, FOX_HEADS))
    q_gain = jnp.tile(q_norm_gain.reshape(1, FOX_DH), (1, FOX_HEADS))

    w_names = ["ret_in", "ret_out", "mlp_in0", "mlp_out0", "kv", "fox_in", "fox_out", "mlp_in1", "mlp_out1"]
    shards = [ret_w_in[0].astype(BF16), ret_w_out[0].astype(BF16), w_mlp_in[0].astype(BF16), w_mlp_out[0].astype(BF16),
              kv_w.astype(BF16), fox_w_in[0].astype(BF16), fox_w_out[0].astype(BF16), w_mlp_in[1].astype(BF16),
              w_mlp_out[1].astype(BF16)]
    two_level = {"ret_in", "ret_out", "mlp_in0", "mlp_out0"}
    w_plans = {name: plan_gather if name in two_level else plan_gather_direct for name in w_names}
    w_handles, token = exchange_start("gather_weights_start", shards,
                                      [_landing(a, (N_DEV,) + a.shape, a.dtype) for a in shards],
                                      [w_plans[name] for name in w_names], after=(ada_mine, ret_gain))
    w_handles = dict(zip(w_names, w_handles))

    def weight(name, after):
        arrived = exchange_wait("gather_wait_" + name, [w_handles[name]], w_plans[name], after)[1]
        if name in two_level:
            arrived = exchange_now("gather_forward_" + name, None, arrived, plan_forward)
        return arrived[0]

    pos = positions.reshape(S, 1).astype(F32)
    half = RET_QK // 2
    inv_freq = jnp.asarray((ROPE_BASE ** (-np.arange(half, dtype=np.float32) / half)).reshape(1, half), F32)

    def angles(p, f):
        ang = p * f
        return jnp.cos(ang), jnp.sin(ang)

    cos, sin = rowwise("rope_table", angles, [pos], [inv_freq], [(half, F32), (half, F32)])

    def mod_mix(layer):
        def fn(xb, ada, bias, gain):
            sh, sc = _ada_slices(ada, bias)[:2]
            return _modulate(xb, gain[layer:layer + 1], sc, sh)
        return fn

    def mod_mlp(layer):
        def fn(xb, ada, bias, gain):
            sh, sc = _ada_slices(ada, bias)[3:5]
            return _modulate(xb, gain[layer:layer + 1], sc, sh)
        return fn

    h1_0 = rowwise("mod_mix0", mod_mix(0), [x0], [ada_raw[0], b_ada[0:1], norm_mix_gain], [(D, BF16)], after=token)[0]
    W_ret_in = weight("ret_in", h1_0)
    proj = mm_nn("ret_proj", h1_0, W_ret_in)[0]

    def rope_fwd(qb, kb, cs, sn):
        return _rotate(qb, cs, sn, RET_HEADS, 1.0), _rotate(kb, cs, sn, RET_HEADS, 1.0) * (RET_QK ** -0.5)

    q_rot, k_rot = rowwise("rope", rope_fwd, [(proj, D, 0), (proj, D, 1), cos, sin], [], [(D, F32), (D, F32)])
    v_ret = (proj, RET_V, (2 * D) // RET_V)
    y_ret = retention("ret_fwd", (q_rot, RET_QK, 0), (k_rot, RET_QK, 0), v_ret, reverse=False)

    def ret_gate(yb, gb, gain):
        return _silu(gb) * _norm_wide_heads(yb, gain, RET_HEADS)

    mixin0 = rowwise("ret_gate", ret_gate, [y_ret, (proj, 2 * D, 2)], [ret_gain], [(2 * D, BF16)])[0]
    W_ret_out = weight("ret_out", mixin0).reshape(1, 2 * D, D)
    mix0 = mm_nn("ret_out", mixin0, W_ret_out)[0]

    def residual_mod(layer, slot):
        def fn(xb, bb, ada, bias, gain):
            s = _ada_slices(ada, bias)
            xn = xb + s[2] * bb
            return xn, _modulate(xn, gain[layer:layer + 1], s[4], s[3])
        return fn

    x1, h2_0 = rowwise("res_mix0", residual_mod(0, 0), [x0, mix0], [ada_raw[0], b_ada[0:1], norm_mlp_gain],
                       [(D, F32), (D, BF16)])

    W_mlp_in, W_mlp_out = {}, {}

    def mlp_forward(tag, h2, layer):
        W_mlp_in[layer] = weight("mlp_in" + tag, h2)
        u, act = mm_nn("mlp_in" + tag, h2, W_mlp_in[layer], (BF16, BF16),
                       epilogue=lambda acc: (acc, jnp.square(jnp.maximum(acc, 0.0))))
        W_mlp_out[layer] = weight("mlp_out" + tag, act).reshape(1, 4 * D, D)
        return u, act, mm_nn("mlp_out" + tag, act, W_mlp_out[layer])[0]

    u0, act0, mlp0 = mlp_forward("0", h2_0, 0)

    def res_mlp0(xb, bb, ada0, bias0, ada1, bias1, kva, kvb, gain_mix, gain_kv):
        xn = xb + _ada_slices(ada0, bias0)[5] * bb
        s1 = _ada_slices(ada1, bias1)
        kv_shift, kv_scale = _ada_slices(kva, kvb)
        return xn, _modulate(xn, gain_kv, kv_scale, kv_shift), _modulate(xn, gain_mix[1:2], s1[1], s1[0])

    x2, h_kv, h1_1 = rowwise("res_mlp0", res_mlp0, [x1, mlp0],
                             [ada_raw[0], b_ada[0:1], ada_raw[1], b_ada[1:2], kvada_raw, kv_bias, norm_mix_gain, kv_gain],
                             [(D, F32), (D, BF16), (D, BF16)])

    kv_full = jnp.transpose(weight("kv", h_kv), (1, 0, 2)).reshape(D, N_DEV * n_kv)
    W_kv = jnp.pad(kv_full, ((0, 0), (0, KV_PAD - N_DEV * n_kv)))[None]
    kvf = mm_nn("kv_proj", h_kv, W_kv)[0]

    def kv_post(kb, fblk, kg, bias):
        head = lax.broadcasted_iota(jnp.int32, fblk.shape, 1) < FOX_HEADS
        return _norm_fox_heads(kb, kg), jnp.where(head, _log_sigmoid(fblk + bias), 0.0)

    kn, log_f = rowwise("kv_post", kv_post, [(kvf, D, 0), (kvf, LANES, 2 * D // LANES)], [k_gain, fb],
                        [(D, F32), (LANES, F32)])
    f_cum = cumsum_rows("f_cumsum", log_f, reverse=False)
    f_cum_t = f_cum.T

    W_fox_in = weight("fox_in", kvf)
    qo = mm_nn("fox_proj", h1_1, W_fox_in)[0]
    qn = rowwise("q_norm", _norm_fox_heads, [(qo, D, 0)], [q_gain], [(D, F32)])[0]
    y_att, lse = fox_forward("fox_fwd", qn, kn, kvf, f_cum_t)
    mixin1 = rowwise("fox_gate", lambda ob, yb: jax.nn.sigmoid(ob) * yb, [(qo, D, 1), y_att], [], [(D, BF16)])[0]
    W_fox_out = weight("fox_out", mixin1).reshape(1, D, D)
    mix1 = mm_nn("fox_out", mixin1, W_fox_out)[0]
    x3, h2_1 = rowwise("res_mix1", residual_mod(1, 0), [x2, mix1], [ada_raw[1], b_ada[1:2], norm_mlp_gain],
                       [(D, F32), (D, BF16)])
    u1, act1, mlp1 = mlp_forward("1", h2_1, 1)

    def scatter_start(tag, gws, after=()):
        lands = [lax.empty((4,) + g.shape[1:], g.dtype) for g in gws]
        return exchange_start("scatter_sibling_start_" + tag, gws, lands, plan_to_sibling, after)

    def scatter_relay(tag, handles, after, start_after=()):
        gws, from_sibling = exchange_wait("scatter_sibling_wait_" + tag, handles, plan_to_sibling, after)
        sums = [chip_sum("chip_sum_%s%d" % (tag, i), g, s) for i, (g, s) in enumerate(zip(gws, from_sibling))]
        lands = [lax.empty((3,) + s.shape[1:], s.dtype) for s in sums]
        return exchange_start("scatter_owner_start_" + tag, sums, lands, plan_to_owners, start_after)

    def scatter_direct_start(tag, gws):
        me = _my_index()
        lands = [_landing(lax.dynamic_index_in_dim(g, me, 0, keepdims=False), g.shape, g.dtype) for g in gws]
        return exchange_start("scatter_direct_start_" + tag, gws, lands, plan_scatter_direct)

    def scatter_direct_finish(tag, handles, after):
        return [[(r, N_DEV)] for r in exchange_wait("scatter_direct_wait_" + tag, handles, plan_scatter_direct, after)[1]]

    def scatter_finish(tag, handles, after):
        sums, received = exchange_wait("scatter_owner_wait_" + tag, handles, plan_to_owners, after)
        return [[(s, 1), (r, 3)] for s, r in zip(sums, received)]

    def loss_head(xb, bb, tb, ada, bias):
        g2 = _ada_slices(ada, bias)[5]
        err = xb + g2 * bb - tb
        dx = err * (1.0 / D)
        loss = 0.5 * jnp.sum(jnp.sum(err * err, axis=1, keepdims=True) * (1.0 / D), axis=0, keepdims=True)
        return dx, (dx * g2), jnp.broadcast_to(loss, (1, LANES)), jnp.sum(dx * bb, axis=0, keepdims=True)

    dx4, dmlp1, loss_acc, dg2_1 = rowwise("loss_head", loss_head, [x3, mlp1, target], [ada_raw[1], b_ada[1:2]],
                                          [(D, F32), (D, BF16)], [(1, LANES), (1, D)])
    loss = lax.psum(loss_acc[0, 0], ("x", "y", "c"))

    def mlp_backward(tag, dmlp, act, u, h2, layer, after=None):
        du = mm_nt("mlp_out_dx" + tag, dmlp, W_mlp_out[layer], (BF16,), extra=(u,), after=after,
                   epilogue=lambda acc, ub: (acc * (2.0 * jnp.maximum(ub.astype(F32), 0.0)),))[0]
        gw_out = mm_tn("mlp_out_dw" + tag, act, dmlp, 1, BF16).reshape(N_DEV, -1, D)
        dh2 = mm_nt("mlp_in_dx" + tag, du, W_mlp_in[layer])[0]
        gw_in = mm_tn("mlp_in_dw" + tag, h2, du, N_DEV, BF16)
        return dh2, gw_in, gw_out

    def mod_backward(layer, slots, gate_slot):
        def fn(xb, dhb, dresb, branchb, ada, bias, gain):
            s = _ada_slices(ada, bias)
            g = gain[layer:layer + 1]
            dx, dgain, dsc, dsh = _vjp(_modulate, (xb, g, s[slots[1]], s[slots[0]]), dhb)
            dx = dx + dresb
            d_branch = dx * s[gate_slot]
            return dx, d_branch, dgain, dsc, dsh, jnp.sum(dx * branchb, axis=0, keepdims=True)
        return fn

    vec = (1, D)
    dh2_1, gw_mlp_in1, gw_mlp_out1 = mlp_backward("1", dmlp1, act1, u1, h2_1, 1)
    scat_a1, token_a1 = scatter_direct_start("a1", [gw_mlp_in1, gw_mlp_out1])
    dx3, dmix1, dgain_mlp1, dsc2_1, dsh2_1, dg1_1 = rowwise(
        "mod_mlp1_bwd", mod_backward(1, (3, 4), 2), [x3, dh2_1, dx4, mix1], [ada_raw[1], b_ada[1:2], norm_mlp_gain],
        [(D, F32), (D, BF16)], [vec] * 4, after=token_a1)
    dmixin1 = mm_nt("fox_out_dx", dmix1, W_fox_out)[0]
    gw_fox_out = mm_tn("fox_out_dw", mixin1, dmix1, 1, BF16).reshape(N_DEV, -1, D)

    def fox_gate_bwd(db, ob, yb):
        sg = jax.nn.sigmoid(ob)
        return db * sg, db * yb * sg * (1.0 - sg)

    dy_att, d_og = rowwise("fox_gate_bwd", fox_gate_bwd, [dmixin1, (qo, D, 1), y_att], [], [(D, F32), (D, F32)])
    dqn, dkn, dv_att, dfq, dfk = fox_backward("fox_bwd", qn, kn, kvf, f_cum_t, y_att, dy_att, lse)

    def q_norm_bwd(qb, db, ogb, gain):
        dq, dgain = _vjp(_norm_fox_heads, (qb, gain), db)
        return jnp.concatenate([dq, ogb], axis=1), dgain

    dqo, dq_gain = rowwise("q_norm_bwd", q_norm_bwd, [(qo, D, 0), dqn, d_og], [q_gain], [(2 * D, BF16)], [vec])
    dh1_1 = mm_nt("fox_proj_dx", dqo, W_fox_in)[0]
    gw_fox_in = mm_tn("fox_proj_dw", h1_1, dqo, N_DEV, BF16)

    dfk_rows = jnp.pad(dfk[:, :2, :].reshape(FOX_HEADS, S).T, ((0, 0), (0, LANES - FOX_HEADS)))

    def df_total(*blks):
        tot = blks[0]
        for b in blks[1:]:
            tot = tot + b
        return tot

    d_fcum = rowwise("df_sum", df_total, [dfk_rows] + [dfq[p] for p in range(N_PAIR)], [], [(LANES, F32)])[0]
    d_logf = cumsum_rows("df_cumsum", d_fcum, reverse=True)

    def kv_post_bwd(kb, fblk, dkb, dvb, dlf, kg, bias):
        dk, dgain = _vjp(_norm_fox_heads, (kb, kg), dkb)
        df = dlf * (1.0 / (1.0 + jnp.exp(fblk + bias)))
        return jnp.concatenate([dk, dvb, df], axis=1), dgain, jnp.sum(df, axis=0, keepdims=True)

    dkvf, dk_gain, dfb = rowwise("kv_post_bwd", kv_post_bwd,
                                 [(kvf, D, 0), (kvf, LANES, 2 * D // LANES), dkn, dv_att, d_logf], [k_gain, fb],
                                 [(KV_PAD, BF16)], [vec, (1, LANES)])
    dh_kv = mm_nt("kv_proj_dx", dkvf, W_kv)[0]
    gw_kv = mm_tn("kv_proj_dw", h_kv, dkvf, 1, BF16)[0, :, :N_DEV * n_kv]
    gw_kv = jnp.transpose(gw_kv.reshape(D, N_DEV, n_kv), (1, 0, 2))
    scat_a2, token_a = scatter_direct_start("a2", [gw_fox_out, gw_fox_in, gw_kv])

    def x2_bwd(xb, dh1b, dhkb, dresb, branchb, ada0, bias0, ada1, bias1, kva, kvb, gain_mix, gain_kv):
        s1 = _ada_slices(ada1, bias1)
        kv_shift, kv_scale = _ada_slices(kva, kvb)
        dxa, dgain_mix, dsc1, dsh1 = _vjp(_modulate, (xb, gain_mix[1:2], s1[1], s1[0]), dh1b)
        dxb, dgain_kv, dkv_scale, dkv_shift = _vjp(_modulate, (xb, gain_kv, kv_scale, kv_shift), dhkb)
        dx = dresb + dxa + dxb
        g2 = _ada_slices(ada0, bias0)[5]
        return (dx, dx * g2, dgain_mix, dsc1, dsh1, dgain_kv, dkv_scale, dkv_shift,
                jnp.sum(dx * branchb, axis=0, keepdims=True))

    (dx2, dmlp0, dgain_mix1, dsc1_1, dsh1_1, dgain_kv, dkv_scale, dkv_shift, dg2_0) = rowwise(
        "x2_bwd", x2_bwd, [x2, dh1_1, dh_kv, dx3, mlp0],
        [ada_raw[0], b_ada[0:1], ada_raw[1], b_ada[1:2], kvada_raw, kv_bias, norm_mix_gain, kv_gain],
        [(D, F32), (D, BF16)], [vec] * 7, after=token_a)

    dh2_0, gw_mlp_in0, gw_mlp_out0 = mlp_backward("0", dmlp0, act0, u0, h2_0, 0)
    dx1, dmix0, dgain_mlp0, dsc2_0, dsh2_0, dg1_0 = rowwise(
        "mod_mlp0_bwd", mod_backward(0, (3, 4), 2), [x1, dh2_0, dx2, mix0], [ada_raw[0], b_ada[0:1], norm_mlp_gain],
        [(D, F32), (D, BF16)], [vec] * 4)
    dmixin0 = mm_nt("ret_out_dx", dmix0, W_ret_out)[0]
    gw_ret_out = mm_tn("ret_out_dw", mixin0, dmix0, 1, BF16).reshape(N_DEV, -1, D)
    scat_b, token_b = scatter_start("b", [gw_mlp_in0, gw_mlp_out0, gw_ret_out])

    def ret_gate_bwd(db, yb, gb, gain):
        dy, dg, dgain = _vjp(lambda y_, g_, gn_: ret_gate(y_, g_, gn_), (yb, gb, gain), db)
        return dy, dg, dgain

    dy_ret, dgate, dret_gain = rowwise("ret_gate_bwd", ret_gate_bwd, [dmixin0, y_ret, (proj, 2 * D, 2)], [ret_gain],
                                       [(2 * D, F32), (2 * D, BF16)], [(1, 2 * D)], after=token_b)
    scat_b, token_b = scatter_relay("b", scat_b, dy_ret)
    dy_h = (dy_ret, RET_V, 0)
    dq_rot = retention("ret_dq", dy_h, v_ret, (k_rot, RET_QK, 0), reverse=False)
    dk_rot = retention("ret_dk", v_ret, dy_h, (q_rot, RET_QK, 0), reverse=True)
    dv_ret = retention("ret_dv", (k_rot, RET_QK, 0), (q_rot, RET_QK, 0), dy_h, reverse=True)

    def rope_bwd(dqb, dkb, dvb, dgb, cs, sn):
        dq = _rotate(dqb, cs, sn, RET_HEADS, -1.0)
        dk = _rotate(dkb, cs, sn, RET_HEADS, -1.0) * (RET_QK ** -0.5)
        return jnp.concatenate([dq, dk, dvb, dgb.astype(F32)], axis=1)

    dproj = rowwise("rope_bwd", rope_bwd, [dq_rot, dk_rot, dv_ret, dgate, cos, sin], [], [(6 * D, BF16)], after=token_b)[0]
    gw_ret_in = mm_tn("ret_proj_dw", h1_0, dproj, N_DEV, BF16)
    scat_c, token_c = scatter_start("c", [gw_ret_in])
    dh1_0 = mm_nt("ret_proj_dx", dproj, W_ret_in, after=token_c)[0]

    def x0_bwd(xb, dhb, dresb, ada, bias, gain):
        s = _ada_slices(ada, bias)
        dx, dgain, dsc, dsh = _vjp(_modulate, (xb, gain[0:1], s[1], s[0]), dhb)
        return dx + dresb, dgain, dsc, dsh

    grad_x, dgain_mix0, dsc1_0, dsh1_0 = rowwise("x0_bwd", x0_bwd, [x0, dh1_0, dx1],
                                                 [ada_raw[0], b_ada[0:1], norm_mix_gain], [(D, F32)], [vec] * 3)

    small = jnp.concatenate([
        dsh1_0, dsc1_0, dg1_0, dsh2_0, dsc2_0, dg2_0,
        dsh1_1, dsc1_1, dg1_1, dsh2_1, dsc2_1, dg2_1,
        dkv_shift, dkv_scale,
        dgain_mix0, dgain_mix1, dgain_mlp0, dgain_mlp1, dgain_kv,
        dret_gain,
        dq_gain.reshape(FOX_HEADS, FOX_DH).sum(axis=0).reshape(1, FOX_DH),
        dk_gain.reshape(FOX_HEADS, FOX_DH).sum(axis=0).reshape(1, FOX_DH),
        dfb,
    ], axis=1)
    small_all = all_gather("gather_small", [small])[0]
    scat_c, token_c = scatter_relay("c", scat_c, grad_x, start_after=(small_all,))
    o_ada = 14 * D
    d_ada = small_all[:, 0, :o_ada]
    d_cat = jnp.concatenate([
        lax.dynamic_slice_in_dim(d_ada[:, 0:6 * D], me * n_ada, n_ada, axis=1),
        lax.dynamic_slice_in_dim(d_ada[:, 6 * D:12 * D], me * n_ada, n_ada, axis=1),
        lax.dynamic_slice_in_dim(d_ada[:, 12 * D:14 * D], me * n_kvada, n_kvada, axis=1)], axis=1)
    gw_ada_cat = mm_tn("ada_dw", c_act, d_cat, 1, F32, after=token_c)[0]

    results = {}

    def update(name, parts, w, m, v, layers=1):
        per_layer = parts if layers > 1 else [parts]
        shape = w.shape
        C = shape[-1]
        R = int(np.prod(shape)) // (layers * C)
        per_layer = [p if isinstance(p, list) else [(p, p.shape[0])] for p in per_layer]
        per_layer = [[(a.reshape(a.shape[0], R, C), n) for a, n in p] for p in per_layer]
        outs = adamw("adamw_" + name, per_layer, w.reshape(layers, R, C), m.reshape(layers, R, C), v.reshape(layers, R, C))
        results[name] = tuple(t.reshape(shape) for t in outs)

    def small_parts(lo, width):
        return small_all[:, :, lo:lo + width]

    update("norm_mix_gain", jnp.concatenate([small_parts(o_ada, D), small_parts(o_ada + D, D)], axis=1),
           norm_mix_gain, m_norm_mix_gain, v_norm_mix_gain)
    update("norm_mlp_gain", jnp.concatenate([small_parts(o_ada + 2 * D, D), small_parts(o_ada + 3 * D, D)], axis=1),
           norm_mlp_gain, m_norm_mlp_gain, v_norm_mlp_gain)
    update("w_ada", [gw_ada_cat[None, :, :n_ada], gw_ada_cat[None, :, n_ada:2 * n_ada]], w_ada, m_w_ada, v_w_ada, layers=2)
    update("b_ada", jnp.concatenate([small_parts(0, 6 * D), small_parts(6 * D, 6 * D)], axis=1), b_ada, m_b_ada, v_b_ada)
    o_ret = o_ada + 5 * D
    n_rg = ret_norm_gain.shape[2]
    ret_gain_parts = small_parts(o_ret, 2 * D).reshape(N_DEV, RET_HEADS, RET_V)
    ret_gain_parts = lax.dynamic_slice_in_dim(ret_gain_parts, me * n_rg, n_rg, axis=2)
    update("ret_norm_gain", ret_gain_parts, ret_norm_gain, m_ret_norm_gain, v_ret_norm_gain)
    update("kv_norm_gain", small_parts(o_ada + 4 * D, D), kv_norm_gain, m_kv_norm_gain, v_kv_norm_gain)
    update("kv_w_ada", gw_ada_cat[None, :, 2 * n_ada:], kv_w_ada, m_kv_w_ada, v_kv_w_ada)
    update("kv_b_ada", small_parts(12 * D, 2 * D), kv_b_ada, m_kv_b_ada, v_kv_b_ada)
    o_q = o_ret + 2 * D
    update("forget_bias", small_parts(o_q + 2 * FOX_DH, FOX_HEADS), forget_bias, m_forget_bias, v_forget_bias)
    update("k_norm_gain", small_parts(o_q + FOX_DH, FOX_DH), k_norm_gain, m_k_norm_gain, v_k_norm_gain)
    update("q_norm_gain", small_parts(o_q, FOX_DH), q_norm_gain, m_q_norm_gain, v_q_norm_gain)

    r_mlp_in1, r_mlp_out1 = scatter_direct_finish("a1", scat_a1, results["q_norm_gain"][1])
    r_fox_out, r_fox_in, r_kv = scatter_direct_finish("a2", scat_a2, r_mlp_in1[0][0])
    r_mlp_in0, r_mlp_out0, r_ret_out = scatter_finish("b", scat_b, r_kv[0][0])
    update("kv_w", r_kv, kv_w, m_kv_w, v_kv_w)
    update("fox_w_in", r_fox_in, fox_w_in, m_fox_w_in, v_fox_w_in)
    update("fox_w_out", r_fox_out, fox_w_out, m_fox_w_out, v_fox_w_out)
    update("ret_w_out", r_ret_out, ret_w_out, m_ret_w_out, v_ret_w_out)
    update("w_mlp_in", [r_mlp_in0, r_mlp_in1], w_mlp_in, m_w_mlp_in, v_w_mlp_in, layers=2)
    update("w_mlp_out", [r_mlp_out0, r_mlp_out1], w_mlp_out, m_w_mlp_out, v_w_mlp_out, layers=2)
    r_ret_in = scatter_finish("c", scat_c, results["w_mlp_out"][1])[0]
    update("ret_w_in", r_ret_in, ret_w_in, m_ret_w_in, v_ret_w_in)

    order = ["norm_mix_gain", "norm_mlp_gain", "w_ada", "b_ada", "w_mlp_in", "w_mlp_out", "ret_w_in", "ret_norm_gain",
             "ret_w_out", "kv_norm_gain", "kv_w_ada", "kv_b_ada", "kv_w", "forget_bias", "k_norm_gain", "fox_w_in",
             "q_norm_gain", "fox_w_out"]
    out = [loss, grad_x.reshape(x.shape)]
    for slot in range(4):
        out += [results[n][slot] for n in order]
    return tuple(out)
```

```python
import functools
import math

import numpy as np
import jax
import jax.numpy as jnp
from jax import lax
from jax.experimental import pallas as pl
from jax.experimental.pallas import tpu as pltpu

F32 = jnp.float32
BF16 = jnp.bfloat16

N_DEV = 8
D_MODEL = 1024
RET_HEADS = 4
RET_QK = D_MODEL // RET_HEADS
RET_V = 2 * D_MODEL // RET_HEADS
RET_CHUNK = 128
ROPE_BASE = 10000.0
FOX_HEADS = 16
FOX_DH = D_MODEL // FOX_HEADS
EPS = 1e-6
LANES = 128
KV_PAD = 2 * D_MODEL + LANES

ADAM_LR = 0.001
ADAM_B1 = 0.9
ADAM_B2 = 0.999
ADAM_EPS = 1e-08
ADAM_WD = 0.01
ADAM_STEP = 10

VMEM_LIMIT_BYTES = 56 * 1024 * 1024


def _params(sem=None):
    return pltpu.CompilerParams(dimension_semantics=sem, vmem_limit_bytes=VMEM_LIMIT_BYTES)


def _me():
    return lax.axis_index("x"), lax.axis_index("y"), lax.axis_index("c")


def _peer(k):
    x, y, c = _me()
    return (1 - x if k & 4 else x, 1 - y if k & 2 else y, 1 - c if k & 1 else c)


def _peer_index(k):
    px, py, pc = _peer(k)
    return 4 * px + 2 * py + pc


def _exchange(name, xs, scatter):
    n = len(xs)

    def body(*refs):
        x_refs, o_refs = refs[:n], refs[n:2 * n]
        send_sems, recv_sems, local_sems = refs[2 * n:]
        x, y, c = _me()
        me = 4 * x + 2 * y + c
        local = []
        for i in range(n):
            src = x_refs[i].at[me] if scatter else x_refs[i]
            cp = pltpu.make_async_copy(src, o_refs[i].at[me], local_sems.at[i])
            cp.start()
            local.append(cp)
        remote = []
        for k in range(1, N_DEV):
            for i in range(n):
                src = x_refs[i].at[_peer_index(k)] if scatter else x_refs[i]
                cp = pltpu.make_async_remote_copy(
                    src_ref=src, dst_ref=o_refs[i].at[me],
                    send_sem=send_sems.at[(k - 1) * n + i], recv_sem=recv_sems.at[(k - 1) * n + i],
                    device_id=_peer(k), device_id_type=pl.DeviceIdType.MESH)
                cp.start()
                remote.append(cp)
        for cp in remote:
            cp.wait()
        for cp in local:
            cp.wait()

    out_shape = [jax.ShapeDtypeStruct(x.shape if scatter else (N_DEV,) + x.shape, x.dtype) for x in xs]
    any_spec = pl.BlockSpec(memory_space=pl.ANY)
    return pl.pallas_call(
        body, name=name, out_shape=out_shape,
        in_specs=[any_spec] * n, out_specs=[any_spec] * n,
        scratch_shapes=[pltpu.SemaphoreType.DMA(((N_DEV - 1) * n,)),
                        pltpu.SemaphoreType.DMA(((N_DEV - 1) * n,)),
                        pltpu.SemaphoreType.DMA((n,))],
    )(*xs)


def all_gather(name, xs):
    return _exchange(name, xs, scatter=False)


def all_to_all(name, xs):
    return _exchange(name, xs, scatter=True)


_HBM = pl.BlockSpec(memory_space=pltpu.HBM)
_SEM = pl.BlockSpec(memory_space=pltpu.SEMAPHORE)
_ANY = pl.BlockSpec(memory_space=pl.ANY)
_EFFECT = pltpu.SideEffectType.DATAFLOW_SIDE_EFFECTING

SIBLING = 1
CHIP_PEERS = (2, 4, 6)


def _my_index():
    x, y, c = _me()
    return 4 * x + 2 * y + c


def plan_gather(x, land, me):
    return [(x, land.at[me], k) for k in (SIBLING,) + CHIP_PEERS]


def plan_gather_direct(x, land, me):
    return [(x, land.at[me], k) for k in range(1, N_DEV)]


def plan_scatter_direct(x, land, me):
    return [(x.at[me ^ k], land.at[me], k) for k in range(1, N_DEV)]


def plan_forward(x, land, me):
    return [(x.at[me ^ k], land.at[me ^ k], SIBLING) for k in CHIP_PEERS]


def plan_to_sibling(x, land, me):
    return [(x.at[me ^ SIBLING ^ (2 * q)], land.at[q], SIBLING) for q in range(4)]


def plan_to_owners(x, land, me):
    return [(x.at[q], land.at[q - 1], 2 * q) for q in (1, 2, 3)]


N_COPIES = {plan_gather: 4, plan_gather_direct: 7, plan_scatter_direct: 7, plan_forward: 3, plan_to_sibling: 4,
            plan_to_owners: 3}


def _plans(plan, n):
    return list(plan) if isinstance(plan, (list, tuple)) else [plan] * n


def _plan_copies(plan, x_refs, land_refs, send_sems, recv_sems):
    me = _my_index()
    plans = _plans(plan, len(land_refs))
    return [pltpu.make_async_remote_copy(src_ref=src, dst_ref=dst, send_sem=send_sems[i].at[s], recv_sem=recv_sems[i].at[s],
                                         device_id=_peer(k), device_id_type=pl.DeviceIdType.MESH)
            for i in range(len(land_refs)) for s, (src, dst, k) in enumerate(plans[i](x_refs[i], land_refs[i], me))]


def _landing(block, shape, dtype):
    start = (_my_index(),) + (0,) * (len(shape) - 1)
    return lax.dynamic_update_slice(lax.empty(shape, dtype), block[None], start)


def exchange_now(name, xs, lands, plan):
    n = len(lands)
    n_x = 0 if xs is None else n

    def body(*refs):
        land_in, land_out = refs[n_x:n_x + n], refs[n_x + n:n_x + 2 * n]
        x_refs = land_in if xs is None else refs[:n]
        sems = refs[n_x + 2 * n:]
        copies = _plan_copies(plan, x_refs, land_out, sems[:n], sems[n:])
        for cp in copies:
            cp.start()
        for cp in copies:
            cp.wait()

    return pl.pallas_call(
        body, name=name, out_shape=[jax.ShapeDtypeStruct(a.shape, a.dtype) for a in lands],
        in_specs=[_ANY] * (n_x + n), out_specs=[_ANY] * n,
        input_output_aliases={n_x + i: i for i in range(n)},
        scratch_shapes=[pltpu.SemaphoreType.DMA((N_COPIES[p],)) for p in _plans(plan, n)] * 2,
    )(*([] if xs is None else xs), *lands)


def exchange_start(name, xs, lands, plan, after=()):
    n, m = len(xs), len(after)

    def body(*refs):
        x_refs, land_refs = refs[:n], refs[n:2 * n]
        send_sems, recv_sems = refs[2 * n + m:3 * n + m], refs[3 * n + m:4 * n + m]
        token = refs[6 * n + m]
        for cp in _plan_copies(plan, x_refs, land_refs, send_sems, recv_sems):
            cp.start()
        token[...] = jnp.zeros_like(token)

    sems = [pltpu.SemaphoreType.DMA((N_COPIES[p],)) for p in _plans(plan, n)] * 2
    thru = [pltpu.HBM(a.shape, a.dtype) for a in list(xs) + list(lands)]
    res = pl.pallas_call(
        body, name=name,
        out_shape=sems + thru + [jax.ShapeDtypeStruct((8, LANES), F32)],
        in_specs=[_HBM] * (2 * n) + [_ANY] * m,
        out_specs=[_SEM] * (2 * n) + [_HBM] * (2 * n) + [pl.BlockSpec(memory_space=pltpu.VMEM)],
        input_output_aliases={i: 2 * n + i for i in range(2 * n)},
        compiler_params=pltpu.CompilerParams(has_side_effects=_EFFECT),
    )(*[pltpu.with_memory_space_constraint(a, pltpu.HBM) for a in list(xs) + list(lands)], *after)
    handles = [(res[i], res[n + i], res[2 * n + i], res[3 * n + i]) for i in range(n)]
    return handles, res[4 * n]


def exchange_wait(name, handles, plan, after):
    n = len(handles)

    def body(*refs):
        x_refs, land_refs = refs[:n], refs[n:2 * n]
        send_sems, recv_sems = refs[2 * n:3 * n], refs[3 * n:4 * n]
        for cp in _plan_copies(plan, x_refs, land_refs, send_sems, recv_sems):
            cp.wait_send()
            cp.wait_recv()

    xs = [h[2] for h in handles]
    lands = [h[3] for h in handles]
    res = pl.pallas_call(
        body, name=name,
        out_shape=[pltpu.HBM(a.shape, a.dtype) for a in xs + lands],
        in_specs=[_HBM] * (2 * n) + [_SEM] * (2 * n) + [_ANY],
        out_specs=[_HBM] * (2 * n),
        input_output_aliases={i: i for i in range(2 * n)},
        compiler_params=pltpu.CompilerParams(has_side_effects=_EFFECT),
    )(*xs, *lands, *[h[0] for h in handles], *[h[1] for h in handles], after)
    return res[:n], res[n:]


def chip_sum(name, gw, from_sibling):
    _, R, C = gw.shape
    tr = _row_tile(R, 512)
    me = _my_index().astype(jnp.int32).reshape(1)

    def body(me_ref, g_ref, s_ref, o_ref):
        o_ref[...] = (g_ref[...].astype(F32) + s_ref[...].astype(F32)).astype(o_ref.dtype)

    slot = pl.BlockSpec((None, tr, C), lambda q, i, me_ref: (q, i, 0))
    return pl.pallas_call(
        body, name=name, out_shape=jax.ShapeDtypeStruct((4, R, C), BF16),
        grid_spec=pltpu.PrefetchScalarGridSpec(
            num_scalar_prefetch=1, grid=(4, R // tr),
            in_specs=[pl.BlockSpec((None, tr, C), lambda q, i, me_ref: (me_ref[0] ^ (2 * q), i, 0)), slot],
            out_specs=slot),
        compiler_params=_params(("arbitrary", "arbitrary")),
    )(me, gw, from_sibling)


def _tile(n, cap):
    best = None
    for t in range(LANES, min(n, cap) + 1, LANES):
        if n % t == 0:
            best = t
    if best is None or (best < 256 and n <= 2304):
        return n
    return best


def _row_tile(m, cap):
    if m <= cap:
        return m
    t = cap
    while m % t:
        t //= 2
    return t


def _after_spec(after):
    return [] if after is None else [pl.BlockSpec(memory_space=pl.ANY)]


def _after_arg(after):
    return [] if after is None else [after]


def _mm_call(name, dims, grid, a_spec, b_spec, o_spec, o_shape, tile, a, b, out_dtypes, epilogue, extra, after):
    nk = grid[2]
    n_x, n_o = len(extra), len(out_dtypes)

    def body(a_ref, b_ref, *refs):
        x_refs, o_refs = refs[:n_x], refs[len(refs) - n_s - n_o:len(refs) - n_s]
        part = lax.dot_general(a_ref[...].astype(BF16), b_ref[...].astype(BF16), (dims, ((), ())),
                               preferred_element_type=F32)

        def finish(acc):
            vals = (acc,) if epilogue is None else epilogue(acc, *[x[...] for x in x_refs])
            for o_ref, val in zip(o_refs, vals):
                o_ref[...] = val.astype(o_ref.dtype)

        if nk == 1:
            finish(part)
        else:
            acc_ref = refs[-1]
            k = pl.program_id(2)

            @pl.when(k == 0)
            def _():
                acc_ref[...] = part

            @pl.when(jnp.logical_and(k > 0, k < nk - 1))
            def _():
                acc_ref[...] += part

            @pl.when(k == nk - 1)
            def _():
                finish(acc_ref[...] + part)

    n_s = 0 if nk == 1 else 1
    return pl.pallas_call(
        body, name=name, out_shape=[jax.ShapeDtypeStruct(o_shape, dt) for dt in out_dtypes], grid=grid,
        in_specs=[a_spec, b_spec] + [o_spec] * n_x + _after_spec(after), out_specs=[o_spec] * n_o,
        scratch_shapes=[pltpu.VMEM(tile, F32)] * n_s,
        compiler_params=_params(("parallel", "parallel", "arbitrary")),
    )(a, b, *extra, *_after_arg(after))


def mm_nn(name, a, w, out_dtypes=(F32,), epilogue=None, extra=(), after=None):
    M, K = a.shape
    G, _, n = w.shape
    tn = _tile(n, 1024)
    tm = _row_tile(M, 512 if tn > 1024 else 2048)
    tk = _row_tile(K, 1024)
    r = n // tn
    return _mm_call(
        name, ((1,), (0,)), (M // tm, G * r, K // tk),
        pl.BlockSpec((tm, tk), lambda i, j, k: (i, k)),
        pl.BlockSpec((None, tk, tn), lambda i, j, k: (j // r, k, j % r)),
        pl.BlockSpec((tm, tn), lambda i, j, k: (i, j)), (M, G * n), (tm, tn),
        a, w, out_dtypes, epilogue, extra, after)


def mm_nt(name, dy, w, out_dtypes=(F32,), epilogue=None, extra=(), after=None):
    M, N = dy.shape
    G, K, n = w.shape
    tn = _tile(n, 1024)
    tm = _row_tile(M, 512 if tn > 1024 else 2048)
    tk = _row_tile(K, 1024)
    r = n // tn
    return _mm_call(
        name, ((1,), (1,)), (M // tm, K // tk, G * r),
        pl.BlockSpec((tm, tn), lambda i, j, k: (i, k)),
        pl.BlockSpec((None, tk, tn), lambda i, j, k: (k // r, j, k % r)),
        pl.BlockSpec((tm, tk), lambda i, j, k: (i, j)), (M, K), (tm, tk),
        dy, w, out_dtypes, epilogue, extra, after)


def mm_tn(name, a, dy, G, out_dtype=F32, after=None):
    M, K = a.shape
    n = dy.shape[1] // G
    tn = _tile(n, 1024)
    tk = _row_tile(K, 512 if tn > 1024 else 1024)
    tm = _row_tile(M, 2048)
    r = n // tn
    return _mm_call(
        name, ((0,), (0,)), (K // tk, G * r, M // tm),
        pl.BlockSpec((tm, tk), lambda i, j, k: (k, i)),
        pl.BlockSpec((tm, tn), lambda i, j, k: (k, j)),
        pl.BlockSpec((None, tk, tn), lambda i, j, k: (j // r, i, j % r)), (G, K, n), (tk, tn),
        a, dy, (out_dtype,), None, (), after)[0]


def rowwise(name, fn, rows, vecs, outs, accs=(), tm=256, after=None):
    rows = [r if isinstance(r, tuple) else (r, r.shape[1], 0) for r in rows]
    n_fn = len(rows) + len(vecs)
    vecs = list(vecs) + _after_arg(after)
    S = rows[0][0].shape[0]
    tm = _row_tile(S, tm)
    n_r, n_v, n_o, n_a = len(rows), len(vecs), len(outs), len(accs)

    def body(*refs):
        ins = [ref[...] for ref in refs[:n_r + n_v]]
        o_refs = refs[n_r + n_v:n_r + n_v + n_o]
        a_refs = refs[n_r + n_v + n_o:]
        res = fn(*ins[:n_fn])
        res = res if isinstance(res, (tuple, list)) else (res,)
        for ref, val in zip(o_refs, res[:n_o]):
            ref[...] = val.astype(ref.dtype)
        if n_a:
            @pl.when(pl.program_id(0) == 0)
            def _():
                for ref in a_refs:
                    ref[...] = jnp.zeros_like(ref)
            for ref, val in zip(a_refs, res[n_o:]):
                ref[...] += val

    in_specs = [pl.BlockSpec((tm, w), functools.partial(lambda cb, i: (i, cb), cb)) for _, w, cb in rows]
    in_specs += [pl.BlockSpec(v.shape, lambda i: (0, 0)) for v in vecs]
    out_specs = [pl.BlockSpec((tm, w), lambda i: (i, 0)) for w, _ in outs]
    out_specs += [pl.BlockSpec(a, lambda i: (0, 0)) for a in accs]
    out_shape = [jax.ShapeDtypeStruct((S, w), dt) for w, dt in outs]
    out_shape += [jax.ShapeDtypeStruct(a, F32) for a in accs]
    res = pl.pallas_call(
        body, name=name, out_shape=out_shape, grid=(S // tm,),
        in_specs=in_specs, out_specs=out_specs,
        compiler_params=_params(("arbitrary",)),
    )(*[r[0] for r in rows], *vecs)
    return res


def _rms(x):
    return x * lax.rsqrt(jnp.mean(x * x, axis=-1, keepdims=True) + EPS)


def _modulate(x, gain, scale, shift):
    return _rms(x) * gain * (1.0 + scale) + shift


def _ada_slices(ada_raw, bias):
    ada = ada_raw + bias
    return [ada[:, i * D_MODEL:(i + 1) * D_MODEL] for i in range(ada.shape[1] // D_MODEL)]


def _norm_wide_heads(y, gain, heads):
    w = y.shape[1] // heads
    return jnp.concatenate([_rms(y[:, h * w:(h + 1) * w]) * gain[:, h * w:(h + 1) * w] for h in range(heads)], axis=1)


def _norm_fox_heads(x, gain):
    outs = []
    for p in range(x.shape[1] // LANES):
        blk = x[:, p * LANES:(p + 1) * LANES]
        low = lax.broadcasted_iota(jnp.int32, blk.shape, 1) < FOX_DH
        sq = blk * blk
        ss_low = jnp.sum(jnp.where(low, sq, 0.0), axis=1, keepdims=True)
        ss_high = jnp.sum(jnp.where(low, 0.0, sq), axis=1, keepdims=True)
        outs.append(blk * lax.rsqrt(jnp.where(low, ss_low, ss_high) * (1.0 / FOX_DH) + EPS))
    return jnp.concatenate(outs, axis=1) * gain


def _silu(x):
    return x * jax.nn.sigmoid(x)


def _log_sigmoid(z):
    return -(jnp.maximum(-z, 0.0) + jnp.log(1.0 + jnp.exp(-jnp.abs(z))))


def _rotate(x, cos, sin, heads, sign):
    w = x.shape[1] // heads
    half = w // 2
    outs = []
    for h in range(heads):
        x1 = x[:, h * w:h * w + half]
        x2 = x[:, h * w + half:(h + 1) * w]
        outs += [x1 * cos - sign * x2 * sin, sign * x1 * sin + x2 * cos]
    return jnp.concatenate(outs, axis=1)


def _vjp(fn, primals, ct):
    return jax.vjp(fn, *primals)[1](ct)


_LOG_GAMMAS = [float(np.log(np.float32(1.0) - np.float32(2.0) ** np.float32(-5.0 - h))) for h in range(RET_HEADS)]


RET_ROWS = 512


def retention(name, q, k, v, reverse, out_dtype=F32):
    (qa, dk, qo), (ka, _, ko), (va, dv, vo) = q, k, v
    S = qa.shape[0]
    C = RET_CHUNK
    rows = _row_tile(S, RET_ROWS)
    nb = S // rows

    def body(q_ref, k_ref, v_ref, o_ref, state):
        h = pl.program_id(0)

        @pl.when(pl.program_id(1) == 0)
        def _():
            state[...] = jnp.zeros_like(state)

        log_g = jnp.float32(_LOG_GAMMAS[RET_HEADS - 1])
        for i in range(RET_HEADS - 2, -1, -1):
            log_g = jnp.where(h == i, jnp.float32(_LOG_GAMMAS[i]), log_g)
        row = lax.broadcasted_iota(jnp.int32, (C, C), 0)
        col = lax.broadcasted_iota(jnp.int32, (C, C), 1)
        rel = (col - row if reverse else row - col).astype(F32)
        decay = jnp.where(rel >= 0, jnp.exp(log_g * jnp.maximum(rel, 0.0)), 0.0)
        j = lax.broadcasted_iota(jnp.int32, (C, 1), 0).astype(F32)
        q_decay = jnp.exp(log_g * (C - j if reverse else j + 1.0))
        k_decay = jnp.exp(log_g * (j if reverse else C - 1.0 - j))
        chunk_decay = jnp.exp(jnp.full((1, 1), log_g * C, F32))

        chunks = range(rows // C)
        for ci in (reversed(chunks) if reverse else chunks):
            rs = slice(ci * C, (ci + 1) * C)
            qc = q_ref[rs, :].astype(BF16)
            kf = k_ref[rs, :].astype(F32)
            vc = v_ref[rs, :].astype(BF16)
            scores = lax.dot_general(qc, kf.astype(BF16), (((1,), (1,)), ((), ())), preferred_element_type=F32) * decay
            intra = jnp.dot(scores.astype(BF16), vc, preferred_element_type=F32)
            cross = jnp.dot(qc, state[...].astype(BF16), preferred_element_type=F32) * q_decay
            o_ref[rs, :] = (intra + cross).astype(o_ref.dtype)
            upd = lax.dot_general((kf * k_decay).astype(BF16), vc, (((0,), (0,)), ((), ())), preferred_element_type=F32)
            state[...] = state[...] * chunk_decay + upd

    def block(i):
        return nb - 1 - i if reverse else i

    return pl.pallas_call(
        body, name=name, out_shape=jax.ShapeDtypeStruct((S, RET_HEADS * dv), out_dtype),
        grid=(RET_HEADS, nb),
        in_specs=[pl.BlockSpec((rows, dk), lambda h, i: (block(i), qo + h)),
                  pl.BlockSpec((rows, dk), lambda h, i: (block(i), ko + h)),
                  pl.BlockSpec((rows, dv), lambda h, i: (block(i), vo + h))],
        out_specs=pl.BlockSpec((rows, dv), lambda h, i: (block(i), h)),
        scratch_shapes=[pltpu.VMEM((dk, dv), F32)],
        compiler_params=_params(("parallel", "arbitrary")),
    )(qa, ka, va)


FOX_T = 256
N_PAIR = FOX_HEADS // 2
FOX_SCALE = FOX_DH ** -0.5


def _fox_heads(q2):
    low = lax.broadcasted_iota(jnp.int32, (1, LANES), 1) < FOX_DH
    return [(mask, jnp.where(mask, q2 * FOX_SCALE, 0.0).astype(BF16)) for mask in (low, jnp.logical_not(low))]


def _fox_parts(j, t):
    return ([(0, j * t, False)] if j else []) + [(j * t, (j + 1) * t, True)]


def _fox_scores(qa, k_ref, ft_ref, head, lo, hi, diagonal):
    k_blk = k_ref[lo:hi, :].astype(BF16)
    s = lax.dot_general(qa, k_blk, (((1,), (1,)), ((), ())), preferred_element_type=F32) - ft_ref[pl.ds(head, 1), lo:hi]
    if diagonal:
        n = hi - lo
        s = jnp.where(lax.broadcasted_iota(jnp.int32, (n, n), 1) <= lax.broadcasted_iota(jnp.int32, (n, n), 0), s, -jnp.inf)
    return s


def fox_forward(name, qn, kn, kvf, f_cum_t):
    S = qn.shape[0]
    t = _row_tile(S, FOX_T)
    v_block0 = D_MODEL // LANES

    def variant(j, pair, q_ref, k_ref, v_ref, ft_ref, y_ref, lse_ref):
        ys, lses = [], []
        for a, (mask, qa) in enumerate(_fox_heads(q_ref[...])):
            parts = [(lo, hi, _fox_scores(qa, k_ref, ft_ref, 2 * pair + a, lo, hi, dg)) for lo, hi, dg in _fox_parts(j, t)]
            m = functools.reduce(jnp.maximum, [jnp.max(s, axis=1, keepdims=True) for _, _, s in parts])
            l, acc = 0.0, 0.0
            for lo, hi, s in parts:
                e = jnp.exp(s - m)
                l = l + jnp.sum(e, axis=1, keepdims=True)
                acc = acc + jnp.dot(e.astype(BF16), v_ref[lo:hi, :].astype(BF16), preferred_element_type=F32)
            ys.append(acc / l)
            lses.append(m + jnp.log(l))
        low = lax.broadcasted_iota(jnp.int32, (1, LANES), 1) < FOX_DH
        y_ref[...] = jnp.where(low, ys[0], ys[1])
        lse_ref[...] = jnp.where(low, lses[0], lses[1])

    def body(*refs):
        pair, i = pl.program_id(0), pl.program_id(1)
        for j in range(S // t):
            pl.when(i == j)(functools.partial(variant, j, pair, *refs))

    return pl.pallas_call(
        body, name=name,
        out_shape=[jax.ShapeDtypeStruct((S, D_MODEL), F32), jax.ShapeDtypeStruct((S, D_MODEL), F32)],
        grid=(N_PAIR, S // t),
        in_specs=[pl.BlockSpec((t, LANES), lambda p, i: (i, p)),
                  pl.BlockSpec((S, LANES), lambda p, i: (0, p)),
                  pl.BlockSpec((S, LANES), lambda p, i: (0, v_block0 + p)),
                  pl.BlockSpec((LANES, S), lambda p, i: (0, 0))],
        out_specs=[pl.BlockSpec((t, LANES), lambda p, i: (i, p)),
                   pl.BlockSpec((t, LANES), lambda p, i: (i, p))],
        compiler_params=_params(("parallel", "arbitrary")),
    )(qn, kn, kvf, f_cum_t)


def fox_backward(name, qn, kn, kvf, f_cum_t, y, dy, lse):
    S = qn.shape[0]
    t = _row_tile(S, FOX_T)
    v_block0 = D_MODEL // LANES

    def variant(j, pair, q_ref, k_ref, v_ref, ft_ref, y_ref, dy_ref, lse_ref, dq_ref, dk_ref, dv_ref, dfq_ref, dfk_ref):
        y2, dy2, lse2 = y_ref[...], dy_ref[...], lse_ref[...]
        lane = lax.broadcasted_iota(jnp.int32, (t, LANES), 1)
        dqs, dfq = [], jnp.zeros((t, LANES), F32)
        for a, (mask, qa) in enumerate(_fox_heads(q_ref[...])):
            lse_a = jnp.max(jnp.where(mask, lse2, -jnp.inf), axis=1, keepdims=True)
            dy_a = jnp.where(mask, dy2, 0.0)
            delta = jnp.sum(dy_a * y2, axis=1, keepdims=True)
            dy_b = dy_a.astype(BF16)
            dq, row_sum = 0.0, 0.0
            for lo, hi, dg in _fox_parts(j, t):
                p = jnp.exp(_fox_scores(qa, k_ref, ft_ref, 2 * pair + a, lo, hi, dg) - lse_a)
                dp = lax.dot_general(dy_b, v_ref[lo:hi, :].astype(BF16), (((1,), (1,)), ((), ())), preferred_element_type=F32)
                ds = p * (dp - delta)
                row_sum = row_sum + jnp.sum(ds, axis=1, keepdims=True)
                dfk_ref[pl.ds(a, 1), lo:hi] += -jnp.sum(ds, axis=0, keepdims=True)
                ds_b = ds.astype(BF16)
                dq = dq + jnp.dot(ds_b, k_ref[lo:hi, :].astype(BF16), preferred_element_type=F32)
                dk_ref[lo:hi, :] += lax.dot_general(ds_b, qa, (((0,), (0,)), ((), ())), preferred_element_type=F32)
                dv_ref[lo:hi, :] += lax.dot_general(p.astype(BF16), dy_b, (((0,), (0,)), ((), ())), preferred_element_type=F32)
            dqs.append(dq * FOX_SCALE)
            dfq = dfq + jnp.where(lane == 2 * pair + a, row_sum, 0.0)
        low = lax.broadcasted_iota(jnp.int32, (1, LANES), 1) < FOX_DH
        dq_ref[...] = jnp.where(low, dqs[0], dqs[1])
        dfq_ref[...] = dfq

    def body(*refs):
        pair, i = pl.program_id(0), pl.program_id(1)
        dk_ref, dv_ref, dfk_ref = refs[8], refs[9], refs[11]

        @pl.when(i == 0)
        def _():
            dk_ref[...] = jnp.zeros_like(dk_ref)
            dv_ref[...] = jnp.zeros_like(dv_ref)
            dfk_ref[...] = jnp.zeros_like(dfk_ref)

        for j in range(S // t):
            pl.when(i == j)(functools.partial(variant, j, pair, *refs))

    row_blk = pl.BlockSpec((t, LANES), lambda p, i: (i, p))
    col_blk = pl.BlockSpec((S, LANES), lambda p, i: (0, p))
    return pl.pallas_call(
        body, name=name,
        out_shape=[jax.ShapeDtypeStruct((S, D_MODEL), F32)] * 3
        + [jax.ShapeDtypeStruct((N_PAIR, S, LANES), F32), jax.ShapeDtypeStruct((N_PAIR, 8, S), F32)],
        grid=(N_PAIR, S // t),
        in_specs=[row_blk, col_blk,
                  pl.BlockSpec((S, LANES), lambda p, i: (0, v_block0 + p)),
                  pl.BlockSpec((LANES, S), lambda p, i: (0, 0)),
                  row_blk, row_blk, row_blk],
        out_specs=[row_blk, col_blk, col_blk,
                   pl.BlockSpec((None, t, LANES), lambda p, i: (p, i, 0)),
                   pl.BlockSpec((None, 8, S), lambda p, i: (p, 0, 0))],
        compiler_params=_params(("parallel", "arbitrary")),
    )(qn, kn, kvf, f_cum_t, y, dy, lse)


def cumsum_rows(name, x, reverse):
    S = x.shape[0]
    C = LANES
    nc = S // C

    def body(x_ref, o_ref):
        row = lax.broadcasted_iota(jnp.int32, (C, C), 0)
        col = lax.broadcasted_iota(jnp.int32, (C, C), 1)
        tri = jnp.where(col >= row if reverse else col <= row, 1.0, 0.0).astype(F32)
        carry = jnp.zeros((1, LANES), F32)
        for i in (range(nc - 1, -1, -1) if reverse else range(nc)):
            blk = x_ref[i * C:(i + 1) * C, :]
            loc = jnp.dot(tri, blk, preferred_element_type=F32, precision=lax.Precision.HIGHEST)
            o_ref[i * C:(i + 1) * C, :] = loc + carry
            carry = carry + (loc[0:1, :] if reverse else loc[C - 1:C, :])

    return pl.pallas_call(body, name=name, out_shape=jax.ShapeDtypeStruct((S, LANES), F32),
                          compiler_params=_params())(x)


def adamw(name, parts, w, m, v):
    L, R, C = w.shape
    tr = _row_tile(R, 256)
    nr = R // tr
    counts = [len(p) for p in parts]

    def body(*refs):
        w_ref, m_ref, v_ref, g_out, d_out, m_out, v_out = refs[sum(counts):]
        for layer in range(L):
            p_refs = refs[sum(counts[:layer]):sum(counts[:layer + 1])]

            @pl.when(pl.program_id(0) == layer)
            def _(p_refs=p_refs, slots=[n for _, n in parts[layer]]):
                g = None
                for p_ref, n in zip(p_refs, slots):
                    for i in range(n):
                        g = p_ref[i].astype(F32) if g is None else g + p_ref[i].astype(F32)
                m2 = ADAM_B1 * m_ref[...] + (1.0 - ADAM_B1) * g
                v2 = ADAM_B2 * v_ref[...] + (1.0 - ADAM_B2) * jnp.square(g)
                m_hat = m2 / (1.0 - ADAM_B1 ** ADAM_STEP)
                v_hat = v2 / (1.0 - ADAM_B2 ** ADAM_STEP)
                g_out[...] = g
                d_out[...] = -ADAM_LR * (m_hat / (jnp.sqrt(v_hat) + ADAM_EPS) + ADAM_WD * w_ref[...])
                m_out[...] = m2
                v_out[...] = v2

    def part_spec(layer, n):
        return pl.BlockSpec((n, tr, C), lambda l, i: (0, jnp.where(l == layer, i, jnp.where(l < layer, 0, nr - 1)), 0))

    blk = pl.BlockSpec((None, tr, C), lambda l, i: (l, i, 0))
    return pl.pallas_call(
        body, name=name, out_shape=[jax.ShapeDtypeStruct((L, R, C), F32)] * 4, grid=(L, nr),
        in_specs=[part_spec(layer, n) for layer in range(L) for _, n in parts[layer]] + [blk, blk, blk],
        out_specs=[blk] * 4, compiler_params=_params(("arbitrary", "arbitrary")),
    )(*[a for layer in parts for a, _ in layer], w, m, v)


def kernel(x, c, positions, norm_mix_gain, norm_mlp_gain, w_ada, b_ada, w_mlp_in, w_mlp_out, ret_w_in, ret_norm_gain, ret_w_out, kv_norm_gain, kv_w_ada, kv_b_ada, kv_w, forget_bias, k_norm_gain, fox_w_in, q_norm_gain, fox_w_out, loss_target, m_norm_mix_gain, m_norm_mlp_gain, m_w_ada, m_b_ada, m_w_mlp_in, m_w_mlp_out, m_ret_w_in, m_ret_norm_gain, m_ret_w_out, m_kv_norm_gain, m_kv_w_ada, m_kv_b_ada, m_kv_w, m_forget_bias, m_k_norm_gain, m_fox_w_in, m_q_norm_gain, m_fox_w_out, v_norm_mix_gain, v_norm_mlp_gain, v_w_ada, v_b_ada, v_w_mlp_in, v_w_mlp_out, v_ret_w_in, v_ret_norm_gain, v_ret_w_out, v_kv_norm_gain, v_kv_w_ada, v_kv_b_ada, v_kv_w, v_forget_bias, v_k_norm_gain, v_fox_w_in, v_q_norm_gain, v_fox_w_out):
    D = D_MODEL
    S = x.shape[1]
    x0 = x.reshape(S, D)
    target = loss_target.reshape(S, D)
    me = 4 * lax.axis_index("x") + 2 * lax.axis_index("y") + lax.axis_index("c")
    n_ada = w_ada.shape[2]
    n_kvada = kv_w_ada.shape[1]
    n_kv = kv_w.shape[1]

    c_all, ret_gain = all_gather("gather_c", [c.reshape(1, D), ret_norm_gain.reshape(RET_HEADS, -1)])
    ret_gain = jnp.transpose(ret_gain, (1, 0, 2)).reshape(1, RET_HEADS * RET_V)
    c_act = rowwise("silu_c", _silu, [c_all.reshape(N_DEV, D)], [], [(D, F32)])[0]
    w_ada_cat = jnp.concatenate([w_ada[0], w_ada[1], kv_w_ada], axis=1).astype(BF16)[None]
    n_cat = 2 * n_ada + n_kvada
    ada_part = mm_nn("ada_proj", c_act, w_ada_cat)[0]
    ada_mine = all_to_all("ada_rows", [ada_part.reshape(N_DEV, 1, n_cat)])[0][:, 0]
    ada_raw = [ada_mine[:, l * n_ada:(l + 1) * n_ada].reshape(1, 6 * D) for l in range(2)]
    kvada_raw = ada_mine[:, 2 * n_ada:].reshape(1, 2 * D)
    kv_bias = kv_b_ada.reshape(1, 2 * D)
    kv_gain = kv_norm_gain.reshape(1, D)
    fb = jnp.pad(forget_bias.reshape(1, FOX_HEADS), ((0, 0), (0, LANES - FOX_HEADS)))
    k_gain = jnp.tile(k_norm_gain.reshape(1, FOX_DH), (1, FOX_HEADS))
    q_gain = jnp.tile(q_norm_gain.reshape(1, FOX_DH), (1, FOX_HEADS))

    w_names = ["ret_in", "ret_out", "mlp_in0", "mlp_out0", "kv", "fox_in", "fox_out", "mlp_in1", "mlp_out1"]
    shards = [ret_w_in[0].astype(BF16), ret_w_out[0].astype(BF16), w_mlp_in[0].astype(BF16), w_mlp_out[0].astype(BF16),
              kv_w.astype(BF16), fox_w_in[0].astype(BF16), fox_w_out[0].astype(BF16), w_mlp_in[1].astype(BF16),
              w_mlp_out[1].astype(BF16)]
    two_level = {"ret_in", "ret_out", "mlp_in0", "mlp_out0"}
    w_plans = {name: plan_gather if name in two_level else plan_gather_direct for name in w_names}
    w_handles, token = exchange_start("gather_weights_start", shards,
                                      [_landing(a, (N_DEV,) + a.shape, a.dtype) for a in shards],
                                      [w_plans[name] for name in w_names], after=(ada_mine, ret_gain))
    w_handles = dict(zip(w_names, w_handles))

    def weight(name, after):
        arrived = exchange_wait("gather_wait_" + name, [w_handles[name]], w_plans[name], after)[1]
        if name in two_level:
            arrived = exchange_now("gather_forward_" + name, None, arrived, plan_forward)
        return arrived[0]

    pos = positions.reshape(S, 1).astype(F32)
    half = RET_QK // 2
    inv_freq = jnp.asarray((ROPE_BASE ** (-np.arange(half, dtype=np.float32) / half)).reshape(1, half), F32)

    def angles(p, f):
        ang = p * f
        return jnp.cos(ang), jnp.sin(ang)

    cos, sin = rowwise("rope_table", angles, [pos], [inv_freq], [(half, F32), (half, F32)])

    def mod_mix(layer):
        def fn(xb, ada, bias, gain):
            sh, sc = _ada_slices(ada, bias)[:2]
            return _modulate(xb, gain[layer:layer + 1], sc, sh)
        return fn

    def mod_mlp(layer):
        def fn(xb, ada, bias, gain):
            sh, sc = _ada_slices(ada, bias)[3:5]
            return _modulate(xb, gain[layer:layer + 1], sc, sh)
        return fn

    h1_0 = rowwise("mod_mix0", mod_mix(0), [x0], [ada_raw[0], b_ada[0:1], norm_mix_gain], [(D, BF16)], after=token)[0]
    W_ret_in = weight("ret_in", h1_0)
    proj = mm_nn("ret_proj", h1_0, W_ret_in, (BF16,))[0]

    def rope_fwd(qb, kb, cs, sn):
        return (_rotate(qb.astype(F32), cs, sn, RET_HEADS, 1.0),
                _rotate(kb.astype(F32), cs, sn, RET_HEADS, 1.0) * (RET_QK ** -0.5))

    q_rot, k_rot = rowwise("rope", rope_fwd, [(proj, D, 0), (proj, D, 1), cos, sin], [], [(D, BF16), (D, BF16)])
    v_ret = (proj, RET_V, (2 * D) // RET_V)
    y_ret = retention("ret_fwd", (q_rot, RET_QK, 0), (k_rot, RET_QK, 0), v_ret, reverse=False)

    def ret_gate(yb, gb, gain):
        return _silu(gb.astype(F32)) * _norm_wide_heads(yb, gain, RET_HEADS)

    mixin0 = rowwise("ret_gate", ret_gate, [y_ret, (proj, 2 * D, 2)], [ret_gain], [(2 * D, BF16)])[0]
    W_ret_out = weight("ret_out", mixin0).reshape(1, 2 * D, D)
    mix0 = mm_nn("ret_out", mixin0, W_ret_out)[0]

    def residual_mod(layer, slot):
        def fn(xb, bb, ada, bias, gain):
            s = _ada_slices(ada, bias)
            xn = xb + s[2] * bb
            return xn, _modulate(xn, gain[layer:layer + 1], s[4], s[3])
        return fn

    x1, h2_0 = rowwise("res_mix0", residual_mod(0, 0), [x0, mix0], [ada_raw[0], b_ada[0:1], norm_mlp_gain],
                       [(D, F32), (D, BF16)])

    W_mlp_in, W_mlp_out = {}, {}

    def mlp_forward(tag, h2, layer):
        W_mlp_in[layer] = weight("mlp_in" + tag, h2)
        u, act = mm_nn("mlp_in" + tag, h2, W_mlp_in[layer], (BF16, BF16),
                       epilogue=lambda acc: (acc, jnp.square(jnp.maximum(acc, 0.0))))
        W_mlp_out[layer] = weight("mlp_out" + tag, act).reshape(1, 4 * D, D)
        return u, act, mm_nn("mlp_out" + tag, act, W_mlp_out[layer])[0]

    u0, act0, mlp0 = mlp_forward("0", h2_0, 0)

    def res_mlp0(xb, bb, ada0, bias0, ada1, bias1, kva, kvb, gain_mix, gain_kv):
        xn = xb + _ada_slices(ada0, bias0)[5] * bb
        s1 = _ada_slices(ada1, bias1)
        kv_shift, kv_scale = _ada_slices(kva, kvb)
        return xn, _modulate(xn, gain_kv, kv_scale, kv_shift), _modulate(xn, gain_mix[1:2], s1[1], s1[0])

    x2, h_kv, h1_1 = rowwise("res_mlp0", res_mlp0, [x1, mlp0],
                             [ada_raw[0], b_ada[0:1], ada_raw[1], b_ada[1:2], kvada_raw, kv_bias, norm_mix_gain, kv_gain],
                             [(D, F32), (D, BF16), (D, BF16)])

    kv_full = jnp.transpose(weight("kv", h_kv), (1, 0, 2)).reshape(D, N_DEV * n_kv)
    W_kv = jnp.pad(kv_full, ((0, 0), (0, KV_PAD - N_DEV * n_kv)))[None]
    kvf = mm_nn("kv_proj", h_kv, W_kv)[0]

    def kv_post(kb, fblk, kg, bias):
        head = lax.broadcasted_iota(jnp.int32, fblk.shape, 1) < FOX_HEADS
        return _norm_fox_heads(kb, kg), jnp.where(head, _log_sigmoid(fblk + bias), 0.0)

    kn, log_f = rowwise("kv_post", kv_post, [(kvf, D, 0), (kvf, LANES, 2 * D // LANES)], [k_gain, fb],
                        [(D, F32), (LANES, F32)])
    f_cum = cumsum_rows("f_cumsum", log_f, reverse=False)
    f_cum_t = f_cum.T

    W_fox_in = weight("fox_in", kvf)
    qo = mm_nn("fox_proj", h1_1, W_fox_in)[0]
    qn = rowwise("q_norm", _norm_fox_heads, [(qo, D, 0)], [q_gain], [(D, F32)])[0]
    y_att, lse = fox_forward("fox_fwd", qn, kn, kvf, f_cum_t)
    mixin1 = rowwise("fox_gate", lambda ob, yb: jax.nn.sigmoid(ob) * yb, [(qo, D, 1), y_att], [], [(D, BF16)])[0]
    W_fox_out = weight("fox_out", mixin1).reshape(1, D, D)
    mix1 = mm_nn("fox_out", mixin1, W_fox_out)[0]
    x3, h2_1 = rowwise("res_mix1", residual_mod(1, 0), [x2, mix1], [ada_raw[1], b_ada[1:2], norm_mlp_gain],
                       [(D, F32), (D, BF16)])
    u1, act1, mlp1 = mlp_forward("1", h2_1, 1)

    def scatter_start(tag, gws, after=()):
        lands = [lax.empty((4,) + g.shape[1:], g.dtype) for g in gws]
        return exchange_start("scatter_sibling_start_" + tag, gws, lands, plan_to_sibling, after)

    def scatter_relay(tag, handles, after, start_after=()):
        gws, from_sibling = exchange_wait("scatter_sibling_wait_" + tag, handles, plan_to_sibling, after)
        sums = [chip_sum("chip_sum_%s%d" % (tag, i), g, s) for i, (g, s) in enumerate(zip(gws, from_sibling))]
        lands = [lax.empty((3,) + s.shape[1:], s.dtype) for s in sums]
        return exchange_start("scatter_owner_start_" + tag, sums, lands, plan_to_owners, start_after)

    def scatter_direct_start(tag, gws):
        me = _my_index()
        lands = [_landing(lax.dynamic_index_in_dim(g, me, 0, keepdims=False), g.shape, g.dtype) for g in gws]
        return exchange_start("scatter_direct_start_" + tag, gws, lands, plan_scatter_direct)

    def scatter_direct_finish(tag, handles, after):
        return [[(r, N_DEV)] for r in exchange_wait("scatter_direct_wait_" + tag, handles, plan_scatter_direct, after)[1]]

    def scatter_finish(tag, handles, after):
        sums, received = exchange_wait("scatter_owner_wait_" + tag, handles, plan_to_owners, after)
        return [[(s, 1), (r, 3)] for s, r in zip(sums, received)]

    def loss_head(xb, bb, tb, ada, bias):
        g2 = _ada_slices(ada, bias)[5]
        err = xb + g2 * bb - tb
        dx = err * (1.0 / D)
        loss = 0.5 * jnp.sum(jnp.sum(err * err, axis=1, keepdims=True) * (1.0 / D), axis=0, keepdims=True)
        return dx, (dx * g2), jnp.broadcast_to(loss, (1, LANES)), jnp.sum(dx * bb, axis=0, keepdims=True)

    dx4, dmlp1, loss_acc, dg2_1 = rowwise("loss_head", loss_head, [x3, mlp1, target], [ada_raw[1], b_ada[1:2]],
                                          [(D, F32), (D, BF16)], [(1, LANES), (1, D)])

    def mlp_backward(tag, dmlp, act, u, h2, layer, after=None):
        du = mm_nt("mlp_out_dx" + tag, dmlp, W_mlp_out[layer], (BF16,), extra=(u,), after=after,
                   epilogue=lambda acc, ub: (acc * (2.0 * jnp.maximum(ub.astype(F32), 0.0)),))[0]
        gw_out = mm_tn("mlp_out_dw" + tag, act, dmlp, 1, BF16).reshape(N_DEV, -1, D)
        dh2 = mm_nt("mlp_in_dx" + tag, du, W_mlp_in[layer])[0]
        gw_in = mm_tn("mlp_in_dw" + tag, h2, du, N_DEV, BF16)
        return dh2, gw_in, gw_out

    def mod_backward(layer, slots, gate_slot):
        def fn(xb, dhb, dresb, branchb, ada, bias, gain):
            s = _ada_slices(ada, bias)
            g = gain[layer:layer + 1]
            dx, dgain, dsc, dsh = _vjp(_modulate, (xb, g, s[slots[1]], s[slots[0]]), dhb)
            dx = dx + dresb
            d_branch = dx * s[gate_slot]
            return dx, d_branch, dgain, dsc, dsh, jnp.sum(dx * branchb, axis=0, keepdims=True)
        return fn

    vec = (1, D)
    dh2_1, gw_mlp_in1, gw_mlp_out1 = mlp_backward("1", dmlp1, act1, u1, h2_1, 1)
    scat_a1, token_a1 = scatter_direct_start("a1", [gw_mlp_in1, gw_mlp_out1])
    dx3, dmix1, dgain_mlp1, dsc2_1, dsh2_1, dg1_1 = rowwise(
        "mod_mlp1_bwd", mod_backward(1, (3, 4), 2), [x3, dh2_1, dx4, mix1], [ada_raw[1], b_ada[1:2], norm_mlp_gain],
        [(D, F32), (D, BF16)], [vec] * 4, after=token_a1)
    dmixin1 = mm_nt("fox_out_dx", dmix1, W_fox_out)[0]
    gw_fox_out = mm_tn("fox_out_dw", mixin1, dmix1, 1, BF16).reshape(N_DEV, -1, D)

    def fox_gate_bwd(db, ob, yb):
        sg = jax.nn.sigmoid(ob)
        return db * sg, db * yb * sg * (1.0 - sg)

    dy_att, d_og = rowwise("fox_gate_bwd", fox_gate_bwd, [dmixin1, (qo, D, 1), y_att], [], [(D, F32), (D, F32)])
    dqn, dkn, dv_att, dfq, dfk = fox_backward("fox_bwd", qn, kn, kvf, f_cum_t, y_att, dy_att, lse)

    def q_norm_bwd(qb, db, ogb, gain):
        dq, dgain = _vjp(_norm_fox_heads, (qb, gain), db)
        return jnp.concatenate([dq, ogb], axis=1), dgain

    dqo, dq_gain = rowwise("q_norm_bwd", q_norm_bwd, [(qo, D, 0), dqn, d_og], [q_gain], [(2 * D, BF16)], [vec])
    dh1_1 = mm_nt("fox_proj_dx", dqo, W_fox_in)[0]
    gw_fox_in = mm_tn("fox_proj_dw", h1_1, dqo, N_DEV, BF16)

    dfk_rows = jnp.pad(dfk[:, :2, :].reshape(FOX_HEADS, S).T, ((0, 0), (0, LANES - FOX_HEADS)))

    def df_total(*blks):
        tot = blks[0]
        for b in blks[1:]:
            tot = tot + b
        return tot

    d_fcum = rowwise("df_sum", df_total, [dfk_rows] + [dfq[p] for p in range(N_PAIR)], [], [(LANES, F32)])[0]
    d_logf = cumsum_rows("df_cumsum", d_fcum, reverse=True)

    def kv_post_bwd(kb, fblk, dkb, dvb, dlf, kg, bias):
        dk, dgain = _vjp(_norm_fox_heads, (kb, kg), dkb)
        df = dlf * (1.0 / (1.0 + jnp.exp(fblk + bias)))
        return jnp.concatenate([dk, dvb, df], axis=1), dgain, jnp.sum(df, axis=0, keepdims=True)

    dkvf, dk_gain, dfb = rowwise("kv_post_bwd", kv_post_bwd,
                                 [(kvf, D, 0), (kvf, LANES, 2 * D // LANES), dkn, dv_att, d_logf], [k_gain, fb],
                                 [(KV_PAD, BF16)], [vec, (1, LANES)])
    dh_kv = mm_nt("kv_proj_dx", dkvf, W_kv)[0]
    gw_kv = mm_tn("kv_proj_dw", h_kv, dkvf, 1, BF16)[0, :, :N_DEV * n_kv]
    gw_kv = jnp.transpose(gw_kv.reshape(D, N_DEV, n_kv), (1, 0, 2))
    scat_a2, token_a = scatter_direct_start("a2", [gw_fox_out, gw_fox_in, gw_kv])

    def x2_bwd(xb, dh1b, dhkb, dresb, branchb, ada0, bias0, ada1, bias1, kva, kvb, gain_mix, gain_kv):
        s1 = _ada_slices(ada1, bias1)
        kv_shift, kv_scale = _ada_slices(kva, kvb)
        dxa, dgain_mix, dsc1, dsh1 = _vjp(_modulate, (xb, gain_mix[1:2], s1[1], s1[0]), dh1b)
        dxb, dgain_kv, dkv_scale, dkv_shift = _vjp(_modulate, (xb, gain_kv, kv_scale, kv_shift), dhkb)
        dx = dresb + dxa + dxb
        g2 = _ada_slices(ada0, bias0)[5]
        return (dx, dx * g2, dgain_mix, dsc1, dsh1, dgain_kv, dkv_scale, dkv_shift,
                jnp.sum(dx * branchb, axis=0, keepdims=True))

    (dx2, dmlp0, dgain_mix1, dsc1_1, dsh1_1, dgain_kv, dkv_scale, dkv_shift, dg2_0) = rowwise(
        "x2_bwd", x2_bwd, [x2, dh1_1, dh_kv, dx3, mlp0],
        [ada_raw[0], b_ada[0:1], ada_raw[1], b_ada[1:2], kvada_raw, kv_bias, norm_mix_gain, kv_gain],
        [(D, F32), (D, BF16)], [vec] * 7, after=token_a)

    dh2_0, gw_mlp_in0, gw_mlp_out0 = mlp_backward("0", dmlp0, act0, u0, h2_0, 0)
    dx1, dmix0, dgain_mlp0, dsc2_0, dsh2_0, dg1_0 = rowwise(
        "mod_mlp0_bwd", mod_backward(0, (3, 4), 2), [x1, dh2_0, dx2, mix0], [ada_raw[0], b_ada[0:1], norm_mlp_gain],
        [(D, F32), (D, BF16)], [vec] * 4)
    dmixin0 = mm_nt("ret_out_dx", dmix0, W_ret_out, (BF16,))[0]
    gw_ret_out = mm_tn("ret_out_dw", mixin0, dmix0, 1, BF16).reshape(N_DEV, -1, D)
    scat_b, token_b = scatter_start("b", [gw_mlp_in0, gw_mlp_out0, gw_ret_out])

    def ret_gate_bwd(db, yb, gb, gain):
        return _vjp(ret_gate, (yb, gb.astype(F32), gain), db.astype(F32))

    dy_ret, dgate, dret_gain = rowwise("ret_gate_bwd", ret_gate_bwd, [dmixin0, y_ret, (proj, 2 * D, 2)], [ret_gain],
                                       [(2 * D, BF16), (2 * D, BF16)], [(1, 2 * D)], after=token_b)
    scat_b, token_b = scatter_relay("b", scat_b, dy_ret)
    dy_h = (dy_ret, RET_V, 0)
    dq_rot = retention("ret_dq", dy_h, v_ret, (k_rot, RET_QK, 0), reverse=False)
    dk_rot = retention("ret_dk", v_ret, dy_h, (q_rot, RET_QK, 0), reverse=True)
    dv_ret = retention("ret_dv", (k_rot, RET_QK, 0), (q_rot, RET_QK, 0), dy_h, reverse=True, out_dtype=BF16)

    def rope_bwd(dqb, dkb, dvb, dgb, cs, sn):
        dq = _rotate(dqb, cs, sn, RET_HEADS, -1.0)
        dk = _rotate(dkb, cs, sn, RET_HEADS, -1.0) * (RET_QK ** -0.5)
        return jnp.concatenate([dq, dk, dvb.astype(F32), dgb.astype(F32)], axis=1)

    dproj = rowwise("rope_bwd", rope_bwd, [dq_rot, dk_rot, dv_ret, dgate, cos, sin], [], [(6 * D, BF16)], after=token_b)[0]
    gw_ret_in = mm_tn("ret_proj_dw", h1_0, dproj, N_DEV, BF16)
    scat_c, token_c = scatter_start("c", [gw_ret_in])
    dh1_0 = mm_nt("ret_proj_dx", dproj, W_ret_in, after=token_c)[0]

    def x0_bwd(xb, dhb, dresb, ada, bias, gain):
        s = _ada_slices(ada, bias)
        dx, dgain, dsc, dsh = _vjp(_modulate, (xb, gain[0:1], s[1], s[0]), dhb)
        return dx + dresb, dgain, dsc, dsh

    grad_x, dgain_mix0, dsc1_0, dsh1_0 = rowwise("x0_bwd", x0_bwd, [x0, dh1_0, dx1],
                                                 [ada_raw[0], b_ada[0:1], norm_mix_gain], [(D, F32)], [vec] * 3)

    small = jnp.concatenate([
        dsh1_0, dsc1_0, dg1_0, dsh2_0, dsc2_0, dg2_0,
        dsh1_1, dsc1_1, dg1_1, dsh2_1, dsc2_1, dg2_1,
        dkv_shift, dkv_scale,
        dgain_mix0, dgain_mix1, dgain_mlp0, dgain_mlp1, dgain_kv,
        dret_gain,
        dq_gain.reshape(FOX_HEADS, FOX_DH).sum(axis=0).reshape(1, FOX_DH),
        dk_gain.reshape(FOX_HEADS, FOX_DH).sum(axis=0).reshape(1, FOX_DH),
        dfb,
        loss_acc,
    ], axis=1)
    small_all = all_gather("gather_small", [small])[0]
    loss = jnp.sum(small_all[:, 0, -1])
    scat_c, token_c = scatter_relay("c", scat_c, grad_x, start_after=(small_all,))
    o_ada = 14 * D
    d_ada = small_all[:, 0, :o_ada]
    d_cat = jnp.concatenate([
        lax.dynamic_slice_in_dim(d_ada[:, 0:6 * D], me * n_ada, n_ada, axis=1),
        lax.dynamic_slice_in_dim(d_ada[:, 6 * D:12 * D], me * n_ada, n_ada, axis=1),
        lax.dynamic_slice_in_dim(d_ada[:, 12 * D:14 * D], me * n_kvada, n_kvada, axis=1)], axis=1)
    gw_ada_cat = mm_tn("ada_dw", c_act, d_cat, 1, F32, after=token_c)[0]

    results = {}

    def update(name, parts, w, m, v, layers=1):
        per_layer = parts if layers > 1 else [parts]
        shape = w.shape
        C = shape[-1]
        R = int(np.prod(shape)) // (layers * C)
        per_layer = [p if isinstance(p, list) else [(p, p.shape[0])] for p in per_layer]
        per_layer = [[(a.reshape(a.shape[0], R, C), n) for a, n in p] for p in per_layer]
        outs = adamw("adamw_" + name, per_layer, w.reshape(layers, R, C), m.reshape(layers, R, C), v.reshape(layers, R, C))
        results[name] = tuple(t.reshape(shape) for t in outs)

    def small_parts(lo, width):
        return small_all[:, :, lo:lo + width]

    update("norm_mix_gain", jnp.concatenate([small_parts(o_ada, D), small_parts(o_ada + D, D)], axis=1),
           norm_mix_gain, m_norm_mix_gain, v_norm_mix_gain)
    update("norm_mlp_gain", jnp.concatenate([small_parts(o_ada + 2 * D, D), small_parts(o_ada + 3 * D, D)], axis=1),
           norm_mlp_gain, m_norm_mlp_gain, v_norm_mlp_gain)
    update("w_ada", [gw_ada_cat[None, :, :n_ada], gw_ada_cat[None, :, n_ada:2 * n_ada]], w_ada, m_w_ada, v_w_ada, layers=2)
    update("b_ada", jnp.concatenate([small_parts(0, 6 * D), small_parts(6 * D, 6 * D)], axis=1), b_ada, m_b_ada, v_b_ada)
    o_ret = o_ada + 5 * D
    n_rg = ret_norm_gain.shape[2]
    ret_gain_parts = small_parts(o_ret, 2 * D).reshape(N_DEV, RET_HEADS, RET_V)
    ret_gain_parts = lax.dynamic_slice_in_dim(ret_gain_parts, me * n_rg, n_rg, axis=2)
    update("ret_norm_gain", ret_gain_parts, ret_norm_gain, m_ret_norm_gain, v_ret_norm_gain)
    update("kv_norm_gain", small_parts(o_ada + 4 * D, D), kv_norm_gain, m_kv_norm_gain, v_kv_norm_gain)
    update("kv_w_ada", gw_ada_cat[None, :, 2 * n_ada:], kv_w_ada, m_kv_w_ada, v_kv_w_ada)
    update("kv_b_ada", small_parts(12 * D, 2 * D), kv_b_ada, m_kv_b_ada, v_kv_b_ada)
    o_q = o_ret + 2 * D
    update("forget_bias", small_parts(o_q + 2 * FOX_DH, FOX_HEADS), forget_bias, m_forget_bias, v_forget_bias)
    update("k_norm_gain", small_parts(o_q + FOX_DH, FOX_DH), k_norm_gain, m_k_norm_gain, v_k_norm_gain)
    update("q_norm_gain", small_parts(o_q, FOX_DH), q_norm_gain, m_q_norm_gain, v_q_norm_gain)

    r_mlp_in1, r_mlp_out1 = scatter_direct_finish("a1", scat_a1, results["q_norm_gain"][1])
    r_fox_out, r_fox_in, r_kv = scatter_direct_finish("a2", scat_a2, r_mlp_in1[0][0])
    r_mlp_in0, r_mlp_out0, r_ret_out = scatter_finish("b", scat_b, r_kv[0][0])
    update("kv_w", r_kv, kv_w, m_kv_w, v_kv_w)
    update("fox_w_in", r_fox_in, fox_w_in, m_fox_w_in, v_fox_w_in)
    update("fox_w_out", r_fox_out, fox_w_out, m_fox_w_out, v_fox_w_out)
    update("ret_w_out", r_ret_out, ret_w_out, m_ret_w_out, v_ret_w_out)
    update("w_mlp_in", [r_mlp_in0, r_mlp_in1], w_mlp_in, m_w_mlp_in, v_w_mlp_in, layers=2)
    update("w_mlp_out", [r_mlp_out0, r_mlp_out1], w_mlp_out, m_w_mlp_out, v_w_mlp_out, layers=2)
    r_ret_in = scatter_finish("c", scat_c, results["w_mlp_out"][1])[0]
    update("ret_w_in", r_ret_in, ret_w_in, m_ret_w_in, v_ret_w_in)

    order = ["norm_mix_gain", "norm_mlp_gain", "w_ada", "b_ada", "w_mlp_in", "w_mlp_out", "ret_w_in", "ret_norm_gain",
             "ret_w_out", "kv_norm_gain", "kv_w_ada", "kv_b_ada", "kv_w", "forget_bias", "k_norm_gain", "fox_w_in",
             "q_norm_gain", "fox_w_out"]
    out = [loss, grad_x.reshape(x.shape)]
    for slot in range(4):
        out += [results[n][slot] for n in order]
    return tuple(out)
```

```python
import functools
import math

import numpy as np
import jax
import jax.numpy as jnp
from jax import lax
from jax.experimental import pallas as pl
from jax.experimental.pallas import tpu as pltpu

F32 = jnp.float32
BF16 = jnp.bfloat16

N_DEV = 8
D_MODEL = 1024
RET_HEADS = 4
RET_QK = D_MODEL // RET_HEADS
RET_V = 2 * D_MODEL // RET_HEADS
RET_CHUNK = 128
ROPE_BASE = 10000.0
FOX_HEADS = 16
FOX_DH = D_MODEL // FOX_HEADS
EPS = 1e-6
LANES = 128
KV_PAD = 2 * D_MODEL + LANES

ADAM_LR = 0.001
ADAM_B1 = 0.9
ADAM_B2 = 0.999
ADAM_EPS = 1e-08
ADAM_WD = 0.01
ADAM_STEP = 10

VMEM_LIMIT_BYTES = 56 * 1024 * 1024


def _params(sem=None):
    return pltpu.CompilerParams(dimension_semantics=sem, vmem_limit_bytes=VMEM_LIMIT_BYTES)


def _me():
    return lax.axis_index("x"), lax.axis_index("y"), lax.axis_index("c")


def _peer(k):
    x, y, c = _me()
    return (1 - x if k & 4 else x, 1 - y if k & 2 else y, 1 - c if k & 1 else c)


def _peer_index(k):
    px, py, pc = _peer(k)
    return 4 * px + 2 * py + pc


def _exchange(name, xs, scatter):
    n = len(xs)

    def body(*refs):
        x_refs, o_refs = refs[:n], refs[n:2 * n]
        send_sems, recv_sems, local_sems = refs[2 * n:]
        x, y, c = _me()
        me = 4 * x + 2 * y + c
        local = []
        for i in range(n):
            src = x_refs[i].at[me] if scatter else x_refs[i]
            cp = pltpu.make_async_copy(src, o_refs[i].at[me], local_sems.at[i])
            cp.start()
            local.append(cp)
        remote = []
        for k in range(1, N_DEV):
            for i in range(n):
                src = x_refs[i].at[_peer_index(k)] if scatter else x_refs[i]
                cp = pltpu.make_async_remote_copy(
                    src_ref=src, dst_ref=o_refs[i].at[me],
                    send_sem=send_sems.at[(k - 1) * n + i], recv_sem=recv_sems.at[(k - 1) * n + i],
                    device_id=_peer(k), device_id_type=pl.DeviceIdType.MESH)
                cp.start()
                remote.append(cp)
        for cp in remote:
            cp.wait()
        for cp in local:
            cp.wait()

    out_shape = [jax.ShapeDtypeStruct(x.shape if scatter else (N_DEV,) + x.shape, x.dtype) for x in xs]
    any_spec = pl.BlockSpec(memory_space=pl.ANY)
    return pl.pallas_call(
        body, name=name, out_shape=out_shape,
        in_specs=[any_spec] * n, out_specs=[any_spec] * n,
        scratch_shapes=[pltpu.SemaphoreType.DMA(((N_DEV - 1) * n,)),
                        pltpu.SemaphoreType.DMA(((N_DEV - 1) * n,)),
                        pltpu.SemaphoreType.DMA((n,))],
    )(*xs)


def all_gather(name, xs):
    return _exchange(name, xs, scatter=False)


def all_to_all(name, xs):
    return _exchange(name, xs, scatter=True)


_HBM = pl.BlockSpec(memory_space=pltpu.HBM)
_SEM = pl.BlockSpec(memory_space=pltpu.SEMAPHORE)
_ANY = pl.BlockSpec(memory_space=pl.ANY)
_EFFECT = pltpu.SideEffectType.DATAFLOW_SIDE_EFFECTING

SIBLING = 1
CHIP_PEERS = (2, 4, 6)


def _my_index():
    x, y, c = _me()
    return 4 * x + 2 * y + c


def plan_gather(x, land, me):
    return [(x, land.at[me], k) for k in (SIBLING,) + CHIP_PEERS]


def plan_gather_direct(x, land, me):
    return [(x, land.at[me], k) for k in range(1, N_DEV)]


def plan_scatter_direct(x, land, me):
    return [(x.at[me ^ k], land.at[me], k) for k in range(1, N_DEV)]


def plan_forward(x, land, me):
    return [(x.at[me ^ k], land.at[me ^ k], SIBLING) for k in CHIP_PEERS]


def plan_to_sibling(x, land, me):
    return [(x.at[me ^ SIBLING ^ (2 * q)], land.at[q], SIBLING) for q in range(4)]


def plan_to_owners(x, land, me):
    return [(x.at[q], land.at[q - 1], 2 * q) for q in (1, 2, 3)]


N_COPIES = {plan_gather: 4, plan_gather_direct: 7, plan_scatter_direct: 7, plan_forward: 3, plan_to_sibling: 4,
            plan_to_owners: 3}


def _plans(plan, n):
    return list(plan) if isinstance(plan, (list, tuple)) else [plan] * n


def _plan_copies(plan, x_refs, land_refs, send_sems, recv_sems):
    me = _my_index()
    plans = _plans(plan, len(land_refs))
    return [pltpu.make_async_remote_copy(src_ref=src, dst_ref=dst, send_sem=send_sems[i].at[s], recv_sem=recv_sems[i].at[s],
                                         device_id=_peer(k), device_id_type=pl.DeviceIdType.MESH)
            for i in range(len(land_refs)) for s, (src, dst, k) in enumerate(plans[i](x_refs[i], land_refs[i], me))]


def _landing(block, shape, dtype):
    start = (_my_index(),) + (0,) * (len(shape) - 1)
    return lax.dynamic_update_slice(lax.empty(shape, dtype), block[None], start)


def exchange_now(name, xs, lands, plan):
    n = len(lands)
    n_x = 0 if xs is None else n

    def body(*refs):
        land_in, land_out = refs[n_x:n_x + n], refs[n_x + n:n_x + 2 * n]
        x_refs = land_in if xs is None else refs[:n]
        sems = refs[n_x + 2 * n:]
        copies = _plan_copies(plan, x_refs, land_out, sems[:n], sems[n:])
        for cp in copies:
            cp.start()
        for cp in copies:
            cp.wait()

    return pl.pallas_call(
        body, name=name, out_shape=[jax.ShapeDtypeStruct(a.shape, a.dtype) for a in lands],
        in_specs=[_ANY] * (n_x + n), out_specs=[_ANY] * n,
        input_output_aliases={n_x + i: i for i in range(n)},
        scratch_shapes=[pltpu.SemaphoreType.DMA((N_COPIES[p],)) for p in _plans(plan, n)] * 2,
    )(*([] if xs is None else xs), *lands)


def exchange_start(name, xs, lands, plan, after=()):
    n, m = len(xs), len(after)

    def body(*refs):
        x_refs, land_refs = refs[:n], refs[n:2 * n]
        send_sems, recv_sems = refs[2 * n + m:3 * n + m], refs[3 * n + m:4 * n + m]
        token = refs[6 * n + m]
        for cp in _plan_copies(plan, x_refs, land_refs, send_sems, recv_sems):
            cp.start()
        token[...] = jnp.zeros_like(token)

    sems = [pltpu.SemaphoreType.DMA((N_COPIES[p],)) for p in _plans(plan, n)] * 2
    thru = [pltpu.HBM(a.shape, a.dtype) for a in list(xs) + list(lands)]
    res = pl.pallas_call(
        body, name=name,
        out_shape=sems + thru + [jax.ShapeDtypeStruct((8, LANES), F32)],
        in_specs=[_HBM] * (2 * n) + [_ANY] * m,
        out_specs=[_SEM] * (2 * n) + [_HBM] * (2 * n) + [pl.BlockSpec(memory_space=pltpu.VMEM)],
        input_output_aliases={i: 2 * n + i for i in range(2 * n)},
        compiler_params=pltpu.CompilerParams(has_side_effects=_EFFECT),
    )(*[pltpu.with_memory_space_constraint(a, pltpu.HBM) for a in list(xs) + list(lands)], *after)
    handles = [(res[i], res[n + i], res[2 * n + i], res[3 * n + i]) for i in range(n)]
    return handles, res[4 * n]


def exchange_wait(name, handles, plan, after):
    n = len(handles)

    def body(*refs):
        x_refs, land_refs = refs[:n], refs[n:2 * n]
        send_sems, recv_sems = refs[2 * n:3 * n], refs[3 * n:4 * n]
        for cp in _plan_copies(plan, x_refs, land_refs, send_sems, recv_sems):
            cp.wait_send()
            cp.wait_recv()

    xs = [h[2] for h in handles]
    lands = [h[3] for h in handles]
    res = pl.pallas_call(
        body, name=name,
        out_shape=[pltpu.HBM(a.shape, a.dtype) for a in xs + lands],
        in_specs=[_HBM] * (2 * n) + [_SEM] * (2 * n) + [_ANY],
        out_specs=[_HBM] * (2 * n),
        input_output_aliases={i: i for i in range(2 * n)},
        compiler_params=pltpu.CompilerParams(has_side_effects=_EFFECT),
    )(*xs, *lands, *[h[0] for h in handles], *[h[1] for h in handles], after)
    return res[:n], res[n:]


def chip_sum(name, gw, from_sibling):
    _, R, C = gw.shape
    tr = _row_tile(R, 512)
    me = _my_index().astype(jnp.int32).reshape(1)

    def body(me_ref, g_ref, s_ref, o_ref):
        o_ref[...] = (g_ref[...].astype(F32) + s_ref[...].astype(F32)).astype(o_ref.dtype)

    slot = pl.BlockSpec((None, tr, C), lambda q, i, me_ref: (q, i, 0))
    return pl.pallas_call(
        body, name=name, out_shape=jax.ShapeDtypeStruct((4, R, C), BF16),
        grid_spec=pltpu.PrefetchScalarGridSpec(
            num_scalar_prefetch=1, grid=(4, R // tr),
            in_specs=[pl.BlockSpec((None, tr, C), lambda q, i, me_ref: (me_ref[0] ^ (2 * q), i, 0)), slot],
            out_specs=slot),
        compiler_params=_params(("arbitrary", "arbitrary")),
    )(me, gw, from_sibling)


def _tile(n, cap):
    best = None
    for t in range(LANES, min(n, cap) + 1, LANES):
        if n % t == 0:
            best = t
    if best is None or (best < 256 and n <= 2304):
        return n
    return best


def _row_tile(m, cap):
    if m <= cap:
        return m
    t = cap
    while m % t:
        t //= 2
    return t


def _after_spec(after):
    return [] if after is None else [pl.BlockSpec(memory_space=pl.ANY)]


def _after_arg(after):
    return [] if after is None else [after]


def _mm_call(name, dims, grid, a_spec, b_spec, o_spec, o_shape, tile, a, b, out_dtypes, epilogue, extra, after):
    nk = grid[2]
    n_x, n_o = len(extra), len(out_dtypes)

    def body(a_ref, b_ref, *refs):
        x_refs, o_refs = refs[:n_x], refs[len(refs) - n_s - n_o:len(refs) - n_s]
        part = lax.dot_general(a_ref[...].astype(BF16), b_ref[...].astype(BF16), (dims, ((), ())),
                               preferred_element_type=F32)

        def finish(acc):
            vals = (acc,) if epilogue is None else epilogue(acc, *[x[...] for x in x_refs])
            for o_ref, val in zip(o_refs, vals):
                o_ref[...] = val.astype(o_ref.dtype)

        if nk == 1:
            finish(part)
        else:
            acc_ref = refs[-1]
            k = pl.program_id(2)

            @pl.when(k == 0)
            def _():
                acc_ref[...] = part

            @pl.when(jnp.logical_and(k > 0, k < nk - 1))
            def _():
                acc_ref[...] += part

            @pl.when(k == nk - 1)
            def _():
                finish(acc_ref[...] + part)

    n_s = 0 if nk == 1 else 1
    return pl.pallas_call(
        body, name=name, out_shape=[jax.ShapeDtypeStruct(o_shape, dt) for dt in out_dtypes], grid=grid,
        in_specs=[a_spec, b_spec] + [o_spec] * n_x + _after_spec(after), out_specs=[o_spec] * n_o,
        scratch_shapes=[pltpu.VMEM(tile, F32)] * n_s,
        compiler_params=_params(("parallel", "parallel", "arbitrary")),
    )(a, b, *extra, *_after_arg(after))


def mm_nn(name, a, w, out_dtypes=(F32,), epilogue=None, extra=(), after=None):
    M, K = a.shape
    G, _, n = w.shape
    tn = _tile(n, 1024)
    tm = _row_tile(M, 512 if tn > 1024 else 2048)
    tk = _row_tile(K, 1024)
    r = n // tn
    return _mm_call(
        name, ((1,), (0,)), (M // tm, G * r, K // tk),
        pl.BlockSpec((tm, tk), lambda i, j, k: (i, k)),
        pl.BlockSpec((None, tk, tn), lambda i, j, k: (j // r, k, j % r)),
        pl.BlockSpec((tm, tn), lambda i, j, k: (i, j)), (M, G * n), (tm, tn),
        a, w, out_dtypes, epilogue, extra, after)


def mm_nt(name, dy, w, out_dtypes=(F32,), epilogue=None, extra=(), after=None):
    M, N = dy.shape
    G, K, n = w.shape
    tn = _tile(n, 1024)
    tm = _row_tile(M, 512 if tn > 1024 else 2048)
    tk = _row_tile(K, 1024)
    r = n // tn
    return _mm_call(
        name, ((1,), (1,)), (M // tm, K // tk, G * r),
        pl.BlockSpec((tm, tn), lambda i, j, k: (i, k)),
        pl.BlockSpec((None, tk, tn), lambda i, j, k: (k // r, j, k % r)),
        pl.BlockSpec((tm, tk), lambda i, j, k: (i, j)), (M, K), (tm, tk),
        dy, w, out_dtypes, epilogue, extra, after)


def mm_tn(name, a, dy, G, out_dtype=F32, after=None):
    M, K = a.shape
    n = dy.shape[1] // G
    tn = _tile(n, 1024)
    tk = _row_tile(K, 512 if tn > 1024 else 1024)
    tm = _row_tile(M, 2048)
    r = n // tn
    return _mm_call(
        name, ((0,), (0,)), (K // tk, G * r, M // tm),
        pl.BlockSpec((tm, tk), lambda i, j, k: (k, i)),
        pl.BlockSpec((tm, tn), lambda i, j, k: (k, j)),
        pl.BlockSpec((None, tk, tn), lambda i, j, k: (j // r, i, j % r)), (G, K, n), (tk, tn),
        a, dy, (out_dtype,), None, (), after)[0]


def rowwise(name, fn, rows, vecs, outs, accs=(), tm=256, after=None):
    rows = [r if isinstance(r, tuple) else (r, r.shape[1], 0) for r in rows]
    n_fn = len(rows) + len(vecs)
    vecs = list(vecs) + _after_arg(after)
    S = rows[0][0].shape[0]
    tm = _row_tile(S, tm)
    n_r, n_v, n_o, n_a = len(rows), len(vecs), len(outs), len(accs)

    def body(*refs):
        ins = [ref[...] for ref in refs[:n_r + n_v]]
        o_refs = refs[n_r + n_v:n_r + n_v + n_o]
        a_refs = refs[n_r + n_v + n_o:]
        res = fn(*ins[:n_fn])
        res = res if isinstance(res, (tuple, list)) else (res,)
        for ref, val in zip(o_refs, res[:n_o]):
            ref[...] = val.astype(ref.dtype)
        if n_a:
            @pl.when(pl.program_id(0) == 0)
            def _():
                for ref in a_refs:
                    ref[...] = jnp.zeros_like(ref)
            for ref, val in zip(a_refs, res[n_o:]):
                ref[...] += val

    in_specs = [pl.BlockSpec((tm, w), functools.partial(lambda cb, i: (i, cb), cb)) for _, w, cb in rows]
    in_specs += [pl.BlockSpec(v.shape, lambda i: (0, 0)) for v in vecs]
    out_specs = [pl.BlockSpec((tm, w), lambda i: (i, 0)) for w, _ in outs]
    out_specs += [pl.BlockSpec(a, lambda i: (0, 0)) for a in accs]
    out_shape = [jax.ShapeDtypeStruct((S, w), dt) for w, dt in outs]
    out_shape += [jax.ShapeDtypeStruct(a, F32) for a in accs]
    res = pl.pallas_call(
        body, name=name, out_shape=out_shape, grid=(S // tm,),
        in_specs=in_specs, out_specs=out_specs,
        compiler_params=_params(("arbitrary",)),
    )(*[r[0] for r in rows], *vecs)
    return res


def _rms(x):
    return x * lax.rsqrt(jnp.mean(x * x, axis=-1, keepdims=True) + EPS)


def _modulate(x, gain, scale, shift):
    return _rms(x) * gain * (1.0 + scale) + shift


def _ada_slices(ada_raw, bias):
    ada = ada_raw + bias
    return [ada[:, i * D_MODEL:(i + 1) * D_MODEL] for i in range(ada.shape[1] // D_MODEL)]


def _norm_wide_heads(y, gain, heads):
    w = y.shape[1] // heads
    return jnp.concatenate([_rms(y[:, h * w:(h + 1) * w]) * gain[:, h * w:(h + 1) * w] for h in range(heads)], axis=1)


def _norm_fox_heads(x, gain):
    outs = []
    for p in range(x.shape[1] // LANES):
        blk = x[:, p * LANES:(p + 1) * LANES]
        low = lax.broadcasted_iota(jnp.int32, blk.shape, 1) < FOX_DH
        sq = blk * blk
        ss_low = jnp.sum(jnp.where(low, sq, 0.0), axis=1, keepdims=True)
        ss_high = jnp.sum(jnp.where(low, 0.0, sq), axis=1, keepdims=True)
        outs.append(blk * lax.rsqrt(jnp.where(low, ss_low, ss_high) * (1.0 / FOX_DH) + EPS))
    return jnp.concatenate(outs, axis=1) * gain


def _silu(x):
    return x * jax.nn.sigmoid(x)


def _log_sigmoid(z):
    return -(jnp.maximum(-z, 0.0) + jnp.log(1.0 + jnp.exp(-jnp.abs(z))))


def _rotate(x, cos, sin, heads, sign):
    w = x.shape[1] // heads
    half = w // 2
    outs = []
    for h in range(heads):
        x1 = x[:, h * w:h * w + half]
        x2 = x[:, h * w + half:(h + 1) * w]
        outs += [x1 * cos - sign * x2 * sin, sign * x1 * sin + x2 * cos]
    return jnp.concatenate(outs, axis=1)


def _vjp(fn, primals, ct):
    return jax.vjp(fn, *primals)[1](ct)


_LOG_GAMMAS = [float(np.log(np.float32(1.0) - np.float32(2.0) ** np.float32(-5.0 - h))) for h in range(RET_HEADS)]


RET_ROWS = 512


def retention(name, q, k, v, reverse, out_dtype=F32):
    (qa, dk, qo), (ka, _, ko), (va, dv, vo) = q, k, v
    S = qa.shape[0]
    C = RET_CHUNK
    rows = _row_tile(S, RET_ROWS)
    nb = S // rows

    def body(q_ref, k_ref, v_ref, o_ref, state):
        h = pl.program_id(0)

        @pl.when(pl.program_id(1) == 0)
        def _():
            state[...] = jnp.zeros_like(state)

        log_g = jnp.float32(_LOG_GAMMAS[RET_HEADS - 1])
        for i in range(RET_HEADS - 2, -1, -1):
            log_g = jnp.where(h == i, jnp.float32(_LOG_GAMMAS[i]), log_g)
        row = lax.broadcasted_iota(jnp.int32, (C, C), 0)
        col = lax.broadcasted_iota(jnp.int32, (C, C), 1)
        rel = (col - row if reverse else row - col).astype(F32)
        decay = jnp.where(rel >= 0, jnp.exp(log_g * jnp.maximum(rel, 0.0)), 0.0)
        j = lax.broadcasted_iota(jnp.int32, (C, 1), 0).astype(F32)
        q_decay = jnp.exp(log_g * (C - j if reverse else j + 1.0))
        k_decay = jnp.exp(log_g * (j if reverse else C - 1.0 - j))
        chunk_decay = jnp.exp(jnp.full((1, 1), log_g * C, F32))

        chunks = range(rows // C)
        for ci in (reversed(chunks) if reverse else chunks):
            rs = slice(ci * C, (ci + 1) * C)
            qc = q_ref[rs, :].astype(BF16)
            kf = k_ref[rs, :].astype(F32)
            vc = v_ref[rs, :].astype(BF16)
            scores = lax.dot_general(qc, kf.astype(BF16), (((1,), (1,)), ((), ())), preferred_element_type=F32) * decay
            intra = jnp.dot(scores.astype(BF16), vc, preferred_element_type=F32)
            cross = jnp.dot(qc, state[...].astype(BF16), preferred_element_type=F32) * q_decay
            o_ref[rs, :] = (intra + cross).astype(o_ref.dtype)
            upd = lax.dot_general((kf * k_decay).astype(BF16), vc, (((0,), (0,)), ((), ())), preferred_element_type=F32)
            state[...] = state[...] * chunk_decay + upd

    def block(i):
        return nb - 1 - i if reverse else i

    return pl.pallas_call(
        body, name=name, out_shape=jax.ShapeDtypeStruct((S, RET_HEADS * dv), out_dtype),
        grid=(RET_HEADS, nb),
        in_specs=[pl.BlockSpec((rows, dk), lambda h, i: (block(i), qo + h)),
                  pl.BlockSpec((rows, dk), lambda h, i: (block(i), ko + h)),
                  pl.BlockSpec((rows, dv), lambda h, i: (block(i), vo + h))],
        out_specs=pl.BlockSpec((rows, dv), lambda h, i: (block(i), h)),
        scratch_shapes=[pltpu.VMEM((dk, dv), F32)],
        compiler_params=_params(("parallel", "arbitrary")),
    )(qa, ka, va)


FOX_T = 256
N_PAIR = FOX_HEADS // 2
FOX_SCALE = FOX_DH ** -0.5


def _fox_heads(q2):
    low = lax.broadcasted_iota(jnp.int32, (1, LANES), 1) < FOX_DH
    return [(mask, jnp.where(mask, q2 * FOX_SCALE, 0.0).astype(BF16)) for mask in (low, jnp.logical_not(low))]


def _fox_parts(j, t):
    return ([(0, j * t, False)] if j else []) + [(j * t, (j + 1) * t, True)]


def _fox_scores(qa, k_ref, ft_ref, head, lo, hi, diagonal):
    k_blk = k_ref[lo:hi, :].astype(BF16)
    s = lax.dot_general(qa, k_blk, (((1,), (1,)), ((), ())), preferred_element_type=F32) - ft_ref[pl.ds(head, 1), lo:hi]
    if diagonal:
        n = hi - lo
        s = jnp.where(lax.broadcasted_iota(jnp.int32, (n, n), 1) <= lax.broadcasted_iota(jnp.int32, (n, n), 0), s, -jnp.inf)
    return s


def fox_forward(name, qn, kn, kvf, f_cum_t):
    S = qn.shape[0]
    t = _row_tile(S, FOX_T)
    v_block0 = D_MODEL // LANES

    def variant(j, pair, q_ref, k_ref, v_ref, ft_ref, y_ref, lse_ref):
        ys, lses = [], []
        for a, (mask, qa) in enumerate(_fox_heads(q_ref[...])):
            parts = [(lo, hi, _fox_scores(qa, k_ref, ft_ref, 2 * pair + a, lo, hi, dg)) for lo, hi, dg in _fox_parts(j, t)]
            m = functools.reduce(jnp.maximum, [jnp.max(s, axis=1, keepdims=True) for _, _, s in parts])
            l, acc = 0.0, 0.0
            for lo, hi, s in parts:
                e = jnp.exp(s - m)
                l = l + jnp.sum(e, axis=1, keepdims=True)
                acc = acc + jnp.dot(e.astype(BF16), v_ref[lo:hi, :].astype(BF16), preferred_element_type=F32)
            ys.append(acc / l)
            lses.append(m + jnp.log(l))
        low = lax.broadcasted_iota(jnp.int32, (1, LANES), 1) < FOX_DH
        y_ref[...] = jnp.where(low, ys[0], ys[1])
        lse_ref[...] = jnp.where(low, lses[0], lses[1])

    def body(*refs):
        pair, i = pl.program_id(0), pl.program_id(1)
        for j in range(S // t):
            pl.when(i == j)(functools.partial(variant, j, pair, *refs))

    return pl.pallas_call(
        body, name=name,
        out_shape=[jax.ShapeDtypeStruct((S, D_MODEL), F32), jax.ShapeDtypeStruct((S, D_MODEL), F32)],
        grid=(N_PAIR, S // t),
        in_specs=[pl.BlockSpec((t, LANES), lambda p, i: (i, p)),
                  pl.BlockSpec((S, LANES), lambda p, i: (0, p)),
                  pl.BlockSpec((S, LANES), lambda p, i: (0, v_block0 + p)),
                  pl.BlockSpec((LANES, S), lambda p, i: (0, 0))],
        out_specs=[pl.BlockSpec((t, LANES), lambda p, i: (i, p)),
                   pl.BlockSpec((t, LANES), lambda p, i: (i, p))],
        compiler_params=_params(("parallel", "arbitrary")),
    )(qn, kn, kvf, f_cum_t)


def fox_backward(name, qn, kn, kvf, f_cum_t, y, dy, lse):
    S = qn.shape[0]
    t = _row_tile(S, FOX_T)
    v_block0 = D_MODEL // LANES

    def variant(j, pair, q_ref, k_ref, v_ref, ft_ref, y_ref, dy_ref, lse_ref, dq_ref, dk_ref, dv_ref, dfq_ref, dfk_ref,
                dv_acc):
        y2, dy2, lse2 = y_ref[...], dy_ref[...], lse_ref[...]
        lane = lax.broadcasted_iota(jnp.int32, (t, LANES), 1)
        dqs, dfq = [], jnp.zeros((t, LANES), F32)
        for a, (mask, qa) in enumerate(_fox_heads(q_ref[...].astype(F32))):
            lse_a = jnp.max(jnp.where(mask, lse2, -jnp.inf), axis=1, keepdims=True)
            dy_a = jnp.where(mask, dy2, 0.0)
            delta = jnp.sum(dy_a * y2, axis=1, keepdims=True)
            dy_b = dy_a.astype(BF16)
            dq, row_sum = 0.0, 0.0
            for lo, hi, dg in _fox_parts(j, t):
                p = jnp.exp(_fox_scores(qa, k_ref, ft_ref, 2 * pair + a, lo, hi, dg) - lse_a)
                dp = lax.dot_general(dy_b, v_ref[lo:hi, :].astype(BF16), (((1,), (1,)), ((), ())), preferred_element_type=F32)
                ds = p * (dp - delta)
                row_sum = row_sum + jnp.sum(ds, axis=1, keepdims=True)
                dfk_ref[pl.ds(a, 1), lo:hi] += -jnp.sum(ds, axis=0, keepdims=True)
                ds_b = ds.astype(BF16)
                dq = dq + jnp.dot(ds_b, k_ref[lo:hi, :].astype(BF16), preferred_element_type=F32)
                dk_ref[lo:hi, :] += lax.dot_general(ds_b, qa, (((0,), (0,)), ((), ())), preferred_element_type=F32)
                dv_acc[lo:hi, :] += lax.dot_general(p.astype(BF16), dy_b, (((0,), (0,)), ((), ())), preferred_element_type=F32)
            dqs.append(dq * FOX_SCALE)
            dfq = dfq + jnp.where(lane == 2 * pair + a, row_sum, 0.0)
        low = lax.broadcasted_iota(jnp.int32, (1, LANES), 1) < FOX_DH
        dq_ref[...] = jnp.where(low, dqs[0], dqs[1])
        dfq_ref[...] = dfq

    def body(*refs):
        pair, i = pl.program_id(0), pl.program_id(1)
        dk_ref, dv_ref, _, dfk_ref, dv_acc = refs[8:13]

        @pl.when(i == 0)
        def _():
            dk_ref[...] = jnp.zeros_like(dk_ref)
            dv_acc[...] = jnp.zeros_like(dv_acc)
            dfk_ref[...] = jnp.zeros_like(dfk_ref)

        for j in range(S // t):
            pl.when(i == j)(functools.partial(variant, j, pair, *refs))

        @pl.when(i == S // t - 1)
        def _():
            dv_ref[...] = dv_acc[...].astype(dv_ref.dtype)

    row_blk = pl.BlockSpec((t, LANES), lambda p, i: (i, p))
    col_blk = pl.BlockSpec((S, LANES), lambda p, i: (0, p))
    return pl.pallas_call(
        body, name=name,
        out_shape=[jax.ShapeDtypeStruct((S, D_MODEL), F32)] * 2
        + [jax.ShapeDtypeStruct((S, D_MODEL), BF16), jax.ShapeDtypeStruct((N_PAIR, S, LANES), F32),
           jax.ShapeDtypeStruct((N_PAIR, 8, S), F32)],
        grid=(N_PAIR, S // t),
        in_specs=[row_blk, col_blk,
                  pl.BlockSpec((S, LANES), lambda p, i: (0, v_block0 + p)),
                  pl.BlockSpec((LANES, S), lambda p, i: (0, 0)),
                  row_blk, row_blk, row_blk],
        out_specs=[row_blk, col_blk, col_blk,
                   pl.BlockSpec((None, t, LANES), lambda p, i: (p, i, 0)),
                   pl.BlockSpec((None, 8, S), lambda p, i: (p, 0, 0))],
        scratch_shapes=[pltpu.VMEM((S, LANES), F32)],
        compiler_params=_params(("parallel", "arbitrary")),
    )(qn, kn, kvf, f_cum_t, y, dy, lse)


def cumsum_rows(name, x, reverse):
    S = x.shape[0]
    C = LANES
    nc = S // C

    def body(x_ref, o_ref):
        row = lax.broadcasted_iota(jnp.int32, (C, C), 0)
        col = lax.broadcasted_iota(jnp.int32, (C, C), 1)
        tri = jnp.where(col >= row if reverse else col <= row, 1.0, 0.0).astype(F32)
        carry = jnp.zeros((1, LANES), F32)
        for i in (range(nc - 1, -1, -1) if reverse else range(nc)):
            blk = x_ref[i * C:(i + 1) * C, :]
            loc = jnp.dot(tri, blk, preferred_element_type=F32, precision=lax.Precision.HIGHEST)
            o_ref[i * C:(i + 1) * C, :] = loc + carry
            carry = carry + (loc[0:1, :] if reverse else loc[C - 1:C, :])

    return pl.pallas_call(body, name=name, out_shape=jax.ShapeDtypeStruct((S, LANES), F32),
                          compiler_params=_params())(x)


def adamw(name, parts, w, m, v):
    L, R, C = w.shape
    tr = _row_tile(R, 256)
    nr = R // tr
    counts = [len(p) for p in parts]

    def body(*refs):
        w_ref, m_ref, v_ref, g_out, d_out, m_out, v_out = refs[sum(counts):]
        for layer in range(L):
            p_refs = refs[sum(counts[:layer]):sum(counts[:layer + 1])]

            @pl.when(pl.program_id(0) == layer)
            def _(p_refs=p_refs, slots=[n for _, n in parts[layer]]):
                g = None
                for p_ref, n in zip(p_refs, slots):
                    for i in range(n):
                        g = p_ref[i].astype(F32) if g is None else g + p_ref[i].astype(F32)
                m2 = ADAM_B1 * m_ref[...] + (1.0 - ADAM_B1) * g
                v2 = ADAM_B2 * v_ref[...] + (1.0 - ADAM_B2) * jnp.square(g)
                m_hat = m2 / (1.0 - ADAM_B1 ** ADAM_STEP)
                v_hat = v2 / (1.0 - ADAM_B2 ** ADAM_STEP)
                g_out[...] = g
                d_out[...] = -ADAM_LR * (m_hat / (jnp.sqrt(v_hat) + ADAM_EPS) + ADAM_WD * w_ref[...])
                m_out[...] = m2
                v_out[...] = v2

    def part_spec(layer, n):
        return pl.BlockSpec((n, tr, C), lambda l, i: (0, jnp.where(l == layer, i, jnp.where(l < layer, 0, nr - 1)), 0))

    blk = pl.BlockSpec((None, tr, C), lambda l, i: (l, i, 0))
    return pl.pallas_call(
        body, name=name, out_shape=[jax.ShapeDtypeStruct((L, R, C), F32)] * 4, grid=(L, nr),
        in_specs=[part_spec(layer, n) for layer in range(L) for _, n in parts[layer]] + [blk, blk, blk],
        out_specs=[blk] * 4, compiler_params=_params(("arbitrary", "arbitrary")),
    )(*[a for layer in parts for a, _ in layer], w, m, v)


def kernel(x, c, positions, norm_mix_gain, norm_mlp_gain, w_ada, b_ada, w_mlp_in, w_mlp_out, ret_w_in, ret_norm_gain, ret_w_out, kv_norm_gain, kv_w_ada, kv_b_ada, kv_w, forget_bias, k_norm_gain, fox_w_in, q_norm_gain, fox_w_out, loss_target, m_norm_mix_gain, m_norm_mlp_gain, m_w_ada, m_b_ada, m_w_mlp_in, m_w_mlp_out, m_ret_w_in, m_ret_norm_gain, m_ret_w_out, m_kv_norm_gain, m_kv_w_ada, m_kv_b_ada, m_kv_w, m_forget_bias, m_k_norm_gain, m_fox_w_in, m_q_norm_gain, m_fox_w_out, v_norm_mix_gain, v_norm_mlp_gain, v_w_ada, v_b_ada, v_w_mlp_in, v_w_mlp_out, v_ret_w_in, v_ret_norm_gain, v_ret_w_out, v_kv_norm_gain, v_kv_w_ada, v_kv_b_ada, v_kv_w, v_forget_bias, v_k_norm_gain, v_fox_w_in, v_q_norm_gain, v_fox_w_out):
    D = D_MODEL
    S = x.shape[1]
    x0 = x.reshape(S, D)
    target = loss_target.reshape(S, D)
    me = 4 * lax.axis_index("x") + 2 * lax.axis_index("y") + lax.axis_index("c")
    n_ada = w_ada.shape[2]
    n_kvada = kv_w_ada.shape[1]
    n_kv = kv_w.shape[1]

    c_all, ret_gain = all_gather("gather_c", [c.reshape(D // LANES, LANES), ret_norm_gain.reshape(RET_HEADS, -1)])
    ret_gain = jnp.transpose(ret_gain, (1, 0, 2)).reshape(1, RET_HEADS * RET_V)
    c_act = rowwise("silu_c", _silu, [c_all.reshape(N_DEV, D)], [], [(D, F32)])[0]
    w_ada_cat = jnp.concatenate([w_ada[0], w_ada[1], kv_w_ada], axis=1).astype(BF16)[None]
    n_cat = 2 * n_ada + n_kvada
    ada_part = mm_nn("ada_proj", c_act, w_ada_cat)[0]
    ada_mine = all_to_all("ada_rows", [ada_part.reshape(N_DEV, n_cat // LANES, LANES)])[0]
    ada_mine = ada_mine.reshape(N_DEV, n_cat)
    ada_raw = [ada_mine[:, l * n_ada:(l + 1) * n_ada].reshape(1, 6 * D) for l in range(2)]
    kvada_raw = ada_mine[:, 2 * n_ada:].reshape(1, 2 * D)
    kv_bias = kv_b_ada.reshape(1, 2 * D)
    kv_gain = kv_norm_gain.reshape(1, D)
    fb = jnp.pad(forget_bias.reshape(1, FOX_HEADS), ((0, 0), (0, LANES - FOX_HEADS)))
    k_gain = jnp.tile(k_norm_gain.reshape(1, FOX_DH), (1, FOX_HEADS))
    q_gain = jnp.tile(q_norm_gain.reshape(1, FOX_DH), (1, FOX_HEADS))

    w_names = ["ret_in", "ret_out", "mlp_in0", "mlp_out0", "kv", "fox_in", "fox_out", "mlp_in1", "mlp_out1"]
    shards = [ret_w_in[0].astype(BF16), ret_w_out[0].astype(BF16), w_mlp_in[0].astype(BF16), w_mlp_out[0].astype(BF16),
              kv_w.astype(BF16), fox_w_in[0].astype(BF16), fox_w_out[0].astype(BF16), w_mlp_in[1].astype(BF16),
              w_mlp_out[1].astype(BF16)]
    two_level = {"ret_in", "ret_out", "mlp_in0", "mlp_out0"}
    w_plans = {name: plan_gather if name in two_level else plan_gather_direct for name in w_names}
    w_handles, token = exchange_start("gather_weights_start", shards,
                                      [_landing(a, (N_DEV,) + a.shape, a.dtype) for a in shards],
                                      [w_plans[name] for name in w_names], after=(ada_mine, ret_gain))
    w_handles = dict(zip(w_names, w_handles))

    def weight(name, after):
        arrived = exchange_wait("gather_wait_" + name, [w_handles[name]], w_plans[name], after)[1]
        if name in two_level:
            arrived = exchange_now("gather_forward_" + name, None, arrived, plan_forward)
        return arrived[0]

    pos = positions.reshape(S, 1).astype(F32)
    half = RET_QK // 2
    inv_freq = jnp.asarray((ROPE_BASE ** (-np.arange(half, dtype=np.float32) / half)).reshape(1, half), F32)

    def angles(p, f):
        ang = p * f
        return jnp.cos(ang), jnp.sin(ang)

    cos, sin = rowwise("rope_table", angles, [pos], [inv_freq], [(half, F32), (half, F32)])

    def mod_mix(layer):
        def fn(xb, ada, bias, gain):
            sh, sc = _ada_slices(ada, bias)[:2]
            return _modulate(xb, gain[layer:layer + 1], sc, sh)
        return fn

    def mod_mlp(layer):
        def fn(xb, ada, bias, gain):
            sh, sc = _ada_slices(ada, bias)[3:5]
            return _modulate(xb, gain[layer:layer + 1], sc, sh)
        return fn

    h1_0 = rowwise("mod_mix0", mod_mix(0), [x0], [ada_raw[0], b_ada[0:1], norm_mix_gain], [(D, BF16)], after=token)[0]
    W_ret_in = weight("ret_in", h1_0)
    proj = mm_nn("ret_proj", h1_0, W_ret_in, (BF16,))[0]

    def rope_fwd(qb, kb, cs, sn):
        return (_rotate(qb.astype(F32), cs, sn, RET_HEADS, 1.0),
                _rotate(kb.astype(F32), cs, sn, RET_HEADS, 1.0) * (RET_QK ** -0.5))

    q_rot, k_rot = rowwise("rope", rope_fwd, [(proj, D, 0), (proj, D, 1), cos, sin], [], [(D, BF16), (D, BF16)])
    v_ret = (proj, RET_V, (2 * D) // RET_V)
    y_ret = retention("ret_fwd", (q_rot, RET_QK, 0), (k_rot, RET_QK, 0), v_ret, reverse=False)

    def ret_gate(yb, gb, gain):
        return _silu(gb.astype(F32)) * _norm_wide_heads(yb, gain, RET_HEADS)

    mixin0 = rowwise("ret_gate", ret_gate, [y_ret, (proj, 2 * D, 2)], [ret_gain], [(2 * D, BF16)])[0]
    W_ret_out = weight("ret_out", mixin0).reshape(1, 2 * D, D)
    mix0 = mm_nn("ret_out", mixin0, W_ret_out)[0]

    def residual_mod(layer, slot):
        def fn(xb, bb, ada, bias, gain):
            s = _ada_slices(ada, bias)
            xn = xb + s[2] * bb
            return xn, _modulate(xn, gain[layer:layer + 1], s[4], s[3])
        return fn

    x1, h2_0 = rowwise("res_mix0", residual_mod(0, 0), [x0, mix0], [ada_raw[0], b_ada[0:1], norm_mlp_gain],
                       [(D, F32), (D, BF16)])

    W_mlp_in, W_mlp_out = {}, {}

    def mlp_forward(tag, h2, layer):
        W_mlp_in[layer] = weight("mlp_in" + tag, h2)
        u, act = mm_nn("mlp_in" + tag, h2, W_mlp_in[layer], (BF16, BF16),
                       epilogue=lambda acc: (acc, jnp.square(jnp.maximum(acc, 0.0))))
        W_mlp_out[layer] = weight("mlp_out" + tag, act).reshape(1, 4 * D, D)
        return u, act, mm_nn("mlp_out" + tag, act, W_mlp_out[layer])[0]

    u0, act0, mlp0 = mlp_forward("0", h2_0, 0)

    def res_mlp0(xb, bb, ada0, bias0, ada1, bias1, kva, kvb, gain_mix, gain_kv):
        xn = xb + _ada_slices(ada0, bias0)[5] * bb
        s1 = _ada_slices(ada1, bias1)
        kv_shift, kv_scale = _ada_slices(kva, kvb)
        return xn, _modulate(xn, gain_kv, kv_scale, kv_shift), _modulate(xn, gain_mix[1:2], s1[1], s1[0])

    x2, h_kv, h1_1 = rowwise("res_mlp0", res_mlp0, [x1, mlp0],
                             [ada_raw[0], b_ada[0:1], ada_raw[1], b_ada[1:2], kvada_raw, kv_bias, norm_mix_gain, kv_gain],
                             [(D, F32), (D, BF16), (D, BF16)])

    kv_full = jnp.transpose(weight("kv", h_kv), (1, 0, 2)).reshape(D, N_DEV * n_kv)
    W_kv = jnp.pad(kv_full, ((0, 0), (0, KV_PAD - N_DEV * n_kv)))[None]
    kvf = mm_nn("kv_proj", h_kv, W_kv)[0]

    def kv_post(kb, fblk, kg, bias):
        head = lax.broadcasted_iota(jnp.int32, fblk.shape, 1) < FOX_HEADS
        return _norm_fox_heads(kb, kg), jnp.where(head, _log_sigmoid(fblk + bias), 0.0)

    kn, log_f = rowwise("kv_post", kv_post, [(kvf, D, 0), (kvf, LANES, 2 * D // LANES)], [k_gain, fb],
                        [(D, BF16), (LANES, F32)])
    f_cum = cumsum_rows("f_cumsum", log_f, reverse=False)
    f_cum_t = f_cum.T

    W_fox_in = weight("fox_in", kvf)
    qo = mm_nn("fox_proj", h1_1, W_fox_in)[0]
    qn = rowwise("q_norm", _norm_fox_heads, [(qo, D, 0)], [q_gain], [(D, BF16)])[0]
    y_att, lse = fox_forward("fox_fwd", qn, kn, kvf, f_cum_t)
    mixin1 = rowwise("fox_gate", lambda ob, yb: jax.nn.sigmoid(ob) * yb, [(qo, D, 1), y_att], [], [(D, BF16)])[0]
    W_fox_out = weight("fox_out", mixin1).reshape(1, D, D)
    mix1 = mm_nn("fox_out", mixin1, W_fox_out)[0]
    x3, h2_1 = rowwise("res_mix1", residual_mod(1, 0), [x2, mix1], [ada_raw[1], b_ada[1:2], norm_mlp_gain],
                       [(D, F32), (D, BF16)])
    u1, act1, mlp1 = mlp_forward("1", h2_1, 1)

    def scatter_start(tag, gws, after=()):
        lands = [lax.empty((4,) + g.shape[1:], g.dtype) for g in gws]
        return exchange_start("scatter_sibling_start_" + tag, gws, lands, plan_to_sibling, after)

    def scatter_relay(tag, handles, after, start_after=()):
        gws, from_sibling = exchange_wait("scatter_sibling_wait_" + tag, handles, plan_to_sibling, after)
        sums = [chip_sum("chip_sum_%s%d" % (tag, i), g, s) for i, (g, s) in enumerate(zip(gws, from_sibling))]
        lands = [lax.empty((3,) + s.shape[1:], s.dtype) for s in sums]
        return exchange_start("scatter_owner_start_" + tag, sums, lands, plan_to_owners, start_after)

    def scatter_direct_start(tag, gws):
        me = _my_index()
        lands = [_landing(lax.dynamic_index_in_dim(g, me, 0, keepdims=False), g.shape, g.dtype) for g in gws]
        return exchange_start("scatter_direct_start_" + tag, gws, lands, plan_scatter_direct)

    def scatter_direct_finish(tag, handles, after):
        return [[(r, N_DEV)] for r in exchange_wait("scatter_direct_wait_" + tag, handles, plan_scatter_direct, after)[1]]

    def scatter_finish(tag, handles, after):
        sums, received = exchange_wait("scatter_owner_wait_" + tag, handles, plan_to_owners, after)
        return [[(s, 1), (r, 3)] for s, r in zip(sums, received)]

    def loss_head(xb, bb, tb, ada, bias):
        g2 = _ada_slices(ada, bias)[5]
        err = xb + g2 * bb - tb
        dx = err * (1.0 / D)
        loss = 0.5 * jnp.sum(jnp.sum(err * err, axis=1, keepdims=True) * (1.0 / D), axis=0, keepdims=True)
        return dx, (dx * g2), jnp.broadcast_to(loss, (1, LANES)), jnp.sum(dx * bb, axis=0, keepdims=True)

    dx4, dmlp1, loss_acc, dg2_1 = rowwise("loss_head", loss_head, [x3, mlp1, target], [ada_raw[1], b_ada[1:2]],
                                          [(D, F32), (D, BF16)], [(1, LANES), (1, D)])

    def mlp_backward(tag, dmlp, act, u, h2, layer, after=None):
        du = mm_nt("mlp_out_dx" + tag, dmlp, W_mlp_out[layer], (BF16,), extra=(u,), after=after,
                   epilogue=lambda acc, ub: (acc * (2.0 * jnp.maximum(ub.astype(F32), 0.0)),))[0]
        gw_out = mm_tn("mlp_out_dw" + tag, act, dmlp, 1, BF16).reshape(N_DEV, -1, D)
        dh2 = mm_nt("mlp_in_dx" + tag, du, W_mlp_in[layer], (BF16,))[0]
        gw_in = mm_tn("mlp_in_dw" + tag, h2, du, N_DEV, BF16)
        return dh2, gw_in, gw_out

    def mod_backward(layer, slots, gate_slot):
        def fn(xb, dhb, dresb, branchb, ada, bias, gain):
            s = _ada_slices(ada, bias)
            g = gain[layer:layer + 1]
            dx, dgain, dsc, dsh = _vjp(_modulate, (xb, g, s[slots[1]], s[slots[0]]), dhb.astype(F32))
            dx = dx + dresb
            d_branch = dx * s[gate_slot]
            return dx, d_branch, dgain, dsc, dsh, jnp.sum(dx * branchb, axis=0, keepdims=True)
        return fn

    vec = (1, D)
    dh2_1, gw_mlp_in1, gw_mlp_out1 = mlp_backward("1", dmlp1, act1, u1, h2_1, 1)
    scat_a1, token_a1 = scatter_direct_start("a1", [gw_mlp_in1, gw_mlp_out1])
    dx3, dmix1, dgain_mlp1, dsc2_1, dsh2_1, dg1_1 = rowwise(
        "mod_mlp1_bwd", mod_backward(1, (3, 4), 2), [x3, dh2_1, dx4, mix1], [ada_raw[1], b_ada[1:2], norm_mlp_gain],
        [(D, F32), (D, BF16)], [vec] * 4, after=token_a1)
    dmixin1 = mm_nt("fox_out_dx", dmix1, W_fox_out)[0]
    gw_fox_out = mm_tn("fox_out_dw", mixin1, dmix1, 1, BF16).reshape(N_DEV, -1, D)

    def fox_gate_bwd(db, ob, yb):
        sg = jax.nn.sigmoid(ob)
        return db * sg, db * yb * sg * (1.0 - sg)

    dy_att, d_og = rowwise("fox_gate_bwd", fox_gate_bwd, [dmixin1, (qo, D, 1), y_att], [], [(D, F32), (D, F32)])
    dqn, dkn, dv_att, dfq, dfk = fox_backward("fox_bwd", qn, kn, kvf, f_cum_t, y_att, dy_att, lse)

    def q_norm_bwd(qb, db, ogb, gain):
        dq, dgain = _vjp(_norm_fox_heads, (qb, gain), db)
        return jnp.concatenate([dq, ogb], axis=1), dgain

    dqo, dq_gain = rowwise("q_norm_bwd", q_norm_bwd, [(qo, D, 0), dqn, d_og], [q_gain], [(2 * D, BF16)], [vec])
    dh1_1 = mm_nt("fox_proj_dx", dqo, W_fox_in, (BF16,))[0]
    gw_fox_in = mm_tn("fox_proj_dw", h1_1, dqo, N_DEV, BF16)

    dfk_rows = jnp.pad(dfk[:, :2, :].reshape(FOX_HEADS, S).T, ((0, 0), (0, LANES - FOX_HEADS)))

    def df_total(*blks):
        tot = blks[0]
        for b in blks[1:]:
            tot = tot + b
        return tot

    d_fcum = rowwise("df_sum", df_total, [dfk_rows] + [dfq[p] for p in range(N_PAIR)], [], [(LANES, F32)])[0]
    d_logf = cumsum_rows("df_cumsum", d_fcum, reverse=True)

    def kv_post_bwd(kb, fblk, dkb, dvb, dlf, kg, bias):
        dk, dgain = _vjp(_norm_fox_heads, (kb, kg), dkb)
        df = dlf * (1.0 / (1.0 + jnp.exp(fblk + bias)))
        return jnp.concatenate([dk, dvb.astype(F32), df], axis=1), dgain, jnp.sum(df, axis=0, keepdims=True)

    dkvf, dk_gain, dfb = rowwise("kv_post_bwd", kv_post_bwd,
                                 [(kvf, D, 0), (kvf, LANES, 2 * D // LANES), dkn, dv_att, d_logf], [k_gain, fb],
                                 [(KV_PAD, BF16)], [vec, (1, LANES)])
    dh_kv = mm_nt("kv_proj_dx", dkvf, W_kv, (BF16,))[0]
    gw_kv = mm_tn("kv_proj_dw", h_kv, dkvf, 1, BF16)[0, :, :N_DEV * n_kv]
    gw_kv = jnp.transpose(gw_kv.reshape(D, N_DEV, n_kv), (1, 0, 2))
    scat_a2, token_a = scatter_direct_start("a2", [gw_fox_out, gw_fox_in, gw_kv])

    def x2_bwd(xb, dh1b, dhkb, dresb, branchb, ada0, bias0, ada1, bias1, kva, kvb, gain_mix, gain_kv):
        s1 = _ada_slices(ada1, bias1)
        kv_shift, kv_scale = _ada_slices(kva, kvb)
        dxa, dgain_mix, dsc1, dsh1 = _vjp(_modulate, (xb, gain_mix[1:2], s1[1], s1[0]), dh1b.astype(F32))
        dxb, dgain_kv, dkv_scale, dkv_shift = _vjp(_modulate, (xb, gain_kv, kv_scale, kv_shift), dhkb.astype(F32))
        dx = dresb + dxa + dxb
        g2 = _ada_slices(ada0, bias0)[5]
        return (dx, dx * g2, dgain_mix, dsc1, dsh1, dgain_kv, dkv_scale, dkv_shift,
                jnp.sum(dx * branchb, axis=0, keepdims=True))

    (dx2, dmlp0, dgain_mix1, dsc1_1, dsh1_1, dgain_kv, dkv_scale, dkv_shift, dg2_0) = rowwise(
        "x2_bwd", x2_bwd, [x2, dh1_1, dh_kv, dx3, mlp0],
        [ada_raw[0], b_ada[0:1], ada_raw[1], b_ada[1:2], kvada_raw, kv_bias, norm_mix_gain, kv_gain],
        [(D, F32), (D, BF16)], [vec] * 7, after=token_a)

    dh2_0, gw_mlp_in0, gw_mlp_out0 = mlp_backward("0", dmlp0, act0, u0, h2_0, 0)
    dx1, dmix0, dgain_mlp0, dsc2_0, dsh2_0, dg1_0 = rowwise(
        "mod_mlp0_bwd", mod_backward(0, (3, 4), 2), [x1, dh2_0, dx2, mix0], [ada_raw[0], b_ada[0:1], norm_mlp_gain],
        [(D, F32), (D, BF16)], [vec] * 4)
    dmixin0 = mm_nt("ret_out_dx", dmix0, W_ret_out, (BF16,))[0]
    gw_ret_out = mm_tn("ret_out_dw", mixin0, dmix0, 1, BF16).reshape(N_DEV, -1, D)
    scat_b, token_b = scatter_start("b", [gw_mlp_in0, gw_mlp_out0, gw_ret_out])

    def ret_gate_bwd(db, yb, gb, gain):
        return _vjp(ret_gate, (yb, gb.astype(F32), gain), db.astype(F32))

    dy_ret, dgate, dret_gain = rowwise("ret_gate_bwd", ret_gate_bwd, [dmixin0, y_ret, (proj, 2 * D, 2)], [ret_gain],
                                       [(2 * D, BF16), (2 * D, BF16)], [(1, 2 * D)], after=token_b)
    scat_b, token_b = scatter_relay("b", scat_b, dy_ret)
    dy_h = (dy_ret, RET_V, 0)
    dq_rot = retention("ret_dq", dy_h, v_ret, (k_rot, RET_QK, 0), reverse=False)
    dk_rot = retention("ret_dk", v_ret, dy_h, (q_rot, RET_QK, 0), reverse=True)
    dv_ret = retention("ret_dv", (k_rot, RET_QK, 0), (q_rot, RET_QK, 0), dy_h, reverse=True, out_dtype=BF16)

    def rope_bwd(dqb, dkb, dvb, dgb, cs, sn):
        dq = _rotate(dqb, cs, sn, RET_HEADS, -1.0)
        dk = _rotate(dkb, cs, sn, RET_HEADS, -1.0) * (RET_QK ** -0.5)
        return jnp.concatenate([dq, dk, dvb.astype(F32), dgb.astype(F32)], axis=1)

    dproj = rowwise("rope_bwd", rope_bwd, [dq_rot, dk_rot, dv_ret, dgate, cos, sin], [], [(6 * D, BF16)], after=token_b)[0]
    gw_ret_in = mm_tn("ret_proj_dw", h1_0, dproj, N_DEV, BF16)
    scat_c, token_c = scatter_start("c", [gw_ret_in])
    dh1_0 = mm_nt("ret_proj_dx", dproj, W_ret_in, (BF16,), after=token_c)[0]

    def x0_bwd(xb, dhb, dresb, ada, bias, gain):
        s = _ada_slices(ada, bias)
        dx, dgain, dsc, dsh = _vjp(_modulate, (xb, gain[0:1], s[1], s[0]), dhb.astype(F32))
        return dx + dresb, dgain, dsc, dsh

    grad_x, dgain_mix0, dsc1_0, dsh1_0 = rowwise("x0_bwd", x0_bwd, [x0, dh1_0, dx1],
                                                 [ada_raw[0], b_ada[0:1], norm_mix_gain], [(D, F32)], [vec] * 3)

    small = jnp.concatenate([
        dsh1_0, dsc1_0, dg1_0, dsh2_0, dsc2_0, dg2_0,
        dsh1_1, dsc1_1, dg1_1, dsh2_1, dsc2_1, dg2_1,
        dkv_shift, dkv_scale,
        dgain_mix0, dgain_mix1, dgain_mlp0, dgain_mlp1, dgain_kv,
        dret_gain,
        dq_gain.reshape(FOX_HEADS, FOX_DH).sum(axis=0).reshape(1, FOX_DH),
        dk_gain.reshape(FOX_HEADS, FOX_DH).sum(axis=0).reshape(1, FOX_DH),
        dfb,
        loss_acc,
    ], axis=1)
    small_all = all_gather("gather_small", [small.reshape(-1, LANES)])[0].reshape(N_DEV, 1, -1)
    loss = jnp.sum(small_all[:, 0, -1])
    scat_c, token_c = scatter_relay("c", scat_c, grad_x, start_after=(small_all,))
    o_ada = 14 * D
    d_ada = small_all[:, 0, :o_ada]
    d_cat = jnp.concatenate([
        lax.dynamic_slice_in_dim(d_ada[:, 0:6 * D], me * n_ada, n_ada, axis=1),
        lax.dynamic_slice_in_dim(d_ada[:, 6 * D:12 * D], me * n_ada, n_ada, axis=1),
        lax.dynamic_slice_in_dim(d_ada[:, 12 * D:14 * D], me * n_kvada, n_kvada, axis=1)], axis=1)
    gw_ada_cat = mm_tn("ada_dw", c_act, d_cat, 1, F32, after=token_c)[0]

    results = {}

    def update(name, parts, w, m, v, layers=1):
        per_layer = parts if layers > 1 else [parts]
        shape = w.shape
        C = shape[-1]
        R = int(np.prod(shape)) // (layers * C)
        per_layer = [p if isinstance(p, list) else [(p, p.shape[0])] for p in per_layer]
        per_layer = [[(a.reshape(a.shape[0], R, C), n) for a, n in p] for p in per_layer]
        outs = adamw("adamw_" + name, per_layer, w.reshape(layers, R, C), m.reshape(layers, R, C), v.reshape(layers, R, C))
        results[name] = tuple(t.reshape(shape) for t in outs)

    def small_parts(lo, width):
        return small_all[:, :, lo:lo + width]

    update("norm_mix_gain", jnp.concatenate([small_parts(o_ada, D), small_parts(o_ada + D, D)], axis=1),
           norm_mix_gain, m_norm_mix_gain, v_norm_mix_gain)
    update("norm_mlp_gain", jnp.concatenate([small_parts(o_ada + 2 * D, D), small_parts(o_ada + 3 * D, D)], axis=1),
           norm_mlp_gain, m_norm_mlp_gain, v_norm_mlp_gain)
    update("w_ada", [gw_ada_cat[None, :, :n_ada], gw_ada_cat[None, :, n_ada:2 * n_ada]], w_ada, m_w_ada, v_w_ada, layers=2)
    update("b_ada", jnp.concatenate([small_parts(0, 6 * D), small_parts(6 * D, 6 * D)], axis=1), b_ada, m_b_ada, v_b_ada)
    o_ret = o_ada + 5 * D
    n_rg = ret_norm_gain.shape[2]
    ret_gain_parts = small_parts(o_ret, 2 * D).reshape(N_DEV, RET_HEADS, RET_V)
    ret_gain_parts = lax.dynamic_slice_in_dim(ret_gain_parts, me * n_rg, n_rg, axis=2)
    update("ret_norm_gain", ret_gain_parts, ret_norm_gain, m_ret_norm_gain, v_ret_norm_gain)
    update("kv_norm_gain", small_parts(o_ada + 4 * D, D), kv_norm_gain, m_kv_norm_gain, v_kv_norm_gain)
    update("kv_w_ada", gw_ada_cat[None, :, 2 * n_ada:], kv_w_ada, m_kv_w_ada, v_kv_w_ada)
    update("kv_b_ada", small_parts(12 * D, 2 * D), kv_b_ada, m_kv_b_ada, v_kv_b_ada)
    o_q = o_ret + 2 * D
    update("forget_bias", small_parts(o_q + 2 * FOX_DH, FOX_HEADS), forget_bias, m_forget_bias, v_forget_bias)
    update("k_norm_gain", small_parts(o_q + FOX_DH, FOX_DH), k_norm_gain, m_k_norm_gain, v_k_norm_gain)
    update("q_norm_gain", small_parts(o_q, FOX_DH), q_norm_gain, m_q_norm_gain, v_q_norm_gain)

    r_mlp_in1, r_mlp_out1 = scatter_direct_finish("a1", scat_a1, results["q_norm_gain"][1])
    r_fox_out, r_fox_in, r_kv = scatter_direct_finish("a2", scat_a2, r_mlp_in1[0][0])
    r_mlp_in0, r_mlp_out0, r_ret_out = scatter_finish("b", scat_b, r_kv[0][0])
    update("kv_w", r_kv, kv_w, m_kv_w, v_kv_w)
    update("fox_w_in", r_fox_in, fox_w_in, m_fox_w_in, v_fox_w_in)
    update("fox_w_out", r_fox_out, fox_w_out, m_fox_w_out, v_fox_w_out)
    update("ret_w_out", r_ret_out, ret_w_out, m_ret_w_out, v_ret_w_out)
    update("w_mlp_in", [r_mlp_in0, r_mlp_in1], w_mlp_in, m_w_mlp_in, v_w_mlp_in, layers=2)
    update("w_mlp_out", [r_mlp_out0, r_mlp_out1], w_mlp_out, m_w_mlp_out, v_w_mlp_out, layers=2)
    r_ret_in = scatter_finish("c", scat_c, results["w_mlp_out"][1])[0]
    update("ret_w_in", r_ret_in, ret_w_in, m_ret_w_in, v_ret_w_in)

    order = ["norm_mix_gain", "norm_mlp_gain", "w_ada", "b_ada", "w_mlp_in", "w_mlp_out", "ret_w_in", "ret_norm_gain",
             "ret_w_out", "kv_norm_gain", "kv_w_ada", "kv_b_ada", "kv_w", "forget_bias", "k_norm_gain", "fox_w_in",
             "q_norm_gain", "fox_w_out"]
    out = [loss, grad_x.reshape(x.shape)]
    for slot in range(4):
        out += [results[n][slot] for n in order]
    return tuple(out)
```

```python
import functools
import math

import numpy as np
import jax
import jax.numpy as jnp
from jax import lax
from jax.experimental import pallas as pl
from jax.experimental.pallas import tpu as pltpu

F32 = jnp.float32
BF16 = jnp.bfloat16

N_DEV = 8
D_MODEL = 1024
RET_HEADS = 4
RET_QK = D_MODEL // RET_HEADS
RET_V = 2 * D_MODEL // RET_HEADS
RET_CHUNK = 128
ROPE_BASE = 10000.0
FOX_HEADS = 16
FOX_DH = D_MODEL // FOX_HEADS
EPS = 1e-6
LANES = 128
KV_PAD = 2 * D_MODEL + LANES

ADAM_LR = 0.001
ADAM_B1 = 0.9
ADAM_B2 = 0.999
ADAM_EPS = 1e-08
ADAM_WD = 0.01
ADAM_STEP = 10

VMEM_LIMIT_BYTES = 56 * 1024 * 1024


def _params(sem=None):
    return pltpu.CompilerParams(dimension_semantics=sem, vmem_limit_bytes=VMEM_LIMIT_BYTES)


def _me():
    return lax.axis_index("x"), lax.axis_index("y"), lax.axis_index("c")


def _peer(k):
    x, y, c = _me()
    return (1 - x if k & 4 else x, 1 - y if k & 2 else y, 1 - c if k & 1 else c)


def _peer_index(k):
    px, py, pc = _peer(k)
    return 4 * px + 2 * py + pc


def _exchange(name, xs, scatter):
    n = len(xs)

    def body(*refs):
        x_refs, o_refs = refs[:n], refs[n:2 * n]
        send_sems, recv_sems, local_sems = refs[2 * n:]
        x, y, c = _me()
        me = 4 * x + 2 * y + c
        local = []
        for i in range(n):
            src = x_refs[i].at[me] if scatter else x_refs[i]
            cp = pltpu.make_async_copy(src, o_refs[i].at[me], local_sems.at[i])
            cp.start()
            local.append(cp)
        remote = []
        for k in range(1, N_DEV):
            for i in range(n):
                src = x_refs[i].at[_peer_index(k)] if scatter else x_refs[i]
                cp = pltpu.make_async_remote_copy(
                    src_ref=src, dst_ref=o_refs[i].at[me],
                    send_sem=send_sems.at[(k - 1) * n + i], recv_sem=recv_sems.at[(k - 1) * n + i],
                    device_id=_peer(k), device_id_type=pl.DeviceIdType.MESH)
                cp.start()
                remote.append(cp)
        for cp in remote:
            cp.wait()
        for cp in local:
            cp.wait()

    out_shape = [jax.ShapeDtypeStruct(x.shape if scatter else (N_DEV,) + x.shape, x.dtype) for x in xs]
    any_spec = pl.BlockSpec(memory_space=pl.ANY)
    return pl.pallas_call(
        body, name=name, out_shape=out_shape,
        in_specs=[any_spec] * n, out_specs=[any_spec] * n,
        scratch_shapes=[pltpu.SemaphoreType.DMA(((N_DEV - 1) * n,)),
                        pltpu.SemaphoreType.DMA(((N_DEV - 1) * n,)),
                        pltpu.SemaphoreType.DMA((n,))],
    )(*xs)


def all_gather(name, xs):
    return _exchange(name, xs, scatter=False)


def all_to_all(name, xs):
    return _exchange(name, xs, scatter=True)


_HBM = pl.BlockSpec(memory_space=pltpu.HBM)
_SEM = pl.BlockSpec(memory_space=pltpu.SEMAPHORE)
_ANY = pl.BlockSpec(memory_space=pl.ANY)
_EFFECT = pltpu.SideEffectType.DATAFLOW_SIDE_EFFECTING

SIBLING = 1
CHIP_PEERS = (2, 4, 6)


def _my_index():
    x, y, c = _me()
    return 4 * x + 2 * y + c


def plan_gather(x, land, me):
    return [(x, land.at[me], k) for k in (SIBLING,) + CHIP_PEERS]


def plan_gather_direct(x, land, me):
    return [(x, land.at[me], k) for k in range(1, N_DEV)]


def plan_scatter_direct(x, land, me):
    return [(x.at[me ^ k], land.at[me], k) for k in range(1, N_DEV)]


def plan_forward(x, land, me):
    return [(x.at[me ^ k], land.at[me ^ k], SIBLING) for k in CHIP_PEERS]


def plan_to_sibling(x, land, me):
    return [(x.at[me ^ SIBLING ^ (2 * q)], land.at[q], SIBLING) for q in range(4)]


def plan_to_owners(x, land, me):
    return [(x.at[q], land.at[q - 1], 2 * q) for q in (1, 2, 3)]


N_COPIES = {plan_gather: 4, plan_gather_direct: 7, plan_scatter_direct: 7, plan_forward: 3, plan_to_sibling: 4,
            plan_to_owners: 3}


def _plans(plan, n):
    return list(plan) if isinstance(plan, (list, tuple)) else [plan] * n


def _plan_copies(plan, x_refs, land_refs, send_sems, recv_sems):
    me = _my_index()
    plans = _plans(plan, len(land_refs))
    return [pltpu.make_async_remote_copy(src_ref=src, dst_ref=dst, send_sem=send_sems[i].at[s], recv_sem=recv_sems[i].at[s],
                                         device_id=_peer(k), device_id_type=pl.DeviceIdType.MESH)
            for i in range(len(land_refs)) for s, (src, dst, k) in enumerate(plans[i](x_refs[i], land_refs[i], me))]


def _landing(block, shape, dtype):
    start = (_my_index(),) + (0,) * (len(shape) - 1)
    return lax.dynamic_update_slice(lax.empty(shape, dtype), block[None], start)


def exchange_now(name, xs, lands, plan):
    n = len(lands)
    n_x = 0 if xs is None else n

    def body(*refs):
        land_in, land_out = refs[n_x:n_x + n], refs[n_x + n:n_x + 2 * n]
        x_refs = land_in if xs is None else refs[:n]
        sems = refs[n_x + 2 * n:]
        copies = _plan_copies(plan, x_refs, land_out, sems[:n], sems[n:])
        for cp in copies:
            cp.start()
        for cp in copies:
            cp.wait()

    return pl.pallas_call(
        body, name=name, out_shape=[jax.ShapeDtypeStruct(a.shape, a.dtype) for a in lands],
        in_specs=[_ANY] * (n_x + n), out_specs=[_ANY] * n,
        input_output_aliases={n_x + i: i for i in range(n)},
        scratch_shapes=[pltpu.SemaphoreType.DMA((N_COPIES[p],)) for p in _plans(plan, n)] * 2,
    )(*([] if xs is None else xs), *lands)


def exchange_start(name, xs, lands, plan, after=()):
    n, m = len(xs), len(after)

    def body(*refs):
        x_refs, land_refs = refs[:n], refs[n:2 * n]
        send_sems, recv_sems = refs[2 * n + m:3 * n + m], refs[3 * n + m:4 * n + m]
        token = refs[6 * n + m]
        for cp in _plan_copies(plan, x_refs, land_refs, send_sems, recv_sems):
            cp.start()
        token[...] = jnp.zeros_like(token)

    sems = [pltpu.SemaphoreType.DMA((N_COPIES[p],)) for p in _plans(plan, n)] * 2
    thru = [pltpu.HBM(a.shape, a.dtype) for a in list(xs) + list(lands)]
    res = pl.pallas_call(
        body, name=name,
        out_shape=sems + thru + [jax.ShapeDtypeStruct((8, LANES), F32)],
        in_specs=[_HBM] * (2 * n) + [_ANY] * m,
        out_specs=[_SEM] * (2 * n) + [_HBM] * (2 * n) + [pl.BlockSpec(memory_space=pltpu.VMEM)],
        input_output_aliases={i: 2 * n + i for i in range(2 * n)},
        compiler_params=pltpu.CompilerParams(has_side_effects=_EFFECT),
    )(*[pltpu.with_memory_space_constraint(a, pltpu.HBM) for a in list(xs) + list(lands)], *after)
    handles = [(res[i], res[n + i], res[2 * n + i], res[3 * n + i]) for i in range(n)]
    return handles, res[4 * n]


def exchange_wait(name, handles, plan, after):
    n = len(handles)

    def body(*refs):
        x_refs, land_refs = refs[:n], refs[n:2 * n]
        send_sems, recv_sems = refs[2 * n:3 * n], refs[3 * n:4 * n]
        for cp in _plan_copies(plan, x_refs, land_refs, send_sems, recv_sems):
            cp.wait_send()
            cp.wait_recv()

    xs = [h[2] for h in handles]
    lands = [h[3] for h in handles]
    res = pl.pallas_call(
        body, name=name,
        out_shape=[pltpu.HBM(a.shape, a.dtype) for a in xs + lands],
        in_specs=[_HBM] * (2 * n) + [_SEM] * (2 * n) + [_ANY],
        out_specs=[_HBM] * (2 * n),
        input_output_aliases={i: i for i in range(2 * n)},
        compiler_params=pltpu.CompilerParams(has_side_effects=_EFFECT),
    )(*xs, *lands, *[h[0] for h in handles], *[h[1] for h in handles], after)
    return res[:n], res[n:]


def chip_sum(name, gw, from_sibling):
    _, R, C = gw.shape
    tr = _row_tile(R, 512)
    me = _my_index().astype(jnp.int32).reshape(1)

    def body(me_ref, g_ref, s_ref, o_ref):
        o_ref[...] = (g_ref[...].astype(F32) + s_ref[...].astype(F32)).astype(o_ref.dtype)

    slot = pl.BlockSpec((None, tr, C), lambda q, i, me_ref: (q, i, 0))
    return pl.pallas_call(
        body, name=name, out_shape=jax.ShapeDtypeStruct((4, R, C), BF16),
        grid_spec=pltpu.PrefetchScalarGridSpec(
            num_scalar_prefetch=1, grid=(4, R // tr),
            in_specs=[pl.BlockSpec((None, tr, C), lambda q, i, me_ref: (me_ref[0] ^ (2 * q), i, 0)), slot],
            out_specs=slot),
        compiler_params=_params(("arbitrary", "arbitrary")),
    )(me, gw, from_sibling)


def _tile(n, cap):
    best = None
    for t in range(LANES, min(n, cap) + 1, LANES):
        if n % t == 0:
            best = t
    if best is None or (best < 256 and n <= 2304):
        return n
    return best


def _row_tile(m, cap):
    if m <= cap:
        return m
    t = cap
    while m % t:
        t //= 2
    return t


def _after_spec(after):
    return [] if after is None else [pl.BlockSpec(memory_space=pl.ANY)]


def _after_arg(after):
    return [] if after is None else [after]


def _mm_call(name, dims, grid, a_spec, b_spec, o_spec, o_shape, tile, a, b, out_dtypes, epilogue, extra, after):
    nk = grid[2]
    n_x, n_o = len(extra), len(out_dtypes)

    def body(a_ref, b_ref, *refs):
        x_refs, o_refs = refs[:n_x], refs[len(refs) - n_s - n_o:len(refs) - n_s]
        part = lax.dot_general(a_ref[...].astype(BF16), b_ref[...].astype(BF16), (dims, ((), ())),
                               preferred_element_type=F32)

        def finish(acc):
            vals = (acc,) if epilogue is None else epilogue(acc, *[x[...] for x in x_refs])
            for o_ref, val in zip(o_refs, vals):
                o_ref[...] = val.astype(o_ref.dtype)

        if nk == 1:
            finish(part)
        else:
            acc_ref = refs[-1]
            k = pl.program_id(2)

            @pl.when(k == 0)
            def _():
                acc_ref[...] = part

            @pl.when(jnp.logical_and(k > 0, k < nk - 1))
            def _():
                acc_ref[...] += part

            @pl.when(k == nk - 1)
            def _():
                finish(acc_ref[...] + part)

    n_s = 0 if nk == 1 else 1
    return pl.pallas_call(
        body, name=name, out_shape=[jax.ShapeDtypeStruct(o_shape, dt) for dt in out_dtypes], grid=grid,
        in_specs=[a_spec, b_spec] + [o_spec] * n_x + _after_spec(after), out_specs=[o_spec] * n_o,
        scratch_shapes=[pltpu.VMEM(tile, F32)] * n_s,
        compiler_params=_params(("parallel", "parallel", "arbitrary")),
    )(a, b, *extra, *_after_arg(after))


def mm_nn(name, a, w, out_dtypes=(F32,), epilogue=None, extra=(), after=None):
    M, K = a.shape
    G, _, n = w.shape
    tn = _tile(n, 1024)
    tm = _row_tile(M, 512 if tn > 1024 else 2048)
    tk = _row_tile(K, 1024)
    r = n // tn
    return _mm_call(
        name, ((1,), (0,)), (M // tm, G * r, K // tk),
        pl.BlockSpec((tm, tk), lambda i, j, k: (i, k)),
        pl.BlockSpec((None, tk, tn), lambda i, j, k: (j // r, k, j % r)),
        pl.BlockSpec((tm, tn), lambda i, j, k: (i, j)), (M, G * n), (tm, tn),
        a, w, out_dtypes, epilogue, extra, after)


def mm_nt(name, dy, w, out_dtypes=(F32,), epilogue=None, extra=(), after=None):
    M, N = dy.shape
    G, K, n = w.shape
    tn = _tile(n, 1024)
    tm = _row_tile(M, 512 if tn > 1024 else 2048)
    tk = _row_tile(K, 1024)
    r = n // tn
    return _mm_call(
        name, ((1,), (1,)), (M // tm, K // tk, G * r),
        pl.BlockSpec((tm, tn), lambda i, j, k: (i, k)),
        pl.BlockSpec((None, tk, tn), lambda i, j, k: (k // r, j, k % r)),
        pl.BlockSpec((tm, tk), lambda i, j, k: (i, j)), (M, K), (tm, tk),
        dy, w, out_dtypes, epilogue, extra, after)


def mm_tn(name, a, dy, G, out_dtype=F32, after=None):
    M, K = a.shape
    n = dy.shape[1] // G
    tn = _tile(n, 1024)
    tk = _row_tile(K, 512 if tn > 1024 else 1024)
    tm = _row_tile(M, 2048)
    r = n // tn
    return _mm_call(
        name, ((0,), (0,)), (K // tk, G * r, M // tm),
        pl.BlockSpec((tm, tk), lambda i, j, k: (k, i)),
        pl.BlockSpec((tm, tn), lambda i, j, k: (k, j)),
        pl.BlockSpec((None, tk, tn), lambda i, j, k: (j // r, i, j % r)), (G, K, n), (tk, tn),
        a, dy, (out_dtype,), None, (), after)[0]


ROW_BLOCK_BYTES = 12 * 1024 * 1024


def rowwise(name, fn, rows, vecs, outs, accs=(), after=None):
    rows = [r if isinstance(r, tuple) else (r, r.shape[1], 0) for r in rows]
    n_fn = len(rows) + len(vecs)
    vecs = list(vecs) + _after_arg(after)
    S = rows[0][0].shape[0]
    row_bytes = sum(w * a.dtype.itemsize for a, w, _ in rows) + sum(w * jnp.dtype(dt).itemsize for w, dt in outs)
    tm = _row_tile(S, 1024)
    while tm > 256 and tm * row_bytes > ROW_BLOCK_BYTES:
        tm //= 2
    n_r, n_v, n_o, n_a = len(rows), len(vecs), len(outs), len(accs)

    def body(*refs):
        ins = [ref[...] for ref in refs[:n_r + n_v]]
        o_refs = refs[n_r + n_v:n_r + n_v + n_o]
        a_refs = refs[n_r + n_v + n_o:]
        res = fn(*ins[:n_fn])
        res = res if isinstance(res, (tuple, list)) else (res,)
        for ref, val in zip(o_refs, res[:n_o]):
            ref[...] = val.astype(ref.dtype)
        if n_a:
            @pl.when(pl.program_id(0) == 0)
            def _():
                for ref in a_refs:
                    ref[...] = jnp.zeros_like(ref)
            for ref, val in zip(a_refs, res[n_o:]):
                ref[...] += val

    in_specs = [pl.BlockSpec((tm, w), functools.partial(lambda cb, i: (i, cb), cb)) for _, w, cb in rows]
    in_specs += [pl.BlockSpec(v.shape, lambda i: (0, 0)) for v in vecs]
    out_specs = [pl.BlockSpec((tm, w), lambda i: (i, 0)) for w, _ in outs]
    out_specs += [pl.BlockSpec(a, lambda i: (0, 0)) for a in accs]
    out_shape = [jax.ShapeDtypeStruct((S, w), dt) for w, dt in outs]
    out_shape += [jax.ShapeDtypeStruct(a, F32) for a in accs]
    res = pl.pallas_call(
        body, name=name, out_shape=out_shape, grid=(S // tm,),
        in_specs=in_specs, out_specs=out_specs,
        compiler_params=_params(("arbitrary",)),
    )(*[r[0] for r in rows], *vecs)
    return res


def _rms(x):
    return x * lax.rsqrt(jnp.mean(x * x, axis=-1, keepdims=True) + EPS)


def _modulate(x, gain, scale, shift):
    return _rms(x) * gain * (1.0 + scale) + shift


def _ada_slices(ada_raw, bias):
    ada = ada_raw + bias
    return [ada[:, i * D_MODEL:(i + 1) * D_MODEL] for i in range(ada.shape[1] // D_MODEL)]


def _norm_wide_heads(y, gain, heads):
    w = y.shape[1] // heads
    return jnp.concatenate([_rms(y[:, h * w:(h + 1) * w]) * gain[:, h * w:(h + 1) * w] for h in range(heads)], axis=1)


def _norm_fox_heads(x, gain):
    outs = []
    for p in range(x.shape[1] // LANES):
        blk = x[:, p * LANES:(p + 1) * LANES]
        low = lax.broadcasted_iota(jnp.int32, blk.shape, 1) < FOX_DH
        sq = blk * blk
        ss_low = jnp.sum(jnp.where(low, sq, 0.0), axis=1, keepdims=True)
        ss_high = jnp.sum(jnp.where(low, 0.0, sq), axis=1, keepdims=True)
        outs.append(blk * lax.rsqrt(jnp.where(low, ss_low, ss_high) * (1.0 / FOX_DH) + EPS))
    return jnp.concatenate(outs, axis=1) * gain


def _silu(x):
    return x * jax.nn.sigmoid(x)


def _log_sigmoid(z):
    return -(jnp.maximum(-z, 0.0) + jnp.log(1.0 + jnp.exp(-jnp.abs(z))))


def _rotate(x, cos, sin, heads, sign):
    w = x.shape[1] // heads
    half = w // 2
    outs = []
    for h in range(heads):
        x1 = x[:, h * w:h * w + half]
        x2 = x[:, h * w + half:(h + 1) * w]
        outs += [x1 * cos - sign * x2 * sin, sign * x1 * sin + x2 * cos]
    return jnp.concatenate(outs, axis=1)


def _vjp(fn, primals, ct):
    return jax.vjp(fn, *primals)[1](ct)


_LOG_GAMMAS = [float(np.log(np.float32(1.0) - np.float32(2.0) ** np.float32(-5.0 - h))) for h in range(RET_HEADS)]


RET_ROWS = 1024


def retention(name, q, k, v, reverse, out_dtype=F32):
    (qa, dk, qo), (ka, _, ko), (va, dv, vo) = q, k, v
    S = qa.shape[0]
    C = RET_CHUNK
    rows = _row_tile(S, RET_ROWS)
    nb = S // rows

    def body(q_ref, k_ref, v_ref, o_ref, state):
        h = pl.program_id(0)

        @pl.when(pl.program_id(1) == 0)
        def _():
            state[...] = jnp.zeros_like(state)

        log_g = jnp.float32(_LOG_GAMMAS[RET_HEADS - 1])
        for i in range(RET_HEADS - 2, -1, -1):
            log_g = jnp.where(h == i, jnp.float32(_LOG_GAMMAS[i]), log_g)
        row = lax.broadcasted_iota(jnp.int32, (C, C), 0)
        col = lax.broadcasted_iota(jnp.int32, (C, C), 1)
        rel = (col - row if reverse else row - col).astype(F32)
        decay = jnp.where(rel >= 0, jnp.exp(log_g * jnp.maximum(rel, 0.0)), 0.0)
        j = lax.broadcasted_iota(jnp.int32, (C, 1), 0).astype(F32)
        q_decay = jnp.exp(log_g * (C - j if reverse else j + 1.0))
        k_decay = jnp.exp(log_g * (j if reverse else C - 1.0 - j))
        chunk_decay = jnp.exp(jnp.full((1, 1), log_g * C, F32))

        chunks = range(rows // C)
        for ci in (reversed(chunks) if reverse else chunks):
            rs = slice(ci * C, (ci + 1) * C)
            qc = q_ref[rs, :].astype(BF16)
            kf = k_ref[rs, :].astype(F32)
            vc = v_ref[rs, :].astype(BF16)
            scores = lax.dot_general(qc, kf.astype(BF16), (((1,), (1,)), ((), ())), preferred_element_type=F32) * decay
            intra = jnp.dot(scores.astype(BF16), vc, preferred_element_type=F32)
            cross = jnp.dot(qc, state[...].astype(BF16), preferred_element_type=F32) * q_decay
            o_ref[rs, :] = (intra + cross).astype(o_ref.dtype)
            upd = lax.dot_general((kf * k_decay).astype(BF16), vc, (((0,), (0,)), ((), ())), preferred_element_type=F32)
            state[...] = state[...] * chunk_decay + upd

    def block(i):
        return nb - 1 - i if reverse else i

    return pl.pallas_call(
        body, name=name, out_shape=jax.ShapeDtypeStruct((S, RET_HEADS * dv), out_dtype),
        grid=(RET_HEADS, nb),
        in_specs=[pl.BlockSpec((rows, dk), lambda h, i: (block(i), qo + h)),
                  pl.BlockSpec((rows, dk), lambda h, i: (block(i), ko + h)),
                  pl.BlockSpec((rows, dv), lambda h, i: (block(i), vo + h))],
        out_specs=pl.BlockSpec((rows, dv), lambda h, i: (block(i), h)),
        scratch_shapes=[pltpu.VMEM((dk, dv), F32)],
        compiler_params=_params(("parallel", "arbitrary")),
    )(qa, ka, va)


FOX_T = 256
N_PAIR = FOX_HEADS // 2
FOX_SCALE = FOX_DH ** -0.5


def _fox_heads(q2):
    low = lax.broadcasted_iota(jnp.int32, (1, LANES), 1) < FOX_DH
    return [(mask, jnp.where(mask, q2 * FOX_SCALE, 0.0).astype(BF16)) for mask in (low, jnp.logical_not(low))]


def _fox_parts(j, t):
    return ([(0, j * t, False)] if j else []) + [(j * t, (j + 1) * t, True)]


def _fox_scores(qa, k_ref, ft_ref, head, lo, hi, diagonal):
    k_blk = k_ref[lo:hi, :].astype(BF16)
    s = lax.dot_general(qa, k_blk, (((1,), (1,)), ((), ())), preferred_element_type=F32) - ft_ref[pl.ds(head, 1), lo:hi]
    if diagonal:
        n = hi - lo
        s = jnp.where(lax.broadcasted_iota(jnp.int32, (n, n), 1) <= lax.broadcasted_iota(jnp.int32, (n, n), 0), s, -jnp.inf)
    return s


def fox_forward(name, qn, kn, kvf, f_cum_t):
    S = qn.shape[0]
    t = _row_tile(S, FOX_T)
    v_block0 = D_MODEL // LANES

    def variant(j, pair, q_ref, k_ref, v_ref, ft_ref, y_ref, lse_ref):
        ys, lses = [], []
        for a, (mask, qa) in enumerate(_fox_heads(q_ref[...])):
            parts = [(lo, hi, _fox_scores(qa, k_ref, ft_ref, 2 * pair + a, lo, hi, dg)) for lo, hi, dg in _fox_parts(j, t)]
            m = functools.reduce(jnp.maximum, [jnp.max(s, axis=1, keepdims=True) for _, _, s in parts])
            l, acc = 0.0, 0.0
            for lo, hi, s in parts:
                e = jnp.exp(s - m)
                l = l + jnp.sum(e, axis=1, keepdims=True)
                acc = acc + jnp.dot(e.astype(BF16), v_ref[lo:hi, :].astype(BF16), preferred_element_type=F32)
            ys.append(acc / l)
            lses.append(m + jnp.log(l))
        low = lax.broadcasted_iota(jnp.int32, (1, LANES), 1) < FOX_DH
        y_ref[...] = jnp.where(low, ys[0], ys[1])
        lse_ref[...] = jnp.where(low, lses[0], lses[1])

    def body(*refs):
        pair, i = pl.program_id(0), pl.program_id(1)
        for j in range(S // t):
            pl.when(i == j)(functools.partial(variant, j, pair, *refs))

    return pl.pallas_call(
        body, name=name,
        out_shape=[jax.ShapeDtypeStruct((S, D_MODEL), F32), jax.ShapeDtypeStruct((S, D_MODEL), F32)],
        grid=(N_PAIR, S // t),
        in_specs=[pl.BlockSpec((t, LANES), lambda p, i: (i, p)),
                  pl.BlockSpec((S, LANES), lambda p, i: (0, p)),
                  pl.BlockSpec((S, LANES), lambda p, i: (0, v_block0 + p)),
                  pl.BlockSpec((LANES, S), lambda p, i: (0, 0))],
        out_specs=[pl.BlockSpec((t, LANES), lambda p, i: (i, p)),
                   pl.BlockSpec((t, LANES), lambda p, i: (i, p))],
        compiler_params=_params(("parallel", "arbitrary")),
    )(qn, kn, kvf, f_cum_t)


def fox_backward(name, qn, kn, kvf, f_cum_t, y, dy, lse):
    S = qn.shape[0]
    t = _row_tile(S, FOX_T)
    v_block0 = D_MODEL // LANES

    def variant(j, pair, q_ref, k_ref, v_ref, ft_ref, y_ref, dy_ref, lse_ref, dq_ref, dk_ref, dv_ref, dfq_ref, dfk_ref,
                dv_acc):
        y2, dy2, lse2 = y_ref[...], dy_ref[...], lse_ref[...]
        lane = lax.broadcasted_iota(jnp.int32, (t, LANES), 1)
        dqs, dfq = [], jnp.zeros((t, LANES), F32)
        for a, (mask, qa) in enumerate(_fox_heads(q_ref[...].astype(F32))):
            lse_a = jnp.max(jnp.where(mask, lse2, -jnp.inf), axis=1, keepdims=True)
            dy_a = jnp.where(mask, dy2, 0.0)
            delta = jnp.sum(dy_a * y2, axis=1, keepdims=True)
            dy_b = dy_a.astype(BF16)
            dq, row_sum = 0.0, 0.0
            for lo, hi, dg in _fox_parts(j, t):
                p = jnp.exp(_fox_scores(qa, k_ref, ft_ref, 2 * pair + a, lo, hi, dg) - lse_a)
                dp = lax.dot_general(dy_b, v_ref[lo:hi, :].astype(BF16), (((1,), (1,)), ((), ())), preferred_element_type=F32)
                ds = p * (dp - delta)
                row_sum = row_sum + jnp.sum(ds, axis=1, keepdims=True)
                dfk_ref[pl.ds(a, 1), lo:hi] += -jnp.sum(ds, axis=0, keepdims=True)
                ds_b = ds.astype(BF16)
                dq = dq + jnp.dot(ds_b, k_ref[lo:hi, :].astype(BF16), preferred_element_type=F32)
                dk_ref[lo:hi, :] += lax.dot_general(ds_b, qa, (((0,), (0,)), ((), ())), preferred_element_type=F32)
                dv_acc[lo:hi, :] += lax.dot_general(p.astype(BF16), dy_b, (((0,), (0,)), ((), ())), preferred_element_type=F32)
            dqs.append(dq * FOX_SCALE)
            dfq = dfq + jnp.where(lane == 2 * pair + a, row_sum, 0.0)
        low = lax.broadcasted_iota(jnp.int32, (1, LANES), 1) < FOX_DH
        dq_ref[...] = jnp.where(low, dqs[0], dqs[1])
        dfq_ref[...] = dfq

    def body(*refs):
        pair, i = pl.program_id(0), pl.program_id(1)
        dk_ref, dv_ref, _, dfk_ref, dv_acc = refs[8:13]

        @pl.when(i == 0)
        def _():
            dk_ref[...] = jnp.zeros_like(dk_ref)
            dv_acc[...] = jnp.zeros_like(dv_acc)
            dfk_ref[...] = jnp.zeros_like(dfk_ref)

        for j in range(S // t):
            pl.when(i == j)(functools.partial(variant, j, pair, *refs))

        @pl.when(i == S // t - 1)
        def _():
            dv_ref[...] = dv_acc[...].astype(dv_ref.dtype)

    row_blk = pl.BlockSpec((t, LANES), lambda p, i: (i, p))
    col_blk = pl.BlockSpec((S, LANES), lambda p, i: (0, p))
    return pl.pallas_call(
        body, name=name,
        out_shape=[jax.ShapeDtypeStruct((S, D_MODEL), F32)] * 2
        + [jax.ShapeDtypeStruct((S, D_MODEL), BF16), jax.ShapeDtypeStruct((N_PAIR, S, LANES), F32),
           jax.ShapeDtypeStruct((N_PAIR, 8, S), F32)],
        grid=(N_PAIR, S // t),
        in_specs=[row_blk, col_blk,
                  pl.BlockSpec((S, LANES), lambda p, i: (0, v_block0 + p)),
                  pl.BlockSpec((LANES, S), lambda p, i: (0, 0)),
                  row_blk, row_blk, row_blk],
        out_specs=[row_blk, col_blk, col_blk,
                   pl.BlockSpec((None, t, LANES), lambda p, i: (p, i, 0)),
                   pl.BlockSpec((None, 8, S), lambda p, i: (p, 0, 0))],
        scratch_shapes=[pltpu.VMEM((S, LANES), F32)],
        compiler_params=_params(("parallel", "arbitrary")),
    )(qn, kn, kvf, f_cum_t, y, dy, lse)


def cumsum_rows(name, x, reverse):
    S = x.shape[0]
    C = LANES
    nc = S // C

    def body(x_ref, o_ref):
        row = lax.broadcasted_iota(jnp.int32, (C, C), 0)
        col = lax.broadcasted_iota(jnp.int32, (C, C), 1)
        tri = jnp.where(col >= row if reverse else col <= row, 1.0, 0.0).astype(F32)
        carry = jnp.zeros((1, LANES), F32)
        for i in (range(nc - 1, -1, -1) if reverse else range(nc)):
            blk = x_ref[i * C:(i + 1) * C, :]
            loc = jnp.dot(tri, blk, preferred_element_type=F32, precision=lax.Precision.HIGHEST)
            o_ref[i * C:(i + 1) * C, :] = loc + carry
            carry = carry + (loc[0:1, :] if reverse else loc[C - 1:C, :])

    return pl.pallas_call(body, name=name, out_shape=jax.ShapeDtypeStruct((S, LANES), F32),
                          compiler_params=_params())(x)


def adamw(name, parts, w, m, v):
    L, R, C = w.shape
    tr = _row_tile(R, 256)
    nr = R // tr
    counts = [len(p) for p in parts]

    def body(*refs):
        w_ref, m_ref, v_ref, g_out, d_out, m_out, v_out = refs[sum(counts):]
        for layer in range(L):
            p_refs = refs[sum(counts[:layer]):sum(counts[:layer + 1])]

            @pl.when(pl.program_id(0) == layer)
            def _(p_refs=p_refs, slots=[n for _, n in parts[layer]]):
                g = None
                for p_ref, n in zip(p_refs, slots):
                    for i in range(n):
                        g = p_ref[i].astype(F32) if g is None else g + p_ref[i].astype(F32)
                m2 = ADAM_B1 * m_ref[...] + (1.0 - ADAM_B1) * g
                v2 = ADAM_B2 * v_ref[...] + (1.0 - ADAM_B2) * jnp.square(g)
                m_hat = m2 / (1.0 - ADAM_B1 ** ADAM_STEP)
                v_hat = v2 / (1.0 - ADAM_B2 ** ADAM_STEP)
                g_out[...] = g
                d_out[...] = -ADAM_LR * (m_hat / (jnp.sqrt(v_hat) + ADAM_EPS) + ADAM_WD * w_ref[...])
                m_out[...] = m2
                v_out[...] = v2

    def part_spec(layer, n):
        return pl.BlockSpec((n, tr, C), lambda l, i: (0, jnp.where(l == layer, i, jnp.where(l < layer, 0, nr - 1)), 0))

    blk = pl.BlockSpec((None, tr, C), lambda l, i: (l, i, 0))
    return pl.pallas_call(
        body, name=name, out_shape=[jax.ShapeDtypeStruct((L, R, C), F32)] * 4, grid=(L, nr),
        in_specs=[part_spec(layer, n) for layer in range(L) for _, n in parts[layer]] + [blk, blk, blk],
        out_specs=[blk] * 4, compiler_params=_params(("arbitrary", "arbitrary")),
    )(*[a for layer in parts for a, _ in layer], w, m, v)


def kernel(x, c, positions, norm_mix_gain, norm_mlp_gain, w_ada, b_ada, w_mlp_in, w_mlp_out, ret_w_in, ret_norm_gain, ret_w_out, kv_norm_gain, kv_w_ada, kv_b_ada, kv_w, forget_bias, k_norm_gain, fox_w_in, q_norm_gain, fox_w_out, loss_target, m_norm_mix_gain, m_norm_mlp_gain, m_w_ada, m_b_ada, m_w_mlp_in, m_w_mlp_out, m_ret_w_in, m_ret_norm_gain, m_ret_w_out, m_kv_norm_gain, m_kv_w_ada, m_kv_b_ada, m_kv_w, m_forget_bias, m_k_norm_gain, m_fox_w_in, m_q_norm_gain, m_fox_w_out, v_norm_mix_gain, v_norm_mlp_gain, v_w_ada, v_b_ada, v_w_mlp_in, v_w_mlp_out, v_ret_w_in, v_ret_norm_gain, v_ret_w_out, v_kv_norm_gain, v_kv_w_ada, v_kv_b_ada, v_kv_w, v_forget_bias, v_k_norm_gain, v_fox_w_in, v_q_norm_gain, v_fox_w_out):
    D = D_MODEL
    S = x.shape[1]
    x0 = x.reshape(S, D)
    target = loss_target.reshape(S, D)
    me = 4 * lax.axis_index("x") + 2 * lax.axis_index("y") + lax.axis_index("c")
    n_ada = w_ada.shape[2]
    n_kvada = kv_w_ada.shape[1]
    n_kv = kv_w.shape[1]

    c_all, ret_gain = all_gather("gather_c", [c.reshape(D // LANES, LANES), ret_norm_gain.reshape(RET_HEADS, -1)])
    ret_gain = jnp.transpose(ret_gain, (1, 0, 2)).reshape(1, RET_HEADS * RET_V)
    c_act = rowwise("silu_c", _silu, [c_all.reshape(N_DEV, D)], [], [(D, F32)])[0]
    w_ada_cat = jnp.concatenate([w_ada[0], w_ada[1], kv_w_ada], axis=1).astype(BF16)[None]
    n_cat = 2 * n_ada + n_kvada
    ada_part = mm_nn("ada_proj", c_act, w_ada_cat)[0]
    ada_mine = all_to_all("ada_rows", [ada_part.reshape(N_DEV, n_cat // LANES, LANES)])[0]
    ada_mine = ada_mine.reshape(N_DEV, n_cat)
    ada_raw = [ada_mine[:, l * n_ada:(l + 1) * n_ada].reshape(1, 6 * D) for l in range(2)]
    kvada_raw = ada_mine[:, 2 * n_ada:].reshape(1, 2 * D)
    kv_bias = kv_b_ada.reshape(1, 2 * D)
    kv_gain = kv_norm_gain.reshape(1, D)
    fb = jnp.pad(forget_bias.reshape(1, FOX_HEADS), ((0, 0), (0, LANES - FOX_HEADS)))
    k_gain = jnp.tile(k_norm_gain.reshape(1, FOX_DH), (1, FOX_HEADS))
    q_gain = jnp.tile(q_norm_gain.reshape(1, FOX_DH), (1, FOX_HEADS))

    w_names = ["ret_in", "ret_out", "mlp_in0", "mlp_out0", "kv", "fox_in", "fox_out", "mlp_in1", "mlp_out1"]
    shards = [ret_w_in[0].astype(BF16), ret_w_out[0].astype(BF16), w_mlp_in[0].astype(BF16), w_mlp_out[0].astype(BF16),
              kv_w.astype(BF16), fox_w_in[0].astype(BF16), fox_w_out[0].astype(BF16), w_mlp_in[1].astype(BF16),
              w_mlp_out[1].astype(BF16)]
    two_level = {"ret_in", "ret_out", "mlp_in0", "mlp_out0"}
    w_plans = {name: plan_gather if name in two_level else plan_gather_direct for name in w_names}
    w_handles, token = exchange_start("gather_weights_start", shards,
                                      [_landing(a, (N_DEV,) + a.shape, a.dtype) for a in shards],
                                      [w_plans[name] for name in w_names], after=(ada_mine, ret_gain))
    w_handles = dict(zip(w_names, w_handles))

    def weight(name, after):
        arrived = exchange_wait("gather_wait_" + name, [w_handles[name]], w_plans[name], after)[1]
        if name in two_level:
            arrived = exchange_now("gather_forward_" + name, None, arrived, plan_forward)
        return arrived[0]

    pos = positions.reshape(S, 1).astype(F32)
    half = RET_QK // 2
    inv_freq = jnp.asarray((ROPE_BASE ** (-np.arange(half, dtype=np.float32) / half)).reshape(1, half), F32)

    def angles(p, f):
        ang = p * f
        return jnp.cos(ang), jnp.sin(ang)

    cos, sin = rowwise("rope_table", angles, [pos], [inv_freq], [(half, F32), (half, F32)])

    def mod_mix(layer):
        def fn(xb, ada, bias, gain):
            sh, sc = _ada_slices(ada, bias)[:2]
            return _modulate(xb, gain[layer:layer + 1], sc, sh)
        return fn

    def mod_mlp(layer):
        def fn(xb, ada, bias, gain):
            sh, sc = _ada_slices(ada, bias)[3:5]
            return _modulate(xb, gain[layer:layer + 1], sc, sh)
        return fn

    h1_0 = rowwise("mod_mix0", mod_mix(0), [x0], [ada_raw[0], b_ada[0:1], norm_mix_gain], [(D, BF16)], after=token)[0]
    W_ret_in = weight("ret_in", h1_0)
    proj = mm_nn("ret_proj", h1_0, W_ret_in, (BF16,))[0]

    def rope_fwd(qb, kb, cs, sn):
        return (_rotate(qb.astype(F32), cs, sn, RET_HEADS, 1.0),
                _rotate(kb.astype(F32), cs, sn, RET_HEADS, 1.0) * (RET_QK ** -0.5))

    q_rot, k_rot = rowwise("rope", rope_fwd, [(proj, D, 0), (proj, D, 1), cos, sin], [], [(D, BF16), (D, BF16)])
    v_ret = (proj, RET_V, (2 * D) // RET_V)
    y_ret = retention("ret_fwd", (q_rot, RET_QK, 0), (k_rot, RET_QK, 0), v_ret, reverse=False)

    def ret_gate(yb, gb, gain):
        return _silu(gb.astype(F32)) * _norm_wide_heads(yb, gain, RET_HEADS)

    mixin0 = rowwise("ret_gate", ret_gate, [y_ret, (proj, 2 * D, 2)], [ret_gain], [(2 * D, BF16)])[0]
    W_ret_out = weight("ret_out", mixin0).reshape(1, 2 * D, D)
    mix0 = mm_nn("ret_out", mixin0, W_ret_out)[0]

    def residual_mod(layer, slot):
        def fn(xb, bb, ada, bias, gain):
            s = _ada_slices(ada, bias)
            xn = xb + s[2] * bb
            return xn, _modulate(xn, gain[layer:layer + 1], s[4], s[3])
        return fn

    x1, h2_0 = rowwise("res_mix0", residual_mod(0, 0), [x0, mix0], [ada_raw[0], b_ada[0:1], norm_mlp_gain],
                       [(D, F32), (D, BF16)])

    W_mlp_in, W_mlp_out = {}, {}

    def mlp_forward(tag, h2, layer):
        W_mlp_in[layer] = weight("mlp_in" + tag, h2)
        u, act = mm_nn("mlp_in" + tag, h2, W_mlp_in[layer], (BF16, BF16),
                       epilogue=lambda acc: (acc, jnp.square(jnp.maximum(acc, 0.0))))
        W_mlp_out[layer] = weight("mlp_out" + tag, act).reshape(1, 4 * D, D)
        return u, act, mm_nn("mlp_out" + tag, act, W_mlp_out[layer])[0]

    u0, act0, mlp0 = mlp_forward("0", h2_0, 0)

    def res_mlp0(xb, bb, ada0, bias0, ada1, bias1, kva, kvb, gain_mix, gain_kv):
        xn = xb + _ada_slices(ada0, bias0)[5] * bb
        s1 = _ada_slices(ada1, bias1)
        kv_shift, kv_scale = _ada_slices(kva, kvb)
        return xn, _modulate(xn, gain_kv, kv_scale, kv_shift), _modulate(xn, gain_mix[1:2], s1[1], s1[0])

    x2, h_kv, h1_1 = rowwise("res_mlp0", res_mlp0, [x1, mlp0],
                             [ada_raw[0], b_ada[0:1], ada_raw[1], b_ada[1:2], kvada_raw, kv_bias, norm_mix_gain, kv_gain],
                             [(D, F32), (D, BF16), (D, BF16)])

    kv_full = jnp.transpose(weight("kv", h_kv), (1, 0, 2)).reshape(D, N_DEV * n_kv)
    W_kv = jnp.pad(kv_full, ((0, 0), (0, KV_PAD - N_DEV * n_kv)))[None]
    kvf = mm_nn("kv_proj", h_kv, W_kv)[0]

    def kv_post(kb, fblk, kg, bias):
        head = lax.broadcasted_iota(jnp.int32, fblk.shape, 1) < FOX_HEADS
        return _norm_fox_heads(kb, kg), jnp.where(head, _log_sigmoid(fblk + bias), 0.0)

    kn, log_f = rowwise("kv_post", kv_post, [(kvf, D, 0), (kvf, LANES, 2 * D // LANES)], [k_gain, fb],
                        [(D, BF16), (LANES, F32)])
    f_cum = cumsum_rows("f_cumsum", log_f, reverse=False)
    f_cum_t = f_cum.T

    W_fox_in = weight("fox_in", kvf)
    qo = mm_nn("fox_proj", h1_1, W_fox_in)[0]
    qn = rowwise("q_norm", _norm_fox_heads, [(qo, D, 0)], [q_gain], [(D, BF16)])[0]
    y_att, lse = fox_forward("fox_fwd", qn, kn, kvf, f_cum_t)
    mixin1 = rowwise("fox_gate", lambda ob, yb: jax.nn.sigmoid(ob) * yb, [(qo, D, 1), y_att], [], [(D, BF16)])[0]
    W_fox_out = weight("fox_out", mixin1).reshape(1, D, D)
    mix1 = mm_nn("fox_out", mixin1, W_fox_out)[0]
    x3, h2_1 = rowwise("res_mix1", residual_mod(1, 0), [x2, mix1], [ada_raw[1], b_ada[1:2], norm_mlp_gain],
                       [(D, F32), (D, BF16)])
    u1, act1, mlp1 = mlp_forward("1", h2_1, 1)

    def scatter_start(tag, gws, after=()):
        lands = [lax.empty((4,) + g.shape[1:], g.dtype) for g in gws]
        return exchange_start("scatter_sibling_start_" + tag, gws, lands, plan_to_sibling, after)

    def scatter_relay(tag, handles, after, start_after=()):
        gws, from_sibling = exchange_wait("scatter_sibling_wait_" + tag, handles, plan_to_sibling, after)
        sums = [chip_sum("chip_sum_%s%d" % (tag, i), g, s) for i, (g, s) in enumerate(zip(gws, from_sibling))]
        lands = [lax.empty((3,) + s.shape[1:], s.dtype) for s in sums]
        return exchange_start("scatter_owner_start_" + tag, sums, lands, plan_to_owners, start_after)

    def scatter_direct_start(tag, gws):
        me = _my_index()
        lands = [_landing(lax.dynamic_index_in_dim(g, me, 0, keepdims=False), g.shape, g.dtype) for g in gws]
        return exchange_start("scatter_direct_start_" + tag, gws, lands, plan_scatter_direct)

    def scatter_direct_finish(tag, handles, after):
        return [[(r, N_DEV)] for r in exchange_wait("scatter_direct_wait_" + tag, handles, plan_scatter_direct, after)[1]]

    def scatter_finish(tag, handles, after):
        sums, received = exchange_wait("scatter_owner_wait_" + tag, handles, plan_to_owners, after)
        return [[(s, 1), (r, 3)] for s, r in zip(sums, received)]

    def loss_head(xb, bb, tb, ada, bias):
        g2 = _ada_slices(ada, bias)[5]
        err = xb + g2 * bb - tb
        dx = err * (1.0 / D)
        loss = 0.5 * jnp.sum(jnp.sum(err * err, axis=1, keepdims=True) * (1.0 / D), axis=0, keepdims=True)
        return dx, (dx * g2), jnp.broadcast_to(loss, (1, LANES)), jnp.sum(dx * bb, axis=0, keepdims=True)

    dx4, dmlp1, loss_acc, dg2_1 = rowwise("loss_head", loss_head, [x3, mlp1, target], [ada_raw[1], b_ada[1:2]],
                                          [(D, F32), (D, BF16)], [(1, LANES), (1, D)])

    def mlp_backward(tag, dmlp, act, u, h2, layer, after=None):
        du = mm_nt("mlp_out_dx" + tag, dmlp, W_mlp_out[layer], (BF16,), extra=(u,), after=after,
                   epilogue=lambda acc, ub: (acc * (2.0 * jnp.maximum(ub.astype(F32), 0.0)),))[0]
        gw_out = mm_tn("mlp_out_dw" + tag, act, dmlp, 1, BF16).reshape(N_DEV, -1, D)
        dh2 = mm_nt("mlp_in_dx" + tag, du, W_mlp_in[layer], (BF16,))[0]
        gw_in = mm_tn("mlp_in_dw" + tag, h2, du, N_DEV, BF16)
        return dh2, gw_in, gw_out

    def mod_backward(layer, slots, gate_slot):
        def fn(xb, dhb, dresb, branchb, ada, bias, gain):
            s = _ada_slices(ada, bias)
            g = gain[layer:layer + 1]
            dx, dgain, dsc, dsh = _vjp(_modulate, (xb, g, s[slots[1]], s[slots[0]]), dhb.astype(F32))
            dx = dx + dresb
            d_branch = dx * s[gate_slot]
            return dx, d_branch, dgain, dsc, dsh, jnp.sum(dx * branchb, axis=0, keepdims=True)
        return fn

    vec = (1, D)
    dh2_1, gw_mlp_in1, gw_mlp_out1 = mlp_backward("1", dmlp1, act1, u1, h2_1, 1)
    scat_a1, token_a1 = scatter_direct_start("a1", [gw_mlp_in1, gw_mlp_out1])
    dx3, dmix1, dgain_mlp1, dsc2_1, dsh2_1, dg1_1 = rowwise(
        "mod_mlp1_bwd", mod_backward(1, (3, 4), 2), [x3, dh2_1, dx4, mix1], [ada_raw[1], b_ada[1:2], norm_mlp_gain],
        [(D, F32), (D, BF16)], [vec] * 4, after=token_a1)
    dmixin1 = mm_nt("fox_out_dx", dmix1, W_fox_out)[0]
    gw_fox_out = mm_tn("fox_out_dw", mixin1, dmix1, 1, BF16).reshape(N_DEV, -1, D)

    def fox_gate_bwd(db, ob, yb):
        sg = jax.nn.sigmoid(ob)
        return db * sg, db * yb * sg * (1.0 - sg)

    dy_att, d_og = rowwise("fox_gate_bwd", fox_gate_bwd, [dmixin1, (qo, D, 1), y_att], [], [(D, F32), (D, F32)])
    dqn, dkn, dv_att, dfq, dfk = fox_backward("fox_bwd", qn, kn, kvf, f_cum_t, y_att, dy_att, lse)

    def q_norm_bwd(qb, db, ogb, gain):
        dq, dgain = _vjp(_norm_fox_heads, (qb, gain), db)
        return jnp.concatenate([dq, ogb], axis=1), dgain

    dqo, dq_gain = rowwise("q_norm_bwd", q_norm_bwd, [(qo, D, 0), dqn, d_og], [q_gain], [(2 * D, BF16)], [vec])
    dh1_1 = mm_nt("fox_proj_dx", dqo, W_fox_in, (BF16,))[0]
    gw_fox_in = mm_tn("fox_proj_dw", h1_1, dqo, N_DEV, BF16)

    dfk_rows = jnp.pad(dfk[:, :2, :].reshape(FOX_HEADS, S).T, ((0, 0), (0, LANES - FOX_HEADS)))

    def df_total(*blks):
        tot = blks[0]
        for b in blks[1:]:
            tot = tot + b
        return tot

    d_fcum = rowwise("df_sum", df_total, [dfk_rows] + [dfq[p] for p in range(N_PAIR)], [], [(LANES, F32)])[0]
    d_logf = cumsum_rows("df_cumsum", d_fcum, reverse=True)

    def kv_post_bwd(kb, fblk, dkb, dvb, dlf, kg, bias):
        dk, dgain = _vjp(_norm_fox_heads, (kb, kg), dkb)
        df = dlf * (1.0 / (1.0 + jnp.exp(fblk + bias)))
        return jnp.concatenate([dk, dvb.astype(F32), df], axis=1), dgain, jnp.sum(df, axis=0, keepdims=True)

    dkvf, dk_gain, dfb = rowwise("kv_post_bwd", kv_post_bwd,
                                 [(kvf, D, 0), (kvf, LANES, 2 * D // LANES), dkn, dv_att, d_logf], [k_gain, fb],
                                 [(KV_PAD, BF16)], [vec, (1, LANES)])
    dh_kv = mm_nt("kv_proj_dx", dkvf, W_kv, (BF16,))[0]
    gw_kv = mm_tn("kv_proj_dw", h_kv, dkvf, 1, BF16)[0, :, :N_DEV * n_kv]
    gw_kv = jnp.transpose(gw_kv.reshape(D, N_DEV, n_kv), (1, 0, 2))
    scat_a2, token_a = scatter_direct_start("a2", [gw_fox_out, gw_fox_in, gw_kv])

    def x2_bwd(xb, dh1b, dhkb, dresb, branchb, ada0, bias0, ada1, bias1, kva, kvb, gain_mix, gain_kv):
        s1 = _ada_slices(ada1, bias1)
        kv_shift, kv_scale = _ada_slices(kva, kvb)
        dxa, dgain_mix, dsc1, dsh1 = _vjp(_modulate, (xb, gain_mix[1:2], s1[1], s1[0]), dh1b.astype(F32))
        dxb, dgain_kv, dkv_scale, dkv_shift = _vjp(_modulate, (xb, gain_kv, kv_scale, kv_shift), dhkb.astype(F32))
        dx = dresb + dxa + dxb
        g2 = _ada_slices(ada0, bias0)[5]
        return (dx, dx * g2, dgain_mix, dsc1, dsh1, dgain_kv, dkv_scale, dkv_shift,
                jnp.sum(dx * branchb, axis=0, keepdims=True))

    (dx2, dmlp0, dgain_mix1, dsc1_1, dsh1_1, dgain_kv, dkv_scale, dkv_shift, dg2_0) = rowwise(
        "x2_bwd", x2_bwd, [x2, dh1_1, dh_kv, dx3, mlp0],
        [ada_raw[0], b_ada[0:1], ada_raw[1], b_ada[1:2], kvada_raw, kv_bias, norm_mix_gain, kv_gain],
        [(D, F32), (D, BF16)], [vec] * 7, after=token_a)

    dh2_0, gw_mlp_in0, gw_mlp_out0 = mlp_backward("0", dmlp0, act0, u0, h2_0, 0)
    dx1, dmix0, dgain_mlp0, dsc2_0, dsh2_0, dg1_0 = rowwise(
        "mod_mlp0_bwd", mod_backward(0, (3, 4), 2), [x1, dh2_0, dx2, mix0], [ada_raw[0], b_ada[0:1], norm_mlp_gain],
        [(D, F32), (D, BF16)], [vec] * 4)
    dmixin0 = mm_nt("ret_out_dx", dmix0, W_ret_out, (BF16,))[0]
    gw_ret_out = mm_tn("ret_out_dw", mixin0, dmix0, 1, BF16).reshape(N_DEV, -1, D)
    scat_b, token_b = scatter_start("b", [gw_mlp_in0, gw_mlp_out0, gw_ret_out])

    def ret_gate_bwd(db, yb, gb, gain):
        return _vjp(ret_gate, (yb, gb.astype(F32), gain), db.astype(F32))

    dy_ret, dgate, dret_gain = rowwise("ret_gate_bwd", ret_gate_bwd, [dmixin0, y_ret, (proj, 2 * D, 2)], [ret_gain],
                                       [(2 * D, BF16), (2 * D, BF16)], [(1, 2 * D)], after=token_b)
    scat_b, token_b = scatter_relay("b", scat_b, dy_ret)
    dy_h = (dy_ret, RET_V, 0)
    dq_rot = retention("ret_dq", dy_h, v_ret, (k_rot, RET_QK, 0), reverse=False)
    dk_rot = retention("ret_dk", v_ret, dy_h, (q_rot, RET_QK, 0), reverse=True)
    dv_ret = retention("ret_dv", (k_rot, RET_QK, 0), (q_rot, RET_QK, 0), dy_h, reverse=True, out_dtype=BF16)

    def rope_bwd(dqb, dkb, dvb, dgb, cs, sn):
        dq = _rotate(dqb, cs, sn, RET_HEADS, -1.0)
        dk = _rotate(dkb, cs, sn, RET_HEADS, -1.0) * (RET_QK ** -0.5)
        return jnp.concatenate([dq, dk, dvb.astype(F32), dgb.astype(F32)], axis=1)

    dproj = rowwise("rope_bwd", rope_bwd, [dq_rot, dk_rot, dv_ret, dgate, cos, sin], [], [(6 * D, BF16)], after=token_b)[0]
    gw_ret_in = mm_tn("ret_proj_dw", h1_0, dproj, N_DEV, BF16)
    scat_c, token_c = scatter_start("c", [gw_ret_in])
    dh1_0 = mm_nt("ret_proj_dx", dproj, W_ret_in, (BF16,), after=token_c)[0]

    def x0_bwd(xb, dhb, dresb, ada, bias, gain):
        s = _ada_slices(ada, bias)
        dx, dgain, dsc, dsh = _vjp(_modulate, (xb, gain[0:1], s[1], s[0]), dhb.astype(F32))
        return dx + dresb, dgain, dsc, dsh

    grad_x, dgain_mix0, dsc1_0, dsh1_0 = rowwise("x0_bwd", x0_bwd, [x0, dh1_0, dx1],
                                                 [ada_raw[0], b_ada[0:1], norm_mix_gain], [(D, F32)], [vec] * 3)

    small = jnp.concatenate([
        dsh1_0, dsc1_0, dg1_0, dsh2_0, dsc2_0, dg2_0,
        dsh1_1, dsc1_1, dg1_1, dsh2_1, dsc2_1, dg2_1,
        dkv_shift, dkv_scale,
        dgain_mix0, dgain_mix1, dgain_mlp0, dgain_mlp1, dgain_kv,
        dret_gain,
        dq_gain.reshape(FOX_HEADS, FOX_DH).sum(axis=0).reshape(1, FOX_DH),
        dk_gain.reshape(FOX_HEADS, FOX_DH).sum(axis=0).reshape(1, FOX_DH),
        dfb,
        loss_acc,
    ], axis=1)
    small_all = all_gather("gather_small", [small.reshape(-1, LANES)])[0].reshape(N_DEV, 1, -1)
    loss = jnp.sum(small_all[:, 0, -1])
    scat_c, token_c = scatter_relay("c", scat_c, grad_x, start_after=(small_all,))
    o_ada = 14 * D
    d_ada = small_all[:, 0, :o_ada]
    d_cat = jnp.concatenate([
        lax.dynamic_slice_in_dim(d_ada[:, 0:6 * D], me * n_ada, n_ada, axis=1),
        lax.dynamic_slice_in_dim(d_ada[:, 6 * D:12 * D], me * n_ada, n_ada, axis=1),
        lax.dynamic_slice_in_dim(d_ada[:, 12 * D:14 * D], me * n_kvada, n_kvada, axis=1)], axis=1)
    gw_ada_cat = mm_tn("ada_dw", c_act, d_cat, 1, F32, after=token_c)[0]

    results = {}

    def update(name, parts, w, m, v, layers=1):
        per_layer = parts if layers > 1 else [parts]
        shape = w.shape
        C = shape[-1]
        R = int(np.prod(shape)) // (layers * C)
        per_layer = [p if isinstance(p, list) else [(p, p.shape[0])] for p in per_layer]
        per_layer = [[(a.reshape(a.shape[0], R, C), n) for a, n in p] for p in per_layer]
        outs = adamw("adamw_" + name, per_layer, w.reshape(layers, R, C), m.reshape(layers, R, C), v.reshape(layers, R, C))
        results[name] = tuple(t.reshape(shape) for t in outs)

    def small_parts(lo, width):
        return small_all[:, :, lo:lo + width]

    update("norm_mix_gain", jnp.concatenate([small_parts(o_ada, D), small_parts(o_ada + D, D)], axis=1),
           norm_mix_gain, m_norm_mix_gain, v_norm_mix_gain)
    update("norm_mlp_gain", jnp.concatenate([small_parts(o_ada + 2 * D, D), small_parts(o_ada + 3 * D, D)], axis=1),
           norm_mlp_gain, m_norm_mlp_gain, v_norm_mlp_gain)
    update("w_ada", [gw_ada_cat[None, :, :n_ada], gw_ada_cat[None, :, n_ada:2 * n_ada]], w_ada, m_w_ada, v_w_ada, layers=2)
    update("b_ada", jnp.concatenate([small_parts(0, 6 * D), small_parts(6 * D, 6 * D)], axis=1), b_ada, m_b_ada, v_b_ada)
    o_ret = o_ada + 5 * D
    n_rg = ret_norm_gain.shape[2]
    ret_gain_parts = small_parts(o_ret, 2 * D).reshape(N_DEV, RET_HEADS, RET_V)
    ret_gain_parts = lax.dynamic_slice_in_dim(ret_gain_parts, me * n_rg, n_rg, axis=2)
    update("ret_norm_gain", ret_gain_parts, ret_norm_gain, m_ret_norm_gain, v_ret_norm_gain)
    update("kv_norm_gain", small_parts(o_ada + 4 * D, D), kv_norm_gain, m_kv_norm_gain, v_kv_norm_gain)
    update("kv_w_ada", gw_ada_cat[None, :, 2 * n_ada:], kv_w_ada, m_kv_w_ada, v_kv_w_ada)
    update("kv_b_ada", small_parts(12 * D, 2 * D), kv_b_ada, m_kv_b_ada, v_kv_b_ada)
    o_q = o_ret + 2 * D
    update("forget_bias", small_parts(o_q + 2 * FOX_DH, FOX_HEADS), forget_bias, m_forget_bias, v_forget_bias)
    update("k_norm_gain", small_parts(o_q + FOX_DH, FOX_DH), k_norm_gain, m_k_norm_gain, v_k_norm_gain)
    update("q_norm_gain", small_parts(o_q, FOX_DH), q_norm_gain, m_q_norm_gain, v_q_norm_gain)

    r_mlp_in1, r_mlp_out1 = scatter_direct_finish("a1", scat_a1, results["q_norm_gain"][1])
    r_fox_out, r_fox_in, r_kv = scatter_direct_finish("a2", scat_a2, r_mlp_in1[0][0])
    r_mlp_in0, r_mlp_out0, r_ret_out = scatter_finish("b", scat_b, r_kv[0][0])
    update("kv_w", r_kv, kv_w, m_kv_w, v_kv_w)
    update("fox_w_in", r_fox_in, fox_w_in, m_fox_w_in, v_fox_w_in)
    update("fox_w_out", r_fox_out, fox_w_out, m_fox_w_out, v_fox_w_out)
    update("ret_w_out", r_ret_out, ret_w_out, m_ret_w_out, v_ret_w_out)
    update("w_mlp_in", [r_mlp_in0, r_mlp_in1], w_mlp_in, m_w_mlp_in, v_w_mlp_in, layers=2)
    update("w_mlp_out", [r_mlp_out0, r_mlp_out1], w_mlp_out, m_w_mlp_out, v_w_mlp_out, layers=2)
    r_ret_in = scatter_finish("c", scat_c, results["w_mlp_out"][1])[0]
    update("ret_w_in", r_ret_in, ret_w_in, m_ret_w_in, v_ret_w_in)

    order = ["norm_mix_gain", "norm_mlp_gain", "w_ada", "b_ada", "w_mlp_in", "w_mlp_out", "ret_w_in", "ret_norm_gain",
             "ret_w_out", "kv_norm_gain", "kv_w_ada", "kv_b_ada", "kv_w", "forget_bias", "k_norm_gain", "fox_w_in",
             "q_norm_gain", "fox_w_out"]
    out = [loss, grad_x.reshape(x.shape)]
    for slot in range(4):
        out += [results[n][slot] for n in order]
    return tuple(out)
```

```python
import functools
import math

import numpy as np
import jax
import jax.numpy as jnp
from jax import lax
from jax.experimental import pallas as pl
from jax.experimental.pallas import tpu as pltpu

F32 = jnp.float32
BF16 = jnp.bfloat16

N_DEV = 8
D_MODEL = 1024
RET_HEADS = 4
RET_QK = D_MODEL // RET_HEADS
RET_V = 2 * D_MODEL // RET_HEADS
RET_CHUNK = 128
ROPE_BASE = 10000.0
FOX_HEADS = 16
FOX_DH = D_MODEL // FOX_HEADS
EPS = 1e-6
LANES = 128
KV_PAD = 2 * D_MODEL + LANES

ADAM_LR = 0.001
ADAM_B1 = 0.9
ADAM_B2 = 0.999
ADAM_EPS = 1e-08
ADAM_WD = 0.01
ADAM_STEP = 10

VMEM_LIMIT_BYTES = 56 * 1024 * 1024


def _params(sem=None):
    return pltpu.CompilerParams(dimension_semantics=sem, vmem_limit_bytes=VMEM_LIMIT_BYTES)


def _me():
    return lax.axis_index("x"), lax.axis_index("y"), lax.axis_index("c")


def _peer(k):
    x, y, c = _me()
    return (1 - x if k & 4 else x, 1 - y if k & 2 else y, 1 - c if k & 1 else c)


def _peer_index(k):
    px, py, pc = _peer(k)
    return 4 * px + 2 * py + pc


def _exchange(name, xs, scatter):
    n = len(xs)

    def body(*refs):
        x_refs, o_refs = refs[:n], refs[n:2 * n]
        send_sems, recv_sems, local_sems = refs[2 * n:]
        x, y, c = _me()
        me = 4 * x + 2 * y + c
        local = []
        for i in range(n):
            src = x_refs[i].at[me] if scatter else x_refs[i]
            cp = pltpu.make_async_copy(src, o_refs[i].at[me], local_sems.at[i])
            cp.start()
            local.append(cp)
        remote = []
        for k in range(1, N_DEV):
            for i in range(n):
                src = x_refs[i].at[_peer_index(k)] if scatter else x_refs[i]
                cp = pltpu.make_async_remote_copy(
                    src_ref=src, dst_ref=o_refs[i].at[me],
                    send_sem=send_sems.at[(k - 1) * n + i], recv_sem=recv_sems.at[(k - 1) * n + i],
                    device_id=_peer(k), device_id_type=pl.DeviceIdType.MESH)
                cp.start()
                remote.append(cp)
        for cp in remote:
            cp.wait()
        for cp in local:
            cp.wait()

    out_shape = [jax.ShapeDtypeStruct(x.shape if scatter else (N_DEV,) + x.shape, x.dtype) for x in xs]
    any_spec = pl.BlockSpec(memory_space=pl.ANY)
    return pl.pallas_call(
        body, name=name, out_shape=out_shape,
        in_specs=[any_spec] * n, out_specs=[any_spec] * n,
        scratch_shapes=[pltpu.SemaphoreType.DMA(((N_DEV - 1) * n,)),
                        pltpu.SemaphoreType.DMA(((N_DEV - 1) * n,)),
                        pltpu.SemaphoreType.DMA((n,))],
    )(*xs)


def all_gather(name, xs):
    return _exchange(name, xs, scatter=False)


def all_to_all(name, xs):
    return _exchange(name, xs, scatter=True)


_HBM = pl.BlockSpec(memory_space=pltpu.HBM)
_SEM = pl.BlockSpec(memory_space=pltpu.SEMAPHORE)
_ANY = pl.BlockSpec(memory_space=pl.ANY)
_EFFECT = pltpu.SideEffectType.DATAFLOW_SIDE_EFFECTING

SIBLING = 1
CHIP_PEERS = (2, 4, 6)


def _my_index():
    x, y, c = _me()
    return 4 * x + 2 * y + c


def plan_gather(x, land, me):
    return [(x, land.at[me], k) for k in (SIBLING,) + CHIP_PEERS]


def plan_gather_direct(x, land, me):
    return [(x, land.at[me], k) for k in range(1, N_DEV)]


def plan_scatter_direct(x, land, me):
    return [(x.at[me ^ k], land.at[me], k) for k in range(1, N_DEV)]


def plan_forward(x, land, me):
    return [(x.at[me ^ k], land.at[me ^ k], SIBLING) for k in CHIP_PEERS]


def plan_to_sibling(x, land, me):
    return [(x.at[me ^ SIBLING ^ (2 * q)], land.at[q], SIBLING) for q in range(4)]


def plan_to_owners(x, land, me):
    return [(x.at[q], land.at[q - 1], 2 * q) for q in (1, 2, 3)]


N_COPIES = {plan_gather: 4, plan_gather_direct: 7, plan_scatter_direct: 7, plan_forward: 3, plan_to_sibling: 4,
            plan_to_owners: 3}


def _plans(plan, n):
    return list(plan) if isinstance(plan, (list, tuple)) else [plan] * n


def _plan_copies(plan, x_refs, land_refs, send_sems, recv_sems):
    me = _my_index()
    plans = _plans(plan, len(land_refs))
    return [pltpu.make_async_remote_copy(src_ref=src, dst_ref=dst, send_sem=send_sems[i].at[s], recv_sem=recv_sems[i].at[s],
                                         device_id=_peer(k), device_id_type=pl.DeviceIdType.MESH)
            for i in range(len(land_refs)) for s, (src, dst, k) in enumerate(plans[i](x_refs[i], land_refs[i], me))]


def _landing(block, shape, dtype):
    start = (_my_index(),) + (0,) * (len(shape) - 1)
    return lax.dynamic_update_slice(lax.empty(shape, dtype), block[None], start)


def exchange_now(name, xs, lands, plan):
    n = len(lands)
    n_x = 0 if xs is None else n

    def body(*refs):
        land_in, land_out = refs[n_x:n_x + n], refs[n_x + n:n_x + 2 * n]
        x_refs = land_in if xs is None else refs[:n]
        sems = refs[n_x + 2 * n:]
        copies = _plan_copies(plan, x_refs, land_out, sems[:n], sems[n:])
        for cp in copies:
            cp.start()
        for cp in copies:
            cp.wait()

    return pl.pallas_call(
        body, name=name, out_shape=[jax.ShapeDtypeStruct(a.shape, a.dtype) for a in lands],
        in_specs=[_ANY] * (n_x + n), out_specs=[_ANY] * n,
        input_output_aliases={n_x + i: i for i in range(n)},
        scratch_shapes=[pltpu.SemaphoreType.DMA((N_COPIES[p],)) for p in _plans(plan, n)] * 2,
    )(*([] if xs is None else xs), *lands)


def exchange_start(name, xs, lands, plan, after=()):
    n, m = len(lands), len(after)
    n_x = 0 if xs is None else n
    arrays = ([] if xs is None else list(xs)) + list(lands)
    n_a = len(arrays)

    def body(*refs):
        land_refs = refs[n_x:n_a]
        x_refs = land_refs if xs is None else refs[:n]
        send_sems, recv_sems = refs[n_a + m:n_a + m + n], refs[n_a + m + n:n_a + m + 2 * n]
        token = refs[-1]
        for cp in _plan_copies(plan, x_refs, land_refs, send_sems, recv_sems):
            cp.start()
        token[...] = jnp.zeros_like(token)

    sems = [pltpu.SemaphoreType.DMA((N_COPIES[p],)) for p in _plans(plan, n)] * 2
    res = pl.pallas_call(
        body, name=name,
        out_shape=sems + [pltpu.HBM(a.shape, a.dtype) for a in arrays] + [jax.ShapeDtypeStruct((8, LANES), F32)],
        in_specs=[_HBM] * n_a + [_ANY] * m,
        out_specs=[_SEM] * (2 * n) + [_HBM] * n_a + [pl.BlockSpec(memory_space=pltpu.VMEM)],
        input_output_aliases={i: 2 * n + i for i in range(n_a)},
        compiler_params=pltpu.CompilerParams(has_side_effects=_EFFECT),
    )(*[pltpu.with_memory_space_constraint(a, pltpu.HBM) for a in arrays], *after)
    thru = res[2 * n:2 * n + n_a]
    handles = [(res[i], res[n + i], None if xs is None else thru[i], thru[n_x + i]) for i in range(n)]
    return handles, res[-1]


def exchange_wait(name, handles, plan, after):
    n = len(handles)
    in_place = handles[0][2] is None
    arrays = ([] if in_place else [h[2] for h in handles]) + [h[3] for h in handles]
    n_a = len(arrays)

    def body(*refs):
        land_refs = refs[n_a - n:n_a]
        x_refs = land_refs if in_place else refs[:n]
        send_sems, recv_sems = refs[n_a:n_a + n], refs[n_a + n:n_a + 2 * n]
        for cp in _plan_copies(plan, x_refs, land_refs, send_sems, recv_sems):
            cp.wait_send()
            cp.wait_recv()

    res = pl.pallas_call(
        body, name=name,
        out_shape=[pltpu.HBM(a.shape, a.dtype) for a in arrays],
        in_specs=[_HBM] * n_a + [_SEM] * (2 * n) + [_ANY],
        out_specs=[_HBM] * n_a,
        input_output_aliases={i: i for i in range(n_a)},
        compiler_params=pltpu.CompilerParams(has_side_effects=_EFFECT),
    )(*arrays, *[h[0] for h in handles], *[h[1] for h in handles], after)
    return (None if in_place else res[:n]), res[n_a - n:]


def chip_sum(name, gw, from_sibling):
    _, R, C = gw.shape
    tr = _row_tile(R, 512)
    me = _my_index().astype(jnp.int32).reshape(1)

    def body(me_ref, g_ref, s_ref, o_ref):
        o_ref[...] = (g_ref[...].astype(F32) + s_ref[...].astype(F32)).astype(o_ref.dtype)

    slot = pl.BlockSpec((None, tr, C), lambda q, i, me_ref: (q, i, 0))
    return pl.pallas_call(
        body, name=name, out_shape=jax.ShapeDtypeStruct((4, R, C), BF16),
        grid_spec=pltpu.PrefetchScalarGridSpec(
            num_scalar_prefetch=1, grid=(4, R // tr),
            in_specs=[pl.BlockSpec((None, tr, C), lambda q, i, me_ref: (me_ref[0] ^ (2 * q), i, 0)), slot],
            out_specs=slot),
        compiler_params=_params(("arbitrary", "arbitrary")),
    )(me, gw, from_sibling)


def _tile(n, cap):
    best = None
    for t in range(LANES, min(n, cap) + 1, LANES):
        if n % t == 0:
            best = t
    if best is None or (best < 256 and n <= 2304):
        return n
    return best


def _row_tile(m, cap):
    if m <= cap:
        return m
    t = cap
    while m % t:
        t //= 2
    return t


def _after_spec(after):
    return [] if after is None else [pl.BlockSpec(memory_space=pl.ANY)]


def _after_arg(after):
    return [] if after is None else [after]


def _mm_call(name, dims, grid, a_spec, b_spec, o_spec, o_shape, tile, a, b, out_dtypes, epilogue, extra, after):
    nk = grid[2]
    n_x, n_o = len(extra), len(out_dtypes)

    def body(a_ref, b_ref, *refs):
        x_refs, o_refs = refs[:n_x], refs[len(refs) - n_s - n_o:len(refs) - n_s]
        part = lax.dot_general(a_ref[...].astype(BF16), b_ref[...].astype(BF16), (dims, ((), ())),
                               preferred_element_type=F32)

        def finish(acc):
            vals = (acc,) if epilogue is None else epilogue(acc, *[x[...] for x in x_refs])
            for o_ref, val in zip(o_refs, vals):
                o_ref[...] = val.astype(o_ref.dtype)

        if nk == 1:
            finish(part)
        else:
            acc_ref = refs[-1]
            k = pl.program_id(2)

            @pl.when(k == 0)
            def _():
                acc_ref[...] = part

            @pl.when(jnp.logical_and(k > 0, k < nk - 1))
            def _():
                acc_ref[...] += part

            @pl.when(k == nk - 1)
            def _():
                finish(acc_ref[...] + part)

    n_s = 0 if nk == 1 else 1
    return pl.pallas_call(
        body, name=name, out_shape=[jax.ShapeDtypeStruct(o_shape, dt) for dt in out_dtypes], grid=grid,
        in_specs=[a_spec, b_spec] + [o_spec] * n_x + _after_spec(after), out_specs=[o_spec] * n_o,
        scratch_shapes=[pltpu.VMEM(tile, F32)] * n_s,
        compiler_params=_params(("parallel", "parallel", "arbitrary")),
    )(a, b, *extra, *_after_arg(after))


def mm_nn(name, a, w, out_dtypes=(F32,), epilogue=None, extra=(), after=None):
    M, K = a.shape
    G, _, n = w.shape
    tn = _tile(n, 1024)
    tm = _row_tile(M, 512 if tn > 1024 else 2048)
    tk = _row_tile(K, 1024)
    r = n // tn
    return _mm_call(
        name, ((1,), (0,)), (M // tm, G * r, K // tk),
        pl.BlockSpec((tm, tk), lambda i, j, k: (i, k)),
        pl.BlockSpec((None, tk, tn), lambda i, j, k: (j // r, k, j % r)),
        pl.BlockSpec((tm, tn), lambda i, j, k: (i, j)), (M, G * n), (tm, tn),
        a, w, out_dtypes, epilogue, extra, after)


def mm_nt(name, dy, w, out_dtypes=(F32,), epilogue=None, extra=(), after=None):
    M, N = dy.shape
    G, K, n = w.shape
    tn = _tile(n, 1024)
    tm = _row_tile(M, 512 if tn > 1024 else 2048)
    tk = _row_tile(K, 1024)
    r = n // tn
    return _mm_call(
        name, ((1,), (1,)), (M // tm, K // tk, G * r),
        pl.BlockSpec((tm, tn), lambda i, j, k: (i, k)),
        pl.BlockSpec((None, tk, tn), lambda i, j, k: (k // r, j, k % r)),
        pl.BlockSpec((tm, tk), lambda i, j, k: (i, j)), (M, K), (tm, tk),
        dy, w, out_dtypes, epilogue, extra, after)


def mm_tn(name, a, dy, G, out_dtype=F32, after=None):
    M, K = a.shape
    n = dy.shape[1] // G
    tn = _tile(n, 1024)
    tk = _row_tile(K, 512 if tn > 1024 else 1024)
    tm = _row_tile(M, 2048)
    r = n // tn
    return _mm_call(
        name, ((0,), (0,)), (K // tk, G * r, M // tm),
        pl.BlockSpec((tm, tk), lambda i, j, k: (k, i)),
        pl.BlockSpec((tm, tn), lambda i, j, k: (k, j)),
        pl.BlockSpec((None, tk, tn), lambda i, j, k: (j // r, i, j % r)), (G, K, n), (tk, tn),
        a, dy, (out_dtype,), None, (), after)[0]


ROW_BLOCK_BYTES = 12 * 1024 * 1024


def rowwise(name, fn, rows, vecs, outs, accs=(), after=None):
    rows = [r if isinstance(r, tuple) else (r, r.shape[1], 0) for r in rows]
    n_fn = len(rows) + len(vecs)
    vecs = list(vecs) + _after_arg(after)
    S = rows[0][0].shape[0]
    row_bytes = sum(w * a.dtype.itemsize for a, w, _ in rows) + sum(w * jnp.dtype(dt).itemsize for w, dt in outs)
    tm = _row_tile(S, 1024)
    while tm > 256 and tm * row_bytes > ROW_BLOCK_BYTES:
        tm //= 2
    n_r, n_v, n_o, n_a = len(rows), len(vecs), len(outs), len(accs)

    def body(*refs):
        ins = [ref[...] for ref in refs[:n_r + n_v]]
        o_refs = refs[n_r + n_v:n_r + n_v + n_o]
        a_refs = refs[n_r + n_v + n_o:]
        res = fn(*ins[:n_fn])
        res = res if isinstance(res, (tuple, list)) else (res,)
        for ref, val in zip(o_refs, res[:n_o]):
            ref[...] = val.astype(ref.dtype)
        if n_a:
            @pl.when(pl.program_id(0) == 0)
            def _():
                for ref in a_refs:
                    ref[...] = jnp.zeros_like(ref)
            for ref, val in zip(a_refs, res[n_o:]):
                ref[...] += val

    in_specs = [pl.BlockSpec((tm, w), functools.partial(lambda cb, i: (i, cb), cb)) for _, w, cb in rows]
    in_specs += [pl.BlockSpec(v.shape, lambda i: (0, 0)) for v in vecs]
    out_specs = [pl.BlockSpec((tm, w), lambda i: (i, 0)) for w, _ in outs]
    out_specs += [pl.BlockSpec(a, lambda i: (0, 0)) for a in accs]
    out_shape = [jax.ShapeDtypeStruct((S, w), dt) for w, dt in outs]
    out_shape += [jax.ShapeDtypeStruct(a, F32) for a in accs]
    res = pl.pallas_call(
        body, name=name, out_shape=out_shape, grid=(S // tm,),
        in_specs=in_specs, out_specs=out_specs,
        compiler_params=_params(("arbitrary",)),
    )(*[r[0] for r in rows], *vecs)
    return res


def _rms(x):
    return x * lax.rsqrt(jnp.mean(x * x, axis=-1, keepdims=True) + EPS)


def _modulate(x, gain, scale, shift):
    return _rms(x) * gain * (1.0 + scale) + shift


def _ada_slices(ada_raw, bias):
    ada = ada_raw + bias
    return [ada[:, i * D_MODEL:(i + 1) * D_MODEL] for i in range(ada.shape[1] // D_MODEL)]


def _norm_wide_heads(y, gain, heads):
    w = y.shape[1] // heads
    return jnp.concatenate([_rms(y[:, h * w:(h + 1) * w]) * gain[:, h * w:(h + 1) * w] for h in range(heads)], axis=1)


def _norm_fox_heads(x, gain):
    outs = []
    for p in range(x.shape[1] // LANES):
        blk = x[:, p * LANES:(p + 1) * LANES]
        low = lax.broadcasted_iota(jnp.int32, blk.shape, 1) < FOX_DH
        sq = blk * blk
        ss_low = jnp.sum(jnp.where(low, sq, 0.0), axis=1, keepdims=True)
        ss_high = jnp.sum(jnp.where(low, 0.0, sq), axis=1, keepdims=True)
        outs.append(blk * lax.rsqrt(jnp.where(low, ss_low, ss_high) * (1.0 / FOX_DH) + EPS))
    return jnp.concatenate(outs, axis=1) * gain


def _silu(x):
    return x * jax.nn.sigmoid(x)


def _log_sigmoid(z):
    return -(jnp.maximum(-z, 0.0) + jnp.log(1.0 + jnp.exp(-jnp.abs(z))))


def _rotate(x, cos, sin, heads, sign):
    w = x.shape[1] // heads
    half = w // 2
    outs = []
    for h in range(heads):
        x1 = x[:, h * w:h * w + half]
        x2 = x[:, h * w + half:(h + 1) * w]
        outs += [x1 * cos - sign * x2 * sin, sign * x1 * sin + x2 * cos]
    return jnp.concatenate(outs, axis=1)


def _vjp(fn, primals, ct):
    return jax.vjp(fn, *primals)[1](ct)


_LOG_GAMMAS = [float(np.log(np.float32(1.0) - np.float32(2.0) ** np.float32(-5.0 - h))) for h in range(RET_HEADS)]


RET_ROWS = 1024


def retention(name, q, k, v, reverse, out_dtype=F32):
    (qa, dk, qo), (ka, _, ko), (va, dv, vo) = q, k, v
    S = qa.shape[0]
    C = RET_CHUNK
    rows = _row_tile(S, RET_ROWS)
    nb = S // rows

    def body(q_ref, k_ref, v_ref, o_ref, state):
        h = pl.program_id(0)

        @pl.when(pl.program_id(1) == 0)
        def _():
            state[...] = jnp.zeros_like(state)

        log_g = jnp.float32(_LOG_GAMMAS[RET_HEADS - 1])
        for i in range(RET_HEADS - 2, -1, -1):
            log_g = jnp.where(h == i, jnp.float32(_LOG_GAMMAS[i]), log_g)
        row = lax.broadcasted_iota(jnp.int32, (C, C), 0)
        col = lax.broadcasted_iota(jnp.int32, (C, C), 1)
        rel = (col - row if reverse else row - col).astype(F32)
        decay = jnp.where(rel >= 0, jnp.exp(log_g * jnp.maximum(rel, 0.0)), 0.0)
        j = lax.broadcasted_iota(jnp.int32, (C, 1), 0).astype(F32)
        q_decay = jnp.exp(log_g * (C - j if reverse else j + 1.0))
        k_decay = jnp.exp(log_g * (j if reverse else C - 1.0 - j))
        chunk_decay = jnp.exp(jnp.full((1, 1), log_g * C, F32))

        chunks = range(rows // C)
        for ci in (reversed(chunks) if reverse else chunks):
            rs = slice(ci * C, (ci + 1) * C)
            qc = q_ref[rs, :].astype(BF16)
            kf = k_ref[rs, :].astype(F32)
            vc = v_ref[rs, :].astype(BF16)
            scores = lax.dot_general(qc, kf.astype(BF16), (((1,), (1,)), ((), ())), preferred_element_type=F32) * decay
            intra = jnp.dot(scores.astype(BF16), vc, preferred_element_type=F32)
            cross = jnp.dot(qc, state[...].astype(BF16), preferred_element_type=F32) * q_decay
            o_ref[rs, :] = (intra + cross).astype(o_ref.dtype)
            upd = lax.dot_general((kf * k_decay).astype(BF16), vc, (((0,), (0,)), ((), ())), preferred_element_type=F32)
            state[...] = state[...] * chunk_decay + upd

    def block(i):
        return nb - 1 - i if reverse else i

    return pl.pallas_call(
        body, name=name, out_shape=jax.ShapeDtypeStruct((S, RET_HEADS * dv), out_dtype),
        grid=(RET_HEADS, nb),
        in_specs=[pl.BlockSpec((rows, dk), lambda h, i: (block(i), qo + h)),
                  pl.BlockSpec((rows, dk), lambda h, i: (block(i), ko + h)),
                  pl.BlockSpec((rows, dv), lambda h, i: (block(i), vo + h))],
        out_specs=pl.BlockSpec((rows, dv), lambda h, i: (block(i), h)),
        scratch_shapes=[pltpu.VMEM((dk, dv), F32)],
        compiler_params=_params(("parallel", "arbitrary")),
    )(qa, ka, va)


FOX_T = 256
N_PAIR = FOX_HEADS // 2
FOX_SCALE = FOX_DH ** -0.5


def _fox_heads(q2):
    low = lax.broadcasted_iota(jnp.int32, (1, LANES), 1) < FOX_DH
    return [(mask, jnp.where(mask, q2 * FOX_SCALE, 0.0).astype(BF16)) for mask in (low, jnp.logical_not(low))]


def _fox_parts(j, t):
    return ([(0, j * t, False)] if j else []) + [(j * t, (j + 1) * t, True)]


def _fox_scores(qa, k_ref, ft_ref, head, lo, hi, diagonal):
    k_blk = k_ref[lo:hi, :].astype(BF16)
    s = lax.dot_general(qa, k_blk, (((1,), (1,)), ((), ())), preferred_element_type=F32) - ft_ref[pl.ds(head, 1), lo:hi]
    if diagonal:
        n = hi - lo
        s = jnp.where(lax.broadcasted_iota(jnp.int32, (n, n), 1) <= lax.broadcasted_iota(jnp.int32, (n, n), 0), s, -jnp.inf)
    return s


def fox_forward(name, qn, kn, kvf, f_cum_t):
    S = qn.shape[0]
    t = _row_tile(S, FOX_T)
    v_block0 = D_MODEL // LANES

    def variant(j, pair, q_ref, k_ref, v_ref, ft_ref, y_ref, lse_ref):
        ys, lses = [], []
        for a, (mask, qa) in enumerate(_fox_heads(q_ref[...])):
            parts = [(lo, hi, _fox_scores(qa, k_ref, ft_ref, 2 * pair + a, lo, hi, dg)) for lo, hi, dg in _fox_parts(j, t)]
            m = functools.reduce(jnp.maximum, [jnp.max(s, axis=1, keepdims=True) for _, _, s in parts])
            l, acc = 0.0, 0.0
            for lo, hi, s in parts:
                e = jnp.exp(s - m)
                l = l + jnp.sum(e, axis=1, keepdims=True)
                acc = acc + jnp.dot(e.astype(BF16), v_ref[lo:hi, :].astype(BF16), preferred_element_type=F32)
            ys.append(acc / l)
            lses.append(m + jnp.log(l))
        low = lax.broadcasted_iota(jnp.int32, (1, LANES), 1) < FOX_DH
        y_ref[...] = jnp.where(low, ys[0], ys[1])
        lse_ref[...] = jnp.where(low, lses[0], lses[1])

    def body(*refs):
        pair, i = pl.program_id(0), pl.program_id(1)
        for j in range(S // t):
            pl.when(i == j)(functools.partial(variant, j, pair, *refs))

    return pl.pallas_call(
        body, name=name,
        out_shape=[jax.ShapeDtypeStruct((S, D_MODEL), F32), jax.ShapeDtypeStruct((S, D_MODEL), F32)],
        grid=(N_PAIR, S // t),
        in_specs=[pl.BlockSpec((t, LANES), lambda p, i: (i, p)),
                  pl.BlockSpec((S, LANES), lambda p, i: (0, p)),
                  pl.BlockSpec((S, LANES), lambda p, i: (0, v_block0 + p)),
                  pl.BlockSpec((LANES, S), lambda p, i: (0, 0))],
        out_specs=[pl.BlockSpec((t, LANES), lambda p, i: (i, p)),
                   pl.BlockSpec((t, LANES), lambda p, i: (i, p))],
        compiler_params=_params(("parallel", "arbitrary")),
    )(qn, kn, kvf, f_cum_t)


def fox_backward(name, qn, kn, kvf, f_cum_t, y, dy, lse):
    S = qn.shape[0]
    t = _row_tile(S, FOX_T)
    v_block0 = D_MODEL // LANES

    def variant(j, pair, q_ref, k_ref, v_ref, ft_ref, y_ref, dy_ref, lse_ref, dq_ref, dk_ref, dv_ref, dfq_ref, dfk_ref,
                dv_acc):
        y2, dy2, lse2 = y_ref[...], dy_ref[...], lse_ref[...]
        lane = lax.broadcasted_iota(jnp.int32, (t, LANES), 1)
        dqs, dfq = [], jnp.zeros((t, LANES), F32)
        for a, (mask, qa) in enumerate(_fox_heads(q_ref[...].astype(F32))):
            lse_a = jnp.max(jnp.where(mask, lse2, -jnp.inf), axis=1, keepdims=True)
            dy_a = jnp.where(mask, dy2, 0.0)
            delta = jnp.sum(dy_a * y2, axis=1, keepdims=True)
            dy_b = dy_a.astype(BF16)
            dq, row_sum = 0.0, 0.0
            for lo, hi, dg in _fox_parts(j, t):
                p = jnp.exp(_fox_scores(qa, k_ref, ft_ref, 2 * pair + a, lo, hi, dg) - lse_a)
                dp = lax.dot_general(dy_b, v_ref[lo:hi, :].astype(BF16), (((1,), (1,)), ((), ())), preferred_element_type=F32)
                ds = p * (dp - delta)
                row_sum = row_sum + jnp.sum(ds, axis=1, keepdims=True)
                dfk_ref[pl.ds(a, 1), lo:hi] += -jnp.sum(ds, axis=0, keepdims=True)
                ds_b = ds.astype(BF16)
                dq = dq + jnp.dot(ds_b, k_ref[lo:hi, :].astype(BF16), preferred_element_type=F32)
                dk_ref[lo:hi, :] += lax.dot_general(ds_b, qa, (((0,), (0,)), ((), ())), preferred_element_type=F32)
                dv_acc[lo:hi, :] += lax.dot_general(p.astype(BF16), dy_b, (((0,), (0,)), ((), ())), preferred_element_type=F32)
            dqs.append(dq * FOX_SCALE)
            dfq = dfq + jnp.where(lane == 2 * pair + a, row_sum, 0.0)
        low = lax.broadcasted_iota(jnp.int32, (1, LANES), 1) < FOX_DH
        dq_ref[...] = jnp.where(low, dqs[0], dqs[1])
        dfq_ref[...] = dfq

    def body(*refs):
        pair, i = pl.program_id(0), pl.program_id(1)
        dk_ref, dv_ref, _, dfk_ref, dv_acc = refs[8:13]

        @pl.when(i == 0)
        def _():
            dk_ref[...] = jnp.zeros_like(dk_ref)
            dv_acc[...] = jnp.zeros_like(dv_acc)
            dfk_ref[...] = jnp.zeros_like(dfk_ref)

        for j in range(S // t):
            pl.when(i == j)(functools.partial(variant, j, pair, *refs))

        @pl.when(i == S // t - 1)
        def _():
            dv_ref[...] = dv_acc[...].astype(dv_ref.dtype)

    row_blk = pl.BlockSpec((t, LANES), lambda p, i: (i, p))
    col_blk = pl.BlockSpec((S, LANES), lambda p, i: (0, p))
    return pl.pallas_call(
        body, name=name,
        out_shape=[jax.ShapeDtypeStruct((S, D_MODEL), F32)] * 2
        + [jax.ShapeDtypeStruct((S, D_MODEL), BF16), jax.ShapeDtypeStruct((N_PAIR, S, LANES), F32),
           jax.ShapeDtypeStruct((N_PAIR, 8, S), F32)],
        grid=(N_PAIR, S // t),
        in_specs=[row_blk, col_blk,
                  pl.BlockSpec((S, LANES), lambda p, i: (0, v_block0 + p)),
                  pl.BlockSpec((LANES, S), lambda p, i: (0, 0)),
                  row_blk, row_blk, row_blk],
        out_specs=[row_blk, col_blk, col_blk,
                   pl.BlockSpec((None, t, LANES), lambda p, i: (p, i, 0)),
                   pl.BlockSpec((None, 8, S), lambda p, i: (p, 0, 0))],
        scratch_shapes=[pltpu.VMEM((S, LANES), F32)],
        compiler_params=_params(("parallel", "arbitrary")),
    )(qn, kn, kvf, f_cum_t, y, dy, lse)


def cumsum_rows(name, x, reverse):
    S = x.shape[0]
    C = LANES
    nc = S // C

    def body(x_ref, o_ref):
        row = lax.broadcasted_iota(jnp.int32, (C, C), 0)
        col = lax.broadcasted_iota(jnp.int32, (C, C), 1)
        tri = jnp.where(col >= row if reverse else col <= row, 1.0, 0.0).astype(F32)
        carry = jnp.zeros((1, LANES), F32)
        for i in (range(nc - 1, -1, -1) if reverse else range(nc)):
            blk = x_ref[i * C:(i + 1) * C, :]
            loc = jnp.dot(tri, blk, preferred_element_type=F32, precision=lax.Precision.HIGHEST)
            o_ref[i * C:(i + 1) * C, :] = loc + carry
            carry = carry + (loc[0:1, :] if reverse else loc[C - 1:C, :])

    return pl.pallas_call(body, name=name, out_shape=jax.ShapeDtypeStruct((S, LANES), F32),
                          compiler_params=_params())(x)


def adamw(name, parts, w, m, v):
    L, R, C = w.shape
    tr = _row_tile(R, 256)
    nr = R // tr
    counts = [len(p) for p in parts]

    def body(*refs):
        w_ref, m_ref, v_ref, g_out, d_out, m_out, v_out = refs[sum(counts):]
        for layer in range(L):
            p_refs = refs[sum(counts[:layer]):sum(counts[:layer + 1])]

            @pl.when(pl.program_id(0) == layer)
            def _(p_refs=p_refs, slots=[n for _, n in parts[layer]]):
                g = None
                for p_ref, n in zip(p_refs, slots):
                    for i in range(n):
                        g = p_ref[i].astype(F32) if g is None else g + p_ref[i].astype(F32)
                m2 = ADAM_B1 * m_ref[...] + (1.0 - ADAM_B1) * g
                v2 = ADAM_B2 * v_ref[...] + (1.0 - ADAM_B2) * jnp.square(g)
                m_hat = m2 / (1.0 - ADAM_B1 ** ADAM_STEP)
                v_hat = v2 / (1.0 - ADAM_B2 ** ADAM_STEP)
                g_out[...] = g
                d_out[...] = -ADAM_LR * (m_hat / (jnp.sqrt(v_hat) + ADAM_EPS) + ADAM_WD * w_ref[...])
                m_out[...] = m2
                v_out[...] = v2

    def part_spec(layer, n):
        return pl.BlockSpec((n, tr, C), lambda l, i: (0, jnp.where(l == layer, i, jnp.where(l < layer, 0, nr - 1)), 0))

    blk = pl.BlockSpec((None, tr, C), lambda l, i: (l, i, 0))
    return pl.pallas_call(
        body, name=name, out_shape=[jax.ShapeDtypeStruct((L, R, C), F32)] * 4, grid=(L, nr),
        in_specs=[part_spec(layer, n) for layer in range(L) for _, n in parts[layer]] + [blk, blk, blk],
        out_specs=[blk] * 4, compiler_params=_params(("arbitrary", "arbitrary")),
    )(*[a for layer in parts for a, _ in layer], w, m, v)


def kernel(x, c, positions, norm_mix_gain, norm_mlp_gain, w_ada, b_ada, w_mlp_in, w_mlp_out, ret_w_in, ret_norm_gain, ret_w_out, kv_norm_gain, kv_w_ada, kv_b_ada, kv_w, forget_bias, k_norm_gain, fox_w_in, q_norm_gain, fox_w_out, loss_target, m_norm_mix_gain, m_norm_mlp_gain, m_w_ada, m_b_ada, m_w_mlp_in, m_w_mlp_out, m_ret_w_in, m_ret_norm_gain, m_ret_w_out, m_kv_norm_gain, m_kv_w_ada, m_kv_b_ada, m_kv_w, m_forget_bias, m_k_norm_gain, m_fox_w_in, m_q_norm_gain, m_fox_w_out, v_norm_mix_gain, v_norm_mlp_gain, v_w_ada, v_b_ada, v_w_mlp_in, v_w_mlp_out, v_ret_w_in, v_ret_norm_gain, v_ret_w_out, v_kv_norm_gain, v_kv_w_ada, v_kv_b_ada, v_kv_w, v_forget_bias, v_k_norm_gain, v_fox_w_in, v_q_norm_gain, v_fox_w_out):
    D = D_MODEL
    S = x.shape[1]
    x0 = x.reshape(S, D)
    target = loss_target.reshape(S, D)
    me = 4 * lax.axis_index("x") + 2 * lax.axis_index("y") + lax.axis_index("c")
    n_ada = w_ada.shape[2]
    n_kvada = kv_w_ada.shape[1]
    n_kv = kv_w.shape[1]

    c_all, ret_gain = all_gather("gather_c", [c.reshape(D // LANES, LANES), ret_norm_gain.reshape(RET_HEADS, -1)])
    ret_gain = jnp.transpose(ret_gain, (1, 0, 2)).reshape(1, RET_HEADS * RET_V)
    c_act = rowwise("silu_c", _silu, [c_all.reshape(N_DEV, D)], [], [(D, F32)])[0]
    w_ada_cat = jnp.concatenate([w_ada[0], w_ada[1], kv_w_ada], axis=1).astype(BF16)[None]
    n_cat = 2 * n_ada + n_kvada
    ada_part = mm_nn("ada_proj", c_act, w_ada_cat)[0]
    ada_mine = all_to_all("ada_rows", [ada_part.reshape(N_DEV, n_cat // LANES, LANES)])[0]
    ada_mine = ada_mine.reshape(N_DEV, n_cat)
    ada_raw = [ada_mine[:, l * n_ada:(l + 1) * n_ada].reshape(1, 6 * D) for l in range(2)]
    kvada_raw = ada_mine[:, 2 * n_ada:].reshape(1, 2 * D)
    kv_bias = kv_b_ada.reshape(1, 2 * D)
    kv_gain = kv_norm_gain.reshape(1, D)
    fb = jnp.pad(forget_bias.reshape(1, FOX_HEADS), ((0, 0), (0, LANES - FOX_HEADS)))
    k_gain = jnp.tile(k_norm_gain.reshape(1, FOX_DH), (1, FOX_HEADS))
    q_gain = jnp.tile(q_norm_gain.reshape(1, FOX_DH), (1, FOX_HEADS))

    w_names = ["ret_in", "ret_out", "mlp_in0", "mlp_out0", "kv", "fox_in", "fox_out", "mlp_in1", "mlp_out1"]
    shards = [ret_w_in[0].astype(BF16), ret_w_out[0].astype(BF16), w_mlp_in[0].astype(BF16), w_mlp_out[0].astype(BF16),
              kv_w.astype(BF16), fox_w_in[0].astype(BF16), fox_w_out[0].astype(BF16), w_mlp_in[1].astype(BF16),
              w_mlp_out[1].astype(BF16)]
    two_level = {"ret_in", "ret_out", "mlp_in0", "mlp_out0"}
    w_plans = {name: plan_gather if name in two_level else plan_gather_direct for name in w_names}
    w_handles, token = exchange_start("gather_weights_start", shards,
                                      [_landing(a, (N_DEV,) + a.shape, a.dtype) for a in shards],
                                      [w_plans[name] for name in w_names], after=(ada_mine, ret_gain))
    w_handles = dict(zip(w_names, w_handles))

    forwards = {}

    def forward_early(name, after):
        arrived = exchange_wait("gather_wait_" + name, [w_handles[name]], w_plans[name], after)[1]
        forwards[name], tok = exchange_start("gather_forward_start_" + name, None, arrived, plan_forward)
        return tok

    def weight(name, after):
        if name in two_level:
            return exchange_wait("gather_forward_wait_" + name, forwards[name], plan_forward, after)[1][0]
        return exchange_wait("gather_wait_" + name, [w_handles[name]], w_plans[name], after)[1][0]

    pos = positions.reshape(S, 1).astype(F32)
    half = RET_QK // 2
    inv_freq = jnp.asarray((ROPE_BASE ** (-np.arange(half, dtype=np.float32) / half)).reshape(1, half), F32)

    def angles(p, f):
        ang = p * f
        return jnp.cos(ang), jnp.sin(ang)

    cos, sin = rowwise("rope_table", angles, [pos], [inv_freq], [(half, F32), (half, F32)])

    def mod_mix(layer):
        def fn(xb, ada, bias, gain):
            sh, sc = _ada_slices(ada, bias)[:2]
            return _modulate(xb, gain[layer:layer + 1], sc, sh)
        return fn

    def mod_mlp(layer):
        def fn(xb, ada, bias, gain):
            sh, sc = _ada_slices(ada, bias)[3:5]
            return _modulate(xb, gain[layer:layer + 1], sc, sh)
        return fn

    h1_0 = rowwise("mod_mix0", mod_mix(0), [x0], [ada_raw[0], b_ada[0:1], norm_mix_gain], [(D, BF16)], after=token)[0]
    W_ret_in = weight("ret_in", forward_early("ret_in", h1_0))
    proj = mm_nn("ret_proj", h1_0, W_ret_in, (BF16,))[0]
    token = forward_early("ret_out", proj)

    def rope_fwd(qb, kb, cs, sn):
        return (_rotate(qb.astype(F32), cs, sn, RET_HEADS, 1.0),
                _rotate(kb.astype(F32), cs, sn, RET_HEADS, 1.0) * (RET_QK ** -0.5))

    q_rot, k_rot = rowwise("rope", rope_fwd, [(proj, D, 0), (proj, D, 1), cos, sin], [], [(D, BF16), (D, BF16)],
                           after=token)
    v_ret = (proj, RET_V, (2 * D) // RET_V)
    y_ret = retention("ret_fwd", (q_rot, RET_QK, 0), (k_rot, RET_QK, 0), v_ret, reverse=False)

    def ret_gate(yb, gb, gain):
        return _silu(gb.astype(F32)) * _norm_wide_heads(yb, gain, RET_HEADS)

    mixin0 = rowwise("ret_gate", ret_gate, [y_ret, (proj, 2 * D, 2)], [ret_gain], [(2 * D, BF16)])[0]
    W_ret_out = weight("ret_out", mixin0).reshape(1, 2 * D, D)
    mix0 = mm_nn("ret_out", mixin0, W_ret_out, after=forward_early("mlp_in0", W_ret_out))[0]

    def residual_mod(layer, slot):
        def fn(xb, bb, ada, bias, gain):
            s = _ada_slices(ada, bias)
            xn = xb + s[2] * bb
            return xn, _modulate(xn, gain[layer:layer + 1], s[4], s[3])
        return fn

    x1, h2_0 = rowwise("res_mix0", residual_mod(0, 0), [x0, mix0], [ada_raw[0], b_ada[0:1], norm_mlp_gain],
                       [(D, F32), (D, BF16)])

    W_mlp_in, W_mlp_out = {}, {}

    def mlp_forward(tag, h2, layer):
        W_mlp_in[layer] = weight("mlp_in" + tag, h2)
        early = forward_early("mlp_out" + tag, W_mlp_in[layer]) if "mlp_out" + tag in two_level else None
        u, act = mm_nn("mlp_in" + tag, h2, W_mlp_in[layer], (BF16, BF16), after=early,
                       epilogue=lambda acc: (acc, jnp.square(jnp.maximum(acc, 0.0))))
        W_mlp_out[layer] = weight("mlp_out" + tag, act).reshape(1, 4 * D, D)
        return u, act, mm_nn("mlp_out" + tag, act, W_mlp_out[layer])[0]

    u0, act0, mlp0 = mlp_forward("0", h2_0, 0)

    def res_mlp0(xb, bb, ada0, bias0, ada1, bias1, kva, kvb, gain_mix, gain_kv):
        xn = xb + _ada_slices(ada0, bias0)[5] * bb
        s1 = _ada_slices(ada1, bias1)
        kv_shift, kv_scale = _ada_slices(kva, kvb)
        return xn, _modulate(xn, gain_kv, kv_scale, kv_shift), _modulate(xn, gain_mix[1:2], s1[1], s1[0])

    x2, h_kv, h1_1 = rowwise("res_mlp0", res_mlp0, [x1, mlp0],
                             [ada_raw[0], b_ada[0:1], ada_raw[1], b_ada[1:2], kvada_raw, kv_bias, norm_mix_gain, kv_gain],
                             [(D, F32), (D, BF16), (D, BF16)])

    kv_full = jnp.transpose(weight("kv", h_kv), (1, 0, 2)).reshape(D, N_DEV * n_kv)
    W_kv = jnp.pad(kv_full, ((0, 0), (0, KV_PAD - N_DEV * n_kv)))[None]
    kvf = mm_nn("kv_proj", h_kv, W_kv)[0]

    def kv_post(kb, fblk, kg, bias):
        head = lax.broadcasted_iota(jnp.int32, fblk.shape, 1) < FOX_HEADS
        return _norm_fox_heads(kb, kg), jnp.where(head, _log_sigmoid(fblk + bias), 0.0)

    kn, log_f = rowwise("kv_post", kv_post, [(kvf, D, 0), (kvf, LANES, 2 * D // LANES)], [k_gain, fb],
                        [(D, BF16), (LANES, F32)])
    f_cum = cumsum_rows("f_cumsum", log_f, reverse=False)
    f_cum_t = f_cum.T

    W_fox_in = weight("fox_in", kvf)
    qo = mm_nn("fox_proj", h1_1, W_fox_in)[0]
    qn = rowwise("q_norm", _norm_fox_heads, [(qo, D, 0)], [q_gain], [(D, BF16)])[0]
    y_att, lse = fox_forward("fox_fwd", qn, kn, kvf, f_cum_t)
    mixin1 = rowwise("fox_gate", lambda ob, yb: jax.nn.sigmoid(ob) * yb, [(qo, D, 1), y_att], [], [(D, BF16)])[0]
    W_fox_out = weight("fox_out", mixin1).reshape(1, D, D)
    mix1 = mm_nn("fox_out", mixin1, W_fox_out)[0]
    x3, h2_1 = rowwise("res_mix1", residual_mod(1, 0), [x2, mix1], [ada_raw[1], b_ada[1:2], norm_mlp_gain],
                       [(D, F32), (D, BF16)])
    u1, act1, mlp1 = mlp_forward("1", h2_1, 1)

    def scatter_start(tag, gws, after=()):
        lands = [lax.empty((4,) + g.shape[1:], g.dtype) for g in gws]
        return exchange_start("scatter_sibling_start_" + tag, gws, lands, plan_to_sibling, after)

    def scatter_relay(tag, handles, after, start_after=()):
        gws, from_sibling = exchange_wait("scatter_sibling_wait_" + tag, handles, plan_to_sibling, after)
        sums = [chip_sum("chip_sum_%s%d" % (tag, i), g, s) for i, (g, s) in enumerate(zip(gws, from_sibling))]
        lands = [lax.empty((3,) + s.shape[1:], s.dtype) for s in sums]
        return exchange_start("scatter_owner_start_" + tag, sums, lands, plan_to_owners, start_after)

    def scatter_direct_start(tag, gws):
        me = _my_index()
        lands = [_landing(lax.dynamic_index_in_dim(g, me, 0, keepdims=False), g.shape, g.dtype) for g in gws]
        return exchange_start("scatter_direct_start_" + tag, gws, lands, plan_scatter_direct)

    def scatter_direct_finish(tag, handles, after):
        return [[(r, N_DEV)] for r in exchange_wait("scatter_direct_wait_" + tag, handles, plan_scatter_direct, after)[1]]

    def scatter_finish(tag, handles, after):
        sums, received = exchange_wait("scatter_owner_wait_" + tag, handles, plan_to_owners, after)
        return [[(s, 1), (r, 3)] for s, r in zip(sums, received)]

    def loss_head(xb, bb, tb, ada, bias):
        g2 = _ada_slices(ada, bias)[5]
        err = xb + g2 * bb - tb
        dx = err * (1.0 / D)
        loss = 0.5 * jnp.sum(jnp.sum(err * err, axis=1, keepdims=True) * (1.0 / D), axis=0, keepdims=True)
        return dx, (dx * g2), jnp.broadcast_to(loss, (1, LANES)), jnp.sum(dx * bb, axis=0, keepdims=True)

    dx4, dmlp1, loss_acc, dg2_1 = rowwise("loss_head", loss_head, [x3, mlp1, target], [ada_raw[1], b_ada[1:2]],
                                          [(D, F32), (D, BF16)], [(1, LANES), (1, D)])

    def mlp_backward(tag, dmlp, act, u, h2, layer, after=None):
        du = mm_nt("mlp_out_dx" + tag, dmlp, W_mlp_out[layer], (BF16,), extra=(u,), after=after,
                   epilogue=lambda acc, ub: (acc * (2.0 * jnp.maximum(ub.astype(F32), 0.0)),))[0]
        gw_out = mm_tn("mlp_out_dw" + tag, act, dmlp, 1, BF16).reshape(N_DEV, -1, D)
        dh2 = mm_nt("mlp_in_dx" + tag, du, W_mlp_in[layer], (BF16,))[0]
        gw_in = mm_tn("mlp_in_dw" + tag, h2, du, N_DEV, BF16)
        return dh2, gw_in, gw_out

    def mod_backward(layer, slots, gate_slot):
        def fn(xb, dhb, dresb, branchb, ada, bias, gain):
            s = _ada_slices(ada, bias)
            g = gain[layer:layer + 1]
            dx, dgain, dsc, dsh = _vjp(_modulate, (xb, g, s[slots[1]], s[slots[0]]), dhb.astype(F32))
            dx = dx + dresb
            d_branch = dx * s[gate_slot]
            return dx, d_branch, dgain, dsc, dsh, jnp.sum(dx * branchb, axis=0, keepdims=True)
        return fn

    vec = (1, D)
    dh2_1, gw_mlp_in1, gw_mlp_out1 = mlp_backward("1", dmlp1, act1, u1, h2_1, 1)
    scat_a1, token_a1 = scatter_direct_start("a1", [gw_mlp_in1, gw_mlp_out1])
    dx3, dmix1, dgain_mlp1, dsc2_1, dsh2_1, dg1_1 = rowwise(
        "mod_mlp1_bwd", mod_backward(1, (3, 4), 2), [x3, dh2_1, dx4, mix1], [ada_raw[1], b_ada[1:2], norm_mlp_gain],
        [(D, F32), (D, BF16)], [vec] * 4, after=token_a1)
    dmixin1 = mm_nt("fox_out_dx", dmix1, W_fox_out)[0]
    gw_fox_out = mm_tn("fox_out_dw", mixin1, dmix1, 1, BF16).reshape(N_DEV, -1, D)

    def fox_gate_bwd(db, ob, yb):
        sg = jax.nn.sigmoid(ob)
        return db * sg, db * yb * sg * (1.0 - sg)

    dy_att, d_og = rowwise("fox_gate_bwd", fox_gate_bwd, [dmixin1, (qo, D, 1), y_att], [], [(D, F32), (D, F32)])
    dqn, dkn, dv_att, dfq, dfk = fox_backward("fox_bwd", qn, kn, kvf, f_cum_t, y_att, dy_att, lse)

    def q_norm_bwd(qb, db, ogb, gain):
        dq, dgain = _vjp(_norm_fox_heads, (qb, gain), db)
        return jnp.concatenate([dq, ogb], axis=1), dgain

    dqo, dq_gain = rowwise("q_norm_bwd", q_norm_bwd, [(qo, D, 0), dqn, d_og], [q_gain], [(2 * D, BF16)], [vec])
    dh1_1 = mm_nt("fox_proj_dx", dqo, W_fox_in, (BF16,))[0]
    gw_fox_in = mm_tn("fox_proj_dw", h1_1, dqo, N_DEV, BF16)

    dfk_rows = jnp.pad(dfk[:, :2, :].reshape(FOX_HEADS, S).T, ((0, 0), (0, LANES - FOX_HEADS)))

    def df_total(*blks):
        tot = blks[0]
        for b in blks[1:]:
            tot = tot + b
        return tot

    d_fcum = rowwise("df_sum", df_total, [dfk_rows] + [dfq[p] for p in range(N_PAIR)], [], [(LANES, F32)])[0]
    d_logf = cumsum_rows("df_cumsum", d_fcum, reverse=True)

    def kv_post_bwd(kb, fblk, dkb, dvb, dlf, kg, bias):
        dk, dgain = _vjp(_norm_fox_heads, (kb, kg), dkb)
        df = dlf * (1.0 / (1.0 + jnp.exp(fblk + bias)))
        return jnp.concatenate([dk, dvb.astype(F32), df], axis=1), dgain, jnp.sum(df, axis=0, keepdims=True)

    dkvf, dk_gain, dfb = rowwise("kv_post_bwd", kv_post_bwd,
                                 [(kvf, D, 0), (kvf, LANES, 2 * D // LANES), dkn, dv_att, d_logf], [k_gain, fb],
                                 [(KV_PAD, BF16)], [vec, (1, LANES)])
    dh_kv = mm_nt("kv_proj_dx", dkvf, W_kv, (BF16,))[0]
    gw_kv = mm_tn("kv_proj_dw", h_kv, dkvf, 1, BF16)[0, :, :N_DEV * n_kv]
    gw_kv = jnp.transpose(gw_kv.reshape(D, N_DEV, n_kv), (1, 0, 2))
    scat_a2, token_a = scatter_direct_start("a2", [gw_fox_out, gw_fox_in, gw_kv])

    def x2_bwd(xb, dh1b, dhkb, dresb, branchb, ada0, bias0, ada1, bias1, kva, kvb, gain_mix, gain_kv):
        s1 = _ada_slices(ada1, bias1)
        kv_shift, kv_scale = _ada_slices(kva, kvb)
        dxa, dgain_mix, dsc1, dsh1 = _vjp(_modulate, (xb, gain_mix[1:2], s1[1], s1[0]), dh1b.astype(F32))
        dxb, dgain_kv, dkv_scale, dkv_shift = _vjp(_modulate, (xb, gain_kv, kv_scale, kv_shift), dhkb.astype(F32))
        dx = dresb + dxa + dxb
        g2 = _ada_slices(ada0, bias0)[5]
        return (dx, dx * g2, dgain_mix, dsc1, dsh1, dgain_kv, dkv_scale, dkv_shift,
                jnp.sum(dx * branchb, axis=0, keepdims=True))

    (dx2, dmlp0, dgain_mix1, dsc1_1, dsh1_1, dgain_kv, dkv_scale, dkv_shift, dg2_0) = rowwise(
        "x2_bwd", x2_bwd, [x2, dh1_1, dh_kv, dx3, mlp0],
        [ada_raw[0], b_ada[0:1], ada_raw[1], b_ada[1:2], kvada_raw, kv_bias, norm_mix_gain, kv_gain],
        [(D, F32), (D, BF16)], [vec] * 7, after=token_a)

    dh2_0, gw_mlp_in0, gw_mlp_out0 = mlp_backward("0", dmlp0, act0, u0, h2_0, 0)
    scat_b1, token_b1 = scatter_direct_start("b1", [gw_mlp_in0, gw_mlp_out0])
    dx1, dmix0, dgain_mlp0, dsc2_0, dsh2_0, dg1_0 = rowwise(
        "mod_mlp0_bwd", mod_backward(0, (3, 4), 2), [x1, dh2_0, dx2, mix0], [ada_raw[0], b_ada[0:1], norm_mlp_gain],
        [(D, F32), (D, BF16)], [vec] * 4, after=token_b1)
    dmixin0 = mm_nt("ret_out_dx", dmix0, W_ret_out, (BF16,))[0]
    gw_ret_out = mm_tn("ret_out_dw", mixin0, dmix0, 1, BF16).reshape(N_DEV, -1, D)
    scat_b2, token_b = scatter_direct_start("b2", [gw_ret_out])

    def ret_gate_bwd(db, yb, gb, gain):
        return _vjp(ret_gate, (yb, gb.astype(F32), gain), db.astype(F32))

    dy_ret, dgate, dret_gain = rowwise("ret_gate_bwd", ret_gate_bwd, [dmixin0, y_ret, (proj, 2 * D, 2)], [ret_gain],
                                       [(2 * D, BF16), (2 * D, BF16)], [(1, 2 * D)], after=token_b)
    dy_h = (dy_ret, RET_V, 0)
    dq_rot = retention("ret_dq", dy_h, v_ret, (k_rot, RET_QK, 0), reverse=False)
    dk_rot = retention("ret_dk", v_ret, dy_h, (q_rot, RET_QK, 0), reverse=True)
    dv_ret = retention("ret_dv", (k_rot, RET_QK, 0), (q_rot, RET_QK, 0), dy_h, reverse=True, out_dtype=BF16)

    def rope_bwd(dqb, dkb, dvb, dgb, cs, sn):
        dq = _rotate(dqb, cs, sn, RET_HEADS, -1.0)
        dk = _rotate(dkb, cs, sn, RET_HEADS, -1.0) * (RET_QK ** -0.5)
        return jnp.concatenate([dq, dk, dvb.astype(F32), dgb.astype(F32)], axis=1)

    dproj = rowwise("rope_bwd", rope_bwd, [dq_rot, dk_rot, dv_ret, dgate, cos, sin], [], [(6 * D, BF16)])[0]
    gw_ret_in = mm_tn("ret_proj_dw", h1_0, dproj, N_DEV, BF16)
    scat_c, token_c = scatter_start("c", [gw_ret_in])
    dh1_0 = mm_nt("ret_proj_dx", dproj, W_ret_in, (BF16,), after=token_c)[0]

    def x0_bwd(xb, dhb, dresb, ada, bias, gain):
        s = _ada_slices(ada, bias)
        dx, dgain, dsc, dsh = _vjp(_modulate, (xb, gain[0:1], s[1], s[0]), dhb.astype(F32))
        return dx + dresb, dgain, dsc, dsh

    grad_x, dgain_mix0, dsc1_0, dsh1_0 = rowwise("x0_bwd", x0_bwd, [x0, dh1_0, dx1],
                                                 [ada_raw[0], b_ada[0:1], norm_mix_gain], [(D, F32)], [vec] * 3)

    small = jnp.concatenate([
        dsh1_0, dsc1_0, dg1_0, dsh2_0, dsc2_0, dg2_0,
        dsh1_1, dsc1_1, dg1_1, dsh2_1, dsc2_1, dg2_1,
        dkv_shift, dkv_scale,
        dgain_mix0, dgain_mix1, dgain_mlp0, dgain_mlp1, dgain_kv,
        dret_gain,
        dq_gain.reshape(FOX_HEADS, FOX_DH).sum(axis=0).reshape(1, FOX_DH),
        dk_gain.reshape(FOX_HEADS, FOX_DH).sum(axis=0).reshape(1, FOX_DH),
        dfb,
        loss_acc,
    ], axis=1)
    small_all = all_gather("gather_small", [small.reshape(-1, LANES)])[0].reshape(N_DEV, 1, -1)
    loss = jnp.sum(small_all[:, 0, -1])
    scat_c, token_c = scatter_relay("c", scat_c, grad_x, start_after=(small_all,))
    o_ada = 14 * D
    d_ada = small_all[:, 0, :o_ada]
    d_cat = jnp.concatenate([
        lax.dynamic_slice_in_dim(d_ada[:, 0:6 * D], me * n_ada, n_ada, axis=1),
        lax.dynamic_slice_in_dim(d_ada[:, 6 * D:12 * D], me * n_ada, n_ada, axis=1),
        lax.dynamic_slice_in_dim(d_ada[:, 12 * D:14 * D], me * n_kvada, n_kvada, axis=1)], axis=1)
    gw_ada_cat = mm_tn("ada_dw", c_act, d_cat, 1, F32, after=token_c)[0]

    results = {}

    def update(name, parts, w, m, v, layers=1):
        per_layer = parts if layers > 1 else [parts]
        shape = w.shape
        C = shape[-1]
        R = int(np.prod(shape)) // (layers * C)
        per_layer = [p if isinstance(p, list) else [(p, p.shape[0])] for p in per_layer]
        per_layer = [[(a.reshape(a.shape[0], R, C), n) for a, n in p] for p in per_layer]
        outs = adamw("adamw_" + name, per_layer, w.reshape(layers, R, C), m.reshape(layers, R, C), v.reshape(layers, R, C))
        results[name] = tuple(t.reshape(shape) for t in outs)

    def small_parts(lo, width):
        return small_all[:, :, lo:lo + width]

    update("norm_mix_gain", jnp.concatenate([small_parts(o_ada, D), small_parts(o_ada + D, D)], axis=1),
           norm_mix_gain, m_norm_mix_gain, v_norm_mix_gain)
    update("norm_mlp_gain", jnp.concatenate([small_parts(o_ada + 2 * D, D), small_parts(o_ada + 3 * D, D)], axis=1),
           norm_mlp_gain, m_norm_mlp_gain, v_norm_mlp_gain)
    update("w_ada", [gw_ada_cat[None, :, :n_ada], gw_ada_cat[None, :, n_ada:2 * n_ada]], w_ada, m_w_ada, v_w_ada, layers=2)
    update("b_ada", jnp.concatenate([small_parts(0, 6 * D), small_parts(6 * D, 6 * D)], axis=1), b_ada, m_b_ada, v_b_ada)
    o_ret = o_ada + 5 * D
    n_rg = ret_norm_gain.shape[2]
    ret_gain_parts = small_parts(o_ret, 2 * D).reshape(N_DEV, RET_HEADS, RET_V)
    ret_gain_parts = lax.dynamic_slice_in_dim(ret_gain_parts, me * n_rg, n_rg, axis=2)
    update("ret_norm_gain", ret_gain_parts, ret_norm_gain, m_ret_norm_gain, v_ret_norm_gain)
    update("kv_norm_gain", small_parts(o_ada + 4 * D, D), kv_norm_gain, m_kv_norm_gain, v_kv_norm_gain)
    update("kv_w_ada", gw_ada_cat[None, :, 2 * n_ada:], kv_w_ada, m_kv_w_ada, v_kv_w_ada)
    update("kv_b_ada", small_parts(12 * D, 2 * D), kv_b_ada, m_kv_b_ada, v_kv_b_ada)
    o_q = o_ret + 2 * D
    update("forget_bias", small_parts(o_q + 2 * FOX_DH, FOX_HEADS), forget_bias, m_forget_bias, v_forget_bias)
    update("k_norm_gain", small_parts(o_q + FOX_DH, FOX_DH), k_norm_gain, m_k_norm_gain, v_k_norm_gain)
    update("q_norm_gain", small_parts(o_q, FOX_DH), q_norm_gain, m_q_norm_gain, v_q_norm_gain)

    r_mlp_in1, r_mlp_out1 = scatter_direct_finish("a1", scat_a1, results["q_norm_gain"][1])
    r_fox_out, r_fox_in, r_kv = scatter_direct_finish("a2", scat_a2, r_mlp_in1[0][0])
    r_mlp_in0, r_mlp_out0 = scatter_direct_finish("b1", scat_b1, r_kv[0][0])
    r_ret_out = scatter_direct_finish("b2", scat_b2, r_mlp_in0[0][0])[0]
    update("kv_w", r_kv, kv_w, m_kv_w, v_kv_w)
    update("fox_w_in", r_fox_in, fox_w_in, m_fox_w_in, v_fox_w_in)
    update("fox_w_out", r_fox_out, fox_w_out, m_fox_w_out, v_fox_w_out)
    update("ret_w_out", r_ret_out, ret_w_out, m_ret_w_out, v_ret_w_out)
    update("w_mlp_in", [r_mlp_in0, r_mlp_in1], w_mlp_in, m_w_mlp_in, v_w_mlp_in, layers=2)
    update("w_mlp_out", [r_mlp_out0, r_mlp_out1], w_mlp_out, m_w_mlp_out, v_w_mlp_out, layers=2)
    r_ret_in = scatter_finish("c", scat_c, results["w_mlp_out"][1])[0]
    update("ret_w_in", r_ret_in, ret_w_in, m_ret_w_in, v_ret_w_in)

    order = ["norm_mix_gain", "norm_mlp_gain", "w_ada", "b_ada", "w_mlp_in", "w_mlp_out", "ret_w_in", "ret_norm_gain",
             "ret_w_out", "kv_norm_gain", "kv_w_ada", "kv_b_ada", "kv_w", "forget_bias", "k_norm_gain", "fox_w_in",
             "q_norm_gain", "fox_w_out"]
    out = [loss, grad_x.reshape(x.shape)]
    for slot in range(4):
        out += [results[n][slot] for n in order]
    return tuple(out)
```

```python
import functools
import math

import numpy as np
import jax
import jax.numpy as jnp
from jax import lax
from jax.experimental import pallas as pl
from jax.experimental.pallas import tpu as pltpu

F32 = jnp.float32
BF16 = jnp.bfloat16

N_DEV = 8
D_MODEL = 1024
RET_HEADS = 4
RET_QK = D_MODEL // RET_HEADS
RET_V = 2 * D_MODEL // RET_HEADS
RET_CHUNK = 128
ROPE_BASE = 10000.0
FOX_HEADS = 16
FOX_DH = D_MODEL // FOX_HEADS
EPS = 1e-6
LANES = 128
KV_PAD = 2 * D_MODEL + LANES

ADAM_LR = 0.001
ADAM_B1 = 0.9
ADAM_B2 = 0.999
ADAM_EPS = 1e-08
ADAM_WD = 0.01
ADAM_STEP = 10

VMEM_LIMIT_BYTES = 56 * 1024 * 1024


def _params(sem=None):
    return pltpu.CompilerParams(dimension_semantics=sem, vmem_limit_bytes=VMEM_LIMIT_BYTES)


def _me():
    return lax.axis_index("x"), lax.axis_index("y"), lax.axis_index("c")


def _peer(k):
    x, y, c = _me()
    return (1 - x if k & 4 else x, 1 - y if k & 2 else y, 1 - c if k & 1 else c)


def _peer_index(k):
    px, py, pc = _peer(k)
    return 4 * px + 2 * py + pc


def _exchange(name, xs, scatter):
    n = len(xs)

    def body(*refs):
        x_refs, o_refs = refs[:n], refs[n:2 * n]
        send_sems, recv_sems, local_sems = refs[2 * n:]
        x, y, c = _me()
        me = 4 * x + 2 * y + c
        local = []
        for i in range(n):
            src = x_refs[i].at[me] if scatter else x_refs[i]
            cp = pltpu.make_async_copy(src, o_refs[i].at[me], local_sems.at[i])
            cp.start()
            local.append(cp)
        remote = []
        for k in range(1, N_DEV):
            for i in range(n):
                src = x_refs[i].at[_peer_index(k)] if scatter else x_refs[i]
                cp = pltpu.make_async_remote_copy(
                    src_ref=src, dst_ref=o_refs[i].at[me],
                    send_sem=send_sems.at[(k - 1) * n + i], recv_sem=recv_sems.at[(k - 1) * n + i],
                    device_id=_peer(k), device_id_type=pl.DeviceIdType.MESH)
                cp.start()
                remote.append(cp)
        for cp in remote:
            cp.wait()
        for cp in local:
            cp.wait()

    out_shape = [jax.ShapeDtypeStruct(x.shape if scatter else (N_DEV,) + x.shape, x.dtype) for x in xs]
    any_spec = pl.BlockSpec(memory_space=pl.ANY)
    return pl.pallas_call(
        body, name=name, out_shape=out_shape,
        in_specs=[any_spec] * n, out_specs=[any_spec] * n,
        scratch_shapes=[pltpu.SemaphoreType.DMA(((N_DEV - 1) * n,)),
                        pltpu.SemaphoreType.DMA(((N_DEV - 1) * n,)),
                        pltpu.SemaphoreType.DMA((n,))],
    )(*xs)


def all_gather(name, xs):
    return _exchange(name, xs, scatter=False)


def all_to_all(name, xs):
    return _exchange(name, xs, scatter=True)


_HBM = pl.BlockSpec(memory_space=pltpu.HBM)
_SEM = pl.BlockSpec(memory_space=pltpu.SEMAPHORE)
_ANY = pl.BlockSpec(memory_space=pl.ANY)
_EFFECT = pltpu.SideEffectType.DATAFLOW_SIDE_EFFECTING

SIBLING = 1
CHIP_PEERS = (2, 4, 6)


def _my_index():
    x, y, c = _me()
    return 4 * x + 2 * y + c


def plan_gather(x, land, me):
    return [(x, land.at[me], k) for k in (SIBLING,) + CHIP_PEERS]


def plan_gather_direct(x, land, me):
    return [(x, land.at[me], k) for k in range(1, N_DEV)]


def plan_scatter_direct(x, land, me):
    return [(x.at[me ^ k], land.at[me], k) for k in range(1, N_DEV)]


def plan_forward(x, land, me):
    return [(x.at[me ^ k], land.at[me ^ k], SIBLING) for k in CHIP_PEERS]


def plan_to_sibling(x, land, me):
    return [(x.at[me ^ SIBLING ^ (2 * q)], land.at[q], SIBLING) for q in range(4)]


def plan_to_owners(x, land, me):
    return [(x.at[q], land.at[q - 1], 2 * q) for q in (1, 2, 3)]


def own_gather(x, land, me):
    return x, land.at[me]


def own_scatter(x, land, me):
    return x.at[me], land.at[me]


OWN_COPY = {plan_gather: own_gather, plan_gather_direct: own_gather, plan_scatter_direct: own_scatter}
N_COPIES = {plan_gather: 4, plan_gather_direct: 7, plan_scatter_direct: 7, plan_forward: 3, plan_to_sibling: 4,
            plan_to_owners: 3}


def _plans(plan, n):
    return list(plan) if isinstance(plan, (list, tuple)) else [plan] * n


def _own_copies(plan, x_refs, land_refs, own_sems):
    me = _my_index()
    plans = _plans(plan, len(land_refs))
    with_own = [i for i, p in enumerate(plans) if p in OWN_COPY]
    return [pltpu.make_async_copy(*OWN_COPY[plans[i]](x_refs[i], land_refs[i], me), own_sems[s])
            for s, i in enumerate(with_own)]


def _plan_copies(plan, x_refs, land_refs, send_sems, recv_sems):
    me = _my_index()
    plans = _plans(plan, len(land_refs))
    return [pltpu.make_async_remote_copy(src_ref=src, dst_ref=dst, send_sem=send_sems[i].at[s], recv_sem=recv_sems[i].at[s],
                                         device_id=_peer(k), device_id_type=pl.DeviceIdType.MESH)
            for i in range(len(land_refs)) for s, (src, dst, k) in enumerate(plans[i](x_refs[i], land_refs[i], me))]


def exchange_now(name, xs, lands, plan):
    n = len(lands)
    n_x = 0 if xs is None else n

    def body(*refs):
        land_in, land_out = refs[n_x:n_x + n], refs[n_x + n:n_x + 2 * n]
        x_refs = land_in if xs is None else refs[:n]
        sems = refs[n_x + 2 * n:]
        copies = _own_copies(plan, x_refs, land_out, sems[2 * n:]) + _plan_copies(plan, x_refs, land_out, sems[:n], sems[n:2 * n])
        for cp in copies:
            cp.start()
        for cp in copies:
            cp.wait()

    return pl.pallas_call(
        body, name=name, out_shape=[jax.ShapeDtypeStruct(a.shape, a.dtype) for a in lands],
        in_specs=[_ANY] * (n_x + n), out_specs=[_ANY] * n,
        input_output_aliases={n_x + i: i for i in range(n)},
        scratch_shapes=[pltpu.SemaphoreType.DMA((N_COPIES[p],)) for p in _plans(plan, n)] * 2
        + [pltpu.SemaphoreType.DMA(())] * sum(p in OWN_COPY for p in _plans(plan, n)),
    )(*([] if xs is None else xs), *lands)


def exchange_start(name, xs, lands, plan, after=()):
    n, m = len(lands), len(after)
    n_x = 0 if xs is None else n
    arrays = ([] if xs is None else list(xs)) + list(lands)
    n_a = len(arrays)
    n_own = sum(p in OWN_COPY for p in _plans(plan, n))
    n_s = 2 * n + n_own

    def body(*refs):
        land_refs = refs[n_x:n_a]
        x_refs = land_refs if xs is None else refs[:n]
        sems = refs[n_a + m:n_a + m + n_s]
        token = refs[-1]
        for cp in _own_copies(plan, x_refs, land_refs, sems[2 * n:]):
            cp.start()
        for cp in _plan_copies(plan, x_refs, land_refs, sems[:n], sems[n:2 * n]):
            cp.start()
        token[...] = jnp.zeros_like(token)

    sems = [pltpu.SemaphoreType.DMA((N_COPIES[p],)) for p in _plans(plan, n)] * 2 + [pltpu.SemaphoreType.DMA(())] * n_own
    res = pl.pallas_call(
        body, name=name,
        out_shape=sems + [pltpu.HBM(a.shape, a.dtype) for a in arrays] + [jax.ShapeDtypeStruct((8, LANES), F32)],
        in_specs=[_HBM] * n_a + [_ANY] * m,
        out_specs=[_SEM] * n_s + [_HBM] * n_a + [pl.BlockSpec(memory_space=pltpu.VMEM)],
        input_output_aliases={i: n_s + i for i in range(n_a)},
        compiler_params=pltpu.CompilerParams(has_side_effects=_EFFECT),
    )(*[pltpu.with_memory_space_constraint(a, pltpu.HBM) for a in arrays], *after)
    thru = res[n_s:n_s + n_a]
    own = iter(res[2 * n:n_s])
    handles = [(res[i], res[n + i], None if xs is None else thru[i], thru[n_x + i],
                next(own) if p in OWN_COPY else None) for i, p in enumerate(_plans(plan, n))]
    return handles, res[-1]


def exchange_wait(name, handles, plan, after):
    n = len(handles)
    in_place = handles[0][2] is None
    arrays = ([] if in_place else [h[2] for h in handles]) + [h[3] for h in handles]
    n_a = len(arrays)
    own_sems = [h[4] for h in handles if h[4] is not None]
    n_s = 2 * n + len(own_sems)

    def body(*refs):
        land_refs = refs[n_a - n:n_a]
        x_refs = land_refs if in_place else refs[:n]
        sems = refs[n_a:n_a + n_s]
        for cp in _own_copies(plan, x_refs, land_refs, sems[2 * n:]):
            cp.wait()
        for cp in _plan_copies(plan, x_refs, land_refs, sems[:n], sems[n:2 * n]):
            cp.wait_send()
            cp.wait_recv()

    res = pl.pallas_call(
        body, name=name,
        out_shape=[pltpu.HBM(a.shape, a.dtype) for a in arrays],
        in_specs=[_HBM] * n_a + [_SEM] * n_s + [_ANY],
        out_specs=[_HBM] * n_a,
        input_output_aliases={i: i for i in range(n_a)},
        compiler_params=pltpu.CompilerParams(has_side_effects=_EFFECT),
    )(*arrays, *[h[0] for h in handles], *[h[1] for h in handles], *own_sems, after)
    return (None if in_place else res[:n]), res[n_a - n:]


def chip_sum(name, gw, from_sibling):
    _, R, C = gw.shape
    tr = _row_tile(R, 512)
    me = _my_index().astype(jnp.int32).reshape(1)

    def body(me_ref, g_ref, s_ref, o_ref):
        o_ref[...] = (g_ref[...].astype(F32) + s_ref[...].astype(F32)).astype(o_ref.dtype)

    slot = pl.BlockSpec((None, tr, C), lambda q, i, me_ref: (q, i, 0))
    return pl.pallas_call(
        body, name=name, out_shape=jax.ShapeDtypeStruct((4, R, C), BF16),
        grid_spec=pltpu.PrefetchScalarGridSpec(
            num_scalar_prefetch=1, grid=(4, R // tr),
            in_specs=[pl.BlockSpec((None, tr, C), lambda q, i, me_ref: (me_ref[0] ^ (2 * q), i, 0)), slot],
            out_specs=slot),
        compiler_params=_params(("arbitrary", "arbitrary")),
    )(me, gw, from_sibling)


def _tile(n, cap):
    best = None
    for t in range(LANES, min(n, cap) + 1, LANES):
        if n % t == 0:
            best = t
    if best is None or (best < 256 and n <= 2304):
        return n
    return best


def _row_tile(m, cap):
    if m <= cap:
        return m
    t = cap
    while m % t:
        t //= 2
    return t


def _after_spec(after):
    return [] if after is None else [pl.BlockSpec(memory_space=pl.ANY)]


def _after_arg(after):
    return [] if after is None else [after]


def _mm_call(name, dims, grid, a_spec, b_spec, o_spec, o_shape, tile, a, b, out_dtypes, epilogue, extra, after):
    nk = grid[2]
    n_x, n_o = len(extra), len(out_dtypes)

    def body(a_ref, b_ref, *refs):
        x_refs, o_refs = refs[:n_x], refs[len(refs) - n_s - n_o:len(refs) - n_s]
        part = lax.dot_general(a_ref[...].astype(BF16), b_ref[...].astype(BF16), (dims, ((), ())),
                               preferred_element_type=F32)

        def finish(acc):
            vals = (acc,) if epilogue is None else epilogue(acc, *[x[...] for x in x_refs])
            for o_ref, val in zip(o_refs, vals):
                o_ref[...] = val.astype(o_ref.dtype)

        if nk == 1:
            finish(part)
        else:
            acc_ref = refs[-1]
            k = pl.program_id(2)

            @pl.when(k == 0)
            def _():
                acc_ref[...] = part

            @pl.when(jnp.logical_and(k > 0, k < nk - 1))
            def _():
                acc_ref[...] += part

            @pl.when(k == nk - 1)
            def _():
                finish(acc_ref[...] + part)

    n_s = 0 if nk == 1 else 1
    return pl.pallas_call(
        body, name=name, out_shape=[jax.ShapeDtypeStruct(o_shape, dt) for dt in out_dtypes], grid=grid,
        in_specs=[a_spec, b_spec] + [o_spec] * n_x + _after_spec(after), out_specs=[o_spec] * n_o,
        scratch_shapes=[pltpu.VMEM(tile, F32)] * n_s,
        compiler_params=_params(("parallel", "parallel", "arbitrary")),
    )(a, b, *extra, *_after_arg(after))


def mm_nn(name, a, w, out_dtypes=(F32,), epilogue=None, extra=(), after=None):
    M, K = a.shape
    G, _, n = w.shape
    tn = _tile(n, 1024)
    tm = _row_tile(M, 512 if tn > 1024 else 2048)
    tk = _row_tile(K, 1024)
    r = n // tn
    return _mm_call(
        name, ((1,), (0,)), (M // tm, G * r, K // tk),
        pl.BlockSpec((tm, tk), lambda i, j, k: (i, k)),
        pl.BlockSpec((None, tk, tn), lambda i, j, k: (j // r, k, j % r)),
        pl.BlockSpec((tm, tn), lambda i, j, k: (i, j)), (M, G * n), (tm, tn),
        a, w, out_dtypes, epilogue, extra, after)


def mm_nt(name, dy, w, out_dtypes=(F32,), epilogue=None, extra=(), after=None):
    M, N = dy.shape
    G, K, n = w.shape
    tn = _tile(n, 1024)
    tm = _row_tile(M, 512 if tn > 1024 else 2048)
    tk = _row_tile(K, 1024)
    r = n // tn
    return _mm_call(
        name, ((1,), (1,)), (M // tm, K // tk, G * r),
        pl.BlockSpec((tm, tn), lambda i, j, k: (i, k)),
        pl.BlockSpec((None, tk, tn), lambda i, j, k: (k // r, j, k % r)),
        pl.BlockSpec((tm, tk), lambda i, j, k: (i, j)), (M, K), (tm, tk),
        dy, w, out_dtypes, epilogue, extra, after)


def mm_tn(name, a, dy, G, out_dtype=F32, after=None):
    M, K = a.shape
    n = dy.shape[1] // G
    tn = _tile(n, 1024)
    tk = _row_tile(K, 512 if tn > 1024 else 1024)
    tm = _row_tile(M, 2048)
    r = n // tn
    return _mm_call(
        name, ((0,), (0,)), (K // tk, G * r, M // tm),
        pl.BlockSpec((tm, tk), lambda i, j, k: (k, i)),
        pl.BlockSpec((tm, tn), lambda i, j, k: (k, j)),
        pl.BlockSpec((None, tk, tn), lambda i, j, k: (j // r, i, j % r)), (G, K, n), (tk, tn),
        a, dy, (out_dtype,), None, (), after)[0]


ROW_BLOCK_BYTES = 12 * 1024 * 1024


def rowwise(name, fn, rows, vecs, outs, accs=(), after=None):
    rows = [r if isinstance(r, tuple) else (r, r.shape[1], 0) for r in rows]
    rows = [r if len(r) == 4 else r + (None,) for r in rows]
    n_fn = len(rows) + len(vecs)
    vecs = list(vecs) + _after_arg(after)
    S = rows[0][0].shape[-2]
    row_bytes = sum(w * a.dtype.itemsize for a, w, _, _ in rows) + sum(w * jnp.dtype(dt).itemsize for w, dt in outs)
    tm = _row_tile(S, 1024)
    while tm > 256 and tm * row_bytes > ROW_BLOCK_BYTES:
        tm //= 2
    n_r, n_v, n_o, n_a = len(rows), len(vecs), len(outs), len(accs)

    def body(*refs):
        ins = [ref[...] for ref in refs[:n_r + n_v]]
        o_refs = refs[n_r + n_v:n_r + n_v + n_o]
        a_refs = refs[n_r + n_v + n_o:]
        res = fn(*ins[:n_fn])
        res = res if isinstance(res, (tuple, list)) else (res,)
        for ref, val in zip(o_refs, res[:n_o]):
            ref[...] = val.astype(ref.dtype)
        if n_a:
            @pl.when(pl.program_id(0) == 0)
            def _():
                for ref in a_refs:
                    ref[...] = jnp.zeros_like(ref)
            for ref, val in zip(a_refs, res[n_o:]):
                ref[...] += val

    in_specs = [pl.BlockSpec((tm, w), functools.partial(lambda cb, i: (i, cb), cb)) if slab is None else
                pl.BlockSpec((None, tm, w), functools.partial(lambda cb, slab, i: (slab, i, cb), cb, slab))
                for _, w, cb, slab in rows]
    in_specs += [pl.BlockSpec(v.shape, lambda i: (0, 0)) for v in vecs]
    out_specs = [pl.BlockSpec((tm, w), lambda i: (i, 0)) for w, _ in outs]
    out_specs += [pl.BlockSpec(a, lambda i: (0, 0)) for a in accs]
    out_shape = [jax.ShapeDtypeStruct((S, w), dt) for w, dt in outs]
    out_shape += [jax.ShapeDtypeStruct(a, F32) for a in accs]
    res = pl.pallas_call(
        body, name=name, out_shape=out_shape, grid=(S // tm,),
        in_specs=in_specs, out_specs=out_specs,
        compiler_params=_params(("arbitrary",)),
    )(*[r[0] for r in rows], *vecs)
    return res


def _rms(x):
    return x * lax.rsqrt(jnp.mean(x * x, axis=-1, keepdims=True) + EPS)


def _modulate(x, gain, scale, shift):
    return _rms(x) * gain * (1.0 + scale) + shift


def _ada_slices(ada_raw, bias):
    ada = ada_raw + bias
    return [ada[:, i * D_MODEL:(i + 1) * D_MODEL] for i in range(ada.shape[1] // D_MODEL)]


def _norm_wide_heads(y, gain, heads):
    w = y.shape[1] // heads
    return jnp.concatenate([_rms(y[:, h * w:(h + 1) * w]) * gain[:, h * w:(h + 1) * w] for h in range(heads)], axis=1)


def _norm_fox_heads(x, gain):
    outs = []
    for p in range(x.shape[1] // LANES):
        blk = x[:, p * LANES:(p + 1) * LANES]
        low = lax.broadcasted_iota(jnp.int32, blk.shape, 1) < FOX_DH
        sq = blk * blk
        ss_low = jnp.sum(jnp.where(low, sq, 0.0), axis=1, keepdims=True)
        ss_high = jnp.sum(jnp.where(low, 0.0, sq), axis=1, keepdims=True)
        outs.append(blk * lax.rsqrt(jnp.where(low, ss_low, ss_high) * (1.0 / FOX_DH) + EPS))
    return jnp.concatenate(outs, axis=1) * gain


def _silu(x):
    return x * jax.nn.sigmoid(x)


def _log_sigmoid(z):
    return -(jnp.maximum(-z, 0.0) + jnp.log(1.0 + jnp.exp(-jnp.abs(z))))


def _rotate(x, cos, sin, heads, sign):
    w = x.shape[1] // heads
    half = w // 2
    outs = []
    for h in range(heads):
        x1 = x[:, h * w:h * w + half]
        x2 = x[:, h * w + half:(h + 1) * w]
        outs += [x1 * cos - sign * x2 * sin, sign * x1 * sin + x2 * cos]
    return jnp.concatenate(outs, axis=1)


def _vjp(fn, primals, ct):
    return jax.vjp(fn, *primals)[1](ct)


_LOG_GAMMAS = [float(np.log(np.float32(1.0) - np.float32(2.0) ** np.float32(-5.0 - h))) for h in range(RET_HEADS)]


RET_ROWS = 1024


def retention(name, q, k, v, reverse, out_dtype=F32):
    (qa, dk, qo), (ka, _, ko), (va, dv, vo) = q, k, v
    S = qa.shape[0]
    C = RET_CHUNK
    rows = _row_tile(S, RET_ROWS)
    nb = S // rows

    def body(q_ref, k_ref, v_ref, o_ref, state):
        h = pl.program_id(0)

        @pl.when(pl.program_id(1) == 0)
        def _():
            state[...] = jnp.zeros_like(state)

        log_g = jnp.float32(_LOG_GAMMAS[RET_HEADS - 1])
        for i in range(RET_HEADS - 2, -1, -1):
            log_g = jnp.where(h == i, jnp.float32(_LOG_GAMMAS[i]), log_g)
        row = lax.broadcasted_iota(jnp.int32, (C, C), 0)
        col = lax.broadcasted_iota(jnp.int32, (C, C), 1)
        rel = (col - row if reverse else row - col).astype(F32)
        decay = jnp.where(rel >= 0, jnp.exp(log_g * jnp.maximum(rel, 0.0)), 0.0)
        j = lax.broadcasted_iota(jnp.int32, (C, 1), 0).astype(F32)
        q_decay = jnp.exp(log_g * (C - j if reverse else j + 1.0))
        k_decay = jnp.exp(log_g * (j if reverse else C - 1.0 - j))
        chunk_decay = jnp.exp(jnp.full((1, 1), log_g * C, F32))

        chunks = range(rows // C)
        for ci in (reversed(chunks) if reverse else chunks):
            rs = slice(ci * C, (ci + 1) * C)
            qc = q_ref[rs, :].astype(BF16)
            kf = k_ref[rs, :].astype(F32)
            vc = v_ref[rs, :].astype(BF16)
            scores = lax.dot_general(qc, kf.astype(BF16), (((1,), (1,)), ((), ())), preferred_element_type=F32) * decay
            intra = jnp.dot(scores.astype(BF16), vc, preferred_element_type=F32)
            cross = jnp.dot(qc, state[...].astype(BF16), preferred_element_type=F32) * q_decay
            o_ref[rs, :] = (intra + cross).astype(o_ref.dtype)
            upd = lax.dot_general((kf * k_decay).astype(BF16), vc, (((0,), (0,)), ((), ())), preferred_element_type=F32)
            state[...] = state[...] * chunk_decay + upd

    def block(i):
        return nb - 1 - i if reverse else i

    return pl.pallas_call(
        body, name=name, out_shape=jax.ShapeDtypeStruct((S, RET_HEADS * dv), out_dtype),
        grid=(RET_HEADS, nb),
        in_specs=[pl.BlockSpec((rows, dk), lambda h, i: (block(i), qo + h)),
                  pl.BlockSpec((rows, dk), lambda h, i: (block(i), ko + h)),
                  pl.BlockSpec((rows, dv), lambda h, i: (block(i), vo + h))],
        out_specs=pl.BlockSpec((rows, dv), lambda h, i: (block(i), h)),
        scratch_shapes=[pltpu.VMEM((dk, dv), F32)],
        compiler_params=_params(("parallel", "arbitrary")),
    )(qa, ka, va)


FOX_T = 256
N_PAIR = FOX_HEADS // 2
FOX_SCALE = FOX_DH ** -0.5


def _fox_heads(q2):
    low = lax.broadcasted_iota(jnp.int32, (1, LANES), 1) < FOX_DH
    return [(mask, jnp.where(mask, q2 * FOX_SCALE, 0.0).astype(BF16)) for mask in (low, jnp.logical_not(low))]


def _fox_parts(j, t):
    return ([(0, j * t, False)] if j else []) + [(j * t, (j + 1) * t, True)]


def _fox_scores(qa, k_ref, ft_ref, head, lo, hi, diagonal):
    k_blk = k_ref[lo:hi, :].astype(BF16)
    s = lax.dot_general(qa, k_blk, (((1,), (1,)), ((), ())), preferred_element_type=F32) - ft_ref[pl.ds(head, 1), lo:hi]
    if diagonal:
        n = hi - lo
        s = jnp.where(lax.broadcasted_iota(jnp.int32, (n, n), 1) <= lax.broadcasted_iota(jnp.int32, (n, n), 0), s, -jnp.inf)
    return s


def fox_forward(name, qn, kn, kvf, f_cum_t):
    S = qn.shape[0]
    t = _row_tile(S, FOX_T)
    v_block0 = D_MODEL // LANES

    def variant(j, pair, q_ref, k_ref, v_ref, ft_ref, y_ref, lse_ref):
        ys, lses = [], []
        for a, (mask, qa) in enumerate(_fox_heads(q_ref[...])):
            parts = [(lo, hi, _fox_scores(qa, k_ref, ft_ref, 2 * pair + a, lo, hi, dg)) for lo, hi, dg in _fox_parts(j, t)]
            m = functools.reduce(jnp.maximum, [jnp.max(s, axis=1, keepdims=True) for _, _, s in parts])
            l, acc = 0.0, 0.0
            for lo, hi, s in parts:
                e = jnp.exp(s - m)
                l = l + jnp.sum(e, axis=1, keepdims=True)
                acc = acc + jnp.dot(e.astype(BF16), v_ref[lo:hi, :].astype(BF16), preferred_element_type=F32)
            ys.append(acc / l)
            lses.append(m + jnp.log(l))
        low = lax.broadcasted_iota(jnp.int32, (1, LANES), 1) < FOX_DH
        y_ref[...] = jnp.where(low, ys[0], ys[1])
        lse_ref[...] = jnp.where(low, lses[0], lses[1])

    def body(*refs):
        pair, i = pl.program_id(0), pl.program_id(1)
        for j in range(S // t):
            pl.when(i == j)(functools.partial(variant, j, pair, *refs))

    return pl.pallas_call(
        body, name=name,
        out_shape=[jax.ShapeDtypeStruct((S, D_MODEL), F32), jax.ShapeDtypeStruct((S, D_MODEL), F32)],
        grid=(N_PAIR, S // t),
        in_specs=[pl.BlockSpec((t, LANES), lambda p, i: (i, p)),
                  pl.BlockSpec((S, LANES), lambda p, i: (0, p)),
                  pl.BlockSpec((S, LANES), lambda p, i: (0, v_block0 + p)),
                  pl.BlockSpec((LANES, S), lambda p, i: (0, 0))],
        out_specs=[pl.BlockSpec((t, LANES), lambda p, i: (i, p)),
                   pl.BlockSpec((t, LANES), lambda p, i: (i, p))],
        compiler_params=_params(("parallel", "arbitrary")),
    )(qn, kn, kvf, f_cum_t)


def fox_backward(name, qn, kn, kvf, f_cum_t, y, dy, lse):
    S = qn.shape[0]
    t = _row_tile(S, FOX_T)
    v_block0 = D_MODEL // LANES

    def variant(j, pair, q_ref, k_ref, v_ref, ft_ref, y_ref, dy_ref, lse_ref, dq_ref, dk_ref, dv_ref, dfq_ref, dfk_ref,
                dv_acc):
        y2, dy2, lse2 = y_ref[...], dy_ref[...], lse_ref[...]
        lane = lax.broadcasted_iota(jnp.int32, (t, LANES), 1)
        dqs, dfq = [], jnp.zeros((t, LANES), F32)
        for a, (mask, qa) in enumerate(_fox_heads(q_ref[...].astype(F32))):
            lse_a = jnp.max(jnp.where(mask, lse2, -jnp.inf), axis=1, keepdims=True)
            dy_a = jnp.where(mask, dy2, 0.0)
            delta = jnp.sum(dy_a * y2, axis=1, keepdims=True)
            dy_b = dy_a.astype(BF16)
            dq, row_sum = 0.0, 0.0
            for lo, hi, dg in _fox_parts(j, t):
                p = jnp.exp(_fox_scores(qa, k_ref, ft_ref, 2 * pair + a, lo, hi, dg) - lse_a)
                dp = lax.dot_general(dy_b, v_ref[lo:hi, :].astype(BF16), (((1,), (1,)), ((), ())), preferred_element_type=F32)
                ds = p * (dp - delta)
                row_sum = row_sum + jnp.sum(ds, axis=1, keepdims=True)
                dfk_ref[pl.ds(a, 1), lo:hi] += -jnp.sum(ds, axis=0, keepdims=True)
                ds_b = ds.astype(BF16)
                dq = dq + jnp.dot(ds_b, k_ref[lo:hi, :].astype(BF16), preferred_element_type=F32)
                dk_ref[lo:hi, :] += lax.dot_general(ds_b, qa, (((0,), (0,)), ((), ())), preferred_element_type=F32)
                dv_acc[lo:hi, :] += lax.dot_general(p.astype(BF16), dy_b, (((0,), (0,)), ((), ())), preferred_element_type=F32)
            dqs.append(dq * FOX_SCALE)
            dfq = dfq + jnp.where(lane == 2 * pair + a, row_sum, 0.0)
        low = lax.broadcasted_iota(jnp.int32, (1, LANES), 1) < FOX_DH
        dq_ref[...] = jnp.where(low, dqs[0], dqs[1])
        dfq_ref[...] = dfq

    def body(*refs):
        pair, i = pl.program_id(0), pl.program_id(1)
        dk_ref, dv_ref, _, dfk_ref, dv_acc = refs[8:13]

        @pl.when(i == 0)
        def _():
            dk_ref[...] = jnp.zeros_like(dk_ref)
            dv_acc[...] = jnp.zeros_like(dv_acc)
            dfk_ref[...] = jnp.zeros_like(dfk_ref)

        for j in range(S // t):
            pl.when(i == j)(functools.partial(variant, j, pair, *refs))

        @pl.when(i == S // t - 1)
        def _():
            dv_ref[...] = dv_acc[...].astype(dv_ref.dtype)

    row_blk = pl.BlockSpec((t, LANES), lambda p, i: (i, p))
    col_blk = pl.BlockSpec((S, LANES), lambda p, i: (0, p))
    return pl.pallas_call(
        body, name=name,
        out_shape=[jax.ShapeDtypeStruct((S, D_MODEL), F32)] * 2
        + [jax.ShapeDtypeStruct((S, D_MODEL), BF16), jax.ShapeDtypeStruct((N_PAIR, S, LANES), F32),
           jax.ShapeDtypeStruct((N_PAIR, 8, S), F32)],
        grid=(N_PAIR, S // t),
        in_specs=[row_blk, col_blk,
                  pl.BlockSpec((S, LANES), lambda p, i: (0, v_block0 + p)),
                  pl.BlockSpec((LANES, S), lambda p, i: (0, 0)),
                  row_blk, row_blk, row_blk],
        out_specs=[row_blk, col_blk, col_blk,
                   pl.BlockSpec((None, t, LANES), lambda p, i: (p, i, 0)),
                   pl.BlockSpec((None, 8, S), lambda p, i: (p, 0, 0))],
        scratch_shapes=[pltpu.VMEM((S, LANES), F32)],
        compiler_params=_params(("parallel", "arbitrary")),
    )(qn, kn, kvf, f_cum_t, y, dy, lse)


def cumsum_rows(name, x, reverse):
    S = x.shape[0]
    C = LANES
    nc = S // C

    def body(x_ref, o_ref):
        row = lax.broadcasted_iota(jnp.int32, (C, C), 0)
        col = lax.broadcasted_iota(jnp.int32, (C, C), 1)
        tri = jnp.where(col >= row if reverse else col <= row, 1.0, 0.0).astype(F32)
        carry = jnp.zeros((1, LANES), F32)
        for i in (range(nc - 1, -1, -1) if reverse else range(nc)):
            blk = x_ref[i * C:(i + 1) * C, :]
            loc = jnp.dot(tri, blk, preferred_element_type=F32, precision=lax.Precision.HIGHEST)
            o_ref[i * C:(i + 1) * C, :] = loc + carry
            carry = carry + (loc[0:1, :] if reverse else loc[C - 1:C, :])

    return pl.pallas_call(body, name=name, out_shape=jax.ShapeDtypeStruct((S, LANES), F32),
                          compiler_params=_params())(x)


def adamw(name, parts, w, m, v):
    L, R, C = w.shape
    tr = _row_tile(R, 256)
    nr = R // tr
    counts = [len(p) for p in parts]

    def body(*refs):
        w_ref, m_ref, v_ref, g_out, d_out, m_out, v_out = refs[sum(counts):]
        for layer in range(L):
            p_refs = refs[sum(counts[:layer]):sum(counts[:layer + 1])]

            @pl.when(pl.program_id(0) == layer)
            def _(p_refs=p_refs, slots=[n for _, n in parts[layer]]):
                g = None
                for p_ref, n in zip(p_refs, slots):
                    for i in range(n):
                        g = p_ref[i].astype(F32) if g is None else g + p_ref[i].astype(F32)
                m2 = ADAM_B1 * m_ref[...] + (1.0 - ADAM_B1) * g
                v2 = ADAM_B2 * v_ref[...] + (1.0 - ADAM_B2) * jnp.square(g)
                m_hat = m2 / (1.0 - ADAM_B1 ** ADAM_STEP)
                v_hat = v2 / (1.0 - ADAM_B2 ** ADAM_STEP)
                g_out[...] = g
                d_out[...] = -ADAM_LR * (m_hat / (jnp.sqrt(v_hat) + ADAM_EPS) + ADAM_WD * w_ref[...])
                m_out[...] = m2
                v_out[...] = v2

    def part_spec(layer, n):
        return pl.BlockSpec((n, tr, C), lambda l, i: (0, jnp.where(l == layer, i, jnp.where(l < layer, 0, nr - 1)), 0))

    blk = pl.BlockSpec((None, tr, C), lambda l, i: (l, i, 0))
    return pl.pallas_call(
        body, name=name, out_shape=[jax.ShapeDtypeStruct((L, R, C), F32)] * 4, grid=(L, nr),
        in_specs=[part_spec(layer, n) for layer in range(L) for _, n in parts[layer]] + [blk, blk, blk],
        out_specs=[blk] * 4, compiler_params=_params(("arbitrary", "arbitrary")),
    )(*[a for layer in parts for a, _ in layer], w, m, v)


def kernel(x, c, positions, norm_mix_gain, norm_mlp_gain, w_ada, b_ada, w_mlp_in, w_mlp_out, ret_w_in, ret_norm_gain, ret_w_out, kv_norm_gain, kv_w_ada, kv_b_ada, kv_w, forget_bias, k_norm_gain, fox_w_in, q_norm_gain, fox_w_out, loss_target, m_norm_mix_gain, m_norm_mlp_gain, m_w_ada, m_b_ada, m_w_mlp_in, m_w_mlp_out, m_ret_w_in, m_ret_norm_gain, m_ret_w_out, m_kv_norm_gain, m_kv_w_ada, m_kv_b_ada, m_kv_w, m_forget_bias, m_k_norm_gain, m_fox_w_in, m_q_norm_gain, m_fox_w_out, v_norm_mix_gain, v_norm_mlp_gain, v_w_ada, v_b_ada, v_w_mlp_in, v_w_mlp_out, v_ret_w_in, v_ret_norm_gain, v_ret_w_out, v_kv_norm_gain, v_kv_w_ada, v_kv_b_ada, v_kv_w, v_forget_bias, v_k_norm_gain, v_fox_w_in, v_q_norm_gain, v_fox_w_out):
    D = D_MODEL
    S = x.shape[1]
    x0 = x.reshape(S, D)
    target = loss_target.reshape(S, D)
    me = 4 * lax.axis_index("x") + 2 * lax.axis_index("y") + lax.axis_index("c")
    n_ada = w_ada.shape[2]
    n_kvada = kv_w_ada.shape[1]
    n_kv = kv_w.shape[1]

    c_all, ret_gain = all_gather("gather_c", [c.reshape(D // LANES, LANES), ret_norm_gain.reshape(RET_HEADS, -1)])
    ret_gain = jnp.transpose(ret_gain, (1, 0, 2)).reshape(1, RET_HEADS * RET_V)
    c_act = rowwise("silu_c", _silu, [c_all.reshape(N_DEV, D)], [], [(D, F32)])[0]
    w_ada_cat = jnp.concatenate([w_ada[0], w_ada[1], kv_w_ada], axis=1).astype(BF16)[None]
    n_cat = 2 * n_ada + n_kvada
    ada_part = mm_nn("ada_proj", c_act, w_ada_cat)[0]
    ada_mine = all_to_all("ada_rows", [ada_part.reshape(N_DEV, n_cat // LANES, LANES)])[0]
    ada_mine = ada_mine.reshape(N_DEV, n_cat)
    ada_raw = [ada_mine[:, l * n_ada:(l + 1) * n_ada].reshape(1, 6 * D) for l in range(2)]
    kvada_raw = ada_mine[:, 2 * n_ada:].reshape(1, 2 * D)
    kv_bias = kv_b_ada.reshape(1, 2 * D)
    kv_gain = kv_norm_gain.reshape(1, D)
    fb = jnp.pad(forget_bias.reshape(1, FOX_HEADS), ((0, 0), (0, LANES - FOX_HEADS)))
    k_gain = jnp.tile(k_norm_gain.reshape(1, FOX_DH), (1, FOX_HEADS))
    q_gain = jnp.tile(q_norm_gain.reshape(1, FOX_DH), (1, FOX_HEADS))

    w_names = ["ret_in", "ret_out", "mlp_in0", "mlp_out0", "kv", "fox_in", "fox_out", "mlp_in1", "mlp_out1"]
    shards = [ret_w_in[0].astype(BF16), ret_w_out[0].astype(BF16), w_mlp_in[0].astype(BF16), w_mlp_out[0].astype(BF16),
              kv_w.astype(BF16), fox_w_in[0].astype(BF16), fox_w_out[0].astype(BF16), w_mlp_in[1].astype(BF16),
              w_mlp_out[1].astype(BF16)]
    two_level = {"ret_in", "ret_out", "mlp_in0", "mlp_out0"}
    w_plans = {name: plan_gather if name in two_level else plan_gather_direct for name in w_names}
    w_handles, token = exchange_start("gather_weights_start", shards,
                                      [lax.empty((N_DEV,) + a.shape, a.dtype) for a in shards],
                                      [w_plans[name] for name in w_names], after=(ada_mine, ret_gain))
    w_handles = dict(zip(w_names, w_handles))

    forwards = {}

    def forward_early(name, after):
        arrived = exchange_wait("gather_wait_" + name, [w_handles[name]], w_plans[name], after)[1]
        forwards[name], tok = exchange_start("gather_forward_start_" + name, None, arrived, plan_forward)
        return tok

    def weight(name, after):
        if name in two_level:
            return exchange_wait("gather_forward_wait_" + name, forwards[name], plan_forward, after)[1][0]
        return exchange_wait("gather_wait_" + name, [w_handles[name]], w_plans[name], after)[1][0]

    pos = positions.reshape(S, 1).astype(F32)
    half = RET_QK // 2
    inv_freq = jnp.asarray((ROPE_BASE ** (-np.arange(half, dtype=np.float32) / half)).reshape(1, half), F32)

    def angles(p, f):
        ang = p * f
        return jnp.cos(ang), jnp.sin(ang)

    cos, sin = rowwise("rope_table", angles, [pos], [inv_freq], [(half, F32), (half, F32)])

    def mod_mix(layer):
        def fn(xb, ada, bias, gain):
            sh, sc = _ada_slices(ada, bias)[:2]
            return _modulate(xb, gain[layer:layer + 1], sc, sh)
        return fn

    def mod_mlp(layer):
        def fn(xb, ada, bias, gain):
            sh, sc = _ada_slices(ada, bias)[3:5]
            return _modulate(xb, gain[layer:layer + 1], sc, sh)
        return fn

    h1_0 = rowwise("mod_mix0", mod_mix(0), [x0], [ada_raw[0], b_ada[0:1], norm_mix_gain], [(D, BF16)], after=token)[0]
    W_ret_in = weight("ret_in", forward_early("ret_in", h1_0))
    proj = mm_nn("ret_proj", h1_0, W_ret_in, (BF16,))[0]
    token = forward_early("ret_out", proj)

    def rope_fwd(qb, kb, cs, sn):
        return (_rotate(qb.astype(F32), cs, sn, RET_HEADS, 1.0),
                _rotate(kb.astype(F32), cs, sn, RET_HEADS, 1.0) * (RET_QK ** -0.5))

    q_rot, k_rot = rowwise("rope", rope_fwd, [(proj, D, 0), (proj, D, 1), cos, sin], [], [(D, BF16), (D, BF16)],
                           after=token)
    v_ret = (proj, RET_V, (2 * D) // RET_V)
    y_ret = retention("ret_fwd", (q_rot, RET_QK, 0), (k_rot, RET_QK, 0), v_ret, reverse=False)

    def ret_gate(yb, gb, gain):
        return _silu(gb.astype(F32)) * _norm_wide_heads(yb, gain, RET_HEADS)

    mixin0 = rowwise("ret_gate", ret_gate, [y_ret, (proj, 2 * D, 2)], [ret_gain], [(2 * D, BF16)])[0]
    W_ret_out = weight("ret_out", mixin0).reshape(1, 2 * D, D)
    mix0 = mm_nn("ret_out", mixin0, W_ret_out, after=forward_early("mlp_in0", W_ret_out))[0]

    def residual_mod(layer, slot):
        def fn(xb, bb, ada, bias, gain):
            s = _ada_slices(ada, bias)
            xn = xb + s[2] * bb
            return xn, _modulate(xn, gain[layer:layer + 1], s[4], s[3])
        return fn

    x1, h2_0 = rowwise("res_mix0", residual_mod(0, 0), [x0, mix0], [ada_raw[0], b_ada[0:1], norm_mlp_gain],
                       [(D, F32), (D, BF16)])

    W_mlp_in, W_mlp_out = {}, {}

    def mlp_forward(tag, h2, layer):
        W_mlp_in[layer] = weight("mlp_in" + tag, h2)
        early = forward_early("mlp_out" + tag, W_mlp_in[layer]) if "mlp_out" + tag in two_level else None
        u, act = mm_nn("mlp_in" + tag, h2, W_mlp_in[layer], (BF16, BF16), after=early,
                       epilogue=lambda acc: (acc, jnp.square(jnp.maximum(acc, 0.0))))
        W_mlp_out[layer] = weight("mlp_out" + tag, act).reshape(1, 4 * D, D)
        return u, act, mm_nn("mlp_out" + tag, act, W_mlp_out[layer])[0]

    u0, act0, mlp0 = mlp_forward("0", h2_0, 0)

    def res_mlp0(xb, bb, ada0, bias0, ada1, bias1, kva, kvb, gain_mix, gain_kv):
        xn = xb + _ada_slices(ada0, bias0)[5] * bb
        s1 = _ada_slices(ada1, bias1)
        kv_shift, kv_scale = _ada_slices(kva, kvb)
        return xn, _modulate(xn, gain_kv, kv_scale, kv_shift), _modulate(xn, gain_mix[1:2], s1[1], s1[0])

    x2, h_kv, h1_1 = rowwise("res_mlp0", res_mlp0, [x1, mlp0],
                             [ada_raw[0], b_ada[0:1], ada_raw[1], b_ada[1:2], kvada_raw, kv_bias, norm_mix_gain, kv_gain],
                             [(D, F32), (D, BF16), (D, BF16)])

    kv_full = jnp.transpose(weight("kv", h_kv), (1, 0, 2)).reshape(D, N_DEV * n_kv)
    W_kv = jnp.pad(kv_full, ((0, 0), (0, KV_PAD - N_DEV * n_kv)))[None]
    kvf = mm_nn("kv_proj", h_kv, W_kv)[0]

    def kv_post(kb, fblk, kg, bias):
        head = lax.broadcasted_iota(jnp.int32, fblk.shape, 1) < FOX_HEADS
        return _norm_fox_heads(kb, kg), jnp.where(head, _log_sigmoid(fblk + bias), 0.0)

    kn, log_f = rowwise("kv_post", kv_post, [(kvf, D, 0), (kvf, LANES, 2 * D // LANES)], [k_gain, fb],
                        [(D, BF16), (LANES, F32)])
    f_cum = cumsum_rows("f_cumsum", log_f, reverse=False)
    f_cum_t = f_cum.T

    W_fox_in = weight("fox_in", kvf)
    qo = mm_nn("fox_proj", h1_1, W_fox_in)[0]
    qn = rowwise("q_norm", _norm_fox_heads, [(qo, D, 0)], [q_gain], [(D, BF16)])[0]
    y_att, lse = fox_forward("fox_fwd", qn, kn, kvf, f_cum_t)
    mixin1 = rowwise("fox_gate", lambda ob, yb: jax.nn.sigmoid(ob) * yb, [(qo, D, 1), y_att], [], [(D, BF16)])[0]
    W_fox_out = weight("fox_out", mixin1).reshape(1, D, D)
    mix1 = mm_nn("fox_out", mixin1, W_fox_out)[0]
    x3, h2_1 = rowwise("res_mix1", residual_mod(1, 0), [x2, mix1], [ada_raw[1], b_ada[1:2], norm_mlp_gain],
                       [(D, F32), (D, BF16)])
    u1, act1, mlp1 = mlp_forward("1", h2_1, 1)

    def scatter_start(tag, gws, after=()):
        lands = [lax.empty((4,) + g.shape[1:], g.dtype) for g in gws]
        return exchange_start("scatter_sibling_start_" + tag, gws, lands, plan_to_sibling, after)

    def scatter_relay(tag, handles, after, start_after=()):
        gws, from_sibling = exchange_wait("scatter_sibling_wait_" + tag, handles, plan_to_sibling, after)
        sums = [chip_sum("chip_sum_%s%d" % (tag, i), g, s) for i, (g, s) in enumerate(zip(gws, from_sibling))]
        lands = [lax.empty((3,) + s.shape[1:], s.dtype) for s in sums]
        return exchange_start("scatter_owner_start_" + tag, sums, lands, plan_to_owners, start_after)

    def scatter_direct_start(tag, gws):
        lands = [lax.empty(g.shape, g.dtype) for g in gws]
        return exchange_start("scatter_direct_start_" + tag, gws, lands, plan_scatter_direct)

    def scatter_direct_finish(tag, handles, after):
        return [[(r, N_DEV)] for r in exchange_wait("scatter_direct_wait_" + tag, handles, plan_scatter_direct, after)[1]]

    def scatter_finish(tag, handles, after):
        sums, received = exchange_wait("scatter_owner_wait_" + tag, handles, plan_to_owners, after)
        return [[(s, 1), (r, 3)] for s, r in zip(sums, received)]

    def loss_head(xb, bb, tb, ada, bias):
        g2 = _ada_slices(ada, bias)[5]
        err = xb + g2 * bb - tb
        dx = err * (1.0 / D)
        loss = 0.5 * jnp.sum(jnp.sum(err * err, axis=1, keepdims=True) * (1.0 / D), axis=0, keepdims=True)
        return dx, (dx * g2), jnp.broadcast_to(loss, (1, LANES)), jnp.sum(dx * bb, axis=0, keepdims=True)

    dx4, dmlp1, loss_acc, dg2_1 = rowwise("loss_head", loss_head, [x3, mlp1, target], [ada_raw[1], b_ada[1:2]],
                                          [(D, F32), (D, BF16)], [(1, LANES), (1, D)])

    def mlp_backward(tag, dmlp, act, u, h2, layer, after=None):
        du = mm_nt("mlp_out_dx" + tag, dmlp, W_mlp_out[layer], (BF16,), extra=(u,), after=after,
                   epilogue=lambda acc, ub: (acc * (2.0 * jnp.maximum(ub.astype(F32), 0.0)),))[0]
        gw_out = mm_tn("mlp_out_dw" + tag, act, dmlp, 1, BF16).reshape(N_DEV, -1, D)
        dh2 = mm_nt("mlp_in_dx" + tag, du, W_mlp_in[layer], (BF16,))[0]
        gw_in = mm_tn("mlp_in_dw" + tag, h2, du, N_DEV, BF16)
        return dh2, gw_in, gw_out

    def mod_backward(layer, slots, gate_slot):
        def fn(xb, dhb, dresb, branchb, ada, bias, gain):
            s = _ada_slices(ada, bias)
            g = gain[layer:layer + 1]
            dx, dgain, dsc, dsh = _vjp(_modulate, (xb, g, s[slots[1]], s[slots[0]]), dhb.astype(F32))
            dx = dx + dresb
            d_branch = dx * s[gate_slot]
            return dx, d_branch, dgain, dsc, dsh, jnp.sum(dx * branchb, axis=0, keepdims=True)
        return fn

    vec = (1, D)
    dh2_1, gw_mlp_in1, gw_mlp_out1 = mlp_backward("1", dmlp1, act1, u1, h2_1, 1)
    scat_a1, token_a1 = scatter_direct_start("a1", [gw_mlp_in1, gw_mlp_out1])
    dx3, dmix1, dgain_mlp1, dsc2_1, dsh2_1, dg1_1 = rowwise(
        "mod_mlp1_bwd", mod_backward(1, (3, 4), 2), [x3, dh2_1, dx4, mix1], [ada_raw[1], b_ada[1:2], norm_mlp_gain],
        [(D, F32), (D, BF16)], [vec] * 4, after=token_a1)
    dmixin1 = mm_nt("fox_out_dx", dmix1, W_fox_out)[0]
    gw_fox_out = mm_tn("fox_out_dw", mixin1, dmix1, 1, BF16).reshape(N_DEV, -1, D)

    def fox_gate_bwd(db, ob, yb):
        sg = jax.nn.sigmoid(ob)
        return db * sg, db * yb * sg * (1.0 - sg)

    dy_att, d_og = rowwise("fox_gate_bwd", fox_gate_bwd, [dmixin1, (qo, D, 1), y_att], [], [(D, F32), (D, BF16)])
    dqn, dkn, dv_att, dfq, dfk = fox_backward("fox_bwd", qn, kn, kvf, f_cum_t, y_att, dy_att, lse)

    def q_norm_bwd(qb, db, ogb, gain):
        dq, dgain = _vjp(_norm_fox_heads, (qb, gain), db)
        return jnp.concatenate([dq, ogb.astype(F32)], axis=1), dgain

    dqo, dq_gain = rowwise("q_norm_bwd", q_norm_bwd, [(qo, D, 0), dqn, d_og], [q_gain], [(2 * D, BF16)], [vec])
    dh1_1 = mm_nt("fox_proj_dx", dqo, W_fox_in, (BF16,))[0]
    gw_fox_in = mm_tn("fox_proj_dw", h1_1, dqo, N_DEV, BF16)

    dfk_rows = jnp.pad(dfk[:, :2, :].reshape(FOX_HEADS, S).T, ((0, 0), (0, LANES - FOX_HEADS)))

    def df_total(*blks):
        tot = blks[0]
        for b in blks[1:]:
            tot = tot + b
        return tot

    d_fcum = rowwise("df_sum", df_total, [dfk_rows] + [(dfq, LANES, 0, p) for p in range(N_PAIR)], [], [(LANES, F32)])[0]
    d_logf = cumsum_rows("df_cumsum", d_fcum, reverse=True)

    def kv_post_bwd(kb, fblk, dkb, dvb, dlf, kg, bias):
        dk, dgain = _vjp(_norm_fox_heads, (kb, kg), dkb)
        df = dlf * (1.0 / (1.0 + jnp.exp(fblk + bias)))
        return jnp.concatenate([dk, dvb.astype(F32), df], axis=1), dgain, jnp.sum(df, axis=0, keepdims=True)

    dkvf, dk_gain, dfb = rowwise("kv_post_bwd", kv_post_bwd,
                                 [(kvf, D, 0), (kvf, LANES, 2 * D // LANES), dkn, dv_att, d_logf], [k_gain, fb],
                                 [(KV_PAD, BF16)], [vec, (1, LANES)])
    dh_kv = mm_nt("kv_proj_dx", dkvf, W_kv, (BF16,))[0]
    gw_kv = mm_tn("kv_proj_dw", h_kv, dkvf, 1, BF16)[0, :, :N_DEV * n_kv]
    gw_kv = jnp.transpose(gw_kv.reshape(D, N_DEV, n_kv), (1, 0, 2))
    scat_a2, token_a = scatter_direct_start("a2", [gw_fox_out, gw_fox_in, gw_kv])

    def x2_bwd(xb, dh1b, dhkb, dresb, branchb, ada0, bias0, ada1, bias1, kva, kvb, gain_mix, gain_kv):
        s1 = _ada_slices(ada1, bias1)
        kv_shift, kv_scale = _ada_slices(kva, kvb)
        dxa, dgain_mix, dsc1, dsh1 = _vjp(_modulate, (xb, gain_mix[1:2], s1[1], s1[0]), dh1b.astype(F32))
        dxb, dgain_kv, dkv_scale, dkv_shift = _vjp(_modulate, (xb, gain_kv, kv_scale, kv_shift), dhkb.astype(F32))
        dx = dresb + dxa + dxb
        g2 = _ada_slices(ada0, bias0)[5]
        return (dx, dx * g2, dgain_mix, dsc1, dsh1, dgain_kv, dkv_scale, dkv_shift,
                jnp.sum(dx * branchb, axis=0, keepdims=True))

    (dx2, dmlp0, dgain_mix1, dsc1_1, dsh1_1, dgain_kv, dkv_scale, dkv_shift, dg2_0) = rowwise(
        "x2_bwd", x2_bwd, [x2, dh1_1, dh_kv, dx3, mlp0],
        [ada_raw[0], b_ada[0:1], ada_raw[1], b_ada[1:2], kvada_raw, kv_bias, norm_mix_gain, kv_gain],
        [(D, F32), (D, BF16)], [vec] * 7, after=token_a)

    dh2_0, gw_mlp_in0, gw_mlp_out0 = mlp_backward("0", dmlp0, act0, u0, h2_0, 0)
    scat_b1, token_b1 = scatter_direct_start("b1", [gw_mlp_in0, gw_mlp_out0])
    dx1, dmix0, dgain_mlp0, dsc2_0, dsh2_0, dg1_0 = rowwise(
        "mod_mlp0_bwd", mod_backward(0, (3, 4), 2), [x1, dh2_0, dx2, mix0], [ada_raw[0], b_ada[0:1], norm_mlp_gain],
        [(D, F32), (D, BF16)], [vec] * 4, after=token_b1)
    dmixin0 = mm_nt("ret_out_dx", dmix0, W_ret_out, (BF16,))[0]
    gw_ret_out = mm_tn("ret_out_dw", mixin0, dmix0, 1, BF16).reshape(N_DEV, -1, D)
    scat_b2, token_b = scatter_direct_start("b2", [gw_ret_out])

    def ret_gate_bwd(db, yb, gb, gain):
        return _vjp(ret_gate, (yb, gb.astype(F32), gain), db.astype(F32))

    dy_ret, dgate, dret_gain = rowwise("ret_gate_bwd", ret_gate_bwd, [dmixin0, y_ret, (proj, 2 * D, 2)], [ret_gain],
                                       [(2 * D, BF16), (2 * D, BF16)], [(1, 2 * D)], after=token_b)
    dy_h = (dy_ret, RET_V, 0)
    dq_rot = retention("ret_dq", dy_h, v_ret, (k_rot, RET_QK, 0), reverse=False)
    dk_rot = retention("ret_dk", v_ret, dy_h, (q_rot, RET_QK, 0), reverse=True)
    dv_ret = retention("ret_dv", (k_rot, RET_QK, 0), (q_rot, RET_QK, 0), dy_h, reverse=True, out_dtype=BF16)

    def rope_bwd(dqb, dkb, dvb, dgb, cs, sn):
        dq = _rotate(dqb, cs, sn, RET_HEADS, -1.0)
        dk = _rotate(dkb, cs, sn, RET_HEADS, -1.0) * (RET_QK ** -0.5)
        return jnp.concatenate([dq, dk, dvb.astype(F32), dgb.astype(F32)], axis=1)

    dproj = rowwise("rope_bwd", rope_bwd, [dq_rot, dk_rot, dv_ret, dgate, cos, sin], [], [(6 * D, BF16)])[0]
    gw_ret_in = mm_tn("ret_proj_dw", h1_0, dproj, N_DEV, BF16)
    scat_c, token_c = scatter_start("c", [gw_ret_in])
    dh1_0 = mm_nt("ret_proj_dx", dproj, W_ret_in, (BF16,), after=token_c)[0]

    def x0_bwd(xb, dhb, dresb, ada, bias, gain):
        s = _ada_slices(ada, bias)
        dx, dgain, dsc, dsh = _vjp(_modulate, (xb, gain[0:1], s[1], s[0]), dhb.astype(F32))
        return dx + dresb, dgain, dsc, dsh

    grad_x, dgain_mix0, dsc1_0, dsh1_0 = rowwise("x0_bwd", x0_bwd, [x0, dh1_0, dx1],
                                                 [ada_raw[0], b_ada[0:1], norm_mix_gain], [(D, F32)], [vec] * 3)

    small = jnp.concatenate([
        dsh1_0, dsc1_0, dg1_0, dsh2_0, dsc2_0, dg2_0,
        dsh1_1, dsc1_1, dg1_1, dsh2_1, dsc2_1, dg2_1,
        dkv_shift, dkv_scale,
        dgain_mix0, dgain_mix1, dgain_mlp0, dgain_mlp1, dgain_kv,
        dret_gain,
        dq_gain.reshape(FOX_HEADS, FOX_DH).sum(axis=0).reshape(1, FOX_DH),
        dk_gain.reshape(FOX_HEADS, FOX_DH).sum(axis=0).reshape(1, FOX_DH),
        dfb,
        loss_acc,
    ], axis=1)
    small_all = all_gather("gather_small", [small.reshape(-1, LANES)])[0].reshape(N_DEV, 1, -1)
    loss = jnp.sum(small_all[:, 0, -1])
    scat_c, token_c = scatter_relay("c", scat_c, grad_x, start_after=(small_all,))
    o_ada = 14 * D
    d_ada = small_all[:, 0, :o_ada]
    d_cat = jnp.concatenate([
        lax.dynamic_slice_in_dim(d_ada[:, 0:6 * D], me * n_ada, n_ada, axis=1),
        lax.dynamic_slice_in_dim(d_ada[:, 6 * D:12 * D], me * n_ada, n_ada, axis=1),
        lax.dynamic_slice_in_dim(d_ada[:, 12 * D:14 * D], me * n_kvada, n_kvada, axis=1)], axis=1)
    gw_ada_cat = mm_tn("ada_dw", c_act, d_cat, 1, F32, after=token_c)[0]

    results = {}

    def update(name, parts, w, m, v, layers=1):
        per_layer = parts if layers > 1 else [parts]
        shape = w.shape
        C = shape[-1]
        R = int(np.prod(shape)) // (layers * C)
        per_layer = [p if isinstance(p, list) else [(p, p.shape[0])] for p in per_layer]
        per_layer = [[(a.reshape(a.shape[0], R, C), n) for a, n in p] for p in per_layer]
        outs = adamw("adamw_" + name, per_layer, w.reshape(layers, R, C), m.reshape(layers, R, C), v.reshape(layers, R, C))
        results[name] = tuple(t.reshape(shape) for t in outs)

    def small_parts(lo, width):
        return small_all[:, :, lo:lo + width]

    update("norm_mix_gain", jnp.concatenate([small_parts(o_ada, D), small_parts(o_ada + D, D)], axis=1),
           norm_mix_gain, m_norm_mix_gain, v_norm_mix_gain)
    update("norm_mlp_gain", jnp.concatenate([small_parts(o_ada + 2 * D, D), small_parts(o_ada + 3 * D, D)], axis=1),
           norm_mlp_gain, m_norm_mlp_gain, v_norm_mlp_gain)
    update("w_ada", [gw_ada_cat[None, :, :n_ada], gw_ada_cat[None, :, n_ada:2 * n_ada]], w_ada, m_w_ada, v_w_ada, layers=2)
    update("b_ada", jnp.concatenate([small_parts(0, 6 * D), small_parts(6 * D, 6 * D)], axis=1), b_ada, m_b_ada, v_b_ada)
    o_ret = o_ada + 5 * D
    n_rg = ret_norm_gain.shape[2]
    ret_gain_parts = small_parts(o_ret, 2 * D).reshape(N_DEV, RET_HEADS, RET_V)
    ret_gain_parts = lax.dynamic_slice_in_dim(ret_gain_parts, me * n_rg, n_rg, axis=2)
    update("ret_norm_gain", ret_gain_parts, ret_norm_gain, m_ret_norm_gain, v_ret_norm_gain)
    update("kv_norm_gain", small_parts(o_ada + 4 * D, D), kv_norm_gain, m_kv_norm_gain, v_kv_norm_gain)
    update("kv_w_ada", gw_ada_cat[None, :, 2 * n_ada:], kv_w_ada, m_kv_w_ada, v_kv_w_ada)
    update("kv_b_ada", small_parts(12 * D, 2 * D), kv_b_ada, m_kv_b_ada, v_kv_b_ada)
    o_q = o_ret + 2 * D
    update("forget_bias", small_parts(o_q + 2 * FOX_DH, FOX_HEADS), forget_bias, m_forget_bias, v_forget_bias)
    update("k_norm_gain", small_parts(o_q + FOX_DH, FOX_DH), k_norm_gain, m_k_norm_gain, v_k_norm_gain)
    update("q_norm_gain", small_parts(o_q, FOX_DH), q_norm_gain, m_q_norm_gain, v_q_norm_gain)

    r_mlp_in1, r_mlp_out1 = scatter_direct_finish("a1", scat_a1, results["q_norm_gain"][1])
    r_fox_out, r_fox_in, r_kv = scatter_direct_finish("a2", scat_a2, r_mlp_in1[0][0])
    r_mlp_in0, r_mlp_out0 = scatter_direct_finish("b1", scat_b1, r_kv[0][0])
    r_ret_out = scatter_direct_finish("b2", scat_b2, r_mlp_in0[0][0])[0]
    update("kv_w", r_kv, kv_w, m_kv_w, v_kv_w)
    update("fox_w_in", r_fox_in, fox_w_in, m_fox_w_in, v_fox_w_in)
    update("fox_w_out", r_fox_out, fox_w_out, m_fox_w_out, v_fox_w_out)
    update("ret_w_out", r_ret_out, ret_w_out, m_ret_w_out, v_ret_w_out)
    update("w_mlp_in", [r_mlp_in0, r_mlp_in1], w_mlp_in, m_w_mlp_in, v_w_mlp_in, layers=2)
    update("w_mlp_out", [r_mlp_out0, r_mlp_out1], w_mlp_out, m_w_mlp_out, v_w_mlp_out, layers=2)
    r_ret_in = scatter_finish("c", scat_c, results["w_mlp_out"][1])[0]
    update("ret_w_in", r_ret_in, ret_w_in, m_ret_w_in, v_ret_w_in)

    order = ["norm_mix_gain", "norm_mlp_gain", "w_ada", "b_ada", "w_mlp_in", "w_mlp_out", "ret_w_in", "ret_norm_gain",
             "ret_w_out", "kv_norm_gain", "kv_w_ada", "kv_b_ada", "kv_w", "forget_bias", "k_norm_gain", "fox_w_in",
             "q_norm_gain", "fox_w_out"]
    out = [loss, grad_x.reshape(x.shape)]
    for slot in range(4):
        out += [results[n][slot] for n in order]
    return tuple(out)
```

```python
import functools
import math

import numpy as np
import jax
import jax.numpy as jnp
from jax import lax
from jax.experimental import pallas as pl
from jax.experimental.pallas import tpu as pltpu

F32 = jnp.float32
BF16 = jnp.bfloat16

N_DEV = 8
D_MODEL = 1024
RET_HEADS = 4
RET_QK = D_MODEL // RET_HEADS
RET_V = 2 * D_MODEL // RET_HEADS
RET_CHUNK = 128
ROPE_BASE = 10000.0
FOX_HEADS = 16
FOX_DH = D_MODEL // FOX_HEADS
EPS = 1e-6
LANES = 128
KV_PAD = 2 * D_MODEL + LANES

ADAM_LR = 0.001
ADAM_B1 = 0.9
ADAM_B2 = 0.999
ADAM_EPS = 1e-08
ADAM_WD = 0.01
ADAM_STEP = 10

VMEM_LIMIT_BYTES = 56 * 1024 * 1024


def _params(sem=None):
    return pltpu.CompilerParams(dimension_semantics=sem, vmem_limit_bytes=VMEM_LIMIT_BYTES)


def _me():
    return lax.axis_index("x"), lax.axis_index("y"), lax.axis_index("c")


def _peer(k):
    x, y, c = _me()
    return (1 - x if k & 4 else x, 1 - y if k & 2 else y, 1 - c if k & 1 else c)


def _peer_index(k):
    px, py, pc = _peer(k)
    return 4 * px + 2 * py + pc


def _exchange(name, xs, scatter):
    n = len(xs)

    def body(*refs):
        x_refs, o_refs = refs[:n], refs[n:2 * n]
        send_sems, recv_sems, local_sems = refs[2 * n:]
        x, y, c = _me()
        me = 4 * x + 2 * y + c
        local = []
        for i in range(n):
            src = x_refs[i].at[me] if scatter else x_refs[i]
            cp = pltpu.make_async_copy(src, o_refs[i].at[me], local_sems.at[i])
            cp.start()
            local.append(cp)
        remote = []
        for k in range(1, N_DEV):
            for i in range(n):
                src = x_refs[i].at[_peer_index(k)] if scatter else x_refs[i]
                cp = pltpu.make_async_remote_copy(
                    src_ref=src, dst_ref=o_refs[i].at[me],
                    send_sem=send_sems.at[(k - 1) * n + i], recv_sem=recv_sems.at[(k - 1) * n + i],
                    device_id=_peer(k), device_id_type=pl.DeviceIdType.MESH)
                cp.start()
                remote.append(cp)
        for cp in remote:
            cp.wait()
        for cp in local:
            cp.wait()

    out_shape = [jax.ShapeDtypeStruct(x.shape if scatter else (N_DEV,) + x.shape, x.dtype) for x in xs]
    any_spec = pl.BlockSpec(memory_space=pl.ANY)
    return pl.pallas_call(
        body, name=name, out_shape=out_shape,
        in_specs=[any_spec] * n, out_specs=[any_spec] * n,
        scratch_shapes=[pltpu.SemaphoreType.DMA(((N_DEV - 1) * n,)),
                        pltpu.SemaphoreType.DMA(((N_DEV - 1) * n,)),
                        pltpu.SemaphoreType.DMA((n,))],
    )(*xs)


def all_gather(name, xs):
    return _exchange(name, xs, scatter=False)


def all_to_all(name, xs):
    return _exchange(name, xs, scatter=True)


_HBM = pl.BlockSpec(memory_space=pltpu.HBM)
_SEM = pl.BlockSpec(memory_space=pltpu.SEMAPHORE)
_ANY = pl.BlockSpec(memory_space=pl.ANY)
_EFFECT = pltpu.SideEffectType.DATAFLOW_SIDE_EFFECTING

SIBLING = 1
CHIP_PEERS = (2, 4, 6)


def _my_index():
    x, y, c = _me()
    return 4 * x + 2 * y + c


def plan_gather(x, land, me):
    return [(x, land.at[me], k) for k in (SIBLING,) + CHIP_PEERS]


def plan_gather_direct(x, land, me):
    return [(x, land.at[me], k) for k in range(1, N_DEV)]


def plan_scatter_direct(x, land, me):
    return [(x.at[me ^ k], land.at[me], k) for k in range(1, N_DEV)]


def plan_forward(x, land, me):
    return [(x.at[me ^ k], land.at[me ^ k], SIBLING) for k in CHIP_PEERS]


def plan_to_sibling(x, land, me):
    return [(x.at[me ^ SIBLING ^ (2 * q)], land.at[q], SIBLING) for q in range(4)]


def plan_to_owners(x, land, me):
    return [(x.at[q], land.at[q - 1], 2 * q) for q in (1, 2, 3)]


def own_gather(x, land, me):
    return x, land.at[me]


def own_scatter(x, land, me):
    return x.at[me], land.at[me]


OWN_COPY = {plan_gather: own_gather, plan_gather_direct: own_gather, plan_scatter_direct: own_scatter}
N_COPIES = {plan_gather: 4, plan_gather_direct: 7, plan_scatter_direct: 7, plan_forward: 3, plan_to_sibling: 4,
            plan_to_owners: 3}


def _plans(plan, n):
    return list(plan) if isinstance(plan, (list, tuple)) else [plan] * n


def _own_copies(plan, x_refs, land_refs, own_sems):
    me = _my_index()
    plans = _plans(plan, len(land_refs))
    with_own = [i for i, p in enumerate(plans) if p in OWN_COPY]
    return [pltpu.make_async_copy(*OWN_COPY[plans[i]](x_refs[i], land_refs[i], me), own_sems[s])
            for s, i in enumerate(with_own)]


def _plan_copies(plan, x_refs, land_refs, send_sems, recv_sems):
    me = _my_index()
    plans = _plans(plan, len(land_refs))
    return [pltpu.make_async_remote_copy(src_ref=src, dst_ref=dst, send_sem=send_sems[i].at[s], recv_sem=recv_sems[i].at[s],
                                         device_id=_peer(k), device_id_type=pl.DeviceIdType.MESH)
            for i in range(len(land_refs)) for s, (src, dst, k) in enumerate(plans[i](x_refs[i], land_refs[i], me))]


def exchange_now(name, xs, lands, plan):
    n = len(lands)
    n_x = 0 if xs is None else n

    def body(*refs):
        land_in, land_out = refs[n_x:n_x + n], refs[n_x + n:n_x + 2 * n]
        x_refs = land_in if xs is None else refs[:n]
        sems = refs[n_x + 2 * n:]
        copies = _own_copies(plan, x_refs, land_out, sems[2 * n:]) + _plan_copies(plan, x_refs, land_out, sems[:n], sems[n:2 * n])
        for cp in copies:
            cp.start()
        for cp in copies:
            cp.wait()

    return pl.pallas_call(
        body, name=name, out_shape=[jax.ShapeDtypeStruct(a.shape, a.dtype) for a in lands],
        in_specs=[_ANY] * (n_x + n), out_specs=[_ANY] * n,
        input_output_aliases={n_x + i: i for i in range(n)},
        scratch_shapes=[pltpu.SemaphoreType.DMA((N_COPIES[p],)) for p in _plans(plan, n)] * 2
        + [pltpu.SemaphoreType.DMA(())] * sum(p in OWN_COPY for p in _plans(plan, n)),
    )(*([] if xs is None else xs), *lands)


def exchange_start(name, xs, lands, plan, after=()):
    n, m = len(lands), len(after)
    n_x = 0 if xs is None else n
    arrays = ([] if xs is None else list(xs)) + list(lands)
    n_a = len(arrays)
    n_own = sum(p in OWN_COPY for p in _plans(plan, n))
    n_s = 2 * n + n_own

    def body(*refs):
        land_refs = refs[n_x:n_a]
        x_refs = land_refs if xs is None else refs[:n]
        sems = refs[n_a + m:n_a + m + n_s]
        token = refs[-1]
        for cp in _own_copies(plan, x_refs, land_refs, sems[2 * n:]):
            cp.start()
        for cp in _plan_copies(plan, x_refs, land_refs, sems[:n], sems[n:2 * n]):
            cp.start()
        token[...] = jnp.zeros_like(token)

    sems = [pltpu.SemaphoreType.DMA((N_COPIES[p],)) for p in _plans(plan, n)] * 2 + [pltpu.SemaphoreType.DMA(())] * n_own
    res = pl.pallas_call(
        body, name=name,
        out_shape=sems + [pltpu.HBM(a.shape, a.dtype) for a in arrays] + [jax.ShapeDtypeStruct((8, LANES), F32)],
        in_specs=[_HBM] * n_a + [_ANY] * m,
        out_specs=[_SEM] * n_s + [_HBM] * n_a + [pl.BlockSpec(memory_space=pltpu.VMEM)],
        input_output_aliases={i: n_s + i for i in range(n_a)},
        compiler_params=pltpu.CompilerParams(has_side_effects=_EFFECT),
    )(*[pltpu.with_memory_space_constraint(a, pltpu.HBM) for a in arrays], *after)
    thru = res[n_s:n_s + n_a]
    own = iter(res[2 * n:n_s])
    handles = [(res[i], res[n + i], None if xs is None else thru[i], thru[n_x + i],
                next(own) if p in OWN_COPY else None) for i, p in enumerate(_plans(plan, n))]
    return handles, res[-1]


def exchange_wait(name, handles, plan, after):
    n = len(handles)
    in_place = handles[0][2] is None
    arrays = ([] if in_place else [h[2] for h in handles]) + [h[3] for h in handles]
    n_a = len(arrays)
    own_sems = [h[4] for h in handles if h[4] is not None]
    n_s = 2 * n + len(own_sems)

    def body(*refs):
        land_refs = refs[n_a - n:n_a]
        x_refs = land_refs if in_place else refs[:n]
        sems = refs[n_a:n_a + n_s]
        for cp in _own_copies(plan, x_refs, land_refs, sems[2 * n:]):
            cp.wait()
        for cp in _plan_copies(plan, x_refs, land_refs, sems[:n], sems[n:2 * n]):
            cp.wait_send()
            cp.wait_recv()

    res = pl.pallas_call(
        body, name=name,
        out_shape=[pltpu.HBM(a.shape, a.dtype) for a in arrays],
        in_specs=[_HBM] * n_a + [_SEM] * n_s + [_ANY],
        out_specs=[_HBM] * n_a,
        input_output_aliases={i: i for i in range(n_a)},
        compiler_params=pltpu.CompilerParams(has_side_effects=_EFFECT),
    )(*arrays, *[h[0] for h in handles], *[h[1] for h in handles], *own_sems, after)
    return (None if in_place else res[:n]), res[n_a - n:]


def chip_sum(name, gw, from_sibling):
    _, R, C = gw.shape
    tr = _row_tile(R, 512)
    me = _my_index().astype(jnp.int32).reshape(1)

    def body(me_ref, g_ref, s_ref, o_ref):
        o_ref[...] = (g_ref[...].astype(F32) + s_ref[...].astype(F32)).astype(o_ref.dtype)

    slot = pl.BlockSpec((None, tr, C), lambda q, i, me_ref: (q, i, 0))
    return pl.pallas_call(
        body, name=name, out_shape=jax.ShapeDtypeStruct((4, R, C), BF16),
        grid_spec=pltpu.PrefetchScalarGridSpec(
            num_scalar_prefetch=1, grid=(4, R // tr),
            in_specs=[pl.BlockSpec((None, tr, C), lambda q, i, me_ref: (me_ref[0] ^ (2 * q), i, 0)), slot],
            out_specs=slot),
        compiler_params=_params(("arbitrary", "arbitrary")),
    )(me, gw, from_sibling)


def _tile(n, cap):
    best = None
    for t in range(LANES, min(n, cap) + 1, LANES):
        if n % t == 0:
            best = t
    if best is None or (best < 256 and n <= 2304):
        return n
    return best


def _row_tile(m, cap):
    if m <= cap:
        return m
    t = cap
    while m % t:
        t //= 2
    return t


def _after_spec(after):
    return [] if after is None else [pl.BlockSpec(memory_space=pl.ANY)]


def _after_arg(after):
    return [] if after is None else [after]


def _mm_call(name, dims, grid, a_spec, b_spec, o_spec, o_shape, tile, a, b, out_dtypes, epilogue, extra, after, vecs=()):
    nk = grid[2]
    n_x, n_o = len(extra) + len(vecs), len(out_dtypes)

    def body(a_ref, b_ref, *refs):
        x_refs, o_refs = refs[:n_x], refs[len(refs) - n_s - n_o:len(refs) - n_s]
        part = lax.dot_general(a_ref[...].astype(BF16), b_ref[...].astype(BF16), (dims, ((), ())),
                               preferred_element_type=F32)

        def finish(acc):
            vals = (acc,) if epilogue is None else epilogue(acc, *[x[...] for x in x_refs])
            for o_ref, val in zip(o_refs, vals):
                o_ref[...] = val.astype(o_ref.dtype)

        if nk == 1:
            finish(part)
        else:
            acc_ref = refs[-1]
            k = pl.program_id(2)

            @pl.when(k == 0)
            def _():
                acc_ref[...] = part

            @pl.when(jnp.logical_and(k > 0, k < nk - 1))
            def _():
                acc_ref[...] += part

            @pl.when(k == nk - 1)
            def _():
                finish(acc_ref[...] + part)

    n_s = 0 if nk == 1 else 1
    return pl.pallas_call(
        body, name=name, out_shape=[jax.ShapeDtypeStruct(o_shape, dt) for dt in out_dtypes], grid=grid,
        in_specs=[a_spec, b_spec] + [o_spec] * len(extra) + [pl.BlockSpec(v.shape, lambda i, j, k: (0, 0)) for v in vecs]
        + _after_spec(after), out_specs=[o_spec] * n_o,
        scratch_shapes=[pltpu.VMEM(tile, F32)] * n_s,
        compiler_params=_params(("parallel", "parallel", "arbitrary")),
    )(a, b, *extra, *vecs, *_after_arg(after))


def mm_nn(name, a, w, out_dtypes=(F32,), epilogue=None, extra=(), after=None, vecs=(), tm_cap=2048):
    M, K = a.shape
    G, _, n = w.shape
    tn = _tile(n, 1024)
    tm = _row_tile(M, 512 if tn > 1024 else tm_cap)
    tk = _row_tile(K, 1024)
    r = n // tn
    return _mm_call(
        name, ((1,), (0,)), (M // tm, G * r, K // tk),
        pl.BlockSpec((tm, tk), lambda i, j, k: (i, k)),
        pl.BlockSpec((None, tk, tn), lambda i, j, k: (j // r, k, j % r)),
        pl.BlockSpec((tm, tn), lambda i, j, k: (i, j)), (M, G * n), (tm, tn),
        a, w, out_dtypes, epilogue, extra, after, vecs)


def mm_nt(name, dy, w, out_dtypes=(F32,), epilogue=None, extra=(), after=None):
    M, N = dy.shape
    G, K, n = w.shape
    tn = _tile(n, 1024)
    tm = _row_tile(M, 512 if tn > 1024 else 2048)
    tk = _row_tile(K, 1024)
    r = n // tn
    return _mm_call(
        name, ((1,), (1,)), (M // tm, K // tk, G * r),
        pl.BlockSpec((tm, tn), lambda i, j, k: (i, k)),
        pl.BlockSpec((None, tk, tn), lambda i, j, k: (k // r, j, k % r)),
        pl.BlockSpec((tm, tk), lambda i, j, k: (i, j)), (M, K), (tm, tk),
        dy, w, out_dtypes, epilogue, extra, after)


def mm_tn(name, a, dy, G, out_dtype=F32, after=None):
    M, K = a.shape
    n = dy.shape[1] // G
    tn = _tile(n, 1024)
    tk = _row_tile(K, 512 if tn > 1024 else 1024)
    tm = _row_tile(M, 2048)
    r = n // tn
    return _mm_call(
        name, ((0,), (0,)), (K // tk, G * r, M // tm),
        pl.BlockSpec((tm, tk), lambda i, j, k: (k, i)),
        pl.BlockSpec((tm, tn), lambda i, j, k: (k, j)),
        pl.BlockSpec((None, tk, tn), lambda i, j, k: (j // r, i, j % r)), (G, K, n), (tk, tn),
        a, dy, (out_dtype,), None, (), after)[0]


ROW_BLOCK_BYTES = 12 * 1024 * 1024


def rowwise(name, fn, rows, vecs, outs, accs=(), after=None):
    rows = [r if isinstance(r, tuple) else (r, r.shape[1], 0) for r in rows]
    rows = [r if len(r) == 4 else r + (None,) for r in rows]
    n_fn = len(rows) + len(vecs)
    vecs = list(vecs) + _after_arg(after)
    S = rows[0][0].shape[-2]
    row_bytes = sum(w * a.dtype.itemsize for a, w, _, _ in rows) + sum(w * jnp.dtype(dt).itemsize for w, dt in outs)
    tm = _row_tile(S, 1024)
    while tm > 256 and tm * row_bytes > ROW_BLOCK_BYTES:
        tm //= 2
    n_r, n_v, n_o, n_a = len(rows), len(vecs), len(outs), len(accs)

    def body(*refs):
        ins = [ref[...] for ref in refs[:n_r + n_v]]
        o_refs = refs[n_r + n_v:n_r + n_v + n_o]
        a_refs = refs[n_r + n_v + n_o:]
        res = fn(*ins[:n_fn])
        res = res if isinstance(res, (tuple, list)) else (res,)
        for ref, val in zip(o_refs, res[:n_o]):
            ref[...] = val.astype(ref.dtype)
        if n_a:
            @pl.when(pl.program_id(0) == 0)
            def _():
                for ref in a_refs:
                    ref[...] = jnp.zeros_like(ref)
            for ref, val in zip(a_refs, res[n_o:]):
                ref[...] += val

    in_specs = [pl.BlockSpec((tm, w), functools.partial(lambda cb, i: (i, cb), cb)) if slab is None else
                pl.BlockSpec((None, tm, w), functools.partial(lambda cb, slab, i: (slab, i, cb), cb, slab))
                for _, w, cb, slab in rows]
    in_specs += [pl.BlockSpec(v.shape, lambda i: (0, 0)) for v in vecs]
    out_specs = [pl.BlockSpec((tm, w), lambda i: (i, 0)) for w, _ in outs]
    out_specs += [pl.BlockSpec(a, lambda i: (0, 0)) for a in accs]
    out_shape = [jax.ShapeDtypeStruct((S, w), dt) for w, dt in outs]
    out_shape += [jax.ShapeDtypeStruct(a, F32) for a in accs]
    res = pl.pallas_call(
        body, name=name, out_shape=out_shape, grid=(S // tm,),
        in_specs=in_specs, out_specs=out_specs,
        compiler_params=_params(("arbitrary",)),
    )(*[r[0] for r in rows], *vecs)
    return res


def _rms(x):
    return x * lax.rsqrt(jnp.mean(x * x, axis=-1, keepdims=True) + EPS)


def _modulate(x, gain, scale, shift):
    return _rms(x) * gain * (1.0 + scale) + shift


def _ada_slices(ada_raw, bias):
    ada = ada_raw + bias
    return [ada[:, i * D_MODEL:(i + 1) * D_MODEL] for i in range(ada.shape[1] // D_MODEL)]


def _norm_wide_heads(y, gain, heads):
    w = y.shape[1] // heads
    return jnp.concatenate([_rms(y[:, h * w:(h + 1) * w]) * gain[:, h * w:(h + 1) * w] for h in range(heads)], axis=1)


def _norm_fox_heads(x, gain):
    outs = []
    for p in range(x.shape[1] // LANES):
        blk = x[:, p * LANES:(p + 1) * LANES]
        low = lax.broadcasted_iota(jnp.int32, blk.shape, 1) < FOX_DH
        sq = blk * blk
        ss_low = jnp.sum(jnp.where(low, sq, 0.0), axis=1, keepdims=True)
        ss_high = jnp.sum(jnp.where(low, 0.0, sq), axis=1, keepdims=True)
        outs.append(blk * lax.rsqrt(jnp.where(low, ss_low, ss_high) * (1.0 / FOX_DH) + EPS))
    return jnp.concatenate(outs, axis=1) * gain


def _silu(x):
    return x * jax.nn.sigmoid(x)


def _log_sigmoid(z):
    return -(jnp.maximum(-z, 0.0) + jnp.log(1.0 + jnp.exp(-jnp.abs(z))))


def _rotate(x, cos, sin, heads, sign):
    w = x.shape[1] // heads
    half = w // 2
    outs = []
    for h in range(heads):
        x1 = x[:, h * w:h * w + half]
        x2 = x[:, h * w + half:(h + 1) * w]
        outs += [x1 * cos - sign * x2 * sin, sign * x1 * sin + x2 * cos]
    return jnp.concatenate(outs, axis=1)


def _vjp(fn, primals, ct):
    return jax.vjp(fn, *primals)[1](ct)


_LOG_GAMMAS = [float(np.log(np.float32(1.0) - np.float32(2.0) ** np.float32(-5.0 - h))) for h in range(RET_HEADS)]


RET_ROWS = 1024


def retention(name, q, k, v, reverse, out_dtype=F32):
    (qa, dk, qo), (ka, _, ko), (va, dv, vo) = q, k, v
    S = qa.shape[0]
    C = RET_CHUNK
    rows = _row_tile(S, RET_ROWS)
    nb = S // rows

    def body(q_ref, k_ref, v_ref, o_ref, state):
        h = pl.program_id(0)

        @pl.when(pl.program_id(1) == 0)
        def _():
            state[...] = jnp.zeros_like(state)

        log_g = jnp.float32(_LOG_GAMMAS[RET_HEADS - 1])
        for i in range(RET_HEADS - 2, -1, -1):
            log_g = jnp.where(h == i, jnp.float32(_LOG_GAMMAS[i]), log_g)
        row = lax.broadcasted_iota(jnp.int32, (C, C), 0)
        col = lax.broadcasted_iota(jnp.int32, (C, C), 1)
        rel = (col - row if reverse else row - col).astype(F32)
        decay = jnp.where(rel >= 0, jnp.exp(log_g * jnp.maximum(rel, 0.0)), 0.0)
        j = lax.broadcasted_iota(jnp.int32, (C, 1), 0).astype(F32)
        q_decay = jnp.exp(log_g * (C - j if reverse else j + 1.0))
        k_decay = jnp.exp(log_g * (j if reverse else C - 1.0 - j))
        chunk_decay = jnp.exp(jnp.full((1, 1), log_g * C, F32))

        chunks = range(rows // C)
        for ci in (reversed(chunks) if reverse else chunks):
            rs = slice(ci * C, (ci + 1) * C)
            qc = q_ref[rs, :].astype(BF16)
            kf = k_ref[rs, :].astype(F32)
            vc = v_ref[rs, :].astype(BF16)
            scores = lax.dot_general(qc, kf.astype(BF16), (((1,), (1,)), ((), ())), preferred_element_type=F32) * decay
            intra = jnp.dot(scores.astype(BF16), vc, preferred_element_type=F32)
            cross = jnp.dot(qc, state[...].astype(BF16), preferred_element_type=F32) * q_decay
            o_ref[rs, :] = (intra + cross).astype(o_ref.dtype)
            upd = lax.dot_general((kf * k_decay).astype(BF16), vc, (((0,), (0,)), ((), ())), preferred_element_type=F32)
            state[...] = state[...] * chunk_decay + upd

    def block(i):
        return nb - 1 - i if reverse else i

    return pl.pallas_call(
        body, name=name, out_shape=jax.ShapeDtypeStruct((S, RET_HEADS * dv), out_dtype),
        grid=(RET_HEADS, nb),
        in_specs=[pl.BlockSpec((rows, dk), lambda h, i: (block(i), qo + h)),
                  pl.BlockSpec((rows, dk), lambda h, i: (block(i), ko + h)),
                  pl.BlockSpec((rows, dv), lambda h, i: (block(i), vo + h))],
        out_specs=pl.BlockSpec((rows, dv), lambda h, i: (block(i), h)),
        scratch_shapes=[pltpu.VMEM((dk, dv), F32)],
        compiler_params=_params(("parallel", "arbitrary")),
    )(qa, ka, va)


FOX_T = 256
N_PAIR = FOX_HEADS // 2
FOX_SCALE = FOX_DH ** -0.5


def _fox_heads(q2):
    low = lax.broadcasted_iota(jnp.int32, (1, LANES), 1) < FOX_DH
    return [(mask, jnp.where(mask, q2 * FOX_SCALE, 0.0).astype(BF16)) for mask in (low, jnp.logical_not(low))]


def _fox_parts(j, t):
    return ([(0, j * t, False)] if j else []) + [(j * t, (j + 1) * t, True)]


def _fox_scores(qa, k_ref, ft_ref, head, lo, hi, diagonal):
    k_blk = k_ref[lo:hi, :].astype(BF16)
    s = lax.dot_general(qa, k_blk, (((1,), (1,)), ((), ())), preferred_element_type=F32) - ft_ref[pl.ds(head, 1), lo:hi]
    if diagonal:
        n = hi - lo
        s = jnp.where(lax.broadcasted_iota(jnp.int32, (n, n), 1) <= lax.broadcasted_iota(jnp.int32, (n, n), 0), s, -jnp.inf)
    return s


def fox_forward(name, qn, kn, kvf, f_cum_t):
    S = qn.shape[0]
    t = _row_tile(S, FOX_T)
    v_block0 = D_MODEL // LANES

    def variant(j, pair, q_ref, k_ref, v_ref, ft_ref, y_ref, lse_ref):
        ys, lses = [], []
        for a, (mask, qa) in enumerate(_fox_heads(q_ref[...])):
            parts = [(lo, hi, _fox_scores(qa, k_ref, ft_ref, 2 * pair + a, lo, hi, dg)) for lo, hi, dg in _fox_parts(j, t)]
            m = functools.reduce(jnp.maximum, [jnp.max(s, axis=1, keepdims=True) for _, _, s in parts])
            l, acc = 0.0, 0.0
            for lo, hi, s in parts:
                e = jnp.exp(s - m)
                l = l + jnp.sum(e, axis=1, keepdims=True)
                acc = acc + jnp.dot(e.astype(BF16), v_ref[lo:hi, :].astype(BF16), preferred_element_type=F32)
            ys.append(acc / l)
            lses.append(m + jnp.log(l))
        low = lax.broadcasted_iota(jnp.int32, (1, LANES), 1) < FOX_DH
        y_ref[...] = jnp.where(low, ys[0], ys[1])
        lse_ref[...] = jnp.where(low, lses[0], lses[1])

    def body(*refs):
        pair, i = pl.program_id(0), pl.program_id(1)
        for j in range(S // t):
            pl.when(i == j)(functools.partial(variant, j, pair, *refs))

    return pl.pallas_call(
        body, name=name,
        out_shape=[jax.ShapeDtypeStruct((S, D_MODEL), F32), jax.ShapeDtypeStruct((S, D_MODEL), F32)],
        grid=(N_PAIR, S // t),
        in_specs=[pl.BlockSpec((t, LANES), lambda p, i: (i, p)),
                  pl.BlockSpec((S, LANES), lambda p, i: (0, p)),
                  pl.BlockSpec((S, LANES), lambda p, i: (0, v_block0 + p)),
                  pl.BlockSpec((LANES, S), lambda p, i: (0, 0))],
        out_specs=[pl.BlockSpec((t, LANES), lambda p, i: (i, p)),
                   pl.BlockSpec((t, LANES), lambda p, i: (i, p))],
        compiler_params=_params(("parallel", "arbitrary")),
    )(qn, kn, kvf, f_cum_t)


def fox_backward(name, qn, kn, kvf, f_cum_t, y, dy, lse):
    S = qn.shape[0]
    t = _row_tile(S, FOX_T)
    v_block0 = D_MODEL // LANES

    def variant(j, pair, q_ref, k_ref, v_ref, ft_ref, y_ref, dy_ref, lse_ref, dq_ref, dk_ref, dv_ref, dfq_ref, dfk_ref,
                dv_acc):
        y2, dy2, lse2 = y_ref[...], dy_ref[...], lse_ref[...]
        lane = lax.broadcasted_iota(jnp.int32, (t, LANES), 1)
        dqs, dfq = [], jnp.zeros((t, LANES), F32)
        for a, (mask, qa) in enumerate(_fox_heads(q_ref[...].astype(F32))):
            lse_a = jnp.max(jnp.where(mask, lse2, -jnp.inf), axis=1, keepdims=True)
            dy_a = jnp.where(mask, dy2, 0.0)
            delta = jnp.sum(dy_a * y2, axis=1, keepdims=True)
            dy_b = dy_a.astype(BF16)
            dq, row_sum = 0.0, 0.0
            for lo, hi, dg in _fox_parts(j, t):
                p = jnp.exp(_fox_scores(qa, k_ref, ft_ref, 2 * pair + a, lo, hi, dg) - lse_a)
                dp = lax.dot_general(dy_b, v_ref[lo:hi, :].astype(BF16), (((1,), (1,)), ((), ())), preferred_element_type=F32)
                ds = p * (dp - delta)
                row_sum = row_sum + jnp.sum(ds, axis=1, keepdims=True)
                dfk_ref[pl.ds(a, 1), lo:hi] += -jnp.sum(ds, axis=0, keepdims=True)
                ds_b = ds.astype(BF16)
                dq = dq + jnp.dot(ds_b, k_ref[lo:hi, :].astype(BF16), preferred_element_type=F32)
                dk_ref[lo:hi, :] += lax.dot_general(ds_b, qa, (((0,), (0,)), ((), ())), preferred_element_type=F32)
                dv_acc[lo:hi, :] += lax.dot_general(p.astype(BF16), dy_b, (((0,), (0,)), ((), ())), preferred_element_type=F32)
            dqs.append(dq * FOX_SCALE)
            dfq = dfq + jnp.where(lane == 2 * pair + a, row_sum, 0.0)
        low = lax.broadcasted_iota(jnp.int32, (1, LANES), 1) < FOX_DH
        dq_ref[...] = jnp.where(low, dqs[0], dqs[1])
        dfq_ref[...] = dfq

    def body(*refs):
        pair, i = pl.program_id(0), pl.program_id(1)
        dk_ref, dv_ref, _, dfk_ref, dv_acc = refs[8:13]

        @pl.when(i == 0)
        def _():
            dk_ref[...] = jnp.zeros_like(dk_ref)
            dv_acc[...] = jnp.zeros_like(dv_acc)
            dfk_ref[...] = jnp.zeros_like(dfk_ref)

        for j in range(S // t):
            pl.when(i == j)(functools.partial(variant, j, pair, *refs))

        @pl.when(i == S // t - 1)
        def _():
            dv_ref[...] = dv_acc[...].astype(dv_ref.dtype)

    row_blk = pl.BlockSpec((t, LANES), lambda p, i: (i, p))
    col_blk = pl.BlockSpec((S, LANES), lambda p, i: (0, p))
    return pl.pallas_call(
        body, name=name,
        out_shape=[jax.ShapeDtypeStruct((S, D_MODEL), F32)] * 2
        + [jax.ShapeDtypeStruct((S, D_MODEL), BF16), jax.ShapeDtypeStruct((N_PAIR, S, LANES), F32),
           jax.ShapeDtypeStruct((N_PAIR, 8, S), F32)],
        grid=(N_PAIR, S // t),
        in_specs=[row_blk, col_blk,
                  pl.BlockSpec((S, LANES), lambda p, i: (0, v_block0 + p)),
                  pl.BlockSpec((LANES, S), lambda p, i: (0, 0)),
                  row_blk, row_blk, row_blk],
        out_specs=[row_blk, col_blk, col_blk,
                   pl.BlockSpec((None, t, LANES), lambda p, i: (p, i, 0)),
                   pl.BlockSpec((None, 8, S), lambda p, i: (p, 0, 0))],
        scratch_shapes=[pltpu.VMEM((S, LANES), F32)],
        compiler_params=_params(("parallel", "arbitrary")),
    )(qn, kn, kvf, f_cum_t, y, dy, lse)


def cumsum_rows(name, x, reverse):
    S = x.shape[0]
    C = LANES
    nc = S // C

    def body(x_ref, o_ref):
        row = lax.broadcasted_iota(jnp.int32, (C, C), 0)
        col = lax.broadcasted_iota(jnp.int32, (C, C), 1)
        tri = jnp.where(col >= row if reverse else col <= row, 1.0, 0.0).astype(F32)
        carry = jnp.zeros((1, LANES), F32)
        for i in (range(nc - 1, -1, -1) if reverse else range(nc)):
            blk = x_ref[i * C:(i + 1) * C, :]
            loc = jnp.dot(tri, blk, preferred_element_type=F32, precision=lax.Precision.HIGHEST)
            o_ref[i * C:(i + 1) * C, :] = loc + carry
            carry = carry + (loc[0:1, :] if reverse else loc[C - 1:C, :])

    return pl.pallas_call(body, name=name, out_shape=jax.ShapeDtypeStruct((S, LANES), F32),
                          compiler_params=_params())(x)


def adamw(name, parts, w, m, v):
    L, R, C = w.shape
    tr = _row_tile(R, 256)
    nr = R // tr
    counts = [len(p) for p in parts]

    def body(*refs):
        w_ref, m_ref, v_ref, g_out, d_out, m_out, v_out = refs[sum(counts):]
        for layer in range(L):
            p_refs = refs[sum(counts[:layer]):sum(counts[:layer + 1])]

            @pl.when(pl.program_id(0) == layer)
            def _(p_refs=p_refs, slots=[n for _, n in parts[layer]]):
                g = None
                for p_ref, n in zip(p_refs, slots):
                    for i in range(n):
                        g = p_ref[i].astype(F32) if g is None else g + p_ref[i].astype(F32)
                m2 = ADAM_B1 * m_ref[...] + (1.0 - ADAM_B1) * g
                v2 = ADAM_B2 * v_ref[...] + (1.0 - ADAM_B2) * jnp.square(g)
                m_hat = m2 / (1.0 - ADAM_B1 ** ADAM_STEP)
                v_hat = v2 / (1.0 - ADAM_B2 ** ADAM_STEP)
                g_out[...] = g
                d_out[...] = -ADAM_LR * (m_hat / (jnp.sqrt(v_hat) + ADAM_EPS) + ADAM_WD * w_ref[...])
                m_out[...] = m2
                v_out[...] = v2

    def part_spec(layer, n):
        return pl.BlockSpec((n, tr, C), lambda l, i: (0, jnp.where(l == layer, i, jnp.where(l < layer, 0, nr - 1)), 0))

    blk = pl.BlockSpec((None, tr, C), lambda l, i: (l, i, 0))
    return pl.pallas_call(
        body, name=name, out_shape=[jax.ShapeDtypeStruct((L, R, C), F32)] * 4, grid=(L, nr),
        in_specs=[part_spec(layer, n) for layer in range(L) for _, n in parts[layer]] + [blk, blk, blk],
        out_specs=[blk] * 4, compiler_params=_params(("arbitrary", "arbitrary")),
    )(*[a for layer in parts for a, _ in layer], w, m, v)


def kernel(x, c, positions, norm_mix_gain, norm_mlp_gain, w_ada, b_ada, w_mlp_in, w_mlp_out, ret_w_in, ret_norm_gain, ret_w_out, kv_norm_gain, kv_w_ada, kv_b_ada, kv_w, forget_bias, k_norm_gain, fox_w_in, q_norm_gain, fox_w_out, loss_target, m_norm_mix_gain, m_norm_mlp_gain, m_w_ada, m_b_ada, m_w_mlp_in, m_w_mlp_out, m_ret_w_in, m_ret_norm_gain, m_ret_w_out, m_kv_norm_gain, m_kv_w_ada, m_kv_b_ada, m_kv_w, m_forget_bias, m_k_norm_gain, m_fox_w_in, m_q_norm_gain, m_fox_w_out, v_norm_mix_gain, v_norm_mlp_gain, v_w_ada, v_b_ada, v_w_mlp_in, v_w_mlp_out, v_ret_w_in, v_ret_norm_gain, v_ret_w_out, v_kv_norm_gain, v_kv_w_ada, v_kv_b_ada, v_kv_w, v_forget_bias, v_k_norm_gain, v_fox_w_in, v_q_norm_gain, v_fox_w_out):
    D = D_MODEL
    S = x.shape[1]
    x0 = x.reshape(S, D)
    target = loss_target.reshape(S, D)
    me = 4 * lax.axis_index("x") + 2 * lax.axis_index("y") + lax.axis_index("c")
    n_ada = w_ada.shape[2]
    n_kvada = kv_w_ada.shape[1]
    n_kv = kv_w.shape[1]

    c_all, ret_gain = all_gather("gather_c", [c.reshape(D // LANES, LANES), ret_norm_gain.reshape(RET_HEADS, -1)])
    ret_gain = jnp.transpose(ret_gain, (1, 0, 2)).reshape(1, RET_HEADS * RET_V)
    c_act = rowwise("silu_c", _silu, [c_all.reshape(N_DEV, D)], [], [(D, F32)])[0]
    w_ada_cat = jnp.concatenate([w_ada[0], w_ada[1], kv_w_ada], axis=1).astype(BF16)[None]
    n_cat = 2 * n_ada + n_kvada
    ada_part = mm_nn("ada_proj", c_act, w_ada_cat)[0]
    ada_mine = all_to_all("ada_rows", [ada_part.reshape(N_DEV, n_cat // LANES, LANES)])[0]
    ada_mine = ada_mine.reshape(N_DEV, n_cat)
    ada_raw = [ada_mine[:, l * n_ada:(l + 1) * n_ada].reshape(1, 6 * D) for l in range(2)]
    kvada_raw = ada_mine[:, 2 * n_ada:].reshape(1, 2 * D)
    kv_bias = kv_b_ada.reshape(1, 2 * D)
    kv_gain = kv_norm_gain.reshape(1, D)
    fb = jnp.pad(forget_bias.reshape(1, FOX_HEADS), ((0, 0), (0, LANES - FOX_HEADS)))
    k_gain = jnp.tile(k_norm_gain.reshape(1, FOX_DH), (1, FOX_HEADS))
    q_gain = jnp.tile(q_norm_gain.reshape(1, FOX_DH), (1, FOX_HEADS))

    w_names = ["ret_in", "ret_out", "mlp_in0", "mlp_out0", "kv", "fox_in", "fox_out", "mlp_in1", "mlp_out1"]
    shards = [ret_w_in[0].astype(BF16), ret_w_out[0].astype(BF16), w_mlp_in[0].astype(BF16), w_mlp_out[0].astype(BF16),
              kv_w.astype(BF16), fox_w_in[0].astype(BF16), fox_w_out[0].astype(BF16), w_mlp_in[1].astype(BF16),
              w_mlp_out[1].astype(BF16)]
    two_level = {"ret_in", "ret_out", "mlp_in0", "mlp_out0"}
    w_plans = {name: plan_gather if name in two_level else plan_gather_direct for name in w_names}
    w_handles, token = exchange_start("gather_weights_start", shards,
                                      [lax.empty((N_DEV,) + a.shape, a.dtype) for a in shards],
                                      [w_plans[name] for name in w_names], after=(ada_mine, ret_gain))
    w_handles = dict(zip(w_names, w_handles))

    forwards = {}

    def forward_early(name, after):
        arrived = exchange_wait("gather_wait_" + name, [w_handles[name]], w_plans[name], after)[1]
        forwards[name], tok = exchange_start("gather_forward_start_" + name, None, arrived, plan_forward)
        return tok

    def weight(name, after):
        if name in two_level:
            return exchange_wait("gather_forward_wait_" + name, forwards[name], plan_forward, after)[1][0]
        return exchange_wait("gather_wait_" + name, [w_handles[name]], w_plans[name], after)[1][0]

    pos = positions.reshape(S, 1).astype(F32)
    half = RET_QK // 2
    inv_freq = jnp.asarray((ROPE_BASE ** (-np.arange(half, dtype=np.float32) / half)).reshape(1, half), F32)

    def angles(p, f):
        ang = p * f
        return jnp.cos(ang), jnp.sin(ang)

    cos, sin = rowwise("rope_table", angles, [pos], [inv_freq], [(half, F32), (half, F32)])

    def mod_mix(layer):
        def fn(xb, ada, bias, gain):
            sh, sc = _ada_slices(ada, bias)[:2]
            return _modulate(xb, gain[layer:layer + 1], sc, sh)
        return fn

    def mod_mlp(layer):
        def fn(xb, ada, bias, gain):
            sh, sc = _ada_slices(ada, bias)[3:5]
            return _modulate(xb, gain[layer:layer + 1], sc, sh)
        return fn

    h1_0 = rowwise("mod_mix0", mod_mix(0), [x0], [ada_raw[0], b_ada[0:1], norm_mix_gain], [(D, BF16)], after=token)[0]
    W_ret_in = weight("ret_in", forward_early("ret_in", h1_0))
    proj = mm_nn("ret_proj", h1_0, W_ret_in, (BF16,))[0]
    token = forward_early("ret_out", proj)

    def rope_fwd(qb, kb, cs, sn):
        return (_rotate(qb.astype(F32), cs, sn, RET_HEADS, 1.0),
                _rotate(kb.astype(F32), cs, sn, RET_HEADS, 1.0) * (RET_QK ** -0.5))

    q_rot, k_rot = rowwise("rope", rope_fwd, [(proj, D, 0), (proj, D, 1), cos, sin], [], [(D, BF16), (D, BF16)],
                           after=token)
    v_ret = (proj, RET_V, (2 * D) // RET_V)
    y_ret = retention("ret_fwd", (q_rot, RET_QK, 0), (k_rot, RET_QK, 0), v_ret, reverse=False)

    def ret_gate(yb, gb, gain):
        return _silu(gb.astype(F32)) * _norm_wide_heads(yb, gain, RET_HEADS)

    mixin0 = rowwise("ret_gate", ret_gate, [y_ret, (proj, 2 * D, 2)], [ret_gain], [(2 * D, BF16)])[0]
    W_ret_out = weight("ret_out", mixin0).reshape(1, 2 * D, D)
    FUSED_ROWS = 512

    def residual_mod(layer, slot):
        def fn(xb, bb, ada, bias, gain):
            s = _ada_slices(ada, bias)
            xn = xb + s[2] * bb
            return xn, _modulate(xn, gain[layer:layer + 1], s[4], s[3])
        return fn

    def with_residual(fn):
        return lambda acc, xb, *vs: (acc,) + tuple(fn(xb, acc, *vs))

    mix0, x1, h2_0 = mm_nn("ret_out", mixin0, W_ret_out, (F32, F32, BF16), with_residual(residual_mod(0, 0)), (x0,),
                           forward_early("mlp_in0", W_ret_out), (ada_raw[0], b_ada[0:1], norm_mlp_gain), FUSED_ROWS)

    W_mlp_in, W_mlp_out = {}, {}

    def mlp_forward(tag, h2, layer, out_dtypes=(F32,), epilogue=None, extra=(), vecs=()):
        W_mlp_in[layer] = weight("mlp_in" + tag, h2)
        early = forward_early("mlp_out" + tag, W_mlp_in[layer]) if "mlp_out" + tag in two_level else None
        u, act = mm_nn("mlp_in" + tag, h2, W_mlp_in[layer], (BF16, BF16), after=early,
                       epilogue=lambda acc: (acc, jnp.square(jnp.maximum(acc, 0.0))))
        W_mlp_out[layer] = weight("mlp_out" + tag, act).reshape(1, 4 * D, D)
        return u, act, mm_nn("mlp_out" + tag, act, W_mlp_out[layer], out_dtypes, epilogue, extra, None, vecs,
                             FUSED_ROWS if epilogue else 2048)

    def res_mlp0(xb, bb, ada0, bias0, ada1, bias1, kva, kvb, gain_mix, gain_kv):
        xn = xb + _ada_slices(ada0, bias0)[5] * bb
        s1 = _ada_slices(ada1, bias1)
        kv_shift, kv_scale = _ada_slices(kva, kvb)
        return xn, _modulate(xn, gain_kv, kv_scale, kv_shift), _modulate(xn, gain_mix[1:2], s1[1], s1[0])

    u0, act0, (mlp0,) = mlp_forward("0", h2_0, 0)
    x2, h_kv, h1_1 = rowwise("res_mlp0", res_mlp0, [x1, mlp0],
                             [ada_raw[0], b_ada[0:1], ada_raw[1], b_ada[1:2], kvada_raw, kv_bias, norm_mix_gain, kv_gain],
                             [(D, F32), (D, BF16), (D, BF16)])

    kv_full = jnp.transpose(weight("kv", h_kv), (1, 0, 2)).reshape(D, N_DEV * n_kv)
    W_kv = jnp.pad(kv_full, ((0, 0), (0, KV_PAD - N_DEV * n_kv)))[None]
    kvf = mm_nn("kv_proj", h_kv, W_kv)[0]

    def kv_post(kb, fblk, kg, bias):
        head = lax.broadcasted_iota(jnp.int32, fblk.shape, 1) < FOX_HEADS
        return _norm_fox_heads(kb, kg), jnp.where(head, _log_sigmoid(fblk + bias), 0.0)

    kn, log_f = rowwise("kv_post", kv_post, [(kvf, D, 0), (kvf, LANES, 2 * D // LANES)], [k_gain, fb],
                        [(D, BF16), (LANES, F32)])
    f_cum = cumsum_rows("f_cumsum", log_f, reverse=False)
    f_cum_t = f_cum.T

    W_fox_in = weight("fox_in", kvf)
    qo = mm_nn("fox_proj", h1_1, W_fox_in)[0]
    qn = rowwise("q_norm", _norm_fox_heads, [(qo, D, 0)], [q_gain], [(D, BF16)])[0]
    y_att, lse = fox_forward("fox_fwd", qn, kn, kvf, f_cum_t)
    mixin1 = rowwise("fox_gate", lambda ob, yb: jax.nn.sigmoid(ob) * yb, [(qo, D, 1), y_att], [], [(D, BF16)])[0]
    W_fox_out = weight("fox_out", mixin1).reshape(1, D, D)
    mix1, x3, h2_1 = mm_nn("fox_out", mixin1, W_fox_out, (F32, F32, BF16), with_residual(residual_mod(1, 0)), (x2,),
                           None, (ada_raw[1], b_ada[1:2], norm_mlp_gain), FUSED_ROWS)
    u1, act1, (mlp1,) = mlp_forward("1", h2_1, 1)

    def scatter_start(tag, gws, after=()):
        lands = [lax.empty((4,) + g.shape[1:], g.dtype) for g in gws]
        return exchange_start("scatter_sibling_start_" + tag, gws, lands, plan_to_sibling, after)

    def scatter_relay(tag, handles, after, start_after=()):
        gws, from_sibling = exchange_wait("scatter_sibling_wait_" + tag, handles, plan_to_sibling, after)
        sums = [chip_sum("chip_sum_%s%d" % (tag, i), g, s) for i, (g, s) in enumerate(zip(gws, from_sibling))]
        lands = [lax.empty((3,) + s.shape[1:], s.dtype) for s in sums]
        return exchange_start("scatter_owner_start_" + tag, sums, lands, plan_to_owners, start_after)

    def scatter_direct_start(tag, gws):
        lands = [lax.empty(g.shape, g.dtype) for g in gws]
        return exchange_start("scatter_direct_start_" + tag, gws, lands, plan_scatter_direct)

    def scatter_direct_finish(tag, handles, after):
        return [[(r, N_DEV)] for r in exchange_wait("scatter_direct_wait_" + tag, handles, plan_scatter_direct, after)[1]]

    def scatter_finish(tag, handles, after):
        sums, received = exchange_wait("scatter_owner_wait_" + tag, handles, plan_to_owners, after)
        return [[(s, 1), (r, 3)] for s, r in zip(sums, received)]

    def loss_head(xb, bb, tb, ada, bias):
        g2 = _ada_slices(ada, bias)[5]
        err = xb + g2 * bb - tb
        dx = err * (1.0 / D)
        loss = 0.5 * jnp.sum(jnp.sum(err * err, axis=1, keepdims=True) * (1.0 / D), axis=0, keepdims=True)
        return dx, (dx * g2), jnp.broadcast_to(loss, (1, LANES)), jnp.sum(dx * bb, axis=0, keepdims=True)

    dx4, dmlp1, loss_acc, dg2_1 = rowwise("loss_head", loss_head, [x3, mlp1, target], [ada_raw[1], b_ada[1:2]],
                                          [(D, F32), (D, BF16)], [(1, LANES), (1, D)])

    def mlp_backward(tag, dmlp, act, u, h2, layer, after=None):
        du = mm_nt("mlp_out_dx" + tag, dmlp, W_mlp_out[layer], (BF16,), extra=(u,), after=after,
                   epilogue=lambda acc, ub: (acc * (2.0 * jnp.maximum(ub.astype(F32), 0.0)),))[0]
        gw_out = mm_tn("mlp_out_dw" + tag, act, dmlp, 1, BF16).reshape(N_DEV, -1, D)
        dh2 = mm_nt("mlp_in_dx" + tag, du, W_mlp_in[layer], (BF16,))[0]
        gw_in = mm_tn("mlp_in_dw" + tag, h2, du, N_DEV, BF16)
        return dh2, gw_in, gw_out

    def mod_backward(layer, slots, gate_slot):
        def fn(xb, dhb, dresb, branchb, ada, bias, gain):
            s = _ada_slices(ada, bias)
            g = gain[layer:layer + 1]
            dx, dgain, dsc, dsh = _vjp(_modulate, (xb, g, s[slots[1]], s[slots[0]]), dhb.astype(F32))
            dx = dx + dresb
            d_branch = dx * s[gate_slot]
            return dx, d_branch, dgain, dsc, dsh, jnp.sum(dx * branchb, axis=0, keepdims=True)
        return fn

    vec = (1, D)
    dh2_1, gw_mlp_in1, gw_mlp_out1 = mlp_backward("1", dmlp1, act1, u1, h2_1, 1)
    scat_a1, token_a1 = scatter_direct_start("a1", [gw_mlp_in1, gw_mlp_out1])
    dx3, dmix1, dgain_mlp1, dsc2_1, dsh2_1, dg1_1 = rowwise(
        "mod_mlp1_bwd", mod_backward(1, (3, 4), 2), [x3, dh2_1, dx4, mix1], [ada_raw[1], b_ada[1:2], norm_mlp_gain],
        [(D, F32), (D, BF16)], [vec] * 4, after=token_a1)
    dmixin1 = mm_nt("fox_out_dx", dmix1, W_fox_out)[0]
    gw_fox_out = mm_tn("fox_out_dw", mixin1, dmix1, 1, BF16).reshape(N_DEV, -1, D)

    def fox_gate_bwd(db, ob, yb):
        sg = jax.nn.sigmoid(ob)
        return db * sg, db * yb * sg * (1.0 - sg)

    dy_att, d_og = rowwise("fox_gate_bwd", fox_gate_bwd, [dmixin1, (qo, D, 1), y_att], [], [(D, F32), (D, BF16)])
    dqn, dkn, dv_att, dfq, dfk = fox_backward("fox_bwd", qn, kn, kvf, f_cum_t, y_att, dy_att, lse)

    def q_norm_bwd(qb, db, ogb, gain):
        dq, dgain = _vjp(_norm_fox_heads, (qb, gain), db)
        return jnp.concatenate([dq, ogb.astype(F32)], axis=1), dgain

    dqo, dq_gain = rowwise("q_norm_bwd", q_norm_bwd, [(qo, D, 0), dqn, d_og], [q_gain], [(2 * D, BF16)], [vec])
    dh1_1 = mm_nt("fox_proj_dx", dqo, W_fox_in, (BF16,))[0]
    gw_fox_in = mm_tn("fox_proj_dw", h1_1, dqo, N_DEV, BF16)

    dfk_rows = jnp.pad(dfk[:, :2, :].reshape(FOX_HEADS, S).T, ((0, 0), (0, LANES - FOX_HEADS)))

    def df_total(*blks):
        tot = blks[0]
        for b in blks[1:]:
            tot = tot + b
        return tot

    d_fcum = rowwise("df_sum", df_total, [dfk_rows] + [(dfq, LANES, 0, p) for p in range(N_PAIR)], [], [(LANES, F32)])[0]
    d_logf = cumsum_rows("df_cumsum", d_fcum, reverse=True)

    def kv_post_bwd(kb, fblk, dkb, dvb, dlf, kg, bias):
        dk, dgain = _vjp(_norm_fox_heads, (kb, kg), dkb)
        df = dlf * (1.0 / (1.0 + jnp.exp(fblk + bias)))
        return jnp.concatenate([dk, dvb.astype(F32), df], axis=1), dgain, jnp.sum(df, axis=0, keepdims=True)

    dkvf, dk_gain, dfb = rowwise("kv_post_bwd", kv_post_bwd,
                                 [(kvf, D, 0), (kvf, LANES, 2 * D // LANES), dkn, dv_att, d_logf], [k_gain, fb],
                                 [(KV_PAD, BF16)], [vec, (1, LANES)])
    dh_kv = mm_nt("kv_proj_dx", dkvf, W_kv, (BF16,))[0]
    gw_kv = mm_tn("kv_proj_dw", h_kv, dkvf, 1, BF16)[0, :, :N_DEV * n_kv]
    gw_kv = jnp.transpose(gw_kv.reshape(D, N_DEV, n_kv), (1, 0, 2))
    scat_a2, token_a = scatter_direct_start("a2", [gw_fox_out, gw_fox_in, gw_kv])

    def x2_bwd(xb, dh1b, dhkb, dresb, branchb, ada0, bias0, ada1, bias1, kva, kvb, gain_mix, gain_kv):
        s1 = _ada_slices(ada1, bias1)
        kv_shift, kv_scale = _ada_slices(kva, kvb)
        dxa, dgain_mix, dsc1, dsh1 = _vjp(_modulate, (xb, gain_mix[1:2], s1[1], s1[0]), dh1b.astype(F32))
        dxb, dgain_kv, dkv_scale, dkv_shift = _vjp(_modulate, (xb, gain_kv, kv_scale, kv_shift), dhkb.astype(F32))
        dx = dresb + dxa + dxb
        g2 = _ada_slices(ada0, bias0)[5]
        return (dx, dx * g2, dgain_mix, dsc1, dsh1, dgain_kv, dkv_scale, dkv_shift,
                jnp.sum(dx * branchb, axis=0, keepdims=True))

    (dx2, dmlp0, dgain_mix1, dsc1_1, dsh1_1, dgain_kv, dkv_scale, dkv_shift, dg2_0) = rowwise(
        "x2_bwd", x2_bwd, [x2, dh1_1, dh_kv, dx3, mlp0],
        [ada_raw[0], b_ada[0:1], ada_raw[1], b_ada[1:2], kvada_raw, kv_bias, norm_mix_gain, kv_gain],
        [(D, F32), (D, BF16)], [vec] * 7, after=token_a)

    dh2_0, gw_mlp_in0, gw_mlp_out0 = mlp_backward("0", dmlp0, act0, u0, h2_0, 0)
    scat_b1, token_b1 = scatter_direct_start("b1", [gw_mlp_in0, gw_mlp_out0])
    dx1, dmix0, dgain_mlp0, dsc2_0, dsh2_0, dg1_0 = rowwise(
        "mod_mlp0_bwd", mod_backward(0, (3, 4), 2), [x1, dh2_0, dx2, mix0], [ada_raw[0], b_ada[0:1], norm_mlp_gain],
        [(D, F32), (D, BF16)], [vec] * 4, after=token_b1)
    dmixin0 = mm_nt("ret_out_dx", dmix0, W_ret_out, (BF16,))[0]
    gw_ret_out = mm_tn("ret_out_dw", mixin0, dmix0, 1, BF16).reshape(N_DEV, -1, D)
    scat_b2, token_b = scatter_direct_start("b2", [gw_ret_out])

    def ret_gate_bwd(db, yb, gb, gain):
        return _vjp(ret_gate, (yb, gb.astype(F32), gain), db.astype(F32))

    dy_ret, dgate, dret_gain = rowwise("ret_gate_bwd", ret_gate_bwd, [dmixin0, y_ret, (proj, 2 * D, 2)], [ret_gain],
                                       [(2 * D, BF16), (2 * D, BF16)], [(1, 2 * D)], after=token_b)
    dy_h = (dy_ret, RET_V, 0)
    dq_rot = retention("ret_dq", dy_h, v_ret, (k_rot, RET_QK, 0), reverse=False)
    dk_rot = retention("ret_dk", v_ret, dy_h, (q_rot, RET_QK, 0), reverse=True)
    dv_ret = retention("ret_dv", (k_rot, RET_QK, 0), (q_rot, RET_QK, 0), dy_h, reverse=True, out_dtype=BF16)

    def rope_bwd(dqb, dkb, dvb, dgb, cs, sn):
        dq = _rotate(dqb, cs, sn, RET_HEADS, -1.0)
        dk = _rotate(dkb, cs, sn, RET_HEADS, -1.0) * (RET_QK ** -0.5)
        return jnp.concatenate([dq, dk, dvb.astype(F32), dgb.astype(F32)], axis=1)

    dproj = rowwise("rope_bwd", rope_bwd, [dq_rot, dk_rot, dv_ret, dgate, cos, sin], [], [(6 * D, BF16)])[0]
    gw_ret_in = mm_tn("ret_proj_dw", h1_0, dproj, N_DEV, BF16)
    scat_c, token_c = scatter_start("c", [gw_ret_in])
    dh1_0 = mm_nt("ret_proj_dx", dproj, W_ret_in, (BF16,), after=token_c)[0]

    def x0_bwd(xb, dhb, dresb, ada, bias, gain):
        s = _ada_slices(ada, bias)
        dx, dgain, dsc, dsh = _vjp(_modulate, (xb, gain[0:1], s[1], s[0]), dhb.astype(F32))
        return dx + dresb, dgain, dsc, dsh

    grad_x, dgain_mix0, dsc1_0, dsh1_0 = rowwise("x0_bwd", x0_bwd, [x0, dh1_0, dx1],
                                                 [ada_raw[0], b_ada[0:1], norm_mix_gain], [(D, F32)], [vec] * 3)

    small = jnp.concatenate([
        dsh1_0, dsc1_0, dg1_0, dsh2_0, dsc2_0, dg2_0,
        dsh1_1, dsc1_1, dg1_1, dsh2_1, dsc2_1, dg2_1,
        dkv_shift, dkv_scale,
        dgain_mix0, dgain_mix1, dgain_mlp0, dgain_mlp1, dgain_kv,
        dret_gain,
        dq_gain.reshape(FOX_HEADS, FOX_DH).sum(axis=0).reshape(1, FOX_DH),
        dk_gain.reshape(FOX_HEADS, FOX_DH).sum(axis=0).reshape(1, FOX_DH),
        dfb,
        loss_acc,
    ], axis=1)
    small_all = all_gather("gather_small", [small.reshape(-1, LANES)])[0].reshape(N_DEV, 1, -1)
    loss = jnp.sum(small_all[:, 0, -1])
    scat_c, token_c = scatter_relay("c", scat_c, grad_x, start_after=(small_all,))
    o_ada = 14 * D
    d_ada = small_all[:, 0, :o_ada]
    d_cat = jnp.concatenate([
        lax.dynamic_slice_in_dim(d_ada[:, 0:6 * D], me * n_ada, n_ada, axis=1),
        lax.dynamic_slice_in_dim(d_ada[:, 6 * D:12 * D], me * n_ada, n_ada, axis=1),
        lax.dynamic_slice_in_dim(d_ada[:, 12 * D:14 * D], me * n_kvada, n_kvada, axis=1)], axis=1)
    gw_ada_cat = mm_tn("ada_dw", c_act, d_cat, 1, F32, after=token_c)[0]

    results = {}

    def update(name, parts, w, m, v, layers=1):
        per_layer = parts if layers > 1 else [parts]
        shape = w.shape
        C = shape[-1]
        R = int(np.prod(shape)) // (layers * C)
        per_layer = [p if isinstance(p, list) else [(p, p.shape[0])] for p in per_layer]
        per_layer = [[(a.reshape(a.shape[0], R, C), n) for a, n in p] for p in per_layer]
        outs = adamw("adamw_" + name, per_layer, w.reshape(layers, R, C), m.reshape(layers, R, C), v.reshape(layers, R, C))
        results[name] = tuple(t.reshape(shape) for t in outs)

    def small_parts(lo, width):
        return small_all[:, :, lo:lo + width]

    update("norm_mix_gain", jnp.concatenate([small_parts(o_ada, D), small_parts(o_ada + D, D)], axis=1),
           norm_mix_gain, m_norm_mix_gain, v_norm_mix_gain)
    update("norm_mlp_gain", jnp.concatenate([small_parts(o_ada + 2 * D, D), small_parts(o_ada + 3 * D, D)], axis=1),
           norm_mlp_gain, m_norm_mlp_gain, v_norm_mlp_gain)
    update("w_ada", [gw_ada_cat[None, :, :n_ada], gw_ada_cat[None, :, n_ada:2 * n_ada]], w_ada, m_w_ada, v_w_ada, layers=2)
    update("b_ada", jnp.concatenate([small_parts(0, 6 * D), small_parts(6 * D, 6 * D)], axis=1), b_ada, m_b_ada, v_b_ada)
    o_ret = o_ada + 5 * D
    n_rg = ret_norm_gain.shape[2]
    ret_gain_parts = small_parts(o_ret, 2 * D).reshape(N_DEV, RET_HEADS, RET_V)
    ret_gain_parts = lax.dynamic_slice_in_dim(ret_gain_parts, me * n_rg, n_rg, axis=2)
    update("ret_norm_gain", ret_gain_parts, ret_norm_gain, m_ret_norm_gain, v_ret_norm_gain)
    update("kv_norm_gain", small_parts(o_ada + 4 * D, D), kv_norm_gain, m_kv_norm_gain, v_kv_norm_gain)
    update("kv_w_ada", gw_ada_cat[None, :, 2 * n_ada:], kv_w_ada, m_kv_w_ada, v_kv_w_ada)
    update("kv_b_ada", small_parts(12 * D, 2 * D), kv_b_ada, m_kv_b_ada, v_kv_b_ada)
    o_q = o_ret + 2 * D
    update("forget_bias", small_parts(o_q + 2 * FOX_DH, FOX_HEADS), forget_bias, m_forget_bias, v_forget_bias)
    update("k_norm_gain", small_parts(o_q + FOX_DH, FOX_DH), k_norm_gain, m_k_norm_gain, v_k_norm_gain)
    update("q_norm_gain", small_parts(o_q, FOX_DH), q_norm_gain, m_q_norm_gain, v_q_norm_gain)

    r_mlp_in1, r_mlp_out1 = scatter_direct_finish("a1", scat_a1, results["q_norm_gain"][1])
    r_fox_out, r_fox_in, r_kv = scatter_direct_finish("a2", scat_a2, r_mlp_in1[0][0])
    r_mlp_in0, r_mlp_out0 = scatter_direct_finish("b1", scat_b1, r_kv[0][0])
    r_ret_out = scatter_direct_finish("b2", scat_b2, r_mlp_in0[0][0])[0]
    update("kv_w", r_kv, kv_w, m_kv_w, v_kv_w)
    update("fox_w_in", r_fox_in, fox_w_in, m_fox_w_in, v_fox_w_in)
    update("fox_w_out", r_fox_out, fox_w_out, m_fox_w_out, v_fox_w_out)
    update("ret_w_out", r_ret_out, ret_w_out, m_ret_w_out, v_ret_w_out)
    update("w_mlp_in", [r_mlp_in0, r_mlp_in1], w_mlp_in, m_w_mlp_in, v_w_mlp_in, layers=2)
    update("w_mlp_out", [r_mlp_out0, r_mlp_out1], w_mlp_out, m_w_mlp_out, v_w_mlp_out, layers=2)
    r_ret_in = scatter_finish("c", scat_c, results["w_mlp_out"][1])[0]
    update("ret_w_in", r_ret_in, ret_w_in, m_ret_w_in, v_ret_w_in)

    order = ["norm_mix_gain", "norm_mlp_gain", "w_ada", "b_ada", "w_mlp_in", "w_mlp_out", "ret_w_in", "ret_norm_gain",
             "ret_w_out", "kv_norm_gain", "kv_w_ada", "kv_b_ada", "kv_w", "forget_bias", "k_norm_gain", "fox_w_in",
             "q_norm_gain", "fox_w_out"]
    out = [loss, grad_x.reshape(x.shape)]
    for slot in range(4):
        out += [results[n][slot] for n in order]
    return tuple(out)
```

```python
import functools
import math

import numpy as np
import jax
import jax.numpy as jnp
from jax import lax
from jax.experimental import pallas as pl
from jax.experimental.pallas import tpu as pltpu

F32 = jnp.float32
BF16 = jnp.bfloat16

N_DEV = 8
D_MODEL = 1024
RET_HEADS = 4
RET_QK = D_MODEL // RET_HEADS
RET_V = 2 * D_MODEL // RET_HEADS
RET_CHUNK = 128
ROPE_BASE = 10000.0
FOX_HEADS = 16
FOX_DH = D_MODEL // FOX_HEADS
EPS = 1e-6
LANES = 128
KV_PAD = 2 * D_MODEL + LANES

ADAM_LR = 0.001
ADAM_B1 = 0.9
ADAM_B2 = 0.999
ADAM_EPS = 1e-08
ADAM_WD = 0.01
ADAM_STEP = 10

VMEM_LIMIT_BYTES = 56 * 1024 * 1024


def _params(sem=None):
    return pltpu.CompilerParams(dimension_semantics=sem, vmem_limit_bytes=VMEM_LIMIT_BYTES)


def _me():
    return lax.axis_index("x"), lax.axis_index("y"), lax.axis_index("c")


def _peer(k):
    x, y, c = _me()
    return (1 - x if k & 4 else x, 1 - y if k & 2 else y, 1 - c if k & 1 else c)


def _peer_index(k):
    px, py, pc = _peer(k)
    return 4 * px + 2 * py + pc


def _exchange(name, xs, scatter):
    n = len(xs)

    def body(*refs):
        x_refs, o_refs = refs[:n], refs[n:2 * n]
        send_sems, recv_sems, local_sems = refs[2 * n:]
        x, y, c = _me()
        me = 4 * x + 2 * y + c
        local = []
        for i in range(n):
            src = x_refs[i].at[me] if scatter else x_refs[i]
            cp = pltpu.make_async_copy(src, o_refs[i].at[me], local_sems.at[i])
            cp.start()
            local.append(cp)
        remote = []
        for k in range(1, N_DEV):
            for i in range(n):
                src = x_refs[i].at[_peer_index(k)] if scatter else x_refs[i]
                cp = pltpu.make_async_remote_copy(
                    src_ref=src, dst_ref=o_refs[i].at[me],
                    send_sem=send_sems.at[(k - 1) * n + i], recv_sem=recv_sems.at[(k - 1) * n + i],
                    device_id=_peer(k), device_id_type=pl.DeviceIdType.MESH)
                cp.start()
                remote.append(cp)
        for cp in remote:
            cp.wait()
        for cp in local:
            cp.wait()

    out_shape = [jax.ShapeDtypeStruct(x.shape if scatter else (N_DEV,) + x.shape, x.dtype) for x in xs]
    any_spec = pl.BlockSpec(memory_space=pl.ANY)
    return pl.pallas_call(
        body, name=name, out_shape=out_shape,
        in_specs=[any_spec] * n, out_specs=[any_spec] * n,
        scratch_shapes=[pltpu.SemaphoreType.DMA(((N_DEV - 1) * n,)),
                        pltpu.SemaphoreType.DMA(((N_DEV - 1) * n,)),
                        pltpu.SemaphoreType.DMA((n,))],
    )(*xs)


def all_gather(name, xs):
    return _exchange(name, xs, scatter=False)


def all_to_all(name, xs):
    return _exchange(name, xs, scatter=True)


_HBM = pl.BlockSpec(memory_space=pltpu.HBM)
_SEM = pl.BlockSpec(memory_space=pltpu.SEMAPHORE)
_ANY = pl.BlockSpec(memory_space=pl.ANY)
_EFFECT = pltpu.SideEffectType.DATAFLOW_SIDE_EFFECTING

SIBLING = 1
CHIP_PEERS = (2, 4, 6)


def _my_index():
    x, y, c = _me()
    return 4 * x + 2 * y + c


def plan_gather(x, land, me):
    return [(x, land.at[me], k) for k in (SIBLING,) + CHIP_PEERS]


def plan_gather_direct(x, land, me):
    return [(x, land.at[me], k) for k in range(1, N_DEV)]


def plan_scatter_direct(x, land, me):
    return [(x.at[me ^ k], land.at[me], k) for k in range(1, N_DEV)]


def plan_forward(x, land, me):
    return [(x.at[me ^ k], land.at[me ^ k], SIBLING) for k in CHIP_PEERS]


def plan_to_sibling(x, land, me):
    return [(x.at[me ^ SIBLING ^ (2 * q)], land.at[q], SIBLING) for q in range(4)]


def plan_to_owners(x, land, me):
    return [(x.at[q], land.at[q - 1], 2 * q) for q in (1, 2, 3)]


def own_gather(x, land, me):
    return x, land.at[me]


def own_scatter(x, land, me):
    return x.at[me], land.at[me]


OWN_COPY = {plan_gather: own_gather, plan_gather_direct: own_gather, plan_scatter_direct: own_scatter}
N_COPIES = {plan_gather: 4, plan_gather_direct: 7, plan_scatter_direct: 7, plan_forward: 3, plan_to_sibling: 4,
            plan_to_owners: 3}


def _plans(plan, n):
    return list(plan) if isinstance(plan, (list, tuple)) else [plan] * n


def _own_copies(plan, x_refs, land_refs, own_sems):
    me = _my_index()
    plans = _plans(plan, len(land_refs))
    with_own = [i for i, p in enumerate(plans) if p in OWN_COPY]
    return [pltpu.make_async_copy(*OWN_COPY[plans[i]](x_refs[i], land_refs[i], me), own_sems[s])
            for s, i in enumerate(with_own)]


def _plan_copies(plan, x_refs, land_refs, send_sems, recv_sems):
    me = _my_index()
    plans = _plans(plan, len(land_refs))
    return [pltpu.make_async_remote_copy(src_ref=src, dst_ref=dst, send_sem=send_sems[i].at[s], recv_sem=recv_sems[i].at[s],
                                         device_id=_peer(k), device_id_type=pl.DeviceIdType.MESH)
            for i in range(len(land_refs)) for s, (src, dst, k) in enumerate(plans[i](x_refs[i], land_refs[i], me))]


def exchange_now(name, xs, lands, plan):
    n = len(lands)
    n_x = 0 if xs is None else n

    def body(*refs):
        land_in, land_out = refs[n_x:n_x + n], refs[n_x + n:n_x + 2 * n]
        x_refs = land_in if xs is None else refs[:n]
        sems = refs[n_x + 2 * n:]
        copies = _own_copies(plan, x_refs, land_out, sems[2 * n:]) + _plan_copies(plan, x_refs, land_out, sems[:n], sems[n:2 * n])
        for cp in copies:
            cp.start()
        for cp in copies:
            cp.wait()

    return pl.pallas_call(
        body, name=name, out_shape=[jax.ShapeDtypeStruct(a.shape, a.dtype) for a in lands],
        in_specs=[_ANY] * (n_x + n), out_specs=[_ANY] * n,
        input_output_aliases={n_x + i: i for i in range(n)},
        scratch_shapes=[pltpu.SemaphoreType.DMA((N_COPIES[p],)) for p in _plans(plan, n)] * 2
        + [pltpu.SemaphoreType.DMA(())] * sum(p in OWN_COPY for p in _plans(plan, n)),
    )(*([] if xs is None else xs), *lands)


def exchange_start(name, xs, lands, plan, after=()):
    n, m = len(lands), len(after)
    n_x = 0 if xs is None else n
    arrays = ([] if xs is None else list(xs)) + list(lands)
    n_a = len(arrays)
    n_own = sum(p in OWN_COPY for p in _plans(plan, n))
    n_s = 2 * n + n_own

    def body(*refs):
        land_refs = refs[n_x:n_a]
        x_refs = land_refs if xs is None else refs[:n]
        sems = refs[n_a + m:n_a + m + n_s]
        token = refs[-1]
        for cp in _own_copies(plan, x_refs, land_refs, sems[2 * n:]):
            cp.start()
        for cp in _plan_copies(plan, x_refs, land_refs, sems[:n], sems[n:2 * n]):
            cp.start()
        token[...] = jnp.zeros_like(token)

    sems = [pltpu.SemaphoreType.DMA((N_COPIES[p],)) for p in _plans(plan, n)] * 2 + [pltpu.SemaphoreType.DMA(())] * n_own
    res = pl.pallas_call(
        body, name=name,
        out_shape=sems + [pltpu.HBM(a.shape, a.dtype) for a in arrays] + [jax.ShapeDtypeStruct((8, LANES), F32)],
        in_specs=[_HBM] * n_a + [_ANY] * m,
        out_specs=[_SEM] * n_s + [_HBM] * n_a + [pl.BlockSpec(memory_space=pltpu.VMEM)],
        input_output_aliases={i: n_s + i for i in range(n_a)},
        compiler_params=pltpu.CompilerParams(has_side_effects=_EFFECT),
    )(*[pltpu.with_memory_space_constraint(a, pltpu.HBM) for a in arrays], *after)
    thru = res[n_s:n_s + n_a]
    own = iter(res[2 * n:n_s])
    handles = [(res[i], res[n + i], None if xs is None else thru[i], thru[n_x + i],
                next(own) if p in OWN_COPY else None) for i, p in enumerate(_plans(plan, n))]
    return handles, res[-1]


def exchange_wait(name, handles, plan, after):
    n = len(handles)
    in_place = handles[0][2] is None
    arrays = ([] if in_place else [h[2] for h in handles]) + [h[3] for h in handles]
    n_a = len(arrays)
    own_sems = [h[4] for h in handles if h[4] is not None]
    n_s = 2 * n + len(own_sems)

    def body(*refs):
        land_refs = refs[n_a - n:n_a]
        x_refs = land_refs if in_place else refs[:n]
        sems = refs[n_a:n_a + n_s]
        for cp in _own_copies(plan, x_refs, land_refs, sems[2 * n:]):
            cp.wait()
        for cp in _plan_copies(plan, x_refs, land_refs, sems[:n], sems[n:2 * n]):
            cp.wait_send()
            cp.wait_recv()

    res = pl.pallas_call(
        body, name=name,
        out_shape=[pltpu.HBM(a.shape, a.dtype) for a in arrays],
        in_specs=[_HBM] * n_a + [_SEM] * n_s + [_ANY],
        out_specs=[_HBM] * n_a,
        input_output_aliases={i: i for i in range(n_a)},
        compiler_params=pltpu.CompilerParams(has_side_effects=_EFFECT),
    )(*arrays, *[h[0] for h in handles], *[h[1] for h in handles], *own_sems, after)
    return (None if in_place else res[:n]), res[n_a - n:]


def chip_sum(name, gw, from_sibling):
    _, R, C = gw.shape
    tr = _row_tile(R, 512)
    me = _my_index().astype(jnp.int32).reshape(1)

    def body(me_ref, g_ref, s_ref, o_ref):
        o_ref[...] = (g_ref[...].astype(F32) + s_ref[...].astype(F32)).astype(o_ref.dtype)

    slot = pl.BlockSpec((None, tr, C), lambda q, i, me_ref: (q, i, 0))
    return pl.pallas_call(
        body, name=name, out_shape=jax.ShapeDtypeStruct((4, R, C), BF16),
        grid_spec=pltpu.PrefetchScalarGridSpec(
            num_scalar_prefetch=1, grid=(4, R // tr),
            in_specs=[pl.BlockSpec((None, tr, C), lambda q, i, me_ref: (me_ref[0] ^ (2 * q), i, 0)), slot],
            out_specs=slot),
        compiler_params=_params(("arbitrary", "arbitrary")),
    )(me, gw, from_sibling)


def _tile(n, cap):
    best = None
    for t in range(LANES, min(n, cap) + 1, LANES):
        if n % t == 0:
            best = t
    if best is None or (best < 256 and n <= 2304):
        return n
    return best


def _row_tile(m, cap):
    if m <= cap:
        return m
    t = cap
    while m % t:
        t //= 2
    return t if t >= 256 else m


def _after_spec(after):
    return [] if after is None else [pl.BlockSpec(memory_space=pl.ANY)]


def _after_arg(after):
    return [] if after is None else [after]


def _mm_call(name, dims, grid, a_spec, b_spec, o_spec, o_shape, tile, a, b, out_dtypes, epilogue, extra, after, vecs=()):
    nk = grid[2]
    n_x, n_o = len(extra) + len(vecs), len(out_dtypes)

    def body(a_ref, b_ref, *refs):
        x_refs, o_refs = refs[:n_x], refs[len(refs) - n_s - n_o:len(refs) - n_s]
        part = lax.dot_general(a_ref[...].astype(BF16), b_ref[...].astype(BF16), (dims, ((), ())),
                               preferred_element_type=F32)

        def finish(acc):
            vals = (acc,) if epilogue is None else epilogue(acc, *[x[...] for x in x_refs])
            for o_ref, val in zip(o_refs, vals):
                o_ref[...] = val.astype(o_ref.dtype)

        if nk == 1:
            finish(part)
        else:
            acc_ref = refs[-1]
            k = pl.program_id(2)

            @pl.when(k == 0)
            def _():
                acc_ref[...] = part

            @pl.when(jnp.logical_and(k > 0, k < nk - 1))
            def _():
                acc_ref[...] += part

            @pl.when(k == nk - 1)
            def _():
                finish(acc_ref[...] + part)

    n_s = 0 if nk == 1 else 1
    return pl.pallas_call(
        body, name=name, out_shape=[jax.ShapeDtypeStruct(o_shape, dt) for dt in out_dtypes], grid=grid,
        in_specs=[a_spec, b_spec] + [o_spec] * len(extra) + [pl.BlockSpec(v.shape, lambda i, j, k: (0, 0)) for v in vecs]
        + _after_spec(after), out_specs=[o_spec] * n_o,
        scratch_shapes=[pltpu.VMEM(tile, F32)] * n_s,
        compiler_params=_params(("parallel", "parallel", "arbitrary")),
    )(a, b, *extra, *vecs, *_after_arg(after))


def mm_nn(name, a, w, out_dtypes=(F32,), epilogue=None, extra=(), after=None, vecs=(), tm_cap=2048):
    M, K = a.shape
    G, _, n = w.shape
    tn = _tile(n, 1024)
    tk = _row_tile(K, 1024)
    tm = _row_tile(M, 512 if tn > 1024 else (1024 if tk > 1024 else tm_cap))
    r = n // tn
    return _mm_call(
        name, ((1,), (0,)), (M // tm, G * r, K // tk),
        pl.BlockSpec((tm, tk), lambda i, j, k: (i, k)),
        pl.BlockSpec((None, tk, tn), lambda i, j, k: (j // r, k, j % r)),
        pl.BlockSpec((tm, tn), lambda i, j, k: (i, j)), (M, G * n), (tm, tn),
        a, w, out_dtypes, epilogue, extra, after, vecs)


def mm_nt(name, dy, w, out_dtypes=(F32,), epilogue=None, extra=(), after=None):
    M, N = dy.shape
    G, K, n = w.shape
    tn = _tile(n, 1024)
    tk = _row_tile(K, 1024)
    tm = _row_tile(M, 512 if max(tn, tk) > 1024 else 2048)
    r = n // tn
    return _mm_call(
        name, ((1,), (1,)), (M // tm, K // tk, G * r),
        pl.BlockSpec((tm, tn), lambda i, j, k: (i, k)),
        pl.BlockSpec((None, tk, tn), lambda i, j, k: (k // r, j, k % r)),
        pl.BlockSpec((tm, tk), lambda i, j, k: (i, j)), (M, K), (tm, tk),
        dy, w, out_dtypes, epilogue, extra, after)


def mm_tn(name, a, dy, G, out_dtype=F32, after=None):
    M, K = a.shape
    n = dy.shape[1] // G
    tn = _tile(n, 1024)
    tk = _row_tile(K, 512 if tn > 1024 else 1024)
    tm = _row_tile(M, 1024 if tk > 1024 else 2048)
    r = n // tn
    return _mm_call(
        name, ((0,), (0,)), (K // tk, G * r, M // tm),
        pl.BlockSpec((tm, tk), lambda i, j, k: (k, i)),
        pl.BlockSpec((tm, tn), lambda i, j, k: (k, j)),
        pl.BlockSpec((None, tk, tn), lambda i, j, k: (j // r, i, j % r)), (G, K, n), (tk, tn),
        a, dy, (out_dtype,), None, (), after)[0]


ROW_BLOCK_BYTES = 12 * 1024 * 1024


def rowwise(name, fn, rows, vecs, outs, accs=(), after=None):
    rows = [r if isinstance(r, tuple) else (r, r.shape[1], 0) for r in rows]
    rows = [r if len(r) == 4 else r + (None,) for r in rows]
    n_fn = len(rows) + len(vecs)
    vecs = list(vecs) + _after_arg(after)
    S = rows[0][0].shape[-2]
    row_bytes = sum(w * a.dtype.itemsize for a, w, _, _ in rows) + sum(w * jnp.dtype(dt).itemsize for w, dt in outs)
    tm = _row_tile(S, 1024)
    while tm > 256 and tm * row_bytes > ROW_BLOCK_BYTES:
        tm //= 2
    n_r, n_v, n_o, n_a = len(rows), len(vecs), len(outs), len(accs)

    def body(*refs):
        ins = [ref[...] for ref in refs[:n_r + n_v]]
        o_refs = refs[n_r + n_v:n_r + n_v + n_o]
        a_refs = refs[n_r + n_v + n_o:]
        res = fn(*ins[:n_fn])
        res = res if isinstance(res, (tuple, list)) else (res,)
        for ref, val in zip(o_refs, res[:n_o]):
            ref[...] = val.astype(ref.dtype)
        if n_a:
            @pl.when(pl.program_id(0) == 0)
            def _():
                for ref in a_refs:
                    ref[...] = jnp.zeros_like(ref)
            for ref, val in zip(a_refs, res[n_o:]):
                ref[...] += val

    in_specs = [pl.BlockSpec((tm, w), functools.partial(lambda cb, i: (i, cb), cb)) if slab is None else
                pl.BlockSpec((None, tm, w), functools.partial(lambda cb, slab, i: (slab, i, cb), cb, slab))
                for _, w, cb, slab in rows]
    in_specs += [pl.BlockSpec(v.shape, lambda i: (0, 0)) for v in vecs]
    out_specs = [pl.BlockSpec((tm, w), lambda i: (i, 0)) for w, _ in outs]
    out_specs += [pl.BlockSpec(a, lambda i: (0, 0)) for a in accs]
    out_shape = [jax.ShapeDtypeStruct((S, w), dt) for w, dt in outs]
    out_shape += [jax.ShapeDtypeStruct(a, F32) for a in accs]
    res = pl.pallas_call(
        body, name=name, out_shape=out_shape, grid=(S // tm,),
        in_specs=in_specs, out_specs=out_specs,
        compiler_params=_params(("arbitrary",)),
    )(*[r[0] for r in rows], *vecs)
    return res


def _rms(x):
    return x * lax.rsqrt(jnp.mean(x * x, axis=-1, keepdims=True) + EPS)


def _modulate(x, gain, scale, shift):
    return _rms(x) * gain * (1.0 + scale) + shift


def _ada_slices(ada_raw, bias):
    ada = ada_raw + bias
    return [ada[:, i * D_MODEL:(i + 1) * D_MODEL] for i in range(ada.shape[1] // D_MODEL)]


def _norm_wide_heads(y, gain, heads):
    w = y.shape[1] // heads
    return jnp.concatenate([_rms(y[:, h * w:(h + 1) * w]) * gain[:, h * w:(h + 1) * w] for h in range(heads)], axis=1)


def _norm_fox_heads(x, gain):
    outs = []
    for p in range(x.shape[1] // LANES):
        blk = x[:, p * LANES:(p + 1) * LANES]
        low = lax.broadcasted_iota(jnp.int32, blk.shape, 1) < FOX_DH
        sq = blk * blk
        ss_low = jnp.sum(jnp.where(low, sq, 0.0), axis=1, keepdims=True)
        ss_high = jnp.sum(jnp.where(low, 0.0, sq), axis=1, keepdims=True)
        outs.append(blk * lax.rsqrt(jnp.where(low, ss_low, ss_high) * (1.0 / FOX_DH) + EPS))
    return jnp.concatenate(outs, axis=1) * gain


def _silu(x):
    return x * jax.nn.sigmoid(x)


def _log_sigmoid(z):
    return -(jnp.maximum(-z, 0.0) + jnp.log(1.0 + jnp.exp(-jnp.abs(z))))


def _rotate(x, cos, sin, heads, sign):
    w = x.shape[1] // heads
    half = w // 2
    outs = []
    for h in range(heads):
        x1 = x[:, h * w:h * w + half]
        x2 = x[:, h * w + half:(h + 1) * w]
        outs += [x1 * cos - sign * x2 * sin, sign * x1 * sin + x2 * cos]
    return jnp.concatenate(outs, axis=1)


def _vjp(fn, primals, ct):
    return jax.vjp(fn, *primals)[1](ct)


_LOG_GAMMAS = [float(np.log(np.float32(1.0) - np.float32(2.0) ** np.float32(-5.0 - h))) for h in range(RET_HEADS)]


RET_ROWS = 1024


def retention(name, q, k, v, reverse, out_dtype=F32):
    (qa, dk, qo), (ka, _, ko), (va, dv, vo) = q, k, v
    S = qa.shape[0]
    C = RET_CHUNK
    rows = _row_tile(S, RET_ROWS)
    nb = S // rows

    def body(q_ref, k_ref, v_ref, o_ref, state):
        h = pl.program_id(0)

        @pl.when(pl.program_id(1) == 0)
        def _():
            state[...] = jnp.zeros_like(state)

        log_g = jnp.float32(_LOG_GAMMAS[RET_HEADS - 1])
        for i in range(RET_HEADS - 2, -1, -1):
            log_g = jnp.where(h == i, jnp.float32(_LOG_GAMMAS[i]), log_g)
        row = lax.broadcasted_iota(jnp.int32, (C, C), 0)
        col = lax.broadcasted_iota(jnp.int32, (C, C), 1)
        rel = (col - row if reverse else row - col).astype(F32)
        decay = jnp.where(rel >= 0, jnp.exp(log_g * jnp.maximum(rel, 0.0)), 0.0)
        j = lax.broadcasted_iota(jnp.int32, (C, 1), 0).astype(F32)
        q_decay = jnp.exp(log_g * (C - j if reverse else j + 1.0))
        k_decay = jnp.exp(log_g * (j if reverse else C - 1.0 - j))
        chunk_decay = jnp.exp(jnp.full((1, 1), log_g * C, F32))

        chunks = range(rows // C)
        for ci in (reversed(chunks) if reverse else chunks):
            rs = slice(ci * C, (ci + 1) * C)
            qc = q_ref[rs, :].astype(BF16)
            kf = k_ref[rs, :].astype(F32)
            vc = v_ref[rs, :].astype(BF16)
            scores = lax.dot_general(qc, kf.astype(BF16), (((1,), (1,)), ((), ())), preferred_element_type=F32) * decay
            intra = jnp.dot(scores.astype(BF16), vc, preferred_element_type=F32)
            cross = jnp.dot(qc, state[...].astype(BF16), preferred_element_type=F32) * q_decay
            o_ref[rs, :] = (intra + cross).astype(o_ref.dtype)
            upd = lax.dot_general((kf * k_decay).astype(BF16), vc, (((0,), (0,)), ((), ())), preferred_element_type=F32)
            state[...] = state[...] * chunk_decay + upd

    def block(i):
        return nb - 1 - i if reverse else i

    return pl.pallas_call(
        body, name=name, out_shape=jax.ShapeDtypeStruct((S, RET_HEADS * dv), out_dtype),
        grid=(RET_HEADS, nb),
        in_specs=[pl.BlockSpec((rows, dk), lambda h, i: (block(i), qo + h)),
                  pl.BlockSpec((rows, dk), lambda h, i: (block(i), ko + h)),
                  pl.BlockSpec((rows, dv), lambda h, i: (block(i), vo + h))],
        out_specs=pl.BlockSpec((rows, dv), lambda h, i: (block(i), h)),
        scratch_shapes=[pltpu.VMEM((dk, dv), F32)],
        compiler_params=_params(("parallel", "arbitrary")),
    )(qa, ka, va)


FOX_T = 256
N_PAIR = FOX_HEADS // 2
FOX_SCALE = FOX_DH ** -0.5


def _fox_heads(q2):
    low = lax.broadcasted_iota(jnp.int32, (1, LANES), 1) < FOX_DH
    return [(mask, jnp.where(mask, q2 * FOX_SCALE, 0.0).astype(BF16)) for mask in (low, jnp.logical_not(low))]


def _fox_parts(j, t):
    return ([(0, j * t, False)] if j else []) + [(j * t, (j + 1) * t, True)]


def _fox_scores(qa, k_ref, ft_ref, head, lo, hi, diagonal):
    k_blk = k_ref[lo:hi, :].astype(BF16)
    s = lax.dot_general(qa, k_blk, (((1,), (1,)), ((), ())), preferred_element_type=F32) - ft_ref[pl.ds(head, 1), lo:hi]
    if diagonal:
        n = hi - lo
        s = jnp.where(lax.broadcasted_iota(jnp.int32, (n, n), 1) <= lax.broadcasted_iota(jnp.int32, (n, n), 0), s, -jnp.inf)
    return s


def fox_forward(name, qn, kn, kvf, f_cum_t):
    S = qn.shape[0]
    t = _row_tile(S, FOX_T)
    v_block0 = D_MODEL // LANES

    def variant(j, pair, q_ref, k_ref, v_ref, ft_ref, y_ref, lse_ref):
        ys, lses = [], []
        for a, (mask, qa) in enumerate(_fox_heads(q_ref[...])):
            parts = [(lo, hi, _fox_scores(qa, k_ref, ft_ref, 2 * pair + a, lo, hi, dg)) for lo, hi, dg in _fox_parts(j, t)]
            m = functools.reduce(jnp.maximum, [jnp.max(s, axis=1, keepdims=True) for _, _, s in parts])
            l, acc = 0.0, 0.0
            for lo, hi, s in parts:
                e = jnp.exp(s - m)
                l = l + jnp.sum(e, axis=1, keepdims=True)
                acc = acc + jnp.dot(e.astype(BF16), v_ref[lo:hi, :].astype(BF16), preferred_element_type=F32)
            ys.append(acc / l)
            lses.append(m + jnp.log(l))
        low = lax.broadcasted_iota(jnp.int32, (1, LANES), 1) < FOX_DH
        y_ref[...] = jnp.where(low, ys[0], ys[1])
        lse_ref[...] = jnp.where(low, lses[0], lses[1])

    def body(*refs):
        pair, i = pl.program_id(0), pl.program_id(1)
        for j in range(S // t):
            pl.when(i == j)(functools.partial(variant, j, pair, *refs))

    return pl.pallas_call(
        body, name=name,
        out_shape=[jax.ShapeDtypeStruct((S, D_MODEL), F32), jax.ShapeDtypeStruct((S, D_MODEL), F32)],
        grid=(N_PAIR, S // t),
        in_specs=[pl.BlockSpec((t, LANES), lambda p, i: (i, p)),
                  pl.BlockSpec((S, LANES), lambda p, i: (0, p)),
                  pl.BlockSpec((S, LANES), lambda p, i: (0, v_block0 + p)),
                  pl.BlockSpec((LANES, S), lambda p, i: (0, 0))],
        out_specs=[pl.BlockSpec((t, LANES), lambda p, i: (i, p)),
                   pl.BlockSpec((t, LANES), lambda p, i: (i, p))],
        compiler_params=_params(("parallel", "arbitrary")),
    )(qn, kn, kvf, f_cum_t)


def fox_backward(name, qn, kn, kvf, f_cum_t, y, dy, lse):
    S = qn.shape[0]
    t = _row_tile(S, FOX_T)
    v_block0 = D_MODEL // LANES

    def variant(j, pair, q_ref, k_ref, v_ref, ft_ref, y_ref, dy_ref, lse_ref, dq_ref, dk_ref, dv_ref, dfq_ref, dfk_ref,
                dv_acc):
        y2, dy2, lse2 = y_ref[...], dy_ref[...], lse_ref[...]
        lane = lax.broadcasted_iota(jnp.int32, (t, LANES), 1)
        dqs, dfq = [], jnp.zeros((t, LANES), F32)
        for a, (mask, qa) in enumerate(_fox_heads(q_ref[...].astype(F32))):
            lse_a = jnp.max(jnp.where(mask, lse2, -jnp.inf), axis=1, keepdims=True)
            dy_a = jnp.where(mask, dy2, 0.0)
            delta = jnp.sum(dy_a * y2, axis=1, keepdims=True)
            dy_b = dy_a.astype(BF16)
            dq, row_sum = 0.0, 0.0
            for lo, hi, dg in _fox_parts(j, t):
                p = jnp.exp(_fox_scores(qa, k_ref, ft_ref, 2 * pair + a, lo, hi, dg) - lse_a)
                dp = lax.dot_general(dy_b, v_ref[lo:hi, :].astype(BF16), (((1,), (1,)), ((), ())), preferred_element_type=F32)
                ds = p * (dp - delta)
                row_sum = row_sum + jnp.sum(ds, axis=1, keepdims=True)
                dfk_ref[pl.ds(a, 1), lo:hi] += -jnp.sum(ds, axis=0, keepdims=True)
                ds_b = ds.astype(BF16)
                dq = dq + jnp.dot(ds_b, k_ref[lo:hi, :].astype(BF16), preferred_element_type=F32)
                dk_ref[lo:hi, :] += lax.dot_general(ds_b, qa, (((0,), (0,)), ((), ())), preferred_element_type=F32)
                dv_acc[lo:hi, :] += lax.dot_general(p.astype(BF16), dy_b, (((0,), (0,)), ((), ())), preferred_element_type=F32)
            dqs.append(dq * FOX_SCALE)
            dfq = dfq + jnp.where(lane == 2 * pair + a, row_sum, 0.0)
        low = lax.broadcasted_iota(jnp.int32, (1, LANES), 1) < FOX_DH
        dq_ref[...] = jnp.where(low, dqs[0], dqs[1])
        dfq_ref[...] = dfq

    def body(*refs):
        pair, i = pl.program_id(0), pl.program_id(1)
        dk_ref, dv_ref, _, dfk_ref, dv_acc = refs[8:13]

        @pl.when(i == 0)
        def _():
            dk_ref[...] = jnp.zeros_like(dk_ref)
            dv_acc[...] = jnp.zeros_like(dv_acc)
            dfk_ref[...] = jnp.zeros_like(dfk_ref)

        for j in range(S // t):
            pl.when(i == j)(functools.partial(variant, j, pair, *refs))

        @pl.when(i == S // t - 1)
        def _():
            dv_ref[...] = dv_acc[...].astype(dv_ref.dtype)

    row_blk = pl.BlockSpec((t, LANES), lambda p, i: (i, p))
    col_blk = pl.BlockSpec((S, LANES), lambda p, i: (0, p))
    return pl.pallas_call(
        body, name=name,
        out_shape=[jax.ShapeDtypeStruct((S, D_MODEL), F32)] * 2
        + [jax.ShapeDtypeStruct((S, D_MODEL), BF16), jax.ShapeDtypeStruct((N_PAIR, S, LANES), F32),
           jax.ShapeDtypeStruct((N_PAIR, 8, S), F32)],
        grid=(N_PAIR, S // t),
        in_specs=[row_blk, col_blk,
                  pl.BlockSpec((S, LANES), lambda p, i: (0, v_block0 + p)),
                  pl.BlockSpec((LANES, S), lambda p, i: (0, 0)),
                  row_blk, row_blk, row_blk],
        out_specs=[row_blk, col_blk, col_blk,
                   pl.BlockSpec((None, t, LANES), lambda p, i: (p, i, 0)),
                   pl.BlockSpec((None, 8, S), lambda p, i: (p, 0, 0))],
        scratch_shapes=[pltpu.VMEM((S, LANES), F32)],
        compiler_params=_params(("parallel", "arbitrary")),
    )(qn, kn, kvf, f_cum_t, y, dy, lse)


def cumsum_rows(name, x, reverse):
    S = x.shape[0]
    C = LANES
    nc = S // C

    def body(x_ref, o_ref):
        row = lax.broadcasted_iota(jnp.int32, (C, C), 0)
        col = lax.broadcasted_iota(jnp.int32, (C, C), 1)
        tri = jnp.where(col >= row if reverse else col <= row, 1.0, 0.0).astype(F32)
        carry = jnp.zeros((1, LANES), F32)
        for i in (range(nc - 1, -1, -1) if reverse else range(nc)):
            blk = x_ref[i * C:(i + 1) * C, :]
            loc = jnp.dot(tri, blk, preferred_element_type=F32, precision=lax.Precision.HIGHEST)
            o_ref[i * C:(i + 1) * C, :] = loc + carry
            carry = carry + (loc[0:1, :] if reverse else loc[C - 1:C, :])

    return pl.pallas_call(body, name=name, out_shape=jax.ShapeDtypeStruct((S, LANES), F32),
                          compiler_params=_params())(x)


def adamw(name, parts, w, m, v):
    L, R, C = w.shape
    tr = _row_tile(R, 256)
    nr = R // tr
    counts = [len(p) for p in parts]

    def body(*refs):
        w_ref, m_ref, v_ref, g_out, d_out, m_out, v_out = refs[sum(counts):]
        for layer in range(L):
            p_refs = refs[sum(counts[:layer]):sum(counts[:layer + 1])]

            @pl.when(pl.program_id(0) == layer)
            def _(p_refs=p_refs, slots=[n for _, n in parts[layer]]):
                g = None
                for p_ref, n in zip(p_refs, slots):
                    for i in range(n):
                        g = p_ref[i].astype(F32) if g is None else g + p_ref[i].astype(F32)
                m2 = ADAM_B1 * m_ref[...] + (1.0 - ADAM_B1) * g
                v2 = ADAM_B2 * v_ref[...] + (1.0 - ADAM_B2) * jnp.square(g)
                m_hat = m2 / (1.0 - ADAM_B1 ** ADAM_STEP)
                v_hat = v2 / (1.0 - ADAM_B2 ** ADAM_STEP)
                g_out[...] = g
                d_out[...] = -ADAM_LR * (m_hat / (jnp.sqrt(v_hat) + ADAM_EPS) + ADAM_WD * w_ref[...])
                m_out[...] = m2
                v_out[...] = v2

    def part_spec(layer, n):
        return pl.BlockSpec((n, tr, C), lambda l, i: (0, jnp.where(l == layer, i, jnp.where(l < layer, 0, nr - 1)), 0))

    blk = pl.BlockSpec((None, tr, C), lambda l, i: (l, i, 0))
    return pl.pallas_call(
        body, name=name, out_shape=[jax.ShapeDtypeStruct((L, R, C), F32)] * 4, grid=(L, nr),
        in_specs=[part_spec(layer, n) for layer in range(L) for _, n in parts[layer]] + [blk, blk, blk],
        out_specs=[blk] * 4, compiler_params=_params(("arbitrary", "arbitrary")),
    )(*[a for layer in parts for a, _ in layer], w, m, v)


def kernel(x, c, positions, norm_mix_gain, norm_mlp_gain, w_ada, b_ada, w_mlp_in, w_mlp_out, ret_w_in, ret_norm_gain, ret_w_out, kv_norm_gain, kv_w_ada, kv_b_ada, kv_w, forget_bias, k_norm_gain, fox_w_in, q_norm_gain, fox_w_out, loss_target, m_norm_mix_gain, m_norm_mlp_gain, m_w_ada, m_b_ada, m_w_mlp_in, m_w_mlp_out, m_ret_w_in, m_ret_norm_gain, m_ret_w_out, m_kv_norm_gain, m_kv_w_ada, m_kv_b_ada, m_kv_w, m_forget_bias, m_k_norm_gain, m_fox_w_in, m_q_norm_gain, m_fox_w_out, v_norm_mix_gain, v_norm_mlp_gain, v_w_ada, v_b_ada, v_w_mlp_in, v_w_mlp_out, v_ret_w_in, v_ret_norm_gain, v_ret_w_out, v_kv_norm_gain, v_kv_w_ada, v_kv_b_ada, v_kv_w, v_forget_bias, v_k_norm_gain, v_fox_w_in, v_q_norm_gain, v_fox_w_out):
    D = D_MODEL
    S = x.shape[1]
    x0 = x.reshape(S, D)
    target = loss_target.reshape(S, D)
    me = 4 * lax.axis_index("x") + 2 * lax.axis_index("y") + lax.axis_index("c")
    n_ada = w_ada.shape[2]
    n_kvada = kv_w_ada.shape[1]
    n_kv = kv_w.shape[1]

    c_all, ret_gain = all_gather("gather_c", [c.reshape(D // LANES, LANES), ret_norm_gain.reshape(RET_HEADS, -1)])
    ret_gain = jnp.transpose(ret_gain, (1, 0, 2)).reshape(1, RET_HEADS * RET_V)
    c_act = rowwise("silu_c", _silu, [c_all.reshape(N_DEV, D)], [], [(D, F32)])[0]
    w_ada_cat = jnp.concatenate([w_ada[0], w_ada[1], kv_w_ada], axis=1).astype(BF16)[None]
    n_cat = 2 * n_ada + n_kvada
    ada_part = mm_nn("ada_proj", c_act, w_ada_cat)[0]
    ada_mine = all_to_all("ada_rows", [ada_part.reshape(N_DEV, n_cat // LANES, LANES)])[0]
    ada_mine = ada_mine.reshape(N_DEV, n_cat)
    ada_raw = [ada_mine[:, l * n_ada:(l + 1) * n_ada].reshape(1, 6 * D) for l in range(2)]
    kvada_raw = ada_mine[:, 2 * n_ada:].reshape(1, 2 * D)
    kv_bias = kv_b_ada.reshape(1, 2 * D)
    kv_gain = kv_norm_gain.reshape(1, D)
    fb = jnp.pad(forget_bias.reshape(1, FOX_HEADS), ((0, 0), (0, LANES - FOX_HEADS)))
    k_gain = jnp.tile(k_norm_gain.reshape(1, FOX_DH), (1, FOX_HEADS))
    q_gain = jnp.tile(q_norm_gain.reshape(1, FOX_DH), (1, FOX_HEADS))

    w_names = ["ret_in", "ret_out", "mlp_in0", "mlp_out0", "kv", "fox_in", "fox_out", "mlp_in1", "mlp_out1"]
    shards = [ret_w_in[0].astype(BF16), ret_w_out[0].astype(BF16), w_mlp_in[0].astype(BF16), w_mlp_out[0].astype(BF16),
              kv_w.T.astype(BF16), fox_w_in[0].astype(BF16), fox_w_out[0].astype(BF16), w_mlp_in[1].astype(BF16),
              w_mlp_out[1].astype(BF16)]
    two_level = {"ret_in", "ret_out", "mlp_in0", "mlp_out0"}
    w_plans = {name: plan_gather if name in two_level else plan_gather_direct for name in w_names}
    w_handles, token = exchange_start("gather_weights_start", shards,
                                      [lax.empty((N_DEV,) + a.shape, a.dtype) for a in shards],
                                      [w_plans[name] for name in w_names], after=(ada_mine, ret_gain))
    w_handles = dict(zip(w_names, w_handles))

    forwards = {}

    def forward_early(name, after):
        arrived = exchange_wait("gather_wait_" + name, [w_handles[name]], w_plans[name], after)[1]
        forwards[name], tok = exchange_start("gather_forward_start_" + name, None, arrived, plan_forward)
        return tok

    def weight(name, after):
        if name in two_level:
            return exchange_wait("gather_forward_wait_" + name, forwards[name], plan_forward, after)[1][0]
        return exchange_wait("gather_wait_" + name, [w_handles[name]], w_plans[name], after)[1][0]

    pos = positions.reshape(S, 1).astype(F32)
    half = RET_QK // 2
    inv_freq = jnp.asarray((ROPE_BASE ** (-np.arange(half, dtype=np.float32) / half)).reshape(1, half), F32)

    def angles(p, f):
        ang = p * f
        return jnp.cos(ang), jnp.sin(ang)

    cos, sin = rowwise("rope_table", angles, [pos], [inv_freq], [(half, F32), (half, F32)])

    def mod_mix(layer):
        def fn(xb, ada, bias, gain):
            sh, sc = _ada_slices(ada, bias)[:2]
            return _modulate(xb, gain[layer:layer + 1], sc, sh)
        return fn

    def mod_mlp(layer):
        def fn(xb, ada, bias, gain):
            sh, sc = _ada_slices(ada, bias)[3:5]
            return _modulate(xb, gain[layer:layer + 1], sc, sh)
        return fn

    h1_0 = rowwise("mod_mix0", mod_mix(0), [x0], [ada_raw[0], b_ada[0:1], norm_mix_gain], [(D, BF16)], after=token)[0]
    W_ret_in = weight("ret_in", forward_early("ret_in", h1_0))
    proj = mm_nn("ret_proj", h1_0, W_ret_in, (BF16,))[0]
    token = forward_early("ret_out", proj)

    def rope_fwd(qb, kb, cs, sn):
        return (_rotate(qb.astype(F32), cs, sn, RET_HEADS, 1.0),
                _rotate(kb.astype(F32), cs, sn, RET_HEADS, 1.0) * (RET_QK ** -0.5))

    q_rot, k_rot = rowwise("rope", rope_fwd, [(proj, D, 0), (proj, D, 1), cos, sin], [], [(D, BF16), (D, BF16)],
                           after=token)
    v_ret = (proj, RET_V, (2 * D) // RET_V)
    y_ret = retention("ret_fwd", (q_rot, RET_QK, 0), (k_rot, RET_QK, 0), v_ret, reverse=False)

    def ret_gate(yb, gb, gain):
        return _silu(gb.astype(F32)) * _norm_wide_heads(yb, gain, RET_HEADS)

    mixin0 = rowwise("ret_gate", ret_gate, [y_ret, (proj, 2 * D, 2)], [ret_gain], [(2 * D, BF16)])[0]
    W_ret_out = weight("ret_out", mixin0).reshape(1, 2 * D, D)
    FUSED_ROWS = 512

    def residual_mod(layer, slot):
        def fn(xb, bb, ada, bias, gain):
            s = _ada_slices(ada, bias)
            xn = xb + s[2] * bb
            return xn, _modulate(xn, gain[layer:layer + 1], s[4], s[3])
        return fn

    def with_residual(fn):
        return lambda acc, xb, *vs: (acc,) + tuple(fn(xb, acc, *vs))

    mix0, x1, h2_0 = mm_nn("ret_out", mixin0, W_ret_out, (F32, F32, BF16), with_residual(residual_mod(0, 0)), (x0,),
                           forward_early("mlp_in0", W_ret_out), (ada_raw[0], b_ada[0:1], norm_mlp_gain), FUSED_ROWS)

    W_mlp_in, W_mlp_out = {}, {}

    def mlp_forward(tag, h2, layer, out_dtypes=(F32,), epilogue=None, extra=(), vecs=()):
        W_mlp_in[layer] = weight("mlp_in" + tag, h2)
        early = forward_early("mlp_out" + tag, W_mlp_in[layer]) if "mlp_out" + tag in two_level else None
        u, act = mm_nn("mlp_in" + tag, h2, W_mlp_in[layer], (BF16, BF16), after=early,
                       epilogue=lambda acc: (acc, jnp.square(jnp.maximum(acc, 0.0))))
        W_mlp_out[layer] = weight("mlp_out" + tag, act).reshape(1, 4 * D, D)
        return u, act, mm_nn("mlp_out" + tag, act, W_mlp_out[layer], out_dtypes, epilogue, extra, None, vecs,
                             FUSED_ROWS if epilogue else 2048)

    def res_mlp0(xb, bb, ada0, bias0, ada1, bias1, kva, kvb, gain_mix, gain_kv):
        xn = xb + _ada_slices(ada0, bias0)[5] * bb
        s1 = _ada_slices(ada1, bias1)
        kv_shift, kv_scale = _ada_slices(kva, kvb)
        return xn, _modulate(xn, gain_kv, kv_scale, kv_shift), _modulate(xn, gain_mix[1:2], s1[1], s1[0])

    u0, act0, (mlp0,) = mlp_forward("0", h2_0, 0)
    x2, h_kv, h1_1 = rowwise("res_mlp0", res_mlp0, [x1, mlp0],
                             [ada_raw[0], b_ada[0:1], ada_raw[1], b_ada[1:2], kvada_raw, kv_bias, norm_mix_gain, kv_gain],
                             [(D, F32), (D, BF16), (D, BF16)])

    W_kv = jnp.pad(weight("kv", h_kv).reshape(N_DEV * n_kv, D), ((0, KV_PAD - N_DEV * n_kv), (0, 0)))[None]
    kvf = mm_nt("kv_proj", h_kv, W_kv)[0]

    def kv_post(kb, fblk, kg, bias):
        head = lax.broadcasted_iota(jnp.int32, fblk.shape, 1) < FOX_HEADS
        return _norm_fox_heads(kb, kg), jnp.where(head, _log_sigmoid(fblk + bias), 0.0)

    kn, log_f = rowwise("kv_post", kv_post, [(kvf, D, 0), (kvf, LANES, 2 * D // LANES)], [k_gain, fb],
                        [(D, BF16), (LANES, F32)])
    f_cum = cumsum_rows("f_cumsum", log_f, reverse=False)
    f_cum_t = f_cum.T

    W_fox_in = weight("fox_in", kvf)
    qo = mm_nn("fox_proj", h1_1, W_fox_in)[0]
    qn = rowwise("q_norm", _norm_fox_heads, [(qo, D, 0)], [q_gain], [(D, BF16)])[0]
    y_att, lse = fox_forward("fox_fwd", qn, kn, kvf, f_cum_t)
    mixin1 = rowwise("fox_gate", lambda ob, yb: jax.nn.sigmoid(ob) * yb, [(qo, D, 1), y_att], [], [(D, BF16)])[0]
    W_fox_out = weight("fox_out", mixin1).reshape(1, D, D)
    mix1, x3, h2_1 = mm_nn("fox_out", mixin1, W_fox_out, (F32, F32, BF16), with_residual(residual_mod(1, 0)), (x2,),
                           None, (ada_raw[1], b_ada[1:2], norm_mlp_gain), FUSED_ROWS)
    u1, act1, (mlp1,) = mlp_forward("1", h2_1, 1)

    def scatter_start(tag, gws, after=()):
        lands = [lax.empty((4,) + g.shape[1:], g.dtype) for g in gws]
        return exchange_start("scatter_sibling_start_" + tag, gws, lands, plan_to_sibling, after)

    def scatter_relay(tag, handles, after, start_after=()):
        gws, from_sibling = exchange_wait("scatter_sibling_wait_" + tag, handles, plan_to_sibling, after)
        sums = [chip_sum("chip_sum_%s%d" % (tag, i), g, s) for i, (g, s) in enumerate(zip(gws, from_sibling))]
        lands = [lax.empty((3,) + s.shape[1:], s.dtype) for s in sums]
        return exchange_start("scatter_owner_start_" + tag, sums, lands, plan_to_owners, start_after)

    def scatter_direct_start(tag, gws):
        lands = [lax.empty(g.shape, g.dtype) for g in gws]
        return exchange_start("scatter_direct_start_" + tag, gws, lands, plan_scatter_direct)

    def scatter_direct_finish(tag, handles, after):
        return [[(r, N_DEV)] for r in exchange_wait("scatter_direct_wait_" + tag, handles, plan_scatter_direct, after)[1]]

    def scatter_finish(tag, handles, after):
        sums, received = exchange_wait("scatter_owner_wait_" + tag, handles, plan_to_owners, after)
        return [[(s, 1), (r, 3)] for s, r in zip(sums, received)]

    def loss_head(xb, bb, tb, ada, bias):
        g2 = _ada_slices(ada, bias)[5]
        err = xb + g2 * bb - tb
        dx = err * (1.0 / D)
        loss = 0.5 * jnp.sum(jnp.sum(err * err, axis=1, keepdims=True) * (1.0 / D), axis=0, keepdims=True)
        return dx, (dx * g2), jnp.broadcast_to(loss, (1, LANES)), jnp.sum(dx * bb, axis=0, keepdims=True)

    dx4, dmlp1, loss_acc, dg2_1 = rowwise("loss_head", loss_head, [x3, mlp1, target], [ada_raw[1], b_ada[1:2]],
                                          [(D, F32), (D, BF16)], [(1, LANES), (1, D)])

    def mlp_backward(tag, dmlp, act, u, h2, layer, after=None):
        du = mm_nt("mlp_out_dx" + tag, dmlp, W_mlp_out[layer], (BF16,), extra=(u,), after=after,
                   epilogue=lambda acc, ub: (acc * (2.0 * jnp.maximum(ub.astype(F32), 0.0)),))[0]
        gw_out = mm_tn("mlp_out_dw" + tag, act, dmlp, 1, BF16).reshape(N_DEV, -1, D)
        dh2 = mm_nt("mlp_in_dx" + tag, du, W_mlp_in[layer], (BF16,))[0]
        gw_in = mm_tn("mlp_in_dw" + tag, h2, du, N_DEV, BF16)
        return dh2, gw_in, gw_out

    def mod_backward(layer, slots, gate_slot):
        def fn(xb, dhb, dresb, branchb, ada, bias, gain):
            s = _ada_slices(ada, bias)
            g = gain[layer:layer + 1]
            dx, dgain, dsc, dsh = _vjp(_modulate, (xb, g, s[slots[1]], s[slots[0]]), dhb.astype(F32))
            dx = dx + dresb
            d_branch = dx * s[gate_slot]
            return dx, d_branch, dgain, dsc, dsh, jnp.sum(dx * branchb, axis=0, keepdims=True)
        return fn

    vec = (1, D)
    dh2_1, gw_mlp_in1, gw_mlp_out1 = mlp_backward("1", dmlp1, act1, u1, h2_1, 1)
    scat_a1, token_a1 = scatter_direct_start("a1", [gw_mlp_in1, gw_mlp_out1])
    dx3, dmix1, dgain_mlp1, dsc2_1, dsh2_1, dg1_1 = rowwise(
        "mod_mlp1_bwd", mod_backward(1, (3, 4), 2), [x3, dh2_1, dx4, mix1], [ada_raw[1], b_ada[1:2], norm_mlp_gain],
        [(D, F32), (D, BF16)], [vec] * 4, after=token_a1)
    dmixin1 = mm_nt("fox_out_dx", dmix1, W_fox_out)[0]
    gw_fox_out = mm_tn("fox_out_dw", mixin1, dmix1, 1, BF16).reshape(N_DEV, -1, D)

    def fox_gate_bwd(db, ob, yb):
        sg = jax.nn.sigmoid(ob)
        return db * sg, db * yb * sg * (1.0 - sg)

    dy_att, d_og = rowwise("fox_gate_bwd", fox_gate_bwd, [dmixin1, (qo, D, 1), y_att], [], [(D, F32), (D, BF16)])
    dqn, dkn, dv_att, dfq, dfk = fox_backward("fox_bwd", qn, kn, kvf, f_cum_t, y_att, dy_att, lse)

    def q_norm_bwd(qb, db, ogb, gain):
        dq, dgain = _vjp(_norm_fox_heads, (qb, gain), db)
        return jnp.concatenate([dq, ogb.astype(F32)], axis=1), dgain

    dqo, dq_gain = rowwise("q_norm_bwd", q_norm_bwd, [(qo, D, 0), dqn, d_og], [q_gain], [(2 * D, BF16)], [vec])
    dh1_1 = mm_nt("fox_proj_dx", dqo, W_fox_in, (BF16,))[0]
    gw_fox_in = mm_tn("fox_proj_dw", h1_1, dqo, N_DEV, BF16)

    dfk_rows = jnp.pad(dfk[:, :2, :].reshape(FOX_HEADS, S).T, ((0, 0), (0, LANES - FOX_HEADS)))

    def df_total(*blks):
        tot = blks[0]
        for b in blks[1:]:
            tot = tot + b
        return tot

    d_fcum = rowwise("df_sum", df_total, [dfk_rows] + [(dfq, LANES, 0, p) for p in range(N_PAIR)], [], [(LANES, F32)])[0]
    d_logf = cumsum_rows("df_cumsum", d_fcum, reverse=True)

    def kv_post_bwd(kb, fblk, dkb, dvb, dlf, kg, bias):
        dk, dgain = _vjp(_norm_fox_heads, (kb, kg), dkb)
        df = dlf * (1.0 / (1.0 + jnp.exp(fblk + bias)))
        return jnp.concatenate([dk, dvb.astype(F32), df], axis=1), dgain, jnp.sum(df, axis=0, keepdims=True)

    dkvf, dk_gain, dfb = rowwise("kv_post_bwd", kv_post_bwd,
                                 [(kvf, D, 0), (kvf, LANES, 2 * D // LANES), dkn, dv_att, d_logf], [k_gain, fb],
                                 [(KV_PAD, BF16)], [vec, (1, LANES)])
    dh_kv = mm_nn("kv_proj_dx", dkvf, W_kv, (BF16,))[0]
    gw_kv = mm_tn("kv_proj_dw", dkvf, h_kv, 1, BF16)[0, :N_DEV * n_kv].reshape(N_DEV, n_kv, D)
    scat_a2, token_a = scatter_direct_start("a2", [gw_fox_out, gw_fox_in, gw_kv])

    def x2_bwd(xb, dh1b, dhkb, dresb, branchb, ada0, bias0, ada1, bias1, kva, kvb, gain_mix, gain_kv):
        s1 = _ada_slices(ada1, bias1)
        kv_shift, kv_scale = _ada_slices(kva, kvb)
        dxa, dgain_mix, dsc1, dsh1 = _vjp(_modulate, (xb, gain_mix[1:2], s1[1], s1[0]), dh1b.astype(F32))
        dxb, dgain_kv, dkv_scale, dkv_shift = _vjp(_modulate, (xb, gain_kv, kv_scale, kv_shift), dhkb.astype(F32))
        dx = dresb + dxa + dxb
        g2 = _ada_slices(ada0, bias0)[5]
        return (dx, dx * g2, dgain_mix, dsc1, dsh1, dgain_kv, dkv_scale, dkv_shift,
                jnp.sum(dx * branchb, axis=0, keepdims=True))

    (dx2, dmlp0, dgain_mix1, dsc1_1, dsh1_1, dgain_kv, dkv_scale, dkv_shift, dg2_0) = rowwise(
        "x2_bwd", x2_bwd, [x2, dh1_1, dh_kv, dx3, mlp0],
        [ada_raw[0], b_ada[0:1], ada_raw[1], b_ada[1:2], kvada_raw, kv_bias, norm_mix_gain, kv_gain],
        [(D, F32), (D, BF16)], [vec] * 7, after=token_a)

    dh2_0, gw_mlp_in0, gw_mlp_out0 = mlp_backward("0", dmlp0, act0, u0, h2_0, 0)
    scat_b1, token_b1 = scatter_direct_start("b1", [gw_mlp_in0, gw_mlp_out0])
    dx1, dmix0, dgain_mlp0, dsc2_0, dsh2_0, dg1_0 = rowwise(
        "mod_mlp0_bwd", mod_backward(0, (3, 4), 2), [x1, dh2_0, dx2, mix0], [ada_raw[0], b_ada[0:1], norm_mlp_gain],
        [(D, F32), (D, BF16)], [vec] * 4, after=token_b1)
    dmixin0 = mm_nt("ret_out_dx", dmix0, W_ret_out, (BF16,))[0]
    gw_ret_out = mm_tn("ret_out_dw", mixin0, dmix0, 1, BF16).reshape(N_DEV, -1, D)
    scat_b2, token_b = scatter_direct_start("b2", [gw_ret_out])

    def ret_gate_bwd(db, yb, gb, gain):
        return _vjp(ret_gate, (yb, gb.astype(F32), gain), db.astype(F32))

    dy_ret, dgate, dret_gain = rowwise("ret_gate_bwd", ret_gate_bwd, [dmixin0, y_ret, (proj, 2 * D, 2)], [ret_gain],
                                       [(2 * D, BF16), (2 * D, BF16)], [(1, 2 * D)], after=token_b)
    dy_h = (dy_ret, RET_V, 0)
    dq_rot = retention("ret_dq", dy_h, v_ret, (k_rot, RET_QK, 0), reverse=False)
    dk_rot = retention("ret_dk", v_ret, dy_h, (q_rot, RET_QK, 0), reverse=True)
    dv_ret = retention("ret_dv", (k_rot, RET_QK, 0), (q_rot, RET_QK, 0), dy_h, reverse=True, out_dtype=BF16)

    def rope_bwd(dqb, dkb, dvb, dgb, cs, sn):
        dq = _rotate(dqb, cs, sn, RET_HEADS, -1.0)
        dk = _rotate(dkb, cs, sn, RET_HEADS, -1.0) * (RET_QK ** -0.5)
        return jnp.concatenate([dq, dk, dvb.astype(F32), dgb.astype(F32)], axis=1)

    dproj = rowwise("rope_bwd", rope_bwd, [dq_rot, dk_rot, dv_ret, dgate, cos, sin], [], [(6 * D, BF16)])[0]
    gw_ret_in = mm_tn("ret_proj_dw", h1_0, dproj, N_DEV, BF16)
    scat_c, token_c = scatter_start("c", [gw_ret_in])
    dh1_0 = mm_nt("ret_proj_dx", dproj, W_ret_in, (BF16,), after=token_c)[0]

    def x0_bwd(xb, dhb, dresb, ada, bias, gain):
        s = _ada_slices(ada, bias)
        dx, dgain, dsc, dsh = _vjp(_modulate, (xb, gain[0:1], s[1], s[0]), dhb.astype(F32))
        return dx + dresb, dgain, dsc, dsh

    grad_x, dgain_mix0, dsc1_0, dsh1_0 = rowwise("x0_bwd", x0_bwd, [x0, dh1_0, dx1],
                                                 [ada_raw[0], b_ada[0:1], norm_mix_gain], [(D, F32)], [vec] * 3)

    small = jnp.concatenate([
        dsh1_0, dsc1_0, dg1_0, dsh2_0, dsc2_0, dg2_0,
        dsh1_1, dsc1_1, dg1_1, dsh2_1, dsc2_1, dg2_1,
        dkv_shift, dkv_scale,
        dgain_mix0, dgain_mix1, dgain_mlp0, dgain_mlp1, dgain_kv,
        dret_gain,
        dq_gain.reshape(FOX_HEADS, FOX_DH).sum(axis=0).reshape(1, FOX_DH),
        dk_gain.reshape(FOX_HEADS, FOX_DH).sum(axis=0).reshape(1, FOX_DH),
        dfb,
        loss_acc,
    ], axis=1)
    small_all = all_gather("gather_small", [small.reshape(-1, LANES)])[0].reshape(N_DEV, 1, -1)
    loss = jnp.sum(small_all[:, 0, -1])
    scat_c, token_c = scatter_relay("c", scat_c, grad_x, start_after=(small_all,))
    o_ada = 14 * D
    d_ada = small_all[:, 0, :o_ada]
    d_cat = jnp.concatenate([
        lax.dynamic_slice_in_dim(d_ada[:, 0:6 * D], me * n_ada, n_ada, axis=1),
        lax.dynamic_slice_in_dim(d_ada[:, 6 * D:12 * D], me * n_ada, n_ada, axis=1),
        lax.dynamic_slice_in_dim(d_ada[:, 12 * D:14 * D], me * n_kvada, n_kvada, axis=1)], axis=1)
    gw_ada_cat = mm_tn("ada_dw", c_act, d_cat, 1, F32, after=token_c)[0]

    results = {}

    def update(name, parts, w, m, v, layers=1):
        per_layer = parts if layers > 1 else [parts]
        shape = w.shape
        C = shape[-1]
        R = int(np.prod(shape)) // (layers * C)
        per_layer = [p if isinstance(p, list) else [(p, p.shape[0])] for p in per_layer]
        per_layer = [[(a.reshape(a.shape[0], R, C), n) for a, n in p] for p in per_layer]
        outs = adamw("adamw_" + name, per_layer, w.reshape(layers, R, C), m.reshape(layers, R, C), v.reshape(layers, R, C))
        results[name] = tuple(t.reshape(shape) for t in outs)

    def small_parts(lo, width):
        return small_all[:, :, lo:lo + width]

    update("norm_mix_gain", jnp.concatenate([small_parts(o_ada, D), small_parts(o_ada + D, D)], axis=1),
           norm_mix_gain, m_norm_mix_gain, v_norm_mix_gain)
    update("norm_mlp_gain", jnp.concatenate([small_parts(o_ada + 2 * D, D), small_parts(o_ada + 3 * D, D)], axis=1),
           norm_mlp_gain, m_norm_mlp_gain, v_norm_mlp_gain)
    update("w_ada", [gw_ada_cat[None, :, :n_ada], gw_ada_cat[None, :, n_ada:2 * n_ada]], w_ada, m_w_ada, v_w_ada, layers=2)
    update("b_ada", jnp.concatenate([small_parts(0, 6 * D), small_parts(6 * D, 6 * D)], axis=1), b_ada, m_b_ada, v_b_ada)
    o_ret = o_ada + 5 * D
    n_rg = ret_norm_gain.shape[2]
    ret_gain_parts = small_parts(o_ret, 2 * D).reshape(N_DEV, RET_HEADS, RET_V)
    ret_gain_parts = lax.dynamic_slice_in_dim(ret_gain_parts, me * n_rg, n_rg, axis=2)
    update("ret_norm_gain", ret_gain_parts, ret_norm_gain, m_ret_norm_gain, v_ret_norm_gain)
    update("kv_norm_gain", small_parts(o_ada + 4 * D, D), kv_norm_gain, m_kv_norm_gain, v_kv_norm_gain)
    update("kv_w_ada", gw_ada_cat[None, :, 2 * n_ada:], kv_w_ada, m_kv_w_ada, v_kv_w_ada)
    update("kv_b_ada", small_parts(12 * D, 2 * D), kv_b_ada, m_kv_b_ada, v_kv_b_ada)
    o_q = o_ret + 2 * D
    update("forget_bias", small_parts(o_q + 2 * FOX_DH, FOX_HEADS), forget_bias, m_forget_bias, v_forget_bias)
    update("k_norm_gain", small_parts(o_q + FOX_DH, FOX_DH), k_norm_gain, m_k_norm_gain, v_k_norm_gain)
    update("q_norm_gain", small_parts(o_q, FOX_DH), q_norm_gain, m_q_norm_gain, v_q_norm_gain)

    r_mlp_in1, r_mlp_out1 = scatter_direct_finish("a1", scat_a1, results["q_norm_gain"][1])
    r_fox_out, r_fox_in, r_kv = scatter_direct_finish("a2", scat_a2, r_mlp_in1[0][0])
    r_mlp_in0, r_mlp_out0 = scatter_direct_finish("b1", scat_b1, r_kv[0][0])
    r_ret_out = scatter_direct_finish("b2", scat_b2, r_mlp_in0[0][0])[0]
    update("kv_w", r_kv, kv_w.T, m_kv_w.T, v_kv_w.T)
    results["kv_w"] = tuple(t.T for t in results["kv_w"])
    update("fox_w_in", r_fox_in, fox_w_in, m_fox_w_in, v_fox_w_in)
    update("fox_w_out", r_fox_out, fox_w_out, m_fox_w_out, v_fox_w_out)
    update("ret_w_out", r_ret_out, ret_w_out, m_ret_w_out, v_ret_w_out)
    update("w_mlp_in", [r_mlp_in0, r_mlp_in1], w_mlp_in, m_w_mlp_in, v_w_mlp_in, layers=2)
    update("w_mlp_out", [r_mlp_out0, r_mlp_out1], w_mlp_out, m_w_mlp_out, v_w_mlp_out, layers=2)
    r_ret_in = scatter_finish("c", scat_c, results["w_mlp_out"][1])[0]
    update("ret_w_in", r_ret_in, ret_w_in, m_ret_w_in, v_ret_w_in)

    order = ["norm_mix_gain", "norm_mlp_gain", "w_ada", "b_ada", "w_mlp_in", "w_mlp_out", "ret_w_in", "ret_norm_gain",
             "ret_w_out", "kv_norm_gain", "kv_w_ada", "kv_b_ada", "kv_w", "forget_bias", "k_norm_gain", "fox_w_in",
             "q_norm_gain", "fox_w_out"]
    out = [loss, grad_x.reshape(x.shape)]
    for slot in range(4):
        out += [results[n][slot] for n in order]
    return tuple(out)
```

```python
import functools
import math

import numpy as np
import jax
import jax.numpy as jnp
from jax import lax
from jax.experimental import pallas as pl
from jax.experimental.pallas import tpu as pltpu

F32 = jnp.float32
BF16 = jnp.bfloat16

N_DEV = 8
D_MODEL = 1024
RET_HEADS = 4
RET_QK = D_MODEL // RET_HEADS
RET_V = 2 * D_MODEL // RET_HEADS
RET_CHUNK = 128
ROPE_BASE = 10000.0
FOX_HEADS = 16
FOX_DH = D_MODEL // FOX_HEADS
EPS = 1e-6
LANES = 128
KV_PAD = 2 * D_MODEL + LANES

ADAM_LR = 0.001
ADAM_B1 = 0.9
ADAM_B2 = 0.999
ADAM_EPS = 1e-08
ADAM_WD = 0.01
ADAM_STEP = 10

VMEM_LIMIT_BYTES = 56 * 1024 * 1024


def _params(sem=None):
    return pltpu.CompilerParams(dimension_semantics=sem, vmem_limit_bytes=VMEM_LIMIT_BYTES)


def _me():
    return lax.axis_index("x"), lax.axis_index("y"), lax.axis_index("c")


def _peer(k):
    x, y, c = _me()
    return (1 - x if k & 4 else x, 1 - y if k & 2 else y, 1 - c if k & 1 else c)


def _peer_index(k):
    px, py, pc = _peer(k)
    return 4 * px + 2 * py + pc


def _exchange(name, xs, scatter):
    n = len(xs)

    def body(*refs):
        x_refs, o_refs = refs[:n], refs[n:2 * n]
        send_sems, recv_sems, local_sems = refs[2 * n:]
        x, y, c = _me()
        me = 4 * x + 2 * y + c
        local = []
        for i in range(n):
            src = x_refs[i].at[me] if scatter else x_refs[i]
            cp = pltpu.make_async_copy(src, o_refs[i].at[me], local_sems.at[i])
            cp.start()
            local.append(cp)
        remote = []
        for k in range(1, N_DEV):
            for i in range(n):
                src = x_refs[i].at[_peer_index(k)] if scatter else x_refs[i]
                cp = pltpu.make_async_remote_copy(
                    src_ref=src, dst_ref=o_refs[i].at[me],
                    send_sem=send_sems.at[(k - 1) * n + i], recv_sem=recv_sems.at[(k - 1) * n + i],
                    device_id=_peer(k), device_id_type=pl.DeviceIdType.MESH)
                cp.start()
                remote.append(cp)
        for cp in remote:
            cp.wait()
        for cp in local:
            cp.wait()

    out_shape = [jax.ShapeDtypeStruct(x.shape if scatter else (N_DEV,) + x.shape, x.dtype) for x in xs]
    any_spec = pl.BlockSpec(memory_space=pl.ANY)
    return pl.pallas_call(
        body, name=name, out_shape=out_shape,
        in_specs=[any_spec] * n, out_specs=[any_spec] * n,
        scratch_shapes=[pltpu.SemaphoreType.DMA(((N_DEV - 1) * n,)),
                        pltpu.SemaphoreType.DMA(((N_DEV - 1) * n,)),
                        pltpu.SemaphoreType.DMA((n,))],
    )(*xs)


def all_gather(name, xs):
    return _exchange(name, xs, scatter=False)


def all_to_all(name, xs):
    return _exchange(name, xs, scatter=True)


_HBM = pl.BlockSpec(memory_space=pltpu.HBM)
_SEM = pl.BlockSpec(memory_space=pltpu.SEMAPHORE)
_ANY = pl.BlockSpec(memory_space=pl.ANY)
_EFFECT = pltpu.SideEffectType.DATAFLOW_SIDE_EFFECTING

SIBLING = 1
CHIP_PEERS = (2, 4, 6)


def _my_index():
    x, y, c = _me()
    return 4 * x + 2 * y + c


def _slot(land, j):
    if len(land.shape) == 2:
        n = land.shape[1] // N_DEV
        return land.at[:, pl.ds(pl.multiple_of(j * n, LANES), n)]
    return land.at[j]


def plan_gather(x, land, me):
    return [(x, _slot(land, me), k) for k in (SIBLING,) + CHIP_PEERS]


def plan_gather_direct(x, land, me):
    return [(x, _slot(land, me), k) for k in range(1, N_DEV)]


def plan_scatter_direct(x, land, me):
    return [(x.at[me ^ k], land.at[me], k) for k in range(1, N_DEV)]


def plan_forward(x, land, me):
    return [(_slot(x, me ^ k), _slot(land, me ^ k), SIBLING) for k in CHIP_PEERS]


def plan_to_sibling(x, land, me):
    return [(x.at[me ^ SIBLING ^ (2 * q)], land.at[q], SIBLING) for q in range(4)]


def plan_to_owners(x, land, me):
    return [(x.at[q], land.at[q - 1], 2 * q) for q in (1, 2, 3)]


def own_gather(x, land, me):
    return x, _slot(land, me)


def own_scatter(x, land, me):
    return x.at[me], land.at[me]


OWN_COPY = {plan_gather: own_gather, plan_gather_direct: own_gather, plan_scatter_direct: own_scatter}
N_COPIES = {plan_gather: 4, plan_gather_direct: 7, plan_scatter_direct: 7, plan_forward: 3, plan_to_sibling: 4,
            plan_to_owners: 3}


def _plans(plan, n):
    return list(plan) if isinstance(plan, (list, tuple)) else [plan] * n


def _own_copies(plan, x_refs, land_refs, own_sems):
    me = _my_index()
    plans = _plans(plan, len(land_refs))
    with_own = [i for i, p in enumerate(plans) if p in OWN_COPY]
    return [pltpu.make_async_copy(*OWN_COPY[plans[i]](x_refs[i], land_refs[i], me), own_sems[s])
            for s, i in enumerate(with_own)]


def _plan_copies(plan, x_refs, land_refs, send_sems, recv_sems):
    me = _my_index()
    plans = _plans(plan, len(land_refs))
    return [pltpu.make_async_remote_copy(src_ref=src, dst_ref=dst, send_sem=send_sems[i].at[s], recv_sem=recv_sems[i].at[s],
                                         device_id=_peer(k), device_id_type=pl.DeviceIdType.MESH)
            for i in range(len(land_refs)) for s, (src, dst, k) in enumerate(plans[i](x_refs[i], land_refs[i], me))]


def exchange_now(name, xs, lands, plan):
    n = len(lands)
    n_x = 0 if xs is None else n

    def body(*refs):
        land_in, land_out = refs[n_x:n_x + n], refs[n_x + n:n_x + 2 * n]
        x_refs = land_in if xs is None else refs[:n]
        sems = refs[n_x + 2 * n:]
        copies = _own_copies(plan, x_refs, land_out, sems[2 * n:]) + _plan_copies(plan, x_refs, land_out, sems[:n], sems[n:2 * n])
        for cp in copies:
            cp.start()
        for cp in copies:
            cp.wait()

    return pl.pallas_call(
        body, name=name, out_shape=[jax.ShapeDtypeStruct(a.shape, a.dtype) for a in lands],
        in_specs=[_ANY] * (n_x + n), out_specs=[_ANY] * n,
        input_output_aliases={n_x + i: i for i in range(n)},
        scratch_shapes=[pltpu.SemaphoreType.DMA((N_COPIES[p],)) for p in _plans(plan, n)] * 2
        + [pltpu.SemaphoreType.DMA(())] * sum(p in OWN_COPY for p in _plans(plan, n)),
    )(*([] if xs is None else xs), *lands)


def exchange_start(name, xs, lands, plan, after=()):
    n, m = len(lands), len(after)
    n_x = 0 if xs is None else n
    arrays = ([] if xs is None else list(xs)) + list(lands)
    n_a = len(arrays)
    n_own = sum(p in OWN_COPY for p in _plans(plan, n))
    n_s = 2 * n + n_own

    def body(*refs):
        land_refs = refs[n_x:n_a]
        x_refs = land_refs if xs is None else refs[:n]
        sems = refs[n_a + m:n_a + m + n_s]
        token = refs[-1]
        for cp in _own_copies(plan, x_refs, land_refs, sems[2 * n:]):
            cp.start()
        for cp in _plan_copies(plan, x_refs, land_refs, sems[:n], sems[n:2 * n]):
            cp.start()
        token[...] = jnp.zeros_like(token)

    sems = [pltpu.SemaphoreType.DMA((N_COPIES[p],)) for p in _plans(plan, n)] * 2 + [pltpu.SemaphoreType.DMA(())] * n_own
    res = pl.pallas_call(
        body, name=name,
        out_shape=sems + [pltpu.HBM(a.shape, a.dtype) for a in arrays] + [jax.ShapeDtypeStruct((8, LANES), F32)],
        in_specs=[_HBM] * n_a + [_ANY] * m,
        out_specs=[_SEM] * n_s + [_HBM] * n_a + [pl.BlockSpec(memory_space=pltpu.VMEM)],
        input_output_aliases={i: n_s + i for i in range(n_a)},
        compiler_params=pltpu.CompilerParams(has_side_effects=_EFFECT),
    )(*[pltpu.with_memory_space_constraint(a, pltpu.HBM) for a in arrays], *after)
    thru = res[n_s:n_s + n_a]
    own = iter(res[2 * n:n_s])
    handles = [(res[i], res[n + i], None if xs is None else thru[i], thru[n_x + i],
                next(own) if p in OWN_COPY else None) for i, p in enumerate(_plans(plan, n))]
    return handles, res[-1]


def exchange_wait(name, handles, plan, after):
    n = len(handles)
    in_place = handles[0][2] is None
    arrays = ([] if in_place else [h[2] for h in handles]) + [h[3] for h in handles]
    n_a = len(arrays)
    own_sems = [h[4] for h in handles if h[4] is not None]
    n_s = 2 * n + len(own_sems)

    def body(*refs):
        land_refs = refs[n_a - n:n_a]
        x_refs = land_refs if in_place else refs[:n]
        sems = refs[n_a:n_a + n_s]
        for cp in _own_copies(plan, x_refs, land_refs, sems[2 * n:]):
            cp.wait()
        for cp in _plan_copies(plan, x_refs, land_refs, sems[:n], sems[n:2 * n]):
            cp.wait_send()
            cp.wait_recv()

    res = pl.pallas_call(
        body, name=name,
        out_shape=[pltpu.HBM(a.shape, a.dtype) for a in arrays],
        in_specs=[_HBM] * n_a + [_SEM] * n_s + [_ANY],
        out_specs=[_HBM] * n_a,
        input_output_aliases={i: i for i in range(n_a)},
        compiler_params=pltpu.CompilerParams(has_side_effects=_EFFECT),
    )(*arrays, *[h[0] for h in handles], *[h[1] for h in handles], *own_sems, after)
    return (None if in_place else res[:n]), res[n_a - n:]


def chip_sum(name, gw, from_sibling):
    _, R, C = gw.shape
    tr = _row_tile(R, 512)
    me = _my_index().astype(jnp.int32).reshape(1)

    def body(me_ref, g_ref, s_ref, o_ref):
        o_ref[...] = (g_ref[...].astype(F32) + s_ref[...].astype(F32)).astype(o_ref.dtype)

    slot = pl.BlockSpec((None, tr, C), lambda q, i, me_ref: (q, i, 0))
    return pl.pallas_call(
        body, name=name, out_shape=jax.ShapeDtypeStruct((4, R, C), BF16),
        grid_spec=pltpu.PrefetchScalarGridSpec(
            num_scalar_prefetch=1, grid=(4, R // tr),
            in_specs=[pl.BlockSpec((None, tr, C), lambda q, i, me_ref: (me_ref[0] ^ (2 * q), i, 0)), slot],
            out_specs=slot),
        compiler_params=_params(("arbitrary", "arbitrary")),
    )(me, gw, from_sibling)


def _tile(n, cap):
    best = None
    for t in range(LANES, min(n, cap) + 1, LANES):
        if n % t == 0:
            best = t
    if best is None or (best < 256 and n <= 2304):
        return n
    return best


def _row_tile(m, cap):
    if m <= cap:
        return m
    t = cap
    while m % t:
        t //= 2
    return t if t >= 256 else m


def _after_spec(after):
    return [] if after is None else [pl.BlockSpec(memory_space=pl.ANY)]


def _after_arg(after):
    return [] if after is None else [after]


def _mm_call(name, dims, grid, a_spec, b_spec, o_spec, o_shape, tile, a, b, out_dtypes, epilogue, extra, after, vecs=()):
    nk = grid[2]
    n_x, n_o = len(extra) + len(vecs), len(out_dtypes)

    def body(a_ref, b_ref, *refs):
        x_refs, o_refs = refs[:n_x], refs[len(refs) - n_s - n_o:len(refs) - n_s]
        part = lax.dot_general(a_ref[...].astype(BF16), b_ref[...].astype(BF16), (dims, ((), ())),
                               preferred_element_type=F32)

        def finish(acc):
            vals = (acc,) if epilogue is None else epilogue(acc, *[x[...] for x in x_refs])
            for o_ref, val in zip(o_refs, vals):
                o_ref[...] = val.astype(o_ref.dtype)

        if nk == 1:
            finish(part)
        else:
            acc_ref = refs[-1]
            k = pl.program_id(2)

            @pl.when(k == 0)
            def _():
                acc_ref[...] = part

            @pl.when(jnp.logical_and(k > 0, k < nk - 1))
            def _():
                acc_ref[...] += part

            @pl.when(k == nk - 1)
            def _():
                finish(acc_ref[...] + part)

    n_s = 0 if nk == 1 else 1
    return pl.pallas_call(
        body, name=name, out_shape=[jax.ShapeDtypeStruct(o_shape, dt) for dt in out_dtypes], grid=grid,
        in_specs=[a_spec, b_spec] + [o_spec] * len(extra) + [pl.BlockSpec(v.shape, lambda i, j, k: (0, 0)) for v in vecs]
        + _after_spec(after), out_specs=[o_spec] * n_o,
        scratch_shapes=[pltpu.VMEM(tile, F32)] * n_s,
        compiler_params=_params(("parallel", "parallel", "arbitrary")),
    )(a, b, *extra, *vecs, *_after_arg(after))


def mm_nn(name, a, w, out_dtypes=(F32,), epilogue=None, extra=(), after=None, vecs=(), tm_cap=2048):
    M, K = a.shape
    G, _, n = w.shape
    tn = _tile(n, 1024)
    tk = _row_tile(K, 1024)
    tm = _row_tile(M, 512 if tn > 1024 else (1024 if tk > 1024 else tm_cap))
    r = n // tn
    return _mm_call(
        name, ((1,), (0,)), (M // tm, G * r, K // tk),
        pl.BlockSpec((tm, tk), lambda i, j, k: (i, k)),
        pl.BlockSpec((None, tk, tn), lambda i, j, k: (j // r, k, j % r)),
        pl.BlockSpec((tm, tn), lambda i, j, k: (i, j)), (M, G * n), (tm, tn),
        a, w, out_dtypes, epilogue, extra, after, vecs)


def mm_nt(name, dy, w, out_dtypes=(F32,), epilogue=None, extra=(), after=None):
    M, N = dy.shape
    G, K, n = w.shape
    tn = _tile(n, 2048)
    tk = _row_tile(K, 1024)
    tm = _row_tile(M, 512 if tk > 1024 else 2048)
    r = n // tn
    return _mm_call(
        name, ((1,), (1,)), (M // tm, K // tk, G * r),
        pl.BlockSpec((tm, tn), lambda i, j, k: (i, k)),
        pl.BlockSpec((None, tk, tn), lambda i, j, k: (k // r, j, k % r)),
        pl.BlockSpec((tm, tk), lambda i, j, k: (i, j)), (M, K), (tm, tk),
        dy, w, out_dtypes, epilogue, extra, after)


def mm_tn(name, a, dy, G, out_dtype=F32, after=None):
    M, K = a.shape
    n = dy.shape[1] // G
    tn = _tile(n, 1024)
    tk = _row_tile(K, 512 if tn > 1024 else 1024)
    tm = _row_tile(M, 1024 if tk > 1024 else 2048)
    r = n // tn
    return _mm_call(
        name, ((0,), (0,)), (K // tk, G * r, M // tm),
        pl.BlockSpec((tm, tk), lambda i, j, k: (k, i)),
        pl.BlockSpec((tm, tn), lambda i, j, k: (k, j)),
        pl.BlockSpec((None, tk, tn), lambda i, j, k: (j // r, i, j % r)), (G, K, n), (tk, tn),
        a, dy, (out_dtype,), None, (), after)[0]


ROW_BLOCK_BYTES = 12 * 1024 * 1024


def rowwise(name, fn, rows, vecs, outs, accs=(), after=None):
    rows = [r if isinstance(r, tuple) else (r, r.shape[1], 0) for r in rows]
    rows = [r if len(r) == 4 else r + (None,) for r in rows]
    n_fn = len(rows) + len(vecs)
    vecs = list(vecs) + _after_arg(after)
    S = rows[0][0].shape[-2]
    row_bytes = sum(w * a.dtype.itemsize for a, w, _, _ in rows) + sum(w * jnp.dtype(dt).itemsize for w, dt in outs)
    tm = _row_tile(S, 1024)
    while tm > 256 and tm * row_bytes > ROW_BLOCK_BYTES:
        tm //= 2
    n_r, n_v, n_o, n_a = len(rows), len(vecs), len(outs), len(accs)

    def body(*refs):
        ins = [ref[...] for ref in refs[:n_r + n_v]]
        o_refs = refs[n_r + n_v:n_r + n_v + n_o]
        a_refs = refs[n_r + n_v + n_o:]
        res = fn(*ins[:n_fn])
        res = res if isinstance(res, (tuple, list)) else (res,)
        for ref, val in zip(o_refs, res[:n_o]):
            ref[...] = val.astype(ref.dtype)
        if n_a:
            @pl.when(pl.program_id(0) == 0)
            def _():
                for ref in a_refs:
                    ref[...] = jnp.zeros_like(ref)
            for ref, val in zip(a_refs, res[n_o:]):
                ref[...] += val

    in_specs = [pl.BlockSpec((tm, w), functools.partial(lambda cb, i: (i, cb), cb)) if slab is None else
                pl.BlockSpec((None, tm, w), functools.partial(lambda cb, slab, i: (slab, i, cb), cb, slab))
                for _, w, cb, slab in rows]
    in_specs += [pl.BlockSpec(v.shape, lambda i: (0, 0)) for v in vecs]
    out_specs = [pl.BlockSpec((tm, w), lambda i: (i, 0)) for w, _ in outs]
    out_specs += [pl.BlockSpec(a, lambda i: (0, 0)) for a in accs]
    out_shape = [jax.ShapeDtypeStruct((S, w), dt) for w, dt in outs]
    out_shape += [jax.ShapeDtypeStruct(a, F32) for a in accs]
    res = pl.pallas_call(
        body, name=name, out_shape=out_shape, grid=(S // tm,),
        in_specs=in_specs, out_specs=out_specs,
        compiler_params=_params(("arbitrary",)),
    )(*[r[0] for r in rows], *vecs)
    return res


def _rms(x):
    return x * lax.rsqrt(jnp.mean(x * x, axis=-1, keepdims=True) + EPS)


def _modulate(x, gain, scale, shift):
    return _rms(x) * gain * (1.0 + scale) + shift


def _ada_slices(ada_raw, bias):
    ada = ada_raw + bias
    return [ada[:, i * D_MODEL:(i + 1) * D_MODEL] for i in range(ada.shape[1] // D_MODEL)]


def _norm_wide_heads(y, gain, heads):
    w = y.shape[1] // heads
    return jnp.concatenate([_rms(y[:, h * w:(h + 1) * w]) * gain[:, h * w:(h + 1) * w] for h in range(heads)], axis=1)


def _norm_fox_heads(x, gain):
    outs = []
    for p in range(x.shape[1] // LANES):
        blk = x[:, p * LANES:(p + 1) * LANES]
        low = lax.broadcasted_iota(jnp.int32, blk.shape, 1) < FOX_DH
        sq = blk * blk
        ss_low = jnp.sum(jnp.where(low, sq, 0.0), axis=1, keepdims=True)
        ss_high = jnp.sum(jnp.where(low, 0.0, sq), axis=1, keepdims=True)
        outs.append(blk * lax.rsqrt(jnp.where(low, ss_low, ss_high) * (1.0 / FOX_DH) + EPS))
    return jnp.concatenate(outs, axis=1) * gain


def _silu(x):
    return x * jax.nn.sigmoid(x)


def _log_sigmoid(z):
    return -(jnp.maximum(-z, 0.0) + jnp.log(1.0 + jnp.exp(-jnp.abs(z))))


def _rotate(x, cos, sin, heads, sign):
    w = x.shape[1] // heads
    half = w // 2
    outs = []
    for h in range(heads):
        x1 = x[:, h * w:h * w + half]
        x2 = x[:, h * w + half:(h + 1) * w]
        outs += [x1 * cos - sign * x2 * sin, sign * x1 * sin + x2 * cos]
    return jnp.concatenate(outs, axis=1)


def _vjp(fn, primals, ct):
    return jax.vjp(fn, *primals)[1](ct)


_LOG_GAMMAS = [float(np.log(np.float32(1.0) - np.float32(2.0) ** np.float32(-5.0 - h))) for h in range(RET_HEADS)]


RET_ROWS = 1024


def retention(name, q, k, v, reverse, out_dtype=F32):
    (qa, dk, qo), (ka, _, ko), (va, dv, vo) = q, k, v
    S = qa.shape[0]
    C = RET_CHUNK
    rows = _row_tile(S, RET_ROWS)
    nb = S // rows

    def body(q_ref, k_ref, v_ref, o_ref, state):
        h = pl.program_id(0)

        @pl.when(pl.program_id(1) == 0)
        def _():
            state[...] = jnp.zeros_like(state)

        log_g = jnp.float32(_LOG_GAMMAS[RET_HEADS - 1])
        for i in range(RET_HEADS - 2, -1, -1):
            log_g = jnp.where(h == i, jnp.float32(_LOG_GAMMAS[i]), log_g)
        row = lax.broadcasted_iota(jnp.int32, (C, C), 0)
        col = lax.broadcasted_iota(jnp.int32, (C, C), 1)
        rel = (col - row if reverse else row - col).astype(F32)
        decay = jnp.where(rel >= 0, jnp.exp(log_g * jnp.maximum(rel, 0.0)), 0.0)
        j = lax.broadcasted_iota(jnp.int32, (C, 1), 0).astype(F32)
        q_decay = jnp.exp(log_g * (C - j if reverse else j + 1.0))
        k_decay = jnp.exp(log_g * (j if reverse else C - 1.0 - j))
        chunk_decay = jnp.exp(jnp.full((1, 1), log_g * C, F32))

        chunks = range(rows // C)
        for ci in (reversed(chunks) if reverse else chunks):
            rs = slice(ci * C, (ci + 1) * C)
            qc = q_ref[rs, :].astype(BF16)
            kf = k_ref[rs, :].astype(F32)
            vc = v_ref[rs, :].astype(BF16)
            scores = lax.dot_general(qc, kf.astype(BF16), (((1,), (1,)), ((), ())), preferred_element_type=F32) * decay
            intra = jnp.dot(scores.astype(BF16), vc, preferred_element_type=F32)
            cross = jnp.dot(qc, state[...].astype(BF16), preferred_element_type=F32) * q_decay
            o_ref[rs, :] = (intra + cross).astype(o_ref.dtype)
            upd = lax.dot_general((kf * k_decay).astype(BF16), vc, (((0,), (0,)), ((), ())), preferred_element_type=F32)
            state[...] = state[...] * chunk_decay + upd

    def block(i):
        return nb - 1 - i if reverse else i

    return pl.pallas_call(
        body, name=name, out_shape=jax.ShapeDtypeStruct((S, RET_HEADS * dv), out_dtype),
        grid=(RET_HEADS, nb),
        in_specs=[pl.BlockSpec((rows, dk), lambda h, i: (block(i), qo + h)),
                  pl.BlockSpec((rows, dk), lambda h, i: (block(i), ko + h)),
                  pl.BlockSpec((rows, dv), lambda h, i: (block(i), vo + h))],
        out_specs=pl.BlockSpec((rows, dv), lambda h, i: (block(i), h)),
        scratch_shapes=[pltpu.VMEM((dk, dv), F32)],
        compiler_params=_params(("parallel", "arbitrary")),
    )(qa, ka, va)


FOX_T = 256
N_PAIR = FOX_HEADS // 2
FOX_SCALE = FOX_DH ** -0.5


def _fox_heads(q2):
    low = lax.broadcasted_iota(jnp.int32, (1, LANES), 1) < FOX_DH
    return [(mask, jnp.where(mask, q2 * FOX_SCALE, 0.0).astype(BF16)) for mask in (low, jnp.logical_not(low))]


def _fox_parts(j, t):
    return ([(0, j * t, False)] if j else []) + [(j * t, (j + 1) * t, True)]


def _fox_scores(qa, k_ref, ft_ref, head, lo, hi, diagonal):
    k_blk = k_ref[lo:hi, :].astype(BF16)
    s = lax.dot_general(qa, k_blk, (((1,), (1,)), ((), ())), preferred_element_type=F32) - ft_ref[pl.ds(head, 1), lo:hi]
    if diagonal:
        n = hi - lo
        s = jnp.where(lax.broadcasted_iota(jnp.int32, (n, n), 1) <= lax.broadcasted_iota(jnp.int32, (n, n), 0), s, -jnp.inf)
    return s


def fox_forward(name, qn, kn, kvf, f_cum_t):
    S = qn.shape[0]
    t = _row_tile(S, FOX_T)
    v_block0 = D_MODEL // LANES

    def variant(j, pair, q_ref, k_ref, v_ref, ft_ref, y_ref, lse_ref):
        ys, lses = [], []
        for a, (mask, qa) in enumerate(_fox_heads(q_ref[...])):
            parts = [(lo, hi, _fox_scores(qa, k_ref, ft_ref, 2 * pair + a, lo, hi, dg)) for lo, hi, dg in _fox_parts(j, t)]
            m = functools.reduce(jnp.maximum, [jnp.max(s, axis=1, keepdims=True) for _, _, s in parts])
            l, acc = 0.0, 0.0
            for lo, hi, s in parts:
                e = jnp.exp(s - m)
                l = l + jnp.sum(e, axis=1, keepdims=True)
                acc = acc + jnp.dot(e.astype(BF16), v_ref[lo:hi, :].astype(BF16), preferred_element_type=F32)
            ys.append(acc / l)
            lses.append(m + jnp.log(l))
        low = lax.broadcasted_iota(jnp.int32, (1, LANES), 1) < FOX_DH
        y_ref[...] = jnp.where(low, ys[0], ys[1])
        lse_ref[...] = jnp.where(low, lses[0], lses[1])

    def body(*refs):
        pair, i = pl.program_id(0), pl.program_id(1)
        for j in range(S // t):
            pl.when(i == j)(functools.partial(variant, j, pair, *refs))

    return pl.pallas_call(
        body, name=name,
        out_shape=[jax.ShapeDtypeStruct((S, D_MODEL), F32), jax.ShapeDtypeStruct((S, D_MODEL), F32)],
        grid=(N_PAIR, S // t),
        in_specs=[pl.BlockSpec((t, LANES), lambda p, i: (i, p)),
                  pl.BlockSpec((S, LANES), lambda p, i: (0, p)),
                  pl.BlockSpec((S, LANES), lambda p, i: (0, v_block0 + p)),
                  pl.BlockSpec((LANES, S), lambda p, i: (0, 0))],
        out_specs=[pl.BlockSpec((t, LANES), lambda p, i: (i, p)),
                   pl.BlockSpec((t, LANES), lambda p, i: (i, p))],
        compiler_params=_params(("parallel", "arbitrary")),
    )(qn, kn, kvf, f_cum_t)


def fox_backward(name, qn, kn, kvf, f_cum_t, y, dy, lse):
    S = qn.shape[0]
    t = _row_tile(S, FOX_T)
    v_block0 = D_MODEL // LANES

    def variant(j, pair, q_ref, k_ref, v_ref, ft_ref, y_ref, dy_ref, lse_ref, dq_ref, dk_ref, dv_ref, dfq_ref, dfk_ref,
                dv_acc):
        y2, dy2, lse2 = y_ref[...], dy_ref[...], lse_ref[...]
        lane = lax.broadcasted_iota(jnp.int32, (t, LANES), 1)
        dqs, dfq = [], jnp.zeros((t, LANES), F32)
        for a, (mask, qa) in enumerate(_fox_heads(q_ref[...].astype(F32))):
            lse_a = jnp.max(jnp.where(mask, lse2, -jnp.inf), axis=1, keepdims=True)
            dy_a = jnp.where(mask, dy2, 0.0)
            delta = jnp.sum(dy_a * y2, axis=1, keepdims=True)
            dy_b = dy_a.astype(BF16)
            dq, row_sum = 0.0, 0.0
            for lo, hi, dg in _fox_parts(j, t):
                p = jnp.exp(_fox_scores(qa, k_ref, ft_ref, 2 * pair + a, lo, hi, dg) - lse_a)
                dp = lax.dot_general(dy_b, v_ref[lo:hi, :].astype(BF16), (((1,), (1,)), ((), ())), preferred_element_type=F32)
                ds = p * (dp - delta)
                row_sum = row_sum + jnp.sum(ds, axis=1, keepdims=True)
                dfk_ref[pl.ds(a, 1), lo:hi] += -jnp.sum(ds, axis=0, keepdims=True)
                ds_b = ds.astype(BF16)
                dq = dq + jnp.dot(ds_b, k_ref[lo:hi, :].astype(BF16), preferred_element_type=F32)
                dk_ref[lo:hi, :] += lax.dot_general(ds_b, qa, (((0,), (0,)), ((), ())), preferred_element_type=F32)
                dv_acc[lo:hi, :] += lax.dot_general(p.astype(BF16), dy_b, (((0,), (0,)), ((), ())), preferred_element_type=F32)
            dqs.append(dq * FOX_SCALE)
            dfq = dfq + jnp.where(lane == 2 * pair + a, row_sum, 0.0)
        low = lax.broadcasted_iota(jnp.int32, (1, LANES), 1) < FOX_DH
        dq_ref[...] = jnp.where(low, dqs[0], dqs[1])
        dfq_ref[...] = dfq

    def body(*refs):
        pair, i = pl.program_id(0), pl.program_id(1)
        dk_ref, dv_ref, _, dfk_ref, dv_acc = refs[8:13]

        @pl.when(i == 0)
        def _():
            dk_ref[...] = jnp.zeros_like(dk_ref)
            dv_acc[...] = jnp.zeros_like(dv_acc)
            dfk_ref[...] = jnp.zeros_like(dfk_ref)

        for j in range(S // t):
            pl.when(i == j)(functools.partial(variant, j, pair, *refs))

        @pl.when(i == S // t - 1)
        def _():
            dv_ref[...] = dv_acc[...].astype(dv_ref.dtype)

    row_blk = pl.BlockSpec((t, LANES), lambda p, i: (i, p))
    col_blk = pl.BlockSpec((S, LANES), lambda p, i: (0, p))
    return pl.pallas_call(
        body, name=name,
        out_shape=[jax.ShapeDtypeStruct((S, D_MODEL), F32)] * 2
        + [jax.ShapeDtypeStruct((S, D_MODEL), BF16), jax.ShapeDtypeStruct((N_PAIR, S, LANES), F32),
           jax.ShapeDtypeStruct((N_PAIR, 8, S), F32)],
        grid=(N_PAIR, S // t),
        in_specs=[row_blk, col_blk,
                  pl.BlockSpec((S, LANES), lambda p, i: (0, v_block0 + p)),
                  pl.BlockSpec((LANES, S), lambda p, i: (0, 0)),
                  row_blk, row_blk, row_blk],
        out_specs=[row_blk, col_blk, col_blk,
                   pl.BlockSpec((None, t, LANES), lambda p, i: (p, i, 0)),
                   pl.BlockSpec((None, 8, S), lambda p, i: (p, 0, 0))],
        scratch_shapes=[pltpu.VMEM((S, LANES), F32)],
        compiler_params=_params(("parallel", "arbitrary")),
    )(qn, kn, kvf, f_cum_t, y, dy, lse)


def cumsum_rows(name, x, reverse):
    S = x.shape[0]
    C = LANES
    nc = S // C

    def body(x_ref, o_ref):
        row = lax.broadcasted_iota(jnp.int32, (C, C), 0)
        col = lax.broadcasted_iota(jnp.int32, (C, C), 1)
        tri = jnp.where(col >= row if reverse else col <= row, 1.0, 0.0).astype(F32)
        carry = jnp.zeros((1, LANES), F32)
        for i in (range(nc - 1, -1, -1) if reverse else range(nc)):
            blk = x_ref[i * C:(i + 1) * C, :]
            loc = jnp.dot(tri, blk, preferred_element_type=F32, precision=lax.Precision.HIGHEST)
            o_ref[i * C:(i + 1) * C, :] = loc + carry
            carry = carry + (loc[0:1, :] if reverse else loc[C - 1:C, :])

    return pl.pallas_call(body, name=name, out_shape=jax.ShapeDtypeStruct((S, LANES), F32),
                          compiler_params=_params())(x)


def adamw(name, parts, w, m, v):
    L, R, C = w.shape
    tr = _row_tile(R, 256)
    nr = R // tr
    counts = [len(p) for p in parts]

    def body(*refs):
        w_ref, m_ref, v_ref, g_out, d_out, m_out, v_out = refs[sum(counts):]
        for layer in range(L):
            p_refs = refs[sum(counts[:layer]):sum(counts[:layer + 1])]

            @pl.when(pl.program_id(0) == layer)
            def _(p_refs=p_refs, slots=[n for _, n in parts[layer]]):
                g = None
                for p_ref, n in zip(p_refs, slots):
                    for i in range(n):
                        g = p_ref[i].astype(F32) if g is None else g + p_ref[i].astype(F32)
                m2 = ADAM_B1 * m_ref[...] + (1.0 - ADAM_B1) * g
                v2 = ADAM_B2 * v_ref[...] + (1.0 - ADAM_B2) * jnp.square(g)
                m_hat = m2 / (1.0 - ADAM_B1 ** ADAM_STEP)
                v_hat = v2 / (1.0 - ADAM_B2 ** ADAM_STEP)
                g_out[...] = g
                d_out[...] = -ADAM_LR * (m_hat / (jnp.sqrt(v_hat) + ADAM_EPS) + ADAM_WD * w_ref[...])
                m_out[...] = m2
                v_out[...] = v2

    def part_spec(layer, n):
        return pl.BlockSpec((n, tr, C), lambda l, i: (0, jnp.where(l == layer, i, jnp.where(l < layer, 0, nr - 1)), 0))

    blk = pl.BlockSpec((None, tr, C), lambda l, i: (l, i, 0))
    return pl.pallas_call(
        body, name=name, out_shape=[jax.ShapeDtypeStruct((L, R, C), F32)] * 4, grid=(L, nr),
        in_specs=[part_spec(layer, n) for layer in range(L) for _, n in parts[layer]] + [blk, blk, blk],
        out_specs=[blk] * 4, compiler_params=_params(("arbitrary", "arbitrary")),
    )(*[a for layer in parts for a, _ in layer], w, m, v)


def kernel(x, c, positions, norm_mix_gain, norm_mlp_gain, w_ada, b_ada, w_mlp_in, w_mlp_out, ret_w_in, ret_norm_gain, ret_w_out, kv_norm_gain, kv_w_ada, kv_b_ada, kv_w, forget_bias, k_norm_gain, fox_w_in, q_norm_gain, fox_w_out, loss_target, m_norm_mix_gain, m_norm_mlp_gain, m_w_ada, m_b_ada, m_w_mlp_in, m_w_mlp_out, m_ret_w_in, m_ret_norm_gain, m_ret_w_out, m_kv_norm_gain, m_kv_w_ada, m_kv_b_ada, m_kv_w, m_forget_bias, m_k_norm_gain, m_fox_w_in, m_q_norm_gain, m_fox_w_out, v_norm_mix_gain, v_norm_mlp_gain, v_w_ada, v_b_ada, v_w_mlp_in, v_w_mlp_out, v_ret_w_in, v_ret_norm_gain, v_ret_w_out, v_kv_norm_gain, v_kv_w_ada, v_kv_b_ada, v_kv_w, v_forget_bias, v_k_norm_gain, v_fox_w_in, v_q_norm_gain, v_fox_w_out):
    D = D_MODEL
    S = x.shape[1]
    x0 = x.reshape(S, D)
    target = loss_target.reshape(S, D)
    me = 4 * lax.axis_index("x") + 2 * lax.axis_index("y") + lax.axis_index("c")
    n_ada = w_ada.shape[2]
    n_kvada = kv_w_ada.shape[1]
    n_kv = kv_w.shape[1]

    c_all, ret_gain = all_gather("gather_c", [c.reshape(D // LANES, LANES), ret_norm_gain.reshape(RET_HEADS, -1)])
    ret_gain = jnp.transpose(ret_gain, (1, 0, 2)).reshape(1, RET_HEADS * RET_V)
    c_act = rowwise("silu_c", _silu, [c_all.reshape(N_DEV, D)], [], [(D, F32)])[0]
    w_ada_cat = jnp.concatenate([w_ada[0], w_ada[1], kv_w_ada], axis=1).astype(BF16)[None]
    n_cat = 2 * n_ada + n_kvada
    ada_part = mm_nn("ada_proj", c_act, w_ada_cat)[0]
    ada_mine = all_to_all("ada_rows", [ada_part.reshape(N_DEV, n_cat // LANES, LANES)])[0]
    ada_mine = ada_mine.reshape(N_DEV, n_cat)
    ada_raw = [ada_mine[:, l * n_ada:(l + 1) * n_ada].reshape(1, 6 * D) for l in range(2)]
    kvada_raw = ada_mine[:, 2 * n_ada:].reshape(1, 2 * D)
    kv_bias = kv_b_ada.reshape(1, 2 * D)
    kv_gain = kv_norm_gain.reshape(1, D)
    fb = jnp.pad(forget_bias.reshape(1, FOX_HEADS), ((0, 0), (0, LANES - FOX_HEADS)))
    k_gain = jnp.tile(k_norm_gain.reshape(1, FOX_DH), (1, FOX_HEADS))
    q_gain = jnp.tile(q_norm_gain.reshape(1, FOX_DH), (1, FOX_HEADS))

    w_names = ["ret_in", "ret_out", "mlp_in0", "mlp_out0", "kv", "fox_in", "fox_out", "mlp_in1", "mlp_out1"]
    shards = [ret_w_in[0].astype(BF16), ret_w_out[0].astype(BF16), w_mlp_in[0].astype(BF16), w_mlp_out[0].astype(BF16),
              kv_w.T.astype(BF16), fox_w_in[0].astype(BF16), fox_w_out[0].astype(BF16), w_mlp_in[1].astype(BF16),
              w_mlp_out[1].astype(BF16)]
    two_level = {"ret_in", "ret_out", "mlp_in0", "mlp_out0"}
    by_columns = {"ret_in", "mlp_in0", "mlp_in1", "fox_in"}
    w_plans = {name: plan_gather if name in two_level else plan_gather_direct for name in w_names}
    w_handles, token = exchange_start("gather_weights_start", shards,
                                      [lax.empty((a.shape[0], N_DEV * a.shape[1]) if name in by_columns else
                                                 (N_DEV,) + a.shape, a.dtype) for name, a in zip(w_names, shards)],
                                      [w_plans[name] for name in w_names], after=(ada_mine, ret_gain))
    w_handles = dict(zip(w_names, w_handles))

    forwards = {}

    def forward_early(name, after):
        arrived = exchange_wait("gather_wait_" + name, [w_handles[name]], w_plans[name], after)[1]
        forwards[name], tok = exchange_start("gather_forward_start_" + name, None, arrived, plan_forward)
        return tok

    def weight(name, after):
        if name in two_level:
            return exchange_wait("gather_forward_wait_" + name, forwards[name], plan_forward, after)[1][0]
        return exchange_wait("gather_wait_" + name, [w_handles[name]], w_plans[name], after)[1][0]

    pos = positions.reshape(S, 1).astype(F32)
    half = RET_QK // 2
    inv_freq = jnp.asarray((ROPE_BASE ** (-np.arange(half, dtype=np.float32) / half)).reshape(1, half), F32)

    def angles(p, f):
        ang = p * f
        return jnp.cos(ang), jnp.sin(ang)

    cos, sin = rowwise("rope_table", angles, [pos], [inv_freq], [(half, F32), (half, F32)])

    def mod_mix(layer):
        def fn(xb, ada, bias, gain):
            sh, sc = _ada_slices(ada, bias)[:2]
            return _modulate(xb, gain[layer:layer + 1], sc, sh)
        return fn

    def mod_mlp(layer):
        def fn(xb, ada, bias, gain):
            sh, sc = _ada_slices(ada, bias)[3:5]
            return _modulate(xb, gain[layer:layer + 1], sc, sh)
        return fn

    h1_0 = rowwise("mod_mix0", mod_mix(0), [x0], [ada_raw[0], b_ada[0:1], norm_mix_gain], [(D, BF16)], after=token)[0]
    W_ret_in = weight("ret_in", forward_early("ret_in", h1_0))[None]
    proj = mm_nn("ret_proj", h1_0, W_ret_in, (BF16,))[0]
    token = forward_early("ret_out", proj)

    def rope_fwd(qb, kb, cs, sn):
        return (_rotate(qb.astype(F32), cs, sn, RET_HEADS, 1.0),
                _rotate(kb.astype(F32), cs, sn, RET_HEADS, 1.0) * (RET_QK ** -0.5))

    q_rot, k_rot = rowwise("rope", rope_fwd, [(proj, D, 0), (proj, D, 1), cos, sin], [], [(D, BF16), (D, BF16)],
                           after=token)
    v_ret = (proj, RET_V, (2 * D) // RET_V)
    y_ret = retention("ret_fwd", (q_rot, RET_QK, 0), (k_rot, RET_QK, 0), v_ret, reverse=False)

    def ret_gate(yb, gb, gain):
        return _silu(gb.astype(F32)) * _norm_wide_heads(yb, gain, RET_HEADS)

    mixin0 = rowwise("ret_gate", ret_gate, [y_ret, (proj, 2 * D, 2)], [ret_gain], [(2 * D, BF16)])[0]
    W_ret_out = weight("ret_out", mixin0).reshape(1, 2 * D, D)
    FUSED_ROWS = 512

    def residual_mod(layer, slot):
        def fn(xb, bb, ada, bias, gain):
            s = _ada_slices(ada, bias)
            xn = xb + s[2] * bb
            return xn, _modulate(xn, gain[layer:layer + 1], s[4], s[3])
        return fn

    def with_residual(fn):
        return lambda acc, xb, *vs: (acc,) + tuple(fn(xb, acc, *vs))

    mix0, x1, h2_0 = mm_nn("ret_out", mixin0, W_ret_out, (F32, F32, BF16), with_residual(residual_mod(0, 0)), (x0,),
                           forward_early("mlp_in0", W_ret_out), (ada_raw[0], b_ada[0:1], norm_mlp_gain), FUSED_ROWS)

    W_mlp_in, W_mlp_out = {}, {}

    def mlp_forward(tag, h2, layer, out_dtypes=(F32,), epilogue=None, extra=(), vecs=()):
        W_mlp_in[layer] = weight("mlp_in" + tag, h2)[None]
        early = forward_early("mlp_out" + tag, W_mlp_in[layer]) if "mlp_out" + tag in two_level else None
        u, act = mm_nn("mlp_in" + tag, h2, W_mlp_in[layer], (BF16, BF16), after=early,
                       epilogue=lambda acc: (acc, jnp.square(jnp.maximum(acc, 0.0))))
        W_mlp_out[layer] = weight("mlp_out" + tag, act).reshape(1, 4 * D, D)
        return u, act, mm_nn("mlp_out" + tag, act, W_mlp_out[layer], out_dtypes, epilogue, extra, None, vecs,
                             FUSED_ROWS if epilogue else 2048)

    def res_mlp0(xb, bb, ada0, bias0, ada1, bias1, kva, kvb, gain_mix, gain_kv):
        xn = xb + _ada_slices(ada0, bias0)[5] * bb
        s1 = _ada_slices(ada1, bias1)
        kv_shift, kv_scale = _ada_slices(kva, kvb)
        return xn, _modulate(xn, gain_kv, kv_scale, kv_shift), _modulate(xn, gain_mix[1:2], s1[1], s1[0])

    u0, act0, (mlp0,) = mlp_forward("0", h2_0, 0)
    x2, h_kv, h1_1 = rowwise("res_mlp0", res_mlp0, [x1, mlp0],
                             [ada_raw[0], b_ada[0:1], ada_raw[1], b_ada[1:2], kvada_raw, kv_bias, norm_mix_gain, kv_gain],
                             [(D, F32), (D, BF16), (D, BF16)])

    W_kv = jnp.pad(weight("kv", h_kv).reshape(N_DEV * n_kv, D), ((0, KV_PAD - N_DEV * n_kv), (0, 0)))[None]
    kvf = mm_nt("kv_proj", h_kv, W_kv)[0]

    def kv_post(kb, fblk, kg, bias):
        head = lax.broadcasted_iota(jnp.int32, fblk.shape, 1) < FOX_HEADS
        return _norm_fox_heads(kb, kg), jnp.where(head, _log_sigmoid(fblk + bias), 0.0)

    kn, log_f = rowwise("kv_post", kv_post, [(kvf, D, 0), (kvf, LANES, 2 * D // LANES)], [k_gain, fb],
                        [(D, BF16), (LANES, F32)])
    f_cum = cumsum_rows("f_cumsum", log_f, reverse=False)
    f_cum_t = f_cum.T

    W_fox_in = weight("fox_in", kvf)[None]
    qo = mm_nn("fox_proj", h1_1, W_fox_in)[0]
    qn = rowwise("q_norm", _norm_fox_heads, [(qo, D, 0)], [q_gain], [(D, BF16)])[0]
    y_att, lse = fox_forward("fox_fwd", qn, kn, kvf, f_cum_t)
    mixin1 = rowwise("fox_gate", lambda ob, yb: jax.nn.sigmoid(ob) * yb, [(qo, D, 1), y_att], [], [(D, BF16)])[0]
    W_fox_out = weight("fox_out", mixin1).reshape(1, D, D)
    mix1, x3, h2_1 = mm_nn("fox_out", mixin1, W_fox_out, (F32, F32, BF16), with_residual(residual_mod(1, 0)), (x2,),
                           None, (ada_raw[1], b_ada[1:2], norm_mlp_gain), FUSED_ROWS)
    u1, act1, (mlp1,) = mlp_forward("1", h2_1, 1)

    def scatter_start(tag, gws, after=()):
        lands = [lax.empty((4,) + g.shape[1:], g.dtype) for g in gws]
        return exchange_start("scatter_sibling_start_" + tag, gws, lands, plan_to_sibling, after)

    def scatter_relay(tag, handles, after, start_after=()):
        gws, from_sibling = exchange_wait("scatter_sibling_wait_" + tag, handles, plan_to_sibling, after)
        sums = [chip_sum("chip_sum_%s%d" % (tag, i), g, s) for i, (g, s) in enumerate(zip(gws, from_sibling))]
        lands = [lax.empty((3,) + s.shape[1:], s.dtype) for s in sums]
        return exchange_start("scatter_owner_start_" + tag, sums, lands, plan_to_owners, start_after)

    def scatter_direct_start(tag, gws):
        lands = [lax.empty(g.shape, g.dtype) for g in gws]
        return exchange_start("scatter_direct_start_" + tag, gws, lands, plan_scatter_direct)

    def scatter_direct_finish(tag, handles, after):
        return [[(r, N_DEV)] for r in exchange_wait("scatter_direct_wait_" + tag, handles, plan_scatter_direct, after)[1]]

    def scatter_finish(tag, handles, after):
        sums, received = exchange_wait("scatter_owner_wait_" + tag, handles, plan_to_owners, after)
        return [[(s, 1), (r, 3)] for s, r in zip(sums, received)]

    def loss_head(xb, bb, tb, ada, bias):
        g2 = _ada_slices(ada, bias)[5]
        err = xb + g2 * bb - tb
        dx = err * (1.0 / D)
        loss = 0.5 * jnp.sum(jnp.sum(err * err, axis=1, keepdims=True) * (1.0 / D), axis=0, keepdims=True)
        return dx, (dx * g2), jnp.broadcast_to(loss, (1, LANES)), jnp.sum(dx * bb, axis=0, keepdims=True)

    dx4, dmlp1, loss_acc, dg2_1 = rowwise("loss_head", loss_head, [x3, mlp1, target], [ada_raw[1], b_ada[1:2]],
                                          [(D, F32), (D, BF16)], [(1, LANES), (1, D)])

    def mlp_backward(tag, dmlp, act, u, h2, layer, after=None):
        du = mm_nt("mlp_out_dx" + tag, dmlp, W_mlp_out[layer], (BF16,), extra=(u,), after=after,
                   epilogue=lambda acc, ub: (acc * (2.0 * jnp.maximum(ub.astype(F32), 0.0)),))[0]
        gw_out = mm_tn("mlp_out_dw" + tag, act, dmlp, 1, BF16).reshape(N_DEV, -1, D)
        dh2 = mm_nt("mlp_in_dx" + tag, du, W_mlp_in[layer], (BF16,))[0]
        gw_in = mm_tn("mlp_in_dw" + tag, h2, du, N_DEV, BF16)
        return dh2, gw_in, gw_out

    def mod_backward(layer, slots, gate_slot):
        def fn(xb, dhb, dresb, branchb, ada, bias, gain):
            s = _ada_slices(ada, bias)
            g = gain[layer:layer + 1]
            dx, dgain, dsc, dsh = _vjp(_modulate, (xb, g, s[slots[1]], s[slots[0]]), dhb.astype(F32))
            dx = dx + dresb
            d_branch = dx * s[gate_slot]
            return dx, d_branch, dgain, dsc, dsh, jnp.sum(dx * branchb, axis=0, keepdims=True)
        return fn

    vec = (1, D)
    dh2_1, gw_mlp_in1, gw_mlp_out1 = mlp_backward("1", dmlp1, act1, u1, h2_1, 1)
    scat_a1, token_a1 = scatter_direct_start("a1", [gw_mlp_in1, gw_mlp_out1])
    dx3, dmix1, dgain_mlp1, dsc2_1, dsh2_1, dg1_1 = rowwise(
        "mod_mlp1_bwd", mod_backward(1, (3, 4), 2), [x3, dh2_1, dx4, mix1], [ada_raw[1], b_ada[1:2], norm_mlp_gain],
        [(D, F32), (D, BF16)], [vec] * 4, after=token_a1)
    dmixin1 = mm_nt("fox_out_dx", dmix1, W_fox_out)[0]
    gw_fox_out = mm_tn("fox_out_dw", mixin1, dmix1, 1, BF16).reshape(N_DEV, -1, D)

    def fox_gate_bwd(db, ob, yb):
        sg = jax.nn.sigmoid(ob)
        return db * sg, db * yb * sg * (1.0 - sg)

    dy_att, d_og = rowwise("fox_gate_bwd", fox_gate_bwd, [dmixin1, (qo, D, 1), y_att], [], [(D, F32), (D, BF16)])
    dqn, dkn, dv_att, dfq, dfk = fox_backward("fox_bwd", qn, kn, kvf, f_cum_t, y_att, dy_att, lse)

    def q_norm_bwd(qb, db, ogb, gain):
        dq, dgain = _vjp(_norm_fox_heads, (qb, gain), db)
        return jnp.concatenate([dq, ogb.astype(F32)], axis=1), dgain

    dqo, dq_gain = rowwise("q_norm_bwd", q_norm_bwd, [(qo, D, 0), dqn, d_og], [q_gain], [(2 * D, BF16)], [vec])
    dh1_1 = mm_nt("fox_proj_dx", dqo, W_fox_in, (BF16,))[0]
    gw_fox_in = mm_tn("fox_proj_dw", h1_1, dqo, N_DEV, BF16)

    dfk_rows = jnp.pad(dfk[:, :2, :].reshape(FOX_HEADS, S).T, ((0, 0), (0, LANES - FOX_HEADS)))

    def df_total(*blks):
        tot = blks[0]
        for b in blks[1:]:
            tot = tot + b
        return tot

    d_fcum = rowwise("df_sum", df_total, [dfk_rows] + [(dfq, LANES, 0, p) for p in range(N_PAIR)], [], [(LANES, F32)])[0]
    d_logf = cumsum_rows("df_cumsum", d_fcum, reverse=True)

    def kv_post_bwd(kb, fblk, dkb, dvb, dlf, kg, bias):
        dk, dgain = _vjp(_norm_fox_heads, (kb, kg), dkb)
        df = dlf * (1.0 / (1.0 + jnp.exp(fblk + bias)))
        return jnp.concatenate([dk, dvb.astype(F32), df], axis=1), dgain, jnp.sum(df, axis=0, keepdims=True)

    dkvf, dk_gain, dfb = rowwise("kv_post_bwd", kv_post_bwd,
                                 [(kvf, D, 0), (kvf, LANES, 2 * D // LANES), dkn, dv_att, d_logf], [k_gain, fb],
                                 [(KV_PAD, BF16)], [vec, (1, LANES)])
    dh_kv = mm_nn("kv_proj_dx", dkvf, W_kv, (BF16,))[0]
    gw_kv = mm_tn("kv_proj_dw", dkvf, h_kv, 1, BF16)[0, :N_DEV * n_kv].reshape(N_DEV, n_kv, D)
    scat_a2, token_a = scatter_direct_start("a2", [gw_fox_out, gw_fox_in, gw_kv])

    def x2_bwd(xb, dh1b, dhkb, dresb, branchb, ada0, bias0, ada1, bias1, kva, kvb, gain_mix, gain_kv):
        s1 = _ada_slices(ada1, bias1)
        kv_shift, kv_scale = _ada_slices(kva, kvb)
        dxa, dgain_mix, dsc1, dsh1 = _vjp(_modulate, (xb, gain_mix[1:2], s1[1], s1[0]), dh1b.astype(F32))
        dxb, dgain_kv, dkv_scale, dkv_shift = _vjp(_modulate, (xb, gain_kv, kv_scale, kv_shift), dhkb.astype(F32))
        dx = dresb + dxa + dxb
        g2 = _ada_slices(ada0, bias0)[5]
        return (dx, dx * g2, dgain_mix, dsc1, dsh1, dgain_kv, dkv_scale, dkv_shift,
                jnp.sum(dx * branchb, axis=0, keepdims=True))

    (dx2, dmlp0, dgain_mix1, dsc1_1, dsh1_1, dgain_kv, dkv_scale, dkv_shift, dg2_0) = rowwise(
        "x2_bwd", x2_bwd, [x2, dh1_1, dh_kv, dx3, mlp0],
        [ada_raw[0], b_ada[0:1], ada_raw[1], b_ada[1:2], kvada_raw, kv_bias, norm_mix_gain, kv_gain],
        [(D, F32), (D, BF16)], [vec] * 7, after=token_a)

    dh2_0, gw_mlp_in0, gw_mlp_out0 = mlp_backward("0", dmlp0, act0, u0, h2_0, 0)
    scat_b1, token_b1 = scatter_direct_start("b1", [gw_mlp_in0, gw_mlp_out0])
    dx1, dmix0, dgain_mlp0, dsc2_0, dsh2_0, dg1_0 = rowwise(
        "mod_mlp0_bwd", mod_backward(0, (3, 4), 2), [x1, dh2_0, dx2, mix0], [ada_raw[0], b_ada[0:1], norm_mlp_gain],
        [(D, F32), (D, BF16)], [vec] * 4, after=token_b1)
    dmixin0 = mm_nt("ret_out_dx", dmix0, W_ret_out, (BF16,))[0]
    gw_ret_out = mm_tn("ret_out_dw", mixin0, dmix0, 1, BF16).reshape(N_DEV, -1, D)
    scat_b2, token_b = scatter_direct_start("b2", [gw_ret_out])

    def ret_gate_bwd(db, yb, gb, gain):
        return _vjp(ret_gate, (yb, gb.astype(F32), gain), db.astype(F32))

    dy_ret, dgate, dret_gain = rowwise("ret_gate_bwd", ret_gate_bwd, [dmixin0, y_ret, (proj, 2 * D, 2)], [ret_gain],
                                       [(2 * D, BF16), (2 * D, BF16)], [(1, 2 * D)], after=token_b)
    dy_h = (dy_ret, RET_V, 0)
    dq_rot = retention("ret_dq", dy_h, v_ret, (k_rot, RET_QK, 0), reverse=False)
    dk_rot = retention("ret_dk", v_ret, dy_h, (q_rot, RET_QK, 0), reverse=True)
    dv_ret = retention("ret_dv", (k_rot, RET_QK, 0), (q_rot, RET_QK, 0), dy_h, reverse=True, out_dtype=BF16)

    def rope_bwd(dqb, dkb, dvb, dgb, cs, sn):
        dq = _rotate(dqb, cs, sn, RET_HEADS, -1.0)
        dk = _rotate(dkb, cs, sn, RET_HEADS, -1.0) * (RET_QK ** -0.5)
        return jnp.concatenate([dq, dk, dvb.astype(F32), dgb.astype(F32)], axis=1)

    dproj = rowwise("rope_bwd", rope_bwd, [dq_rot, dk_rot, dv_ret, dgate, cos, sin], [], [(6 * D, BF16)])[0]
    gw_ret_in = mm_tn("ret_proj_dw", h1_0, dproj, N_DEV, BF16)
    scat_c, token_c = scatter_start("c", [gw_ret_in])
    dh1_0 = mm_nt("ret_proj_dx", dproj, W_ret_in, (BF16,), after=token_c)[0]

    def x0_bwd(xb, dhb, dresb, ada, bias, gain):
        s = _ada_slices(ada, bias)
        dx, dgain, dsc, dsh = _vjp(_modulate, (xb, gain[0:1], s[1], s[0]), dhb.astype(F32))
        return dx + dresb, dgain, dsc, dsh

    grad_x, dgain_mix0, dsc1_0, dsh1_0 = rowwise("x0_bwd", x0_bwd, [x0, dh1_0, dx1],
                                                 [ada_raw[0], b_ada[0:1], norm_mix_gain], [(D, F32)], [vec] * 3)

    small = jnp.concatenate([
        dsh1_0, dsc1_0, dg1_0, dsh2_0, dsc2_0, dg2_0,
        dsh1_1, dsc1_1, dg1_1, dsh2_1, dsc2_1, dg2_1,
        dkv_shift, dkv_scale,
        dgain_mix0, dgain_mix1, dgain_mlp0, dgain_mlp1, dgain_kv,
        dret_gain,
        dq_gain.reshape(FOX_HEADS, FOX_DH).sum(axis=0).reshape(1, FOX_DH),
        dk_gain.reshape(FOX_HEADS, FOX_DH).sum(axis=0).reshape(1, FOX_DH),
        dfb,
        loss_acc,
    ], axis=1)
    small_all = all_gather("gather_small", [small.reshape(-1, LANES)])[0].reshape(N_DEV, 1, -1)
    loss = jnp.sum(small_all[:, 0, -1])
    scat_c, token_c = scatter_relay("c", scat_c, grad_x, start_after=(small_all,))
    o_ada = 14 * D
    d_ada = small_all[:, 0, :o_ada]
    d_cat = jnp.concatenate([
        lax.dynamic_slice_in_dim(d_ada[:, 0:6 * D], me * n_ada, n_ada, axis=1),
        lax.dynamic_slice_in_dim(d_ada[:, 6 * D:12 * D], me * n_ada, n_ada, axis=1),
        lax.dynamic_slice_in_dim(d_ada[:, 12 * D:14 * D], me * n_kvada, n_kvada, axis=1)], axis=1)
    gw_ada_cat = mm_tn("ada_dw", c_act, d_cat, 1, F32, after=token_c)[0]

    results = {}

    def update(name, parts, w, m, v, layers=1):
        per_layer = parts if layers > 1 else [parts]
        shape = w.shape
        C = shape[-1]
        R = int(np.prod(shape)) // (layers * C)
        per_layer = [p if isinstance(p, list) else [(p, p.shape[0])] for p in per_layer]
        per_layer = [[(a.reshape(a.shape[0], R, C), n) for a, n in p] for p in per_layer]
        outs = adamw("adamw_" + name, per_layer, w.reshape(layers, R, C), m.reshape(layers, R, C), v.reshape(layers, R, C))
        results[name] = tuple(t.reshape(shape) for t in outs)

    def small_parts(lo, width):
        return small_all[:, :, lo:lo + width]

    update("norm_mix_gain", jnp.concatenate([small_parts(o_ada, D), small_parts(o_ada + D, D)], axis=1),
           norm_mix_gain, m_norm_mix_gain, v_norm_mix_gain)
    update("norm_mlp_gain", jnp.concatenate([small_parts(o_ada + 2 * D, D), small_parts(o_ada + 3 * D, D)], axis=1),
           norm_mlp_gain, m_norm_mlp_gain, v_norm_mlp_gain)
    update("w_ada", [gw_ada_cat[None, :, :n_ada], gw_ada_cat[None, :, n_ada:2 * n_ada]], w_ada, m_w_ada, v_w_ada, layers=2)
    update("b_ada", jnp.concatenate([small_parts(0, 6 * D), small_parts(6 * D, 6 * D)], axis=1), b_ada, m_b_ada, v_b_ada)
    o_ret = o_ada + 5 * D
    n_rg = ret_norm_gain.shape[2]
    ret_gain_parts = small_parts(o_ret, 2 * D).reshape(N_DEV, RET_HEADS, RET_V)
    ret_gain_parts = lax.dynamic_slice_in_dim(ret_gain_parts, me * n_rg, n_rg, axis=2)
    update("ret_norm_gain", ret_gain_parts, ret_norm_gain, m_ret_norm_gain, v_ret_norm_gain)
    update("kv_norm_gain", small_parts(o_ada + 4 * D, D), kv_norm_gain, m_kv_norm_gain, v_kv_norm_gain)
    update("kv_w_ada", gw_ada_cat[None, :, 2 * n_ada:], kv_w_ada, m_kv_w_ada, v_kv_w_ada)
    update("kv_b_ada", small_parts(12 * D, 2 * D), kv_b_ada, m_kv_b_ada, v_kv_b_ada)
    o_q = o_ret + 2 * D
    update("forget_bias", small_parts(o_q + 2 * FOX_DH, FOX_HEADS), forget_bias, m_forget_bias, v_forget_bias)
    update("k_norm_gain", small_parts(o_q + FOX_DH, FOX_DH), k_norm_gain, m_k_norm_gain, v_k_norm_gain)
    update("q_norm_gain", small_parts(o_q, FOX_DH), q_norm_gain, m_q_norm_gain, v_q_norm_gain)

    r_mlp_in1, r_mlp_out1 = scatter_direct_finish("a1", scat_a1, results["q_norm_gain"][1])
    r_fox_out, r_fox_in, r_kv = scatter_direct_finish("a2", scat_a2, r_mlp_in1[0][0])
    r_mlp_in0, r_mlp_out0 = scatter_direct_finish("b1", scat_b1, r_kv[0][0])
    r_ret_out = scatter_direct_finish("b2", scat_b2, r_mlp_in0[0][0])[0]
    update("kv_w", r_kv, kv_w.T, m_kv_w.T, v_kv_w.T)
    results["kv_w"] = tuple(t.T for t in results["kv_w"])
    update("fox_w_in", r_fox_in, fox_w_in, m_fox_w_in, v_fox_w_in)
    update("fox_w_out", r_fox_out, fox_w_out, m_fox_w_out, v_fox_w_out)
    update("ret_w_out", r_ret_out, ret_w_out, m_ret_w_out, v_ret_w_out)
    update("w_mlp_in", [r_mlp_in0, r_mlp_in1], w_mlp_in, m_w_mlp_in, v_w_mlp_in, layers=2)
    update("w_mlp_out", [r_mlp_out0, r_mlp_out1], w_mlp_out, m_w_mlp_out, v_w_mlp_out, layers=2)
    r_ret_in = scatter_finish("c", scat_c, results["w_mlp_out"][1])[0]
    update("ret_w_in", r_ret_in, ret_w_in, m_ret_w_in, v_ret_w_in)

    order = ["norm_mix_gain", "norm_mlp_gain", "w_ada", "b_ada", "w_mlp_in", "w_mlp_out", "ret_w_in", "ret_norm_gain",
             "ret_w_out", "kv_norm_gain", "kv_w_ada", "kv_b_ada", "kv_w", "forget_bias", "k_norm_gain", "fox_w_in",
             "q_norm_gain", "fox_w_out"]
    out = [loss, grad_x.reshape(x.shape)]
    for slot in range(4):
        out += [results[n][slot] for n in order]
    return tuple(out)
```

```python
import functools
import math

import numpy as np
import jax
import jax.numpy as jnp
from jax import lax
from jax.experimental import pallas as pl
from jax.experimental.pallas import tpu as pltpu

F32 = jnp.float32
BF16 = jnp.bfloat16

N_DEV = 8
D_MODEL = 1024
RET_HEADS = 4
RET_QK = D_MODEL // RET_HEADS
RET_V = 2 * D_MODEL // RET_HEADS
RET_CHUNK = 128
ROPE_BASE = 10000.0
FOX_HEADS = 16
FOX_DH = D_MODEL // FOX_HEADS
EPS = 1e-6
LANES = 128
KV_PAD = 2 * D_MODEL + LANES

ADAM_LR = 0.001
ADAM_B1 = 0.9
ADAM_B2 = 0.999
ADAM_EPS = 1e-08
ADAM_WD = 0.01
ADAM_STEP = 10

VMEM_LIMIT_BYTES = 56 * 1024 * 1024


def _params(sem=None):
    return pltpu.CompilerParams(dimension_semantics=sem, vmem_limit_bytes=VMEM_LIMIT_BYTES)


def _me():
    return lax.axis_index("x"), lax.axis_index("y"), lax.axis_index("c")


def _peer(k):
    x, y, c = _me()
    return (1 - x if k & 4 else x, 1 - y if k & 2 else y, 1 - c if k & 1 else c)


def _peer_index(k):
    px, py, pc = _peer(k)
    return 4 * px + 2 * py + pc


def _exchange(name, xs, scatter):
    n = len(xs)

    def body(*refs):
        x_refs, o_refs = refs[:n], refs[n:2 * n]
        send_sems, recv_sems, local_sems = refs[2 * n:]
        x, y, c = _me()
        me = 4 * x + 2 * y + c
        local = []
        for i in range(n):
            src = x_refs[i].at[me] if scatter else x_refs[i]
            cp = pltpu.make_async_copy(src, o_refs[i].at[me], local_sems.at[i])
            cp.start()
            local.append(cp)
        remote = []
        for k in range(1, N_DEV):
            for i in range(n):
                src = x_refs[i].at[_peer_index(k)] if scatter else x_refs[i]
                cp = pltpu.make_async_remote_copy(
                    src_ref=src, dst_ref=o_refs[i].at[me],
                    send_sem=send_sems.at[(k - 1) * n + i], recv_sem=recv_sems.at[(k - 1) * n + i],
                    device_id=_peer(k), device_id_type=pl.DeviceIdType.MESH)
                cp.start()
                remote.append(cp)
        for cp in remote:
            cp.wait()
        for cp in local:
            cp.wait()

    out_shape = [jax.ShapeDtypeStruct(x.shape if scatter else (N_DEV,) + x.shape, x.dtype) for x in xs]
    any_spec = pl.BlockSpec(memory_space=pl.ANY)
    return pl.pallas_call(
        body, name=name, out_shape=out_shape,
        in_specs=[any_spec] * n, out_specs=[any_spec] * n,
        scratch_shapes=[pltpu.SemaphoreType.DMA(((N_DEV - 1) * n,)),
                        pltpu.SemaphoreType.DMA(((N_DEV - 1) * n,)),
                        pltpu.SemaphoreType.DMA((n,))],
    )(*xs)


def all_gather(name, xs):
    return _exchange(name, xs, scatter=False)


def all_to_all(name, xs):
    return _exchange(name, xs, scatter=True)


_HBM = pl.BlockSpec(memory_space=pltpu.HBM)
_SEM = pl.BlockSpec(memory_space=pltpu.SEMAPHORE)
_ANY = pl.BlockSpec(memory_space=pl.ANY)
_EFFECT = pltpu.SideEffectType.DATAFLOW_SIDE_EFFECTING

SIBLING = 1
CHIP_PEERS = (2, 4, 6)


def _my_index():
    x, y, c = _me()
    return 4 * x + 2 * y + c


def _slot(land, j):
    if len(land.shape) == 2:
        n = land.shape[1] // N_DEV
        return land.at[:, pl.ds(pl.multiple_of(j * n, LANES), n)]
    return land.at[j]


def plan_gather(x, land, me):
    return [(x, _slot(land, me), k) for k in (SIBLING,) + CHIP_PEERS]


def plan_gather_direct(x, land, me):
    return [(x, _slot(land, me), k) for k in range(1, N_DEV)]


def plan_scatter_direct(x, land, me):
    return [(x.at[me ^ k], land.at[me], k) for k in range(1, N_DEV)]


def plan_forward(x, land, me):
    return [(_slot(x, me ^ k), _slot(land, me ^ k), SIBLING) for k in CHIP_PEERS]


def plan_to_sibling(x, land, me):
    return [(x.at[me ^ SIBLING ^ (2 * q)], land.at[q], SIBLING) for q in range(4)]


def plan_to_owners(x, land, me):
    return [(x.at[q], land.at[q - 1], 2 * q) for q in (1, 2, 3)]


def own_gather(x, land, me):
    return x, _slot(land, me)


def own_scatter(x, land, me):
    return x.at[me], land.at[me]


OWN_COPY = {plan_gather: own_gather, plan_gather_direct: own_gather, plan_scatter_direct: own_scatter}
N_COPIES = {plan_gather: 4, plan_gather_direct: 7, plan_scatter_direct: 7, plan_forward: 3, plan_to_sibling: 4,
            plan_to_owners: 3}


def _plans(plan, n):
    return list(plan) if isinstance(plan, (list, tuple)) else [plan] * n


def _own_copies(plan, x_refs, land_refs, own_sems):
    me = _my_index()
    plans = _plans(plan, len(land_refs))
    with_own = [i for i, p in enumerate(plans) if p in OWN_COPY]
    return [pltpu.make_async_copy(*OWN_COPY[plans[i]](x_refs[i], land_refs[i], me), own_sems[s])
            for s, i in enumerate(with_own)]


def _plan_copies(plan, x_refs, land_refs, send_sems, recv_sems):
    me = _my_index()
    plans = _plans(plan, len(land_refs))
    return [pltpu.make_async_remote_copy(src_ref=src, dst_ref=dst, send_sem=send_sems[i].at[s], recv_sem=recv_sems[i].at[s],
                                         device_id=_peer(k), device_id_type=pl.DeviceIdType.MESH)
            for i in range(len(land_refs)) for s, (src, dst, k) in enumerate(plans[i](x_refs[i], land_refs[i], me))]


def exchange_now(name, xs, lands, plan):
    n = len(lands)
    n_x = 0 if xs is None else n

    def body(*refs):
        land_in, land_out = refs[n_x:n_x + n], refs[n_x + n:n_x + 2 * n]
        x_refs = land_in if xs is None else refs[:n]
        sems = refs[n_x + 2 * n:]
        copies = _own_copies(plan, x_refs, land_out, sems[2 * n:]) + _plan_copies(plan, x_refs, land_out, sems[:n], sems[n:2 * n])
        for cp in copies:
            cp.start()
        for cp in copies:
            cp.wait()

    return pl.pallas_call(
        body, name=name, out_shape=[jax.ShapeDtypeStruct(a.shape, a.dtype) for a in lands],
        in_specs=[_ANY] * (n_x + n), out_specs=[_ANY] * n,
        input_output_aliases={n_x + i: i for i in range(n)},
        scratch_shapes=[pltpu.SemaphoreType.DMA((N_COPIES[p],)) for p in _plans(plan, n)] * 2
        + [pltpu.SemaphoreType.DMA(())] * sum(p in OWN_COPY for p in _plans(plan, n)),
    )(*([] if xs is None else xs), *lands)


def exchange_start(name, xs, lands, plan, after=()):
    n, m = len(lands), len(after)
    n_x = 0 if xs is None else n
    arrays = ([] if xs is None else list(xs)) + list(lands)
    n_a = len(arrays)
    n_own = sum(p in OWN_COPY for p in _plans(plan, n))
    n_s = 2 * n + n_own

    def body(*refs):
        land_refs = refs[n_x:n_a]
        x_refs = land_refs if xs is None else refs[:n]
        sems = refs[n_a + m:n_a + m + n_s]
        token = refs[-1]
        for cp in _own_copies(plan, x_refs, land_refs, sems[2 * n:]):
            cp.start()
        for cp in _plan_copies(plan, x_refs, land_refs, sems[:n], sems[n:2 * n]):
            cp.start()
        token[...] = jnp.zeros_like(token)

    sems = [pltpu.SemaphoreType.DMA((N_COPIES[p],)) for p in _plans(plan, n)] * 2 + [pltpu.SemaphoreType.DMA(())] * n_own
    res = pl.pallas_call(
        body, name=name,
        out_shape=sems + [pltpu.HBM(a.shape, a.dtype) for a in arrays] + [jax.ShapeDtypeStruct((8, LANES), F32)],
        in_specs=[_HBM] * n_a + [_ANY] * m,
        out_specs=[_SEM] * n_s + [_HBM] * n_a + [pl.BlockSpec(memory_space=pltpu.VMEM)],
        input_output_aliases={i: n_s + i for i in range(n_a)},
        compiler_params=pltpu.CompilerParams(has_side_effects=_EFFECT),
    )(*[pltpu.with_memory_space_constraint(a, pltpu.HBM) for a in arrays], *after)
    thru = res[n_s:n_s + n_a]
    own = iter(res[2 * n:n_s])
    handles = [(res[i], res[n + i], None if xs is None else thru[i], thru[n_x + i],
                next(own) if p in OWN_COPY else None) for i, p in enumerate(_plans(plan, n))]
    return handles, res[-1]


def exchange_wait(name, handles, plan, after):
    n = len(handles)
    in_place = handles[0][2] is None
    arrays = ([] if in_place else [h[2] for h in handles]) + [h[3] for h in handles]
    n_a = len(arrays)
    own_sems = [h[4] for h in handles if h[4] is not None]
    n_s = 2 * n + len(own_sems)

    def body(*refs):
        land_refs = refs[n_a - n:n_a]
        x_refs = land_refs if in_place else refs[:n]
        sems = refs[n_a:n_a + n_s]
        for cp in _own_copies(plan, x_refs, land_refs, sems[2 * n:]):
            cp.wait()
        for cp in _plan_copies(plan, x_refs, land_refs, sems[:n], sems[n:2 * n]):
            cp.wait_send()
            cp.wait_recv()

    res = pl.pallas_call(
        body, name=name,
        out_shape=[pltpu.HBM(a.shape, a.dtype) for a in arrays],
        in_specs=[_HBM] * n_a + [_SEM] * n_s + [_ANY],
        out_specs=[_HBM] * n_a,
        input_output_aliases={i: i for i in range(n_a)},
        compiler_params=pltpu.CompilerParams(has_side_effects=_EFFECT),
    )(*arrays, *[h[0] for h in handles], *[h[1] for h in handles], *own_sems, after)
    return (None if in_place else res[:n]), res[n_a - n:]


def chip_sum(name, gw, from_sibling):
    _, R, C = gw.shape
    tr = _row_tile(R, 512)
    me = _my_index().astype(jnp.int32).reshape(1)

    def body(me_ref, g_ref, s_ref, o_ref):
        o_ref[...] = (g_ref[...].astype(F32) + s_ref[...].astype(F32)).astype(o_ref.dtype)

    slot = pl.BlockSpec((None, tr, C), lambda q, i, me_ref: (q, i, 0))
    return pl.pallas_call(
        body, name=name, out_shape=jax.ShapeDtypeStruct((4, R, C), BF16),
        grid_spec=pltpu.PrefetchScalarGridSpec(
            num_scalar_prefetch=1, grid=(4, R // tr),
            in_specs=[pl.BlockSpec((None, tr, C), lambda q, i, me_ref: (me_ref[0] ^ (2 * q), i, 0)), slot],
            out_specs=slot),
        compiler_params=_params(("arbitrary", "arbitrary")),
    )(me, gw, from_sibling)


def _tile(n, cap):
    best = None
    for t in range(LANES, min(n, cap) + 1, LANES):
        if n % t == 0:
            best = t
    if best is None or (best < 256 and n <= 2304):
        return n
    return best


def _row_tile(m, cap):
    if m <= cap:
        return m
    t = cap
    while m % t:
        t //= 2
    return t if t >= 256 else m


def _after_spec(after):
    return [] if after is None else [pl.BlockSpec(memory_space=pl.ANY)]


def _after_arg(after):
    return [] if after is None else [after]


def _mm_call(name, dims, grid, a_spec, b_spec, o_spec, o_shape, tile, a, b, out_dtypes, epilogue, extra, after, vecs=()):
    nk = grid[2]
    n_x, n_o = len(extra) + len(vecs), len(out_dtypes)

    def body(a_ref, b_ref, *refs):
        x_refs, o_refs = refs[:n_x], refs[len(refs) - n_s - n_o:len(refs) - n_s]
        part = lax.dot_general(a_ref[...].astype(BF16), b_ref[...].astype(BF16), (dims, ((), ())),
                               preferred_element_type=F32)

        def finish(acc):
            vals = (acc,) if epilogue is None else epilogue(acc, *[x[...] for x in x_refs])
            for o_ref, val in zip(o_refs, vals):
                o_ref[...] = val.astype(o_ref.dtype)

        if nk == 1:
            finish(part)
        else:
            acc_ref = refs[-1]
            k = pl.program_id(2)

            @pl.when(k == 0)
            def _():
                acc_ref[...] = part

            @pl.when(jnp.logical_and(k > 0, k < nk - 1))
            def _():
                acc_ref[...] += part

            @pl.when(k == nk - 1)
            def _():
                finish(acc_ref[...] + part)

    n_s = 0 if nk == 1 else 1
    return pl.pallas_call(
        body, name=name, out_shape=[jax.ShapeDtypeStruct(o_shape, dt) for dt in out_dtypes], grid=grid,
        in_specs=[a_spec, b_spec] + [o_spec] * len(extra) + [pl.BlockSpec(v.shape, lambda i, j, k: (0, 0)) for v in vecs]
        + _after_spec(after), out_specs=[o_spec] * n_o,
        scratch_shapes=[pltpu.VMEM(tile, F32)] * n_s,
        compiler_params=_params(("parallel", "parallel", "arbitrary")),
    )(a, b, *extra, *vecs, *_after_arg(after))


def mm_nn(name, a, w, out_dtypes=(F32,), epilogue=None, extra=(), after=None, vecs=(), tm_cap=2048):
    M, K = a.shape
    G, _, n = w.shape
    tk = _row_tile(K, 1024)
    tn = _tile(n, 512 if (K == tk and n >= 2048) else 1024)
    tm = _row_tile(M, 512 if tn > 1024 else (1024 if tk > 1024 else tm_cap))
    r = n // tn
    return _mm_call(
        name, ((1,), (0,)), (M // tm, G * r, K // tk),
        pl.BlockSpec((tm, tk), lambda i, j, k: (i, k)),
        pl.BlockSpec((None, tk, tn), lambda i, j, k: (j // r, k, j % r)),
        pl.BlockSpec((tm, tn), lambda i, j, k: (i, j)), (M, G * n), (tm, tn),
        a, w, out_dtypes, epilogue, extra, after, vecs)


def mm_nt(name, dy, w, out_dtypes=(F32,), epilogue=None, extra=(), after=None):
    M, N = dy.shape
    G, K, n = w.shape
    tn = _tile(n, 2048)
    r = n // tn
    tk = _row_tile(K, 512 if (G * r == 1 and K >= 2048 and K % 512 == 0) else 1024)
    tm = _row_tile(M, 512 if tk > 1024 else 2048)
    return _mm_call(
        name, ((1,), (1,)), (M // tm, K // tk, G * r),
        pl.BlockSpec((tm, tn), lambda i, j, k: (i, k)),
        pl.BlockSpec((None, tk, tn), lambda i, j, k: (k // r, j, k % r)),
        pl.BlockSpec((tm, tk), lambda i, j, k: (i, j)), (M, K), (tm, tk),
        dy, w, out_dtypes, epilogue, extra, after)


def mm_tn(name, a, dy, G, out_dtype=F32, after=None):
    M, K = a.shape
    n = dy.shape[1] // G
    tn = _tile(n, 1024)
    tk = _row_tile(K, 512 if tn > 1024 else 1024)
    tm = _row_tile(M, 1024 if tk > 1024 else 2048)
    r = n // tn
    return _mm_call(
        name, ((0,), (0,)), (K // tk, G * r, M // tm),
        pl.BlockSpec((tm, tk), lambda i, j, k: (k, i)),
        pl.BlockSpec((tm, tn), lambda i, j, k: (k, j)),
        pl.BlockSpec((None, tk, tn), lambda i, j, k: (j // r, i, j % r)), (G, K, n), (tk, tn),
        a, dy, (out_dtype,), None, (), after)[0]


ROW_BLOCK_BYTES = 12 * 1024 * 1024


def rowwise(name, fn, rows, vecs, outs, accs=(), after=None):
    rows = [r if isinstance(r, tuple) else (r, r.shape[1], 0) for r in rows]
    rows = [r if len(r) == 4 else r + (None,) for r in rows]
    n_fn = len(rows) + len(vecs)
    vecs = list(vecs) + _after_arg(after)
    S = rows[0][0].shape[-2]
    row_bytes = sum(w * a.dtype.itemsize for a, w, _, _ in rows) + sum(w * jnp.dtype(dt).itemsize for w, dt in outs)
    tm = _row_tile(S, 1024)
    while tm > 256 and tm * row_bytes > ROW_BLOCK_BYTES:
        tm //= 2
    n_r, n_v, n_o, n_a = len(rows), len(vecs), len(outs), len(accs)

    def body(*refs):
        ins = [ref[...] for ref in refs[:n_r + n_v]]
        o_refs = refs[n_r + n_v:n_r + n_v + n_o]
        a_refs = refs[n_r + n_v + n_o:]
        res = fn(*ins[:n_fn])
        res = res if isinstance(res, (tuple, list)) else (res,)
        for ref, val in zip(o_refs, res[:n_o]):
            ref[...] = val.astype(ref.dtype)
        if n_a:
            @pl.when(pl.program_id(0) == 0)
            def _():
                for ref in a_refs:
                    ref[...] = jnp.zeros_like(ref)
            for ref, val in zip(a_refs, res[n_o:]):
                ref[...] += val

    in_specs = [pl.BlockSpec((tm, w), functools.partial(lambda cb, i: (i, cb), cb)) if slab is None else
                pl.BlockSpec((None, tm, w), functools.partial(lambda cb, slab, i: (slab, i, cb), cb, slab))
                for _, w, cb, slab in rows]
    in_specs += [pl.BlockSpec(v.shape, lambda i: (0, 0)) for v in vecs]
    out_specs = [pl.BlockSpec((tm, w), lambda i: (i, 0)) for w, _ in outs]
    out_specs += [pl.BlockSpec(a, lambda i: (0, 0)) for a in accs]
    out_shape = [jax.ShapeDtypeStruct((S, w), dt) for w, dt in outs]
    out_shape += [jax.ShapeDtypeStruct(a, F32) for a in accs]
    res = pl.pallas_call(
        body, name=name, out_shape=out_shape, grid=(S // tm,),
        in_specs=in_specs, out_specs=out_specs,
        compiler_params=_params(("arbitrary",)),
    )(*[r[0] for r in rows], *vecs)
    return res


def _rms(x):
    return x * lax.rsqrt(jnp.mean(x * x, axis=-1, keepdims=True) + EPS)


def _modulate(x, gain, scale, shift):
    return _rms(x) * gain * (1.0 + scale) + shift


def _ada_slices(ada_raw, bias):
    ada = ada_raw + bias
    return [ada[:, i * D_MODEL:(i + 1) * D_MODEL] for i in range(ada.shape[1] // D_MODEL)]


def _norm_wide_heads(y, gain, heads):
    w = y.shape[1] // heads
    return jnp.concatenate([_rms(y[:, h * w:(h + 1) * w]) * gain[:, h * w:(h + 1) * w] for h in range(heads)], axis=1)


def _norm_fox_heads(x, gain):
    outs = []
    for p in range(x.shape[1] // LANES):
        blk = x[:, p * LANES:(p + 1) * LANES]
        low = lax.broadcasted_iota(jnp.int32, blk.shape, 1) < FOX_DH
        sq = blk * blk
        ss_low = jnp.sum(jnp.where(low, sq, 0.0), axis=1, keepdims=True)
        ss_high = jnp.sum(jnp.where(low, 0.0, sq), axis=1, keepdims=True)
        outs.append(blk * lax.rsqrt(jnp.where(low, ss_low, ss_high) * (1.0 / FOX_DH) + EPS))
    return jnp.concatenate(outs, axis=1) * gain


def _silu(x):
    return x * jax.nn.sigmoid(x)


def _log_sigmoid(z):
    return -(jnp.maximum(-z, 0.0) + jnp.log(1.0 + jnp.exp(-jnp.abs(z))))


def _rotate(x, cos, sin, heads, sign):
    w = x.shape[1] // heads
    half = w // 2
    outs = []
    for h in range(heads):
        x1 = x[:, h * w:h * w + half]
        x2 = x[:, h * w + half:(h + 1) * w]
        outs += [x1 * cos - sign * x2 * sin, sign * x1 * sin + x2 * cos]
    return jnp.concatenate(outs, axis=1)


def _vjp(fn, primals, ct):
    return jax.vjp(fn, *primals)[1](ct)


_LOG_GAMMAS = [float(np.log(np.float32(1.0) - np.float32(2.0) ** np.float32(-5.0 - h))) for h in range(RET_HEADS)]


RET_ROWS = 1024


def retention(name, jobs, reverse, out_dtypes):
    operands = []
    for job in jobs:
        for op in job:
            if not any(op[0] is o[0] and op[1:] == o[1:] for o in operands):
                operands.append(op)
    index = [[next(i for i, o in enumerate(operands) if op[0] is o[0] and op[1:] == o[1:]) for op in job] for job in jobs]
    S = operands[0][0].shape[0]
    C = RET_CHUNK
    rows = _row_tile(S, RET_ROWS)
    nb = S // rows
    n_in, n_job = len(operands), len(jobs)

    def body(*refs):
        in_refs, o_refs, states = refs[:n_in], refs[n_in:n_in + n_job], refs[n_in + n_job:]
        h = pl.program_id(0)

        @pl.when(pl.program_id(1) == 0)
        def _():
            for state in states:
                state[...] = jnp.zeros_like(state)

        log_g = jnp.float32(_LOG_GAMMAS[RET_HEADS - 1])
        for i in range(RET_HEADS - 2, -1, -1):
            log_g = jnp.where(h == i, jnp.float32(_LOG_GAMMAS[i]), log_g)
        row = lax.broadcasted_iota(jnp.int32, (C, C), 0)
        col = lax.broadcasted_iota(jnp.int32, (C, C), 1)
        rel = (col - row if reverse else row - col).astype(F32)
        decay = jnp.where(rel >= 0, jnp.exp(log_g * jnp.maximum(rel, 0.0)), 0.0)
        j = lax.broadcasted_iota(jnp.int32, (C, 1), 0).astype(F32)
        q_decay = jnp.exp(log_g * (C - j if reverse else j + 1.0))
        k_decay = jnp.exp(log_g * (j if reverse else C - 1.0 - j))
        chunk_decay = jnp.exp(jnp.full((1, 1), log_g * C, F32))

        chunks = range(rows // C)
        for ci in (reversed(chunks) if reverse else chunks):
            rs = slice(ci * C, (ci + 1) * C)
            for (qi, ki, vi), o_ref, state in zip(index, o_refs, states):
                qc = in_refs[qi][rs, :].astype(BF16)
                kf = in_refs[ki][rs, :].astype(F32)
                vc = in_refs[vi][rs, :].astype(BF16)
                scores = lax.dot_general(qc, kf.astype(BF16), (((1,), (1,)), ((), ())), preferred_element_type=F32) * decay
                intra = jnp.dot(scores.astype(BF16), vc, preferred_element_type=F32)
                cross = jnp.dot(qc, state[...].astype(BF16), preferred_element_type=F32) * q_decay
                o_ref[rs, :] = (intra + cross).astype(o_ref.dtype)
                upd = lax.dot_general((kf * k_decay).astype(BF16), vc, (((0,), (0,)), ((), ())), preferred_element_type=F32)
                state[...] = state[...] * chunk_decay + upd

    def block(i):
        return nb - 1 - i if reverse else i

    widths = [(q[1], v[1]) for q, _, v in jobs]
    return pl.pallas_call(
        body, name=name,
        out_shape=[jax.ShapeDtypeStruct((S, RET_HEADS * dv), dt) for (_, dv), dt in zip(widths, out_dtypes)],
        grid=(RET_HEADS, nb),
        in_specs=[pl.BlockSpec((rows, w), functools.partial(lambda off, h, i: (block(i), off + h), off))
                  for _, w, off in operands],
        out_specs=[pl.BlockSpec((rows, dv), lambda h, i: (block(i), h)) for _, dv in widths],
        scratch_shapes=[pltpu.VMEM((dk, dv), F32) for dk, dv in widths],
        compiler_params=_params(("parallel", "arbitrary")),
    )(*[o[0] for o in operands])


FOX_T = 256
N_PAIR = FOX_HEADS // 2
FOX_SCALE = FOX_DH ** -0.5


def _fox_heads(q2):
    low = lax.broadcasted_iota(jnp.int32, (1, LANES), 1) < FOX_DH
    return [(mask, jnp.where(mask, q2 * FOX_SCALE, 0.0).astype(BF16)) for mask in (low, jnp.logical_not(low))]


def _fox_parts(j, t):
    return ([(0, j * t, False)] if j else []) + [(j * t, (j + 1) * t, True)]


def _fox_scores(qa, k_ref, ft_ref, head, lo, hi, diagonal):
    k_blk = k_ref[lo:hi, :].astype(BF16)
    s = lax.dot_general(qa, k_blk, (((1,), (1,)), ((), ())), preferred_element_type=F32) - ft_ref[pl.ds(head, 1), lo:hi]
    if diagonal:
        n = hi - lo
        s = jnp.where(lax.broadcasted_iota(jnp.int32, (n, n), 1) <= lax.broadcasted_iota(jnp.int32, (n, n), 0), s, -jnp.inf)
    return s


def fox_forward(name, qn, kn, kvf, f_cum_t):
    S = qn.shape[0]
    t = _row_tile(S, FOX_T)
    v_block0 = D_MODEL // LANES

    def variant(j, pair, q_ref, k_ref, v_ref, ft_ref, y_ref, lse_ref):
        ys, lses = [], []
        for a, (mask, qa) in enumerate(_fox_heads(q_ref[...])):
            parts = [(lo, hi, _fox_scores(qa, k_ref, ft_ref, 2 * pair + a, lo, hi, dg)) for lo, hi, dg in _fox_parts(j, t)]
            m = functools.reduce(jnp.maximum, [jnp.max(s, axis=1, keepdims=True) for _, _, s in parts])
            l, acc = 0.0, 0.0
            for lo, hi, s in parts:
                e = jnp.exp(s - m)
                l = l + jnp.sum(e, axis=1, keepdims=True)
                acc = acc + jnp.dot(e.astype(BF16), v_ref[lo:hi, :].astype(BF16), preferred_element_type=F32)
            ys.append(acc / l)
            lses.append(m + jnp.log(l))
        low = lax.broadcasted_iota(jnp.int32, (1, LANES), 1) < FOX_DH
        y_ref[...] = jnp.where(low, ys[0], ys[1])
        lse_ref[...] = jnp.where(low, lses[0], lses[1])

    def body(*refs):
        pair, i = pl.program_id(0), pl.program_id(1)
        for j in range(S // t):
            pl.when(i == j)(functools.partial(variant, j, pair, *refs))

    return pl.pallas_call(
        body, name=name,
        out_shape=[jax.ShapeDtypeStruct((S, D_MODEL), F32), jax.ShapeDtypeStruct((S, D_MODEL), F32)],
        grid=(N_PAIR, S // t),
        in_specs=[pl.BlockSpec((t, LANES), lambda p, i: (i, p)),
                  pl.BlockSpec((S, LANES), lambda p, i: (0, p)),
                  pl.BlockSpec((S, LANES), lambda p, i: (0, v_block0 + p)),
                  pl.BlockSpec((LANES, S), lambda p, i: (0, 0))],
        out_specs=[pl.BlockSpec((t, LANES), lambda p, i: (i, p)),
                   pl.BlockSpec((t, LANES), lambda p, i: (i, p))],
        compiler_params=_params(("parallel", "arbitrary")),
    )(qn, kn, kvf, f_cum_t)


def fox_backward(name, qn, kn, kvf, f_cum_t, y, dy, lse):
    S = qn.shape[0]
    t = _row_tile(S, FOX_T)
    v_block0 = D_MODEL // LANES

    def variant(j, pair, q_ref, k_ref, v_ref, ft_ref, y_ref, dy_ref, lse_ref, dq_ref, dk_ref, dv_ref, dfq_ref, dfk_ref,
                dv_acc):
        y2, dy2, lse2 = y_ref[...], dy_ref[...], lse_ref[...]
        lane = lax.broadcasted_iota(jnp.int32, (t, LANES), 1)
        dqs, dfq = [], jnp.zeros((t, LANES), F32)
        for a, (mask, qa) in enumerate(_fox_heads(q_ref[...].astype(F32))):
            lse_a = jnp.max(jnp.where(mask, lse2, -jnp.inf), axis=1, keepdims=True)
            dy_a = jnp.where(mask, dy2, 0.0)
            delta = jnp.sum(dy_a * y2, axis=1, keepdims=True)
            dy_b = dy_a.astype(BF16)
            dq, row_sum = 0.0, 0.0
            for lo, hi, dg in _fox_parts(j, t):
                p = jnp.exp(_fox_scores(qa, k_ref, ft_ref, 2 * pair + a, lo, hi, dg) - lse_a)
                dp = lax.dot_general(dy_b, v_ref[lo:hi, :].astype(BF16), (((1,), (1,)), ((), ())), preferred_element_type=F32)
                ds = p * (dp - delta)
                row_sum = row_sum + jnp.sum(ds, axis=1, keepdims=True)
                dfk_ref[pl.ds(a, 1), lo:hi] += -jnp.sum(ds, axis=0, keepdims=True)
                ds_b = ds.astype(BF16)
                dq = dq + jnp.dot(ds_b, k_ref[lo:hi, :].astype(BF16), preferred_element_type=F32)
                dk_ref[lo:hi, :] += lax.dot_general(ds_b, qa, (((0,), (0,)), ((), ())), preferred_element_type=F32)
                dv_acc[lo:hi, :] += lax.dot_general(p.astype(BF16), dy_b, (((0,), (0,)), ((), ())), preferred_element_type=F32)
            dqs.append(dq * FOX_SCALE)
            dfq = dfq + jnp.where(lane == 2 * pair + a, row_sum, 0.0)
        low = lax.broadcasted_iota(jnp.int32, (1, LANES), 1) < FOX_DH
        dq_ref[...] = jnp.where(low, dqs[0], dqs[1])
        dfq_ref[...] = dfq

    def body(*refs):
        pair, i = pl.program_id(0), pl.program_id(1)
        dk_ref, dv_ref, _, dfk_ref, dv_acc = refs[8:13]

        @pl.when(i == 0)
        def _():
            dk_ref[...] = jnp.zeros_like(dk_ref)
            dv_acc[...] = jnp.zeros_like(dv_acc)
            dfk_ref[...] = jnp.zeros_like(dfk_ref)

        for j in range(S // t):
            pl.when(i == j)(functools.partial(variant, j, pair, *refs))

        @pl.when(i == S // t - 1)
        def _():
            dv_ref[...] = dv_acc[...].astype(dv_ref.dtype)

    row_blk = pl.BlockSpec((t, LANES), lambda p, i: (i, p))
    col_blk = pl.BlockSpec((S, LANES), lambda p, i: (0, p))
    return pl.pallas_call(
        body, name=name,
        out_shape=[jax.ShapeDtypeStruct((S, D_MODEL), F32)] * 2
        + [jax.ShapeDtypeStruct((S, D_MODEL), BF16), jax.ShapeDtypeStruct((N_PAIR, S, LANES), F32),
           jax.ShapeDtypeStruct((N_PAIR, 8, S), F32)],
        grid=(N_PAIR, S // t),
        in_specs=[row_blk, col_blk,
                  pl.BlockSpec((S, LANES), lambda p, i: (0, v_block0 + p)),
                  pl.BlockSpec((LANES, S), lambda p, i: (0, 0)),
                  row_blk, row_blk, row_blk],
        out_specs=[row_blk, col_blk, col_blk,
                   pl.BlockSpec((None, t, LANES), lambda p, i: (p, i, 0)),
                   pl.BlockSpec((None, 8, S), lambda p, i: (p, 0, 0))],
        scratch_shapes=[pltpu.VMEM((S, LANES), F32)],
        compiler_params=_params(("parallel", "arbitrary")),
    )(qn, kn, kvf, f_cum_t, y, dy, lse)


def cumsum_rows(name, x, reverse):
    S = x.shape[0]
    C = LANES
    nc = S // C

    def body(x_ref, o_ref):
        row = lax.broadcasted_iota(jnp.int32, (C, C), 0)
        col = lax.broadcasted_iota(jnp.int32, (C, C), 1)
        tri = jnp.where(col >= row if reverse else col <= row, 1.0, 0.0).astype(F32)
        carry = jnp.zeros((1, LANES), F32)
        for i in (range(nc - 1, -1, -1) if reverse else range(nc)):
            blk = x_ref[i * C:(i + 1) * C, :]
            loc = jnp.dot(tri, blk, preferred_element_type=F32, precision=lax.Precision.HIGHEST)
            o_ref[i * C:(i + 1) * C, :] = loc + carry
            carry = carry + (loc[0:1, :] if reverse else loc[C - 1:C, :])

    return pl.pallas_call(body, name=name, out_shape=jax.ShapeDtypeStruct((S, LANES), F32),
                          compiler_params=_params())(x)


def adamw(name, parts, w, m, v):
    L, R, C = w.shape
    row_bytes = C * (7 * 4 + max(sum(n * a.dtype.itemsize for a, n in p) for p in parts))
    tr = _row_tile(R, 1024)
    while tr > 256 and tr % 2 == 0 and R % (tr // 2) == 0 and tr * row_bytes > ROW_BLOCK_BYTES:
        tr //= 2
    nr = R // tr
    counts = [len(p) for p in parts]

    def body(*refs):
        w_ref, m_ref, v_ref, g_out, d_out, m_out, v_out = refs[sum(counts):]
        for layer in range(L):
            p_refs = refs[sum(counts[:layer]):sum(counts[:layer + 1])]

            @pl.when(pl.program_id(0) == layer)
            def _(p_refs=p_refs, slots=[n for _, n in parts[layer]]):
                g = None
                for p_ref, n in zip(p_refs, slots):
                    for i in range(n):
                        g = p_ref[i].astype(F32) if g is None else g + p_ref[i].astype(F32)
                m2 = ADAM_B1 * m_ref[...] + (1.0 - ADAM_B1) * g
                v2 = ADAM_B2 * v_ref[...] + (1.0 - ADAM_B2) * jnp.square(g)
                m_hat = m2 / (1.0 - ADAM_B1 ** ADAM_STEP)
                v_hat = v2 / (1.0 - ADAM_B2 ** ADAM_STEP)
                g_out[...] = g
                d_out[...] = -ADAM_LR * (m_hat / (jnp.sqrt(v_hat) + ADAM_EPS) + ADAM_WD * w_ref[...])
                m_out[...] = m2
                v_out[...] = v2

    def part_spec(layer, n):
        return pl.BlockSpec((n, tr, C), lambda l, i: (0, jnp.where(l == layer, i, jnp.where(l < layer, 0, nr - 1)), 0))

    blk = pl.BlockSpec((None, tr, C), lambda l, i: (l, i, 0))
    return pl.pallas_call(
        body, name=name, out_shape=[jax.ShapeDtypeStruct((L, R, C), F32)] * 4, grid=(L, nr),
        in_specs=[part_spec(layer, n) for layer in range(L) for _, n in parts[layer]] + [blk, blk, blk],
        out_specs=[blk] * 4, compiler_params=_params(("arbitrary", "arbitrary")),
    )(*[a for layer in parts for a, _ in layer], w, m, v)


def kernel(x, c, positions, norm_mix_gain, norm_mlp_gain, w_ada, b_ada, w_mlp_in, w_mlp_out, ret_w_in, ret_norm_gain, ret_w_out, kv_norm_gain, kv_w_ada, kv_b_ada, kv_w, forget_bias, k_norm_gain, fox_w_in, q_norm_gain, fox_w_out, loss_target, m_norm_mix_gain, m_norm_mlp_gain, m_w_ada, m_b_ada, m_w_mlp_in, m_w_mlp_out, m_ret_w_in, m_ret_norm_gain, m_ret_w_out, m_kv_norm_gain, m_kv_w_ada, m_kv_b_ada, m_kv_w, m_forget_bias, m_k_norm_gain, m_fox_w_in, m_q_norm_gain, m_fox_w_out, v_norm_mix_gain, v_norm_mlp_gain, v_w_ada, v_b_ada, v_w_mlp_in, v_w_mlp_out, v_ret_w_in, v_ret_norm_gain, v_ret_w_out, v_kv_norm_gain, v_kv_w_ada, v_kv_b_ada, v_kv_w, v_forget_bias, v_k_norm_gain, v_fox_w_in, v_q_norm_gain, v_fox_w_out):
    D = D_MODEL
    S = x.shape[1]
    x0 = x.reshape(S, D)
    target = loss_target.reshape(S, D)
    me = 4 * lax.axis_index("x") + 2 * lax.axis_index("y") + lax.axis_index("c")
    n_ada = w_ada.shape[2]
    n_kvada = kv_w_ada.shape[1]
    n_kv = kv_w.shape[1]

    c_all, ret_gain = all_gather("gather_c", [c.reshape(D // LANES, LANES), ret_norm_gain.reshape(RET_HEADS, -1)])
    ret_gain = jnp.transpose(ret_gain, (1, 0, 2)).reshape(1, RET_HEADS * RET_V)
    c_act = rowwise("silu_c", _silu, [c_all.reshape(N_DEV, D)], [], [(D, F32)])[0]
    w_ada_cat = jnp.concatenate([w_ada[0], w_ada[1], kv_w_ada], axis=1).astype(BF16)[None]
    n_cat = 2 * n_ada + n_kvada
    ada_part = mm_nn("ada_proj", c_act, w_ada_cat)[0]
    ada_mine = all_to_all("ada_rows", [ada_part.reshape(N_DEV, n_cat // LANES, LANES)])[0]
    ada_mine = ada_mine.reshape(N_DEV, n_cat)
    ada_raw = [ada_mine[:, l * n_ada:(l + 1) * n_ada].reshape(1, 6 * D) for l in range(2)]
    kvada_raw = ada_mine[:, 2 * n_ada:].reshape(1, 2 * D)
    kv_bias = kv_b_ada.reshape(1, 2 * D)
    kv_gain = kv_norm_gain.reshape(1, D)
    fb = jnp.pad(forget_bias.reshape(1, FOX_HEADS), ((0, 0), (0, LANES - FOX_HEADS)))
    k_gain = jnp.tile(k_norm_gain.reshape(1, FOX_DH), (1, FOX_HEADS))
    q_gain = jnp.tile(q_norm_gain.reshape(1, FOX_DH), (1, FOX_HEADS))

    w_names = ["ret_in", "ret_out", "mlp_in0", "mlp_out0", "kv", "fox_in", "fox_out", "mlp_in1", "mlp_out1"]
    shards = [ret_w_in[0].astype(BF16), ret_w_out[0].astype(BF16), w_mlp_in[0].astype(BF16), w_mlp_out[0].astype(BF16),
              kv_w.T.astype(BF16), fox_w_in[0].astype(BF16), fox_w_out[0].astype(BF16), w_mlp_in[1].astype(BF16),
              w_mlp_out[1].astype(BF16)]
    two_level = {"ret_in", "ret_out", "mlp_in0", "mlp_out0"}
    by_columns = {"ret_in", "mlp_in0", "mlp_in1", "fox_in"}
    w_plans = {name: plan_gather if name in two_level else plan_gather_direct for name in w_names}
    w_handles, token = exchange_start("gather_weights_start", shards,
                                      [lax.empty((a.shape[0], N_DEV * a.shape[1]) if name in by_columns else
                                                 (N_DEV,) + a.shape, a.dtype) for name, a in zip(w_names, shards)],
                                      [w_plans[name] for name in w_names], after=(ada_mine, ret_gain))
    w_handles = dict(zip(w_names, w_handles))

    forwards = {}

    def forward_early(name, after):
        arrived = exchange_wait("gather_wait_" + name, [w_handles[name]], w_plans[name], after)[1]
        forwards[name], tok = exchange_start("gather_forward_start_" + name, None, arrived, plan_forward)
        return tok

    def weight(name, after):
        if name in two_level:
            return exchange_wait("gather_forward_wait_" + name, forwards[name], plan_forward, after)[1][0]
        return exchange_wait("gather_wait_" + name, [w_handles[name]], w_plans[name], after)[1][0]

    pos = positions.reshape(S, 1).astype(F32)
    half = RET_QK // 2
    inv_freq = jnp.asarray((ROPE_BASE ** (-np.arange(half, dtype=np.float32) / half)).reshape(1, half), F32)

    def angles(p, f):
        ang = p * f
        return jnp.cos(ang), jnp.sin(ang)

    cos, sin = rowwise("rope_table", angles, [pos], [inv_freq], [(half, F32), (half, F32)])

    def mod_mix(layer):
        def fn(xb, ada, bias, gain):
            sh, sc = _ada_slices(ada, bias)[:2]
            return _modulate(xb, gain[layer:layer + 1], sc, sh)
        return fn

    def mod_mlp(layer):
        def fn(xb, ada, bias, gain):
            sh, sc = _ada_slices(ada, bias)[3:5]
            return _modulate(xb, gain[layer:layer + 1], sc, sh)
        return fn

    h1_0 = rowwise("mod_mix0", mod_mix(0), [x0], [ada_raw[0], b_ada[0:1], norm_mix_gain], [(D, BF16)], after=token)[0]
    W_ret_in = weight("ret_in", forward_early("ret_in", h1_0))[None]
    proj = mm_nn("ret_proj", h1_0, W_ret_in, (BF16,))[0]
    token = forward_early("ret_out", proj)

    def rope_fwd(qb, kb, cs, sn):
        return (_rotate(qb.astype(F32), cs, sn, RET_HEADS, 1.0),
                _rotate(kb.astype(F32), cs, sn, RET_HEADS, 1.0) * (RET_QK ** -0.5))

    q_rot, k_rot = rowwise("rope", rope_fwd, [(proj, D, 0), (proj, D, 1), cos, sin], [], [(D, BF16), (D, BF16)],
                           after=token)
    v_ret = (proj, RET_V, (2 * D) // RET_V)
    y_ret = retention("ret_fwd", [((q_rot, RET_QK, 0), (k_rot, RET_QK, 0), v_ret)], False, [F32])[0]

    def ret_gate(yb, gb, gain):
        return _silu(gb.astype(F32)) * _norm_wide_heads(yb, gain, RET_HEADS)

    mixin0 = rowwise("ret_gate", ret_gate, [y_ret, (proj, 2 * D, 2)], [ret_gain], [(2 * D, BF16)])[0]
    W_ret_out = weight("ret_out", mixin0).reshape(1, 2 * D, D)
    FUSED_ROWS = 512

    def residual_mod(layer, slot):
        def fn(xb, bb, ada, bias, gain):
            s = _ada_slices(ada, bias)
            xn = xb + s[2] * bb
            return xn, _modulate(xn, gain[layer:layer + 1], s[4], s[3])
        return fn

    def with_residual(fn):
        return lambda acc, xb, *vs: (acc,) + tuple(fn(xb, acc, *vs))

    mix0, x1, h2_0 = mm_nn("ret_out", mixin0, W_ret_out, (F32, F32, BF16), with_residual(residual_mod(0, 0)), (x0,),
                           forward_early("mlp_in0", W_ret_out), (ada_raw[0], b_ada[0:1], norm_mlp_gain), FUSED_ROWS)

    W_mlp_in, W_mlp_out = {}, {}

    def mlp_forward(tag, h2, layer, out_dtypes=(F32,), epilogue=None, extra=(), vecs=()):
        W_mlp_in[layer] = weight("mlp_in" + tag, h2)[None]
        early = forward_early("mlp_out" + tag, W_mlp_in[layer]) if "mlp_out" + tag in two_level else None
        u, act = mm_nn("mlp_in" + tag, h2, W_mlp_in[layer], (BF16, BF16), after=early,
                       epilogue=lambda acc: (acc, jnp.square(jnp.maximum(acc, 0.0))))
        W_mlp_out[layer] = weight("mlp_out" + tag, act).reshape(1, 4 * D, D)
        return u, act, mm_nn("mlp_out" + tag, act, W_mlp_out[layer], out_dtypes, epilogue, extra, None, vecs,
                             FUSED_ROWS if epilogue else 2048)

    def res_mlp0(xb, bb, ada0, bias0, ada1, bias1, kva, kvb, gain_mix, gain_kv):
        xn = xb + _ada_slices(ada0, bias0)[5] * bb
        s1 = _ada_slices(ada1, bias1)
        kv_shift, kv_scale = _ada_slices(kva, kvb)
        return xn, _modulate(xn, gain_kv, kv_scale, kv_shift), _modulate(xn, gain_mix[1:2], s1[1], s1[0])

    u0, act0, (mlp0,) = mlp_forward("0", h2_0, 0)
    x2, h_kv, h1_1 = rowwise("res_mlp0", res_mlp0, [x1, mlp0],
                             [ada_raw[0], b_ada[0:1], ada_raw[1], b_ada[1:2], kvada_raw, kv_bias, norm_mix_gain, kv_gain],
                             [(D, F32), (D, BF16), (D, BF16)])

    W_kv = jnp.pad(weight("kv", h_kv).reshape(N_DEV * n_kv, D), ((0, KV_PAD - N_DEV * n_kv), (0, 0)))[None]
    kvf = mm_nt("kv_proj", h_kv, W_kv)[0]

    def kv_post(kb, fblk, kg, bias):
        head = lax.broadcasted_iota(jnp.int32, fblk.shape, 1) < FOX_HEADS
        return _norm_fox_heads(kb, kg), jnp.where(head, _log_sigmoid(fblk + bias), 0.0)

    kn, log_f = rowwise("kv_post", kv_post, [(kvf, D, 0), (kvf, LANES, 2 * D // LANES)], [k_gain, fb],
                        [(D, BF16), (LANES, F32)])
    f_cum = cumsum_rows("f_cumsum", log_f, reverse=False)
    f_cum_t = f_cum.T

    W_fox_in = weight("fox_in", kvf)[None]
    qo = mm_nn("fox_proj", h1_1, W_fox_in)[0]
    qn = rowwise("q_norm", _norm_fox_heads, [(qo, D, 0)], [q_gain], [(D, BF16)])[0]
    y_att, lse = fox_forward("fox_fwd", qn, kn, kvf, f_cum_t)
    mixin1 = rowwise("fox_gate", lambda ob, yb: jax.nn.sigmoid(ob) * yb, [(qo, D, 1), y_att], [], [(D, BF16)])[0]
    W_fox_out = weight("fox_out", mixin1).reshape(1, D, D)
    mix1, x3, h2_1 = mm_nn("fox_out", mixin1, W_fox_out, (F32, F32, BF16), with_residual(residual_mod(1, 0)), (x2,),
                           None, (ada_raw[1], b_ada[1:2], norm_mlp_gain), FUSED_ROWS)
    u1, act1, (mlp1,) = mlp_forward("1", h2_1, 1)

    def scatter_start(tag, gws, after=()):
        lands = [lax.empty((4,) + g.shape[1:], g.dtype) for g in gws]
        return exchange_start("scatter_sibling_start_" + tag, gws, lands, plan_to_sibling, after)

    def scatter_relay(tag, handles, after, start_after=()):
        gws, from_sibling = exchange_wait("scatter_sibling_wait_" + tag, handles, plan_to_sibling, after)
        sums = [chip_sum("chip_sum_%s%d" % (tag, i), g, s) for i, (g, s) in enumerate(zip(gws, from_sibling))]
        lands = [lax.empty((3,) + s.shape[1:], s.dtype) for s in sums]
        return exchange_start("scatter_owner_start_" + tag, sums, lands, plan_to_owners, start_after)

    def scatter_direct_start(tag, gws):
        lands = [lax.empty(g.shape, g.dtype) for g in gws]
        return exchange_start("scatter_direct_start_" + tag, gws, lands, plan_scatter_direct)

    def scatter_direct_finish(tag, handles, after):
        return [[(r, N_DEV)] for r in exchange_wait("scatter_direct_wait_" + tag, handles, plan_scatter_direct, after)[1]]

    def scatter_finish(tag, handles, after):
        sums, received = exchange_wait("scatter_owner_wait_" + tag, handles, plan_to_owners, after)
        return [[(s, 1), (r, 3)] for s, r in zip(sums, received)]

    def loss_head(xb, bb, tb, ada, bias):
        g2 = _ada_slices(ada, bias)[5]
        err = xb + g2 * bb - tb
        dx = err * (1.0 / D)
        loss = 0.5 * jnp.sum(jnp.sum(err * err, axis=1, keepdims=True) * (1.0 / D), axis=0, keepdims=True)
        return dx, (dx * g2), jnp.broadcast_to(loss, (1, LANES)), jnp.sum(dx * bb, axis=0, keepdims=True)

    dx4, dmlp1, loss_acc, dg2_1 = rowwise("loss_head", loss_head, [x3, mlp1, target], [ada_raw[1], b_ada[1:2]],
                                          [(D, F32), (D, BF16)], [(1, LANES), (1, D)])

    def mlp_backward(tag, dmlp, act, u, h2, layer, after=None):
        du = mm_nt("mlp_out_dx" + tag, dmlp, W_mlp_out[layer], (BF16,), extra=(u,), after=after,
                   epilogue=lambda acc, ub: (acc * (2.0 * jnp.maximum(ub.astype(F32), 0.0)),))[0]
        gw_out = mm_tn("mlp_out_dw" + tag, act, dmlp, 1, BF16).reshape(N_DEV, -1, D)
        dh2 = mm_nt("mlp_in_dx" + tag, du, W_mlp_in[layer], (BF16,))[0]
        gw_in = mm_tn("mlp_in_dw" + tag, h2, du, N_DEV, BF16)
        return dh2, gw_in, gw_out

    def mod_backward(layer, slots, gate_slot):
        def fn(xb, dhb, dresb, branchb, ada, bias, gain):
            s = _ada_slices(ada, bias)
            g = gain[layer:layer + 1]
            dx, dgain, dsc, dsh = _vjp(_modulate, (xb, g, s[slots[1]], s[slots[0]]), dhb.astype(F32))
            dx = dx + dresb
            d_branch = dx * s[gate_slot]
            return dx, d_branch, dgain, dsc, dsh, jnp.sum(dx * branchb, axis=0, keepdims=True)
        return fn

    vec = (1, D)
    dh2_1, gw_mlp_in1, gw_mlp_out1 = mlp_backward("1", dmlp1, act1, u1, h2_1, 1)
    scat_a1, token_a1 = scatter_direct_start("a1", [gw_mlp_in1, gw_mlp_out1])
    dx3, dmix1, dgain_mlp1, dsc2_1, dsh2_1, dg1_1 = rowwise(
        "mod_mlp1_bwd", mod_backward(1, (3, 4), 2), [x3, dh2_1, dx4, mix1], [ada_raw[1], b_ada[1:2], norm_mlp_gain],
        [(D, F32), (D, BF16)], [vec] * 4, after=token_a1)
    dmixin1 = mm_nt("fox_out_dx", dmix1, W_fox_out)[0]
    gw_fox_out = mm_tn("fox_out_dw", mixin1, dmix1, 1, BF16).reshape(N_DEV, -1, D)

    def fox_gate_bwd(db, ob, yb):
        sg = jax.nn.sigmoid(ob)
        return db * sg, db * yb * sg * (1.0 - sg)

    dy_att, d_og = rowwise("fox_gate_bwd", fox_gate_bwd, [dmixin1, (qo, D, 1), y_att], [], [(D, F32), (D, BF16)])
    dqn, dkn, dv_att, dfq, dfk = fox_backward("fox_bwd", qn, kn, kvf, f_cum_t, y_att, dy_att, lse)

    def q_norm_bwd(qb, db, ogb, gain):
        dq, dgain = _vjp(_norm_fox_heads, (qb, gain), db)
        return jnp.concatenate([dq, ogb.astype(F32)], axis=1), dgain

    dqo, dq_gain = rowwise("q_norm_bwd", q_norm_bwd, [(qo, D, 0), dqn, d_og], [q_gain], [(2 * D, BF16)], [vec])
    dh1_1 = mm_nt("fox_proj_dx", dqo, W_fox_in, (BF16,))[0]
    gw_fox_in = mm_tn("fox_proj_dw", h1_1, dqo, N_DEV, BF16)

    dfk_rows = jnp.pad(dfk[:, :2, :].reshape(FOX_HEADS, S).T, ((0, 0), (0, LANES - FOX_HEADS)))

    def df_total(*blks):
        tot = blks[0]
        for b in blks[1:]:
            tot = tot + b
        return tot

    d_fcum = rowwise("df_sum", df_total, [dfk_rows] + [(dfq, LANES, 0, p) for p in range(N_PAIR)], [], [(LANES, F32)])[0]
    d_logf = cumsum_rows("df_cumsum", d_fcum, reverse=True)

    def kv_post_bwd(kb, fblk, dkb, dvb, dlf, kg, bias):
        dk, dgain = _vjp(_norm_fox_heads, (kb, kg), dkb)
        df = dlf * (1.0 / (1.0 + jnp.exp(fblk + bias)))
        return jnp.concatenate([dk, dvb.astype(F32), df], axis=1), dgain, jnp.sum(df, axis=0, keepdims=True)

    dkvf, dk_gain, dfb = rowwise("kv_post_bwd", kv_post_bwd,
                                 [(kvf, D, 0), (kvf, LANES, 2 * D // LANES), dkn, dv_att, d_logf], [k_gain, fb],
                                 [(KV_PAD, BF16)], [vec, (1, LANES)])
    dh_kv = mm_nn("kv_proj_dx", dkvf, W_kv, (BF16,))[0]
    gw_kv = mm_tn("kv_proj_dw", dkvf, h_kv, 1, BF16)[0, :N_DEV * n_kv].reshape(N_DEV, n_kv, D)
    scat_a2, token_a = scatter_direct_start("a2", [gw_fox_out, gw_fox_in, gw_kv])

    def x2_bwd(xb, dh1b, dhkb, dresb, branchb, ada0, bias0, ada1, bias1, kva, kvb, gain_mix, gain_kv):
        s1 = _ada_slices(ada1, bias1)
        kv_shift, kv_scale = _ada_slices(kva, kvb)
        dxa, dgain_mix, dsc1, dsh1 = _vjp(_modulate, (xb, gain_mix[1:2], s1[1], s1[0]), dh1b.astype(F32))
        dxb, dgain_kv, dkv_scale, dkv_shift = _vjp(_modulate, (xb, gain_kv, kv_scale, kv_shift), dhkb.astype(F32))
        dx = dresb + dxa + dxb
        g2 = _ada_slices(ada0, bias0)[5]
        return (dx, dx * g2, dgain_mix, dsc1, dsh1, dgain_kv, dkv_scale, dkv_shift,
                jnp.sum(dx * branchb, axis=0, keepdims=True))

    (dx2, dmlp0, dgain_mix1, dsc1_1, dsh1_1, dgain_kv, dkv_scale, dkv_shift, dg2_0) = rowwise(
        "x2_bwd", x2_bwd, [x2, dh1_1, dh_kv, dx3, mlp0],
        [ada_raw[0], b_ada[0:1], ada_raw[1], b_ada[1:2], kvada_raw, kv_bias, norm_mix_gain, kv_gain],
        [(D, F32), (D, BF16)], [vec] * 7, after=token_a)

    dh2_0, gw_mlp_in0, gw_mlp_out0 = mlp_backward("0", dmlp0, act0, u0, h2_0, 0)
    scat_b1, token_b1 = scatter_direct_start("b1", [gw_mlp_in0, gw_mlp_out0])
    dx1, dmix0, dgain_mlp0, dsc2_0, dsh2_0, dg1_0 = rowwise(
        "mod_mlp0_bwd", mod_backward(0, (3, 4), 2), [x1, dh2_0, dx2, mix0], [ada_raw[0], b_ada[0:1], norm_mlp_gain],
        [(D, F32), (D, BF16)], [vec] * 4, after=token_b1)
    dmixin0 = mm_nt("ret_out_dx", dmix0, W_ret_out, (BF16,))[0]
    gw_ret_out = mm_tn("ret_out_dw", mixin0, dmix0, 1, BF16).reshape(N_DEV, -1, D)
    scat_b2, token_b = scatter_direct_start("b2", [gw_ret_out])

    def ret_gate_bwd(db, yb, gb, gain):
        return _vjp(ret_gate, (yb, gb.astype(F32), gain), db.astype(F32))

    dy_ret, dgate, dret_gain = rowwise("ret_gate_bwd", ret_gate_bwd, [dmixin0, y_ret, (proj, 2 * D, 2)], [ret_gain],
                                       [(2 * D, BF16), (2 * D, BF16)], [(1, 2 * D)], after=token_b)
    dy_h = (dy_ret, RET_V, 0)
    q_h, k_h = (q_rot, RET_QK, 0), (k_rot, RET_QK, 0)
    dq_rot = retention("ret_dq", [(dy_h, v_ret, k_h)], False, [F32])[0]
    dk_rot, dv_ret = retention("ret_dk_dv", [(v_ret, dy_h, q_h), (k_h, q_h, dy_h)], True, [F32, BF16])

    def rope_bwd(dqb, dkb, dvb, dgb, cs, sn):
        dq = _rotate(dqb, cs, sn, RET_HEADS, -1.0)
        dk = _rotate(dkb, cs, sn, RET_HEADS, -1.0) * (RET_QK ** -0.5)
        return jnp.concatenate([dq, dk, dvb.astype(F32), dgb.astype(F32)], axis=1)

    dproj = rowwise("rope_bwd", rope_bwd, [dq_rot, dk_rot, dv_ret, dgate, cos, sin], [], [(6 * D, BF16)])[0]
    gw_ret_in = mm_tn("ret_proj_dw", h1_0, dproj, N_DEV, BF16)
    scat_c, token_c = scatter_start("c", [gw_ret_in])
    dh1_0 = mm_nt("ret_proj_dx", dproj, W_ret_in, (BF16,), after=token_c)[0]

    def x0_bwd(xb, dhb, dresb, ada, bias, gain):
        s = _ada_slices(ada, bias)
        dx, dgain, dsc, dsh = _vjp(_modulate, (xb, gain[0:1], s[1], s[0]), dhb.astype(F32))
        return dx + dresb, dgain, dsc, dsh

    grad_x, dgain_mix0, dsc1_0, dsh1_0 = rowwise("x0_bwd", x0_bwd, [x0, dh1_0, dx1],
                                                 [ada_raw[0], b_ada[0:1], norm_mix_gain], [(D, F32)], [vec] * 3)

    small = jnp.concatenate([
        dsh1_0, dsc1_0, dg1_0, dsh2_0, dsc2_0, dg2_0,
        dsh1_1, dsc1_1, dg1_1, dsh2_1, dsc2_1, dg2_1,
        dkv_shift, dkv_scale,
        dgain_mix0, dgain_mix1, dgain_mlp0, dgain_mlp1, dgain_kv,
        dret_gain,
        dq_gain.reshape(FOX_HEADS, FOX_DH).sum(axis=0).reshape(1, FOX_DH),
        dk_gain.reshape(FOX_HEADS, FOX_DH).sum(axis=0).reshape(1, FOX_DH),
        dfb,
        loss_acc,
    ], axis=1)
    small_all = all_gather("gather_small", [small.reshape(-1, LANES)])[0].reshape(N_DEV, 1, -1)
    loss = jnp.sum(small_all[:, 0, -1])
    scat_c, token_c = scatter_relay("c", scat_c, grad_x, start_after=(small_all,))
    o_ada = 14 * D
    d_ada = small_all[:, 0, :o_ada]
    d_cat = jnp.concatenate([
        lax.dynamic_slice_in_dim(d_ada[:, 0:6 * D], me * n_ada, n_ada, axis=1),
        lax.dynamic_slice_in_dim(d_ada[:, 6 * D:12 * D], me * n_ada, n_ada, axis=1),
        lax.dynamic_slice_in_dim(d_ada[:, 12 * D:14 * D], me * n_kvada, n_kvada, axis=1)], axis=1)
    gw_ada_cat = mm_tn("ada_dw", c_act, d_cat, 1, F32, after=token_c)[0]

    results = {}

    def update(name, parts, w, m, v, layers=1):
        per_layer = parts if layers > 1 else [parts]
        shape = w.shape
        C = shape[-1]
        R = int(np.prod(shape)) // (layers * C)
        per_layer = [p if isinstance(p, list) else [(p, p.shape[0])] for p in per_layer]
        per_layer = [[(a.reshape(a.shape[0], R, C), n) for a, n in p] for p in per_layer]
        outs = adamw("adamw_" + name, per_layer, w.reshape(layers, R, C), m.reshape(layers, R, C), v.reshape(layers, R, C))
        results[name] = tuple(t.reshape(shape) for t in outs)

    def small_parts(lo, width):
        return small_all[:, :, lo:lo + width]

    update("norm_mix_gain", jnp.concatenate([small_parts(o_ada, D), small_parts(o_ada + D, D)], axis=1),
           norm_mix_gain, m_norm_mix_gain, v_norm_mix_gain)
    update("norm_mlp_gain", jnp.concatenate([small_parts(o_ada + 2 * D, D), small_parts(o_ada + 3 * D, D)], axis=1),
           norm_mlp_gain, m_norm_mlp_gain, v_norm_mlp_gain)
    update("w_ada", [gw_ada_cat[None, :, :n_ada], gw_ada_cat[None, :, n_ada:2 * n_ada]], w_ada, m_w_ada, v_w_ada, layers=2)
    update("b_ada", jnp.concatenate([small_parts(0, 6 * D), small_parts(6 * D, 6 * D)], axis=1), b_ada, m_b_ada, v_b_ada)
    o_ret = o_ada + 5 * D
    n_rg = ret_norm_gain.shape[2]
    ret_gain_parts = small_parts(o_ret, 2 * D).reshape(N_DEV, RET_HEADS, RET_V)
    ret_gain_parts = lax.dynamic_slice_in_dim(ret_gain_parts, me * n_rg, n_rg, axis=2)
    update("ret_norm_gain", ret_gain_parts, ret_norm_gain, m_ret_norm_gain, v_ret_norm_gain)
    update("kv_norm_gain", small_parts(o_ada + 4 * D, D), kv_norm_gain, m_kv_norm_gain, v_kv_norm_gain)
    update("kv_w_ada", gw_ada_cat[None, :, 2 * n_ada:], kv_w_ada, m_kv_w_ada, v_kv_w_ada)
    update("kv_b_ada", small_parts(12 * D, 2 * D), kv_b_ada, m_kv_b_ada, v_kv_b_ada)
    o_q = o_ret + 2 * D
    update("forget_bias", small_parts(o_q + 2 * FOX_DH, FOX_HEADS), forget_bias, m_forget_bias, v_forget_bias)
    update("k_norm_gain", small_parts(o_q + FOX_DH, FOX_DH), k_norm_gain, m_k_norm_gain, v_k_norm_gain)
    update("q_norm_gain", small_parts(o_q, FOX_DH), q_norm_gain, m_q_norm_gain, v_q_norm_gain)

    r_mlp_in1, r_mlp_out1 = scatter_direct_finish("a1", scat_a1, results["q_norm_gain"][1])
    r_fox_out, r_fox_in, r_kv = scatter_direct_finish("a2", scat_a2, r_mlp_in1[0][0])
    r_mlp_in0, r_mlp_out0 = scatter_direct_finish("b1", scat_b1, r_kv[0][0])
    r_ret_out = scatter_direct_finish("b2", scat_b2, r_mlp_in0[0][0])[0]
    update("kv_w", r_kv, kv_w.T, m_kv_w.T, v_kv_w.T)
    results["kv_w"] = tuple(t.T for t in results["kv_w"])
    update("fox_w_in", r_fox_in, fox_w_in, m_fox_w_in, v_fox_w_in)
    update("fox_w_out", r_fox_out, fox_w_out, m_fox_w_out, v_fox_w_out)
    update("ret_w_out", r_ret_out, ret_w_out, m_ret_w_out, v_ret_w_out)
    update("w_mlp_in", [r_mlp_in0, r_mlp_in1], w_mlp_in, m_w_mlp_in, v_w_mlp_in, layers=2)
    update("w_mlp_out", [r_mlp_out0, r_mlp_out1], w_mlp_out, m_w_mlp_out, v_w_mlp_out, layers=2)
    r_ret_in = scatter_finish("c", scat_c, results["w_mlp_out"][1])[0]
    update("ret_w_in", r_ret_in, ret_w_in, m_ret_w_in, v_ret_w_in)

    order = ["norm_mix_gain", "norm_mlp_gain", "w_ada", "b_ada", "w_mlp_in", "w_mlp_out", "ret_w_in", "ret_norm_gain",
             "ret_w_out", "kv_norm_gain", "kv_w_ada", "kv_b_ada", "kv_w", "forget_bias", "k_norm_gain", "fox_w_in",
             "q_norm_gain", "fox_w_out"]
    out = [loss, grad_x.reshape(x.shape)]
    for slot in range(4):
        out += [results[n][slot] for n in order]
    return tuple(out)
```

```python
import functools
import math

import numpy as np
import jax
import jax.numpy as jnp
from jax import lax
from jax.experimental import pallas as pl
from jax.experimental.pallas import tpu as pltpu

F32 = jnp.float32
BF16 = jnp.bfloat16

N_DEV = 8
D_MODEL = 1024
RET_HEADS = 4
RET_QK = D_MODEL // RET_HEADS
RET_V = 2 * D_MODEL // RET_HEADS
RET_CHUNK = 128
ROPE_BASE = 10000.0
FOX_HEADS = 16
FOX_DH = D_MODEL // FOX_HEADS
EPS = 1e-6
LANES = 128
KV_PAD = 2 * D_MODEL + LANES

ADAM_LR = 0.001
ADAM_B1 = 0.9
ADAM_B2 = 0.999
ADAM_EPS = 1e-08
ADAM_WD = 0.01
ADAM_STEP = 10

VMEM_LIMIT_BYTES = 56 * 1024 * 1024


def _params(sem=None):
    return pltpu.CompilerParams(dimension_semantics=sem, vmem_limit_bytes=VMEM_LIMIT_BYTES)


def _me():
    return lax.axis_index("x"), lax.axis_index("y"), lax.axis_index("c")


def _peer(k):
    x, y, c = _me()
    return (1 - x if k & 4 else x, 1 - y if k & 2 else y, 1 - c if k & 1 else c)


def _peer_index(k):
    px, py, pc = _peer(k)
    return 4 * px + 2 * py + pc


def _exchange(name, xs, scatter):
    n = len(xs)

    def body(*refs):
        x_refs, o_refs = refs[:n], refs[n:2 * n]
        send_sems, recv_sems, local_sems = refs[2 * n:]
        x, y, c = _me()
        me = 4 * x + 2 * y + c
        local = []
        for i in range(n):
            src = x_refs[i].at[me] if scatter else x_refs[i]
            cp = pltpu.make_async_copy(src, o_refs[i].at[me], local_sems.at[i])
            cp.start()
            local.append(cp)
        remote = []
        for k in range(1, N_DEV):
            for i in range(n):
                src = x_refs[i].at[_peer_index(k)] if scatter else x_refs[i]
                cp = pltpu.make_async_remote_copy(
                    src_ref=src, dst_ref=o_refs[i].at[me],
                    send_sem=send_sems.at[(k - 1) * n + i], recv_sem=recv_sems.at[(k - 1) * n + i],
                    device_id=_peer(k), device_id_type=pl.DeviceIdType.MESH)
                cp.start()
                remote.append(cp)
        for cp in remote:
            cp.wait()
        for cp in local:
            cp.wait()

    out_shape = [jax.ShapeDtypeStruct(x.shape if scatter else (N_DEV,) + x.shape, x.dtype) for x in xs]
    any_spec = pl.BlockSpec(memory_space=pl.ANY)
    return pl.pallas_call(
        body, name=name, out_shape=out_shape,
        in_specs=[any_spec] * n, out_specs=[any_spec] * n,
        scratch_shapes=[pltpu.SemaphoreType.DMA(((N_DEV - 1) * n,)),
                        pltpu.SemaphoreType.DMA(((N_DEV - 1) * n,)),
                        pltpu.SemaphoreType.DMA((n,))],
    )(*xs)


def all_gather(name, xs):
    return _exchange(name, xs, scatter=False)


def all_to_all(name, xs):
    return _exchange(name, xs, scatter=True)


_HBM = pl.BlockSpec(memory_space=pltpu.HBM)
_SEM = pl.BlockSpec(memory_space=pltpu.SEMAPHORE)
_ANY = pl.BlockSpec(memory_space=pl.ANY)
_EFFECT = pltpu.SideEffectType.DATAFLOW_SIDE_EFFECTING

SIBLING = 1
CHIP_PEERS = (2, 4, 6)


def _my_index():
    x, y, c = _me()
    return 4 * x + 2 * y + c


def _slot(land, j):
    if len(land.shape) == 2:
        n = land.shape[1] // N_DEV
        return land.at[:, pl.ds(pl.multiple_of(j * n, LANES), n)]
    return land.at[j]


def plan_gather(x, land, me):
    return [(x, _slot(land, me), k) for k in (SIBLING,) + CHIP_PEERS]


def plan_gather_direct(x, land, me):
    return [(x, _slot(land, me), k) for k in range(1, N_DEV)]


def plan_scatter_direct(x, land, me):
    return [(x.at[me ^ k], land.at[me], k) for k in range(1, N_DEV)]


def plan_forward(x, land, me):
    return [(_slot(x, me ^ k), _slot(land, me ^ k), SIBLING) for k in CHIP_PEERS]


def plan_to_sibling(x, land, me):
    return [(x.at[me ^ SIBLING ^ (2 * q)], land.at[q], SIBLING) for q in range(4)]


def plan_to_owners(x, land, me):
    return [(x.at[q], land.at[q - 1], 2 * q) for q in (1, 2, 3)]


def own_gather(x, land, me):
    return x, _slot(land, me)


def own_scatter(x, land, me):
    return x.at[me], land.at[me]


OWN_COPY = {plan_gather: own_gather, plan_gather_direct: own_gather, plan_scatter_direct: own_scatter}
N_COPIES = {plan_gather: 4, plan_gather_direct: 7, plan_scatter_direct: 7, plan_forward: 3, plan_to_sibling: 4,
            plan_to_owners: 3}


def _plans(plan, n):
    return list(plan) if isinstance(plan, (list, tuple)) else [plan] * n


def _own_copies(plan, x_refs, land_refs, own_sems):
    me = _my_index()
    plans = _plans(plan, len(land_refs))
    with_own = [i for i, p in enumerate(plans) if p in OWN_COPY]
    return [pltpu.make_async_copy(*OWN_COPY[plans[i]](x_refs[i], land_refs[i], me), own_sems[s])
            for s, i in enumerate(with_own)]


def _plan_copies(plan, x_refs, land_refs, send_sems, recv_sems):
    me = _my_index()
    plans = _plans(plan, len(land_refs))
    return [pltpu.make_async_remote_copy(src_ref=src, dst_ref=dst, send_sem=send_sems[i].at[s], recv_sem=recv_sems[i].at[s],
                                         device_id=_peer(k), device_id_type=pl.DeviceIdType.MESH)
            for i in range(len(land_refs)) for s, (src, dst, k) in enumerate(plans[i](x_refs[i], land_refs[i], me))]


def exchange_now(name, xs, lands, plan):
    n = len(lands)
    n_x = 0 if xs is None else n

    def body(*refs):
        land_in, land_out = refs[n_x:n_x + n], refs[n_x + n:n_x + 2 * n]
        x_refs = land_in if xs is None else refs[:n]
        sems = refs[n_x + 2 * n:]
        copies = _own_copies(plan, x_refs, land_out, sems[2 * n:]) + _plan_copies(plan, x_refs, land_out, sems[:n], sems[n:2 * n])
        for cp in copies:
            cp.start()
        for cp in copies:
            cp.wait()

    return pl.pallas_call(
        body, name=name, out_shape=[jax.ShapeDtypeStruct(a.shape, a.dtype) for a in lands],
        in_specs=[_ANY] * (n_x + n), out_specs=[_ANY] * n,
        input_output_aliases={n_x + i: i for i in range(n)},
        scratch_shapes=[pltpu.SemaphoreType.DMA((N_COPIES[p],)) for p in _plans(plan, n)] * 2
        + [pltpu.SemaphoreType.DMA(())] * sum(p in OWN_COPY for p in _plans(plan, n)),
    )(*([] if xs is None else xs), *lands)


def exchange_start(name, xs, lands, plan, after=()):
    n, m = len(lands), len(after)
    n_x = 0 if xs is None else n
    arrays = ([] if xs is None else list(xs)) + list(lands)
    n_a = len(arrays)
    n_own = sum(p in OWN_COPY for p in _plans(plan, n))
    n_s = 2 * n + n_own

    def body(*refs):
        land_refs = refs[n_x:n_a]
        x_refs = land_refs if xs is None else refs[:n]
        sems = refs[n_a + m:n_a + m + n_s]
        token = refs[-1]
        for cp in _own_copies(plan, x_refs, land_refs, sems[2 * n:]):
            cp.start()
        for cp in _plan_copies(plan, x_refs, land_refs, sems[:n], sems[n:2 * n]):
            cp.start()
        token[...] = jnp.zeros_like(token)

    sems = [pltpu.SemaphoreType.DMA((N_COPIES[p],)) for p in _plans(plan, n)] * 2 + [pltpu.SemaphoreType.DMA(())] * n_own
    res = pl.pallas_call(
        body, name=name,
        out_shape=sems + [pltpu.HBM(a.shape, a.dtype) for a in arrays] + [jax.ShapeDtypeStruct((8, LANES), F32)],
        in_specs=[_HBM] * n_a + [_ANY] * m,
        out_specs=[_SEM] * n_s + [_HBM] * n_a + [pl.BlockSpec(memory_space=pltpu.VMEM)],
        input_output_aliases={i: n_s + i for i in range(n_a)},
        compiler_params=pltpu.CompilerParams(has_side_effects=_EFFECT),
    )(*[pltpu.with_memory_space_constraint(a, pltpu.HBM) for a in arrays], *after)
    thru = res[n_s:n_s + n_a]
    own = iter(res[2 * n:n_s])
    handles = [(res[i], res[n + i], None if xs is None else thru[i], thru[n_x + i],
                next(own) if p in OWN_COPY else None) for i, p in enumerate(_plans(plan, n))]
    return handles, res[-1]


def exchange_wait(name, handles, plan, after):
    n = len(handles)
    in_place = handles[0][2] is None
    arrays = ([] if in_place else [h[2] for h in handles]) + [h[3] for h in handles]
    n_a = len(arrays)
    own_sems = [h[4] for h in handles if h[4] is not None]
    n_s = 2 * n + len(own_sems)

    def body(*refs):
        land_refs = refs[n_a - n:n_a]
        x_refs = land_refs if in_place else refs[:n]
        sems = refs[n_a:n_a + n_s]
        for cp in _own_copies(plan, x_refs, land_refs, sems[2 * n:]):
            cp.wait()
        for cp in _plan_copies(plan, x_refs, land_refs, sems[:n], sems[n:2 * n]):
            cp.wait_send()
            cp.wait_recv()

    res = pl.pallas_call(
        body, name=name,
        out_shape=[pltpu.HBM(a.shape, a.dtype) for a in arrays],
        in_specs=[_HBM] * n_a + [_SEM] * n_s + [_ANY],
        out_specs=[_HBM] * n_a,
        input_output_aliases={i: i for i in range(n_a)},
        compiler_params=pltpu.CompilerParams(has_side_effects=_EFFECT),
    )(*arrays, *[h[0] for h in handles], *[h[1] for h in handles], *own_sems, after)
    return (None if in_place else res[:n]), res[n_a - n:]


def chip_sum(name, gw, from_sibling):
    _, R, C = gw.shape
    tr = _row_tile(R, 512)
    me = _my_index().astype(jnp.int32).reshape(1)

    def body(me_ref, g_ref, s_ref, o_ref):
        o_ref[...] = (g_ref[...].astype(F32) + s_ref[...].astype(F32)).astype(o_ref.dtype)

    slot = pl.BlockSpec((None, tr, C), lambda q, i, me_ref: (q, i, 0))
    return pl.pallas_call(
        body, name=name, out_shape=jax.ShapeDtypeStruct((4, R, C), BF16),
        grid_spec=pltpu.PrefetchScalarGridSpec(
            num_scalar_prefetch=1, grid=(4, R // tr),
            in_specs=[pl.BlockSpec((None, tr, C), lambda q, i, me_ref: (me_ref[0] ^ (2 * q), i, 0)), slot],
            out_specs=slot),
        compiler_params=_params(("arbitrary", "arbitrary")),
    )(me, gw, from_sibling)


def _tile(n, cap):
    best = None
    for t in range(LANES, min(n, cap) + 1, LANES):
        if n % t == 0:
            best = t
    if best is None or (best < 256 and n <= 2304):
        return n
    return best


def _row_tile(m, cap):
    if m <= cap:
        return m
    t = cap
    while m % t:
        t //= 2
    return t if t >= 256 else m


def _after_spec(after):
    return [] if after is None else [pl.BlockSpec(memory_space=pl.ANY)]


def _after_arg(after):
    return [] if after is None else [after]


def _mm_call(name, dims, grid, a_spec, b_spec, o_spec, o_shape, tile, a, b, out_dtypes, epilogue, extra, after, vecs=()):
    nk = grid[2]
    n_x, n_o = len(extra) + len(vecs), len(out_dtypes)

    def body(a_ref, b_ref, *refs):
        x_refs, o_refs = refs[:n_x], refs[len(refs) - n_s - n_o:len(refs) - n_s]
        part = lax.dot_general(a_ref[...].astype(BF16), b_ref[...].astype(BF16), (dims, ((), ())),
                               preferred_element_type=F32)

        def finish(acc):
            vals = (acc,) if epilogue is None else epilogue(acc, *[x[...] for x in x_refs])
            for o_ref, val in zip(o_refs, vals):
                o_ref[...] = val.astype(o_ref.dtype)

        if nk == 1:
            finish(part)
        else:
            acc_ref = refs[-1]
            k = pl.program_id(2)

            @pl.when(k == 0)
            def _():
                acc_ref[...] = part

            @pl.when(jnp.logical_and(k > 0, k < nk - 1))
            def _():
                acc_ref[...] += part

            @pl.when(k == nk - 1)
            def _():
                finish(acc_ref[...] + part)

    n_s = 0 if nk == 1 else 1
    return pl.pallas_call(
        body, name=name, out_shape=[jax.ShapeDtypeStruct(o_shape, dt) for dt in out_dtypes], grid=grid,
        in_specs=[a_spec, b_spec] + [o_spec] * len(extra) + [pl.BlockSpec(v.shape, lambda i, j, k: (0, 0)) for v in vecs]
        + _after_spec(after), out_specs=[o_spec] * n_o,
        scratch_shapes=[pltpu.VMEM(tile, F32)] * n_s,
        compiler_params=_params(("parallel", "parallel", "arbitrary")),
    )(a, b, *extra, *vecs, *_after_arg(after))


def mm_nn(name, a, w, out_dtypes=(F32,), epilogue=None, extra=(), after=None, vecs=(), tm_cap=2048):
    M, K = a.shape
    G, _, n = w.shape
    tk = _row_tile(K, 2048)
    tn = _tile(n, 512 if (K == tk and n >= 2048) else 1024)
    tm = _row_tile(M, 512 if tn > 1024 else (min(1024, tm_cap) if tk > 1024 else tm_cap))
    r = n // tn
    return _mm_call(
        name, ((1,), (0,)), (M // tm, G * r, K // tk),
        pl.BlockSpec((tm, tk), lambda i, j, k: (i, k)),
        pl.BlockSpec((None, tk, tn), lambda i, j, k: (j // r, k, j % r)),
        pl.BlockSpec((tm, tn), lambda i, j, k: (i, j)), (M, G * n), (tm, tn),
        a, w, out_dtypes, epilogue, extra, after, vecs)


def mm_nt(name, dy, w, out_dtypes=(F32,), epilogue=None, extra=(), after=None):
    M, N = dy.shape
    G, K, n = w.shape
    tn = _tile(n, 3072)
    r = n // tn
    tk = _row_tile(K, 512 if (G * r == 1 and K >= 2048 and K % 512 == 0) else 1024)
    tm = _row_tile(M, 512 if tk > 1024 else (1024 if tn > 2048 else 2048))
    return _mm_call(
        name, ((1,), (1,)), (M // tm, K // tk, G * r),
        pl.BlockSpec((tm, tn), lambda i, j, k: (i, k)),
        pl.BlockSpec((None, tk, tn), lambda i, j, k: (k // r, j, k % r)),
        pl.BlockSpec((tm, tk), lambda i, j, k: (i, j)), (M, K), (tm, tk),
        dy, w, out_dtypes, epilogue, extra, after)


def mm_tn(name, a, dy, G, out_dtype=F32, after=None):
    M, K = a.shape
    n = dy.shape[1] // G
    tn = _tile(n, 1024)
    tk = _row_tile(K, 512 if tn > 1024 else 1024)
    tm = _row_tile(M, 1024 if tk > 1024 else 2048)
    r = n // tn
    return _mm_call(
        name, ((0,), (0,)), (K // tk, G * r, M // tm),
        pl.BlockSpec((tm, tk), lambda i, j, k: (k, i)),
        pl.BlockSpec((tm, tn), lambda i, j, k: (k, j)),
        pl.BlockSpec((None, tk, tn), lambda i, j, k: (j // r, i, j % r)), (G, K, n), (tk, tn),
        a, dy, (out_dtype,), None, (), after)[0]


ROW_BLOCK_BYTES = 12 * 1024 * 1024


def rowwise(name, fn, rows, vecs, outs, accs=(), after=None):
    rows = [r if isinstance(r, tuple) else (r, r.shape[1], 0) for r in rows]
    rows = [r if len(r) == 4 else r + (None,) for r in rows]
    n_fn = len(rows) + len(vecs)
    vecs = list(vecs) + _after_arg(after)
    S = rows[0][0].shape[-2]
    row_bytes = sum(w * a.dtype.itemsize for a, w, _, _ in rows) + sum(w * jnp.dtype(dt).itemsize for w, dt in outs)
    tm = _row_tile(S, 1024)
    while tm > 256 and tm * row_bytes > ROW_BLOCK_BYTES:
        tm //= 2
    n_r, n_v, n_o, n_a = len(rows), len(vecs), len(outs), len(accs)

    def body(*refs):
        ins = [ref[...] for ref in refs[:n_r + n_v]]
        o_refs = refs[n_r + n_v:n_r + n_v + n_o]
        a_refs = refs[n_r + n_v + n_o:]
        res = fn(*ins[:n_fn])
        res = res if isinstance(res, (tuple, list)) else (res,)
        for ref, val in zip(o_refs, res[:n_o]):
            ref[...] = val.astype(ref.dtype)
        if n_a:
            @pl.when(pl.program_id(0) == 0)
            def _():
                for ref in a_refs:
                    ref[...] = jnp.zeros_like(ref)
            for ref, val in zip(a_refs, res[n_o:]):
                ref[...] += val

    in_specs = [pl.BlockSpec((tm, w), functools.partial(lambda cb, i: (i, cb), cb)) if slab is None else
                pl.BlockSpec((None, tm, w), functools.partial(lambda cb, slab, i: (slab, i, cb), cb, slab))
                for _, w, cb, slab in rows]
    in_specs += [pl.BlockSpec(v.shape, lambda i: (0, 0)) for v in vecs]
    out_specs = [pl.BlockSpec((tm, w), lambda i: (i, 0)) for w, _ in outs]
    out_specs += [pl.BlockSpec(a, lambda i: (0, 0)) for a in accs]
    out_shape = [jax.ShapeDtypeStruct((S, w), dt) for w, dt in outs]
    out_shape += [jax.ShapeDtypeStruct(a, F32) for a in accs]
    res = pl.pallas_call(
        body, name=name, out_shape=out_shape, grid=(S // tm,),
        in_specs=in_specs, out_specs=out_specs,
        compiler_params=_params(("arbitrary",)),
    )(*[r[0] for r in rows], *vecs)
    return res


def _rms(x):
    return x * lax.rsqrt(jnp.mean(x * x, axis=-1, keepdims=True) + EPS)


def _modulate(x, gain, scale, shift):
    return _rms(x) * gain * (1.0 + scale) + shift


def _ada_slices(ada_raw, bias):
    ada = ada_raw + bias
    return [ada[:, i * D_MODEL:(i + 1) * D_MODEL] for i in range(ada.shape[1] // D_MODEL)]


def _norm_wide_heads(y, gain, heads):
    w = y.shape[1] // heads
    return jnp.concatenate([_rms(y[:, h * w:(h + 1) * w]) * gain[:, h * w:(h + 1) * w] for h in range(heads)], axis=1)


def _norm_fox_heads(x, gain):
    outs = []
    for p in range(x.shape[1] // LANES):
        blk = x[:, p * LANES:(p + 1) * LANES]
        low = lax.broadcasted_iota(jnp.int32, blk.shape, 1) < FOX_DH
        sq = blk * blk
        ss_low = jnp.sum(jnp.where(low, sq, 0.0), axis=1, keepdims=True)
        ss_high = jnp.sum(jnp.where(low, 0.0, sq), axis=1, keepdims=True)
        outs.append(blk * lax.rsqrt(jnp.where(low, ss_low, ss_high) * (1.0 / FOX_DH) + EPS))
    return jnp.concatenate(outs, axis=1) * gain


def _silu(x):
    return x * jax.nn.sigmoid(x)


def _log_sigmoid(z):
    return -(jnp.maximum(-z, 0.0) + jnp.log(1.0 + jnp.exp(-jnp.abs(z))))


def _rotate(x, cos, sin, heads, sign):
    w = x.shape[1] // heads
    half = w // 2
    outs = []
    for h in range(heads):
        x1 = x[:, h * w:h * w + half]
        x2 = x[:, h * w + half:(h + 1) * w]
        outs += [x1 * cos - sign * x2 * sin, sign * x1 * sin + x2 * cos]
    return jnp.concatenate(outs, axis=1)


def _vjp(fn, primals, ct):
    return jax.vjp(fn, *primals)[1](ct)


_LOG_GAMMAS = [float(np.log(np.float32(1.0) - np.float32(2.0) ** np.float32(-5.0 - h))) for h in range(RET_HEADS)]


RET_ROWS = 1024


def retention(name, jobs, reverse, out_dtypes):
    operands = []
    for job in jobs:
        for op in job:
            if not any(op[0] is o[0] and op[1:] == o[1:] for o in operands):
                operands.append(op)
    index = [[next(i for i, o in enumerate(operands) if op[0] is o[0] and op[1:] == o[1:]) for op in job] for job in jobs]
    S = operands[0][0].shape[0]
    C = RET_CHUNK
    rows = _row_tile(S, RET_ROWS)
    nb = S // rows
    n_in, n_job = len(operands), len(jobs)

    def body(*refs):
        in_refs, o_refs, states = refs[:n_in], refs[n_in:n_in + n_job], refs[n_in + n_job:]
        h = pl.program_id(0)

        @pl.when(pl.program_id(1) == 0)
        def _():
            for state in states:
                state[...] = jnp.zeros_like(state)

        log_g = jnp.float32(_LOG_GAMMAS[RET_HEADS - 1])
        for i in range(RET_HEADS - 2, -1, -1):
            log_g = jnp.where(h == i, jnp.float32(_LOG_GAMMAS[i]), log_g)
        row = lax.broadcasted_iota(jnp.int32, (C, C), 0)
        col = lax.broadcasted_iota(jnp.int32, (C, C), 1)
        rel = (col - row if reverse else row - col).astype(F32)
        decay = jnp.where(rel >= 0, jnp.exp(log_g * jnp.maximum(rel, 0.0)), 0.0)
        j = lax.broadcasted_iota(jnp.int32, (C, 1), 0).astype(F32)
        q_decay = jnp.exp(log_g * (C - j if reverse else j + 1.0))
        k_decay = jnp.exp(log_g * (j if reverse else C - 1.0 - j))
        chunk_decay = jnp.exp(jnp.full((1, 1), log_g * C, F32))

        chunks = range(rows // C)
        for ci in (reversed(chunks) if reverse else chunks):
            rs = slice(ci * C, (ci + 1) * C)
            for (qi, ki, vi), o_ref, state in zip(index, o_refs, states):
                qc = in_refs[qi][rs, :].astype(BF16)
                kf = in_refs[ki][rs, :].astype(F32)
                vc = in_refs[vi][rs, :].astype(BF16)
                scores = lax.dot_general(qc, kf.astype(BF16), (((1,), (1,)), ((), ())), preferred_element_type=F32) * decay
                intra = jnp.dot(scores.astype(BF16), vc, preferred_element_type=F32)
                cross = jnp.dot(qc, state[...].astype(BF16), preferred_element_type=F32) * q_decay
                o_ref[rs, :] = (intra + cross).astype(o_ref.dtype)
                upd = lax.dot_general((kf * k_decay).astype(BF16), vc, (((0,), (0,)), ((), ())), preferred_element_type=F32)
                state[...] = state[...] * chunk_decay + upd

    def block(i):
        return nb - 1 - i if reverse else i

    widths = [(q[1], v[1]) for q, _, v in jobs]
    return pl.pallas_call(
        body, name=name,
        out_shape=[jax.ShapeDtypeStruct((S, RET_HEADS * dv), dt) for (_, dv), dt in zip(widths, out_dtypes)],
        grid=(RET_HEADS, nb),
        in_specs=[pl.BlockSpec((rows, w), functools.partial(lambda off, h, i: (block(i), off + h), off))
                  for _, w, off in operands],
        out_specs=[pl.BlockSpec((rows, dv), lambda h, i: (block(i), h)) for _, dv in widths],
        scratch_shapes=[pltpu.VMEM((dk, dv), F32) for dk, dv in widths],
        compiler_params=_params(("parallel", "arbitrary")),
    )(*[o[0] for o in operands])


FOX_T = 256
N_PAIR = FOX_HEADS // 2
FOX_SCALE = FOX_DH ** -0.5


def _fox_heads(q2):
    low = lax.broadcasted_iota(jnp.int32, (1, LANES), 1) < FOX_DH
    return [(mask, jnp.where(mask, q2 * FOX_SCALE, 0.0).astype(BF16)) for mask in (low, jnp.logical_not(low))]


def _fox_parts(j, t):
    return ([(0, j * t, False)] if j else []) + [(j * t, (j + 1) * t, True)]


def _fox_scores(qa, k_ref, ft_ref, head, lo, hi, diagonal):
    k_blk = k_ref[lo:hi, :].astype(BF16)
    s = lax.dot_general(qa, k_blk, (((1,), (1,)), ((), ())), preferred_element_type=F32) - ft_ref[pl.ds(head, 1), lo:hi]
    if diagonal:
        n = hi - lo
        s = jnp.where(lax.broadcasted_iota(jnp.int32, (n, n), 1) <= lax.broadcasted_iota(jnp.int32, (n, n), 0), s, -jnp.inf)
    return s


def fox_forward(name, qn, kn, kvf, f_cum_t):
    S = qn.shape[0]
    t = _row_tile(S, FOX_T)
    v_block0 = D_MODEL // LANES

    def variant(j, pair, q_ref, k_ref, v_ref, ft_ref, y_ref, lse_ref):
        ys, lses = [], []
        for a, (mask, qa) in enumerate(_fox_heads(q_ref[...])):
            parts = [(lo, hi, _fox_scores(qa, k_ref, ft_ref, 2 * pair + a, lo, hi, dg)) for lo, hi, dg in _fox_parts(j, t)]
            m = functools.reduce(jnp.maximum, [jnp.max(s, axis=1, keepdims=True) for _, _, s in parts])
            l, acc = 0.0, 0.0
            for lo, hi, s in parts:
                e = jnp.exp(s - m)
                l = l + jnp.sum(e, axis=1, keepdims=True)
                acc = acc + jnp.dot(e.astype(BF16), v_ref[lo:hi, :].astype(BF16), preferred_element_type=F32)
            ys.append(acc / l)
            lses.append(m + jnp.log(l))
        low = lax.broadcasted_iota(jnp.int32, (1, LANES), 1) < FOX_DH
        y_ref[...] = jnp.where(low, ys[0], ys[1])
        lse_ref[...] = jnp.where(low, lses[0], lses[1])

    def body(*refs):
        pair, i = pl.program_id(0), pl.program_id(1)
        for j in range(S // t):
            pl.when(i == j)(functools.partial(variant, j, pair, *refs))

    return pl.pallas_call(
        body, name=name,
        out_shape=[jax.ShapeDtypeStruct((S, D_MODEL), F32), jax.ShapeDtypeStruct((S, D_MODEL), F32)],
        grid=(N_PAIR, S // t),
        in_specs=[pl.BlockSpec((t, LANES), lambda p, i: (i, p)),
                  pl.BlockSpec((S, LANES), lambda p, i: (0, p)),
                  pl.BlockSpec((S, LANES), lambda p, i: (0, v_block0 + p)),
                  pl.BlockSpec((LANES, S), lambda p, i: (0, 0))],
        out_specs=[pl.BlockSpec((t, LANES), lambda p, i: (i, p)),
                   pl.BlockSpec((t, LANES), lambda p, i: (i, p))],
        compiler_params=_params(("parallel", "arbitrary")),
    )(qn, kn, kvf, f_cum_t)


def fox_backward(name, qn, kn, kvf, f_cum_t, y, dy, lse):
    S = qn.shape[0]
    t = _row_tile(S, FOX_T)
    v_block0 = D_MODEL // LANES

    def variant(j, pair, q_ref, k_ref, v_ref, ft_ref, y_ref, dy_ref, lse_ref, dq_ref, dk_ref, dv_ref, dfq_ref, dfk_ref,
                dv_acc):
        y2, dy2, lse2 = y_ref[...], dy_ref[...], lse_ref[...]
        lane = lax.broadcasted_iota(jnp.int32, (t, LANES), 1)
        dqs, dfq = [], jnp.zeros((t, LANES), F32)
        for a, (mask, qa) in enumerate(_fox_heads(q_ref[...].astype(F32))):
            lse_a = jnp.max(jnp.where(mask, lse2, -jnp.inf), axis=1, keepdims=True)
            dy_a = jnp.where(mask, dy2, 0.0)
            delta = jnp.sum(dy_a * y2, axis=1, keepdims=True)
            dy_b = dy_a.astype(BF16)
            dq, row_sum = 0.0, 0.0
            for lo, hi, dg in _fox_parts(j, t):
                p = jnp.exp(_fox_scores(qa, k_ref, ft_ref, 2 * pair + a, lo, hi, dg) - lse_a)
                dp = lax.dot_general(dy_b, v_ref[lo:hi, :].astype(BF16), (((1,), (1,)), ((), ())), preferred_element_type=F32)
                ds = p * (dp - delta)
                row_sum = row_sum + jnp.sum(ds, axis=1, keepdims=True)
                dfk_ref[pl.ds(a, 1), lo:hi] += -jnp.sum(ds, axis=0, keepdims=True)
                ds_b = ds.astype(BF16)
                dq = dq + jnp.dot(ds_b, k_ref[lo:hi, :].astype(BF16), preferred_element_type=F32)
                dk_ref[lo:hi, :] += lax.dot_general(ds_b, qa, (((0,), (0,)), ((), ())), preferred_element_type=F32)
                dv_acc[lo:hi, :] += lax.dot_general(p.astype(BF16), dy_b, (((0,), (0,)), ((), ())), preferred_element_type=F32)
            dqs.append(dq * FOX_SCALE)
            dfq = dfq + jnp.where(lane == 2 * pair + a, row_sum, 0.0)
        low = lax.broadcasted_iota(jnp.int32, (1, LANES), 1) < FOX_DH
        dq_ref[...] = jnp.where(low, dqs[0], dqs[1])
        dfq_ref[...] = dfq

    def body(*refs):
        pair, i = pl.program_id(0), pl.program_id(1)
        dk_ref, dv_ref, _, dfk_ref, dv_acc = refs[8:13]

        @pl.when(i == 0)
        def _():
            dk_ref[...] = jnp.zeros_like(dk_ref)
            dv_acc[...] = jnp.zeros_like(dv_acc)
            dfk_ref[...] = jnp.zeros_like(dfk_ref)

        for j in range(S // t):
            pl.when(i == j)(functools.partial(variant, j, pair, *refs))

        @pl.when(i == S // t - 1)
        def _():
            dv_ref[...] = dv_acc[...].astype(dv_ref.dtype)

    row_blk = pl.BlockSpec((t, LANES), lambda p, i: (i, p))
    col_blk = pl.BlockSpec((S, LANES), lambda p, i: (0, p))
    return pl.pallas_call(
        body, name=name,
        out_shape=[jax.ShapeDtypeStruct((S, D_MODEL), F32)] * 2
        + [jax.ShapeDtypeStruct((S, D_MODEL), BF16), jax.ShapeDtypeStruct((N_PAIR, S, LANES), F32),
           jax.ShapeDtypeStruct((N_PAIR, 8, S), F32)],
        grid=(N_PAIR, S // t),
        in_specs=[row_blk, col_blk,
                  pl.BlockSpec((S, LANES), lambda p, i: (0, v_block0 + p)),
                  pl.BlockSpec((LANES, S), lambda p, i: (0, 0)),
                  row_blk, row_blk, row_blk],
        out_specs=[row_blk, col_blk, col_blk,
                   pl.BlockSpec((None, t, LANES), lambda p, i: (p, i, 0)),
                   pl.BlockSpec((None, 8, S), lambda p, i: (p, 0, 0))],
        scratch_shapes=[pltpu.VMEM((S, LANES), F32)],
        compiler_params=_params(("parallel", "arbitrary")),
    )(qn, kn, kvf, f_cum_t, y, dy, lse)


def cumsum_rows(name, x, reverse):
    S = x.shape[0]
    C = LANES
    nc = S // C

    def body(x_ref, o_ref):
        row = lax.broadcasted_iota(jnp.int32, (C, C), 0)
        col = lax.broadcasted_iota(jnp.int32, (C, C), 1)
        tri = jnp.where(col >= row if reverse else col <= row, 1.0, 0.0).astype(F32)
        carry = jnp.zeros((1, LANES), F32)
        for i in (range(nc - 1, -1, -1) if reverse else range(nc)):
            blk = x_ref[i * C:(i + 1) * C, :]
            loc = jnp.dot(tri, blk, preferred_element_type=F32, precision=lax.Precision.HIGHEST)
            o_ref[i * C:(i + 1) * C, :] = loc + carry
            carry = carry + (loc[0:1, :] if reverse else loc[C - 1:C, :])

    return pl.pallas_call(body, name=name, out_shape=jax.ShapeDtypeStruct((S, LANES), F32),
                          compiler_params=_params())(x)


def adamw(name, parts, w, m, v):
    L, R, C = w.shape
    row_bytes = C * (7 * 4 + max(sum(n * a.dtype.itemsize for a, n in p) for p in parts))
    tr = _row_tile(R, 1024)
    while tr > 256 and tr % 2 == 0 and R % (tr // 2) == 0 and tr * row_bytes > ROW_BLOCK_BYTES:
        tr //= 2
    nr = R // tr
    counts = [len(p) for p in parts]

    def body(*refs):
        w_ref, m_ref, v_ref, g_out, d_out, m_out, v_out = refs[sum(counts):]
        for layer in range(L):
            p_refs = refs[sum(counts[:layer]):sum(counts[:layer + 1])]

            @pl.when(pl.program_id(0) == layer)
            def _(p_refs=p_refs, slots=[n for _, n in parts[layer]]):
                g = None
                for p_ref, n in zip(p_refs, slots):
                    for i in range(n):
                        g = p_ref[i].astype(F32) if g is None else g + p_ref[i].astype(F32)
                m2 = ADAM_B1 * m_ref[...] + (1.0 - ADAM_B1) * g
                v2 = ADAM_B2 * v_ref[...] + (1.0 - ADAM_B2) * jnp.square(g)
                m_hat = m2 / (1.0 - ADAM_B1 ** ADAM_STEP)
                v_hat = v2 / (1.0 - ADAM_B2 ** ADAM_STEP)
                g_out[...] = g
                d_out[...] = -ADAM_LR * (m_hat / (jnp.sqrt(v_hat) + ADAM_EPS) + ADAM_WD * w_ref[...])
                m_out[...] = m2
                v_out[...] = v2

    def part_spec(layer, n):
        return pl.BlockSpec((n, tr, C), lambda l, i: (0, jnp.where(l == layer, i, jnp.where(l < layer, 0, nr - 1)), 0))

    blk = pl.BlockSpec((None, tr, C), lambda l, i: (l, i, 0))
    return pl.pallas_call(
        body, name=name, out_shape=[jax.ShapeDtypeStruct((L, R, C), F32)] * 4, grid=(L, nr),
        in_specs=[part_spec(layer, n) for layer in range(L) for _, n in parts[layer]] + [blk, blk, blk],
        out_specs=[blk] * 4, compiler_params=_params(("arbitrary", "arbitrary")),
    )(*[a for layer in parts for a, _ in layer], w, m, v)


def kernel(x, c, positions, norm_mix_gain, norm_mlp_gain, w_ada, b_ada, w_mlp_in, w_mlp_out, ret_w_in, ret_norm_gain, ret_w_out, kv_norm_gain, kv_w_ada, kv_b_ada, kv_w, forget_bias, k_norm_gain, fox_w_in, q_norm_gain, fox_w_out, loss_target, m_norm_mix_gain, m_norm_mlp_gain, m_w_ada, m_b_ada, m_w_mlp_in, m_w_mlp_out, m_ret_w_in, m_ret_norm_gain, m_ret_w_out, m_kv_norm_gain, m_kv_w_ada, m_kv_b_ada, m_kv_w, m_forget_bias, m_k_norm_gain, m_fox_w_in, m_q_norm_gain, m_fox_w_out, v_norm_mix_gain, v_norm_mlp_gain, v_w_ada, v_b_ada, v_w_mlp_in, v_w_mlp_out, v_ret_w_in, v_ret_norm_gain, v_ret_w_out, v_kv_norm_gain, v_kv_w_ada, v_kv_b_ada, v_kv_w, v_forget_bias, v_k_norm_gain, v_fox_w_in, v_q_norm_gain, v_fox_w_out):
    D = D_MODEL
    S = x.shape[1]
    x0 = x.reshape(S, D)
    target = loss_target.reshape(S, D)
    me = 4 * lax.axis_index("x") + 2 * lax.axis_index("y") + lax.axis_index("c")
    n_ada = w_ada.shape[2]
    n_kvada = kv_w_ada.shape[1]
    n_kv = kv_w.shape[1]

    c_all, ret_gain = all_gather("gather_c", [c.reshape(D // LANES, LANES), ret_norm_gain.reshape(RET_HEADS, -1)])
    ret_gain = jnp.transpose(ret_gain, (1, 0, 2)).reshape(1, RET_HEADS * RET_V)
    c_act = rowwise("silu_c", _silu, [c_all.reshape(N_DEV, D)], [], [(D, F32)])[0]
    w_ada_cat = jnp.concatenate([w_ada[0], w_ada[1], kv_w_ada], axis=1).astype(BF16)[None]
    n_cat = 2 * n_ada + n_kvada
    ada_part = mm_nn("ada_proj", c_act, w_ada_cat)[0]
    ada_mine = all_to_all("ada_rows", [ada_part.reshape(N_DEV, n_cat // LANES, LANES)])[0]
    ada_mine = ada_mine.reshape(N_DEV, n_cat)
    ada_raw = [ada_mine[:, l * n_ada:(l + 1) * n_ada].reshape(1, 6 * D) for l in range(2)]
    kvada_raw = ada_mine[:, 2 * n_ada:].reshape(1, 2 * D)
    kv_bias = kv_b_ada.reshape(1, 2 * D)
    kv_gain = kv_norm_gain.reshape(1, D)
    fb = jnp.pad(forget_bias.reshape(1, FOX_HEADS), ((0, 0), (0, LANES - FOX_HEADS)))
    k_gain = jnp.tile(k_norm_gain.reshape(1, FOX_DH), (1, FOX_HEADS))
    q_gain = jnp.tile(q_norm_gain.reshape(1, FOX_DH), (1, FOX_HEADS))

    w_names = ["ret_in", "ret_out", "mlp_in0", "mlp_out0", "kv", "fox_in", "fox_out", "mlp_in1", "mlp_out1"]
    shards = [ret_w_in[0].astype(BF16), ret_w_out[0].astype(BF16), w_mlp_in[0].astype(BF16), w_mlp_out[0].astype(BF16),
              kv_w.T.astype(BF16), fox_w_in[0].astype(BF16), fox_w_out[0].astype(BF16), w_mlp_in[1].astype(BF16),
              w_mlp_out[1].astype(BF16)]
    two_level = {"ret_in", "ret_out", "mlp_in0", "mlp_out0"}
    by_columns = {"ret_in", "mlp_in0", "mlp_in1", "fox_in"}
    w_plans = {name: plan_gather if name in two_level else plan_gather_direct for name in w_names}
    w_handles, token = exchange_start("gather_weights_start", shards,
                                      [lax.empty((a.shape[0], N_DEV * a.shape[1]) if name in by_columns else
                                                 (N_DEV,) + a.shape, a.dtype) for name, a in zip(w_names, shards)],
                                      [w_plans[name] for name in w_names], after=(ada_mine, ret_gain))
    w_handles = dict(zip(w_names, w_handles))

    forwards = {}

    def forward_early(name, after):
        arrived = exchange_wait("gather_wait_" + name, [w_handles[name]], w_plans[name], after)[1]
        forwards[name], tok = exchange_start("gather_forward_start_" + name, None, arrived, plan_forward)
        return tok

    def weight(name, after):
        if name in two_level:
            return exchange_wait("gather_forward_wait_" + name, forwards[name], plan_forward, after)[1][0]
        return exchange_wait("gather_wait_" + name, [w_handles[name]], w_plans[name], after)[1][0]

    pos = positions.reshape(S, 1).astype(F32)
    half = RET_QK // 2
    inv_freq = jnp.asarray((ROPE_BASE ** (-np.arange(half, dtype=np.float32) / half)).reshape(1, half), F32)

    def angles(p, f):
        ang = p * f
        return jnp.cos(ang), jnp.sin(ang)

    cos, sin = rowwise("rope_table", angles, [pos], [inv_freq], [(half, F32), (half, F32)])

    def mod_mix(layer):
        def fn(xb, ada, bias, gain):
            sh, sc = _ada_slices(ada, bias)[:2]
            return _modulate(xb, gain[layer:layer + 1], sc, sh)
        return fn

    def mod_mlp(layer):
        def fn(xb, ada, bias, gain):
            sh, sc = _ada_slices(ada, bias)[3:5]
            return _modulate(xb, gain[layer:layer + 1], sc, sh)
        return fn

    h1_0 = rowwise("mod_mix0", mod_mix(0), [x0], [ada_raw[0], b_ada[0:1], norm_mix_gain], [(D, BF16)], after=token)[0]
    W_ret_in = weight("ret_in", forward_early("ret_in", h1_0))[None]
    proj = mm_nn("ret_proj", h1_0, W_ret_in, (BF16,))[0]
    token = forward_early("ret_out", proj)

    def rope_fwd(qb, kb, cs, sn):
        return (_rotate(qb.astype(F32), cs, sn, RET_HEADS, 1.0),
                _rotate(kb.astype(F32), cs, sn, RET_HEADS, 1.0) * (RET_QK ** -0.5))

    q_rot, k_rot = rowwise("rope", rope_fwd, [(proj, D, 0), (proj, D, 1), cos, sin], [], [(D, BF16), (D, BF16)],
                           after=token)
    v_ret = (proj, RET_V, (2 * D) // RET_V)
    y_ret = retention("ret_fwd", [((q_rot, RET_QK, 0), (k_rot, RET_QK, 0), v_ret)], False, [F32])[0]

    def ret_gate(yb, gb, gain):
        return _silu(gb.astype(F32)) * _norm_wide_heads(yb, gain, RET_HEADS)

    mixin0 = rowwise("ret_gate", ret_gate, [y_ret, (proj, 2 * D, 2)], [ret_gain], [(2 * D, BF16)])[0]
    W_ret_out = weight("ret_out", mixin0).reshape(1, 2 * D, D)
    FUSED_ROWS = 512

    def residual_mod(layer, slot):
        def fn(xb, bb, ada, bias, gain):
            s = _ada_slices(ada, bias)
            xn = xb + s[2] * bb
            return xn, _modulate(xn, gain[layer:layer + 1], s[4], s[3])
        return fn

    def with_residual(fn):
        return lambda acc, xb, *vs: (acc,) + tuple(fn(xb, acc, *vs))

    mix0, x1, h2_0 = mm_nn("ret_out", mixin0, W_ret_out, (F32, F32, BF16), with_residual(residual_mod(0, 0)), (x0,),
                           forward_early("mlp_in0", W_ret_out), (ada_raw[0], b_ada[0:1], norm_mlp_gain), FUSED_ROWS)

    W_mlp_in, W_mlp_out = {}, {}

    def mlp_forward(tag, h2, layer, out_dtypes=(F32,), epilogue=None, extra=(), vecs=()):
        W_mlp_in[layer] = weight("mlp_in" + tag, h2)[None]
        early = forward_early("mlp_out" + tag, W_mlp_in[layer]) if "mlp_out" + tag in two_level else None
        u, act = mm_nn("mlp_in" + tag, h2, W_mlp_in[layer], (BF16, BF16), after=early,
                       epilogue=lambda acc: (acc, jnp.square(jnp.maximum(acc, 0.0))))
        W_mlp_out[layer] = weight("mlp_out" + tag, act).reshape(1, 4 * D, D)
        return u, act, mm_nn("mlp_out" + tag, act, W_mlp_out[layer], out_dtypes, epilogue, extra, None, vecs,
                             FUSED_ROWS if epilogue else 2048)

    def res_mlp0(xb, bb, ada0, bias0, ada1, bias1, kva, kvb, gain_mix, gain_kv):
        xn = xb + _ada_slices(ada0, bias0)[5] * bb
        s1 = _ada_slices(ada1, bias1)
        kv_shift, kv_scale = _ada_slices(kva, kvb)
        return xn, _modulate(xn, gain_kv, kv_scale, kv_shift), _modulate(xn, gain_mix[1:2], s1[1], s1[0])

    u0, act0, (mlp0,) = mlp_forward("0", h2_0, 0)
    x2, h_kv, h1_1 = rowwise("res_mlp0", res_mlp0, [x1, mlp0],
                             [ada_raw[0], b_ada[0:1], ada_raw[1], b_ada[1:2], kvada_raw, kv_bias, norm_mix_gain, kv_gain],
                             [(D, F32), (D, BF16), (D, BF16)])

    W_kv = jnp.pad(weight("kv", h_kv).reshape(N_DEV * n_kv, D), ((0, KV_PAD - N_DEV * n_kv), (0, 0)))[None]
    kvf = mm_nt("kv_proj", h_kv, W_kv)[0]

    def kv_post(kb, fblk, kg, bias):
        head = lax.broadcasted_iota(jnp.int32, fblk.shape, 1) < FOX_HEADS
        return _norm_fox_heads(kb, kg), jnp.where(head, _log_sigmoid(fblk + bias), 0.0)

    kn, log_f = rowwise("kv_post", kv_post, [(kvf, D, 0), (kvf, LANES, 2 * D // LANES)], [k_gain, fb],
                        [(D, BF16), (LANES, F32)])
    f_cum = cumsum_rows("f_cumsum", log_f, reverse=False)
    f_cum_t = f_cum.T

    W_fox_in = weight("fox_in", kvf)[None]
    qo = mm_nn("fox_proj", h1_1, W_fox_in)[0]
    qn = rowwise("q_norm", _norm_fox_heads, [(qo, D, 0)], [q_gain], [(D, BF16)])[0]
    y_att, lse = fox_forward("fox_fwd", qn, kn, kvf, f_cum_t)
    mixin1 = rowwise("fox_gate", lambda ob, yb: jax.nn.sigmoid(ob) * yb, [(qo, D, 1), y_att], [], [(D, BF16)])[0]
    W_fox_out = weight("fox_out", mixin1).reshape(1, D, D)
    mix1, x3, h2_1 = mm_nn("fox_out", mixin1, W_fox_out, (F32, F32, BF16), with_residual(residual_mod(1, 0)), (x2,),
                           None, (ada_raw[1], b_ada[1:2], norm_mlp_gain), FUSED_ROWS)
    u1, act1, (mlp1,) = mlp_forward("1", h2_1, 1)

    def scatter_start(tag, gws, after=()):
        lands = [lax.empty((4,) + g.shape[1:], g.dtype) for g in gws]
        return exchange_start("scatter_sibling_start_" + tag, gws, lands, plan_to_sibling, after)

    def scatter_relay(tag, handles, after, start_after=()):
        gws, from_sibling = exchange_wait("scatter_sibling_wait_" + tag, handles, plan_to_sibling, after)
        sums = [chip_sum("chip_sum_%s%d" % (tag, i), g, s) for i, (g, s) in enumerate(zip(gws, from_sibling))]
        lands = [lax.empty((3,) + s.shape[1:], s.dtype) for s in sums]
        return exchange_start("scatter_owner_start_" + tag, sums, lands, plan_to_owners, start_after)

    def scatter_direct_start(tag, gws):
        lands = [lax.empty(g.shape, g.dtype) for g in gws]
        return exchange_start("scatter_direct_start_" + tag, gws, lands, plan_scatter_direct)

    def scatter_direct_finish(tag, handles, after):
        return [[(r, N_DEV)] for r in exchange_wait("scatter_direct_wait_" + tag, handles, plan_scatter_direct, after)[1]]

    def scatter_finish(tag, handles, after):
        sums, received = exchange_wait("scatter_owner_wait_" + tag, handles, plan_to_owners, after)
        return [[(s, 1), (r, 3)] for s, r in zip(sums, received)]

    def loss_head(xb, bb, tb, ada, bias):
        g2 = _ada_slices(ada, bias)[5]
        err = xb + g2 * bb - tb
        dx = err * (1.0 / D)
        loss = 0.5 * jnp.sum(jnp.sum(err * err, axis=1, keepdims=True) * (1.0 / D), axis=0, keepdims=True)
        return dx, (dx * g2), jnp.broadcast_to(loss, (1, LANES)), jnp.sum(dx * bb, axis=0, keepdims=True)

    dx4, dmlp1, loss_acc, dg2_1 = rowwise("loss_head", loss_head, [x3, mlp1, target], [ada_raw[1], b_ada[1:2]],
                                          [(D, F32), (D, BF16)], [(1, LANES), (1, D)])

    def mlp_backward(tag, dmlp, act, u, h2, layer, after=None):
        du = mm_nt("mlp_out_dx" + tag, dmlp, W_mlp_out[layer], (BF16,), extra=(u,), after=after,
                   epilogue=lambda acc, ub: (acc * (2.0 * jnp.maximum(ub.astype(F32), 0.0)),))[0]
        gw_out = mm_tn("mlp_out_dw" + tag, act, dmlp, 1, BF16).reshape(N_DEV, -1, D)
        dh2 = mm_nt("mlp_in_dx" + tag, du, W_mlp_in[layer], (BF16,))[0]
        gw_in = mm_tn("mlp_in_dw" + tag, h2, du, N_DEV, BF16)
        return dh2, gw_in, gw_out

    def mod_backward(layer, slots, gate_slot):
        def fn(xb, dhb, dresb, branchb, ada, bias, gain):
            s = _ada_slices(ada, bias)
            g = gain[layer:layer + 1]
            dx, dgain, dsc, dsh = _vjp(_modulate, (xb, g, s[slots[1]], s[slots[0]]), dhb.astype(F32))
            dx = dx + dresb
            d_branch = dx * s[gate_slot]
            return dx, d_branch, dgain, dsc, dsh, jnp.sum(dx * branchb, axis=0, keepdims=True)
        return fn

    vec = (1, D)
    dh2_1, gw_mlp_in1, gw_mlp_out1 = mlp_backward("1", dmlp1, act1, u1, h2_1, 1)
    scat_a1, token_a1 = scatter_direct_start("a1", [gw_mlp_in1, gw_mlp_out1])
    dx3, dmix1, dgain_mlp1, dsc2_1, dsh2_1, dg1_1 = rowwise(
        "mod_mlp1_bwd", mod_backward(1, (3, 4), 2), [x3, dh2_1, dx4, mix1], [ada_raw[1], b_ada[1:2], norm_mlp_gain],
        [(D, F32), (D, BF16)], [vec] * 4, after=token_a1)
    dmixin1 = mm_nt("fox_out_dx", dmix1, W_fox_out)[0]
    gw_fox_out = mm_tn("fox_out_dw", mixin1, dmix1, 1, BF16).reshape(N_DEV, -1, D)

    def fox_gate_bwd(db, ob, yb):
        sg = jax.nn.sigmoid(ob)
        return db * sg, db * yb * sg * (1.0 - sg)

    dy_att, d_og = rowwise("fox_gate_bwd", fox_gate_bwd, [dmixin1, (qo, D, 1), y_att], [], [(D, F32), (D, BF16)])
    dqn, dkn, dv_att, dfq, dfk = fox_backward("fox_bwd", qn, kn, kvf, f_cum_t, y_att, dy_att, lse)

    def q_norm_bwd(qb, db, ogb, gain):
        dq, dgain = _vjp(_norm_fox_heads, (qb, gain), db)
        return jnp.concatenate([dq, ogb.astype(F32)], axis=1), dgain

    dqo, dq_gain = rowwise("q_norm_bwd", q_norm_bwd, [(qo, D, 0), dqn, d_og], [q_gain], [(2 * D, BF16)], [vec])
    dh1_1 = mm_nt("fox_proj_dx", dqo, W_fox_in, (BF16,))[0]
    gw_fox_in = mm_tn("fox_proj_dw", h1_1, dqo, N_DEV, BF16)

    dfk_rows = jnp.pad(dfk[:, :2, :].reshape(FOX_HEADS, S).T, ((0, 0), (0, LANES - FOX_HEADS)))

    def df_total(*blks):
        tot = blks[0]
        for b in blks[1:]:
            tot = tot + b
        return tot

    d_fcum = rowwise("df_sum", df_total, [dfk_rows] + [(dfq, LANES, 0, p) for p in range(N_PAIR)], [], [(LANES, F32)])[0]
    d_logf = cumsum_rows("df_cumsum", d_fcum, reverse=True)

    def kv_post_bwd(kb, fblk, dkb, dvb, dlf, kg, bias):
        dk, dgain = _vjp(_norm_fox_heads, (kb, kg), dkb)
        df = dlf * (1.0 / (1.0 + jnp.exp(fblk + bias)))
        return jnp.concatenate([dk, dvb.astype(F32), df], axis=1), dgain, jnp.sum(df, axis=0, keepdims=True)

    dkvf, dk_gain, dfb = rowwise("kv_post_bwd", kv_post_bwd,
                                 [(kvf, D, 0), (kvf, LANES, 2 * D // LANES), dkn, dv_att, d_logf], [k_gain, fb],
                                 [(KV_PAD, BF16)], [vec, (1, LANES)])
    dh_kv = mm_nn("kv_proj_dx", dkvf, W_kv, (BF16,))[0]
    gw_kv = mm_tn("kv_proj_dw", dkvf, h_kv, 1, BF16)[0, :N_DEV * n_kv].reshape(N_DEV, n_kv, D)
    scat_a2, token_a = scatter_direct_start("a2", [gw_fox_out, gw_fox_in, gw_kv])

    def x2_bwd(xb, dh1b, dhkb, dresb, branchb, ada0, bias0, ada1, bias1, kva, kvb, gain_mix, gain_kv):
        s1 = _ada_slices(ada1, bias1)
        kv_shift, kv_scale = _ada_slices(kva, kvb)
        dxa, dgain_mix, dsc1, dsh1 = _vjp(_modulate, (xb, gain_mix[1:2], s1[1], s1[0]), dh1b.astype(F32))
        dxb, dgain_kv, dkv_scale, dkv_shift = _vjp(_modulate, (xb, gain_kv, kv_scale, kv_shift), dhkb.astype(F32))
        dx = dresb + dxa + dxb
        g2 = _ada_slices(ada0, bias0)[5]
        return (dx, dx * g2, dgain_mix, dsc1, dsh1, dgain_kv, dkv_scale, dkv_shift,
                jnp.sum(dx * branchb, axis=0, keepdims=True))

    (dx2, dmlp0, dgain_mix1, dsc1_1, dsh1_1, dgain_kv, dkv_scale, dkv_shift, dg2_0) = rowwise(
        "x2_bwd", x2_bwd, [x2, dh1_1, dh_kv, dx3, mlp0],
        [ada_raw[0], b_ada[0:1], ada_raw[1], b_ada[1:2], kvada_raw, kv_bias, norm_mix_gain, kv_gain],
        [(D, F32), (D, BF16)], [vec] * 7, after=token_a)

    dh2_0, gw_mlp_in0, gw_mlp_out0 = mlp_backward("0", dmlp0, act0, u0, h2_0, 0)
    scat_b1, token_b1 = scatter_direct_start("b1", [gw_mlp_in0, gw_mlp_out0])
    dx1, dmix0, dgain_mlp0, dsc2_0, dsh2_0, dg1_0 = rowwise(
        "mod_mlp0_bwd", mod_backward(0, (3, 4), 2), [x1, dh2_0, dx2, mix0], [ada_raw[0], b_ada[0:1], norm_mlp_gain],
        [(D, F32), (D, BF16)], [vec] * 4, after=token_b1)
    dmixin0 = mm_nt("ret_out_dx", dmix0, W_ret_out, (BF16,))[0]
    gw_ret_out = mm_tn("ret_out_dw", mixin0, dmix0, 1, BF16).reshape(N_DEV, -1, D)
    scat_b2, token_b = scatter_direct_start("b2", [gw_ret_out])

    def ret_gate_bwd(db, yb, gb, gain):
        return _vjp(ret_gate, (yb, gb.astype(F32), gain), db.astype(F32))

    dy_ret, dgate, dret_gain = rowwise("ret_gate_bwd", ret_gate_bwd, [dmixin0, y_ret, (proj, 2 * D, 2)], [ret_gain],
                                       [(2 * D, BF16), (2 * D, BF16)], [(1, 2 * D)], after=token_b)
    dy_h = (dy_ret, RET_V, 0)
    q_h, k_h = (q_rot, RET_QK, 0), (k_rot, RET_QK, 0)
    dq_rot = retention("ret_dq", [(dy_h, v_ret, k_h)], False, [F32])[0]
    dk_rot, dv_ret = retention("ret_dk_dv", [(v_ret, dy_h, q_h), (k_h, q_h, dy_h)], True, [F32, BF16])

    def rope_bwd(dqb, dkb, dvb, dgb, cs, sn):
        dq = _rotate(dqb, cs, sn, RET_HEADS, -1.0)
        dk = _rotate(dkb, cs, sn, RET_HEADS, -1.0) * (RET_QK ** -0.5)
        return jnp.concatenate([dq, dk, dvb.astype(F32), dgb.astype(F32)], axis=1)

    dproj = rowwise("rope_bwd", rope_bwd, [dq_rot, dk_rot, dv_ret, dgate, cos, sin], [], [(6 * D, BF16)])[0]
    gw_ret_in = mm_tn("ret_proj_dw", h1_0, dproj, N_DEV, BF16)
    scat_c, token_c = scatter_start("c", [gw_ret_in])
    dh1_0 = mm_nt("ret_proj_dx", dproj, W_ret_in, (BF16,), after=token_c)[0]

    def x0_bwd(xb, dhb, dresb, ada, bias, gain):
        s = _ada_slices(ada, bias)
        dx, dgain, dsc, dsh = _vjp(_modulate, (xb, gain[0:1], s[1], s[0]), dhb.astype(F32))
        return dx + dresb, dgain, dsc, dsh

    grad_x, dgain_mix0, dsc1_0, dsh1_0 = rowwise("x0_bwd", x0_bwd, [x0, dh1_0, dx1],
                                                 [ada_raw[0], b_ada[0:1], norm_mix_gain], [(D, F32)], [vec] * 3)

    small = jnp.concatenate([
        dsh1_0, dsc1_0, dg1_0, dsh2_0, dsc2_0, dg2_0,
        dsh1_1, dsc1_1, dg1_1, dsh2_1, dsc2_1, dg2_1,
        dkv_shift, dkv_scale,
        dgain_mix0, dgain_mix1, dgain_mlp0, dgain_mlp1, dgain_kv,
        dret_gain,
        dq_gain.reshape(FOX_HEADS, FOX_DH).sum(axis=0).reshape(1, FOX_DH),
        dk_gain.reshape(FOX_HEADS, FOX_DH).sum(axis=0).reshape(1, FOX_DH),
        dfb,
        loss_acc,
    ], axis=1)
    small_all = all_gather("gather_small", [small.reshape(-1, LANES)])[0].reshape(N_DEV, 1, -1)
    loss = jnp.sum(small_all[:, 0, -1])
    scat_c, token_c = scatter_relay("c", scat_c, grad_x, start_after=(small_all,))
    o_ada = 14 * D
    d_ada = small_all[:, 0, :o_ada]
    d_cat = jnp.concatenate([
        lax.dynamic_slice_in_dim(d_ada[:, 0:6 * D], me * n_ada, n_ada, axis=1),
        lax.dynamic_slice_in_dim(d_ada[:, 6 * D:12 * D], me * n_ada, n_ada, axis=1),
        lax.dynamic_slice_in_dim(d_ada[:, 12 * D:14 * D], me * n_kvada, n_kvada, axis=1)], axis=1)
    gw_ada_cat = mm_tn("ada_dw", c_act, d_cat, 1, F32, after=token_c)[0]

    results = {}

    def update(name, parts, w, m, v, layers=1):
        per_layer = parts if layers > 1 else [parts]
        shape = w.shape
        C = shape[-1]
        R = int(np.prod(shape)) // (layers * C)
        per_layer = [p if isinstance(p, list) else [(p, p.shape[0])] for p in per_layer]
        per_layer = [[(a.reshape(a.shape[0], R, C), n) for a, n in p] for p in per_layer]
        outs = adamw("adamw_" + name, per_layer, w.reshape(layers, R, C), m.reshape(layers, R, C), v.reshape(layers, R, C))
        results[name] = tuple(t.reshape(shape) for t in outs)

    def small_parts(lo, width):
        return small_all[:, :, lo:lo + width]

    update("norm_mix_gain", jnp.concatenate([small_parts(o_ada, D), small_parts(o_ada + D, D)], axis=1),
           norm_mix_gain, m_norm_mix_gain, v_norm_mix_gain)
    update("norm_mlp_gain", jnp.concatenate([small_parts(o_ada + 2 * D, D), small_parts(o_ada + 3 * D, D)], axis=1),
           norm_mlp_gain, m_norm_mlp_gain, v_norm_mlp_gain)
    update("w_ada", [gw_ada_cat[None, :, :n_ada], gw_ada_cat[None, :, n_ada:2 * n_ada]], w_ada, m_w_ada, v_w_ada, layers=2)
    update("b_ada", jnp.concatenate([small_parts(0, 6 * D), small_parts(6 * D, 6 * D)], axis=1), b_ada, m_b_ada, v_b_ada)
    o_ret = o_ada + 5 * D
    n_rg = ret_norm_gain.shape[2]
    ret_gain_parts = small_parts(o_ret, 2 * D).reshape(N_DEV, RET_HEADS, RET_V)
    ret_gain_parts = lax.dynamic_slice_in_dim(ret_gain_parts, me * n_rg, n_rg, axis=2)
    update("ret_norm_gain", ret_gain_parts, ret_norm_gain, m_ret_norm_gain, v_ret_norm_gain)
    update("kv_norm_gain", small_parts(o_ada + 4 * D, D), kv_norm_gain, m_kv_norm_gain, v_kv_norm_gain)
    update("kv_w_ada", gw_ada_cat[None, :, 2 * n_ada:], kv_w_ada, m_kv_w_ada, v_kv_w_ada)
    update("kv_b_ada", small_parts(12 * D, 2 * D), kv_b_ada, m_kv_b_ada, v_kv_b_ada)
    o_q = o_ret + 2 * D
    update("forget_bias", small_parts(o_q + 2 * FOX_DH, FOX_HEADS), forget_bias, m_forget_bias, v_forget_bias)
    update("k_norm_gain", small_parts(o_q + FOX_DH, FOX_DH), k_norm_gain, m_k_norm_gain, v_k_norm_gain)
    update("q_norm_gain", small_parts(o_q, FOX_DH), q_norm_gain, m_q_norm_gain, v_q_norm_gain)

    r_mlp_in1, r_mlp_out1 = scatter_direct_finish("a1", scat_a1, results["q_norm_gain"][1])
    r_fox_out, r_fox_in, r_kv = scatter_direct_finish("a2", scat_a2, r_mlp_in1[0][0])
    r_mlp_in0, r_mlp_out0 = scatter_direct_finish("b1", scat_b1, r_kv[0][0])
    r_ret_out = scatter_direct_finish("b2", scat_b2, r_mlp_in0[0][0])[0]
    update("kv_w", r_kv, kv_w.T, m_kv_w.T, v_kv_w.T)
    results["kv_w"] = tuple(t.T for t in results["kv_w"])
    update("fox_w_in", r_fox_in, fox_w_in, m_fox_w_in, v_fox_w_in)
    update("fox_w_out", r_fox_out, fox_w_out, m_fox_w_out, v_fox_w_out)
    update("ret_w_out", r_ret_out, ret_w_out, m_ret_w_out, v_ret_w_out)
    update("w_mlp_in", [r_mlp_in0, r_mlp_in1], w_mlp_in, m_w_mlp_in, v_w_mlp_in, layers=2)
    update("w_mlp_out", [r_mlp_out0, r_mlp_out1], w_mlp_out, m_w_mlp_out, v_w_mlp_out, layers=2)
    r_ret_in = scatter_finish("c", scat_c, results["w_mlp_out"][1])[0]
    update("ret_w_in", r_ret_in, ret_w_in, m_ret_w_in, v_ret_w_in)

    order = ["norm_mix_gain", "norm_mlp_gain", "w_ada", "b_ada", "w_mlp_in", "w_mlp_out", "ret_w_in", "ret_norm_gain",
             "ret_w_out", "kv_norm_gain", "kv_w_ada", "kv_b_ada", "kv_w", "forget_bias", "k_norm_gain", "fox_w_in",
             "q_norm_gain", "fox_w_out"]
    out = [loss, grad_x.reshape(x.shape)]
    for slot in range(4):
        out += [results[n][slot] for n in order]
    return tuple(out)
```

```python
import functools
import math

import numpy as np
import jax
import jax.numpy as jnp
from jax import lax
from jax.experimental import pallas as pl
from jax.experimental.pallas import tpu as pltpu

F32 = jnp.float32
BF16 = jnp.bfloat16

N_DEV = 8
D_MODEL = 1024
RET_HEADS = 4
RET_QK = D_MODEL // RET_HEADS
RET_V = 2 * D_MODEL // RET_HEADS
RET_CHUNK = 256
ROPE_BASE = 10000.0
FOX_HEADS = 16
FOX_DH = D_MODEL // FOX_HEADS
EPS = 1e-6
LANES = 128
KV_PAD = 2 * D_MODEL + LANES

ADAM_LR = 0.001
ADAM_B1 = 0.9
ADAM_B2 = 0.999
ADAM_EPS = 1e-08
ADAM_WD = 0.01
ADAM_STEP = 10

VMEM_LIMIT_BYTES = 56 * 1024 * 1024


def _params(sem=None):
    return pltpu.CompilerParams(dimension_semantics=sem, vmem_limit_bytes=VMEM_LIMIT_BYTES)


def _me():
    return lax.axis_index("x"), lax.axis_index("y"), lax.axis_index("c")


def _peer(k):
    x, y, c = _me()
    return (1 - x if k & 4 else x, 1 - y if k & 2 else y, 1 - c if k & 1 else c)


def _peer_index(k):
    px, py, pc = _peer(k)
    return 4 * px + 2 * py + pc


def _exchange(name, xs, scatter):
    n = len(xs)

    def body(*refs):
        x_refs, o_refs = refs[:n], refs[n:2 * n]
        send_sems, recv_sems, local_sems = refs[2 * n:]
        x, y, c = _me()
        me = 4 * x + 2 * y + c
        local = []
        for i in range(n):
            src = x_refs[i].at[me] if scatter else x_refs[i]
            cp = pltpu.make_async_copy(src, o_refs[i].at[me], local_sems.at[i])
            cp.start()
            local.append(cp)
        remote = []
        for k in range(1, N_DEV):
            for i in range(n):
                src = x_refs[i].at[_peer_index(k)] if scatter else x_refs[i]
                cp = pltpu.make_async_remote_copy(
                    src_ref=src, dst_ref=o_refs[i].at[me],
                    send_sem=send_sems.at[(k - 1) * n + i], recv_sem=recv_sems.at[(k - 1) * n + i],
                    device_id=_peer(k), device_id_type=pl.DeviceIdType.MESH)
                cp.start()
                remote.append(cp)
        for cp in remote:
            cp.wait()
        for cp in local:
            cp.wait()

    out_shape = [jax.ShapeDtypeStruct(x.shape if scatter else (N_DEV,) + x.shape, x.dtype) for x in xs]
    any_spec = pl.BlockSpec(memory_space=pl.ANY)
    return pl.pallas_call(
        body, name=name, out_shape=out_shape,
        in_specs=[any_spec] * n, out_specs=[any_spec] * n,
        scratch_shapes=[pltpu.SemaphoreType.DMA(((N_DEV - 1) * n,)),
                        pltpu.SemaphoreType.DMA(((N_DEV - 1) * n,)),
                        pltpu.SemaphoreType.DMA((n,))],
    )(*xs)


def all_gather(name, xs):
    return _exchange(name, xs, scatter=False)


def all_to_all(name, xs):
    return _exchange(name, xs, scatter=True)


_HBM = pl.BlockSpec(memory_space=pltpu.HBM)
_SEM = pl.BlockSpec(memory_space=pltpu.SEMAPHORE)
_ANY = pl.BlockSpec(memory_space=pl.ANY)
_EFFECT = pltpu.SideEffectType.DATAFLOW_SIDE_EFFECTING

SIBLING = 1
CHIP_PEERS = (2, 4, 6)


def _my_index():
    x, y, c = _me()
    return 4 * x + 2 * y + c


def _slot(land, j):
    if len(land.shape) == 2:
        n = land.shape[1] // N_DEV
        return land.at[:, pl.ds(pl.multiple_of(j * n, LANES), n)]
    return land.at[j]


def plan_gather(x, land, me):
    return [(x, _slot(land, me), k) for k in (SIBLING,) + CHIP_PEERS]


def plan_gather_direct(x, land, me):
    return [(x, _slot(land, me), k) for k in range(1, N_DEV)]


def plan_scatter_direct(x, land, me):
    return [(x.at[me ^ k], land.at[me], k) for k in range(1, N_DEV)]


def plan_forward(x, land, me):
    return [(_slot(x, me ^ k), _slot(land, me ^ k), SIBLING) for k in CHIP_PEERS]


def plan_to_sibling(x, land, me):
    return [(x.at[me ^ SIBLING ^ (2 * q)], land.at[q], SIBLING) for q in range(4)]


def plan_to_owners(x, land, me):
    return [(x.at[q], land.at[q - 1], 2 * q) for q in (1, 2, 3)]


def own_gather(x, land, me):
    return x, _slot(land, me)


def own_scatter(x, land, me):
    return x.at[me], land.at[me]


OWN_COPY = {plan_gather: own_gather, plan_gather_direct: own_gather, plan_scatter_direct: own_scatter}
N_COPIES = {plan_gather: 4, plan_gather_direct: 7, plan_scatter_direct: 7, plan_forward: 3, plan_to_sibling: 4,
            plan_to_owners: 3}


def _plans(plan, n):
    return list(plan) if isinstance(plan, (list, tuple)) else [plan] * n


def _own_copies(plan, x_refs, land_refs, own_sems):
    me = _my_index()
    plans = _plans(plan, len(land_refs))
    with_own = [i for i, p in enumerate(plans) if p in OWN_COPY]
    return [pltpu.make_async_copy(*OWN_COPY[plans[i]](x_refs[i], land_refs[i], me), own_sems[s])
            for s, i in enumerate(with_own)]


def _plan_copies(plan, x_refs, land_refs, send_sems, recv_sems):
    me = _my_index()
    plans = _plans(plan, len(land_refs))
    return [pltpu.make_async_remote_copy(src_ref=src, dst_ref=dst, send_sem=send_sems[i].at[s], recv_sem=recv_sems[i].at[s],
                                         device_id=_peer(k), device_id_type=pl.DeviceIdType.MESH)
            for i in range(len(land_refs)) for s, (src, dst, k) in enumerate(plans[i](x_refs[i], land_refs[i], me))]


def exchange_now(name, xs, lands, plan):
    n = len(lands)
    n_x = 0 if xs is None else n

    def body(*refs):
        land_in, land_out = refs[n_x:n_x + n], refs[n_x + n:n_x + 2 * n]
        x_refs = land_in if xs is None else refs[:n]
        sems = refs[n_x + 2 * n:]
        copies = _own_copies(plan, x_refs, land_out, sems[2 * n:]) + _plan_copies(plan, x_refs, land_out, sems[:n], sems[n:2 * n])
        for cp in copies:
            cp.start()
        for cp in copies:
            cp.wait()

    return pl.pallas_call(
        body, name=name, out_shape=[jax.ShapeDtypeStruct(a.shape, a.dtype) for a in lands],
        in_specs=[_ANY] * (n_x + n), out_specs=[_ANY] * n,
        input_output_aliases={n_x + i: i for i in range(n)},
        scratch_shapes=[pltpu.SemaphoreType.DMA((N_COPIES[p],)) for p in _plans(plan, n)] * 2
        + [pltpu.SemaphoreType.DMA(())] * sum(p in OWN_COPY for p in _plans(plan, n)),
    )(*([] if xs is None else xs), *lands)


def exchange_start(name, xs, lands, plan, after=()):
    n, m = len(lands), len(after)
    n_x = 0 if xs is None else n
    arrays = ([] if xs is None else list(xs)) + list(lands)
    n_a = len(arrays)
    n_own = sum(p in OWN_COPY for p in _plans(plan, n))
    n_s = 2 * n + n_own

    def body(*refs):
        land_refs = refs[n_x:n_a]
        x_refs = land_refs if xs is None else refs[:n]
        sems = refs[n_a + m:n_a + m + n_s]
        token = refs[-1]
        for cp in _own_copies(plan, x_refs, land_refs, sems[2 * n:]):
            cp.start()
        for cp in _plan_copies(plan, x_refs, land_refs, sems[:n], sems[n:2 * n]):
            cp.start()
        token[...] = jnp.zeros_like(token)

    sems = [pltpu.SemaphoreType.DMA((N_COPIES[p],)) for p in _plans(plan, n)] * 2 + [pltpu.SemaphoreType.DMA(())] * n_own
    res = pl.pallas_call(
        body, name=name,
        out_shape=sems + [pltpu.HBM(a.shape, a.dtype) for a in arrays] + [jax.ShapeDtypeStruct((8, LANES), F32)],
        in_specs=[_HBM] * n_a + [_ANY] * m,
        out_specs=[_SEM] * n_s + [_HBM] * n_a + [pl.BlockSpec(memory_space=pltpu.VMEM)],
        input_output_aliases={i: n_s + i for i in range(n_a)},
        compiler_params=pltpu.CompilerParams(has_side_effects=_EFFECT),
    )(*[pltpu.with_memory_space_constraint(a, pltpu.HBM) for a in arrays], *after)
    thru = res[n_s:n_s + n_a]
    own = iter(res[2 * n:n_s])
    handles = [(res[i], res[n + i], None if xs is None else thru[i], thru[n_x + i],
                next(own) if p in OWN_COPY else None) for i, p in enumerate(_plans(plan, n))]
    return handles, res[-1]


def exchange_wait(name, handles, plan, after):
    n = len(handles)
    in_place = handles[0][2] is None
    arrays = ([] if in_place else [h[2] for h in handles]) + [h[3] for h in handles]
    n_a = len(arrays)
    own_sems = [h[4] for h in handles if h[4] is not None]
    n_s = 2 * n + len(own_sems)

    def body(*refs):
        land_refs = refs[n_a - n:n_a]
        x_refs = land_refs if in_place else refs[:n]
        sems = refs[n_a:n_a + n_s]
        for cp in _own_copies(plan, x_refs, land_refs, sems[2 * n:]):
            cp.wait()
        for cp in _plan_copies(plan, x_refs, land_refs, sems[:n], sems[n:2 * n]):
            cp.wait_send()
            cp.wait_recv()

    res = pl.pallas_call(
        body, name=name,
        out_shape=[pltpu.HBM(a.shape, a.dtype) for a in arrays],
        in_specs=[_HBM] * n_a + [_SEM] * n_s + [_ANY],
        out_specs=[_HBM] * n_a,
        input_output_aliases={i: i for i in range(n_a)},
        compiler_params=pltpu.CompilerParams(has_side_effects=_EFFECT),
    )(*arrays, *[h[0] for h in handles], *[h[1] for h in handles], *own_sems, after)
    return (None if in_place else res[:n]), res[n_a - n:]


def chip_sum(name, gw, from_sibling):
    _, R, C = gw.shape
    tr = _row_tile(R, 512)
    me = _my_index().astype(jnp.int32).reshape(1)

    def body(me_ref, g_ref, s_ref, o_ref):
        o_ref[...] = (g_ref[...].astype(F32) + s_ref[...].astype(F32)).astype(o_ref.dtype)

    slot = pl.BlockSpec((None, tr, C), lambda q, i, me_ref: (q, i, 0))
    return pl.pallas_call(
        body, name=name, out_shape=jax.ShapeDtypeStruct((4, R, C), BF16),
        grid_spec=pltpu.PrefetchScalarGridSpec(
            num_scalar_prefetch=1, grid=(4, R // tr),
            in_specs=[pl.BlockSpec((None, tr, C), lambda q, i, me_ref: (me_ref[0] ^ (2 * q), i, 0)), slot],
            out_specs=slot),
        compiler_params=_params(("arbitrary", "arbitrary")),
    )(me, gw, from_sibling)


def _tile(n, cap):
    best = None
    for t in range(LANES, min(n, cap) + 1, LANES):
        if n % t == 0:
            best = t
    if best is None or (best < 256 and n <= 2304):
        return n
    return best


def _row_tile(m, cap):
    if m <= cap:
        return m
    t = cap
    while m % t:
        t //= 2
    return t if t >= 256 else m


def _after_spec(after):
    return [] if after is None else [pl.BlockSpec(memory_space=pl.ANY)]


def _after_arg(after):
    return [] if after is None else [after]


def _mm_call(name, dims, grid, a_spec, b_spec, o_spec, o_shape, tile, a, b, out_dtypes, epilogue, extra, after, vecs=()):
    nk = grid[2]
    n_x, n_o = len(extra) + len(vecs), len(out_dtypes)

    def body(a_ref, b_ref, *refs):
        x_refs, o_refs = refs[:n_x], refs[len(refs) - n_s - n_o:len(refs) - n_s]
        part = lax.dot_general(a_ref[...].astype(BF16), b_ref[...].astype(BF16), (dims, ((), ())),
                               preferred_element_type=F32)

        def finish(acc):
            vals = (acc,) if epilogue is None else epilogue(acc, *[x[...] for x in x_refs])
            for o_ref, val in zip(o_refs, vals):
                o_ref[...] = val.astype(o_ref.dtype)

        if nk == 1:
            finish(part)
        else:
            acc_ref = refs[-1]
            k = pl.program_id(2)

            @pl.when(k == 0)
            def _():
                acc_ref[...] = part

            @pl.when(jnp.logical_and(k > 0, k < nk - 1))
            def _():
                acc_ref[...] += part

            @pl.when(k == nk - 1)
            def _():
                finish(acc_ref[...] + part)

    n_s = 0 if nk == 1 else 1
    return pl.pallas_call(
        body, name=name, out_shape=[jax.ShapeDtypeStruct(o_shape, dt) for dt in out_dtypes], grid=grid,
        in_specs=[a_spec, b_spec] + [o_spec] * len(extra) + [pl.BlockSpec(v.shape, lambda i, j, k: (0, 0)) for v in vecs]
        + _after_spec(after), out_specs=[o_spec] * n_o,
        scratch_shapes=[pltpu.VMEM(tile, F32)] * n_s,
        compiler_params=_params(("parallel", "parallel", "arbitrary")),
    )(a, b, *extra, *vecs, *_after_arg(after))


def mm_nn(name, a, w, out_dtypes=(F32,), epilogue=None, extra=(), after=None, vecs=(), tm_cap=2048):
    M, K = a.shape
    G, _, n = w.shape
    tk = _row_tile(K, 2048)
    tn = _tile(n, 512 if (K == tk and n >= 2048) else 1024)
    tm = _row_tile(M, 512 if tn > 1024 else (min(1024, tm_cap) if tk > 1024 else tm_cap))
    r = n // tn
    return _mm_call(
        name, ((1,), (0,)), (M // tm, G * r, K // tk),
        pl.BlockSpec((tm, tk), lambda i, j, k: (i, k)),
        pl.BlockSpec((None, tk, tn), lambda i, j, k: (j // r, k, j % r)),
        pl.BlockSpec((tm, tn), lambda i, j, k: (i, j)), (M, G * n), (tm, tn),
        a, w, out_dtypes, epilogue, extra, after, vecs)


def mm_nt(name, dy, w, out_dtypes=(F32,), epilogue=None, extra=(), after=None):
    M, N = dy.shape
    G, K, n = w.shape
    tn = _tile(n, 3072)
    r = n // tn
    tk = _row_tile(K, 512 if (G * r == 1 and K >= 2048 and K % 512 == 0) else 1024)
    tm = _row_tile(M, 512 if tk > 1024 else (1024 if tn > 2048 else 2048))
    return _mm_call(
        name, ((1,), (1,)), (M // tm, K // tk, G * r),
        pl.BlockSpec((tm, tn), lambda i, j, k: (i, k)),
        pl.BlockSpec((None, tk, tn), lambda i, j, k: (k // r, j, k % r)),
        pl.BlockSpec((tm, tk), lambda i, j, k: (i, j)), (M, K), (tm, tk),
        dy, w, out_dtypes, epilogue, extra, after)


def mm_tn(name, a, dy, G, out_dtype=F32, after=None):
    M, K = a.shape
    n = dy.shape[1] // G
    tn = _tile(n, 1024)
    tk = _row_tile(K, 512 if tn > 1024 else 1024)
    tm = _row_tile(M, 1024 if tk > 1024 else 2048)
    r = n // tn
    return _mm_call(
        name, ((0,), (0,)), (K // tk, G * r, M // tm),
        pl.BlockSpec((tm, tk), lambda i, j, k: (k, i)),
        pl.BlockSpec((tm, tn), lambda i, j, k: (k, j)),
        pl.BlockSpec((None, tk, tn), lambda i, j, k: (j // r, i, j % r)), (G, K, n), (tk, tn),
        a, dy, (out_dtype,), None, (), after)[0]


ROW_BLOCK_BYTES = 12 * 1024 * 1024


def rowwise(name, fn, rows, vecs, outs, accs=(), after=None):
    rows = [r if isinstance(r, tuple) else (r, r.shape[1], 0) for r in rows]
    rows = [r if len(r) == 4 else r + (None,) for r in rows]
    n_fn = len(rows) + len(vecs)
    vecs = list(vecs) + _after_arg(after)
    S = rows[0][0].shape[-2]
    row_bytes = sum(w * a.dtype.itemsize for a, w, _, _ in rows) + sum(w * jnp.dtype(dt).itemsize for w, dt in outs)
    tm = _row_tile(S, 1024)
    while tm > 256 and tm * row_bytes > ROW_BLOCK_BYTES:
        tm //= 2
    n_r, n_v, n_o, n_a = len(rows), len(vecs), len(outs), len(accs)

    def body(*refs):
        ins = [ref[...] for ref in refs[:n_r + n_v]]
        o_refs = refs[n_r + n_v:n_r + n_v + n_o]
        a_refs = refs[n_r + n_v + n_o:]
        res = fn(*ins[:n_fn])
        res = res if isinstance(res, (tuple, list)) else (res,)
        for ref, val in zip(o_refs, res[:n_o]):
            ref[...] = val.astype(ref.dtype)
        if n_a:
            @pl.when(pl.program_id(0) == 0)
            def _():
                for ref in a_refs:
                    ref[...] = jnp.zeros_like(ref)
            for ref, val in zip(a_refs, res[n_o:]):
                ref[...] += val

    in_specs = [pl.BlockSpec((tm, w), functools.partial(lambda cb, i: (i, cb), cb)) if slab is None else
                pl.BlockSpec((None, tm, w), functools.partial(lambda cb, slab, i: (slab, i, cb), cb, slab))
                for _, w, cb, slab in rows]
    in_specs += [pl.BlockSpec(v.shape, lambda i: (0, 0)) for v in vecs]
    out_specs = [pl.BlockSpec((tm, w), lambda i: (i, 0)) for w, _ in outs]
    out_specs += [pl.BlockSpec(a, lambda i: (0, 0)) for a in accs]
    out_shape = [jax.ShapeDtypeStruct((S, w), dt) for w, dt in outs]
    out_shape += [jax.ShapeDtypeStruct(a, F32) for a in accs]
    res = pl.pallas_call(
        body, name=name, out_shape=out_shape, grid=(S // tm,),
        in_specs=in_specs, out_specs=out_specs,
        compiler_params=_params(("arbitrary",)),
    )(*[r[0] for r in rows], *vecs)
    return res


def _rms(x):
    return x * lax.rsqrt(jnp.mean(x * x, axis=-1, keepdims=True) + EPS)


def _modulate(x, gain, scale, shift):
    return _rms(x) * gain * (1.0 + scale) + shift


def _ada_slices(ada_raw, bias):
    ada = ada_raw + bias
    return [ada[:, i * D_MODEL:(i + 1) * D_MODEL] for i in range(ada.shape[1] // D_MODEL)]


def _norm_wide_heads(y, gain, heads):
    w = y.shape[1] // heads
    return jnp.concatenate([_rms(y[:, h * w:(h + 1) * w]) * gain[:, h * w:(h + 1) * w] for h in range(heads)], axis=1)


def _norm_fox_heads(x, gain):
    outs = []
    for p in range(x.shape[1] // LANES):
        blk = x[:, p * LANES:(p + 1) * LANES]
        low = lax.broadcasted_iota(jnp.int32, blk.shape, 1) < FOX_DH
        sq = blk * blk
        ss_low = jnp.sum(jnp.where(low, sq, 0.0), axis=1, keepdims=True)
        ss_high = jnp.sum(jnp.where(low, 0.0, sq), axis=1, keepdims=True)
        outs.append(blk * lax.rsqrt(jnp.where(low, ss_low, ss_high) * (1.0 / FOX_DH) + EPS))
    return jnp.concatenate(outs, axis=1) * gain


def _silu(x):
    return x * jax.nn.sigmoid(x)


def _log_sigmoid(z):
    return -(jnp.maximum(-z, 0.0) + jnp.log(1.0 + jnp.exp(-jnp.abs(z))))


def _rotate(x, cos, sin, heads, sign):
    w = x.shape[1] // heads
    half = w // 2
    outs = []
    for h in range(heads):
        x1 = x[:, h * w:h * w + half]
        x2 = x[:, h * w + half:(h + 1) * w]
        outs += [x1 * cos - sign * x2 * sin, sign * x1 * sin + x2 * cos]
    return jnp.concatenate(outs, axis=1)


def _vjp(fn, primals, ct):
    return jax.vjp(fn, *primals)[1](ct)


_LOG_GAMMAS = [float(np.log(np.float32(1.0) - np.float32(2.0) ** np.float32(-5.0 - h))) for h in range(RET_HEADS)]


RET_ROWS = 1024


def retention(name, jobs, reverse, out_dtypes):
    operands = []
    for job in jobs:
        for op in job:
            if not any(op[0] is o[0] and op[1:] == o[1:] for o in operands):
                operands.append(op)
    index = [[next(i for i, o in enumerate(operands) if op[0] is o[0] and op[1:] == o[1:]) for op in job] for job in jobs]
    S = operands[0][0].shape[0]
    C = RET_CHUNK
    rows = _row_tile(S, RET_ROWS)
    nb = S // rows
    n_in, n_job = len(operands), len(jobs)

    def body(*refs):
        in_refs, o_refs, states = refs[:n_in], refs[n_in:n_in + n_job], refs[n_in + n_job:]
        h = pl.program_id(0)

        @pl.when(pl.program_id(1) == 0)
        def _():
            for state in states:
                state[...] = jnp.zeros_like(state)

        log_g = jnp.float32(_LOG_GAMMAS[RET_HEADS - 1])
        for i in range(RET_HEADS - 2, -1, -1):
            log_g = jnp.where(h == i, jnp.float32(_LOG_GAMMAS[i]), log_g)
        row = lax.broadcasted_iota(jnp.int32, (C, C), 0)
        col = lax.broadcasted_iota(jnp.int32, (C, C), 1)
        rel = (col - row if reverse else row - col).astype(F32)
        decay = jnp.where(rel >= 0, jnp.exp(log_g * jnp.maximum(rel, 0.0)), 0.0)
        j = lax.broadcasted_iota(jnp.int32, (C, 1), 0).astype(F32)
        q_decay = jnp.exp(log_g * (C - j if reverse else j + 1.0))
        k_decay = jnp.exp(log_g * (j if reverse else C - 1.0 - j))
        chunk_decay = jnp.exp(jnp.full((1, 1), log_g * C, F32))

        chunks = range(rows // C)
        for ci in (reversed(chunks) if reverse else chunks):
            rs = slice(ci * C, (ci + 1) * C)
            for (qi, ki, vi), o_ref, state in zip(index, o_refs, states):
                qc = in_refs[qi][rs, :].astype(BF16)
                kf = in_refs[ki][rs, :].astype(F32)
                vc = in_refs[vi][rs, :].astype(BF16)
                scores = lax.dot_general(qc, kf.astype(BF16), (((1,), (1,)), ((), ())), preferred_element_type=F32) * decay
                intra = jnp.dot(scores.astype(BF16), vc, preferred_element_type=F32)
                cross = jnp.dot(qc, state[...].astype(BF16), preferred_element_type=F32) * q_decay
                o_ref[rs, :] = (intra + cross).astype(o_ref.dtype)
                upd = lax.dot_general((kf * k_decay).astype(BF16), vc, (((0,), (0,)), ((), ())), preferred_element_type=F32)
                state[...] = state[...] * chunk_decay + upd

    def block(i):
        return nb - 1 - i if reverse else i

    widths = [(q[1], v[1]) for q, _, v in jobs]
    return pl.pallas_call(
        body, name=name,
        out_shape=[jax.ShapeDtypeStruct((S, RET_HEADS * dv), dt) for (_, dv), dt in zip(widths, out_dtypes)],
        grid=(RET_HEADS, nb),
        in_specs=[pl.BlockSpec((rows, w), functools.partial(lambda off, h, i: (block(i), off + h), off))
                  for _, w, off in operands],
        out_specs=[pl.BlockSpec((rows, dv), lambda h, i: (block(i), h)) for _, dv in widths],
        scratch_shapes=[pltpu.VMEM((dk, dv), F32) for dk, dv in widths],
        compiler_params=_params(("parallel", "arbitrary")),
    )(*[o[0] for o in operands])


FOX_T = 512
N_PAIR = FOX_HEADS // 2
FOX_SCALE = FOX_DH ** -0.5


def _fox_heads(q2):
    low = lax.broadcasted_iota(jnp.int32, (1, LANES), 1) < FOX_DH
    return [(mask, jnp.where(mask, q2 * FOX_SCALE, 0.0).astype(BF16)) for mask in (low, jnp.logical_not(low))]


def _fox_parts(j, t):
    return ([(0, j * t, False)] if j else []) + [(j * t, (j + 1) * t, True)]


def _fox_scores(qa, k_ref, ft_ref, head, lo, hi, diagonal):
    k_blk = k_ref[lo:hi, :].astype(BF16)
    s = lax.dot_general(qa, k_blk, (((1,), (1,)), ((), ())), preferred_element_type=F32) - ft_ref[pl.ds(head, 1), lo:hi]
    if diagonal:
        n = hi - lo
        s = jnp.where(lax.broadcasted_iota(jnp.int32, (n, n), 1) <= lax.broadcasted_iota(jnp.int32, (n, n), 0), s, -jnp.inf)
    return s


def fox_forward(name, qn, kn, kvf, f_cum_t):
    S = qn.shape[0]
    t = _row_tile(S, FOX_T)
    v_block0 = D_MODEL // LANES

    def variant(j, pair, q_ref, k_ref, v_ref, ft_ref, y_ref, lse_ref):
        ys, lses = [], []
        for a, (mask, qa) in enumerate(_fox_heads(q_ref[...])):
            parts = [(lo, hi, _fox_scores(qa, k_ref, ft_ref, 2 * pair + a, lo, hi, dg)) for lo, hi, dg in _fox_parts(j, t)]
            m = functools.reduce(jnp.maximum, [jnp.max(s, axis=1, keepdims=True) for _, _, s in parts])
            l, acc = 0.0, 0.0
            for lo, hi, s in parts:
                e = jnp.exp(s - m)
                l = l + jnp.sum(e, axis=1, keepdims=True)
                acc = acc + jnp.dot(e.astype(BF16), v_ref[lo:hi, :].astype(BF16), preferred_element_type=F32)
            ys.append(acc / l)
            lses.append(m + jnp.log(l))
        low = lax.broadcasted_iota(jnp.int32, (1, LANES), 1) < FOX_DH
        y_ref[...] = jnp.where(low, ys[0], ys[1])
        lse_ref[...] = jnp.where(low, lses[0], lses[1])

    def body(*refs):
        pair, i = pl.program_id(0), pl.program_id(1)
        for j in range(S // t):
            pl.when(i == j)(functools.partial(variant, j, pair, *refs))

    return pl.pallas_call(
        body, name=name,
        out_shape=[jax.ShapeDtypeStruct((S, D_MODEL), F32), jax.ShapeDtypeStruct((S, D_MODEL), F32)],
        grid=(N_PAIR, S // t),
        in_specs=[pl.BlockSpec((t, LANES), lambda p, i: (i, p)),
                  pl.BlockSpec((S, LANES), lambda p, i: (0, p)),
                  pl.BlockSpec((S, LANES), lambda p, i: (0, v_block0 + p)),
                  pl.BlockSpec((LANES, S), lambda p, i: (0, 0))],
        out_specs=[pl.BlockSpec((t, LANES), lambda p, i: (i, p)),
                   pl.BlockSpec((t, LANES), lambda p, i: (i, p))],
        compiler_params=_params(("parallel", "arbitrary")),
    )(qn, kn, kvf, f_cum_t)


def fox_backward(name, qn, kn, kvf, f_cum_t, y, dy, lse):
    S = qn.shape[0]
    t = _row_tile(S, FOX_T)
    v_block0 = D_MODEL // LANES

    def variant(j, pair, q_ref, k_ref, v_ref, ft_ref, y_ref, dy_ref, lse_ref, dq_ref, dk_ref, dv_ref, dfq_ref, dfk_ref,
                dv_acc):
        y2, dy2, lse2 = y_ref[...], dy_ref[...], lse_ref[...]
        lane = lax.broadcasted_iota(jnp.int32, (t, LANES), 1)
        dqs, dfq = [], jnp.zeros((t, LANES), F32)
        for a, (mask, qa) in enumerate(_fox_heads(q_ref[...].astype(F32))):
            lse_a = jnp.max(jnp.where(mask, lse2, -jnp.inf), axis=1, keepdims=True)
            dy_a = jnp.where(mask, dy2, 0.0)
            delta = jnp.sum(dy_a * y2, axis=1, keepdims=True)
            dy_b = dy_a.astype(BF16)
            dq, row_sum = 0.0, 0.0
            for lo, hi, dg in _fox_parts(j, t):
                p = jnp.exp(_fox_scores(qa, k_ref, ft_ref, 2 * pair + a, lo, hi, dg) - lse_a)
                dp = lax.dot_general(dy_b, v_ref[lo:hi, :].astype(BF16), (((1,), (1,)), ((), ())), preferred_element_type=F32)
                ds = p * (dp - delta)
                row_sum = row_sum + jnp.sum(ds, axis=1, keepdims=True)
                dfk_ref[pl.ds(a, 1), lo:hi] += -jnp.sum(ds, axis=0, keepdims=True)
                ds_b = ds.astype(BF16)
                dq = dq + jnp.dot(ds_b, k_ref[lo:hi, :].astype(BF16), preferred_element_type=F32)
                dk_ref[lo:hi, :] += lax.dot_general(ds_b, qa, (((0,), (0,)), ((), ())), preferred_element_type=F32)
                dv_acc[lo:hi, :] += lax.dot_general(p.astype(BF16), dy_b, (((0,), (0,)), ((), ())), preferred_element_type=F32)
            dqs.append(dq * FOX_SCALE)
            dfq = dfq + jnp.where(lane == 2 * pair + a, row_sum, 0.0)
        low = lax.broadcasted_iota(jnp.int32, (1, LANES), 1) < FOX_DH
        dq_ref[...] = jnp.where(low, dqs[0], dqs[1])
        dfq_ref[...] = dfq

    def body(*refs):
        pair, i = pl.program_id(0), pl.program_id(1)
        dk_ref, dv_ref, _, dfk_ref, dv_acc = refs[8:13]

        @pl.when(i == 0)
        def _():
            dk_ref[...] = jnp.zeros_like(dk_ref)
            dv_acc[...] = jnp.zeros_like(dv_acc)
            dfk_ref[...] = jnp.zeros_like(dfk_ref)

        for j in range(S // t):
            pl.when(i == j)(functools.partial(variant, j, pair, *refs))

        @pl.when(i == S // t - 1)
        def _():
            dv_ref[...] = dv_acc[...].astype(dv_ref.dtype)

    row_blk = pl.BlockSpec((t, LANES), lambda p, i: (i, p))
    col_blk = pl.BlockSpec((S, LANES), lambda p, i: (0, p))
    return pl.pallas_call(
        body, name=name,
        out_shape=[jax.ShapeDtypeStruct((S, D_MODEL), F32)] * 2
        + [jax.ShapeDtypeStruct((S, D_MODEL), BF16), jax.ShapeDtypeStruct((N_PAIR, S, LANES), F32),
           jax.ShapeDtypeStruct((N_PAIR, 8, S), F32)],
        grid=(N_PAIR, S // t),
        in_specs=[row_blk, col_blk,
                  pl.BlockSpec((S, LANES), lambda p, i: (0, v_block0 + p)),
                  pl.BlockSpec((LANES, S), lambda p, i: (0, 0)),
                  row_blk, row_blk, row_blk],
        out_specs=[row_blk, col_blk, col_blk,
                   pl.BlockSpec((None, t, LANES), lambda p, i: (p, i, 0)),
                   pl.BlockSpec((None, 8, S), lambda p, i: (p, 0, 0))],
        scratch_shapes=[pltpu.VMEM((S, LANES), F32)],
        compiler_params=_params(("parallel", "arbitrary")),
    )(qn, kn, kvf, f_cum_t, y, dy, lse)


def cumsum_rows(name, x, reverse):
    S = x.shape[0]
    C = LANES
    nc = S // C

    def body(x_ref, o_ref):
        row = lax.broadcasted_iota(jnp.int32, (C, C), 0)
        col = lax.broadcasted_iota(jnp.int32, (C, C), 1)
        tri = jnp.where(col >= row if reverse else col <= row, 1.0, 0.0).astype(F32)
        carry = jnp.zeros((1, LANES), F32)
        for i in (range(nc - 1, -1, -1) if reverse else range(nc)):
            blk = x_ref[i * C:(i + 1) * C, :]
            loc = jnp.dot(tri, blk, preferred_element_type=F32, precision=lax.Precision.HIGHEST)
            o_ref[i * C:(i + 1) * C, :] = loc + carry
            carry = carry + (loc[0:1, :] if reverse else loc[C - 1:C, :])

    return pl.pallas_call(body, name=name, out_shape=jax.ShapeDtypeStruct((S, LANES), F32),
                          compiler_params=_params())(x)


def adamw(name, parts, w, m, v):
    L, R, C = w.shape
    row_bytes = C * (7 * 4 + max(sum(n * a.dtype.itemsize for a, n in p) for p in parts))
    tr = _row_tile(R, 1024)
    while tr > 256 and tr % 2 == 0 and R % (tr // 2) == 0 and tr * row_bytes > ROW_BLOCK_BYTES:
        tr //= 2
    nr = R // tr
    counts = [len(p) for p in parts]

    def body(*refs):
        w_ref, m_ref, v_ref, g_out, d_out, m_out, v_out = refs[sum(counts):]
        for layer in range(L):
            p_refs = refs[sum(counts[:layer]):sum(counts[:layer + 1])]

            @pl.when(pl.program_id(0) == layer)
            def _(p_refs=p_refs, slots=[n for _, n in parts[layer]]):
                g = None
                for p_ref, n in zip(p_refs, slots):
                    for i in range(n):
                        g = p_ref[i].astype(F32) if g is None else g + p_ref[i].astype(F32)
                m2 = ADAM_B1 * m_ref[...] + (1.0 - ADAM_B1) * g
                v2 = ADAM_B2 * v_ref[...] + (1.0 - ADAM_B2) * jnp.square(g)
                m_hat = m2 / (1.0 - ADAM_B1 ** ADAM_STEP)
                v_hat = v2 / (1.0 - ADAM_B2 ** ADAM_STEP)
                g_out[...] = g
                d_out[...] = -ADAM_LR * (m_hat / (jnp.sqrt(v_hat) + ADAM_EPS) + ADAM_WD * w_ref[...])
                m_out[...] = m2
                v_out[...] = v2

    def part_spec(layer, n):
        return pl.BlockSpec((n, tr, C), lambda l, i: (0, jnp.where(l == layer, i, jnp.where(l < layer, 0, nr - 1)), 0))

    blk = pl.BlockSpec((None, tr, C), lambda l, i: (l, i, 0))
    return pl.pallas_call(
        body, name=name, out_shape=[jax.ShapeDtypeStruct((L, R, C), F32)] * 4, grid=(L, nr),
        in_specs=[part_spec(layer, n) for layer in range(L) for _, n in parts[layer]] + [blk, blk, blk],
        out_specs=[blk] * 4, compiler_params=_params(("arbitrary", "arbitrary")),
    )(*[a for layer in parts for a, _ in layer], w, m, v)


def kernel(x, c, positions, norm_mix_gain, norm_mlp_gain, w_ada, b_ada, w_mlp_in, w_mlp_out, ret_w_in, ret_norm_gain, ret_w_out, kv_norm_gain, kv_w_ada, kv_b_ada, kv_w, forget_bias, k_norm_gain, fox_w_in, q_norm_gain, fox_w_out, loss_target, m_norm_mix_gain, m_norm_mlp_gain, m_w_ada, m_b_ada, m_w_mlp_in, m_w_mlp_out, m_ret_w_in, m_ret_norm_gain, m_ret_w_out, m_kv_norm_gain, m_kv_w_ada, m_kv_b_ada, m_kv_w, m_forget_bias, m_k_norm_gain, m_fox_w_in, m_q_norm_gain, m_fox_w_out, v_norm_mix_gain, v_norm_mlp_gain, v_w_ada, v_b_ada, v_w_mlp_in, v_w_mlp_out, v_ret_w_in, v_ret_norm_gain, v_ret_w_out, v_kv_norm_gain, v_kv_w_ada, v_kv_b_ada, v_kv_w, v_forget_bias, v_k_norm_gain, v_fox_w_in, v_q_norm_gain, v_fox_w_out):
    D = D_MODEL
    S = x.shape[1]
    x0 = x.reshape(S, D)
    target = loss_target.reshape(S, D)
    me = 4 * lax.axis_index("x") + 2 * lax.axis_index("y") + lax.axis_index("c")
    n_ada = w_ada.shape[2]
    n_kvada = kv_w_ada.shape[1]
    n_kv = kv_w.shape[1]

    c_all, ret_gain = all_gather("gather_c", [c.reshape(D // LANES, LANES), ret_norm_gain.reshape(RET_HEADS, -1)])
    ret_gain = jnp.transpose(ret_gain, (1, 0, 2)).reshape(1, RET_HEADS * RET_V)
    c_act = rowwise("silu_c", _silu, [c_all.reshape(N_DEV, D)], [], [(D, F32)])[0]
    w_ada_cat = jnp.concatenate([w_ada[0], w_ada[1], kv_w_ada], axis=1).astype(BF16)[None]
    n_cat = 2 * n_ada + n_kvada
    ada_part = mm_nn("ada_proj", c_act, w_ada_cat)[0]
    ada_mine = all_to_all("ada_rows", [ada_part.reshape(N_DEV, n_cat // LANES, LANES)])[0]
    ada_mine = ada_mine.reshape(N_DEV, n_cat)
    ada_raw = [ada_mine[:, l * n_ada:(l + 1) * n_ada].reshape(1, 6 * D) for l in range(2)]
    kvada_raw = ada_mine[:, 2 * n_ada:].reshape(1, 2 * D)
    kv_bias = kv_b_ada.reshape(1, 2 * D)
    kv_gain = kv_norm_gain.reshape(1, D)
    fb = jnp.pad(forget_bias.reshape(1, FOX_HEADS), ((0, 0), (0, LANES - FOX_HEADS)))
    k_gain = jnp.tile(k_norm_gain.reshape(1, FOX_DH), (1, FOX_HEADS))
    q_gain = jnp.tile(q_norm_gain.reshape(1, FOX_DH), (1, FOX_HEADS))

    w_names = ["ret_in", "ret_out", "mlp_in0", "mlp_out0", "kv", "fox_in", "fox_out", "mlp_in1", "mlp_out1"]
    shards = [ret_w_in[0].astype(BF16), ret_w_out[0].astype(BF16), w_mlp_in[0].astype(BF16), w_mlp_out[0].astype(BF16),
              kv_w.T.astype(BF16), fox_w_in[0].astype(BF16), fox_w_out[0].astype(BF16), w_mlp_in[1].astype(BF16),
              w_mlp_out[1].astype(BF16)]
    two_level = {"ret_in", "ret_out", "mlp_in0", "mlp_out0"}
    by_columns = {"ret_in", "mlp_in0", "mlp_in1", "fox_in"}
    w_plans = {name: plan_gather if name in two_level else plan_gather_direct for name in w_names}
    w_handles, token = exchange_start("gather_weights_start", shards,
                                      [lax.empty((a.shape[0], N_DEV * a.shape[1]) if name in by_columns else
                                                 (N_DEV,) + a.shape, a.dtype) for name, a in zip(w_names, shards)],
                                      [w_plans[name] for name in w_names], after=(ada_mine, ret_gain))
    w_handles = dict(zip(w_names, w_handles))

    forwards = {}

    def forward_early(name, after):
        arrived = exchange_wait("gather_wait_" + name, [w_handles[name]], w_plans[name], after)[1]
        forwards[name], tok = exchange_start("gather_forward_start_" + name, None, arrived, plan_forward)
        return tok

    def weight(name, after):
        if name in two_level:
            return exchange_wait("gather_forward_wait_" + name, forwards[name], plan_forward, after)[1][0]
        return exchange_wait("gather_wait_" + name, [w_handles[name]], w_plans[name], after)[1][0]

    pos = positions.reshape(S, 1).astype(F32)
    half = RET_QK // 2
    inv_freq = jnp.asarray((ROPE_BASE ** (-np.arange(half, dtype=np.float32) / half)).reshape(1, half), F32)

    def angles(p, f):
        ang = p * f
        return jnp.cos(ang), jnp.sin(ang)

    cos, sin = rowwise("rope_table", angles, [pos], [inv_freq], [(half, F32), (half, F32)])

    def mod_mix(layer):
        def fn(xb, ada, bias, gain):
            sh, sc = _ada_slices(ada, bias)[:2]
            return _modulate(xb, gain[layer:layer + 1], sc, sh)
        return fn

    def mod_mlp(layer):
        def fn(xb, ada, bias, gain):
            sh, sc = _ada_slices(ada, bias)[3:5]
            return _modulate(xb, gain[layer:layer + 1], sc, sh)
        return fn

    h1_0 = rowwise("mod_mix0", mod_mix(0), [x0], [ada_raw[0], b_ada[0:1], norm_mix_gain], [(D, BF16)], after=token)[0]
    W_ret_in = weight("ret_in", forward_early("ret_in", h1_0))[None]
    proj = mm_nn("ret_proj", h1_0, W_ret_in, (BF16,))[0]
    token = forward_early("ret_out", proj)

    def rope_fwd(qb, kb, cs, sn):
        return (_rotate(qb.astype(F32), cs, sn, RET_HEADS, 1.0),
                _rotate(kb.astype(F32), cs, sn, RET_HEADS, 1.0) * (RET_QK ** -0.5))

    q_rot, k_rot = rowwise("rope", rope_fwd, [(proj, D, 0), (proj, D, 1), cos, sin], [], [(D, BF16), (D, BF16)],
                           after=token)
    v_ret = (proj, RET_V, (2 * D) // RET_V)
    y_ret = retention("ret_fwd", [((q_rot, RET_QK, 0), (k_rot, RET_QK, 0), v_ret)], False, [F32])[0]

    def ret_gate(yb, gb, gain):
        return _silu(gb.astype(F32)) * _norm_wide_heads(yb, gain, RET_HEADS)

    mixin0 = rowwise("ret_gate", ret_gate, [y_ret, (proj, 2 * D, 2)], [ret_gain], [(2 * D, BF16)])[0]
    W_ret_out = weight("ret_out", mixin0).reshape(1, 2 * D, D)
    FUSED_ROWS = 512

    def residual_mod(layer, slot):
        def fn(xb, bb, ada, bias, gain):
            s = _ada_slices(ada, bias)
            xn = xb + s[2] * bb
            return xn, _modulate(xn, gain[layer:layer + 1], s[4], s[3])
        return fn

    def with_residual(fn):
        return lambda acc, xb, *vs: (acc,) + tuple(fn(xb, acc, *vs))

    mix0, x1, h2_0 = mm_nn("ret_out", mixin0, W_ret_out, (F32, F32, BF16), with_residual(residual_mod(0, 0)), (x0,),
                           forward_early("mlp_in0", W_ret_out), (ada_raw[0], b_ada[0:1], norm_mlp_gain), FUSED_ROWS)

    W_mlp_in, W_mlp_out = {}, {}

    def mlp_forward(tag, h2, layer, out_dtypes=(F32,), epilogue=None, extra=(), vecs=()):
        W_mlp_in[layer] = weight("mlp_in" + tag, h2)[None]
        early = forward_early("mlp_out" + tag, W_mlp_in[layer]) if "mlp_out" + tag in two_level else None
        u, act = mm_nn("mlp_in" + tag, h2, W_mlp_in[layer], (BF16, BF16), after=early,
                       epilogue=lambda acc: (acc, jnp.square(jnp.maximum(acc, 0.0))))
        W_mlp_out[layer] = weight("mlp_out" + tag, act).reshape(1, 4 * D, D)
        return u, act, mm_nn("mlp_out" + tag, act, W_mlp_out[layer], out_dtypes, epilogue, extra, None, vecs,
                             FUSED_ROWS if epilogue else 2048)

    def res_mlp0(xb, bb, ada0, bias0, ada1, bias1, kva, kvb, gain_mix, gain_kv):
        xn = xb + _ada_slices(ada0, bias0)[5] * bb
        s1 = _ada_slices(ada1, bias1)
        kv_shift, kv_scale = _ada_slices(kva, kvb)
        return xn, _modulate(xn, gain_kv, kv_scale, kv_shift), _modulate(xn, gain_mix[1:2], s1[1], s1[0])

    u0, act0, (mlp0,) = mlp_forward("0", h2_0, 0)
    x2, h_kv, h1_1 = rowwise("res_mlp0", res_mlp0, [x1, mlp0],
                             [ada_raw[0], b_ada[0:1], ada_raw[1], b_ada[1:2], kvada_raw, kv_bias, norm_mix_gain, kv_gain],
                             [(D, F32), (D, BF16), (D, BF16)])

    W_kv = jnp.pad(weight("kv", h_kv).reshape(N_DEV * n_kv, D), ((0, KV_PAD - N_DEV * n_kv), (0, 0)))[None]
    kvf = mm_nt("kv_proj", h_kv, W_kv)[0]

    def kv_post(kb, fblk, kg, bias):
        head = lax.broadcasted_iota(jnp.int32, fblk.shape, 1) < FOX_HEADS
        return _norm_fox_heads(kb, kg), jnp.where(head, _log_sigmoid(fblk + bias), 0.0)

    kn, log_f = rowwise("kv_post", kv_post, [(kvf, D, 0), (kvf, LANES, 2 * D // LANES)], [k_gain, fb],
                        [(D, BF16), (LANES, F32)])
    f_cum = cumsum_rows("f_cumsum", log_f, reverse=False)
    f_cum_t = f_cum.T

    W_fox_in = weight("fox_in", kvf)[None]
    qo = mm_nn("fox_proj", h1_1, W_fox_in)[0]
    qn = rowwise("q_norm", _norm_fox_heads, [(qo, D, 0)], [q_gain], [(D, BF16)])[0]
    y_att, lse = fox_forward("fox_fwd", qn, kn, kvf, f_cum_t)
    mixin1 = rowwise("fox_gate", lambda ob, yb: jax.nn.sigmoid(ob) * yb, [(qo, D, 1), y_att], [], [(D, BF16)])[0]
    W_fox_out = weight("fox_out", mixin1).reshape(1, D, D)
    mix1, x3, h2_1 = mm_nn("fox_out", mixin1, W_fox_out, (F32, F32, BF16), with_residual(residual_mod(1, 0)), (x2,),
                           None, (ada_raw[1], b_ada[1:2], norm_mlp_gain), FUSED_ROWS)
    u1, act1, (mlp1,) = mlp_forward("1", h2_1, 1)

    def scatter_start(tag, gws, after=()):
        lands = [lax.empty((4,) + g.shape[1:], g.dtype) for g in gws]
        return exchange_start("scatter_sibling_start_" + tag, gws, lands, plan_to_sibling, after)

    def scatter_relay(tag, handles, after, start_after=()):
        gws, from_sibling = exchange_wait("scatter_sibling_wait_" + tag, handles, plan_to_sibling, after)
        sums = [chip_sum("chip_sum_%s%d" % (tag, i), g, s) for i, (g, s) in enumerate(zip(gws, from_sibling))]
        lands = [lax.empty((3,) + s.shape[1:], s.dtype) for s in sums]
        return exchange_start("scatter_owner_start_" + tag, sums, lands, plan_to_owners, start_after)

    def scatter_direct_start(tag, gws):
        lands = [lax.empty(g.shape, g.dtype) for g in gws]
        return exchange_start("scatter_direct_start_" + tag, gws, lands, plan_scatter_direct)

    def scatter_direct_finish(tag, handles, after):
        return [[(r, N_DEV)] for r in exchange_wait("scatter_direct_wait_" + tag, handles, plan_scatter_direct, after)[1]]

    def scatter_finish(tag, handles, after):
        sums, received = exchange_wait("scatter_owner_wait_" + tag, handles, plan_to_owners, after)
        return [[(s, 1), (r, 3)] for s, r in zip(sums, received)]

    def loss_head(xb, bb, tb, ada, bias):
        g2 = _ada_slices(ada, bias)[5]
        err = xb + g2 * bb - tb
        dx = err * (1.0 / D)
        loss = 0.5 * jnp.sum(jnp.sum(err * err, axis=1, keepdims=True) * (1.0 / D), axis=0, keepdims=True)
        return dx, (dx * g2), jnp.broadcast_to(loss, (1, LANES)), jnp.sum(dx * bb, axis=0, keepdims=True)

    dx4, dmlp1, loss_acc, dg2_1 = rowwise("loss_head", loss_head, [x3, mlp1, target], [ada_raw[1], b_ada[1:2]],
                                          [(D, F32), (D, BF16)], [(1, LANES), (1, D)])

    def mlp_backward(tag, dmlp, act, u, h2, layer, after=None):
        du = mm_nt("mlp_out_dx" + tag, dmlp, W_mlp_out[layer], (BF16,), extra=(u,), after=after,
                   epilogue=lambda acc, ub: (acc * (2.0 * jnp.maximum(ub.astype(F32), 0.0)),))[0]
        gw_out = mm_tn("mlp_out_dw" + tag, act, dmlp, 1, BF16).reshape(N_DEV, -1, D)
        dh2 = mm_nt("mlp_in_dx" + tag, du, W_mlp_in[layer], (BF16,))[0]
        gw_in = mm_tn("mlp_in_dw" + tag, h2, du, N_DEV, BF16)
        return dh2, gw_in, gw_out

    def mod_backward(layer, slots, gate_slot):
        def fn(xb, dhb, dresb, branchb, ada, bias, gain):
            s = _ada_slices(ada, bias)
            g = gain[layer:layer + 1]
            dx, dgain, dsc, dsh = _vjp(_modulate, (xb, g, s[slots[1]], s[slots[0]]), dhb.astype(F32))
            dx = dx + dresb
            d_branch = dx * s[gate_slot]
            return dx, d_branch, dgain, dsc, dsh, jnp.sum(dx * branchb, axis=0, keepdims=True)
        return fn

    vec = (1, D)
    dh2_1, gw_mlp_in1, gw_mlp_out1 = mlp_backward("1", dmlp1, act1, u1, h2_1, 1)
    scat_a1, token_a1 = scatter_direct_start("a1", [gw_mlp_in1, gw_mlp_out1])
    dx3, dmix1, dgain_mlp1, dsc2_1, dsh2_1, dg1_1 = rowwise(
        "mod_mlp1_bwd", mod_backward(1, (3, 4), 2), [x3, dh2_1, dx4, mix1], [ada_raw[1], b_ada[1:2], norm_mlp_gain],
        [(D, F32), (D, BF16)], [vec] * 4, after=token_a1)
    dmixin1 = mm_nt("fox_out_dx", dmix1, W_fox_out)[0]
    gw_fox_out = mm_tn("fox_out_dw", mixin1, dmix1, 1, BF16).reshape(N_DEV, -1, D)

    def fox_gate_bwd(db, ob, yb):
        sg = jax.nn.sigmoid(ob)
        return db * sg, db * yb * sg * (1.0 - sg)

    dy_att, d_og = rowwise("fox_gate_bwd", fox_gate_bwd, [dmixin1, (qo, D, 1), y_att], [], [(D, F32), (D, BF16)])
    dqn, dkn, dv_att, dfq, dfk = fox_backward("fox_bwd", qn, kn, kvf, f_cum_t, y_att, dy_att, lse)

    def q_norm_bwd(qb, db, ogb, gain):
        dq, dgain = _vjp(_norm_fox_heads, (qb, gain), db)
        return jnp.concatenate([dq, ogb.astype(F32)], axis=1), dgain

    dqo, dq_gain = rowwise("q_norm_bwd", q_norm_bwd, [(qo, D, 0), dqn, d_og], [q_gain], [(2 * D, BF16)], [vec])
    dh1_1 = mm_nt("fox_proj_dx", dqo, W_fox_in, (BF16,))[0]
    gw_fox_in = mm_tn("fox_proj_dw", h1_1, dqo, N_DEV, BF16)

    dfk_rows = jnp.pad(dfk[:, :2, :].reshape(FOX_HEADS, S).T, ((0, 0), (0, LANES - FOX_HEADS)))

    def df_total(*blks):
        tot = blks[0]
        for b in blks[1:]:
            tot = tot + b
        return tot

    d_fcum = rowwise("df_sum", df_total, [dfk_rows] + [(dfq, LANES, 0, p) for p in range(N_PAIR)], [], [(LANES, F32)])[0]
    d_logf = cumsum_rows("df_cumsum", d_fcum, reverse=True)

    def kv_post_bwd(kb, fblk, dkb, dvb, dlf, kg, bias):
        dk, dgain = _vjp(_norm_fox_heads, (kb, kg), dkb)
        df = dlf * (1.0 / (1.0 + jnp.exp(fblk + bias)))
        return jnp.concatenate([dk, dvb.astype(F32), df], axis=1), dgain, jnp.sum(df, axis=0, keepdims=True)

    dkvf, dk_gain, dfb = rowwise("kv_post_bwd", kv_post_bwd,
                                 [(kvf, D, 0), (kvf, LANES, 2 * D // LANES), dkn, dv_att, d_logf], [k_gain, fb],
                                 [(KV_PAD, BF16)], [vec, (1, LANES)])
    dh_kv = mm_nn("kv_proj_dx", dkvf, W_kv, (BF16,))[0]
    gw_kv = mm_tn("kv_proj_dw", dkvf, h_kv, 1, BF16)[0, :N_DEV * n_kv].reshape(N_DEV, n_kv, D)
    scat_a2, token_a = scatter_direct_start("a2", [gw_fox_out, gw_fox_in, gw_kv])

    def x2_bwd(xb, dh1b, dhkb, dresb, branchb, ada0, bias0, ada1, bias1, kva, kvb, gain_mix, gain_kv):
        s1 = _ada_slices(ada1, bias1)
        kv_shift, kv_scale = _ada_slices(kva, kvb)
        dxa, dgain_mix, dsc1, dsh1 = _vjp(_modulate, (xb, gain_mix[1:2], s1[1], s1[0]), dh1b.astype(F32))
        dxb, dgain_kv, dkv_scale, dkv_shift = _vjp(_modulate, (xb, gain_kv, kv_scale, kv_shift), dhkb.astype(F32))
        dx = dresb + dxa + dxb
        g2 = _ada_slices(ada0, bias0)[5]
        return (dx, dx * g2, dgain_mix, dsc1, dsh1, dgain_kv, dkv_scale, dkv_shift,
                jnp.sum(dx * branchb, axis=0, keepdims=True))

    (dx2, dmlp0, dgain_mix1, dsc1_1, dsh1_1, dgain_kv, dkv_scale, dkv_shift, dg2_0) = rowwise(
        "x2_bwd", x2_bwd, [x2, dh1_1, dh_kv, dx3, mlp0],
        [ada_raw[0], b_ada[0:1], ada_raw[1], b_ada[1:2], kvada_raw, kv_bias, norm_mix_gain, kv_gain],
        [(D, F32), (D, BF16)], [vec] * 7, after=token_a)

    dh2_0, gw_mlp_in0, gw_mlp_out0 = mlp_backward("0", dmlp0, act0, u0, h2_0, 0)
    scat_b1, token_b1 = scatter_direct_start("b1", [gw_mlp_in0, gw_mlp_out0])
    dx1, dmix0, dgain_mlp0, dsc2_0, dsh2_0, dg1_0 = rowwise(
        "mod_mlp0_bwd", mod_backward(0, (3, 4), 2), [x1, dh2_0, dx2, mix0], [ada_raw[0], b_ada[0:1], norm_mlp_gain],
        [(D, F32), (D, BF16)], [vec] * 4, after=token_b1)
    dmixin0 = mm_nt("ret_out_dx", dmix0, W_ret_out, (BF16,))[0]
    gw_ret_out = mm_tn("ret_out_dw", mixin0, dmix0, 1, BF16).reshape(N_DEV, -1, D)
    scat_b2, token_b = scatter_direct_start("b2", [gw_ret_out])

    def ret_gate_bwd(db, yb, gb, gain):
        return _vjp(ret_gate, (yb, gb.astype(F32), gain), db.astype(F32))

    dy_ret, dgate, dret_gain = rowwise("ret_gate_bwd", ret_gate_bwd, [dmixin0, y_ret, (proj, 2 * D, 2)], [ret_gain],
                                       [(2 * D, BF16), (2 * D, BF16)], [(1, 2 * D)], after=token_b)
    dy_h = (dy_ret, RET_V, 0)
    q_h, k_h = (q_rot, RET_QK, 0), (k_rot, RET_QK, 0)
    dq_rot = retention("ret_dq", [(dy_h, v_ret, k_h)], False, [F32])[0]
    dk_rot, dv_ret = retention("ret_dk_dv", [(v_ret, dy_h, q_h), (k_h, q_h, dy_h)], True, [F32, BF16])

    def rope_bwd(dqb, dkb, dvb, dgb, cs, sn):
        dq = _rotate(dqb, cs, sn, RET_HEADS, -1.0)
        dk = _rotate(dkb, cs, sn, RET_HEADS, -1.0) * (RET_QK ** -0.5)
        return jnp.concatenate([dq, dk, dvb.astype(F32), dgb.astype(F32)], axis=1)

    dproj = rowwise("rope_bwd", rope_bwd, [dq_rot, dk_rot, dv_ret, dgate, cos, sin], [], [(6 * D, BF16)])[0]
    gw_ret_in = mm_tn("ret_proj_dw", h1_0, dproj, N_DEV, BF16)
    scat_c, token_c = scatter_start("c", [gw_ret_in])
    dh1_0 = mm_nt("ret_proj_dx", dproj, W_ret_in, (BF16,), after=token_c)[0]

    def x0_bwd(xb, dhb, dresb, ada, bias, gain):
        s = _ada_slices(ada, bias)
        dx, dgain, dsc, dsh = _vjp(_modulate, (xb, gain[0:1], s[1], s[0]), dhb.astype(F32))
        return dx + dresb, dgain, dsc, dsh

    grad_x, dgain_mix0, dsc1_0, dsh1_0 = rowwise("x0_bwd", x0_bwd, [x0, dh1_0, dx1],
                                                 [ada_raw[0], b_ada[0:1], norm_mix_gain], [(D, F32)], [vec] * 3)

    small = jnp.concatenate([
        dsh1_0, dsc1_0, dg1_0, dsh2_0, dsc2_0, dg2_0,
        dsh1_1, dsc1_1, dg1_1, dsh2_1, dsc2_1, dg2_1,
        dkv_shift, dkv_scale,
        dgain_mix0, dgain_mix1, dgain_mlp0, dgain_mlp1, dgain_kv,
        dret_gain,
        dq_gain.reshape(FOX_HEADS, FOX_DH).sum(axis=0).reshape(1, FOX_DH),
        dk_gain.reshape(FOX_HEADS, FOX_DH).sum(axis=0).reshape(1, FOX_DH),
        dfb,
        loss_acc,
    ], axis=1)
    small_all = all_gather("gather_small", [small.reshape(-1, LANES)])[0].reshape(N_DEV, 1, -1)
    loss = jnp.sum(small_all[:, 0, -1])
    scat_c, token_c = scatter_relay("c", scat_c, grad_x, start_after=(small_all,))
    o_ada = 14 * D
    d_ada = small_all[:, 0, :o_ada]
    d_cat = jnp.concatenate([
        lax.dynamic_slice_in_dim(d_ada[:, 0:6 * D], me * n_ada, n_ada, axis=1),
        lax.dynamic_slice_in_dim(d_ada[:, 6 * D:12 * D], me * n_ada, n_ada, axis=1),
        lax.dynamic_slice_in_dim(d_ada[:, 12 * D:14 * D], me * n_kvada, n_kvada, axis=1)], axis=1)
    gw_ada_cat = mm_tn("ada_dw", c_act, d_cat, 1, F32, after=token_c)[0]

    results = {}

    def update(name, parts, w, m, v, layers=1):
        per_layer = parts if layers > 1 else [parts]
        shape = w.shape
        C = shape[-1]
        R = int(np.prod(shape)) // (layers * C)
        per_layer = [p if isinstance(p, list) else [(p, p.shape[0])] for p in per_layer]
        per_layer = [[(a.reshape(a.shape[0], R, C), n) for a, n in p] for p in per_layer]
        outs = adamw("adamw_" + name, per_layer, w.reshape(layers, R, C), m.reshape(layers, R, C), v.reshape(layers, R, C))
        results[name] = tuple(t.reshape(shape) for t in outs)

    def small_parts(lo, width):
        return small_all[:, :, lo:lo + width]

    update("norm_mix_gain", jnp.concatenate([small_parts(o_ada, D), small_parts(o_ada + D, D)], axis=1),
           norm_mix_gain, m_norm_mix_gain, v_norm_mix_gain)
    update("norm_mlp_gain", jnp.concatenate([small_parts(o_ada + 2 * D, D), small_parts(o_ada + 3 * D, D)], axis=1),
           norm_mlp_gain, m_norm_mlp_gain, v_norm_mlp_gain)
    update("w_ada", [gw_ada_cat[None, :, :n_ada], gw_ada_cat[None, :, n_ada:2 * n_ada]], w_ada, m_w_ada, v_w_ada, layers=2)
    update("b_ada", jnp.concatenate([small_parts(0, 6 * D), small_parts(6 * D, 6 * D)], axis=1), b_ada, m_b_ada, v_b_ada)
    o_ret = o_ada + 5 * D
    n_rg = ret_norm_gain.shape[2]
    ret_gain_parts = small_parts(o_ret, 2 * D).reshape(N_DEV, RET_HEADS, RET_V)
    ret_gain_parts = lax.dynamic_slice_in_dim(ret_gain_parts, me * n_rg, n_rg, axis=2)
    update("ret_norm_gain", ret_gain_parts, ret_norm_gain, m_ret_norm_gain, v_ret_norm_gain)
    update("kv_norm_gain", small_parts(o_ada + 4 * D, D), kv_norm_gain, m_kv_norm_gain, v_kv_norm_gain)
    update("kv_w_ada", gw_ada_cat[None, :, 2 * n_ada:], kv_w_ada, m_kv_w_ada, v_kv_w_ada)
    update("kv_b_ada", small_parts(12 * D, 2 * D), kv_b_ada, m_kv_b_ada, v_kv_b_ada)
    o_q = o_ret + 2 * D
    update("forget_bias", small_parts(o_q + 2 * FOX_DH, FOX_HEADS), forget_bias, m_forget_bias, v_forget_bias)
    update("k_norm_gain", small_parts(o_q + FOX_DH, FOX_DH), k_norm_gain, m_k_norm_gain, v_k_norm_gain)
    update("q_norm_gain", small_parts(o_q, FOX_DH), q_norm_gain, m_q_norm_gain, v_q_norm_gain)

    r_mlp_in1, r_mlp_out1 = scatter_direct_finish("a1", scat_a1, results["q_norm_gain"][1])
    r_fox_out, r_fox_in, r_kv = scatter_direct_finish("a2", scat_a2, r_mlp_in1[0][0])
    r_mlp_in0, r_mlp_out0 = scatter_direct_finish("b1", scat_b1, r_kv[0][0])
    r_ret_out = scatter_direct_finish("b2", scat_b2, r_mlp_in0[0][0])[0]
    update("kv_w", r_kv, kv_w.T, m_kv_w.T, v_kv_w.T)
    results["kv_w"] = tuple(t.T for t in results["kv_w"])
    update("fox_w_in", r_fox_in, fox_w_in, m_fox_w_in, v_fox_w_in)
    update("fox_w_out", r_fox_out, fox_w_out, m_fox_w_out, v_fox_w_out)
    update("ret_w_out", r_ret_out, ret_w_out, m_ret_w_out, v_ret_w_out)
    update("w_mlp_in", [r_mlp_in0, r_mlp_in1], w_mlp_in, m_w_mlp_in, v_w_mlp_in, layers=2)
    update("w_mlp_out", [r_mlp_out0, r_mlp_out1], w_mlp_out, m_w_mlp_out, v_w_mlp_out, layers=2)
    r_ret_in = scatter_finish("c", scat_c, results["w_mlp_out"][1])[0]
    update("ret_w_in", r_ret_in, ret_w_in, m_ret_w_in, v_ret_w_in)

    order = ["norm_mix_gain", "norm_mlp_gain", "w_ada", "b_ada", "w_mlp_in", "w_mlp_out", "ret_w_in", "ret_norm_gain",
             "ret_w_out", "kv_norm_gain", "kv_w_ada", "kv_b_ada", "kv_w", "forget_bias", "k_norm_gain", "fox_w_in",
             "q_norm_gain", "fox_w_out"]
    out = [loss, grad_x.reshape(x.shape)]
    for slot in range(4):
        out += [results[n][slot] for n in order]
    return tuple(out)
```

```python
import functools

import numpy as np
import jax
import jax.numpy as jnp
from jax import lax
from jax.experimental import pallas as pl
from jax.experimental.pallas import tpu as pltpu

F32 = jnp.float32
BF16 = jnp.bfloat16

N_DEV = 8
D_MODEL = 1024
RET_HEADS = 4
RET_QK = D_MODEL // RET_HEADS
RET_V = 2 * D_MODEL // RET_HEADS
RET_CHUNK = 256
ROPE_BASE = 10000.0
FOX_HEADS = 16
FOX_DH = D_MODEL // FOX_HEADS
EPS = 1e-6
LANES = 128
KV_PAD = 2 * D_MODEL + LANES

ADAM_LR = 0.001
ADAM_B1 = 0.9
ADAM_B2 = 0.999
ADAM_EPS = 1e-08
ADAM_WD = 0.01
ADAM_STEP = 10

VMEM_LIMIT_BYTES = 56 * 1024 * 1024


def _params(sem=None):
    return pltpu.CompilerParams(dimension_semantics=sem, vmem_limit_bytes=VMEM_LIMIT_BYTES)


def _me():
    return lax.axis_index("x"), lax.axis_index("y"), lax.axis_index("c")


def _peer(k):
    x, y, c = _me()
    return (1 - x if k & 4 else x, 1 - y if k & 2 else y, 1 - c if k & 1 else c)


def _peer_index(k):
    px, py, pc = _peer(k)
    return 4 * px + 2 * py + pc


def _exchange(name, xs, scatter):
    n = len(xs)

    def body(*refs):
        x_refs, o_refs = refs[:n], refs[n:2 * n]
        send_sems, recv_sems, local_sems = refs[2 * n:]
        x, y, c = _me()
        me = 4 * x + 2 * y + c
        local = []
        for i in range(n):
            src = x_refs[i].at[me] if scatter else x_refs[i]
            cp = pltpu.make_async_copy(src, o_refs[i].at[me], local_sems.at[i])
            cp.start()
            local.append(cp)
        remote = []
        for k in range(1, N_DEV):
            for i in range(n):
                src = x_refs[i].at[_peer_index(k)] if scatter else x_refs[i]
                cp = pltpu.make_async_remote_copy(
                    src_ref=src, dst_ref=o_refs[i].at[me],
                    send_sem=send_sems.at[(k - 1) * n + i], recv_sem=recv_sems.at[(k - 1) * n + i],
                    device_id=_peer(k), device_id_type=pl.DeviceIdType.MESH)
                cp.start()
                remote.append(cp)
        for cp in remote:
            cp.wait()
        for cp in local:
            cp.wait()

    out_shape = [jax.ShapeDtypeStruct(x.shape if scatter else (N_DEV,) + x.shape, x.dtype) for x in xs]
    any_spec = pl.BlockSpec(memory_space=pl.ANY)
    return pl.pallas_call(
        body, name=name, out_shape=out_shape,
        in_specs=[any_spec] * n, out_specs=[any_spec] * n,
        scratch_shapes=[pltpu.SemaphoreType.DMA(((N_DEV - 1) * n,)),
                        pltpu.SemaphoreType.DMA(((N_DEV - 1) * n,)),
                        pltpu.SemaphoreType.DMA((n,))],
    )(*xs)


def all_gather(name, xs):
    return _exchange(name, xs, scatter=False)


def all_to_all(name, xs):
    return _exchange(name, xs, scatter=True)


_HBM = pl.BlockSpec(memory_space=pltpu.HBM)
_SEM = pl.BlockSpec(memory_space=pltpu.SEMAPHORE)
_ANY = pl.BlockSpec(memory_space=pl.ANY)
_EFFECT = pltpu.SideEffectType.DATAFLOW_SIDE_EFFECTING

SIBLING = 1
CHIP_PEERS = (2, 4, 6)


def _my_index():
    x, y, c = _me()
    return 4 * x + 2 * y + c


def _slot(land, j):
    if len(land.shape) == 2:
        n = land.shape[1] // N_DEV
        return land.at[:, pl.ds(pl.multiple_of(j * n, LANES), n)]
    return land.at[j]


def plan_gather(x, land, me):
    return [(x, _slot(land, me), k) for k in (SIBLING,) + CHIP_PEERS]


def plan_gather_direct(x, land, me):
    return [(x, _slot(land, me), k) for k in range(1, N_DEV)]


def plan_scatter_direct(x, land, me):
    return [(x.at[me ^ k], land.at[me], k) for k in range(1, N_DEV)]


def plan_forward(x, land, me):
    return [(_slot(x, me ^ k), _slot(land, me ^ k), SIBLING) for k in CHIP_PEERS]


def plan_to_sibling(x, land, me):
    return [(x.at[me ^ SIBLING ^ (2 * q)], land.at[q], SIBLING) for q in range(4)]


def plan_to_owners(x, land, me):
    return [(x.at[q], land.at[q - 1], 2 * q) for q in (1, 2, 3)]


def own_gather(x, land, me):
    return x, _slot(land, me)


def own_scatter(x, land, me):
    return x.at[me], land.at[me]


OWN_COPY = {plan_gather: own_gather, plan_gather_direct: own_gather, plan_scatter_direct: own_scatter}
N_COPIES = {plan_gather: 4, plan_gather_direct: 7, plan_scatter_direct: 7, plan_forward: 3, plan_to_sibling: 4,
            plan_to_owners: 3}


def _plans(plan, n):
    return list(plan) if isinstance(plan, (list, tuple)) else [plan] * n


def _own_copies(plan, x_refs, land_refs, own_sems):
    me = _my_index()
    plans = _plans(plan, len(land_refs))
    with_own = [i for i, p in enumerate(plans) if p in OWN_COPY]
    return [pltpu.make_async_copy(*OWN_COPY[plans[i]](x_refs[i], land_refs[i], me), own_sems[s])
            for s, i in enumerate(with_own)]


def _plan_copies(plan, x_refs, land_refs, send_sems, recv_sems):
    me = _my_index()
    plans = _plans(plan, len(land_refs))
    return [pltpu.make_async_remote_copy(src_ref=src, dst_ref=dst, send_sem=send_sems[i].at[s], recv_sem=recv_sems[i].at[s],
                                         device_id=_peer(k), device_id_type=pl.DeviceIdType.MESH)
            for i in range(len(land_refs)) for s, (src, dst, k) in enumerate(plans[i](x_refs[i], land_refs[i], me))]


def exchange_start(name, xs, lands, plan, after=()):
    n, m = len(lands), len(after)
    n_x = 0 if xs is None else n
    arrays = ([] if xs is None else list(xs)) + list(lands)
    n_a = len(arrays)
    n_own = sum(p in OWN_COPY for p in _plans(plan, n))
    n_s = 2 * n + n_own

    def body(*refs):
        land_refs = refs[n_x:n_a]
        x_refs = land_refs if xs is None else refs[:n]
        sems = refs[n_a + m:n_a + m + n_s]
        token = refs[-1]
        for cp in _own_copies(plan, x_refs, land_refs, sems[2 * n:]):
            cp.start()
        for cp in _plan_copies(plan, x_refs, land_refs, sems[:n], sems[n:2 * n]):
            cp.start()
        token[...] = jnp.zeros_like(token)

    sems = [pltpu.SemaphoreType.DMA((N_COPIES[p],)) for p in _plans(plan, n)] * 2 + [pltpu.SemaphoreType.DMA(())] * n_own
    res = pl.pallas_call(
        body, name=name,
        out_shape=sems + [pltpu.HBM(a.shape, a.dtype) for a in arrays] + [jax.ShapeDtypeStruct((8, LANES), F32)],
        in_specs=[_HBM] * n_a + [_ANY] * m,
        out_specs=[_SEM] * n_s + [_HBM] * n_a + [pl.BlockSpec(memory_space=pltpu.VMEM)],
        input_output_aliases={i: n_s + i for i in range(n_a)},
        compiler_params=pltpu.CompilerParams(has_side_effects=_EFFECT),
    )(*[pltpu.with_memory_space_constraint(a, pltpu.HBM) for a in arrays], *after)
    thru = res[n_s:n_s + n_a]
    own = iter(res[2 * n:n_s])
    handles = [(res[i], res[n + i], None if xs is None else thru[i], thru[n_x + i],
                next(own) if p in OWN_COPY else None) for i, p in enumerate(_plans(plan, n))]
    return handles, res[-1]


def exchange_wait(name, handles, plan, after):
    n = len(handles)
    in_place = handles[0][2] is None
    arrays = ([] if in_place else [h[2] for h in handles]) + [h[3] for h in handles]
    n_a = len(arrays)
    own_sems = [h[4] for h in handles if h[4] is not None]
    n_s = 2 * n + len(own_sems)

    def body(*refs):
        land_refs = refs[n_a - n:n_a]
        x_refs = land_refs if in_place else refs[:n]
        sems = refs[n_a:n_a + n_s]
        for cp in _own_copies(plan, x_refs, land_refs, sems[2 * n:]):
            cp.wait()
        for cp in _plan_copies(plan, x_refs, land_refs, sems[:n], sems[n:2 * n]):
            cp.wait_send()
            cp.wait_recv()

    res = pl.pallas_call(
        body, name=name,
        out_shape=[pltpu.HBM(a.shape, a.dtype) for a in arrays],
        in_specs=[_HBM] * n_a + [_SEM] * n_s + [_ANY],
        out_specs=[_HBM] * n_a,
        input_output_aliases={i: i for i in range(n_a)},
        compiler_params=pltpu.CompilerParams(has_side_effects=_EFFECT),
    )(*arrays, *[h[0] for h in handles], *[h[1] for h in handles], *own_sems, after)
    return (None if in_place else res[:n]), res[n_a - n:]


def chip_sum(name, gw, from_sibling):
    _, R, C = gw.shape
    tr = _row_tile(R, 512)
    me = _my_index().astype(jnp.int32).reshape(1)

    def body(me_ref, g_ref, s_ref, o_ref):
        o_ref[...] = (g_ref[...].astype(F32) + s_ref[...].astype(F32)).astype(o_ref.dtype)

    slot = pl.BlockSpec((None, tr, C), lambda q, i, me_ref: (q, i, 0))
    return pl.pallas_call(
        body, name=name, out_shape=jax.ShapeDtypeStruct((4, R, C), BF16),
        grid_spec=pltpu.PrefetchScalarGridSpec(
            num_scalar_prefetch=1, grid=(4, R // tr),
            in_specs=[pl.BlockSpec((None, tr, C), lambda q, i, me_ref: (me_ref[0] ^ (2 * q), i, 0)), slot],
            out_specs=slot),
        compiler_params=_params(("arbitrary", "arbitrary")),
    )(me, gw, from_sibling)


def _tile(n, cap):
    best = None
    for t in range(LANES, min(n, cap) + 1, LANES):
        if n % t == 0:
            best = t
    if best is None or (best < 256 and n <= 2304):
        return n
    return best


def _row_tile(m, cap):
    if m <= cap:
        return m
    t = cap
    while m % t:
        t //= 2
    return t if t >= 256 else m


def _after_spec(after):
    return [] if after is None else [pl.BlockSpec(memory_space=pl.ANY)]


def _after_arg(after):
    return [] if after is None else [after]


def _mm_call(name, dims, grid, a_spec, b_spec, o_spec, o_shape, tile, a, b, out_dtypes, epilogue, extra, after, vecs=()):
    nk = grid[2]
    n_x, n_o = len(extra) + len(vecs), len(out_dtypes)

    def body(a_ref, b_ref, *refs):
        x_refs, o_refs = refs[:n_x], refs[len(refs) - n_s - n_o:len(refs) - n_s]
        part = lax.dot_general(a_ref[...].astype(BF16), b_ref[...].astype(BF16), (dims, ((), ())),
                               preferred_element_type=F32)

        def finish(acc):
            vals = (acc,) if epilogue is None else epilogue(acc, *[x[...] for x in x_refs])
            for o_ref, val in zip(o_refs, vals):
                o_ref[...] = val.astype(o_ref.dtype)

        if nk == 1:
            finish(part)
        else:
            acc_ref = refs[-1]
            k = pl.program_id(2)

            @pl.when(k == 0)
            def _():
                acc_ref[...] = part

            @pl.when(jnp.logical_and(k > 0, k < nk - 1))
            def _():
                acc_ref[...] += part

            @pl.when(k == nk - 1)
            def _():
                finish(acc_ref[...] + part)

    n_s = 0 if nk == 1 else 1
    return pl.pallas_call(
        body, name=name, out_shape=[jax.ShapeDtypeStruct(o_shape, dt) for dt in out_dtypes], grid=grid,
        in_specs=[a_spec, b_spec] + [o_spec] * len(extra) + [pl.BlockSpec(v.shape, lambda i, j, k: (0, 0)) for v in vecs]
        + _after_spec(after), out_specs=[o_spec] * n_o,
        scratch_shapes=[pltpu.VMEM(tile, F32)] * n_s,
        compiler_params=_params(("parallel", "parallel", "arbitrary")),
    )(a, b, *extra, *vecs, *_after_arg(after))


def mm_nn(name, a, w, out_dtypes=(F32,), epilogue=None, extra=(), after=None, vecs=(), tm_cap=2048):
    M, K = a.shape
    G, _, n = w.shape
    tk = _row_tile(K, 2048)
    tn = _tile(n, 512 if (K == tk and n >= 2048) else 1024)
    tm = _row_tile(M, 512 if tn > 1024 else (min(1024, tm_cap) if tk > 1024 else tm_cap))
    r = n // tn
    return _mm_call(
        name, ((1,), (0,)), (M // tm, G * r, K // tk),
        pl.BlockSpec((tm, tk), lambda i, j, k: (i, k)),
        pl.BlockSpec((None, tk, tn), lambda i, j, k: (j // r, k, j % r)),
        pl.BlockSpec((tm, tn), lambda i, j, k: (i, j)), (M, G * n), (tm, tn),
        a, w, out_dtypes, epilogue, extra, after, vecs)


def mm_nt(name, dy, w, out_dtypes=(F32,), epilogue=None, extra=(), after=None):
    M, N = dy.shape
    G, K, n = w.shape
    tn = _tile(n, 3072)
    r = n // tn
    tk = _row_tile(K, 512 if (G * r == 1 and K >= 2048 and K % 512 == 0) else 1024)
    tm = _row_tile(M, 512 if tk > 1024 else (1024 if tn > 2048 else 2048))
    return _mm_call(
        name, ((1,), (1,)), (M // tm, K // tk, G * r),
        pl.BlockSpec((tm, tn), lambda i, j, k: (i, k)),
        pl.BlockSpec((None, tk, tn), lambda i, j, k: (k // r, j, k % r)),
        pl.BlockSpec((tm, tk), lambda i, j, k: (i, j)), (M, K), (tm, tk),
        dy, w, out_dtypes, epilogue, extra, after)


def mm_tn(name, a, dy, G, out_dtype=F32, after=None):
    M, K = a.shape
    n = dy.shape[1] // G
    tn = _tile(n, 1024)
    tk = _row_tile(K, 512 if tn > 1024 else 1024)
    tm = _row_tile(M, 1024 if tk > 1024 else 2048)
    r = n // tn
    return _mm_call(
        name, ((0,), (0,)), (K // tk, G * r, M // tm),
        pl.BlockSpec((tm, tk), lambda i, j, k: (k, i)),
        pl.BlockSpec((tm, tn), lambda i, j, k: (k, j)),
        pl.BlockSpec((None, tk, tn), lambda i, j, k: (j // r, i, j % r)), (G, K, n), (tk, tn),
        a, dy, (out_dtype,), None, (), after)[0]


ROW_BLOCK_BYTES = 12 * 1024 * 1024


def rowwise(name, fn, rows, vecs, outs, accs=(), after=None):
    rows = [r if isinstance(r, tuple) else (r, r.shape[1], 0) for r in rows]
    rows = [r if len(r) == 4 else r + (None,) for r in rows]
    n_fn = len(rows) + len(vecs)
    vecs = list(vecs) + _after_arg(after)
    S = rows[0][0].shape[-2]
    row_bytes = sum(w * a.dtype.itemsize for a, w, _, _ in rows) + sum(w * jnp.dtype(dt).itemsize for w, dt in outs)
    tm = _row_tile(S, 1024)
    while tm > 256 and tm * row_bytes > ROW_BLOCK_BYTES:
        tm //= 2
    n_r, n_v, n_o, n_a = len(rows), len(vecs), len(outs), len(accs)

    def body(*refs):
        ins = [ref[...] for ref in refs[:n_r + n_v]]
        o_refs = refs[n_r + n_v:n_r + n_v + n_o]
        a_refs = refs[n_r + n_v + n_o:]
        res = fn(*ins[:n_fn])
        res = res if isinstance(res, (tuple, list)) else (res,)
        for ref, val in zip(o_refs, res[:n_o]):
            ref[...] = val.astype(ref.dtype)
        if n_a:
            @pl.when(pl.program_id(0) == 0)
            def _():
                for ref in a_refs:
                    ref[...] = jnp.zeros_like(ref)
            for ref, val in zip(a_refs, res[n_o:]):
                ref[...] += val

    in_specs = [pl.BlockSpec((tm, w), functools.partial(lambda cb, i: (i, cb), cb)) if slab is None else
                pl.BlockSpec((None, tm, w), functools.partial(lambda cb, slab, i: (slab, i, cb), cb, slab))
                for _, w, cb, slab in rows]
    in_specs += [pl.BlockSpec(v.shape, lambda i: (0, 0)) for v in vecs]
    out_specs = [pl.BlockSpec((tm, w), lambda i: (i, 0)) for w, _ in outs]
    out_specs += [pl.BlockSpec(a, lambda i: (0, 0)) for a in accs]
    out_shape = [jax.ShapeDtypeStruct((S, w), dt) for w, dt in outs]
    out_shape += [jax.ShapeDtypeStruct(a, F32) for a in accs]
    res = pl.pallas_call(
        body, name=name, out_shape=out_shape, grid=(S // tm,),
        in_specs=in_specs, out_specs=out_specs,
        compiler_params=_params(("arbitrary",)),
    )(*[r[0] for r in rows], *vecs)
    return res


def _rms(x):
    return x * lax.rsqrt(jnp.mean(x * x, axis=-1, keepdims=True) + EPS)


def _modulate(x, gain, scale, shift):
    return _rms(x) * gain * (1.0 + scale) + shift


def _ada_slices(ada_raw, bias):
    ada = ada_raw + bias
    return [ada[:, i * D_MODEL:(i + 1) * D_MODEL] for i in range(ada.shape[1] // D_MODEL)]


def _norm_wide_heads(y, gain, heads):
    w = y.shape[1] // heads
    return jnp.concatenate([_rms(y[:, h * w:(h + 1) * w]) * gain[:, h * w:(h + 1) * w] for h in range(heads)], axis=1)


def _norm_fox_heads(x, gain):
    outs = []
    for p in range(x.shape[1] // LANES):
        blk = x[:, p * LANES:(p + 1) * LANES]
        low = lax.broadcasted_iota(jnp.int32, blk.shape, 1) < FOX_DH
        sq = blk * blk
        ss_low = jnp.sum(jnp.where(low, sq, 0.0), axis=1, keepdims=True)
        ss_high = jnp.sum(jnp.where(low, 0.0, sq), axis=1, keepdims=True)
        outs.append(blk * lax.rsqrt(jnp.where(low, ss_low, ss_high) * (1.0 / FOX_DH) + EPS))
    return jnp.concatenate(outs, axis=1) * gain


def _silu(x):
    return x * jax.nn.sigmoid(x)


def _log_sigmoid(z):
    return -(jnp.maximum(-z, 0.0) + jnp.log(1.0 + jnp.exp(-jnp.abs(z))))


def _rotate(x, cos, sin, heads, sign):
    w = x.shape[1] // heads
    half = w // 2
    outs = []
    for h in range(heads):
        x1 = x[:, h * w:h * w + half]
        x2 = x[:, h * w + half:(h + 1) * w]
        outs += [x1 * cos - sign * x2 * sin, sign * x1 * sin + x2 * cos]
    return jnp.concatenate(outs, axis=1)


def _vjp(fn, primals, ct):
    return jax.vjp(fn, *primals)[1](ct)


_LOG_GAMMAS = [float(np.log(np.float32(1.0) - np.float32(2.0) ** np.float32(-5.0 - h))) for h in range(RET_HEADS)]


RET_ROWS = 1024


def retention(name, jobs, reverse, out_dtypes):
    operands = []
    for job in jobs:
        for op in job:
            if not any(op[0] is o[0] and op[1:] == o[1:] for o in operands):
                operands.append(op)
    index = [[next(i for i, o in enumerate(operands) if op[0] is o[0] and op[1:] == o[1:]) for op in job] for job in jobs]
    S = operands[0][0].shape[0]
    C = RET_CHUNK
    rows = _row_tile(S, RET_ROWS)
    nb = S // rows
    n_in, n_job = len(operands), len(jobs)

    def body(*refs):
        in_refs, o_refs, states = refs[:n_in], refs[n_in:n_in + n_job], refs[n_in + n_job:]
        h = pl.program_id(0)

        @pl.when(pl.program_id(1) == 0)
        def _():
            for state in states:
                state[...] = jnp.zeros_like(state)

        log_g = jnp.float32(_LOG_GAMMAS[RET_HEADS - 1])
        for i in range(RET_HEADS - 2, -1, -1):
            log_g = jnp.where(h == i, jnp.float32(_LOG_GAMMAS[i]), log_g)
        row = lax.broadcasted_iota(jnp.int32, (C, C), 0)
        col = lax.broadcasted_iota(jnp.int32, (C, C), 1)
        rel = (col - row if reverse else row - col).astype(F32)
        decay = jnp.where(rel >= 0, jnp.exp(log_g * jnp.maximum(rel, 0.0)), 0.0)
        j = lax.broadcasted_iota(jnp.int32, (C, 1), 0).astype(F32)
        q_decay = jnp.exp(log_g * (C - j if reverse else j + 1.0))
        k_decay = jnp.exp(log_g * (j if reverse else C - 1.0 - j))
        chunk_decay = jnp.exp(jnp.full((1, 1), log_g * C, F32))

        chunks = range(rows // C)
        for ci in (reversed(chunks) if reverse else chunks):
            rs = slice(ci * C, (ci + 1) * C)
            for (qi, ki, vi), o_ref, state in zip(index, o_refs, states):
                qc = in_refs[qi][rs, :].astype(BF16)
                kf = in_refs[ki][rs, :].astype(F32)
                vc = in_refs[vi][rs, :].astype(BF16)
                scores = lax.dot_general(qc, kf.astype(BF16), (((1,), (1,)), ((), ())), preferred_element_type=F32) * decay
                intra = jnp.dot(scores.astype(BF16), vc, preferred_element_type=F32)
                cross = jnp.dot(qc, state[...].astype(BF16), preferred_element_type=F32) * q_decay
                o_ref[rs, :] = (intra + cross).astype(o_ref.dtype)
                upd = lax.dot_general((kf * k_decay).astype(BF16), vc, (((0,), (0,)), ((), ())), preferred_element_type=F32)
                state[...] = state[...] * chunk_decay + upd

    def block(i):
        return nb - 1 - i if reverse else i

    widths = [(q[1], v[1]) for q, _, v in jobs]
    return pl.pallas_call(
        body, name=name,
        out_shape=[jax.ShapeDtypeStruct((S, RET_HEADS * dv), dt) for (_, dv), dt in zip(widths, out_dtypes)],
        grid=(RET_HEADS, nb),
        in_specs=[pl.BlockSpec((rows, w), functools.partial(lambda off, h, i: (block(i), off + h), off))
                  for _, w, off in operands],
        out_specs=[pl.BlockSpec((rows, dv), lambda h, i: (block(i), h)) for _, dv in widths],
        scratch_shapes=[pltpu.VMEM((dk, dv), F32) for dk, dv in widths],
        compiler_params=_params(("parallel", "arbitrary")),
    )(*[o[0] for o in operands])


FOX_T = 512
N_PAIR = FOX_HEADS // 2
FOX_SCALE = FOX_DH ** -0.5


def _fox_heads(q2):
    low = lax.broadcasted_iota(jnp.int32, (1, LANES), 1) < FOX_DH
    return [(mask, jnp.where(mask, q2 * FOX_SCALE, 0.0).astype(BF16)) for mask in (low, jnp.logical_not(low))]


def _fox_parts(j, t):
    return ([(0, j * t, False)] if j else []) + [(j * t, (j + 1) * t, True)]


def _fox_scores(qa, k_ref, ft_ref, head, lo, hi, diagonal):
    k_blk = k_ref[lo:hi, :].astype(BF16)
    s = lax.dot_general(qa, k_blk, (((1,), (1,)), ((), ())), preferred_element_type=F32) - ft_ref[pl.ds(head, 1), lo:hi]
    if diagonal:
        n = hi - lo
        s = jnp.where(lax.broadcasted_iota(jnp.int32, (n, n), 1) <= lax.broadcasted_iota(jnp.int32, (n, n), 0), s, -jnp.inf)
    return s


def fox_forward(name, qn, kn, v, f_cum_t):
    S = qn.shape[0]
    t = _row_tile(S, FOX_T)

    col_v = pl.BlockSpec((S, LANES), lambda p, i: (0, p))

    def variant(j, pair, q_ref, k_ref, v_ref, ft_ref, y_ref, lse_ref):
        ys, lses = [], []
        for a, (mask, qa) in enumerate(_fox_heads(q_ref[...])):
            parts = [(lo, hi, _fox_scores(qa, k_ref, ft_ref, 2 * pair + a, lo, hi, dg)) for lo, hi, dg in _fox_parts(j, t)]
            m = functools.reduce(jnp.maximum, [jnp.max(s, axis=1, keepdims=True) for _, _, s in parts])
            l, acc = 0.0, 0.0
            for lo, hi, s in parts:
                e = jnp.exp(s - m)
                l = l + jnp.sum(e, axis=1, keepdims=True)
                acc = acc + jnp.dot(e.astype(BF16), v_ref[lo:hi, :].astype(BF16), preferred_element_type=F32)
            ys.append(acc / l)
            lses.append(m + jnp.log(l))
        low = lax.broadcasted_iota(jnp.int32, (1, LANES), 1) < FOX_DH
        y_ref[...] = jnp.where(low, ys[0], ys[1])
        lse_ref[...] = jnp.where(low, lses[0], lses[1])

    def body(*refs):
        pair, i = pl.program_id(0), pl.program_id(1)
        for j in range(S // t):
            pl.when(i == j)(functools.partial(variant, j, pair, *refs))

    return pl.pallas_call(
        body, name=name,
        out_shape=[jax.ShapeDtypeStruct((S, D_MODEL), F32), jax.ShapeDtypeStruct((S, D_MODEL), F32)],
        grid=(N_PAIR, S // t),
        in_specs=[pl.BlockSpec((t, LANES), lambda p, i: (i, p)),
                  pl.BlockSpec((S, LANES), lambda p, i: (0, p)),
                  col_v,
                  pl.BlockSpec((LANES, S), lambda p, i: (0, 0))],
        out_specs=[pl.BlockSpec((t, LANES), lambda p, i: (i, p)),
                   pl.BlockSpec((t, LANES), lambda p, i: (i, p))],
        compiler_params=_params(("parallel", "arbitrary")),
    )(qn, kn, v, f_cum_t)


def fox_backward(name, qn, kn, v, f_cum_t, y, dy, lse):
    S = qn.shape[0]
    t = _row_tile(S, FOX_T)

    def variant(j, pair, q_ref, k_ref, v_ref, ft_ref, y_ref, dy_ref, lse_ref, dq_ref, dk_ref, dv_ref, dfq_ref, dfk_ref,
                dv_acc):
        y2, dy2, lse2 = y_ref[...], dy_ref[...], lse_ref[...]
        lane = lax.broadcasted_iota(jnp.int32, (t, LANES), 1)
        dqs, dfq = [], jnp.zeros((t, LANES), F32)
        for a, (mask, qa) in enumerate(_fox_heads(q_ref[...].astype(F32))):
            lse_a = jnp.max(jnp.where(mask, lse2, -jnp.inf), axis=1, keepdims=True)
            dy_a = jnp.where(mask, dy2, 0.0)
            delta = jnp.sum(dy_a * y2, axis=1, keepdims=True)
            dy_b = dy_a.astype(BF16)
            dq, row_sum = 0.0, 0.0
            for lo, hi, dg in _fox_parts(j, t):
                p = jnp.exp(_fox_scores(qa, k_ref, ft_ref, 2 * pair + a, lo, hi, dg) - lse_a)
                dp = lax.dot_general(dy_b, v_ref[lo:hi, :].astype(BF16), (((1,), (1,)), ((), ())), preferred_element_type=F32)
                ds = p * (dp - delta)
                row_sum = row_sum + jnp.sum(ds, axis=1, keepdims=True)
                dfk_ref[pl.ds(a, 1), lo:hi] += -jnp.sum(ds, axis=0, keepdims=True)
                ds_b = ds.astype(BF16)
                dq = dq + jnp.dot(ds_b, k_ref[lo:hi, :].astype(BF16), preferred_element_type=F32)
                dk_ref[lo:hi, :] += lax.dot_general(ds_b, qa, (((0,), (0,)), ((), ())), preferred_element_type=F32)
                dv_acc[lo:hi, :] += lax.dot_general(p.astype(BF16), dy_b, (((0,), (0,)), ((), ())), preferred_element_type=F32)
            dqs.append(dq * FOX_SCALE)
            dfq = dfq + jnp.where(lane == 2 * pair + a, row_sum, 0.0)
        low = lax.broadcasted_iota(jnp.int32, (1, LANES), 1) < FOX_DH
        dq_ref[...] = jnp.where(low, dqs[0], dqs[1])
        dfq_ref[...] = dfq

    def body(*refs):
        pair, i = pl.program_id(0), pl.program_id(1)
        dk_ref, dv_ref, _, dfk_ref, dv_acc = refs[8:13]

        @pl.when(i == 0)
        def _():
            dk_ref[...] = jnp.zeros_like(dk_ref)
            dv_acc[...] = jnp.zeros_like(dv_acc)
            dfk_ref[...] = jnp.zeros_like(dfk_ref)

        for j in range(S // t):
            pl.when(i == j)(functools.partial(variant, j, pair, *refs))

        @pl.when(i == S // t - 1)
        def _():
            dv_ref[...] = dv_acc[...].astype(dv_ref.dtype)

    row_blk = pl.BlockSpec((t, LANES), lambda p, i: (i, p))
    col_blk = col_v = pl.BlockSpec((S, LANES), lambda p, i: (0, p))
    return pl.pallas_call(
        body, name=name,
        out_shape=[jax.ShapeDtypeStruct((S, D_MODEL), F32)] * 2
        + [jax.ShapeDtypeStruct((S, D_MODEL), BF16), jax.ShapeDtypeStruct((N_PAIR, S, LANES), F32),
           jax.ShapeDtypeStruct((N_PAIR, 8, S), F32)],
        grid=(N_PAIR, S // t),
        in_specs=[row_blk, col_blk,
                  col_v,
                  pl.BlockSpec((LANES, S), lambda p, i: (0, 0)),
                  row_blk, row_blk, row_blk],
        out_specs=[row_blk, col_blk, col_blk,
                   pl.BlockSpec((None, t, LANES), lambda p, i: (p, i, 0)),
                   pl.BlockSpec((None, 8, S), lambda p, i: (p, 0, 0))],
        scratch_shapes=[pltpu.VMEM((S, LANES), F32)],
        compiler_params=_params(("parallel", "arbitrary")),
    )(qn, kn, v, f_cum_t, y, dy, lse)


def cumsum_rows(name, x, reverse):
    S = x.shape[0]
    C = LANES
    nc = S // C

    def body(x_ref, o_ref):
        row = lax.broadcasted_iota(jnp.int32, (C, C), 0)
        col = lax.broadcasted_iota(jnp.int32, (C, C), 1)
        tri = jnp.where(col >= row if reverse else col <= row, 1.0, 0.0).astype(F32)
        carry = jnp.zeros((1, LANES), F32)
        for i in (range(nc - 1, -1, -1) if reverse else range(nc)):
            blk = x_ref[i * C:(i + 1) * C, :]
            loc = jnp.dot(tri, blk, preferred_element_type=F32, precision=lax.Precision.HIGHEST)
            o_ref[i * C:(i + 1) * C, :] = loc + carry
            carry = carry + (loc[0:1, :] if reverse else loc[C - 1:C, :])

    return pl.pallas_call(body, name=name, out_shape=jax.ShapeDtypeStruct((S, LANES), F32),
                          compiler_params=_params())(x)


def adamw(name, parts, w, m, v):
    L, R, C = w.shape
    row_bytes = C * (7 * 4 + max(sum(n * a.dtype.itemsize for a, n in p) for p in parts))
    tr = _row_tile(R, 1024)
    while tr > 256 and tr % 2 == 0 and R % (tr // 2) == 0 and tr * row_bytes > ROW_BLOCK_BYTES:
        tr //= 2
    nr = R // tr
    counts = [len(p) for p in parts]

    def body(*refs):
        w_ref, m_ref, v_ref, g_out, d_out, m_out, v_out = refs[sum(counts):]
        for layer in range(L):
            p_refs = refs[sum(counts[:layer]):sum(counts[:layer + 1])]

            @pl.when(pl.program_id(0) == layer)
            def _(p_refs=p_refs, slots=[n for _, n in parts[layer]]):
                g = None
                for p_ref, n in zip(p_refs, slots):
                    for i in range(n):
                        g = p_ref[i].astype(F32) if g is None else g + p_ref[i].astype(F32)
                m2 = ADAM_B1 * m_ref[...] + (1.0 - ADAM_B1) * g
                v2 = ADAM_B2 * v_ref[...] + (1.0 - ADAM_B2) * jnp.square(g)
                m_hat = m2 / (1.0 - ADAM_B1 ** ADAM_STEP)
                v_hat = v2 / (1.0 - ADAM_B2 ** ADAM_STEP)
                g_out[...] = g
                d_out[...] = -ADAM_LR * (m_hat / (jnp.sqrt(v_hat) + ADAM_EPS) + ADAM_WD * w_ref[...])
                m_out[...] = m2
                v_out[...] = v2

    def part_spec(layer, n):
        return pl.BlockSpec((n, tr, C), lambda l, i: (0, jnp.where(l == layer, i, jnp.where(l < layer, 0, nr - 1)), 0))

    blk = pl.BlockSpec((None, tr, C), lambda l, i: (l, i, 0))
    return pl.pallas_call(
        body, name=name, out_shape=[jax.ShapeDtypeStruct((L, R, C), F32)] * 4, grid=(L, nr),
        in_specs=[part_spec(layer, n) for layer in range(L) for _, n in parts[layer]] + [blk, blk, blk],
        out_specs=[blk] * 4, compiler_params=_params(("arbitrary", "arbitrary")),
    )(*[a for layer in parts for a, _ in layer], w, m, v)


def kernel(x, c, positions, norm_mix_gain, norm_mlp_gain, w_ada, b_ada, w_mlp_in, w_mlp_out, ret_w_in, ret_norm_gain, ret_w_out, kv_norm_gain, kv_w_ada, kv_b_ada, kv_w, forget_bias, k_norm_gain, fox_w_in, q_norm_gain, fox_w_out, loss_target, m_norm_mix_gain, m_norm_mlp_gain, m_w_ada, m_b_ada, m_w_mlp_in, m_w_mlp_out, m_ret_w_in, m_ret_norm_gain, m_ret_w_out, m_kv_norm_gain, m_kv_w_ada, m_kv_b_ada, m_kv_w, m_forget_bias, m_k_norm_gain, m_fox_w_in, m_q_norm_gain, m_fox_w_out, v_norm_mix_gain, v_norm_mlp_gain, v_w_ada, v_b_ada, v_w_mlp_in, v_w_mlp_out, v_ret_w_in, v_ret_norm_gain, v_ret_w_out, v_kv_norm_gain, v_kv_w_ada, v_kv_b_ada, v_kv_w, v_forget_bias, v_k_norm_gain, v_fox_w_in, v_q_norm_gain, v_fox_w_out):
    D = D_MODEL
    S = x.shape[1]
    x0 = x.reshape(S, D)
    target = loss_target.reshape(S, D)
    me = 4 * lax.axis_index("x") + 2 * lax.axis_index("y") + lax.axis_index("c")
    n_ada = w_ada.shape[2]
    n_kvada = kv_w_ada.shape[1]
    n_kv = kv_w.shape[1]

    c_all, ret_gain = all_gather("gather_c", [c.reshape(D // LANES, LANES), ret_norm_gain.reshape(RET_HEADS, -1)])
    ret_gain = jnp.transpose(ret_gain, (1, 0, 2)).reshape(1, RET_HEADS * RET_V)
    c_act = rowwise("silu_c", _silu, [c_all.reshape(N_DEV, D)], [], [(D, F32)])[0]
    w_ada_cat = jnp.concatenate([w_ada[0], w_ada[1], kv_w_ada], axis=1).astype(BF16)[None]
    n_cat = 2 * n_ada + n_kvada
    ada_part = mm_nn("ada_proj", c_act, w_ada_cat)[0]
    ada_mine = all_to_all("ada_rows", [ada_part.reshape(N_DEV, n_cat // LANES, LANES)])[0]
    ada_mine = ada_mine.reshape(N_DEV, n_cat)
    ada_raw = [ada_mine[:, l * n_ada:(l + 1) * n_ada].reshape(1, 6 * D) for l in range(2)]
    kvada_raw = ada_mine[:, 2 * n_ada:].reshape(1, 2 * D)
    kv_bias = kv_b_ada.reshape(1, 2 * D)
    kv_gain = kv_norm_gain.reshape(1, D)
    fb = jnp.pad(forget_bias.reshape(1, FOX_HEADS), ((0, 0), (0, LANES - FOX_HEADS)))
    k_gain = jnp.tile(k_norm_gain.reshape(1, FOX_DH), (1, FOX_HEADS))
    q_gain = jnp.tile(q_norm_gain.reshape(1, FOX_DH), (1, FOX_HEADS))

    w_names = ["ret_in", "ret_out", "mlp_in0", "mlp_out0", "kv", "fox_in", "fox_out", "mlp_in1", "mlp_out1"]
    shards = [ret_w_in[0].astype(BF16), ret_w_out[0].astype(BF16), w_mlp_in[0].astype(BF16), w_mlp_out[0].astype(BF16),
              kv_w.T.astype(BF16), fox_w_in[0].astype(BF16), fox_w_out[0].astype(BF16), w_mlp_in[1].astype(BF16),
              w_mlp_out[1].astype(BF16)]
    two_level = {"ret_in", "ret_out", "mlp_in0", "mlp_out0"}
    by_columns = {"ret_in", "mlp_in0", "mlp_in1", "fox_in"}
    w_plans = {name: plan_gather if name in two_level else plan_gather_direct for name in w_names}
    w_handles, token = exchange_start("gather_weights_start", shards,
                                      [lax.empty((a.shape[0], N_DEV * a.shape[1]) if name in by_columns else
                                                 (N_DEV,) + a.shape, a.dtype) for name, a in zip(w_names, shards)],
                                      [w_plans[name] for name in w_names], after=(ada_mine, ret_gain))
    w_handles = dict(zip(w_names, w_handles))

    forwards = {}

    def forward_early(name, after):
        arrived = exchange_wait("gather_wait_" + name, [w_handles[name]], w_plans[name], after)[1]
        forwards[name], tok = exchange_start("gather_forward_start_" + name, None, arrived, plan_forward)
        return tok

    def weight(name, after):
        if name in two_level:
            return exchange_wait("gather_forward_wait_" + name, forwards[name], plan_forward, after)[1][0]
        return exchange_wait("gather_wait_" + name, [w_handles[name]], w_plans[name], after)[1][0]

    pos = positions.reshape(S, 1).astype(F32)
    half = RET_QK // 2
    inv_freq = jnp.asarray((ROPE_BASE ** (-np.arange(half, dtype=np.float32) / half)).reshape(1, half), F32)

    def angles(p, f):
        ang = p * f
        return jnp.cos(ang), jnp.sin(ang)

    cos, sin = rowwise("rope_table", angles, [pos], [inv_freq], [(half, F32), (half, F32)])

    def mod_mix(layer):
        def fn(xb, ada, bias, gain):
            sh, sc = _ada_slices(ada, bias)[:2]
            return _modulate(xb, gain[layer:layer + 1], sc, sh)
        return fn

    def mod_mlp(layer):
        def fn(xb, ada, bias, gain):
            sh, sc = _ada_slices(ada, bias)[3:5]
            return _modulate(xb, gain[layer:layer + 1], sc, sh)
        return fn

    h1_0 = rowwise("mod_mix0", mod_mix(0), [x0], [ada_raw[0], b_ada[0:1], norm_mix_gain], [(D, BF16)], after=token)[0]
    W_ret_in = weight("ret_in", forward_early("ret_in", h1_0))[None]
    proj = mm_nn("ret_proj", h1_0, W_ret_in, (BF16,))[0]
    token = forward_early("ret_out", proj)

    def rope_fwd(qb, kb, cs, sn):
        return (_rotate(qb.astype(F32), cs, sn, RET_HEADS, 1.0),
                _rotate(kb.astype(F32), cs, sn, RET_HEADS, 1.0) * (RET_QK ** -0.5))

    q_rot, k_rot = rowwise("rope", rope_fwd, [(proj, D, 0), (proj, D, 1), cos, sin], [], [(D, BF16), (D, BF16)],
                           after=token)
    v_ret = (proj, RET_V, (2 * D) // RET_V)
    y_ret = retention("ret_fwd", [((q_rot, RET_QK, 0), (k_rot, RET_QK, 0), v_ret)], False, [F32])[0]

    def ret_gate(yb, gb, gain):
        return _silu(gb.astype(F32)) * _norm_wide_heads(yb, gain, RET_HEADS)

    mixin0 = rowwise("ret_gate", ret_gate, [y_ret, (proj, 2 * D, 2)], [ret_gain], [(2 * D, BF16)])[0]
    W_ret_out = weight("ret_out", mixin0).reshape(1, 2 * D, D)
    FUSED_ROWS = 512

    def residual_mod(layer, slot):
        def fn(xb, bb, ada, bias, gain):
            s = _ada_slices(ada, bias)
            xn = xb + s[2] * bb
            return xn, _modulate(xn, gain[layer:layer + 1], s[4], s[3])
        return fn

    def with_residual(fn):
        return lambda acc, xb, *vs: (acc,) + tuple(fn(xb, acc, *vs))

    mix0, x1, h2_0 = mm_nn("ret_out", mixin0, W_ret_out, (F32, F32, BF16), with_residual(residual_mod(0, 0)), (x0,),
                           forward_early("mlp_in0", W_ret_out), (ada_raw[0], b_ada[0:1], norm_mlp_gain), FUSED_ROWS)

    W_mlp_in, W_mlp_out = {}, {}

    def mlp_forward(tag, h2, layer, out_dtypes=(F32,), epilogue=None, extra=(), vecs=()):
        W_mlp_in[layer] = weight("mlp_in" + tag, h2)[None]
        early = forward_early("mlp_out" + tag, W_mlp_in[layer]) if "mlp_out" + tag in two_level else None
        u, act = mm_nn("mlp_in" + tag, h2, W_mlp_in[layer], (BF16, BF16), after=early,
                       epilogue=lambda acc: (acc, jnp.square(jnp.maximum(acc, 0.0))))
        W_mlp_out[layer] = weight("mlp_out" + tag, act).reshape(1, 4 * D, D)
        return u, act, mm_nn("mlp_out" + tag, act, W_mlp_out[layer], out_dtypes, epilogue, extra, None, vecs,
                             FUSED_ROWS if epilogue else 2048)

    def res_mlp0(xb, bb, ada0, bias0, ada1, bias1, kva, kvb, gain_mix, gain_kv):
        xn = xb + _ada_slices(ada0, bias0)[5] * bb
        s1 = _ada_slices(ada1, bias1)
        kv_shift, kv_scale = _ada_slices(kva, kvb)
        return xn, _modulate(xn, gain_kv, kv_scale, kv_shift), _modulate(xn, gain_mix[1:2], s1[1], s1[0])

    u0, act0, (mlp0,) = mlp_forward("0", h2_0, 0)
    x2, h_kv, h1_1 = rowwise("res_mlp0", res_mlp0, [x1, mlp0],
                             [ada_raw[0], b_ada[0:1], ada_raw[1], b_ada[1:2], kvada_raw, kv_bias, norm_mix_gain, kv_gain],
                             [(D, F32), (D, BF16), (D, BF16)])

    W_kv = jnp.pad(weight("kv", h_kv).reshape(N_DEV * n_kv, D), ((0, KV_PAD - N_DEV * n_kv), (0, 0)))[None]
    kvf = mm_nt("kv_proj", h_kv, W_kv)[0]

    def kv_post(kb, vb, fblk, kg, bias):
        head = lax.broadcasted_iota(jnp.int32, fblk.shape, 1) < FOX_HEADS
        return _norm_fox_heads(kb, kg), vb, jnp.where(head, _log_sigmoid(fblk + bias), 0.0)

    kn, v_att, log_f = rowwise("kv_post", kv_post, [(kvf, D, 0), (kvf, D, 1), (kvf, LANES, 2 * D // LANES)], [k_gain, fb],
                               [(D, BF16), (D, BF16), (LANES, F32)])
    f_cum = cumsum_rows("f_cumsum", log_f, reverse=False)
    f_cum_t = f_cum.T

    W_fox_in = weight("fox_in", kvf)[None]
    qo = mm_nn("fox_proj", h1_1, W_fox_in)[0]
    qn = rowwise("q_norm", _norm_fox_heads, [(qo, D, 0)], [q_gain], [(D, BF16)])[0]
    y_att, lse = fox_forward("fox_fwd", qn, kn, v_att, f_cum_t)
    mixin1 = rowwise("fox_gate", lambda ob, yb: jax.nn.sigmoid(ob) * yb, [(qo, D, 1), y_att], [], [(D, BF16)])[0]
    W_fox_out = weight("fox_out", mixin1).reshape(1, D, D)
    mix1, x3, h2_1 = mm_nn("fox_out", mixin1, W_fox_out, (F32, F32, BF16), with_residual(residual_mod(1, 0)), (x2,),
                           None, (ada_raw[1], b_ada[1:2], norm_mlp_gain), FUSED_ROWS)
    u1, act1, (mlp1,) = mlp_forward("1", h2_1, 1)

    def scatter_start(tag, gws, after=()):
        lands = [lax.empty((4,) + g.shape[1:], g.dtype) for g in gws]
        return exchange_start("scatter_sibling_start_" + tag, gws, lands, plan_to_sibling, after)

    def scatter_relay(tag, handles, after, start_after=()):
        gws, from_sibling = exchange_wait("scatter_sibling_wait_" + tag, handles, plan_to_sibling, after)
        sums = [chip_sum("chip_sum_%s%d" % (tag, i), g, s) for i, (g, s) in enumerate(zip(gws, from_sibling))]
        lands = [lax.empty((3,) + s.shape[1:], s.dtype) for s in sums]
        return exchange_start("scatter_owner_start_" + tag, sums, lands, plan_to_owners, start_after)

    def scatter_direct_start(tag, gws):
        lands = [lax.empty(g.shape, g.dtype) for g in gws]
        return exchange_start("scatter_direct_start_" + tag, gws, lands, plan_scatter_direct)

    def scatter_direct_finish(tag, handles, after):
        return [[(r, N_DEV)] for r in exchange_wait("scatter_direct_wait_" + tag, handles, plan_scatter_direct, after)[1]]

    def scatter_finish(tag, handles, after):
        sums, received = exchange_wait("scatter_owner_wait_" + tag, handles, plan_to_owners, after)
        return [[(s, 1), (r, 3)] for s, r in zip(sums, received)]

    def loss_head(xb, bb, tb, ada, bias):
        g2 = _ada_slices(ada, bias)[5]
        err = xb + g2 * bb - tb
        dx = err * (1.0 / D)
        loss = 0.5 * jnp.sum(jnp.sum(err * err, axis=1, keepdims=True) * (1.0 / D), axis=0, keepdims=True)
        return dx, (dx * g2), jnp.broadcast_to(loss, (1, LANES)), jnp.sum(dx * bb, axis=0, keepdims=True)

    dx4, dmlp1, loss_acc, dg2_1 = rowwise("loss_head", loss_head, [x3, mlp1, target], [ada_raw[1], b_ada[1:2]],
                                          [(D, F32), (D, BF16)], [(1, LANES), (1, D)])

    def mlp_backward(tag, dmlp, act, u, h2, layer, after=None):
        du = mm_nt("mlp_out_dx" + tag, dmlp, W_mlp_out[layer], (BF16,), extra=(u,), after=after,
                   epilogue=lambda acc, ub: (acc * (2.0 * jnp.maximum(ub.astype(F32), 0.0)),))[0]
        gw_out = mm_tn("mlp_out_dw" + tag, act, dmlp, 1, BF16).reshape(N_DEV, -1, D)
        dh2 = mm_nt("mlp_in_dx" + tag, du, W_mlp_in[layer], (BF16,))[0]
        gw_in = mm_tn("mlp_in_dw" + tag, h2, du, N_DEV, BF16)
        return dh2, gw_in, gw_out

    def mod_backward(layer, slots, gate_slot):
        def fn(xb, dhb, dresb, branchb, ada, bias, gain):
            s = _ada_slices(ada, bias)
            g = gain[layer:layer + 1]
            dx, dgain, dsc, dsh = _vjp(_modulate, (xb, g, s[slots[1]], s[slots[0]]), dhb.astype(F32))
            dx = dx + dresb
            d_branch = dx * s[gate_slot]
            return dx, d_branch, dgain, dsc, dsh, jnp.sum(dx * branchb, axis=0, keepdims=True)
        return fn

    vec = (1, D)
    dh2_1, gw_mlp_in1, gw_mlp_out1 = mlp_backward("1", dmlp1, act1, u1, h2_1, 1)
    scat_a1, token_a1 = scatter_direct_start("a1", [gw_mlp_in1, gw_mlp_out1])
    dx3, dmix1, dgain_mlp1, dsc2_1, dsh2_1, dg1_1 = rowwise(
        "mod_mlp1_bwd", mod_backward(1, (3, 4), 2), [x3, dh2_1, dx4, mix1], [ada_raw[1], b_ada[1:2], norm_mlp_gain],
        [(D, F32), (D, BF16)], [vec] * 4, after=token_a1)
    dmixin1 = mm_nt("fox_out_dx", dmix1, W_fox_out)[0]
    gw_fox_out = mm_tn("fox_out_dw", mixin1, dmix1, 1, BF16).reshape(N_DEV, -1, D)

    def fox_gate_bwd(db, ob, yb):
        sg = jax.nn.sigmoid(ob)
        return db * sg, db * yb * sg * (1.0 - sg)

    dy_att, d_og = rowwise("fox_gate_bwd", fox_gate_bwd, [dmixin1, (qo, D, 1), y_att], [], [(D, F32), (D, BF16)])
    dqn, dkn, dv_att, dfq, dfk = fox_backward("fox_bwd", qn, kn, v_att, f_cum_t, y_att, dy_att, lse)

    def q_norm_bwd(qb, db, ogb, gain):
        dq, dgain = _vjp(_norm_fox_heads, (qb, gain), db)
        return jnp.concatenate([dq, ogb.astype(F32)], axis=1), dgain

    dqo, dq_gain = rowwise("q_norm_bwd", q_norm_bwd, [(qo, D, 0), dqn, d_og], [q_gain], [(2 * D, BF16)], [vec])
    dh1_1 = mm_nt("fox_proj_dx", dqo, W_fox_in, (BF16,))[0]
    gw_fox_in = mm_tn("fox_proj_dw", h1_1, dqo, N_DEV, BF16)

    dfk_rows = jnp.pad(dfk[:, :2, :].reshape(FOX_HEADS, S).T, ((0, 0), (0, LANES - FOX_HEADS)))

    def df_total(*blks):
        tot = blks[0]
        for b in blks[1:]:
            tot = tot + b
        return tot

    d_fcum = rowwise("df_sum", df_total, [dfk_rows] + [(dfq, LANES, 0, p) for p in range(N_PAIR)], [], [(LANES, F32)])[0]
    d_logf = cumsum_rows("df_cumsum", d_fcum, reverse=True)

    def kv_post_bwd(kb, fblk, dkb, dvb, dlf, kg, bias):
        dk, dgain = _vjp(_norm_fox_heads, (kb, kg), dkb)
        df = dlf * (1.0 / (1.0 + jnp.exp(fblk + bias)))
        return jnp.concatenate([dk, dvb.astype(F32), df], axis=1), dgain, jnp.sum(df, axis=0, keepdims=True)

    dkvf, dk_gain, dfb = rowwise("kv_post_bwd", kv_post_bwd,
                                 [(kvf, D, 0), (kvf, LANES, 2 * D // LANES), dkn, dv_att, d_logf], [k_gain, fb],
                                 [(KV_PAD, BF16)], [vec, (1, LANES)])
    dh_kv = mm_nn("kv_proj_dx", dkvf, W_kv, (BF16,))[0]
    gw_kv = mm_tn("kv_proj_dw", dkvf, h_kv, 1, BF16)[0, :N_DEV * n_kv].reshape(N_DEV, n_kv, D)
    scat_a2, token_a = scatter_direct_start("a2", [gw_fox_out, gw_fox_in, gw_kv])

    def x2_bwd(xb, dh1b, dhkb, dresb, branchb, ada0, bias0, ada1, bias1, kva, kvb, gain_mix, gain_kv):
        s1 = _ada_slices(ada1, bias1)
        kv_shift, kv_scale = _ada_slices(kva, kvb)
        dxa, dgain_mix, dsc1, dsh1 = _vjp(_modulate, (xb, gain_mix[1:2], s1[1], s1[0]), dh1b.astype(F32))
        dxb, dgain_kv, dkv_scale, dkv_shift = _vjp(_modulate, (xb, gain_kv, kv_scale, kv_shift), dhkb.astype(F32))
        dx = dresb + dxa + dxb
        g2 = _ada_slices(ada0, bias0)[5]
        return (dx, dx * g2, dgain_mix, dsc1, dsh1, dgain_kv, dkv_scale, dkv_shift,
                jnp.sum(dx * branchb, axis=0, keepdims=True))

    (dx2, dmlp0, dgain_mix1, dsc1_1, dsh1_1, dgain_kv, dkv_scale, dkv_shift, dg2_0) = rowwise(
        "x2_bwd", x2_bwd, [x2, dh1_1, dh_kv, dx3, mlp0],
        [ada_raw[0], b_ada[0:1], ada_raw[1], b_ada[1:2], kvada_raw, kv_bias, norm_mix_gain, kv_gain],
        [(D, F32), (D, BF16)], [vec] * 7, after=token_a)

    dh2_0, gw_mlp_in0, gw_mlp_out0 = mlp_backward("0", dmlp0, act0, u0, h2_0, 0)
    scat_b1, token_b1 = scatter_direct_start("b1", [gw_mlp_in0, gw_mlp_out0])
    dx1, dmix0, dgain_mlp0, dsc2_0, dsh2_0, dg1_0 = rowwise(
        "mod_mlp0_bwd", mod_backward(0, (3, 4), 2), [x1, dh2_0, dx2, mix0], [ada_raw[0], b_ada[0:1], norm_mlp_gain],
        [(D, F32), (D, BF16)], [vec] * 4, after=token_b1)
    dmixin0 = mm_nt("ret_out_dx", dmix0, W_ret_out, (BF16,))[0]
    gw_ret_out = mm_tn("ret_out_dw", mixin0, dmix0, 1, BF16).reshape(N_DEV, -1, D)
    scat_b2, token_b = scatter_direct_start("b2", [gw_ret_out])

    def ret_gate_bwd(db, yb, gb, gain):
        return _vjp(ret_gate, (yb, gb.astype(F32), gain), db.astype(F32))

    dy_ret, dgate, dret_gain = rowwise("ret_gate_bwd", ret_gate_bwd, [dmixin0, y_ret, (proj, 2 * D, 2)], [ret_gain],
                                       [(2 * D, BF16), (2 * D, BF16)], [(1, 2 * D)], after=token_b)
    dy_h = (dy_ret, RET_V, 0)
    q_h, k_h = (q_rot, RET_QK, 0), (k_rot, RET_QK, 0)
    dq_rot = retention("ret_dq", [(dy_h, v_ret, k_h)], False, [F32])[0]
    dk_rot, dv_ret = retention("ret_dk_dv", [(v_ret, dy_h, q_h), (k_h, q_h, dy_h)], True, [F32, BF16])

    def rope_bwd(dqb, dkb, dvb, dgb, cs, sn):
        dq = _rotate(dqb, cs, sn, RET_HEADS, -1.0)
        dk = _rotate(dkb, cs, sn, RET_HEADS, -1.0) * (RET_QK ** -0.5)
        return jnp.concatenate([dq, dk, dvb.astype(F32), dgb.astype(F32)], axis=1)

    dproj = rowwise("rope_bwd", rope_bwd, [dq_rot, dk_rot, dv_ret, dgate, cos, sin], [], [(6 * D, BF16)])[0]
    gw_ret_in = mm_tn("ret_proj_dw", h1_0, dproj, N_DEV, BF16)
    scat_c, token_c = scatter_start("c", [gw_ret_in])
    dh1_0 = mm_nt("ret_proj_dx", dproj, W_ret_in, (BF16,), after=token_c)[0]

    def x0_bwd(xb, dhb, dresb, ada, bias, gain):
        s = _ada_slices(ada, bias)
        dx, dgain, dsc, dsh = _vjp(_modulate, (xb, gain[0:1], s[1], s[0]), dhb.astype(F32))
        return dx + dresb, dgain, dsc, dsh

    grad_x, dgain_mix0, dsc1_0, dsh1_0 = rowwise("x0_bwd", x0_bwd, [x0, dh1_0, dx1],
                                                 [ada_raw[0], b_ada[0:1], norm_mix_gain], [(D, F32)], [vec] * 3)

    small = jnp.concatenate([
        dsh1_0, dsc1_0, dg1_0, dsh2_0, dsc2_0, dg2_0,
        dsh1_1, dsc1_1, dg1_1, dsh2_1, dsc2_1, dg2_1,
        dkv_shift, dkv_scale,
        dgain_mix0, dgain_mix1, dgain_mlp0, dgain_mlp1, dgain_kv,
        dret_gain,
        dq_gain.reshape(FOX_HEADS, FOX_DH).sum(axis=0).reshape(1, FOX_DH),
        dk_gain.reshape(FOX_HEADS, FOX_DH).sum(axis=0).reshape(1, FOX_DH),
        dfb,
        loss_acc,
    ], axis=1)
    small_all = all_gather("gather_small", [small.reshape(-1, LANES)])[0].reshape(N_DEV, 1, -1)
    loss = jnp.sum(small_all[:, 0, -1])
    scat_c, token_c = scatter_relay("c", scat_c, grad_x, start_after=(small_all,))
    o_ada = 14 * D
    d_ada = small_all[:, 0, :o_ada]
    d_cat = jnp.concatenate([
        lax.dynamic_slice_in_dim(d_ada[:, 0:6 * D], me * n_ada, n_ada, axis=1),
        lax.dynamic_slice_in_dim(d_ada[:, 6 * D:12 * D], me * n_ada, n_ada, axis=1),
        lax.dynamic_slice_in_dim(d_ada[:, 12 * D:14 * D], me * n_kvada, n_kvada, axis=1)], axis=1)
    gw_ada_cat = mm_tn("ada_dw", c_act, d_cat, 1, F32, after=token_c)[0]

    results = {}

    def update(name, parts, w, m, v, layers=1):
        per_layer = parts if layers > 1 else [parts]
        shape = w.shape
        C = shape[-1]
        R = int(np.prod(shape)) // (layers * C)
        per_layer = [p if isinstance(p, list) else [(p, p.shape[0])] for p in per_layer]
        per_layer = [[(a.reshape(a.shape[0], R, C), n) for a, n in p] for p in per_layer]
        outs = adamw("adamw_" + name, per_layer, w.reshape(layers, R, C), m.reshape(layers, R, C), v.reshape(layers, R, C))
        results[name] = tuple(t.reshape(shape) for t in outs)

    def small_parts(lo, width):
        return small_all[:, :, lo:lo + width]

    update("norm_mix_gain", jnp.concatenate([small_parts(o_ada, D), small_parts(o_ada + D, D)], axis=1),
           norm_mix_gain, m_norm_mix_gain, v_norm_mix_gain)
    update("norm_mlp_gain", jnp.concatenate([small_parts(o_ada + 2 * D, D), small_parts(o_ada + 3 * D, D)], axis=1),
           norm_mlp_gain, m_norm_mlp_gain, v_norm_mlp_gain)
    update("w_ada", [gw_ada_cat[None, :, :n_ada], gw_ada_cat[None, :, n_ada:2 * n_ada]], w_ada, m_w_ada, v_w_ada, layers=2)
    update("b_ada", jnp.concatenate([small_parts(0, 6 * D), small_parts(6 * D, 6 * D)], axis=1), b_ada, m_b_ada, v_b_ada)
    o_ret = o_ada + 5 * D
    n_rg = ret_norm_gain.shape[2]
    ret_gain_parts = small_parts(o_ret, 2 * D).reshape(N_DEV, RET_HEADS, RET_V)
    ret_gain_parts = lax.dynamic_slice_in_dim(ret_gain_parts, me * n_rg, n_rg, axis=2)
    update("ret_norm_gain", ret_gain_parts, ret_norm_gain, m_ret_norm_gain, v_ret_norm_gain)
    update("kv_norm_gain", small_parts(o_ada + 4 * D, D), kv_norm_gain, m_kv_norm_gain, v_kv_norm_gain)
    update("kv_w_ada", gw_ada_cat[None, :, 2 * n_ada:], kv_w_ada, m_kv_w_ada, v_kv_w_ada)
    update("kv_b_ada", small_parts(12 * D, 2 * D), kv_b_ada, m_kv_b_ada, v_kv_b_ada)
    o_q = o_ret + 2 * D
    update("forget_bias", small_parts(o_q + 2 * FOX_DH, FOX_HEADS), forget_bias, m_forget_bias, v_forget_bias)
    update("k_norm_gain", small_parts(o_q + FOX_DH, FOX_DH), k_norm_gain, m_k_norm_gain, v_k_norm_gain)
    update("q_norm_gain", small_parts(o_q, FOX_DH), q_norm_gain, m_q_norm_gain, v_q_norm_gain)

    r_mlp_in1, r_mlp_out1 = scatter_direct_finish("a1", scat_a1, results["q_norm_gain"][1])
    r_fox_out, r_fox_in, r_kv = scatter_direct_finish("a2", scat_a2, r_mlp_in1[0][0])
    r_mlp_in0, r_mlp_out0 = scatter_direct_finish("b1", scat_b1, r_kv[0][0])
    r_ret_out = scatter_direct_finish("b2", scat_b2, r_mlp_in0[0][0])[0]
    update("kv_w", r_kv, kv_w.T, m_kv_w.T, v_kv_w.T)
    results["kv_w"] = tuple(t.T for t in results["kv_w"])
    update("fox_w_in", r_fox_in, fox_w_in, m_fox_w_in, v_fox_w_in)
    update("fox_w_out", r_fox_out, fox_w_out, m_fox_w_out, v_fox_w_out)
    update("ret_w_out", r_ret_out, ret_w_out, m_ret_w_out, v_ret_w_out)
    update("w_mlp_in", [r_mlp_in0, r_mlp_in1], w_mlp_in, m_w_mlp_in, v_w_mlp_in, layers=2)
    update("w_mlp_out", [r_mlp_out0, r_mlp_out1], w_mlp_out, m_w_mlp_out, v_w_mlp_out, layers=2)
    r_ret_in = scatter_finish("c", scat_c, results["w_mlp_out"][1])[0]
    update("ret_w_in", r_ret_in, ret_w_in, m_ret_w_in, v_ret_w_in)

    order = ["norm_mix_gain", "norm_mlp_gain", "w_ada", "b_ada", "w_mlp_in", "w_mlp_out", "ret_w_in", "ret_norm_gain",
             "ret_w_out", "kv_norm_gain", "kv_w_ada", "kv_b_ada", "kv_w", "forget_bias", "k_norm_gain", "fox_w_in",
             "q_norm_gain", "fox_w_out"]
    out = [loss, grad_x.reshape(x.shape)]
    for slot in range(4):
        out += [results[n][slot] for n in order]
    return tuple(out)
```
